```python
import math
import jax
import jax.numpy as jnp
from jax import lax
import numpy as np

D_MODEL = 1024
BATCH = 1
SEQ = 16384
DEPTH = 2

GRID_W = 64
CTX_LEN = 256
N_EVEN = (DEPTH + 1) // 2
N_ODD = DEPTH // 2
MIX_W = D_MODEL
EPS = 1e-6
CONV_W = 3

HY_W = MIX_W // 2
HY_ORDER = 2
HY_EMB = 33
HY_FFN = 64
HY_TARGET = 1e-2
HY_SHORT_PCT = 0.3
HY_LONG_PCT = 1.5

HEAD_DIM = 64
ATT_HEADS = (MIX_W // 2) // HEAD_DIM
ATT_KV_HEADS = 2
ATT_GROUP = ATT_HEADS // ATT_KV_HEADS
ATT_WINDOW = 128
ATT_BLOCK = 128
ROPE_BASE = 10000.0
ATT_Q_W = ATT_HEADS * HEAD_DIM
ATT_KV_W = ATT_KV_HEADS * HEAD_DIM

SSD_W = MIX_W // 2
SSD_HEAD_DIM = 64
SSD_HEADS = SSD_W // SSD_HEAD_DIM
SSD_GROUPS = 2
SSD_STATE = 128
SSD_CHUNK = 128
SSD_CONV_CH = SSD_W + 2 * SSD_GROUPS * SSD_STATE

HG_W = MIX_W // 2
HG_EXPAND = 128
HG_HEADS = HG_W // HG_EXPAND
HG_VDIM = HG_W // HG_HEADS
HG_CHUNK = 64

N_EXPERTS = 32
TOP_K = 4
D_EXPERT = D_MODEL
SWIGLU_ALPHA = 1.702
SWIGLU_LIMIT = 7.0
MOE_BLOCK = 128

EVEN_IN = 2 * ATT_KV_W + ATT_Q_W + 3 * HY_W
ODD_STATE_COLS = SSD_CONV_CH + 2 * SSD_HEADS + 3 * HG_W
ODD_IN = ODD_STATE_COLS + SSD_W + 2 * HG_W

kernel_name = 'hybrid_hyena_swa_ssd_hgrn2_moe_dit'


def rms_norm(x, g):
    xf = x.astype(jnp.float32)
    y = xf * lax.rsqrt(jnp.mean(xf * xf, axis=-1, keepdims=True) + EPS)
    return (y * g.astype(jnp.float32)).astype(x.dtype)


def modulate(h, shift, scale):
    return h * (1.0 + scale) + shift


def adaln(cond, w, b, j):
    lo, hi = 3 * j * D_MODEL, 3 * (j + 1) * D_MODEL
    m = jax.nn.silu(cond) @ w[:, lo:hi] + b[lo:hi]
    return jnp.split(m, 3, axis=-1)


def dwconv_centred(u, w, b):
    ch = u.shape[-1]
    y = lax.conv_general_dilated(u, w[:, None, :].astype(u.dtype), window_strides=(1,),
                                 padding=[(CONV_W // 2, CONV_W // 2)],
                                 dimension_numbers=('NWC', 'WIO', 'NWC'), feature_group_count=ch)
    return y + b.astype(u.dtype)


def axial_rope_tables(length):
    rows = length // GRID_W
    n_pairs = HEAD_DIM // 4
    inv = ROPE_BASE ** (-jnp.arange(n_pairs, dtype=jnp.float32) / n_pairs)
    row_ang = jnp.arange(rows, dtype=jnp.float32)[:, None] * inv
    col_ang = jnp.arange(GRID_W, dtype=jnp.float32)[:, None] * inv
    ang_r = jnp.broadcast_to(row_ang[:, None], (rows, GRID_W, n_pairs)).reshape(length, n_pairs)
    ang_c = jnp.broadcast_to(col_ang[None], (rows, GRID_W, n_pairs)).reshape(length, n_pairs)
    return jnp.cos(ang_r), jnp.sin(ang_r), jnp.cos(ang_c), jnp.sin(ang_c)


def _rotate(u, cos, sin):
    n = u.shape[-1] // 2
    u1, u2 = u[..., :n], u[..., n:]
    cos = cos[None, :, None, :]
    sin = sin[None, :, None, :]
    return jnp.concatenate([u1 * cos - u2 * sin, u1 * sin + u2 * cos], axis=-1)


def apply_axial_rope(u, tables):
    cr, sr, cc, sc = tables
    half = HEAD_DIM // 2
    return jnp.concatenate([_rotate(u[..., :half], cr, sr), _rotate(u[..., half:], cc, sc)], axis=-1)


def hyena_filters(length, w1, b1, w2, b2, w3, b3, w4, freq):
    f32 = jnp.float32
    t = jnp.linspace(0.0, 1.0, length, dtype=f32)[:, None]
    bands = (HY_EMB - 1) // 2
    w_ang = 2.0 * math.pi * jnp.arange(length, dtype=f32)[:, None] / length
    fr = jnp.linspace(1e-4, bands - 1, bands, dtype=f32)[None]
    z = jnp.concatenate([t, jnp.cos(fr * w_ang), -jnp.sin(fr * w_ang)], axis=-1)
    fq = freq.astype(f32)
    hdn = jnp.sin(fq * (z @ w1.astype(f32) + b1.astype(f32)))
    hdn = jnp.sin(fq * (hdn @ w2.astype(f32) + b2.astype(f32)))
    hdn = jnp.sin(fq * (hdn @ w3.astype(f32) + b3.astype(f32)))
    h = (hdn @ w4.astype(f32)).reshape(length, HY_ORDER, 2, HY_W)
    max_decay = math.log(HY_TARGET) / HY_SHORT_PCT
    min_decay = math.log(HY_TARGET) / HY_LONG_PCT
    deltas = jnp.abs(jnp.linspace(min_decay, max_decay, HY_W, dtype=f32))
    h = h * jnp.exp(-t * deltas)[:, None, None, :]
    h2 = jnp.concatenate([h[:, :, 0], jnp.zeros((1, HY_ORDER, HY_W), f32), h[:0:-1, :, 1]], axis=0)
    h2 = h2 / jnp.sum(jnp.abs(h2), axis=0, keepdims=True)
    return jnp.fft.rfft(h2, axis=0)


def hyena_mix(u, hf, filter_bias, conv_w, conv_b):
    length = u.shape[1]
    u = dwconv_centred(u.astype(jnp.float32), conv_w, conv_b)
    v, x1, x2 = jnp.split(u, 3, axis=-1)
    z = v
    for o, gate in enumerate((x1, x2)):
        zf = jnp.fft.rfft(z, n=2 * length, axis=1)
        zc = jnp.fft.irfft(zf * hf[None, :, o], n=2 * length, axis=1)[:, :length]
        z = gate * (zc + z * filter_bias[o].astype(jnp.float32))
    return z


def window_attention(q, k, v, k_c, v_c, sink):
    bsz, length = q.shape[:2]
    nb = length // ATT_BLOCK
    scale = HEAD_DIM ** -0.5
    qb = q.reshape(bsz, nb, ATT_BLOCK, ATT_KV_HEADS, ATT_GROUP, HEAD_DIM)
    pad = ((0, 0), (ATT_BLOCK, ATT_BLOCK), (0, 0), (0, 0))

    def band(a):
        ap = jnp.pad(a, pad).reshape(bsz, nb + 2, ATT_BLOCK, ATT_KV_HEADS, HEAD_DIM)
        return jnp.concatenate([ap[:, :-2], ap[:, 1:-1], ap[:, 2:]], axis=2)

    kw, vw = band(k), band(v)
    s_loc = jnp.einsum('bnqhgd,bnkhd->bnhgqk', qb, kw) * scale
    s_ctx = jnp.einsum('bnqhgd,bchd->bnhgqc', qb, k_c) * scale
    qpos = jnp.arange(nb)[:, None] * ATT_BLOCK + jnp.arange(ATT_BLOCK)[None]
    kpos = (jnp.arange(nb)[:, None] - 1) * ATT_BLOCK + jnp.arange(3 * ATT_BLOCK)[None]
    rel = kpos[:, None, :] - qpos[:, :, None]
    valid = (jnp.abs(rel) <= ATT_WINDOW) & (kpos[:, None, :] >= 0) & (kpos[:, None, :] < length)
    s_loc = jnp.where(valid[None, :, None, None], s_loc, -jnp.inf)
    sink_l = jnp.broadcast_to(sink.astype(jnp.float32).reshape(1, 1, ATT_KV_HEADS, ATT_GROUP, 1, 1),
                              s_loc.shape[:-1] + (1,))
    p = jax.nn.softmax(jnp.concatenate([s_loc, s_ctx, sink_l], axis=-1), axis=-1)
    n_loc = 3 * ATT_BLOCK
    n_ctx = k_c.shape[1]
    o = (jnp.einsum('bnhgqk,bnkhd->bnqhgd', p[..., :n_loc], vw)
         + jnp.einsum('bnhgqc,bchd->bnqhgd', p[..., n_loc:n_loc + n_ctx], v_c))
    return o.reshape(bsz, length, ATT_Q_W)


def context_attention(q_c, k_c, v_c, sink):
    bsz, n_ctx = q_c.shape[:2]
    s = jnp.einsum('bqhgd,bkhd->bhgqk', q_c, k_c) * HEAD_DIM ** -0.5
    sink_l = jnp.broadcast_to(sink.astype(jnp.float32).reshape(1, ATT_KV_HEADS, ATT_GROUP, 1, 1),
                              s.shape[:-1] + (1,))
    p = jax.nn.softmax(jnp.concatenate([s, sink_l], axis=-1), axis=-1)[..., :-1]
    return jnp.einsum('bhgqk,bkhd->bqhgd', p, v_c).reshape(bsz, n_ctx, ATT_Q_W)


def ssd_scan(x, dt, a, bm, cm, d_skip, init, need_y):
    bsz, length, n_heads, hd = x.shape
    nc = length // SSD_CHUNK
    hpg = n_heads // SSD_GROUPS
    da = (dt * a).reshape(bsz, nc, SSD_CHUNK, SSD_GROUPS, hpg)
    cs = jnp.cumsum(da, axis=2)
    xdt = (x * dt[..., None]).reshape(bsz, nc, SSD_CHUNK, SSD_GROUPS, hpg, hd)
    bc = bm.reshape(bsz, nc, SSD_CHUNK, SSD_GROUPS, SSD_STATE)
    cc = cm.reshape(bsz, nc, SSD_CHUNK, SSD_GROUPS, SSD_STATE)
    to_end = jnp.exp(cs[:, :, -1:] - cs)
    states = jnp.einsum('bcsgn,bcsgh,bcsghp->bcghpn', bc, to_end, xdt)
    chunk_decay = jnp.exp(cs[:, :, -1])

    def step(s, inp):
        st, dec = inp
        return s * dec[..., None, None] + st, s

    s_final, s_in = lax.scan(step, init, (jnp.moveaxis(states, 1, 0), jnp.moveaxis(chunk_decay, 1, 0)))
    if not need_y:
        return None, s_final
    s_in = jnp.moveaxis(s_in, 0, 1)
    cs_t = jnp.moveaxis(cs, 2, -1)
    diff = cs_t[..., :, None] - cs_t[..., None, :]
    lower = jnp.tril(jnp.ones((SSD_CHUNK, SSD_CHUNK), bool))
    decay = jnp.where(lower, jnp.exp(jnp.where(lower, diff, 0.0)), 0.0)
    scores = jnp.einsum('bclgn,bcsgn->bcgls', cc, bc)
    y_diag = jnp.einsum('bcgls,bcghls,bcsghp->bclghp', scores, decay, xdt)
    y_off = jnp.einsum('bclgn,bcghpn,bclgh->bclghp', cc, s_in, jnp.exp(cs))
    y = (y_diag + y_off).reshape(bsz, length, n_heads, hd) + d_skip[:, None] * x
    return y, s_final


def hgrn2_scan(q, k, v, g, init, need_o):
    bsz, length, n_heads, _ = k.shape
    nc = length // HG_CHUNK

    def chunks(a):
        return a.reshape(bsz, nc, HG_CHUNK, n_heads, a.shape[-1]).transpose(1, 0, 3, 2, 4)

    lower = jnp.tril(jnp.ones((HG_CHUNK, HG_CHUNK), bool))[:, :, None]

    def update(s, kc, vc, cum):
        last = cum[:, :, -1]
        return (s * jnp.exp(last)[..., None]
                + jnp.einsum('bhsk,bhsv->bhkv', kc * jnp.exp(last[:, :, None] - cum), vc))

    if not need_o:
        def step_state(s, inp):
            kc, vc, gc = inp
            return update(s, kc, vc, jnp.cumsum(gc, axis=2)), None
        s_final, _ = lax.scan(step_state, init, (chunks(k), chunks(v), chunks(g)))
        return None, s_final

    def step(s, inp):
        qc, kc, vc, gc = inp
        cum = jnp.cumsum(gc, axis=2)
        diff = cum[:, :, :, None, :] - cum[:, :, None, :, :]
        decay = jnp.where(lower, jnp.exp(jnp.where(lower, diff, 0.0)), 0.0)
        att = jnp.einsum('bhtk,bhsk,bhtsk->bhts', qc, kc, decay)
        o = (jnp.einsum('bhtk,bhkv->bhtv', qc * jnp.exp(cum), s)
             + jnp.einsum('bhts,bhsv->bhtv', att, vc))
        return update(s, kc, vc, cum), o

    s_final, o = lax.scan(step, init, (chunks(q), chunks(k), chunks(v), chunks(g)))
    return o.transpose(1, 0, 3, 2, 4).reshape(bsz, length, n_heads, v.shape[-1]), s_final


def even_mixer(h, hc, w_in, conv_w, conv_b, f_w1, f_b1, f_w2, f_b2, f_w3, f_b3, f_w4, f_freq, f_bias,
               q_norm, k_norm, sink, ctx_out):
    f32 = jnp.float32
    bsz, length, _ = h.shape
    n_ctx = hc.shape[1]
    filt = (f_w1, f_b1, f_w2, f_b2, f_w3, f_b3, f_w4, f_freq)
    o_v = ATT_KV_W
    o_q = 2 * ATT_KV_W
    o_hy = o_q + ATT_Q_W
    p = (h @ w_in).astype(f32)
    pc = (hc @ (w_in if ctx_out else w_in[:, :o_q])).astype(f32)

    def heads(a, n_heads):
        return a.reshape(a.shape[0], a.shape[1], n_heads, HEAD_DIM)

    k_c = rms_norm(heads(pc[..., :o_v], ATT_KV_HEADS), k_norm)
    v_c = heads(pc[..., o_v:o_q], ATT_KV_HEADS)
    rope = axial_rope_tables(length)
    k = apply_axial_rope(rms_norm(heads(p[..., :o_v], ATT_KV_HEADS), k_norm), rope)
    v = heads(p[..., o_v:o_q], ATT_KV_HEADS)
    q = apply_axial_rope(rms_norm(heads(p[..., o_q:o_hy], ATT_HEADS), q_norm), rope)
    att = window_attention(q.reshape(bsz, length, ATT_KV_HEADS, ATT_GROUP, HEAD_DIM), k, v, k_c, v_c, sink)
    hy = hyena_mix(p[..., o_hy:], hyena_filters(length, *filt), f_bias, conv_w, conv_b)
    out = jnp.concatenate([hy, att], axis=-1).astype(h.dtype)
    if not ctx_out:
        return out, None
    q_c = rms_norm(heads(pc[..., o_q:o_hy], ATT_HEADS), q_norm).reshape(bsz, n_ctx, ATT_KV_HEADS, ATT_GROUP, HEAD_DIM)
    att_c = context_attention(q_c, k_c, v_c, sink)
    hy_c = hyena_mix(pc[..., o_hy:], hyena_filters(n_ctx, *filt), f_bias, conv_w, conv_b)
    return out, jnp.concatenate([hy_c, att_c], axis=-1).astype(hc.dtype)


def odd_mixer(h, hc, lb, w_in, conv_w, conv_b, dt_bias, a_log, d_skip, ssd_norm, hg_norm, ctx_out):
    f32 = jnp.float32
    bsz, length, _ = h.shape
    n_ctx = hc.shape[1]
    p = (h @ w_in).astype(f32)
    pc = (hc @ (w_in if ctx_out else w_in[:, :ODD_STATE_COLS])).astype(f32)
    o_dt = SSD_CONV_CH
    o_f = SSD_CONV_CH + 2 * SSD_HEADS
    o_i = o_f + 2 * HG_W
    o_z = ODD_STATE_COLS
    o_q = o_z + SSD_W
    o_g = o_q + HG_W
    gn = SSD_GROUPS * SSD_STATE

    def streams(pp):
        n = pp.shape[1]
        xbc = jax.nn.silu(dwconv_centred(pp[..., :SSD_CONV_CH], conv_w, conv_b))
        xs = xbc[..., :SSD_W].reshape(bsz, n, SSD_HEADS, SSD_HEAD_DIM)
        bm = xbc[..., SSD_W:SSD_W + gn].reshape(bsz, n, SSD_GROUPS, SSD_STATE)
        cm = xbc[..., SSD_W + gn:].reshape(bsz, n, SSD_GROUPS, SSD_STATE)
        dt_raw = pp[..., o_dt:o_f].reshape(bsz, n, 2, SSD_HEADS)
        f_raw = pp[..., o_f:o_i].reshape(bsz, n, 2, HG_HEADS, HG_EXPAND)
        iv = pp[..., o_i:o_i + HG_W].reshape(bsz, n, HG_HEADS, HG_VDIM)
        return xs, bm, cm, dt_raw, f_raw, iv

    xs, bm, cm, dt_raw, f_raw, iv = streams(p)
    xs_c, bm_c, cm_c, dt_raw_c, f_raw_c, iv_c = streams(pc)
    q = jax.nn.silu(p[..., o_q:o_g]).reshape(bsz, length, HG_HEADS, HG_EXPAND)
    q_c = jax.nn.silu(pc[..., o_q:o_g]).reshape(bsz, n_ctx, HG_HEADS, HG_EXPAND) if ctx_out else None
    lb = lb.astype(f32).reshape(HG_HEADS, HG_EXPAND)
    ssd0 = jnp.zeros((bsz, SSD_GROUPS, SSD_HEADS // SSD_GROUPS, SSD_HEAD_DIM, SSD_STATE), f32)
    hg0 = jnp.zeros((bsz, HG_HEADS, HG_EXPAND, HG_VDIM), f32)
    y_dirs, o_dirs, yc_dirs, oc_dirs = [], [], [], []
    for d in range(2):
        fl = (lambda a: jnp.flip(a, axis=1)) if d == 1 else (lambda a: a)
        a = -jnp.exp(a_log[d].astype(f32))
        dsk = d_skip[d].astype(f32)
        dtb = dt_bias[d].astype(f32)
        dt_l = jax.nn.softplus(dt_raw[:, :, d] + dtb)
        dt_c = jax.nn.softplus(dt_raw_c[:, :, d] + dtb)
        yc, s_ctx = ssd_scan(fl(xs_c), fl(dt_c), a, fl(bm_c), fl(cm_c), dsk, ssd0, ctx_out)
        yl, _ = ssd_scan(fl(xs), fl(dt_l), a, fl(bm), fl(cm), dsk, s_ctx, True)
        y_dirs.append(fl(yl))
        f_l = lb + (1.0 - lb) * jax.nn.sigmoid(f_raw[:, :, d])
        f_c = lb + (1.0 - lb) * jax.nn.sigmoid(f_raw_c[:, :, d])
        oc, s_hg = hgrn2_scan(fl(q_c) if ctx_out else None, fl(1.0 - f_c), fl(iv_c), fl(jnp.log(f_c)), hg0, ctx_out)
        ol, _ = hgrn2_scan(fl(q), fl(1.0 - f_l), fl(iv), fl(jnp.log(f_l)), s_hg, True)
        o_dirs.append(fl(ol))
        if ctx_out:
            yc_dirs.append(fl(yc))
            oc_dirs.append(fl(oc))

    def merge(yy, oo, pp, n):
        z = pp[..., o_z:o_q]
        g = pp[..., o_g:]
        ys = (yy.reshape(bsz, n, SSD_W) * jax.nn.silu(z)).reshape(bsz, n, SSD_GROUPS, SSD_W // SSD_GROUPS)
        ys = rms_norm(ys, ssd_norm.reshape(SSD_GROUPS, SSD_W // SSD_GROUPS)).reshape(bsz, n, SSD_W)
        hs = rms_norm(oo, hg_norm.reshape(HG_HEADS, HG_VDIM)).reshape(bsz, n, HG_W) * jax.nn.silu(g)
        return jnp.concatenate([ys, hs], axis=-1)

    out = merge(y_dirs[0] + y_dirs[1], o_dirs[0] + o_dirs[1], p, length).astype(h.dtype)
    if not ctx_out:
        return out, None
    out_c = merge(yc_dirs[0] + yc_dirs[1], oc_dirs[0] + oc_dirs[1], pc, n_ctx).astype(hc.dtype)
    return out, out_c


def moe_ffn(t, router_w, router_b, w_gu, b_gu, w_dn, b_dn):
    n_tok, d = t.shape
    logits = (t @ router_w).astype(jnp.float32) + router_b.astype(jnp.float32)
    top_val, top_idx = lax.top_k(logits, TOP_K)
    gates = jax.nn.softmax(top_val, axis=-1)
    flat_e = top_idx.reshape(-1)
    order = jnp.argsort(flat_e)
    e_sorted = flat_e[order]
    counts = jnp.bincount(flat_e, length=N_EXPERTS)
    padded = (counts + MOE_BLOCK - 1) // MOE_BLOCK * MOE_BLOCK
    pad_end = jnp.cumsum(padded)
    first = jnp.cumsum(counts) - counts
    dest = pad_end[e_sorted] - padded[e_sorted] + jnp.arange(n_tok * TOP_K) - first[e_sorted]
    n_blk = -(-(n_tok * TOP_K) // MOE_BLOCK) + N_EXPERTS
    n_rows = n_blk * MOE_BLOCK
    row_tok = jnp.full((n_rows,), n_tok, jnp.int32).at[dest].set((order // TOP_K).astype(jnp.int32))
    row_gate = jnp.zeros((n_rows,), jnp.float32).at[dest].set(gates.reshape(-1)[order])
    blk_e = jnp.minimum(jnp.searchsorted(pad_end, jnp.arange(n_blk) * MOE_BLOCK, side='right'), N_EXPERTS - 1)
    x_rows = jnp.concatenate([t, jnp.zeros((1, d), t.dtype)], axis=0)[row_tok].reshape(n_blk, MOE_BLOCK, d)

    def expert_block(args):
        xb, e = args
        gu = xb @ w_gu[e] + b_gu[e]
        gate = jnp.minimum(gu[:, :D_EXPERT], SWIGLU_LIMIT)
        up = jnp.clip(gu[:, D_EXPERT:], -SWIGLU_LIMIT, SWIGLU_LIMIT)
        act = (up + 1.0) * gate * jax.nn.sigmoid(SWIGLU_ALPHA * gate)
        return act @ w_dn[e] + b_dn[e]

    y_rows = lax.map(expert_block, (x_rows, blk_e)).reshape(n_rows, d)
    y = jax.ops.segment_sum(y_rows * row_gate[:, None].astype(y_rows.dtype), row_tok, num_segments=n_tok + 1)
    return y[:n_tok]


def setup_inputs(seed: int = 0) -> dict:
    key = jax.random.key(seed)
    keys = iter(jax.random.split(key, 48))
    f32 = jnp.float32

    def nrm(shape, scale):
        return jax.random.normal(next(keys), shape, f32) * scale

    def gain(shape):
        return 1.0 + nrm(shape, 0.05)

    dt0 = jnp.exp(jax.random.uniform(next(keys), (N_ODD, 2, SSD_HEADS), f32, math.log(1e-3), math.log(1e-1)))
    a0 = jax.random.uniform(next(keys), (N_ODD, 2, SSD_HEADS), f32, 1.0, 16.0)
    return {
        'x': nrm((BATCH, SEQ, D_MODEL), 1.0),
        'c': nrm((BATCH, D_MODEL), 1.0),
        'ctx': nrm((BATCH, CTX_LEN, D_MODEL), 1.0),
        'c_ctx': nrm((D_MODEL,), 1.0),
        'norm_g': gain((DEPTH, 2, D_MODEL)),
        'ada_w': nrm((DEPTH, D_MODEL, 6 * D_MODEL), 0.5 * D_MODEL ** -0.5),
        'ada_b': nrm((DEPTH, 6 * D_MODEL), 0.02),
        'w_out': nrm((DEPTH, MIX_W, D_MODEL), MIX_W ** -0.5),
        'w_in_even': nrm((N_EVEN, D_MODEL, EVEN_IN), D_MODEL ** -0.5),
        'hy_conv_w': nrm((N_EVEN, CONV_W, 3 * HY_W), CONV_W ** -0.5),
        'hy_conv_b': nrm((N_EVEN, 3 * HY_W), 0.02),
        'hy_w1': nrm((N_EVEN, HY_EMB, HY_FFN), HY_EMB ** -0.5),
        'hy_b1': nrm((N_EVEN, HY_FFN), 0.1),
        'hy_w2': nrm((N_EVEN, HY_FFN, HY_FFN), HY_FFN ** -0.5),
        'hy_b2': nrm((N_EVEN, HY_FFN), 0.1),
        'hy_w3': nrm((N_EVEN, HY_FFN, HY_FFN), HY_FFN ** -0.5),
        'hy_b3': nrm((N_EVEN, HY_FFN), 0.1),
        'hy_w4': nrm((N_EVEN, HY_FFN, HY_ORDER * 2 * HY_W), HY_FFN ** -0.5),
        'hy_freq': gain((N_EVEN, HY_FFN)),
        'hy_filter_bias': nrm((N_EVEN, HY_ORDER, HY_W), 0.5),
        'att_q_norm': gain((N_EVEN, HEAD_DIM)),
        'att_k_norm': gain((N_EVEN, HEAD_DIM)),
        'att_sink': nrm((N_EVEN, ATT_HEADS), 0.5),
        'w_in_odd': nrm((N_ODD, D_MODEL, ODD_IN), D_MODEL ** -0.5),
        'ssd_conv_w': nrm((N_ODD, CONV_W, SSD_CONV_CH), CONV_W ** -0.5),
        'ssd_conv_b': nrm((N_ODD, SSD_CONV_CH), 0.02),
        'ssd_dt_bias': dt0 + jnp.log(-jnp.expm1(-dt0)),
        'ssd_A_log': jnp.log(a0),
        'ssd_D': 1.0 + nrm((N_ODD, 2, SSD_HEADS), 0.1),
        'ssd_norm': gain((N_ODD, SSD_W)),
        'hg_lower_bounds': nrm((DEPTH, HG_W), 0.5),
        'hg_norm': gain((N_ODD, HG_W)),
        'router_w': nrm((DEPTH, D_MODEL, N_EXPERTS), D_MODEL ** -0.5),
        'router_b': nrm((DEPTH, N_EXPERTS), 0.01),
        'moe_w_gu': nrm((DEPTH, N_EXPERTS, D_MODEL, 2 * D_EXPERT), D_MODEL ** -0.5),
        'moe_b_gu': nrm((DEPTH, N_EXPERTS, 2 * D_EXPERT), 0.02),
        'moe_w_dn': nrm((DEPTH, N_EXPERTS, D_EXPERT, D_MODEL), D_EXPERT ** -0.5),
        'moe_b_dn': nrm((DEPTH, N_EXPERTS, D_MODEL), 0.02),
    }


def reference(x, c, ctx, c_ctx, norm_g, ada_w, ada_b, w_out, w_in_even, hy_conv_w, hy_conv_b,
              hy_w1, hy_b1, hy_w2, hy_b2, hy_w3, hy_b3, hy_w4, hy_freq, hy_filter_bias,
              att_q_norm, att_k_norm, att_sink, w_in_odd, ssd_conv_w, ssd_conv_b, ssd_dt_bias,
              ssd_A_log, ssd_D, ssd_norm, hg_lower_bounds, hg_norm, router_w, router_b,
              moe_w_gu, moe_b_gu, moe_w_dn, moe_b_dn):
    lbs = jax.nn.softmax(hg_lower_bounds.astype(jnp.float32), axis=0)
    lbs = jnp.cumsum(lbs, axis=0) - lbs[0]
    xc = ctx
    for layer in range(DEPTH):
        ctx_out = layer < DEPTH - 1
        i = layer // 2
        sh, sc, gt = adaln(c, ada_w[layer], ada_b[layer], 0)
        sh_c, sc_c, gt_c = adaln(c_ctx, ada_w[layer], ada_b[layer], 0)
        h = modulate(rms_norm(x, norm_g[layer, 0]), sh[:, None], sc[:, None])
        hc = modulate(rms_norm(xc, norm_g[layer, 0]), sh_c, sc_c)
        if layer % 2 == 0:
            m, m_c = even_mixer(h, hc, w_in_even[i], hy_conv_w[i], hy_conv_b[i], hy_w1[i], hy_b1[i],
                                hy_w2[i], hy_b2[i], hy_w3[i], hy_b3[i], hy_w4[i], hy_freq[i],
                                hy_filter_bias[i], att_q_norm[i], att_k_norm[i], att_sink[i], ctx_out)
        else:
            m, m_c = odd_mixer(h, hc, lbs[layer], w_in_odd[i], ssd_conv_w[i], ssd_conv_b[i], ssd_dt_bias[i],
                               ssd_A_log[i], ssd_D[i], ssd_norm[i], hg_norm[i], ctx_out)
        x = x + gt[:, None] * (m @ w_out[layer])
        if ctx_out:
            xc = xc + gt_c * (m_c @ w_out[layer])
        sh, sc, gt = adaln(c, ada_w[layer], ada_b[layer], 1)
        h = modulate(rms_norm(x, norm_g[layer, 1]), sh[:, None], sc[:, None])
        moe_params = (router_w[layer], router_b[layer], moe_w_gu[layer], moe_b_gu[layer],
                      moe_w_dn[layer], moe_b_dn[layer])
        n_lat = x.shape[0] * x.shape[1]
        if ctx_out:
            sh_c, sc_c, gt_c = adaln(c_ctx, ada_w[layer], ada_b[layer], 1)
            hc = modulate(rms_norm(xc, norm_g[layer, 1]), sh_c, sc_c)
            y = moe_ffn(jnp.concatenate([h.reshape(n_lat, D_MODEL), hc.reshape(-1, D_MODEL)], axis=0), *moe_params)
            x = x + gt[:, None] * y[:n_lat].reshape(x.shape)
            xc = xc + gt_c * y[n_lat:].reshape(xc.shape)
        else:
            x = x + gt[:, None] * moe_ffn(h.reshape(n_lat, D_MODEL), *moe_params).reshape(x.shape)
    return x
```

```python
import functools
import math

import jax
import jax.numpy as jnp
from jax import lax
from jax.experimental import pallas as pl
from jax.experimental.pallas import tpu as pltpu

D_MODEL = 1024
DEPTH = 2
GRID_W = 64
MIX_W = D_MODEL
EPS = 1e-6
CONV_W = 3

HY_W = MIX_W // 2
HY_ORDER = 2
HY_EMB = 33
HY_FFN = 64
HY_TARGET = 1e-2
HY_SHORT_PCT = 0.3
HY_LONG_PCT = 1.5

HEAD_DIM = 64
ATT_HEADS = (MIX_W // 2) // HEAD_DIM
ATT_KV_HEADS = 2
ATT_GROUP = ATT_HEADS // ATT_KV_HEADS
ATT_WINDOW = 128
ATT_BLOCK = 128
ROPE_BASE = 10000.0
ATT_Q_W = ATT_HEADS * HEAD_DIM
ATT_KV_W = ATT_KV_HEADS * HEAD_DIM

SSD_W = MIX_W // 2
SSD_HEAD_DIM = 64
SSD_HEADS = SSD_W // SSD_HEAD_DIM
SSD_GROUPS = 2
SSD_STATE = 128
SSD_CHUNK = 128
SSD_CONV_CH = SSD_W + 2 * SSD_GROUPS * SSD_STATE

HG_W = MIX_W // 2
HG_EXPAND = 128
HG_HEADS = HG_W // HG_EXPAND
HG_VDIM = HG_W // HG_HEADS
HG_CHUNK = 64

N_EXPERTS = 32
TOP_K = 4
D_EXPERT = D_MODEL
SWIGLU_ALPHA = 1.702
SWIGLU_LIMIT = 7.0
MOE_BLOCK = 128

EVEN_IN = 2 * ATT_KV_W + ATT_Q_W + 3 * HY_W
ODD_STATE_COLS = SSD_CONV_CH + 2 * SSD_HEADS + 3 * HG_W
ODD_IN = ODD_STATE_COLS + SSD_W + 2 * HG_W

LANE = 128
SUBLANE = 8


def _mm_body(a_ref, b_ref, o_ref):
    a = a_ref[...].astype(jnp.bfloat16)
    b = b_ref[...].astype(jnp.bfloat16)
    o_ref[...] = jnp.dot(a, b, preferred_element_type=jnp.float32)


def _pick_tile(n, candidates):
    for c in candidates:
        if n % c == 0:
            return c
    return n


def pmm(a, b):
    m, k = a.shape
    n = b.shape[1]
    n_pad = -(-n // LANE) * LANE
    if n_pad != n:
        b = jnp.pad(b, ((0, 0), (0, n_pad - n)))
    m_pad = -(-m // SUBLANE) * SUBLANE
    if m_pad != m:
        a = jnp.pad(a, ((0, m_pad - m), (0, 0)))
    tm = _pick_tile(m_pad, (512, 256, 128, 64, 32, 16, 8))
    tn = _pick_tile(n_pad, (512, 384, 256, 128))
    out = pl.pallas_call(
        _mm_body,
        grid=(m_pad // tm, n_pad // tn),
        in_specs=[pl.BlockSpec((tm, k), lambda i, j: (i, 0)),
                  pl.BlockSpec((k, tn), lambda i, j: (0, j))],
        out_specs=pl.BlockSpec((tm, tn), lambda i, j: (i, j)),
        out_shape=jax.ShapeDtypeStruct((m_pad, n_pad), jnp.float32),
        name="dense_mm",
    )(a, b)
    return out[:m, :n]


def pmm3(a, b):
    lead = a.shape[:-1]
    return pmm(a.reshape(-1, a.shape[-1]), b).reshape(*lead, b.shape[-1])


def rms_norm(x, g):
    xf = x.astype(jnp.float32)
    y = xf * lax.rsqrt(jnp.mean(xf * xf, axis=-1, keepdims=True) + EPS)
    return (y * g.astype(jnp.float32)).astype(x.dtype)


def modulate(h, shift, scale):
    return h * (1.0 + scale) + shift


def adaln(cond, w, b, j):
    lo, hi = 3 * j * D_MODEL, 3 * (j + 1) * D_MODEL
    m = jax.nn.silu(cond) @ w[:, lo:hi] + b[lo:hi]
    return jnp.split(m, 3, axis=-1)


def dwconv_centred(u, w, b):
    ch = u.shape[-1]
    y = lax.conv_general_dilated(u, w[:, None, :].astype(u.dtype), window_strides=(1,),
                                 padding=[(CONV_W // 2, CONV_W // 2)],
                                 dimension_numbers=('NWC', 'WIO', 'NWC'), feature_group_count=ch)
    return y + b.astype(u.dtype)


def axial_rope_tables(length):
    rows = length // GRID_W
    n_pairs = HEAD_DIM // 4
    inv = ROPE_BASE ** (-jnp.arange(n_pairs, dtype=jnp.float32) / n_pairs)
    row_ang = jnp.arange(rows, dtype=jnp.float32)[:, None] * inv
    col_ang = jnp.arange(GRID_W, dtype=jnp.float32)[:, None] * inv
    ang_r = jnp.broadcast_to(row_ang[:, None], (rows, GRID_W, n_pairs)).reshape(length, n_pairs)
    ang_c = jnp.broadcast_to(col_ang[None], (rows, GRID_W, n_pairs)).reshape(length, n_pairs)
    return jnp.cos(ang_r), jnp.sin(ang_r), jnp.cos(ang_c), jnp.sin(ang_c)


def _rotate(u, cos, sin):
    n = u.shape[-1] // 2
    u1, u2 = u[..., :n], u[..., n:]
    cos = cos[None, :, None, :]
    sin = sin[None, :, None, :]
    return jnp.concatenate([u1 * cos - u2 * sin, u1 * sin + u2 * cos], axis=-1)


def apply_axial_rope(u, tables):
    cr, sr, cc, sc = tables
    half = HEAD_DIM // 2
    return jnp.concatenate([_rotate(u[..., :half], cr, sr), _rotate(u[..., half:], cc, sc)], axis=-1)


def hyena_filters(length, w1, b1, w2, b2, w3, b3, w4, freq):
    f32 = jnp.float32
    t = jnp.linspace(0.0, 1.0, length, dtype=f32)[:, None]
    bands = (HY_EMB - 1) // 2
    w_ang = 2.0 * math.pi * jnp.arange(length, dtype=f32)[:, None] / length
    fr = jnp.linspace(1e-4, bands - 1, bands, dtype=f32)[None]
    z = jnp.concatenate([t, jnp.cos(fr * w_ang), -jnp.sin(fr * w_ang)], axis=-1)
    fq = freq.astype(f32)
    hdn = jnp.sin(fq * (z @ w1.astype(f32) + b1.astype(f32)))
    hdn = jnp.sin(fq * (hdn @ w2.astype(f32) + b2.astype(f32)))
    hdn = jnp.sin(fq * (hdn @ w3.astype(f32) + b3.astype(f32)))
    h = (hdn @ w4.astype(f32)).reshape(length, HY_ORDER, 2, HY_W)
    max_decay = math.log(HY_TARGET) / HY_SHORT_PCT
    min_decay = math.log(HY_TARGET) / HY_LONG_PCT
    deltas = jnp.abs(jnp.linspace(min_decay, max_decay, HY_W, dtype=f32))
    h = h * jnp.exp(-t * deltas)[:, None, None, :]
    h2 = jnp.concatenate([h[:, :, 0], jnp.zeros((1, HY_ORDER, HY_W), f32), h[:0:-1, :, 1]], axis=0)
    h2 = h2 / jnp.sum(jnp.abs(h2), axis=0, keepdims=True)
    return jnp.fft.rfft(h2, axis=0)


def hyena_mix(u, hf, filter_bias, conv_w, conv_b):
    length = u.shape[1]
    u = dwconv_centred(u.astype(jnp.float32), conv_w, conv_b)
    v, x1, x2 = jnp.split(u, 3, axis=-1)
    z = v
    for o, gate in enumerate((x1, x2)):
        zf = jnp.fft.rfft(z, n=2 * length, axis=1)
        zc = jnp.fft.irfft(zf * hf[None, :, o], n=2 * length, axis=1)[:, :length]
        z = gate * (zc + z * filter_bias[o].astype(jnp.float32))
    return z


def window_attention(q, k, v, k_c, v_c, sink):
    bsz, length = q.shape[:2]
    nb = length // ATT_BLOCK
    scale = HEAD_DIM ** -0.5
    qb = q.reshape(bsz, nb, ATT_BLOCK, ATT_KV_HEADS, ATT_GROUP, HEAD_DIM)
    pad = ((0, 0), (ATT_BLOCK, ATT_BLOCK), (0, 0), (0, 0))

    def band(a):
        ap = jnp.pad(a, pad).reshape(bsz, nb + 2, ATT_BLOCK, ATT_KV_HEADS, HEAD_DIM)
        return jnp.concatenate([ap[:, :-2], ap[:, 1:-1], ap[:, 2:]], axis=2)

    kw, vw = band(k), band(v)
    s_loc = jnp.einsum('bnqhgd,bnkhd->bnhgqk', qb, kw) * scale
    s_ctx = jnp.einsum('bnqhgd,bchd->bnhgqc', qb, k_c) * scale
    qpos = jnp.arange(nb)[:, None] * ATT_BLOCK + jnp.arange(ATT_BLOCK)[None]
    kpos = (jnp.arange(nb)[:, None] - 1) * ATT_BLOCK + jnp.arange(3 * ATT_BLOCK)[None]
    rel = kpos[:, None, :] - qpos[:, :, None]
    valid = (jnp.abs(rel) <= ATT_WINDOW) & (kpos[:, None, :] >= 0) & (kpos[:, None, :] < length)
    s_loc = jnp.where(valid[None, :, None, None], s_loc, -jnp.inf)
    sink_l = jnp.broadcast_to(sink.astype(jnp.float32).reshape(1, 1, ATT_KV_HEADS, ATT_GROUP, 1, 1),
                              s_loc.shape[:-1] + (1,))
    p = jax.nn.softmax(jnp.concatenate([s_loc, s_ctx, sink_l], axis=-1), axis=-1)
    n_loc = 3 * ATT_BLOCK
    n_ctx = k_c.shape[1]
    o = (jnp.einsum('bnhgqk,bnkhd->bnqhgd', p[..., :n_loc], vw)
         + jnp.einsum('bnhgqc,bchd->bnqhgd', p[..., n_loc:n_loc + n_ctx], v_c))
    return o.reshape(bsz, length, ATT_Q_W)


def context_attention(q_c, k_c, v_c, sink):
    bsz, n_ctx = q_c.shape[:2]
    s = jnp.einsum('bqhgd,bkhd->bhgqk', q_c, k_c) * HEAD_DIM ** -0.5
    sink_l = jnp.broadcast_to(sink.astype(jnp.float32).reshape(1, ATT_KV_HEADS, ATT_GROUP, 1, 1),
                              s.shape[:-1] + (1,))
    p = jax.nn.softmax(jnp.concatenate([s, sink_l], axis=-1), axis=-1)[..., :-1]
    return jnp.einsum('bhgqk,bkhd->bqhgd', p, v_c).reshape(bsz, n_ctx, ATT_Q_W)


def ssd_scan(x, dt, a, bm, cm, d_skip, init, need_y):
    bsz, length, n_heads, hd = x.shape
    nc = length // SSD_CHUNK
    hpg = n_heads // SSD_GROUPS
    da = (dt * a).reshape(bsz, nc, SSD_CHUNK, SSD_GROUPS, hpg)
    cs = jnp.cumsum(da, axis=2)
    xdt = (x * dt[..., None]).reshape(bsz, nc, SSD_CHUNK, SSD_GROUPS, hpg, hd)
    bc = bm.reshape(bsz, nc, SSD_CHUNK, SSD_GROUPS, SSD_STATE)
    cc = cm.reshape(bsz, nc, SSD_CHUNK, SSD_GROUPS, SSD_STATE)
    to_end = jnp.exp(cs[:, :, -1:] - cs)
    states = jnp.einsum('bcsgn,bcsgh,bcsghp->bcghpn', bc, to_end, xdt)
    chunk_decay = jnp.exp(cs[:, :, -1])

    def step(s, inp):
        st, dec = inp
        return s * dec[..., None, None] + st, s

    s_final, s_in = lax.scan(step, init, (jnp.moveaxis(states, 1, 0), jnp.moveaxis(chunk_decay, 1, 0)))
    if not need_y:
        return None, s_final
    s_in = jnp.moveaxis(s_in, 0, 1)
    cs_t = jnp.moveaxis(cs, 2, -1)
    diff = cs_t[..., :, None] - cs_t[..., None, :]
    lower = jnp.tril(jnp.ones((SSD_CHUNK, SSD_CHUNK), bool))
    decay = jnp.where(lower, jnp.exp(jnp.where(lower, diff, 0.0)), 0.0)
    scores = jnp.einsum('bclgn,bcsgn->bcgls', cc, bc)
    y_diag = jnp.einsum('bcgls,bcghls,bcsghp->bclghp', scores, decay, xdt)
    y_off = jnp.einsum('bclgn,bcghpn,bclgh->bclghp', cc, s_in, jnp.exp(cs))
    y = (y_diag + y_off).reshape(bsz, length, n_heads, hd) + d_skip[:, None] * x
    return y, s_final


def hgrn2_scan(q, k, v, g, init, need_o):
    bsz, length, n_heads, _ = k.shape
    nc = length // HG_CHUNK

    def chunks(a):
        return a.reshape(bsz, nc, HG_CHUNK, n_heads, a.shape[-1]).transpose(1, 0, 3, 2, 4)

    lower = jnp.tril(jnp.ones((HG_CHUNK, HG_CHUNK), bool))[:, :, None]

    def update(s, kc, vc, cum):
        last = cum[:, :, -1]
        return (s * jnp.exp(last)[..., None]
                + jnp.einsum('bhsk,bhsv->bhkv', kc * jnp.exp(last[:, :, None] - cum), vc))

    if not need_o:
        def step_state(s, inp):
            kc, vc, gc = inp
            return update(s, kc, vc, jnp.cumsum(gc, axis=2)), None
        s_final, _ = lax.scan(step_state, init, (chunks(k), chunks(v), chunks(g)))
        return None, s_final

    def step(s, inp):
        qc, kc, vc, gc = inp
        cum = jnp.cumsum(gc, axis=2)
        diff = cum[:, :, :, None, :] - cum[:, :, None, :, :]
        decay = jnp.where(lower, jnp.exp(jnp.where(lower, diff, 0.0)), 0.0)
        att = jnp.einsum('bhtk,bhsk,bhtsk->bhts', qc, kc, decay)
        o = (jnp.einsum('bhtk,bhkv->bhtv', qc * jnp.exp(cum), s)
             + jnp.einsum('bhts,bhsv->bhtv', att, vc))
        return update(s, kc, vc, cum), o

    s_final, o = lax.scan(step, init, (chunks(q), chunks(k), chunks(v), chunks(g)))
    return o.transpose(1, 0, 3, 2, 4).reshape(bsz, length, n_heads, v.shape[-1]), s_final


def even_mixer(h, hc, w_in, conv_w, conv_b, f_w1, f_b1, f_w2, f_b2, f_w3, f_b3, f_w4, f_freq, f_bias,
               q_norm, k_norm, sink, ctx_out):
    f32 = jnp.float32
    bsz, length, _ = h.shape
    n_ctx = hc.shape[1]
    filt = (f_w1, f_b1, f_w2, f_b2, f_w3, f_b3, f_w4, f_freq)
    o_v = ATT_KV_W
    o_q = 2 * ATT_KV_W
    o_hy = o_q + ATT_Q_W
    p = pmm3(h, w_in).astype(f32)
    pc = pmm3(hc, w_in if ctx_out else w_in[:, :o_q]).astype(f32)

    def heads(a, n_heads):
        return a.reshape(a.shape[0], a.shape[1], n_heads, HEAD_DIM)

    k_c = rms_norm(heads(pc[..., :o_v], ATT_KV_HEADS), k_norm)
    v_c = heads(pc[..., o_v:o_q], ATT_KV_HEADS)
    rope = axial_rope_tables(length)
    k = apply_axial_rope(rms_norm(heads(p[..., :o_v], ATT_KV_HEADS), k_norm), rope)
    v = heads(p[..., o_v:o_q], ATT_KV_HEADS)
    q = apply_axial_rope(rms_norm(heads(p[..., o_q:o_hy], ATT_HEADS), q_norm), rope)
    att = window_attention(q.reshape(bsz, length, ATT_KV_HEADS, ATT_GROUP, HEAD_DIM), k, v, k_c, v_c, sink)
    hy = hyena_mix(p[..., o_hy:], hyena_filters(length, *filt), f_bias, conv_w, conv_b)
    out = jnp.concatenate([hy, att], axis=-1).astype(h.dtype)
    if not ctx_out:
        return out, None
    q_c = rms_norm(heads(pc[..., o_q:o_hy], ATT_HEADS), q_norm).reshape(bsz, n_ctx, ATT_KV_HEADS, ATT_GROUP, HEAD_DIM)
    att_c = context_attention(q_c, k_c, v_c, sink)
    hy_c = hyena_mix(pc[..., o_hy:], hyena_filters(n_ctx, *filt), f_bias, conv_w, conv_b)
    return out, jnp.concatenate([hy_c, att_c], axis=-1).astype(hc.dtype)


def odd_mixer(h, hc, lb, w_in, conv_w, conv_b, dt_bias, a_log, d_skip, ssd_norm, hg_norm, ctx_out):
    f32 = jnp.float32
    bsz, length, _ = h.shape
    n_ctx = hc.shape[1]
    p = pmm3(h, w_in).astype(f32)
    pc = pmm3(hc, w_in if ctx_out else w_in[:, :ODD_STATE_COLS]).astype(f32)
    o_dt = SSD_CONV_CH
    o_f = SSD_CONV_CH + 2 * SSD_HEADS
    o_i = o_f + 2 * HG_W
    o_z = ODD_STATE_COLS
    o_q = o_z + SSD_W
    o_g = o_q + HG_W
    gn = SSD_GROUPS * SSD_STATE

    def streams(pp):
        n = pp.shape[1]
        xbc = jax.nn.silu(dwconv_centred(pp[..., :SSD_CONV_CH], conv_w, conv_b))
        xs = xbc[..., :SSD_W].reshape(bsz, n, SSD_HEADS, SSD_HEAD_DIM)
        bm = xbc[..., SSD_W:SSD_W + gn].reshape(bsz, n, SSD_GROUPS, SSD_STATE)
        cm = xbc[..., SSD_W + gn:].reshape(bsz, n, SSD_GROUPS, SSD_STATE)
        dt_raw = pp[..., o_dt:o_f].reshape(bsz, n, 2, SSD_HEADS)
        f_raw = pp[..., o_f:o_i].reshape(bsz, n, 2, HG_HEADS, HG_EXPAND)
        iv = pp[..., o_i:o_i + HG_W].reshape(bsz, n, HG_HEADS, HG_VDIM)
        return xs, bm, cm, dt_raw, f_raw, iv

    xs, bm, cm, dt_raw, f_raw, iv = streams(p)
    xs_c, bm_c, cm_c, dt_raw_c, f_raw_c, iv_c = streams(pc)
    q = jax.nn.silu(p[..., o_q:o_g]).reshape(bsz, length, HG_HEADS, HG_EXPAND)
    q_c = jax.nn.silu(pc[..., o_q:o_g]).reshape(bsz, n_ctx, HG_HEADS, HG_EXPAND) if ctx_out else None
    lb = lb.astype(f32).reshape(HG_HEADS, HG_EXPAND)
    ssd0 = jnp.zeros((bsz, SSD_GROUPS, SSD_HEADS // SSD_GROUPS, SSD_HEAD_DIM, SSD_STATE), f32)
    hg0 = jnp.zeros((bsz, HG_HEADS, HG_EXPAND, HG_VDIM), f32)
    y_dirs, o_dirs, yc_dirs, oc_dirs = [], [], [], []
    for d in range(2):
        fl = (lambda a: jnp.flip(a, axis=1)) if d == 1 else (lambda a: a)
        a = -jnp.exp(a_log[d].astype(f32))
        dsk = d_skip[d].astype(f32)
        dtb = dt_bias[d].astype(f32)
        dt_l = jax.nn.softplus(dt_raw[:, :, d] + dtb)
        dt_c = jax.nn.softplus(dt_raw_c[:, :, d] + dtb)
        yc, s_ctx = ssd_scan(fl(xs_c), fl(dt_c), a, fl(bm_c), fl(cm_c), dsk, ssd0, ctx_out)
        yl, _ = ssd_scan(fl(xs), fl(dt_l), a, fl(bm), fl(cm), dsk, s_ctx, True)
        y_dirs.append(fl(yl))
        f_l = lb + (1.0 - lb) * jax.nn.sigmoid(f_raw[:, :, d])
        f_c = lb + (1.0 - lb) * jax.nn.sigmoid(f_raw_c[:, :, d])
        oc, s_hg = hgrn2_scan(fl(q_c) if ctx_out else None, fl(1.0 - f_c), fl(iv_c), fl(jnp.log(f_c)), hg0, ctx_out)
        ol, _ = hgrn2_scan(fl(q), fl(1.0 - f_l), fl(iv), fl(jnp.log(f_l)), s_hg, True)
        o_dirs.append(fl(ol))
        if ctx_out:
            yc_dirs.append(fl(yc))
            oc_dirs.append(fl(oc))

    def merge(yy, oo, pp, n):
        z = pp[..., o_z:o_q]
        g = pp[..., o_g:]
        ys = (yy.reshape(bsz, n, SSD_W) * jax.nn.silu(z)).reshape(bsz, n, SSD_GROUPS, SSD_W // SSD_GROUPS)
        ys = rms_norm(ys, ssd_norm.reshape(SSD_GROUPS, SSD_W // SSD_GROUPS)).reshape(bsz, n, SSD_W)
        hs = rms_norm(oo, hg_norm.reshape(HG_HEADS, HG_VDIM)).reshape(bsz, n, HG_W) * jax.nn.silu(g)
        return jnp.concatenate([ys, hs], axis=-1)

    out = merge(y_dirs[0] + y_dirs[1], o_dirs[0] + o_dirs[1], p, length).astype(h.dtype)
    if not ctx_out:
        return out, None
    out_c = merge(yc_dirs[0] + yc_dirs[1], oc_dirs[0] + oc_dirs[1], pc, n_ctx).astype(hc.dtype)
    return out, out_c


def moe_ffn(t, router_w, router_b, w_gu, b_gu, w_dn, b_dn):
    n_tok, d = t.shape
    logits = (t @ router_w).astype(jnp.float32) + router_b.astype(jnp.float32)
    top_val, top_idx = lax.top_k(logits, TOP_K)
    gates = jax.nn.softmax(top_val, axis=-1)
    flat_e = top_idx.reshape(-1)
    order = jnp.argsort(flat_e)
    e_sorted = flat_e[order]
    counts = jnp.bincount(flat_e, length=N_EXPERTS)
    padded = (counts + MOE_BLOCK - 1) // MOE_BLOCK * MOE_BLOCK
    pad_end = jnp.cumsum(padded)
    first = jnp.cumsum(counts) - counts
    dest = pad_end[e_sorted] - padded[e_sorted] + jnp.arange(n_tok * TOP_K) - first[e_sorted]
    n_blk = -(-(n_tok * TOP_K) // MOE_BLOCK) + N_EXPERTS
    n_rows = n_blk * MOE_BLOCK
    row_tok = jnp.full((n_rows,), n_tok, jnp.int32).at[dest].set((order // TOP_K).astype(jnp.int32))
    row_gate = jnp.zeros((n_rows,), jnp.float32).at[dest].set(gates.reshape(-1)[order])
    blk_e = jnp.minimum(jnp.searchsorted(pad_end, jnp.arange(n_blk) * MOE_BLOCK, side='right'), N_EXPERTS - 1)
    x_rows = jnp.concatenate([t, jnp.zeros((1, d), t.dtype)], axis=0)[row_tok].reshape(n_blk, MOE_BLOCK, d)

    def expert_block(args):
        xb, e = args
        gu = xb @ w_gu[e] + b_gu[e]
        gate = jnp.minimum(gu[:, :D_EXPERT], SWIGLU_LIMIT)
        up = jnp.clip(gu[:, D_EXPERT:], -SWIGLU_LIMIT, SWIGLU_LIMIT)
        act = (up + 1.0) * gate * jax.nn.sigmoid(SWIGLU_ALPHA * gate)
        return act @ w_dn[e] + b_dn[e]

    y_rows = lax.map(expert_block, (x_rows, blk_e)).reshape(n_rows, d)
    y = jax.ops.segment_sum(y_rows * row_gate[:, None].astype(y_rows.dtype), row_tok, num_segments=n_tok + 1)
    return y[:n_tok]


def kernel(x, c, ctx, c_ctx, norm_g, ada_w, ada_b, w_out, w_in_even, hy_conv_w, hy_conv_b,
           hy_w1, hy_b1, hy_w2, hy_b2, hy_w3, hy_b3, hy_w4, hy_freq, hy_filter_bias,
           att_q_norm, att_k_norm, att_sink, w_in_odd, ssd_conv_w, ssd_conv_b, ssd_dt_bias,
           ssd_A_log, ssd_D, ssd_norm, hg_lower_bounds, hg_norm, router_w, router_b,
           moe_w_gu, moe_b_gu, moe_w_dn, moe_b_dn):
    lbs = jax.nn.softmax(hg_lower_bounds.astype(jnp.float32), axis=0)
    lbs = jnp.cumsum(lbs, axis=0) - lbs[0]
    xc = ctx
    for layer in range(DEPTH):
        ctx_out = layer < DEPTH - 1
        i = layer // 2
        sh, sc, gt = adaln(c, ada_w[layer], ada_b[layer], 0)
        sh_c, sc_c, gt_c = adaln(c_ctx, ada_w[layer], ada_b[layer], 0)
        h = modulate(rms_norm(x, norm_g[layer, 0]), sh[:, None], sc[:, None])
        hc = modulate(rms_norm(xc, norm_g[layer, 0]), sh_c, sc_c)
        if layer % 2 == 0:
            m, m_c = even_mixer(h, hc, w_in_even[i], hy_conv_w[i], hy_conv_b[i], hy_w1[i], hy_b1[i],
                                hy_w2[i], hy_b2[i], hy_w3[i], hy_b3[i], hy_w4[i], hy_freq[i],
                                hy_filter_bias[i], att_q_norm[i], att_k_norm[i], att_sink[i], ctx_out)
        else:
            m, m_c = odd_mixer(h, hc, lbs[layer], w_in_odd[i], ssd_conv_w[i], ssd_conv_b[i], ssd_dt_bias[i],
                               ssd_A_log[i], ssd_D[i], ssd_norm[i], hg_norm[i], ctx_out)
        x = x + gt[:, None] * pmm3(m, w_out[layer])
        if ctx_out:
            xc = xc + gt_c * pmm3(m_c, w_out[layer])
        sh, sc, gt = adaln(c, ada_w[layer], ada_b[layer], 1)
        h = modulate(rms_norm(x, norm_g[layer, 1]), sh[:, None], sc[:, None])
        moe_params = (router_w[layer], router_b[layer], moe_w_gu[layer], moe_b_gu[layer],
                      moe_w_dn[layer], moe_b_dn[layer])
        n_lat = x.shape[0] * x.shape[1]
        if ctx_out:
            sh_c, sc_c, gt_c = adaln(c_ctx, ada_w[layer], ada_b[layer], 1)
            hc = modulate(rms_norm(xc, norm_g[layer, 1]), sh_c, sc_c)
            y = moe_ffn(jnp.concatenate([h.reshape(n_lat, D_MODEL), hc.reshape(-1, D_MODEL)], axis=0), *moe_params)
            x = x + gt[:, None] * y[:n_lat].reshape(x.shape)
            xc = xc + gt_c * y[n_lat:].reshape(xc.shape)
        else:
            x = x + gt[:, None] * moe_ffn(h.reshape(n_lat, D_MODEL), *moe_params).reshape(x.shape)
    return x
```

```python
import functools
import math

import jax
import jax.numpy as jnp
from jax import lax
from jax.experimental import pallas as pl
from jax.experimental.pallas import tpu as pltpu

D_MODEL = 1024
DEPTH = 2
GRID_W = 64
MIX_W = D_MODEL
EPS = 1e-6
CONV_W = 3

HY_W = MIX_W // 2
HY_ORDER = 2
HY_EMB = 33
HY_FFN = 64
HY_TARGET = 1e-2
HY_SHORT_PCT = 0.3
HY_LONG_PCT = 1.5

HEAD_DIM = 64
ATT_HEADS = (MIX_W // 2) // HEAD_DIM
ATT_KV_HEADS = 2
ATT_GROUP = ATT_HEADS // ATT_KV_HEADS
ATT_WINDOW = 128
ATT_BLOCK = 128
ROPE_BASE = 10000.0
ATT_Q_W = ATT_HEADS * HEAD_DIM
ATT_KV_W = ATT_KV_HEADS * HEAD_DIM

SSD_W = MIX_W // 2
SSD_HEAD_DIM = 64
SSD_HEADS = SSD_W // SSD_HEAD_DIM
SSD_GROUPS = 2
SSD_STATE = 128
SSD_CHUNK = 128
SSD_CONV_CH = SSD_W + 2 * SSD_GROUPS * SSD_STATE

HG_W = MIX_W // 2
HG_EXPAND = 128
HG_HEADS = HG_W // HG_EXPAND
HG_VDIM = HG_W // HG_HEADS
HG_CHUNK = 64

N_EXPERTS = 32
TOP_K = 4
D_EXPERT = D_MODEL
SWIGLU_ALPHA = 1.702
SWIGLU_LIMIT = 7.0
MOE_BLOCK = 128

EVEN_IN = 2 * ATT_KV_W + ATT_Q_W + 3 * HY_W
ODD_STATE_COLS = SSD_CONV_CH + 2 * SSD_HEADS + 3 * HG_W
ODD_IN = ODD_STATE_COLS + SSD_W + 2 * HG_W

LANE = 128
SUBLANE = 8


def _mm_body(a_ref, b_ref, o_ref):
    a = a_ref[...].astype(jnp.bfloat16)
    b = b_ref[...].astype(jnp.bfloat16)
    o_ref[...] = jnp.dot(a, b, preferred_element_type=jnp.float32)


def _pick_tile(n, candidates):
    for c in candidates:
        if n % c == 0:
            return c
    return n


def pmm(a, b):
    m, k = a.shape
    n = b.shape[1]
    n_pad = -(-n // LANE) * LANE
    if n_pad != n:
        b = jnp.pad(b, ((0, 0), (0, n_pad - n)))
    m_pad = -(-m // SUBLANE) * SUBLANE
    if m_pad != m:
        a = jnp.pad(a, ((0, m_pad - m), (0, 0)))
    tm = _pick_tile(m_pad, (512, 256, 128, 64, 32, 16, 8))
    tn = _pick_tile(n_pad, (512, 384, 256, 128))
    out = pl.pallas_call(
        _mm_body,
        grid=(m_pad // tm, n_pad // tn),
        in_specs=[pl.BlockSpec((tm, k), lambda i, j: (i, 0)),
                  pl.BlockSpec((k, tn), lambda i, j: (0, j))],
        out_specs=pl.BlockSpec((tm, tn), lambda i, j: (i, j)),
        out_shape=jax.ShapeDtypeStruct((m_pad, n_pad), jnp.float32),
        name="dense_mm",
    )(a, b)
    return out[:m, :n]


def pmm3(a, b):
    lead = a.shape[:-1]
    return pmm(a.reshape(-1, a.shape[-1]), b).reshape(*lead, b.shape[-1])


def rms_norm(x, g):
    xf = x.astype(jnp.float32)
    y = xf * lax.rsqrt(jnp.mean(xf * xf, axis=-1, keepdims=True) + EPS)
    return (y * g.astype(jnp.float32)).astype(x.dtype)


def modulate(h, shift, scale):
    return h * (1.0 + scale) + shift


def adaln(cond, w, b, j):
    lo, hi = 3 * j * D_MODEL, 3 * (j + 1) * D_MODEL
    m = jax.nn.silu(cond) @ w[:, lo:hi] + b[lo:hi]
    return jnp.split(m, 3, axis=-1)


def dwconv_centred(u, w, b):
    ch = u.shape[-1]
    y = lax.conv_general_dilated(u, w[:, None, :].astype(u.dtype), window_strides=(1,),
                                 padding=[(CONV_W // 2, CONV_W // 2)],
                                 dimension_numbers=('NWC', 'WIO', 'NWC'), feature_group_count=ch)
    return y + b.astype(u.dtype)


def axial_rope_tables(length):
    rows = length // GRID_W
    n_pairs = HEAD_DIM // 4
    inv = ROPE_BASE ** (-jnp.arange(n_pairs, dtype=jnp.float32) / n_pairs)
    row_ang = jnp.arange(rows, dtype=jnp.float32)[:, None] * inv
    col_ang = jnp.arange(GRID_W, dtype=jnp.float32)[:, None] * inv
    ang_r = jnp.broadcast_to(row_ang[:, None], (rows, GRID_W, n_pairs)).reshape(length, n_pairs)
    ang_c = jnp.broadcast_to(col_ang[None], (rows, GRID_W, n_pairs)).reshape(length, n_pairs)
    return jnp.cos(ang_r), jnp.sin(ang_r), jnp.cos(ang_c), jnp.sin(ang_c)


def _rotate(u, cos, sin):
    n = u.shape[-1] // 2
    u1, u2 = u[..., :n], u[..., n:]
    cos = cos[None, :, None, :]
    sin = sin[None, :, None, :]
    return jnp.concatenate([u1 * cos - u2 * sin, u1 * sin + u2 * cos], axis=-1)


def apply_axial_rope(u, tables):
    cr, sr, cc, sc = tables
    half = HEAD_DIM // 2
    return jnp.concatenate([_rotate(u[..., :half], cr, sr), _rotate(u[..., half:], cc, sc)], axis=-1)


def hyena_filters(length, w1, b1, w2, b2, w3, b3, w4, freq):
    f32 = jnp.float32
    t = jnp.linspace(0.0, 1.0, length, dtype=f32)[:, None]
    bands = (HY_EMB - 1) // 2
    w_ang = 2.0 * math.pi * jnp.arange(length, dtype=f32)[:, None] / length
    fr = jnp.linspace(1e-4, bands - 1, bands, dtype=f32)[None]
    z = jnp.concatenate([t, jnp.cos(fr * w_ang), -jnp.sin(fr * w_ang)], axis=-1)
    fq = freq.astype(f32)
    hdn = jnp.sin(fq * (z @ w1.astype(f32) + b1.astype(f32)))
    hdn = jnp.sin(fq * (hdn @ w2.astype(f32) + b2.astype(f32)))
    hdn = jnp.sin(fq * (hdn @ w3.astype(f32) + b3.astype(f32)))
    h = (hdn @ w4.astype(f32)).reshape(length, HY_ORDER, 2, HY_W)
    max_decay = math.log(HY_TARGET) / HY_SHORT_PCT
    min_decay = math.log(HY_TARGET) / HY_LONG_PCT
    deltas = jnp.abs(jnp.linspace(min_decay, max_decay, HY_W, dtype=f32))
    h = h * jnp.exp(-t * deltas)[:, None, None, :]
    h2 = jnp.concatenate([h[:, :, 0], jnp.zeros((1, HY_ORDER, HY_W), f32), h[:0:-1, :, 1]], axis=0)
    h2 = h2 / jnp.sum(jnp.abs(h2), axis=0, keepdims=True)
    return jnp.fft.rfft(h2, axis=0)


def hyena_mix(u, hf, filter_bias, conv_w, conv_b):
    length = u.shape[1]
    u = dwconv_centred(u.astype(jnp.float32), conv_w, conv_b)
    v, x1, x2 = jnp.split(u, 3, axis=-1)
    z = v
    for o, gate in enumerate((x1, x2)):
        zf = jnp.fft.rfft(z, n=2 * length, axis=1)
        zc = jnp.fft.irfft(zf * hf[None, :, o], n=2 * length, axis=1)[:, :length]
        z = gate * (zc + z * filter_bias[o].astype(jnp.float32))
    return z


def window_attention(q, k, v, k_c, v_c, sink):
    bsz, length = q.shape[:2]
    nb = length // ATT_BLOCK
    scale = HEAD_DIM ** -0.5
    qb = q.reshape(bsz, nb, ATT_BLOCK, ATT_KV_HEADS, ATT_GROUP, HEAD_DIM)
    pad = ((0, 0), (ATT_BLOCK, ATT_BLOCK), (0, 0), (0, 0))

    def band(a):
        ap = jnp.pad(a, pad).reshape(bsz, nb + 2, ATT_BLOCK, ATT_KV_HEADS, HEAD_DIM)
        return jnp.concatenate([ap[:, :-2], ap[:, 1:-1], ap[:, 2:]], axis=2)

    kw, vw = band(k), band(v)
    s_loc = jnp.einsum('bnqhgd,bnkhd->bnhgqk', qb, kw) * scale
    s_ctx = jnp.einsum('bnqhgd,bchd->bnhgqc', qb, k_c) * scale
    qpos = jnp.arange(nb)[:, None] * ATT_BLOCK + jnp.arange(ATT_BLOCK)[None]
    kpos = (jnp.arange(nb)[:, None] - 1) * ATT_BLOCK + jnp.arange(3 * ATT_BLOCK)[None]
    rel = kpos[:, None, :] - qpos[:, :, None]
    valid = (jnp.abs(rel) <= ATT_WINDOW) & (kpos[:, None, :] >= 0) & (kpos[:, None, :] < length)
    s_loc = jnp.where(valid[None, :, None, None], s_loc, -jnp.inf)
    sink_l = jnp.broadcast_to(sink.astype(jnp.float32).reshape(1, 1, ATT_KV_HEADS, ATT_GROUP, 1, 1),
                              s_loc.shape[:-1] + (1,))
    p = jax.nn.softmax(jnp.concatenate([s_loc, s_ctx, sink_l], axis=-1), axis=-1)
    n_loc = 3 * ATT_BLOCK
    n_ctx = k_c.shape[1]
    o = (jnp.einsum('bnhgqk,bnkhd->bnqhgd', p[..., :n_loc], vw)
         + jnp.einsum('bnhgqc,bchd->bnqhgd', p[..., n_loc:n_loc + n_ctx], v_c))
    return o.reshape(bsz, length, ATT_Q_W)


def context_attention(q_c, k_c, v_c, sink):
    bsz, n_ctx = q_c.shape[:2]
    s = jnp.einsum('bqhgd,bkhd->bhgqk', q_c, k_c) * HEAD_DIM ** -0.5
    sink_l = jnp.broadcast_to(sink.astype(jnp.float32).reshape(1, ATT_KV_HEADS, ATT_GROUP, 1, 1),
                              s.shape[:-1] + (1,))
    p = jax.nn.softmax(jnp.concatenate([s, sink_l], axis=-1), axis=-1)[..., :-1]
    return jnp.einsum('bhgqk,bkhd->bqhgd', p, v_c).reshape(bsz, n_ctx, ATT_Q_W)


def ssd_scan(x, dt, a, bm, cm, d_skip, init, need_y):
    bsz, length, n_heads, hd = x.shape
    nc = length // SSD_CHUNK
    hpg = n_heads // SSD_GROUPS
    da = (dt * a).reshape(bsz, nc, SSD_CHUNK, SSD_GROUPS, hpg)
    cs = jnp.cumsum(da, axis=2)
    xdt = (x * dt[..., None]).reshape(bsz, nc, SSD_CHUNK, SSD_GROUPS, hpg, hd)
    bc = bm.reshape(bsz, nc, SSD_CHUNK, SSD_GROUPS, SSD_STATE)
    cc = cm.reshape(bsz, nc, SSD_CHUNK, SSD_GROUPS, SSD_STATE)
    to_end = jnp.exp(cs[:, :, -1:] - cs)
    states = jnp.einsum('bcsgn,bcsgh,bcsghp->bcghpn', bc, to_end, xdt)
    chunk_decay = jnp.exp(cs[:, :, -1])

    def step(s, inp):
        st, dec = inp
        return s * dec[..., None, None] + st, s

    s_final, s_in = lax.scan(step, init, (jnp.moveaxis(states, 1, 0), jnp.moveaxis(chunk_decay, 1, 0)))
    if not need_y:
        return None, s_final
    s_in = jnp.moveaxis(s_in, 0, 1)
    cs_t = jnp.moveaxis(cs, 2, -1)
    diff = cs_t[..., :, None] - cs_t[..., None, :]
    lower = jnp.tril(jnp.ones((SSD_CHUNK, SSD_CHUNK), bool))
    decay = jnp.where(lower, jnp.exp(jnp.where(lower, diff, 0.0)), 0.0)
    scores = jnp.einsum('bclgn,bcsgn->bcgls', cc, bc)
    y_diag = jnp.einsum('bcgls,bcghls,bcsghp->bclghp', scores, decay, xdt)
    y_off = jnp.einsum('bclgn,bcghpn,bclgh->bclghp', cc, s_in, jnp.exp(cs))
    y = (y_diag + y_off).reshape(bsz, length, n_heads, hd) + d_skip[:, None] * x
    return y, s_final


def hgrn2_scan(q, k, v, g, init, need_o):
    bsz, length, n_heads, _ = k.shape
    nc = length // HG_CHUNK

    def chunks(a):
        return a.reshape(bsz, nc, HG_CHUNK, n_heads, a.shape[-1]).transpose(1, 0, 3, 2, 4)

    lower = jnp.tril(jnp.ones((HG_CHUNK, HG_CHUNK), bool))[:, :, None]

    def update(s, kc, vc, cum):
        last = cum[:, :, -1]
        return (s * jnp.exp(last)[..., None]
                + jnp.einsum('bhsk,bhsv->bhkv', kc * jnp.exp(last[:, :, None] - cum), vc))

    if not need_o:
        def step_state(s, inp):
            kc, vc, gc = inp
            return update(s, kc, vc, jnp.cumsum(gc, axis=2)), None
        s_final, _ = lax.scan(step_state, init, (chunks(k), chunks(v), chunks(g)))
        return None, s_final

    def step(s, inp):
        qc, kc, vc, gc = inp
        cum = jnp.cumsum(gc, axis=2)
        diff = cum[:, :, :, None, :] - cum[:, :, None, :, :]
        decay = jnp.where(lower, jnp.exp(jnp.where(lower, diff, 0.0)), 0.0)
        att = jnp.einsum('bhtk,bhsk,bhtsk->bhts', qc, kc, decay)
        o = (jnp.einsum('bhtk,bhkv->bhtv', qc * jnp.exp(cum), s)
             + jnp.einsum('bhts,bhsv->bhtv', att, vc))
        return update(s, kc, vc, cum), o

    s_final, o = lax.scan(step, init, (chunks(q), chunks(k), chunks(v), chunks(g)))
    return o.transpose(1, 0, 3, 2, 4).reshape(bsz, length, n_heads, v.shape[-1]), s_final


def even_mixer(h, hc, w_in, conv_w, conv_b, f_w1, f_b1, f_w2, f_b2, f_w3, f_b3, f_w4, f_freq, f_bias,
               q_norm, k_norm, sink, ctx_out):
    f32 = jnp.float32
    bsz, length, _ = h.shape
    n_ctx = hc.shape[1]
    filt = (f_w1, f_b1, f_w2, f_b2, f_w3, f_b3, f_w4, f_freq)
    o_v = ATT_KV_W
    o_q = 2 * ATT_KV_W
    o_hy = o_q + ATT_Q_W
    p = pmm3(h, w_in).astype(f32)
    pc = pmm3(hc, w_in if ctx_out else w_in[:, :o_q]).astype(f32)

    def heads(a, n_heads):
        return a.reshape(a.shape[0], a.shape[1], n_heads, HEAD_DIM)

    k_c = rms_norm(heads(pc[..., :o_v], ATT_KV_HEADS), k_norm)
    v_c = heads(pc[..., o_v:o_q], ATT_KV_HEADS)
    rope = axial_rope_tables(length)
    k = apply_axial_rope(rms_norm(heads(p[..., :o_v], ATT_KV_HEADS), k_norm), rope)
    v = heads(p[..., o_v:o_q], ATT_KV_HEADS)
    q = apply_axial_rope(rms_norm(heads(p[..., o_q:o_hy], ATT_HEADS), q_norm), rope)
    att = window_attention(q.reshape(bsz, length, ATT_KV_HEADS, ATT_GROUP, HEAD_DIM), k, v, k_c, v_c, sink)
    hy = hyena_mix(p[..., o_hy:], hyena_filters(length, *filt), f_bias, conv_w, conv_b)
    out = jnp.concatenate([hy, att], axis=-1).astype(h.dtype)
    if not ctx_out:
        return out, None
    q_c = rms_norm(heads(pc[..., o_q:o_hy], ATT_HEADS), q_norm).reshape(bsz, n_ctx, ATT_KV_HEADS, ATT_GROUP, HEAD_DIM)
    att_c = context_attention(q_c, k_c, v_c, sink)
    hy_c = hyena_mix(pc[..., o_hy:], hyena_filters(n_ctx, *filt), f_bias, conv_w, conv_b)
    return out, jnp.concatenate([hy_c, att_c], axis=-1).astype(hc.dtype)


def odd_mixer(h, hc, lb, w_in, conv_w, conv_b, dt_bias, a_log, d_skip, ssd_norm, hg_norm, ctx_out):
    f32 = jnp.float32
    bsz, length, _ = h.shape
    n_ctx = hc.shape[1]
    p = pmm3(h, w_in).astype(f32)
    pc = pmm3(hc, w_in if ctx_out else w_in[:, :ODD_STATE_COLS]).astype(f32)
    o_dt = SSD_CONV_CH
    o_f = SSD_CONV_CH + 2 * SSD_HEADS
    o_i = o_f + 2 * HG_W
    o_z = ODD_STATE_COLS
    o_q = o_z + SSD_W
    o_g = o_q + HG_W
    gn = SSD_GROUPS * SSD_STATE

    def streams(pp):
        n = pp.shape[1]
        xbc = jax.nn.silu(dwconv_centred(pp[..., :SSD_CONV_CH], conv_w, conv_b))
        xs = xbc[..., :SSD_W].reshape(bsz, n, SSD_HEADS, SSD_HEAD_DIM)
        bm = xbc[..., SSD_W:SSD_W + gn].reshape(bsz, n, SSD_GROUPS, SSD_STATE)
        cm = xbc[..., SSD_W + gn:].reshape(bsz, n, SSD_GROUPS, SSD_STATE)
        dt_raw = pp[..., o_dt:o_f].reshape(bsz, n, 2, SSD_HEADS)
        f_raw = pp[..., o_f:o_i].reshape(bsz, n, 2, HG_HEADS, HG_EXPAND)
        iv = pp[..., o_i:o_i + HG_W].reshape(bsz, n, HG_HEADS, HG_VDIM)
        return xs, bm, cm, dt_raw, f_raw, iv

    xs, bm, cm, dt_raw, f_raw, iv = streams(p)
    xs_c, bm_c, cm_c, dt_raw_c, f_raw_c, iv_c = streams(pc)
    q = jax.nn.silu(p[..., o_q:o_g]).reshape(bsz, length, HG_HEADS, HG_EXPAND)
    q_c = jax.nn.silu(pc[..., o_q:o_g]).reshape(bsz, n_ctx, HG_HEADS, HG_EXPAND) if ctx_out else None
    lb = lb.astype(f32).reshape(HG_HEADS, HG_EXPAND)
    ssd0 = jnp.zeros((bsz, SSD_GROUPS, SSD_HEADS // SSD_GROUPS, SSD_HEAD_DIM, SSD_STATE), f32)
    hg0 = jnp.zeros((bsz, HG_HEADS, HG_EXPAND, HG_VDIM), f32)
    y_dirs, o_dirs, yc_dirs, oc_dirs = [], [], [], []
    for d in range(2):
        fl = (lambda a: jnp.flip(a, axis=1)) if d == 1 else (lambda a: a)
        a = -jnp.exp(a_log[d].astype(f32))
        dsk = d_skip[d].astype(f32)
        dtb = dt_bias[d].astype(f32)
        dt_l = jax.nn.softplus(dt_raw[:, :, d] + dtb)
        dt_c = jax.nn.softplus(dt_raw_c[:, :, d] + dtb)
        yc, s_ctx = ssd_scan(fl(xs_c), fl(dt_c), a, fl(bm_c), fl(cm_c), dsk, ssd0, ctx_out)
        yl, _ = ssd_scan(fl(xs), fl(dt_l), a, fl(bm), fl(cm), dsk, s_ctx, True)
        y_dirs.append(fl(yl))
        f_l = lb + (1.0 - lb) * jax.nn.sigmoid(f_raw[:, :, d])
        f_c = lb + (1.0 - lb) * jax.nn.sigmoid(f_raw_c[:, :, d])
        oc, s_hg = hgrn2_scan(fl(q_c) if ctx_out else None, fl(1.0 - f_c), fl(iv_c), fl(jnp.log(f_c)), hg0, ctx_out)
        ol, _ = hgrn2_scan(fl(q), fl(1.0 - f_l), fl(iv), fl(jnp.log(f_l)), s_hg, True)
        o_dirs.append(fl(ol))
        if ctx_out:
            yc_dirs.append(fl(yc))
            oc_dirs.append(fl(oc))

    def merge(yy, oo, pp, n):
        z = pp[..., o_z:o_q]
        g = pp[..., o_g:]
        ys = (yy.reshape(bsz, n, SSD_W) * jax.nn.silu(z)).reshape(bsz, n, SSD_GROUPS, SSD_W // SSD_GROUPS)
        ys = rms_norm(ys, ssd_norm.reshape(SSD_GROUPS, SSD_W // SSD_GROUPS)).reshape(bsz, n, SSD_W)
        hs = rms_norm(oo, hg_norm.reshape(HG_HEADS, HG_VDIM)).reshape(bsz, n, HG_W) * jax.nn.silu(g)
        return jnp.concatenate([ys, hs], axis=-1)

    out = merge(y_dirs[0] + y_dirs[1], o_dirs[0] + o_dirs[1], p, length).astype(h.dtype)
    if not ctx_out:
        return out, None
    out_c = merge(yc_dirs[0] + yc_dirs[1], oc_dirs[0] + oc_dirs[1], pc, n_ctx).astype(hc.dtype)
    return out, out_c


MOE_TM = 256
MOE_BM = 256
NEG_BIG = -1e30
MOE_VMEM_LIMIT = 52 * 1024 * 1024


def _route_body(t_ref, rw_ref, rb_ref, idx_ref, gate_ref, rank_ref, cnt_ref, run_ref):
    i = pl.program_id(0)

    @pl.when(i == 0)
    def _():
        run_ref[...] = jnp.zeros_like(run_ref)

    tm = t_ref.shape[0]
    logits = jnp.dot(t_ref[...].astype(jnp.bfloat16), rw_ref[...].astype(jnp.bfloat16),
                     preferred_element_type=jnp.float32) + rb_ref[...]
    lane = lax.broadcasted_iota(jnp.int32, (tm, LANE), 1)
    lane_f = lane.astype(jnp.float32)
    work = logits
    vals, sels, hots = [], [], []
    for _ in range(TOP_K):
        m = jnp.max(work, axis=-1, keepdims=True)
        sel = jnp.min(jnp.where(work == m, lane_f, float(LANE)), axis=-1, keepdims=True)
        hot = lane_f == sel
        vals.append(m)
        sels.append(sel.astype(jnp.int32))
        hots.append(hot)
        work = jnp.where(hot, -jnp.inf, work)
    exps = [jnp.exp(v - vals[0]) for v in vals]
    denom = exps[0] + exps[1] + exps[2] + exps[3]
    chosen = jnp.zeros((tm, LANE), jnp.float32)
    for hot in hots:
        chosen = chosen + hot.astype(jnp.float32)
    row = lax.broadcasted_iota(jnp.int32, (tm, tm), 0)
    col = lax.broadcasted_iota(jnp.int32, (tm, tm), 1)
    tri = (row > col).astype(jnp.bfloat16)
    before = jnp.dot(tri, chosen.astype(jnp.bfloat16), preferred_element_type=jnp.float32) + run_ref[0:1, :]
    idx_out = jnp.zeros((tm, LANE), jnp.int32)
    gate_out = jnp.zeros((tm, LANE), jnp.float32)
    rank_out = jnp.zeros((tm, LANE), jnp.int32)
    for k in range(TOP_K):
        rank_k = jnp.sum(jnp.where(hots[k], before, 0.0), axis=-1, keepdims=True).astype(jnp.int32)
        idx_out = jnp.where(lane == k, sels[k], idx_out)
        gate_out = jnp.where(lane == k, exps[k] / denom, gate_out)
        rank_out = jnp.where(lane == k, rank_k, rank_out)
    idx_ref[...] = idx_out
    gate_ref[...] = gate_out
    rank_ref[...] = rank_out
    run_new = run_ref[0:1, :] + jnp.sum(chosen, axis=0, keepdims=True)
    run_ref[...] = jnp.broadcast_to(run_new, run_ref.shape)
    cnt_ref[...] = jnp.broadcast_to(run_new, cnt_ref.shape)


def _moe_route(t, router_w, router_b):
    n_tok, d = t.shape
    rw = jnp.pad(router_w, ((0, 0), (0, LANE - N_EXPERTS)))
    rb = jnp.pad(router_b.astype(jnp.float32), (0, LANE - N_EXPERTS), constant_values=NEG_BIG).reshape(1, LANE)
    tile = pl.BlockSpec((MOE_TM, LANE), lambda i: (i, 0))
    idx, gate, rank, cnt = pl.pallas_call(
        _route_body,
        grid=(n_tok // MOE_TM,),
        in_specs=[pl.BlockSpec((MOE_TM, d), lambda i: (i, 0)),
                  pl.BlockSpec((d, LANE), lambda i: (0, 0)),
                  pl.BlockSpec((1, LANE), lambda i: (0, 0))],
        out_specs=[tile, tile, tile, pl.BlockSpec((SUBLANE, LANE), lambda i: (0, 0))],
        out_shape=[jax.ShapeDtypeStruct((n_tok, LANE), jnp.int32),
                   jax.ShapeDtypeStruct((n_tok, LANE), jnp.float32),
                   jax.ShapeDtypeStruct((n_tok, LANE), jnp.int32),
                   jax.ShapeDtypeStruct((SUBLANE, LANE), jnp.float32)],
        scratch_shapes=[pltpu.VMEM((SUBLANE, LANE), jnp.float32)],
        compiler_params=pltpu.CompilerParams(dimension_semantics=("arbitrary",)),
        name="moe_route",
    )(t, rw, rb)
    return idx[:, :TOP_K], gate, rank[:, :TOP_K], cnt[0, :N_EXPERTS].astype(jnp.int32)


def _row_copy(src_ref, src_row, dst_ref, dst_row, sem):
    return pltpu.make_async_copy(src_ref.at[pl.ds(src_row, 1)], dst_ref.at[pl.ds(dst_row, 1)], sem)


def _scatter_body(off_ref, pad_ref, dest_ref, t_ref, xs_ref, zero_ref, sem):
    i = pl.program_id(0)
    tm = t_ref.shape[0]

    @pl.when(i == 0)
    def _():
        zero_ref[...] = jnp.zeros_like(zero_ref)
        used = off_ref[N_EXPERTS - 1] + pad_ref[N_EXPERTS - 1]
        n_rows = xs_ref.shape[0]

        def zero_block(start):
            return pltpu.make_async_copy(zero_ref, xs_ref.at[pl.ds(pl.multiple_of(start, MOE_BM), MOE_BM)], sem)

        for e in range(N_EXPERTS):
            tail = n_rows - (e + 1) * MOE_BM

            @pl.when(pad_ref[e] > 0)
            def _():
                zero_block(off_ref[e] + pad_ref[e] - MOE_BM).start()

            @pl.when(tail >= used)
            def _():
                zero_block(tail).start()
        for e in range(N_EXPERTS):
            tail = n_rows - (e + 1) * MOE_BM

            @pl.when(pad_ref[e] > 0)
            def _():
                zero_block(0).wait()

            @pl.when(tail >= used)
            def _():
                zero_block(0).wait()

    def issue(t, carry):
        for k in range(TOP_K):
            _row_copy(t_ref, t, xs_ref, dest_ref[TOP_K * t + k], sem).start()
        return carry

    lax.fori_loop(0, tm, issue, 0)
    for _ in range(TOP_K):
        pltpu.make_async_copy(t_ref, xs_ref.at[pl.ds(0, tm)], sem).wait()


def _moe_scatter(t, dest_flat, off, padded, n_rows):
    n_tok, d = t.shape
    return pl.pallas_call(
        _scatter_body,
        grid_spec=pltpu.PrefetchScalarGridSpec(
            num_scalar_prefetch=2,
            grid=(n_tok // MOE_TM,),
            in_specs=[pl.BlockSpec((TOP_K * MOE_TM,), lambda i, off, pad: (i,), memory_space=pltpu.SMEM),
                      pl.BlockSpec((MOE_TM, d), lambda i, off, pad: (i, 0))],
            out_specs=pl.BlockSpec(memory_space=pl.ANY),
            scratch_shapes=[pltpu.VMEM((MOE_BM, d), jnp.float32), pltpu.SemaphoreType.DMA],
        ),
        out_shape=jax.ShapeDtypeStruct((n_rows, d), jnp.float32),
        compiler_params=pltpu.CompilerParams(dimension_semantics=("arbitrary",)),
        name="moe_scatter",
    )(off, padded, dest_flat, t)


def _expert_body(blk_e_ref, n_act_ref, x_ref, wgu_ref, bgu_ref, wdn_ref, bdn_ref, y_ref, wgu_bf, wdn_bf):
    i = pl.program_id(0)

    @pl.when(i < n_act_ref[0])
    def _():
        prev = blk_e_ref[jnp.maximum(i - 1, 0)]

        @pl.when((i == 0) | (blk_e_ref[i] != prev))
        def _():
            wgu_bf[...] = wgu_ref[...].astype(jnp.bfloat16)
            wdn_bf[...] = wdn_ref[...].astype(jnp.bfloat16)

        x = x_ref[...].astype(jnp.bfloat16)
        gu = jnp.dot(x, wgu_bf[...], preferred_element_type=jnp.float32) + bgu_ref[...]
        gate = jnp.minimum(gu[:, :D_EXPERT], SWIGLU_LIMIT)
        up = jnp.clip(gu[:, D_EXPERT:], -SWIGLU_LIMIT, SWIGLU_LIMIT)
        act = (up + 1.0) * gate * jax.nn.sigmoid(SWIGLU_ALPHA * gate)
        y_ref[...] = jnp.dot(act.astype(jnp.bfloat16), wdn_bf[...],
                             preferred_element_type=jnp.float32) + bdn_ref[...]

    @pl.when(i >= n_act_ref[0])
    def _():
        y_ref[...] = jnp.zeros_like(y_ref)


def _moe_experts(xs, blk_e, n_act, w_gu, b_gu, w_dn, b_dn):
    n_rows, d = xs.shape
    n_blk = n_rows // MOE_BM

    def blk(i, be, na):
        return jnp.minimum(i, na[0] - 1)

    return pl.pallas_call(
        _expert_body,
        grid_spec=pltpu.PrefetchScalarGridSpec(
            num_scalar_prefetch=2,
            grid=(n_blk,),
            in_specs=[pl.BlockSpec((MOE_BM, d), lambda i, be, na: (blk(i, be, na), 0)),
                      pl.BlockSpec((None, d, 2 * D_EXPERT), lambda i, be, na: (be[blk(i, be, na)], 0, 0)),
                      pl.BlockSpec((None, 1, 2 * D_EXPERT), lambda i, be, na: (be[blk(i, be, na)], 0, 0)),
                      pl.BlockSpec((None, D_EXPERT, d), lambda i, be, na: (be[blk(i, be, na)], 0, 0)),
                      pl.BlockSpec((None, 1, d), lambda i, be, na: (be[blk(i, be, na)], 0, 0))],
            out_specs=pl.BlockSpec((MOE_BM, d), lambda i, be, na: (i, 0)),
            scratch_shapes=[pltpu.VMEM((d, 2 * D_EXPERT), jnp.bfloat16),
                            pltpu.VMEM((D_EXPERT, d), jnp.bfloat16)],
        ),
        out_shape=jax.ShapeDtypeStruct((n_rows, d), jnp.float32),
        compiler_params=pltpu.CompilerParams(dimension_semantics=("arbitrary",),
                                             vmem_limit_bytes=MOE_VMEM_LIMIT),
        name="moe_experts",
    )(blk_e, n_act, xs, w_gu, b_gu.reshape(N_EXPERTS, 1, -1), w_dn, b_dn.reshape(N_EXPERTS, 1, -1))


def _combine_body(dest_ref, dest_nxt_ref, gate_ref, ys_ref, y_ref, buf_ref, sems):
    i = pl.program_id(0)
    n = pl.num_programs(0)
    tm = y_ref.shape[0]

    def fetch(d_ref, slot):
        def issue(t, carry):
            for k in range(TOP_K):
                _row_copy(ys_ref, d_ref[TOP_K * t + k], buf_ref.at[slot, k], t, sems.at[slot]).start()
            return carry
        lax.fori_loop(0, tm, issue, 0)

    @pl.when(i == 0)
    def _():
        fetch(dest_ref, 0)

    @pl.when(i + 1 < n)
    def _():
        fetch(dest_nxt_ref, (i + 1) % 2)

    slot = i % 2
    for k in range(TOP_K):
        pltpu.make_async_copy(ys_ref.at[pl.ds(0, tm)], buf_ref.at[slot, k], sems.at[slot]).wait()
    g = gate_ref[...]
    acc = g[:, 0:1] * buf_ref[slot, 0]
    for k in range(1, TOP_K):
        acc = acc + g[:, k:k + 1] * buf_ref[slot, k]
    y_ref[...] = acc


def _moe_combine(ys, dest_flat, gate, n_tok):
    d = ys.shape[1]
    n_tiles = n_tok // MOE_TM
    return pl.pallas_call(
        _combine_body,
        grid=(n_tiles,),
        in_specs=[pl.BlockSpec((TOP_K * MOE_TM,), lambda i: (i,), memory_space=pltpu.SMEM),
                  pl.BlockSpec((TOP_K * MOE_TM,), lambda i: (jnp.minimum(i + 1, n_tiles - 1),),
                               memory_space=pltpu.SMEM),
                  pl.BlockSpec((MOE_TM, LANE), lambda i: (i, 0)),
                  pl.BlockSpec(memory_space=pl.ANY)],
        out_specs=pl.BlockSpec((MOE_TM, d), lambda i: (i, 0)),
        out_shape=jax.ShapeDtypeStruct((n_tok, d), jnp.float32),
        scratch_shapes=[pltpu.VMEM((2, TOP_K, MOE_TM, d), jnp.float32), pltpu.SemaphoreType.DMA((2,))],
        compiler_params=pltpu.CompilerParams(dimension_semantics=("arbitrary",)),
        name="moe_combine",
    )(dest_flat, dest_flat, gate, ys)


def moe_ffn(t, router_w, router_b, w_gu, b_gu, w_dn, b_dn):
    n_tok, d = t.shape
    assert n_tok % MOE_TM == 0
    idx, gate, rank, counts = _moe_route(t, router_w, router_b)
    padded = (counts + MOE_BM - 1) // MOE_BM * MOE_BM
    pad_end = jnp.cumsum(padded)
    off = pad_end - padded
    experts = jnp.arange(N_EXPERTS, dtype=jnp.int32)
    dest = rank + jnp.sum(jnp.where(idx[..., None] == experts, off, 0), axis=-1)
    dest_flat = dest.reshape(-1).astype(jnp.int32)
    n_blk = -(-(n_tok * TOP_K) // MOE_BM) + N_EXPERTS
    blk_e = jnp.minimum(jnp.searchsorted(pad_end, jnp.arange(n_blk) * MOE_BM, side='right'),
                        N_EXPERTS - 1).astype(jnp.int32)
    n_act = (pad_end[-1:] // MOE_BM).astype(jnp.int32)
    xs = _moe_scatter(t, dest_flat, off.astype(jnp.int32), padded.astype(jnp.int32), n_blk * MOE_BM)
    ys = _moe_experts(xs, blk_e, n_act, w_gu, b_gu, w_dn, b_dn)
    return _moe_combine(ys, dest_flat, gate, n_tok)


def kernel(x, c, ctx, c_ctx, norm_g, ada_w, ada_b, w_out, w_in_even, hy_conv_w, hy_conv_b,
           hy_w1, hy_b1, hy_w2, hy_b2, hy_w3, hy_b3, hy_w4, hy_freq, hy_filter_bias,
           att_q_norm, att_k_norm, att_sink, w_in_odd, ssd_conv_w, ssd_conv_b, ssd_dt_bias,
           ssd_A_log, ssd_D, ssd_norm, hg_lower_bounds, hg_norm, router_w, router_b,
           moe_w_gu, moe_b_gu, moe_w_dn, moe_b_dn):
    lbs = jax.nn.softmax(hg_lower_bounds.astype(jnp.float32), axis=0)
    lbs = jnp.cumsum(lbs, axis=0) - lbs[0]
    xc = ctx
    for layer in range(DEPTH):
        ctx_out = layer < DEPTH - 1
        i = layer // 2
        sh, sc, gt = adaln(c, ada_w[layer], ada_b[layer], 0)
        sh_c, sc_c, gt_c = adaln(c_ctx, ada_w[layer], ada_b[layer], 0)
        h = modulate(rms_norm(x, norm_g[layer, 0]), sh[:, None], sc[:, None])
        hc = modulate(rms_norm(xc, norm_g[layer, 0]), sh_c, sc_c)
        if layer % 2 == 0:
            m, m_c = even_mixer(h, hc, w_in_even[i], hy_conv_w[i], hy_conv_b[i], hy_w1[i], hy_b1[i],
                                hy_w2[i], hy_b2[i], hy_w3[i], hy_b3[i], hy_w4[i], hy_freq[i],
                                hy_filter_bias[i], att_q_norm[i], att_k_norm[i], att_sink[i], ctx_out)
        else:
            m, m_c = odd_mixer(h, hc, lbs[layer], w_in_odd[i], ssd_conv_w[i], ssd_conv_b[i], ssd_dt_bias[i],
                               ssd_A_log[i], ssd_D[i], ssd_norm[i], hg_norm[i], ctx_out)
        x = x + gt[:, None] * pmm3(m, w_out[layer])
        if ctx_out:
            xc = xc + gt_c * pmm3(m_c, w_out[layer])
        sh, sc, gt = adaln(c, ada_w[layer], ada_b[layer], 1)
        h = modulate(rms_norm(x, norm_g[layer, 1]), sh[:, None], sc[:, None])
        moe_params = (router_w[layer], router_b[layer], moe_w_gu[layer], moe_b_gu[layer],
                      moe_w_dn[layer], moe_b_dn[layer])
        n_lat = x.shape[0] * x.shape[1]
        if ctx_out:
            sh_c, sc_c, gt_c = adaln(c_ctx, ada_w[layer], ada_b[layer], 1)
            hc = modulate(rms_norm(xc, norm_g[layer, 1]), sh_c, sc_c)
            y = moe_ffn(jnp.concatenate([h.reshape(n_lat, D_MODEL), hc.reshape(-1, D_MODEL)], axis=0), *moe_params)
            x = x + gt[:, None] * y[:n_lat].reshape(x.shape)
            xc = xc + gt_c * y[n_lat:].reshape(xc.shape)
        else:
            x = x + gt[:, None] * moe_ffn(h.reshape(n_lat, D_MODEL), *moe_params).reshape(x.shape)
    return x
```

```python
import functools
import math

import jax
import jax.numpy as jnp
from jax import lax
from jax.experimental import pallas as pl
from jax.experimental.pallas import tpu as pltpu

D_MODEL = 1024
DEPTH = 2
GRID_W = 64
MIX_W = D_MODEL
EPS = 1e-6
CONV_W = 3

HY_W = MIX_W // 2
HY_ORDER = 2
HY_EMB = 33
HY_FFN = 64
HY_TARGET = 1e-2
HY_SHORT_PCT = 0.3
HY_LONG_PCT = 1.5

HEAD_DIM = 64
ATT_HEADS = (MIX_W // 2) // HEAD_DIM
ATT_KV_HEADS = 2
ATT_GROUP = ATT_HEADS // ATT_KV_HEADS
ATT_WINDOW = 128
ATT_BLOCK = 128
ROPE_BASE = 10000.0
ATT_Q_W = ATT_HEADS * HEAD_DIM
ATT_KV_W = ATT_KV_HEADS * HEAD_DIM

SSD_W = MIX_W // 2
SSD_HEAD_DIM = 64
SSD_HEADS = SSD_W // SSD_HEAD_DIM
SSD_GROUPS = 2
SSD_STATE = 128
SSD_CHUNK = 128
SSD_CONV_CH = SSD_W + 2 * SSD_GROUPS * SSD_STATE

HG_W = MIX_W // 2
HG_EXPAND = 128
HG_HEADS = HG_W // HG_EXPAND
HG_VDIM = HG_W // HG_HEADS
HG_CHUNK = 64

N_EXPERTS = 32
TOP_K = 4
D_EXPERT = D_MODEL
SWIGLU_ALPHA = 1.702
SWIGLU_LIMIT = 7.0
MOE_BLOCK = 128

EVEN_IN = 2 * ATT_KV_W + ATT_Q_W + 3 * HY_W
ODD_STATE_COLS = SSD_CONV_CH + 2 * SSD_HEADS + 3 * HG_W
ODD_IN = ODD_STATE_COLS + SSD_W + 2 * HG_W

LANE = 128
SUBLANE = 8


def _mm_body(a_ref, b_ref, o_ref):
    a = a_ref[...].astype(jnp.bfloat16)
    b = b_ref[...].astype(jnp.bfloat16)
    o_ref[...] = jnp.dot(a, b, preferred_element_type=jnp.float32)


def _pick_tile(n, candidates):
    for c in candidates:
        if n % c == 0:
            return c
    return n


def pmm(a, b):
    m, k = a.shape
    n = b.shape[1]
    n_pad = -(-n // LANE) * LANE
    if n_pad != n:
        b = jnp.pad(b, ((0, 0), (0, n_pad - n)))
    m_pad = -(-m // SUBLANE) * SUBLANE
    if m_pad != m:
        a = jnp.pad(a, ((0, m_pad - m), (0, 0)))
    tm = _pick_tile(m_pad, (512, 256, 128, 64, 32, 16, 8))
    tn = _pick_tile(n_pad, (512, 384, 256, 128))
    out = pl.pallas_call(
        _mm_body,
        grid=(m_pad // tm, n_pad // tn),
        in_specs=[pl.BlockSpec((tm, k), lambda i, j: (i, 0)),
                  pl.BlockSpec((k, tn), lambda i, j: (0, j))],
        out_specs=pl.BlockSpec((tm, tn), lambda i, j: (i, j)),
        out_shape=jax.ShapeDtypeStruct((m_pad, n_pad), jnp.float32),
        name="dense_mm",
    )(a, b)
    return out[:m, :n]


def pmm3(a, b):
    lead = a.shape[:-1]
    return pmm(a.reshape(-1, a.shape[-1]), b).reshape(*lead, b.shape[-1])


def rms_norm(x, g):
    xf = x.astype(jnp.float32)
    y = xf * lax.rsqrt(jnp.mean(xf * xf, axis=-1, keepdims=True) + EPS)
    return (y * g.astype(jnp.float32)).astype(x.dtype)


def modulate(h, shift, scale):
    return h * (1.0 + scale) + shift


def adaln(cond, w, b, j):
    lo, hi = 3 * j * D_MODEL, 3 * (j + 1) * D_MODEL
    m = jax.nn.silu(cond) @ w[:, lo:hi] + b[lo:hi]
    return jnp.split(m, 3, axis=-1)


def dwconv_centred(u, w, b):
    ch = u.shape[-1]
    y = lax.conv_general_dilated(u, w[:, None, :].astype(u.dtype), window_strides=(1,),
                                 padding=[(CONV_W // 2, CONV_W // 2)],
                                 dimension_numbers=('NWC', 'WIO', 'NWC'), feature_group_count=ch)
    return y + b.astype(u.dtype)


def axial_rope_tables(length):
    rows = length // GRID_W
    n_pairs = HEAD_DIM // 4
    inv = ROPE_BASE ** (-jnp.arange(n_pairs, dtype=jnp.float32) / n_pairs)
    row_ang = jnp.arange(rows, dtype=jnp.float32)[:, None] * inv
    col_ang = jnp.arange(GRID_W, dtype=jnp.float32)[:, None] * inv
    ang_r = jnp.broadcast_to(row_ang[:, None], (rows, GRID_W, n_pairs)).reshape(length, n_pairs)
    ang_c = jnp.broadcast_to(col_ang[None], (rows, GRID_W, n_pairs)).reshape(length, n_pairs)
    return jnp.cos(ang_r), jnp.sin(ang_r), jnp.cos(ang_c), jnp.sin(ang_c)


def _rotate(u, cos, sin):
    n = u.shape[-1] // 2
    u1, u2 = u[..., :n], u[..., n:]
    cos = cos[None, :, None, :]
    sin = sin[None, :, None, :]
    return jnp.concatenate([u1 * cos - u2 * sin, u1 * sin + u2 * cos], axis=-1)


def apply_axial_rope(u, tables):
    cr, sr, cc, sc = tables
    half = HEAD_DIM // 2
    return jnp.concatenate([_rotate(u[..., :half], cr, sr), _rotate(u[..., half:], cc, sc)], axis=-1)


def hyena_filters(length, w1, b1, w2, b2, w3, b3, w4, freq):
    f32 = jnp.float32
    t = jnp.linspace(0.0, 1.0, length, dtype=f32)[:, None]
    bands = (HY_EMB - 1) // 2
    w_ang = 2.0 * math.pi * jnp.arange(length, dtype=f32)[:, None] / length
    fr = jnp.linspace(1e-4, bands - 1, bands, dtype=f32)[None]
    z = jnp.concatenate([t, jnp.cos(fr * w_ang), -jnp.sin(fr * w_ang)], axis=-1)
    fq = freq.astype(f32)
    hdn = jnp.sin(fq * (z @ w1.astype(f32) + b1.astype(f32)))
    hdn = jnp.sin(fq * (hdn @ w2.astype(f32) + b2.astype(f32)))
    hdn = jnp.sin(fq * (hdn @ w3.astype(f32) + b3.astype(f32)))
    h = (hdn @ w4.astype(f32)).reshape(length, HY_ORDER, 2, HY_W)
    max_decay = math.log(HY_TARGET) / HY_SHORT_PCT
    min_decay = math.log(HY_TARGET) / HY_LONG_PCT
    deltas = jnp.abs(jnp.linspace(min_decay, max_decay, HY_W, dtype=f32))
    h = h * jnp.exp(-t * deltas)[:, None, None, :]
    h2 = jnp.concatenate([h[:, :, 0], jnp.zeros((1, HY_ORDER, HY_W), f32), h[:0:-1, :, 1]], axis=0)
    h2 = h2 / jnp.sum(jnp.abs(h2), axis=0, keepdims=True)
    return jnp.fft.rfft(h2, axis=0)


def hyena_mix(u, hf, filter_bias, conv_w, conv_b):
    length = u.shape[1]
    u = dwconv_centred(u.astype(jnp.float32), conv_w, conv_b)
    v, x1, x2 = jnp.split(u, 3, axis=-1)
    z = v
    for o, gate in enumerate((x1, x2)):
        zf = jnp.fft.rfft(z, n=2 * length, axis=1)
        zc = jnp.fft.irfft(zf * hf[None, :, o], n=2 * length, axis=1)[:, :length]
        z = gate * (zc + z * filter_bias[o].astype(jnp.float32))
    return z


def window_attention(q, k, v, k_c, v_c, sink):
    bsz, length = q.shape[:2]
    nb = length // ATT_BLOCK
    scale = HEAD_DIM ** -0.5
    qb = q.reshape(bsz, nb, ATT_BLOCK, ATT_KV_HEADS, ATT_GROUP, HEAD_DIM)
    pad = ((0, 0), (ATT_BLOCK, ATT_BLOCK), (0, 0), (0, 0))

    def band(a):
        ap = jnp.pad(a, pad).reshape(bsz, nb + 2, ATT_BLOCK, ATT_KV_HEADS, HEAD_DIM)
        return jnp.concatenate([ap[:, :-2], ap[:, 1:-1], ap[:, 2:]], axis=2)

    kw, vw = band(k), band(v)
    s_loc = jnp.einsum('bnqhgd,bnkhd->bnhgqk', qb, kw) * scale
    s_ctx = jnp.einsum('bnqhgd,bchd->bnhgqc', qb, k_c) * scale
    qpos = jnp.arange(nb)[:, None] * ATT_BLOCK + jnp.arange(ATT_BLOCK)[None]
    kpos = (jnp.arange(nb)[:, None] - 1) * ATT_BLOCK + jnp.arange(3 * ATT_BLOCK)[None]
    rel = kpos[:, None, :] - qpos[:, :, None]
    valid = (jnp.abs(rel) <= ATT_WINDOW) & (kpos[:, None, :] >= 0) & (kpos[:, None, :] < length)
    s_loc = jnp.where(valid[None, :, None, None], s_loc, -jnp.inf)
    sink_l = jnp.broadcast_to(sink.astype(jnp.float32).reshape(1, 1, ATT_KV_HEADS, ATT_GROUP, 1, 1),
                              s_loc.shape[:-1] + (1,))
    p = jax.nn.softmax(jnp.concatenate([s_loc, s_ctx, sink_l], axis=-1), axis=-1)
    n_loc = 3 * ATT_BLOCK
    n_ctx = k_c.shape[1]
    o = (jnp.einsum('bnhgqk,bnkhd->bnqhgd', p[..., :n_loc], vw)
         + jnp.einsum('bnhgqc,bchd->bnqhgd', p[..., n_loc:n_loc + n_ctx], v_c))
    return o.reshape(bsz, length, ATT_Q_W)


def context_attention(q_c, k_c, v_c, sink):
    bsz, n_ctx = q_c.shape[:2]
    s = jnp.einsum('bqhgd,bkhd->bhgqk', q_c, k_c) * HEAD_DIM ** -0.5
    sink_l = jnp.broadcast_to(sink.astype(jnp.float32).reshape(1, ATT_KV_HEADS, ATT_GROUP, 1, 1),
                              s.shape[:-1] + (1,))
    p = jax.nn.softmax(jnp.concatenate([s, sink_l], axis=-1), axis=-1)[..., :-1]
    return jnp.einsum('bhgqk,bkhd->bqhgd', p, v_c).reshape(bsz, n_ctx, ATT_Q_W)


def ssd_scan(x, dt, a, bm, cm, d_skip, init, need_y):
    bsz, length, n_heads, hd = x.shape
    nc = length // SSD_CHUNK
    hpg = n_heads // SSD_GROUPS
    da = (dt * a).reshape(bsz, nc, SSD_CHUNK, SSD_GROUPS, hpg)
    cs = jnp.cumsum(da, axis=2)
    xdt = (x * dt[..., None]).reshape(bsz, nc, SSD_CHUNK, SSD_GROUPS, hpg, hd)
    bc = bm.reshape(bsz, nc, SSD_CHUNK, SSD_GROUPS, SSD_STATE)
    cc = cm.reshape(bsz, nc, SSD_CHUNK, SSD_GROUPS, SSD_STATE)
    to_end = jnp.exp(cs[:, :, -1:] - cs)
    states = jnp.einsum('bcsgn,bcsgh,bcsghp->bcghpn', bc, to_end, xdt)
    chunk_decay = jnp.exp(cs[:, :, -1])

    def step(s, inp):
        st, dec = inp
        return s * dec[..., None, None] + st, s

    s_final, s_in = lax.scan(step, init, (jnp.moveaxis(states, 1, 0), jnp.moveaxis(chunk_decay, 1, 0)))
    if not need_y:
        return None, s_final
    s_in = jnp.moveaxis(s_in, 0, 1)
    cs_t = jnp.moveaxis(cs, 2, -1)
    diff = cs_t[..., :, None] - cs_t[..., None, :]
    lower = jnp.tril(jnp.ones((SSD_CHUNK, SSD_CHUNK), bool))
    decay = jnp.where(lower, jnp.exp(jnp.where(lower, diff, 0.0)), 0.0)
    scores = jnp.einsum('bclgn,bcsgn->bcgls', cc, bc)
    y_diag = jnp.einsum('bcgls,bcghls,bcsghp->bclghp', scores, decay, xdt)
    y_off = jnp.einsum('bclgn,bcghpn,bclgh->bclghp', cc, s_in, jnp.exp(cs))
    y = (y_diag + y_off).reshape(bsz, length, n_heads, hd) + d_skip[:, None] * x
    return y, s_final


def hgrn2_scan(q, k, v, g, init, need_o):
    bsz, length, n_heads, _ = k.shape
    nc = length // HG_CHUNK

    def chunks(a):
        return a.reshape(bsz, nc, HG_CHUNK, n_heads, a.shape[-1]).transpose(1, 0, 3, 2, 4)

    lower = jnp.tril(jnp.ones((HG_CHUNK, HG_CHUNK), bool))[:, :, None]

    def update(s, kc, vc, cum):
        last = cum[:, :, -1]
        return (s * jnp.exp(last)[..., None]
                + jnp.einsum('bhsk,bhsv->bhkv', kc * jnp.exp(last[:, :, None] - cum), vc))

    if not need_o:
        def step_state(s, inp):
            kc, vc, gc = inp
            return update(s, kc, vc, jnp.cumsum(gc, axis=2)), None
        s_final, _ = lax.scan(step_state, init, (chunks(k), chunks(v), chunks(g)))
        return None, s_final

    def step(s, inp):
        qc, kc, vc, gc = inp
        cum = jnp.cumsum(gc, axis=2)
        diff = cum[:, :, :, None, :] - cum[:, :, None, :, :]
        decay = jnp.where(lower, jnp.exp(jnp.where(lower, diff, 0.0)), 0.0)
        att = jnp.einsum('bhtk,bhsk,bhtsk->bhts', qc, kc, decay)
        o = (jnp.einsum('bhtk,bhkv->bhtv', qc * jnp.exp(cum), s)
             + jnp.einsum('bhts,bhsv->bhtv', att, vc))
        return update(s, kc, vc, cum), o

    s_final, o = lax.scan(step, init, (chunks(q), chunks(k), chunks(v), chunks(g)))
    return o.transpose(1, 0, 3, 2, 4).reshape(bsz, length, n_heads, v.shape[-1]), s_final


def even_mixer(h, hc, w_in, conv_w, conv_b, f_w1, f_b1, f_w2, f_b2, f_w3, f_b3, f_w4, f_freq, f_bias,
               q_norm, k_norm, sink, ctx_out):
    f32 = jnp.float32
    bsz, length, _ = h.shape
    n_ctx = hc.shape[1]
    filt = (f_w1, f_b1, f_w2, f_b2, f_w3, f_b3, f_w4, f_freq)
    o_v = ATT_KV_W
    o_q = 2 * ATT_KV_W
    o_hy = o_q + ATT_Q_W
    p = pmm3(h, w_in).astype(f32)
    pc = pmm3(hc, w_in if ctx_out else w_in[:, :o_q]).astype(f32)

    def heads(a, n_heads):
        return a.reshape(a.shape[0], a.shape[1], n_heads, HEAD_DIM)

    k_c = rms_norm(heads(pc[..., :o_v], ATT_KV_HEADS), k_norm)
    v_c = heads(pc[..., o_v:o_q], ATT_KV_HEADS)
    rope = axial_rope_tables(length)
    k = apply_axial_rope(rms_norm(heads(p[..., :o_v], ATT_KV_HEADS), k_norm), rope)
    v = heads(p[..., o_v:o_q], ATT_KV_HEADS)
    q = apply_axial_rope(rms_norm(heads(p[..., o_q:o_hy], ATT_HEADS), q_norm), rope)
    att = window_attention(q.reshape(bsz, length, ATT_KV_HEADS, ATT_GROUP, HEAD_DIM), k, v, k_c, v_c, sink)
    hy = hyena_mix(p[..., o_hy:], hyena_filters(length, *filt), f_bias, conv_w, conv_b)
    out = jnp.concatenate([hy, att], axis=-1).astype(h.dtype)
    if not ctx_out:
        return out, None
    q_c = rms_norm(heads(pc[..., o_q:o_hy], ATT_HEADS), q_norm).reshape(bsz, n_ctx, ATT_KV_HEADS, ATT_GROUP, HEAD_DIM)
    att_c = context_attention(q_c, k_c, v_c, sink)
    hy_c = hyena_mix(pc[..., o_hy:], hyena_filters(n_ctx, *filt), f_bias, conv_w, conv_b)
    return out, jnp.concatenate([hy_c, att_c], axis=-1).astype(hc.dtype)


def odd_mixer(h, hc, lb, w_in, conv_w, conv_b, dt_bias, a_log, d_skip, ssd_norm, hg_norm, ctx_out):
    f32 = jnp.float32
    bsz, length, _ = h.shape
    n_ctx = hc.shape[1]
    p = pmm3(h, w_in).astype(f32)
    pc = pmm3(hc, w_in if ctx_out else w_in[:, :ODD_STATE_COLS]).astype(f32)
    o_dt = SSD_CONV_CH
    o_f = SSD_CONV_CH + 2 * SSD_HEADS
    o_i = o_f + 2 * HG_W
    o_z = ODD_STATE_COLS
    o_q = o_z + SSD_W
    o_g = o_q + HG_W
    gn = SSD_GROUPS * SSD_STATE

    def streams(pp):
        n = pp.shape[1]
        xbc = jax.nn.silu(dwconv_centred(pp[..., :SSD_CONV_CH], conv_w, conv_b))
        xs = xbc[..., :SSD_W].reshape(bsz, n, SSD_HEADS, SSD_HEAD_DIM)
        bm = xbc[..., SSD_W:SSD_W + gn].reshape(bsz, n, SSD_GROUPS, SSD_STATE)
        cm = xbc[..., SSD_W + gn:].reshape(bsz, n, SSD_GROUPS, SSD_STATE)
        dt_raw = pp[..., o_dt:o_f].reshape(bsz, n, 2, SSD_HEADS)
        f_raw = pp[..., o_f:o_i].reshape(bsz, n, 2, HG_HEADS, HG_EXPAND)
        iv = pp[..., o_i:o_i + HG_W].reshape(bsz, n, HG_HEADS, HG_VDIM)
        return xs, bm, cm, dt_raw, f_raw, iv

    xs, bm, cm, dt_raw, f_raw, iv = streams(p)
    xs_c, bm_c, cm_c, dt_raw_c, f_raw_c, iv_c = streams(pc)
    q = jax.nn.silu(p[..., o_q:o_g]).reshape(bsz, length, HG_HEADS, HG_EXPAND)
    q_c = jax.nn.silu(pc[..., o_q:o_g]).reshape(bsz, n_ctx, HG_HEADS, HG_EXPAND) if ctx_out else None
    lb = lb.astype(f32).reshape(HG_HEADS, HG_EXPAND)
    ssd0 = jnp.zeros((bsz, SSD_GROUPS, SSD_HEADS // SSD_GROUPS, SSD_HEAD_DIM, SSD_STATE), f32)
    hg0 = jnp.zeros((bsz, HG_HEADS, HG_EXPAND, HG_VDIM), f32)
    y_dirs, o_dirs, yc_dirs, oc_dirs = [], [], [], []
    for d in range(2):
        fl = (lambda a: jnp.flip(a, axis=1)) if d == 1 else (lambda a: a)
        a = -jnp.exp(a_log[d].astype(f32))
        dsk = d_skip[d].astype(f32)
        dtb = dt_bias[d].astype(f32)
        dt_l = jax.nn.softplus(dt_raw[:, :, d] + dtb)
        dt_c = jax.nn.softplus(dt_raw_c[:, :, d] + dtb)
        yc, s_ctx = ssd_scan(fl(xs_c), fl(dt_c), a, fl(bm_c), fl(cm_c), dsk, ssd0, ctx_out)
        yl, _ = ssd_scan(fl(xs), fl(dt_l), a, fl(bm), fl(cm), dsk, s_ctx, True)
        y_dirs.append(fl(yl))
        f_l = lb + (1.0 - lb) * jax.nn.sigmoid(f_raw[:, :, d])
        f_c = lb + (1.0 - lb) * jax.nn.sigmoid(f_raw_c[:, :, d])
        oc, s_hg = hgrn2_scan(fl(q_c) if ctx_out else None, fl(1.0 - f_c), fl(iv_c), fl(jnp.log(f_c)), hg0, ctx_out)
        ol, _ = hgrn2_scan(fl(q), fl(1.0 - f_l), fl(iv), fl(jnp.log(f_l)), s_hg, True)
        o_dirs.append(fl(ol))
        if ctx_out:
            yc_dirs.append(fl(yc))
            oc_dirs.append(fl(oc))

    def merge(yy, oo, pp, n):
        z = pp[..., o_z:o_q]
        g = pp[..., o_g:]
        ys = (yy.reshape(bsz, n, SSD_W) * jax.nn.silu(z)).reshape(bsz, n, SSD_GROUPS, SSD_W // SSD_GROUPS)
        ys = rms_norm(ys, ssd_norm.reshape(SSD_GROUPS, SSD_W // SSD_GROUPS)).reshape(bsz, n, SSD_W)
        hs = rms_norm(oo, hg_norm.reshape(HG_HEADS, HG_VDIM)).reshape(bsz, n, HG_W) * jax.nn.silu(g)
        return jnp.concatenate([ys, hs], axis=-1)

    out = merge(y_dirs[0] + y_dirs[1], o_dirs[0] + o_dirs[1], p, length).astype(h.dtype)
    if not ctx_out:
        return out, None
    out_c = merge(yc_dirs[0] + yc_dirs[1], oc_dirs[0] + oc_dirs[1], pc, n_ctx).astype(hc.dtype)
    return out, out_c


SCAN_Q = 128
SCAN_LEVELS = 7
ODD_COLS = SSD_CONV_CH + 8 * 512
HALO = SUBLANE


def _scan_constants():
    import numpy as np
    q = SCAN_Q
    d_hg, d_ssd, pairs, laters = [], [], [], []
    for direction in (0, 1):
        pos = np.arange(q) if direction == 0 else q - 1 - np.arange(q)
        pj, pt = pos[None, :], pos[:, None]
        top = pj <= pt
        end = pj > pt
        ones = np.ones((SUBLANE, q), bool)
        lv, pr, lt = [], [], []
        for level in range(SCAN_LEVELS):
            b = 2 ** level
            start = (pos // (2 * b)) * (2 * b)
            mid = (start + b)[:, None]
            later = pos >= start + b
            lv.append(np.where(later[:, None], (pj >= mid) & (pj <= pt), (pj > pt) & (pj < mid)))
            pr.append((start[:, None] == start[None, :]) & later[:, None] & ~later[None, :])
            lt.append(np.broadcast_to(later[:, None], (q, LANE)))
        pr.append(np.eye(q, dtype=bool))
        pr.append(top)
        d_hg.append(np.concatenate([top, end] + lv + [ones], axis=0))
        d_ssd.append(np.concatenate([top, end, ones], axis=0))
        pairs.append(np.stack(pr))
        laters.append(np.stack(lt))
    f = np.float32
    return (np.stack(d_hg).astype(f), np.stack(d_ssd).astype(f), np.stack(pairs).astype(f),
            np.stack(laters).astype(f))


def _split_dot(m_bf16, v):
    hi = v.astype(jnp.bfloat16)
    lo = (v - hi.astype(jnp.float32)).astype(jnp.bfloat16)
    return (jnp.dot(m_bf16, hi, preferred_element_type=jnp.float32)
            + jnp.dot(m_bf16, lo, preferred_element_type=jnp.float32))


def _dot_nt(a, b):
    return lax.dot_general(a, b, (((1,), (1,)), ((), ())), preferred_element_type=jnp.float32)


def _dot_tn(a, b):
    return lax.dot_general(a, b, (((0,), (0,)), ((), ())), preferred_element_type=jnp.float32)


def _softplus(x):
    return jnp.maximum(x, 0.0) + jnp.log1p(jnp.exp(-jnp.abs(x)))


def _scan_body(xbc_ref, prev_ref, next_ref, f_ref, iv_ref, q_ref, dt_ref,
               cw_ref, cb_ref, dtb_ref, a_ref, dsk_ref, lb_ref,
               dhg_ref, dssd_ref, pair_ref, later_ref, sel_ref,
               out_ref, s_ssd, s_hg, ydiag_ref, *, n_ctx_chunks, n_chunks):
    d = pl.program_id(0)
    j = pl.program_id(1)
    q_rows = SCAN_Q
    bf = jnp.bfloat16

    @pl.when(j == 0)
    def _():
        s_ssd[...] = jnp.zeros_like(s_ssd)
        s_hg[...] = jnp.zeros_like(s_hg)

    c = jnp.where(d == 0, j, jnp.where(j < n_ctx_chunks, n_ctx_chunks - 1 - j, n_chunks - 1 + n_ctx_chunks - j))
    first = (c == 0) | (c == n_ctx_chunks)
    last = (c == n_ctx_chunks - 1) | (c == n_chunks - 1)

    u = xbc_ref[...]
    row = lax.broadcasted_iota(jnp.int32, u.shape, 0)
    before = jnp.where(first, 0.0, prev_ref[HALO - 1:HALO, :])
    after = jnp.where(last, 0.0, next_ref[0:1, :])
    up = jnp.where(row == 0, before, pltpu.roll(u, 1, 0))
    un = jnp.where(row == q_rows - 1, after, pltpu.roll(u, q_rows - 1, 0))
    xbc = cw_ref[0:1, :] * up + cw_ref[1:2, :] * u + cw_ref[2:3, :] * un + cb_ref[...]
    xbc = xbc * jax.nn.sigmoid(xbc)
    x = xbc[:, :SSD_W]

    dt = _softplus(dt_ref[...] + dtb_ref[...])
    da = dt * a_ref[...]
    r = _split_dot(dssd_ref[...], da)
    cs, to_end, total = r[:q_rows], r[q_rows:2 * q_rows], r[2 * q_rows:2 * q_rows + 1]
    xdt = x * dt
    cs_hi = cs.astype(bf)
    cs_lo = (cs - cs_hi.astype(jnp.float32)).astype(bf)
    cs_rows = _dot_nt(sel_ref[...], cs_hi) + _dot_nt(sel_ref[...], cs_lo)
    l_mask = pair_ref[SCAN_LEVELS + 1]
    decay_in = jnp.exp(cs)
    w_end = (jnp.exp(to_end) * xdt).astype(bf)
    gn = SSD_GROUPS * SSD_STATE
    hpg = SSD_HEADS // SSD_GROUPS
    gw = hpg * SSD_HEAD_DIM
    for g in range(SSD_GROUPS):
        b_g = xbc[:, SSD_W + g * SSD_STATE:SSD_W + (g + 1) * SSD_STATE].astype(bf)
        c_g = xbc[:, SSD_W + gn + g * SSD_STATE:SSD_W + gn + (g + 1) * SSD_STATE].astype(bf)
        scores = _dot_nt(c_g, b_g)
        y_off = jnp.dot(c_g, s_ssd[g].astype(bf), preferred_element_type=jnp.float32)
        for hh in range(hpg):
            h = g * hpg + hh
            lo = h * SSD_HEAD_DIM
            diff = cs[:, lo:lo + 1] - cs_rows[h:h + 1, :]
            decay = jnp.exp(jnp.minimum(diff, 0.0)) * l_mask
            ydiag_ref[:, lo:lo + SSD_HEAD_DIM] = jnp.dot((scores * decay).astype(bf),
                                                         xdt[:, lo:lo + SSD_HEAD_DIM].astype(bf),
                                                         preferred_element_type=jnp.float32)
        cols = slice(g * gw, (g + 1) * gw)
        out_ref[:, cols] = (ydiag_ref[:, cols] + decay_in[:, cols] * y_off + dsk_ref[:, cols] * x[:, cols])
        s_ssd[g] = jnp.exp(total[:, cols]) * s_ssd[g] + _dot_tn(b_g, w_end[:, cols])

    lb = lb_ref[...]
    f = lb + (1.0 - lb) * jax.nn.sigmoid(f_ref[...])
    k_in = 1.0 - f
    qv = q_ref[...]
    qv = qv * jax.nn.sigmoid(qv)
    v_bf = iv_ref[...].astype(bf)
    e = jnp.exp(_split_dot(dhg_ref[...], jnp.log(f)))
    e_top, e_end = e[:q_rows], e[q_rows:2 * q_rows]
    e_tot = e[(2 + SCAN_LEVELS) * q_rows:(2 + SCAN_LEVELS) * q_rows + 1]
    for h in range(HG_HEADS):
        cols = slice(h * HG_EXPAND, (h + 1) * HG_EXPAND)
        q_h, k_h = qv[:, cols], k_in[:, cols]
        att = pair_ref[SCAN_LEVELS] * _dot_nt(q_h.astype(bf), k_h.astype(bf))
        for level in range(SCAN_LEVELS):
            e_l = e[(2 + level) * q_rows:(3 + level) * q_rows, cols]
            later = later_ref[level]
            q_l = (q_h * e_l * later).astype(bf)
            k_l = (k_h * e_l * (1.0 - later)).astype(bf)
            att = att + pair_ref[level] * _dot_nt(q_l, k_l)
        o = jnp.dot(att.astype(bf), v_bf[:, cols], preferred_element_type=jnp.float32)
        o = o + _dot_nt((q_h * e_top[:, cols]).astype(bf), s_hg[h].astype(bf))
        out_ref[:, SSD_W + h * HG_VDIM:SSD_W + (h + 1) * HG_VDIM] = o
        s_hg[h] = e_tot[:, cols] * s_hg[h] + _dot_tn(v_bf[:, cols], (k_h * e_end[:, cols]).astype(bf))


def _odd_scan(p_all, conv_w, conv_b, dtb, a_cols, dsk, lb, n_ctx):
    n_rows = p_all.shape[0]
    n_chunks = n_rows // SCAN_Q
    ncc = n_ctx // SCAN_Q
    d_hg, d_ssd, pairs, laters = _scan_constants()
    bf = jnp.bfloat16
    import numpy as np
    sel = np.zeros((LANE, SSD_W), np.float32)
    sel[np.arange(SSD_HEADS), np.arange(SSD_HEADS) * SSD_HEAD_DIM] = 1.0

    def chunk(d, j):
        return jnp.where(d == 0, j, jnp.where(j < ncc, ncc - 1 - j, n_chunks - 1 + ncc - j))

    per = SCAN_Q // HALO
    last_halo = n_rows // HALO - 1
    col512 = lambda blk: (lambda d, j: (chunk(d, j), blk))
    const2 = lambda shape: pl.BlockSpec(shape, lambda d, j: (0,) * len(shape))
    dirc = lambda shape: pl.BlockSpec((None,) + shape, lambda d, j: (d,) + (0,) * len(shape))
    body = functools.partial(_scan_body, n_ctx_chunks=ncc, n_chunks=n_chunks)
    return pl.pallas_call(
        body,
        grid=(2, n_chunks),
        in_specs=[
            pl.BlockSpec((SCAN_Q, SSD_CONV_CH), lambda d, j: (chunk(d, j), 0)),
            pl.BlockSpec((HALO, SSD_CONV_CH), lambda d, j: (jnp.maximum(chunk(d, j) * per - 1, 0), 0)),
            pl.BlockSpec((HALO, SSD_CONV_CH), lambda d, j: (jnp.minimum((chunk(d, j) + 1) * per, last_halo), 0)),
            pl.BlockSpec((SCAN_Q, 512), lambda d, j: (chunk(d, j), 2 + d)),
            pl.BlockSpec((SCAN_Q, 512), col512(4)),
            pl.BlockSpec((SCAN_Q, 512), col512(6)),
            pl.BlockSpec((SCAN_Q, 512), lambda d, j: (chunk(d, j), 8 + d)),
            const2((CONV_W, SSD_CONV_CH)), const2((1, SSD_CONV_CH)),
            dirc((1, SSD_W)), dirc((1, SSD_W)), dirc((1, SSD_W)), const2((1, HG_W)),
            dirc(d_hg.shape[1:]), dirc(d_ssd.shape[1:]), dirc(pairs.shape[1:]), dirc(laters.shape[1:]),
            const2((LANE, SSD_W)),
        ],
        out_specs=pl.BlockSpec((None, SCAN_Q, MIX_W), lambda d, j: (d, chunk(d, j), 0)),
        out_shape=jax.ShapeDtypeStruct((2, n_rows, MIX_W), jnp.float32),
        scratch_shapes=[pltpu.VMEM((SSD_GROUPS, SSD_STATE, SSD_W // SSD_GROUPS), jnp.float32),
                        pltpu.VMEM((HG_HEADS, HG_VDIM, HG_EXPAND), jnp.float32),
                        pltpu.VMEM((SCAN_Q, SSD_W), jnp.float32)],
        compiler_params=pltpu.CompilerParams(dimension_semantics=("arbitrary", "arbitrary"),
                                             vmem_limit_bytes=MOE_VMEM_LIMIT),
        name="odd_scan",
    )(p_all, p_all, p_all, p_all, p_all, p_all, p_all,
      conv_w, conv_b.reshape(1, -1), dtb, a_cols, dsk, lb.reshape(1, -1),
      jnp.asarray(d_hg, bf), jnp.asarray(d_ssd, bf), jnp.asarray(pairs), jnp.asarray(laters),
      jnp.asarray(sel, bf))


def _group_rms(v, width):
    parts = []
    for lo in range(0, v.shape[1], width):
        seg = v[:, lo:lo + width]
        parts.append(seg * lax.rsqrt(jnp.mean(seg * seg, axis=-1, keepdims=True) + EPS))
    return jnp.concatenate(parts, axis=1)


def _odd_merge_body(yo_ref, z_ref, g_ref, x_ref, sn_ref, hn_ref, gt_ref, w_ref, o_ref):
    yo = yo_ref[0] + yo_ref[1]
    z = z_ref[...]
    g = g_ref[...]
    ys = _group_rms(yo[:, :SSD_W] * (z * jax.nn.sigmoid(z)), SSD_W // SSD_GROUPS) * sn_ref[...]
    hs = _group_rms(yo[:, SSD_W:], HG_VDIM) * hn_ref[...] * (g * jax.nn.sigmoid(g))
    m = jnp.concatenate([ys, hs], axis=1).astype(jnp.bfloat16)
    o_ref[...] = x_ref[...] + gt_ref[...] * jnp.dot(m, w_ref[...].astype(jnp.bfloat16),
                                                    preferred_element_type=jnp.float32)


ODD_TM = 256


def _odd_merge(yo, p_all, x2d, ssd_norm, hg_norm, gt, w_out, n_ctx):
    n_lat, d = x2d.shape
    skip = n_ctx // ODD_TM
    return pl.pallas_call(
        _odd_merge_body,
        grid=(n_lat // ODD_TM,),
        in_specs=[pl.BlockSpec((2, ODD_TM, MIX_W), lambda i: (0, i + skip, 0)),
                  pl.BlockSpec((ODD_TM, 512), lambda i: (i + skip, 5)),
                  pl.BlockSpec((ODD_TM, 512), lambda i: (i + skip, 7)),
                  pl.BlockSpec((ODD_TM, d), lambda i: (i, 0)),
                  pl.BlockSpec((1, SSD_W), lambda i: (0, 0)),
                  pl.BlockSpec((1, HG_W), lambda i: (0, 0)),
                  pl.BlockSpec((1, d), lambda i: (0, 0)),
                  pl.BlockSpec((MIX_W, d), lambda i: (0, 0))],
        out_specs=pl.BlockSpec((ODD_TM, d), lambda i: (i, 0)),
        out_shape=jax.ShapeDtypeStruct((n_lat, d), jnp.float32),
        compiler_params=pltpu.CompilerParams(dimension_semantics=("arbitrary",)),
        name="odd_merge",
    )(yo, p_all, p_all, x2d, ssd_norm.reshape(1, -1), hg_norm.reshape(1, -1), gt.reshape(1, -1), w_out)


def odd_layer(x, h, hc, lb, w_in, conv_w, conv_b, dt_bias, a_log, d_skip, ssd_norm, hg_norm, gt, w_out):
    f32 = jnp.float32
    n_ctx = hc.shape[1]
    o_dt = SSD_CONV_CH
    o_f = o_dt + 2 * SSD_HEADS
    rep = lambda v: jnp.repeat(v, SSD_HEAD_DIM, axis=-1)
    w_perm = jnp.concatenate([w_in[:, :o_dt], w_in[:, o_f:],
                              rep(w_in[:, o_dt:o_dt + SSD_HEADS]), rep(w_in[:, o_dt + SSD_HEADS:o_f])], axis=1)
    h_all = jnp.concatenate([hc[0], h[0]], axis=0)
    p_all = pmm(h_all, w_perm)
    dtb = rep(dt_bias.astype(f32)).reshape(2, 1, SSD_W)
    a_cols = rep(-jnp.exp(a_log.astype(f32))).reshape(2, 1, SSD_W)
    dsk = rep(d_skip.astype(f32)).reshape(2, 1, SSD_W)
    yo = _odd_scan(p_all, conv_w, conv_b, dtb, a_cols, dsk, lb.astype(f32), n_ctx)
    return _odd_merge(yo, p_all, x[0], ssd_norm, hg_norm, gt, w_out, n_ctx)[None]


MOE_TM = 256
MOE_BM = 256
NEG_BIG = -1e30
MOE_VMEM_LIMIT = 52 * 1024 * 1024


def _route_body(t_ref, rw_ref, rb_ref, idx_ref, gate_ref, rank_ref, cnt_ref, run_ref):
    i = pl.program_id(0)

    @pl.when(i == 0)
    def _():
        run_ref[...] = jnp.zeros_like(run_ref)

    tm = t_ref.shape[0]
    logits = jnp.dot(t_ref[...].astype(jnp.bfloat16), rw_ref[...].astype(jnp.bfloat16),
                     preferred_element_type=jnp.float32) + rb_ref[...]
    lane = lax.broadcasted_iota(jnp.int32, (tm, LANE), 1)
    lane_f = lane.astype(jnp.float32)
    work = logits
    vals, sels, hots = [], [], []
    for _ in range(TOP_K):
        m = jnp.max(work, axis=-1, keepdims=True)
        sel = jnp.min(jnp.where(work == m, lane_f, float(LANE)), axis=-1, keepdims=True)
        hot = lane_f == sel
        vals.append(m)
        sels.append(sel.astype(jnp.int32))
        hots.append(hot)
        work = jnp.where(hot, -jnp.inf, work)
    exps = [jnp.exp(v - vals[0]) for v in vals]
    denom = exps[0] + exps[1] + exps[2] + exps[3]
    chosen = jnp.zeros((tm, LANE), jnp.float32)
    for hot in hots:
        chosen = chosen + hot.astype(jnp.float32)
    row = lax.broadcasted_iota(jnp.int32, (tm, tm), 0)
    col = lax.broadcasted_iota(jnp.int32, (tm, tm), 1)
    tri = (row > col).astype(jnp.bfloat16)
    before = jnp.dot(tri, chosen.astype(jnp.bfloat16), preferred_element_type=jnp.float32) + run_ref[0:1, :]
    idx_out = jnp.zeros((tm, LANE), jnp.int32)
    gate_out = jnp.zeros((tm, LANE), jnp.float32)
    rank_out = jnp.zeros((tm, LANE), jnp.int32)
    for k in range(TOP_K):
        rank_k = jnp.sum(jnp.where(hots[k], before, 0.0), axis=-1, keepdims=True).astype(jnp.int32)
        idx_out = jnp.where(lane == k, sels[k], idx_out)
        gate_out = jnp.where(lane == k, exps[k] / denom, gate_out)
        rank_out = jnp.where(lane == k, rank_k, rank_out)
    idx_ref[...] = idx_out
    gate_ref[...] = gate_out
    rank_ref[...] = rank_out
    run_new = run_ref[0:1, :] + jnp.sum(chosen, axis=0, keepdims=True)
    run_ref[...] = jnp.broadcast_to(run_new, run_ref.shape)
    cnt_ref[...] = jnp.broadcast_to(run_new, cnt_ref.shape)


def _moe_route(t, router_w, router_b):
    n_tok, d = t.shape
    rw = jnp.pad(router_w, ((0, 0), (0, LANE - N_EXPERTS)))
    rb = jnp.pad(router_b.astype(jnp.float32), (0, LANE - N_EXPERTS), constant_values=NEG_BIG).reshape(1, LANE)
    tile = pl.BlockSpec((MOE_TM, LANE), lambda i: (i, 0))
    idx, gate, rank, cnt = pl.pallas_call(
        _route_body,
        grid=(n_tok // MOE_TM,),
        in_specs=[pl.BlockSpec((MOE_TM, d), lambda i: (i, 0)),
                  pl.BlockSpec((d, LANE), lambda i: (0, 0)),
                  pl.BlockSpec((1, LANE), lambda i: (0, 0))],
        out_specs=[tile, tile, tile, pl.BlockSpec((SUBLANE, LANE), lambda i: (0, 0))],
        out_shape=[jax.ShapeDtypeStruct((n_tok, LANE), jnp.int32),
                   jax.ShapeDtypeStruct((n_tok, LANE), jnp.float32),
                   jax.ShapeDtypeStruct((n_tok, LANE), jnp.int32),
                   jax.ShapeDtypeStruct((SUBLANE, LANE), jnp.float32)],
        scratch_shapes=[pltpu.VMEM((SUBLANE, LANE), jnp.float32)],
        compiler_params=pltpu.CompilerParams(dimension_semantics=("arbitrary",)),
        name="moe_route",
    )(t, rw, rb)
    return idx[:, :TOP_K], gate, rank[:, :TOP_K], cnt[0, :N_EXPERTS].astype(jnp.int32)


def _row_copy(src_ref, src_row, dst_ref, dst_row, sem):
    return pltpu.make_async_copy(src_ref.at[pl.ds(src_row, 1)], dst_ref.at[pl.ds(dst_row, 1)], sem)


def _scatter_body(off_ref, pad_ref, dest_ref, t_ref, xs_ref, zero_ref, sem):
    i = pl.program_id(0)
    tm = t_ref.shape[0]

    @pl.when(i == 0)
    def _():
        zero_ref[...] = jnp.zeros_like(zero_ref)
        used = off_ref[N_EXPERTS - 1] + pad_ref[N_EXPERTS - 1]
        n_rows = xs_ref.shape[0]

        def zero_block(start):
            return pltpu.make_async_copy(zero_ref, xs_ref.at[pl.ds(pl.multiple_of(start, MOE_BM), MOE_BM)], sem)

        for e in range(N_EXPERTS):
            tail = n_rows - (e + 1) * MOE_BM

            @pl.when(pad_ref[e] > 0)
            def _():
                zero_block(off_ref[e] + pad_ref[e] - MOE_BM).start()

            @pl.when(tail >= used)
            def _():
                zero_block(tail).start()
        for e in range(N_EXPERTS):
            tail = n_rows - (e + 1) * MOE_BM

            @pl.when(pad_ref[e] > 0)
            def _():
                zero_block(0).wait()

            @pl.when(tail >= used)
            def _():
                zero_block(0).wait()

    def issue(t, carry):
        for k in range(TOP_K):
            _row_copy(t_ref, t, xs_ref, dest_ref[TOP_K * t + k], sem).start()
        return carry

    lax.fori_loop(0, tm, issue, 0)
    for _ in range(TOP_K):
        pltpu.make_async_copy(t_ref, xs_ref.at[pl.ds(0, tm)], sem).wait()


def _moe_scatter(t, dest_flat, off, padded, n_rows):
    n_tok, d = t.shape
    return pl.pallas_call(
        _scatter_body,
        grid_spec=pltpu.PrefetchScalarGridSpec(
            num_scalar_prefetch=2,
            grid=(n_tok // MOE_TM,),
            in_specs=[pl.BlockSpec((TOP_K * MOE_TM,), lambda i, off, pad: (i,), memory_space=pltpu.SMEM),
                      pl.BlockSpec((MOE_TM, d), lambda i, off, pad: (i, 0))],
            out_specs=pl.BlockSpec(memory_space=pl.ANY),
            scratch_shapes=[pltpu.VMEM((MOE_BM, d), jnp.float32), pltpu.SemaphoreType.DMA],
        ),
        out_shape=jax.ShapeDtypeStruct((n_rows, d), jnp.float32),
        compiler_params=pltpu.CompilerParams(dimension_semantics=("arbitrary",)),
        name="moe_scatter",
    )(off, padded, dest_flat, t)


def _expert_body(blk_e_ref, n_act_ref, x_ref, wgu_ref, bgu_ref, wdn_ref, bdn_ref, y_ref, wgu_bf, wdn_bf):
    i = pl.program_id(0)

    @pl.when(i < n_act_ref[0])
    def _():
        prev = blk_e_ref[jnp.maximum(i - 1, 0)]

        @pl.when((i == 0) | (blk_e_ref[i] != prev))
        def _():
            wgu_bf[...] = wgu_ref[...].astype(jnp.bfloat16)
            wdn_bf[...] = wdn_ref[...].astype(jnp.bfloat16)

        x = x_ref[...].astype(jnp.bfloat16)
        gu = jnp.dot(x, wgu_bf[...], preferred_element_type=jnp.float32) + bgu_ref[...]
        gate = jnp.minimum(gu[:, :D_EXPERT], SWIGLU_LIMIT)
        up = jnp.clip(gu[:, D_EXPERT:], -SWIGLU_LIMIT, SWIGLU_LIMIT)
        act = (up + 1.0) * gate * jax.nn.sigmoid(SWIGLU_ALPHA * gate)
        y_ref[...] = jnp.dot(act.astype(jnp.bfloat16), wdn_bf[...],
                             preferred_element_type=jnp.float32) + bdn_ref[...]

    @pl.when(i >= n_act_ref[0])
    def _():
        y_ref[...] = jnp.zeros_like(y_ref)


def _moe_experts(xs, blk_e, n_act, w_gu, b_gu, w_dn, b_dn):
    n_rows, d = xs.shape
    n_blk = n_rows // MOE_BM

    def blk(i, be, na):
        return jnp.minimum(i, na[0] - 1)

    return pl.pallas_call(
        _expert_body,
        grid_spec=pltpu.PrefetchScalarGridSpec(
            num_scalar_prefetch=2,
            grid=(n_blk,),
            in_specs=[pl.BlockSpec((MOE_BM, d), lambda i, be, na: (blk(i, be, na), 0)),
                      pl.BlockSpec((None, d, 2 * D_EXPERT), lambda i, be, na: (be[blk(i, be, na)], 0, 0)),
                      pl.BlockSpec((None, 1, 2 * D_EXPERT), lambda i, be, na: (be[blk(i, be, na)], 0, 0)),
                      pl.BlockSpec((None, D_EXPERT, d), lambda i, be, na: (be[blk(i, be, na)], 0, 0)),
                      pl.BlockSpec((None, 1, d), lambda i, be, na: (be[blk(i, be, na)], 0, 0))],
            out_specs=pl.BlockSpec((MOE_BM, d), lambda i, be, na: (i, 0)),
            scratch_shapes=[pltpu.VMEM((d, 2 * D_EXPERT), jnp.bfloat16),
                            pltpu.VMEM((D_EXPERT, d), jnp.bfloat16)],
        ),
        out_shape=jax.ShapeDtypeStruct((n_rows, d), jnp.float32),
        compiler_params=pltpu.CompilerParams(dimension_semantics=("arbitrary",),
                                             vmem_limit_bytes=MOE_VMEM_LIMIT),
        name="moe_experts",
    )(blk_e, n_act, xs, w_gu, b_gu.reshape(N_EXPERTS, 1, -1), w_dn, b_dn.reshape(N_EXPERTS, 1, -1))


def _combine_body(dest_ref, dest_nxt_ref, gate_ref, ys_ref, y_ref, buf_ref, sems):
    i = pl.program_id(0)
    n = pl.num_programs(0)
    tm = y_ref.shape[0]

    def fetch(d_ref, slot):
        def issue(t, carry):
            for k in range(TOP_K):
                _row_copy(ys_ref, d_ref[TOP_K * t + k], buf_ref.at[slot, k], t, sems.at[slot]).start()
            return carry
        lax.fori_loop(0, tm, issue, 0)

    @pl.when(i == 0)
    def _():
        fetch(dest_ref, 0)

    @pl.when(i + 1 < n)
    def _():
        fetch(dest_nxt_ref, (i + 1) % 2)

    slot = i % 2
    for k in range(TOP_K):
        pltpu.make_async_copy(ys_ref.at[pl.ds(0, tm)], buf_ref.at[slot, k], sems.at[slot]).wait()
    g = gate_ref[...]
    acc = g[:, 0:1] * buf_ref[slot, 0]
    for k in range(1, TOP_K):
        acc = acc + g[:, k:k + 1] * buf_ref[slot, k]
    y_ref[...] = acc


def _moe_combine(ys, dest_flat, gate, n_tok):
    d = ys.shape[1]
    n_tiles = n_tok // MOE_TM
    return pl.pallas_call(
        _combine_body,
        grid=(n_tiles,),
        in_specs=[pl.BlockSpec((TOP_K * MOE_TM,), lambda i: (i,), memory_space=pltpu.SMEM),
                  pl.BlockSpec((TOP_K * MOE_TM,), lambda i: (jnp.minimum(i + 1, n_tiles - 1),),
                               memory_space=pltpu.SMEM),
                  pl.BlockSpec((MOE_TM, LANE), lambda i: (i, 0)),
                  pl.BlockSpec(memory_space=pl.ANY)],
        out_specs=pl.BlockSpec((MOE_TM, d), lambda i: (i, 0)),
        out_shape=jax.ShapeDtypeStruct((n_tok, d), jnp.float32),
        scratch_shapes=[pltpu.VMEM((2, TOP_K, MOE_TM, d), jnp.float32), pltpu.SemaphoreType.DMA((2,))],
        compiler_params=pltpu.CompilerParams(dimension_semantics=("arbitrary",)),
        name="moe_combine",
    )(dest_flat, dest_flat, gate, ys)


def moe_ffn(t, router_w, router_b, w_gu, b_gu, w_dn, b_dn):
    n_tok, d = t.shape
    assert n_tok % MOE_TM == 0
    idx, gate, rank, counts = _moe_route(t, router_w, router_b)
    padded = (counts + MOE_BM - 1) // MOE_BM * MOE_BM
    pad_end = jnp.cumsum(padded)
    off = pad_end - padded
    experts = jnp.arange(N_EXPERTS, dtype=jnp.int32)
    dest = rank + jnp.sum(jnp.where(idx[..., None] == experts, off, 0), axis=-1)
    dest_flat = dest.reshape(-1).astype(jnp.int32)
    n_blk = -(-(n_tok * TOP_K) // MOE_BM) + N_EXPERTS
    blk_e = jnp.minimum(jnp.searchsorted(pad_end, jnp.arange(n_blk) * MOE_BM, side='right'),
                        N_EXPERTS - 1).astype(jnp.int32)
    n_act = (pad_end[-1:] // MOE_BM).astype(jnp.int32)
    xs = _moe_scatter(t, dest_flat, off.astype(jnp.int32), padded.astype(jnp.int32), n_blk * MOE_BM)
    ys = _moe_experts(xs, blk_e, n_act, w_gu, b_gu, w_dn, b_dn)
    return _moe_combine(ys, dest_flat, gate, n_tok)


def kernel(x, c, ctx, c_ctx, norm_g, ada_w, ada_b, w_out, w_in_even, hy_conv_w, hy_conv_b,
           hy_w1, hy_b1, hy_w2, hy_b2, hy_w3, hy_b3, hy_w4, hy_freq, hy_filter_bias,
           att_q_norm, att_k_norm, att_sink, w_in_odd, ssd_conv_w, ssd_conv_b, ssd_dt_bias,
           ssd_A_log, ssd_D, ssd_norm, hg_lower_bounds, hg_norm, router_w, router_b,
           moe_w_gu, moe_b_gu, moe_w_dn, moe_b_dn):
    lbs = jax.nn.softmax(hg_lower_bounds.astype(jnp.float32), axis=0)
    lbs = jnp.cumsum(lbs, axis=0) - lbs[0]
    xc = ctx
    for layer in range(DEPTH):
        ctx_out = layer < DEPTH - 1
        i = layer // 2
        sh, sc, gt = adaln(c, ada_w[layer], ada_b[layer], 0)
        sh_c, sc_c, gt_c = adaln(c_ctx, ada_w[layer], ada_b[layer], 0)
        h = modulate(rms_norm(x, norm_g[layer, 0]), sh[:, None], sc[:, None])
        hc = modulate(rms_norm(xc, norm_g[layer, 0]), sh_c, sc_c)
        if layer % 2 == 0:
            m, m_c = even_mixer(h, hc, w_in_even[i], hy_conv_w[i], hy_conv_b[i], hy_w1[i], hy_b1[i],
                                hy_w2[i], hy_b2[i], hy_w3[i], hy_b3[i], hy_w4[i], hy_freq[i],
                                hy_filter_bias[i], att_q_norm[i], att_k_norm[i], att_sink[i], ctx_out)
            x = x + gt[:, None] * pmm3(m, w_out[layer])
            if ctx_out:
                xc = xc + gt_c * pmm3(m_c, w_out[layer])
        elif not ctx_out:
            x = odd_layer(x, h, hc, lbs[layer], w_in_odd[i], ssd_conv_w[i], ssd_conv_b[i], ssd_dt_bias[i],
                          ssd_A_log[i], ssd_D[i], ssd_norm[i], hg_norm[i], gt, w_out[layer])
        else:
            m, m_c = odd_mixer(h, hc, lbs[layer], w_in_odd[i], ssd_conv_w[i], ssd_conv_b[i], ssd_dt_bias[i],
                               ssd_A_log[i], ssd_D[i], ssd_norm[i], hg_norm[i], ctx_out)
            x = x + gt[:, None] * pmm3(m, w_out[layer])
            xc = xc + gt_c * pmm3(m_c, w_out[layer])
        sh, sc, gt = adaln(c, ada_w[layer], ada_b[layer], 1)
        h = modulate(rms_norm(x, norm_g[layer, 1]), sh[:, None], sc[:, None])
        moe_params = (router_w[layer], router_b[layer], moe_w_gu[layer], moe_b_gu[layer],
                      moe_w_dn[layer], moe_b_dn[layer])
        n_lat = x.shape[0] * x.shape[1]
        if ctx_out:
            sh_c, sc_c, gt_c = adaln(c_ctx, ada_w[layer], ada_b[layer], 1)
            hc = modulate(rms_norm(xc, norm_g[layer, 1]), sh_c, sc_c)
            y = moe_ffn(jnp.concatenate([h.reshape(n_lat, D_MODEL), hc.reshape(-1, D_MODEL)], axis=0), *moe_params)
            x = x + gt[:, None] * y[:n_lat].reshape(x.shape)
            xc = xc + gt_c * y[n_lat:].reshape(xc.shape)
        else:
            x = x + gt[:, None] * moe_ffn(h.reshape(n_lat, D_MODEL), *moe_params).reshape(x.shape)
    return x
```

```python
import functools
import math

import jax
import jax.numpy as jnp
from jax import lax
from jax.experimental import pallas as pl
from jax.experimental.pallas import tpu as pltpu

D_MODEL = 1024
DEPTH = 2
GRID_W = 64
MIX_W = D_MODEL
EPS = 1e-6
CONV_W = 3

HY_W = MIX_W // 2
HY_ORDER = 2
HY_EMB = 33
HY_FFN = 64
HY_TARGET = 1e-2
HY_SHORT_PCT = 0.3
HY_LONG_PCT = 1.5

HEAD_DIM = 64
ATT_HEADS = (MIX_W // 2) // HEAD_DIM
ATT_KV_HEADS = 2
ATT_GROUP = ATT_HEADS // ATT_KV_HEADS
ATT_WINDOW = 128
ATT_BLOCK = 128
ROPE_BASE = 10000.0
ATT_Q_W = ATT_HEADS * HEAD_DIM
ATT_KV_W = ATT_KV_HEADS * HEAD_DIM

SSD_W = MIX_W // 2
SSD_HEAD_DIM = 64
SSD_HEADS = SSD_W // SSD_HEAD_DIM
SSD_GROUPS = 2
SSD_STATE = 128
SSD_CHUNK = 128
SSD_CONV_CH = SSD_W + 2 * SSD_GROUPS * SSD_STATE

HG_W = MIX_W // 2
HG_EXPAND = 128
HG_HEADS = HG_W // HG_EXPAND
HG_VDIM = HG_W // HG_HEADS
HG_CHUNK = 64

N_EXPERTS = 32
TOP_K = 4
D_EXPERT = D_MODEL
SWIGLU_ALPHA = 1.702
SWIGLU_LIMIT = 7.0
MOE_BLOCK = 128

EVEN_IN = 2 * ATT_KV_W + ATT_Q_W + 3 * HY_W
ODD_STATE_COLS = SSD_CONV_CH + 2 * SSD_HEADS + 3 * HG_W
ODD_IN = ODD_STATE_COLS + SSD_W + 2 * HG_W

LANE = 128
SUBLANE = 8


def _mm_body(a_ref, b_ref, o_ref):
    a = a_ref[...].astype(jnp.bfloat16)
    b = b_ref[...].astype(jnp.bfloat16)
    o_ref[...] = jnp.dot(a, b, preferred_element_type=jnp.float32)


def _pick_tile(n, candidates):
    for c in candidates:
        if n % c == 0:
            return c
    return n


def pmm(a, b):
    m, k = a.shape
    n = b.shape[1]
    n_pad = -(-n // LANE) * LANE
    if n_pad != n:
        b = jnp.pad(b, ((0, 0), (0, n_pad - n)))
    m_pad = -(-m // SUBLANE) * SUBLANE
    if m_pad != m:
        a = jnp.pad(a, ((0, m_pad - m), (0, 0)))
    tm = _pick_tile(m_pad, (512, 256, 128, 64, 32, 16, 8))
    tn = _pick_tile(n_pad, (512, 384, 256, 128))
    out = pl.pallas_call(
        _mm_body,
        grid=(m_pad // tm, n_pad // tn),
        in_specs=[pl.BlockSpec((tm, k), lambda i, j: (i, 0)),
                  pl.BlockSpec((k, tn), lambda i, j: (0, j))],
        out_specs=pl.BlockSpec((tm, tn), lambda i, j: (i, j)),
        out_shape=jax.ShapeDtypeStruct((m_pad, n_pad), jnp.float32),
        name="dense_mm",
    )(a, b)
    return out[:m, :n]


def pmm3(a, b):
    lead = a.shape[:-1]
    return pmm(a.reshape(-1, a.shape[-1]), b).reshape(*lead, b.shape[-1])


def rms_norm(x, g):
    xf = x.astype(jnp.float32)
    y = xf * lax.rsqrt(jnp.mean(xf * xf, axis=-1, keepdims=True) + EPS)
    return (y * g.astype(jnp.float32)).astype(x.dtype)


def modulate(h, shift, scale):
    return h * (1.0 + scale) + shift


def adaln(cond, w, b, j):
    lo, hi = 3 * j * D_MODEL, 3 * (j + 1) * D_MODEL
    m = jax.nn.silu(cond) @ w[:, lo:hi] + b[lo:hi]
    return jnp.split(m, 3, axis=-1)


def dwconv_centred(u, w, b):
    ch = u.shape[-1]
    y = lax.conv_general_dilated(u, w[:, None, :].astype(u.dtype), window_strides=(1,),
                                 padding=[(CONV_W // 2, CONV_W // 2)],
                                 dimension_numbers=('NWC', 'WIO', 'NWC'), feature_group_count=ch)
    return y + b.astype(u.dtype)


def axial_rope_tables(length):
    rows = length // GRID_W
    n_pairs = HEAD_DIM // 4
    inv = ROPE_BASE ** (-jnp.arange(n_pairs, dtype=jnp.float32) / n_pairs)
    row_ang = jnp.arange(rows, dtype=jnp.float32)[:, None] * inv
    col_ang = jnp.arange(GRID_W, dtype=jnp.float32)[:, None] * inv
    ang_r = jnp.broadcast_to(row_ang[:, None], (rows, GRID_W, n_pairs)).reshape(length, n_pairs)
    ang_c = jnp.broadcast_to(col_ang[None], (rows, GRID_W, n_pairs)).reshape(length, n_pairs)
    return jnp.cos(ang_r), jnp.sin(ang_r), jnp.cos(ang_c), jnp.sin(ang_c)


def _rotate(u, cos, sin):
    n = u.shape[-1] // 2
    u1, u2 = u[..., :n], u[..., n:]
    cos = cos[None, :, None, :]
    sin = sin[None, :, None, :]
    return jnp.concatenate([u1 * cos - u2 * sin, u1 * sin + u2 * cos], axis=-1)


def apply_axial_rope(u, tables):
    cr, sr, cc, sc = tables
    half = HEAD_DIM // 2
    return jnp.concatenate([_rotate(u[..., :half], cr, sr), _rotate(u[..., half:], cc, sc)], axis=-1)


def hyena_filters(length, w1, b1, w2, b2, w3, b3, w4, freq):
    f32 = jnp.float32
    t = jnp.linspace(0.0, 1.0, length, dtype=f32)[:, None]
    bands = (HY_EMB - 1) // 2
    w_ang = 2.0 * math.pi * jnp.arange(length, dtype=f32)[:, None] / length
    fr = jnp.linspace(1e-4, bands - 1, bands, dtype=f32)[None]
    z = jnp.concatenate([t, jnp.cos(fr * w_ang), -jnp.sin(fr * w_ang)], axis=-1)
    fq = freq.astype(f32)
    hdn = jnp.sin(fq * (z @ w1.astype(f32) + b1.astype(f32)))
    hdn = jnp.sin(fq * (hdn @ w2.astype(f32) + b2.astype(f32)))
    hdn = jnp.sin(fq * (hdn @ w3.astype(f32) + b3.astype(f32)))
    h = (hdn @ w4.astype(f32)).reshape(length, HY_ORDER, 2, HY_W)
    max_decay = math.log(HY_TARGET) / HY_SHORT_PCT
    min_decay = math.log(HY_TARGET) / HY_LONG_PCT
    deltas = jnp.abs(jnp.linspace(min_decay, max_decay, HY_W, dtype=f32))
    h = h * jnp.exp(-t * deltas)[:, None, None, :]
    h2 = jnp.concatenate([h[:, :, 0], jnp.zeros((1, HY_ORDER, HY_W), f32), h[:0:-1, :, 1]], axis=0)
    h2 = h2 / jnp.sum(jnp.abs(h2), axis=0, keepdims=True)
    return jnp.fft.rfft(h2, axis=0)


def hyena_mix(u, hf, filter_bias, conv_w, conv_b):
    length = u.shape[1]
    u = dwconv_centred(u.astype(jnp.float32), conv_w, conv_b)
    v, x1, x2 = jnp.split(u, 3, axis=-1)
    z = v
    for o, gate in enumerate((x1, x2)):
        zf = jnp.fft.rfft(z, n=2 * length, axis=1)
        zc = jnp.fft.irfft(zf * hf[None, :, o], n=2 * length, axis=1)[:, :length]
        z = gate * (zc + z * filter_bias[o].astype(jnp.float32))
    return z


def window_attention(q, k, v, k_c, v_c, sink):
    bsz, length = q.shape[:2]
    nb = length // ATT_BLOCK
    scale = HEAD_DIM ** -0.5
    qb = q.reshape(bsz, nb, ATT_BLOCK, ATT_KV_HEADS, ATT_GROUP, HEAD_DIM)
    pad = ((0, 0), (ATT_BLOCK, ATT_BLOCK), (0, 0), (0, 0))

    def band(a):
        ap = jnp.pad(a, pad).reshape(bsz, nb + 2, ATT_BLOCK, ATT_KV_HEADS, HEAD_DIM)
        return jnp.concatenate([ap[:, :-2], ap[:, 1:-1], ap[:, 2:]], axis=2)

    kw, vw = band(k), band(v)
    s_loc = jnp.einsum('bnqhgd,bnkhd->bnhgqk', qb, kw) * scale
    s_ctx = jnp.einsum('bnqhgd,bchd->bnhgqc', qb, k_c) * scale
    qpos = jnp.arange(nb)[:, None] * ATT_BLOCK + jnp.arange(ATT_BLOCK)[None]
    kpos = (jnp.arange(nb)[:, None] - 1) * ATT_BLOCK + jnp.arange(3 * ATT_BLOCK)[None]
    rel = kpos[:, None, :] - qpos[:, :, None]
    valid = (jnp.abs(rel) <= ATT_WINDOW) & (kpos[:, None, :] >= 0) & (kpos[:, None, :] < length)
    s_loc = jnp.where(valid[None, :, None, None], s_loc, -jnp.inf)
    sink_l = jnp.broadcast_to(sink.astype(jnp.float32).reshape(1, 1, ATT_KV_HEADS, ATT_GROUP, 1, 1),
                              s_loc.shape[:-1] + (1,))
    p = jax.nn.softmax(jnp.concatenate([s_loc, s_ctx, sink_l], axis=-1), axis=-1)
    n_loc = 3 * ATT_BLOCK
    n_ctx = k_c.shape[1]
    o = (jnp.einsum('bnhgqk,bnkhd->bnqhgd', p[..., :n_loc], vw)
         + jnp.einsum('bnhgqc,bchd->bnqhgd', p[..., n_loc:n_loc + n_ctx], v_c))
    return o.reshape(bsz, length, ATT_Q_W)


def context_attention(q_c, k_c, v_c, sink):
    bsz, n_ctx = q_c.shape[:2]
    s = jnp.einsum('bqhgd,bkhd->bhgqk', q_c, k_c) * HEAD_DIM ** -0.5
    sink_l = jnp.broadcast_to(sink.astype(jnp.float32).reshape(1, ATT_KV_HEADS, ATT_GROUP, 1, 1),
                              s.shape[:-1] + (1,))
    p = jax.nn.softmax(jnp.concatenate([s, sink_l], axis=-1), axis=-1)[..., :-1]
    return jnp.einsum('bhgqk,bkhd->bqhgd', p, v_c).reshape(bsz, n_ctx, ATT_Q_W)


def ssd_scan(x, dt, a, bm, cm, d_skip, init, need_y):
    bsz, length, n_heads, hd = x.shape
    nc = length // SSD_CHUNK
    hpg = n_heads // SSD_GROUPS
    da = (dt * a).reshape(bsz, nc, SSD_CHUNK, SSD_GROUPS, hpg)
    cs = jnp.cumsum(da, axis=2)
    xdt = (x * dt[..., None]).reshape(bsz, nc, SSD_CHUNK, SSD_GROUPS, hpg, hd)
    bc = bm.reshape(bsz, nc, SSD_CHUNK, SSD_GROUPS, SSD_STATE)
    cc = cm.reshape(bsz, nc, SSD_CHUNK, SSD_GROUPS, SSD_STATE)
    to_end = jnp.exp(cs[:, :, -1:] - cs)
    states = jnp.einsum('bcsgn,bcsgh,bcsghp->bcghpn', bc, to_end, xdt)
    chunk_decay = jnp.exp(cs[:, :, -1])

    def step(s, inp):
        st, dec = inp
        return s * dec[..., None, None] + st, s

    s_final, s_in = lax.scan(step, init, (jnp.moveaxis(states, 1, 0), jnp.moveaxis(chunk_decay, 1, 0)))
    if not need_y:
        return None, s_final
    s_in = jnp.moveaxis(s_in, 0, 1)
    cs_t = jnp.moveaxis(cs, 2, -1)
    diff = cs_t[..., :, None] - cs_t[..., None, :]
    lower = jnp.tril(jnp.ones((SSD_CHUNK, SSD_CHUNK), bool))
    decay = jnp.where(lower, jnp.exp(jnp.where(lower, diff, 0.0)), 0.0)
    scores = jnp.einsum('bclgn,bcsgn->bcgls', cc, bc)
    y_diag = jnp.einsum('bcgls,bcghls,bcsghp->bclghp', scores, decay, xdt)
    y_off = jnp.einsum('bclgn,bcghpn,bclgh->bclghp', cc, s_in, jnp.exp(cs))
    y = (y_diag + y_off).reshape(bsz, length, n_heads, hd) + d_skip[:, None] * x
    return y, s_final


def hgrn2_scan(q, k, v, g, init, need_o):
    bsz, length, n_heads, _ = k.shape
    nc = length // HG_CHUNK

    def chunks(a):
        return a.reshape(bsz, nc, HG_CHUNK, n_heads, a.shape[-1]).transpose(1, 0, 3, 2, 4)

    lower = jnp.tril(jnp.ones((HG_CHUNK, HG_CHUNK), bool))[:, :, None]

    def update(s, kc, vc, cum):
        last = cum[:, :, -1]
        return (s * jnp.exp(last)[..., None]
                + jnp.einsum('bhsk,bhsv->bhkv', kc * jnp.exp(last[:, :, None] - cum), vc))

    if not need_o:
        def step_state(s, inp):
            kc, vc, gc = inp
            return update(s, kc, vc, jnp.cumsum(gc, axis=2)), None
        s_final, _ = lax.scan(step_state, init, (chunks(k), chunks(v), chunks(g)))
        return None, s_final

    def step(s, inp):
        qc, kc, vc, gc = inp
        cum = jnp.cumsum(gc, axis=2)
        diff = cum[:, :, :, None, :] - cum[:, :, None, :, :]
        decay = jnp.where(lower, jnp.exp(jnp.where(lower, diff, 0.0)), 0.0)
        att = jnp.einsum('bhtk,bhsk,bhtsk->bhts', qc, kc, decay)
        o = (jnp.einsum('bhtk,bhkv->bhtv', qc * jnp.exp(cum), s)
             + jnp.einsum('bhts,bhsv->bhtv', att, vc))
        return update(s, kc, vc, cum), o

    s_final, o = lax.scan(step, init, (chunks(q), chunks(k), chunks(v), chunks(g)))
    return o.transpose(1, 0, 3, 2, 4).reshape(bsz, length, n_heads, v.shape[-1]), s_final


HY_N1 = 128
HY_N2 = 256
HY_NB = 8
HY_TM = 512


def _hy_dft_constants(length):
    import numpy as np
    n = 2 * length
    assert n == HY_N1 * HY_N2
    half = HY_N1 // 2
    k1 = np.arange(HY_N1)[:, None]
    n1 = np.arange(half)[None, :]
    n2 = np.arange(HY_N2)[:, None, None]
    ang = 2 * np.pi * (k1 * n1 / HY_N1)[None] + 2 * np.pi * n2 * k1[None] / n
    fwd = np.concatenate([np.cos(ang), -np.sin(ang)], axis=1)
    inv = np.concatenate([np.cos(ang), -np.sin(ang)], axis=1).transpose(0, 2, 1)
    a2 = 2 * np.pi * np.outer(np.arange(HY_N2), np.arange(HY_N2)) / HY_N2
    return fwd.astype(np.float32), inv.astype(np.float32), np.cos(a2).astype(np.float32), (-np.sin(a2)).astype(np.float32)


def _hy_conv_body(p_ref, prev_ref, next_ref, w_ref, b_ref, o_ref):
    i = pl.program_id(1)
    n = pl.num_programs(1)
    u = p_ref[...]
    rows = u.shape[0]
    row = lax.broadcasted_iota(jnp.int32, u.shape, 0)
    before = jnp.where(i == 0, 0.0, prev_ref[HALO - 1:HALO, :])
    after = jnp.where(i == n - 1, 0.0, next_ref[0:1, :])
    up = jnp.where(row == 0, before, pltpu.roll(u, 1, 0))
    un = jnp.where(row == rows - 1, after, pltpu.roll(u, rows - 1, 0))
    o_ref[...] = w_ref[0:1, :] * up + w_ref[1:2, :] * u + w_ref[2:3, :] * un + b_ref[...]


def _hy_conv(p, conv_w, conv_b, tm):
    length = p.shape[0]
    per = tm // HALO
    last = length // HALO - 1
    return pl.pallas_call(
        _hy_conv_body,
        grid=(3, length // tm),
        in_specs=[pl.BlockSpec((tm, HY_W), lambda c, i: (i, c)),
                  pl.BlockSpec((HALO, HY_W), lambda c, i: (jnp.maximum(i * per - 1, 0), c)),
                  pl.BlockSpec((HALO, HY_W), lambda c, i: (jnp.minimum((i + 1) * per, last), c)),
                  pl.BlockSpec((CONV_W, HY_W), lambda c, i: (0, c)),
                  pl.BlockSpec((1, HY_W), lambda c, i: (0, c))],
        out_specs=pl.BlockSpec((None, tm, HY_W), lambda c, i: (c, i, 0)),
        out_shape=jax.ShapeDtypeStruct((3, length, HY_W), jnp.float32),
        name="hy_conv",
    )(p, p, p, conv_w, conv_b.reshape(1, -1))


def _hy_filter_body(z_ref, w1_ref, b1_ref, w2_ref, b2_ref, w3_ref, b3_ref, w4_ref, fq_ref, dl_ref,
                    e_ref, o_ref, nrm_ref):
    i = pl.program_id(0)
    bf = jnp.bfloat16
    z = z_ref[...]
    fq = fq_ref[...]

    def layer(a, w_ref, b_ref):
        return jnp.sin(fq * (jnp.dot(a.astype(bf), w_ref[...].astype(bf), preferred_element_type=jnp.float32)
                             + b_ref[...]))

    hdn = layer(layer(layer(z, w1_ref, b1_ref), w2_ref, b2_ref), w3_ref, b3_ref)
    h = jnp.dot(hdn.astype(bf), w4_ref[...].astype(bf), preferred_element_type=jnp.float32)
    decay = jnp.exp(-z[:, 0:1] * dl_ref[...])
    first = (lax.broadcasted_iota(jnp.int32, decay.shape, 0) == 0) & (i == 0)
    acc = []
    for order in range(HY_ORDER):
        lo = order * 2 * HY_W
        h0 = h[:, lo:lo + HY_W] * decay
        h1 = jnp.where(first, 0.0, h[:, lo + HY_W:lo + 2 * HY_W] * decay)
        e_ref[:, order * HY_W:(order + 1) * HY_W] = h0 + h1
        o_ref[:, order * HY_W:(order + 1) * HY_W] = h0 - h1
        acc.append(jnp.sum(jnp.abs(h0) + jnp.abs(h1), axis=0, keepdims=True))
    part = jnp.concatenate(acc, axis=1)

    @pl.when(i == 0)
    def _():
        nrm_ref[...] = jnp.zeros_like(nrm_ref)

    nrm_ref[...] = nrm_ref[...] + part


def _hy_filter_time(length, w1, b1, w2, b2, w3, b3, w4, freq, tm):
    f32 = jnp.float32
    t = jnp.linspace(0.0, 1.0, length, dtype=f32)[:, None]
    bands = (HY_EMB - 1) // 2
    w_ang = 2.0 * math.pi * jnp.arange(length, dtype=f32)[:, None] / length
    fr = jnp.linspace(1e-4, bands - 1, bands, dtype=f32)[None]
    z = jnp.concatenate([t, jnp.cos(fr * w_ang), -jnp.sin(fr * w_ang)], axis=-1)
    z = jnp.pad(z, ((0, 0), (0, LANE - HY_EMB)))
    w1p = jnp.pad(w1.astype(f32), ((0, LANE - HY_EMB), (0, 0)))
    max_decay = math.log(HY_TARGET) / HY_SHORT_PCT
    min_decay = math.log(HY_TARGET) / HY_LONG_PCT
    deltas = jnp.abs(jnp.linspace(min_decay, max_decay, HY_W, dtype=f32)).reshape(1, HY_W)
    full = lambda a: pl.BlockSpec(a.shape, lambda i: (0,) * a.ndim)
    row = lambda v: v.astype(f32).reshape(1, -1)
    args = (z, w1p, row(b1), w2.astype(f32), row(b2), w3.astype(f32), row(b3), w4.astype(f32), row(freq), deltas)
    ow = HY_ORDER * HY_W
    return pl.pallas_call(
        _hy_filter_body,
        grid=(length // tm,),
        in_specs=[pl.BlockSpec((tm, LANE), lambda i: (i, 0))] + [full(a) for a in args[1:]],
        out_specs=[pl.BlockSpec((tm, ow), lambda i: (i, 0)), pl.BlockSpec((tm, ow), lambda i: (i, 0)),
                   pl.BlockSpec((1, ow), lambda i: (0, 0))],
        out_shape=[jax.ShapeDtypeStruct((length, ow), f32), jax.ShapeDtypeStruct((length, ow), f32),
                   jax.ShapeDtypeStruct((1, ow), f32)],
        compiler_params=pltpu.CompilerParams(dimension_semantics=("arbitrary",)),
        name="hy_filter_time",
    )(*args)


def _hy_s1_body(x_ref, f_ref, re_ref, im_ref, *, width):
    for i in range(HY_NB):
        cols = slice(i * width, (i + 1) * width)
        acc = jnp.dot(f_ref[i], x_ref[:, cols].astype(jnp.bfloat16), preferred_element_type=jnp.float32)
        re_ref[:, cols] = acc[:HY_N1].astype(re_ref.dtype)
        im_ref[:, cols] = acc[HY_N1:].astype(im_ref.dtype)


def _hy_s1(x2d, fwd, width):
    half, m = x2d.shape
    blk = HY_NB * width
    out = jax.ShapeDtypeStruct((HY_N1, m), jnp.bfloat16)
    return pl.pallas_call(
        functools.partial(_hy_s1_body, width=width),
        grid=(m // blk,),
        in_specs=[pl.BlockSpec((half, blk), lambda j: (0, j)),
                  pl.BlockSpec((HY_NB, 2 * HY_N1, half), lambda j: (j, 0, 0))],
        out_specs=[pl.BlockSpec((HY_N1, blk), lambda j: (0, j)), pl.BlockSpec((HY_N1, blk), lambda j: (0, j))],
        out_shape=[out, out],
        name="hy_stage1",
    )(x2d, fwd)


def _hy_s2f_body(er_ref, ei_ref, or_ref, oi_ref, c_ref, s_ref, sc_ref, hr_ref, hi_ref):
    c, s = c_ref[...], s_ref[...]
    dot = lambda a, b: jnp.dot(a, b[...], preferred_element_type=jnp.float32)
    hr_ref[...] = ((dot(c, er_ref) - dot(s, ei_ref)) * sc_ref[...]).astype(hr_ref.dtype)
    hi_ref[...] = ((dot(c, oi_ref) + dot(s, or_ref)) * sc_ref[...]).astype(hi_ref.dtype)


def _hy_s2f(a_e, a_o, c2, s2, scale):
    width = scale.shape[1]
    slab = pl.BlockSpec((None, HY_N2, width), lambda k: (k, 0, 0))
    mat = pl.BlockSpec((HY_N2, HY_N2), lambda k: (0, 0))
    view = lambda a: a.reshape(HY_N1, HY_N2, width)
    out = jax.ShapeDtypeStruct((HY_N1, HY_N2, width), jnp.bfloat16)
    return pl.pallas_call(
        _hy_s2f_body,
        grid=(HY_N1,),
        in_specs=[slab, slab, slab, slab, mat, mat, pl.BlockSpec((1, width), lambda k: (0, 0))],
        out_specs=[slab, slab],
        out_shape=[out, out],
        name="hy_filter_stage2",
    )(view(a_e[0]), view(a_e[1]), view(a_o[0]), view(a_o[1]), c2, s2, scale)


def _hy_s2_body(ar_ref, ai_ref, hr_ref, hi_ref, cs_ref, br_ref, bi_ref):
    bf = jnp.bfloat16
    cs = cs_ref[...]
    u = jnp.dot(cs, ar_ref[...], preferred_element_type=jnp.float32)
    w = jnp.dot(cs, ai_ref[...], preferred_element_type=jnp.float32)
    xr = u[:HY_N2] - w[HY_N2:]
    xi = w[:HY_N2] + u[HY_N2:]
    hr = hr_ref[...].astype(jnp.float32)
    hi = hi_ref[...].astype(jnp.float32)
    yr = (xr * hr - xi * hi).astype(bf)
    yi = (xr * hi + xi * hr).astype(bf)
    u = jnp.dot(cs, yr, preferred_element_type=jnp.float32)
    w = jnp.dot(cs, yi, preferred_element_type=jnp.float32)
    br_ref[...] = (u[:HY_N2] + w[HY_N2:]).astype(br_ref.dtype)
    bi_ref[...] = (w[:HY_N2] - u[HY_N2:]).astype(bi_ref.dtype)


def _hy_s2(a_re, a_im, h_re, h_im, cs2, order):
    slab = pl.BlockSpec((None, HY_N2, HY_W), lambda k: (k, 0, 0))
    hslab = pl.BlockSpec((None, HY_N2, HY_W), lambda k: (k, 0, order))
    view = lambda a: a.reshape(HY_N1, HY_N2, HY_W)
    out = jax.ShapeDtypeStruct((HY_N1, HY_N2, HY_W), jnp.bfloat16)
    return pl.pallas_call(
        _hy_s2_body,
        grid=(HY_N1,),
        in_specs=[slab, slab, hslab, hslab, pl.BlockSpec((2 * HY_N2, HY_N2), lambda k: (0, 0))],
        out_specs=[slab, slab],
        out_shape=[out, out],
        name="hy_stage2",
    )(view(a_re), view(a_im), h_re, h_im, cs2)


def _hy_is1_body(br_ref, bi_ref, g_ref, z_ref, gate_ref, bias_ref, o_ref, *, width):
    for i in range(HY_NB):
        cols = slice(i * width, (i + 1) * width)
        g = g_ref[i]
        conv = (jnp.dot(g[:, :HY_N1], br_ref[:, cols], preferred_element_type=jnp.float32)
                + jnp.dot(g[:, HY_N1:], bi_ref[:, cols], preferred_element_type=jnp.float32))
        o_ref[:, cols] = gate_ref[:, cols] * (conv + z_ref[:, cols] * bias_ref[...])


def _hy_is1(b_re, b_im, inv, z2d, gate2d, bias):
    half, m = z2d.shape
    width = bias.shape[1]
    blk = HY_NB * width
    spec = pl.BlockSpec((HY_N1, blk), lambda j: (0, j))
    real = pl.BlockSpec((half, blk), lambda j: (0, j))
    return pl.pallas_call(
        functools.partial(_hy_is1_body, width=width),
        grid=(m // blk,),
        in_specs=[spec, spec, pl.BlockSpec((HY_NB, half, 2 * HY_N1), lambda j: (j, 0, 0)), real, real,
                  pl.BlockSpec((1, width), lambda j: (0, 0))],
        out_specs=real,
        out_shape=jax.ShapeDtypeStruct((half, m), jnp.float32),
        name="hy_inv_stage1",
    )(b_re.reshape(HY_N1, m), b_im.reshape(HY_N1, m), inv, z2d, gate2d, bias)


def hyena_long(p_hy, f_w1, f_b1, f_w2, f_b2, f_w3, f_b3, f_w4, f_freq, f_bias, conv_w, conv_b):
    length = p_hy.shape[0]
    bf = jnp.bfloat16
    half = HY_N1 // 2
    fwd, inv, c2, s2 = _hy_dft_constants(length)
    fwd, inv = jnp.asarray(fwd, bf), jnp.asarray(inv, bf)
    cs2 = jnp.asarray(jnp.concatenate([c2, s2], axis=0), bf)
    e, od, nrm = _hy_filter_time(length, f_w1, f_b1, f_w2, f_b2, f_w3, f_b3, f_w4, f_freq, HY_TM)
    ow = HY_ORDER * HY_W
    a_e = _hy_s1(e.reshape(half, HY_N2 * ow), fwd, ow)
    a_o = _hy_s1(od.reshape(half, HY_N2 * ow), fwd, ow)
    scale = 1.0 / (nrm * (2.0 * length))
    h_re, h_im = _hy_s2f(a_e, a_o, jnp.asarray(c2, bf), jnp.asarray(s2, bf), scale)
    u = _hy_conv(p_hy, conv_w, conv_b, HY_TM).reshape(3, half, HY_N2 * HY_W)
    z = u[0]
    for order in range(HY_ORDER):
        a_re, a_im = _hy_s1(z, fwd, HY_W)
        b_re, b_im = _hy_s2(a_re, a_im, h_re, h_im, cs2, order)
        z = _hy_is1(b_re, b_im, inv, z, u[1 + order], f_bias[order].astype(jnp.float32).reshape(1, HY_W))
    return z.reshape(length, HY_W)


def even_mixer(h, hc, w_in, conv_w, conv_b, f_w1, f_b1, f_w2, f_b2, f_w3, f_b3, f_w4, f_freq, f_bias,
               q_norm, k_norm, sink, ctx_out):
    f32 = jnp.float32
    bsz, length, _ = h.shape
    n_ctx = hc.shape[1]
    filt = (f_w1, f_b1, f_w2, f_b2, f_w3, f_b3, f_w4, f_freq)
    o_v = ATT_KV_W
    o_q = 2 * ATT_KV_W
    o_hy = o_q + ATT_Q_W
    w_perm = jnp.concatenate([w_in[:, o_hy:], w_in[:, o_q:o_hy], w_in[:, :o_q]], axis=1)
    hy_w = 3 * HY_W
    p_hy = pmm3(h, w_perm).astype(f32)
    p = jnp.concatenate([p_hy[..., hy_w + ATT_Q_W:], p_hy[..., hy_w:hy_w + ATT_Q_W]], axis=-1)
    pc = pmm3(hc, w_in if ctx_out else w_in[:, :o_q]).astype(f32)

    def heads(a, n_heads):
        return a.reshape(a.shape[0], a.shape[1], n_heads, HEAD_DIM)

    k_c = rms_norm(heads(pc[..., :o_v], ATT_KV_HEADS), k_norm)
    v_c = heads(pc[..., o_v:o_q], ATT_KV_HEADS)
    rope = axial_rope_tables(length)
    k = apply_axial_rope(rms_norm(heads(p[..., :o_v], ATT_KV_HEADS), k_norm), rope)
    v = heads(p[..., o_v:o_q], ATT_KV_HEADS)
    q = apply_axial_rope(rms_norm(heads(p[..., o_q:o_hy], ATT_HEADS), q_norm), rope)
    att = window_attention(q.reshape(bsz, length, ATT_KV_HEADS, ATT_GROUP, HEAD_DIM), k, v, k_c, v_c, sink)
    hy = hyena_long(p_hy[0], *filt, f_bias, conv_w, conv_b)[None]
    out = jnp.concatenate([hy, att], axis=-1).astype(h.dtype)
    if not ctx_out:
        return out, None
    q_c = rms_norm(heads(pc[..., o_q:o_hy], ATT_HEADS), q_norm).reshape(bsz, n_ctx, ATT_KV_HEADS, ATT_GROUP, HEAD_DIM)
    att_c = context_attention(q_c, k_c, v_c, sink)
    hy_c = hyena_mix(pc[..., o_hy:], hyena_filters(n_ctx, *filt), f_bias, conv_w, conv_b)
    return out, jnp.concatenate([hy_c, att_c], axis=-1).astype(hc.dtype)


ATT_TQ = ATT_BLOCK
EVEN_TM = 256
EV_Q_BLK = 3 * HY_W // ATT_Q_W
EV_K_BLK = (3 * HY_W + ATT_Q_W) // ATT_KV_W
EV_V_BLK = EV_K_BLK + 1


def _rope_tables(length):
    cr, sr, cc, sc = axial_rope_tables(length)
    return jnp.concatenate([cr, cr, cc, cc], axis=-1), jnp.concatenate([-sr, sr, -sc, sc], axis=-1)


def _head_norm_rope(x, g_row, c, s, seg):
    sq = x * x
    hi = sq.astype(jnp.bfloat16)
    lo = (sq - hi.astype(jnp.float32)).astype(jnp.bfloat16)
    ms = (jnp.dot(hi, seg, preferred_element_type=jnp.float32) + jnp.dot(lo, seg, preferred_element_type=jnp.float32))
    y = x * lax.rsqrt(ms + EPS) * g_row
    width = x.shape[1]
    quarter = HEAD_DIM // 4
    lane = lax.broadcasted_iota(jnp.int32, x.shape, 1)
    partner = jnp.where((lane & quarter) == 0, pltpu.roll(y, width - quarter, 1), pltpu.roll(y, quarter, 1))
    return y * c + partner * s


def _qk_prep_body(q_ref, k_ref, c_ref, s_ref, qn_ref, kn_ref, segq_ref, segk_ref, qo_ref, ko_ref):
    c, s = c_ref[...], s_ref[...]
    tile = lambda t, n: jnp.concatenate([t] * n, axis=1)
    qo_ref[...] = _head_norm_rope(q_ref[...], tile(qn_ref[...], ATT_HEADS), tile(c, ATT_HEADS), tile(s, ATT_HEADS),
                                  segq_ref[...]).astype(qo_ref.dtype)
    ko_ref[...] = _head_norm_rope(k_ref[...], tile(kn_ref[...], ATT_KV_HEADS), tile(c, ATT_KV_HEADS),
                                  tile(s, ATT_KV_HEADS), segk_ref[...]).astype(ko_ref.dtype)


def _qk_prep(p, rope_c, rope_s, q_norm, k_norm, tm):
    import numpy as np
    rows = p.shape[0]
    seg = lambda w: jnp.asarray(np.kron(np.eye(w // HEAD_DIM), np.full((HEAD_DIM, HEAD_DIM), 1.0 / HEAD_DIM)),
                                jnp.bfloat16)
    const = lambda shape: pl.BlockSpec(shape, lambda i: (0, 0))
    return pl.pallas_call(
        _qk_prep_body,
        grid=(rows // tm,),
        in_specs=[pl.BlockSpec((tm, ATT_Q_W), lambda i: (i, EV_Q_BLK)),
                  pl.BlockSpec((tm, ATT_KV_W), lambda i: (i, EV_K_BLK)),
                  pl.BlockSpec((tm, HEAD_DIM), lambda i: (i, 0)), pl.BlockSpec((tm, HEAD_DIM), lambda i: (i, 0)),
                  const((1, HEAD_DIM)), const((1, HEAD_DIM)),
                  const((ATT_Q_W, ATT_Q_W)), const((ATT_KV_W, ATT_KV_W))],
        out_specs=[pl.BlockSpec((tm, ATT_Q_W), lambda i: (i, 0)), pl.BlockSpec((tm, ATT_KV_W), lambda i: (i, 0))],
        out_shape=[jax.ShapeDtypeStruct((rows, ATT_Q_W), jnp.bfloat16),
                   jax.ShapeDtypeStruct((rows, ATT_KV_W), jnp.bfloat16)],
        name="qk_prep",
    )(p, p, rope_c, rope_s, q_norm.astype(jnp.float32).reshape(1, -1), k_norm.astype(jnp.float32).reshape(1, -1),
      seg(ATT_Q_W), seg(ATT_KV_W))


def _att_body(sink_ref, q_ref, kc_ref, vc_ref, *rest, local):
    if local:
        kp_ref, k0_ref, kn_ref, vp_ref, v0_ref, vn_ref, o_ref = rest
    else:
        (o_ref,) = rest
    b = pl.program_id(0)
    nb = pl.num_programs(0)
    bf = jnp.bfloat16
    scale = HEAD_DIM ** -0.5
    q = q_ref[...]
    kc = kc_ref[...]
    vc = vc_ref[...].astype(bf)
    if local:
        kb = jnp.concatenate([kp_ref[...], k0_ref[...], kn_ref[...]], axis=0)
        vb = jnp.concatenate([vp_ref[...], v0_ref[...], vn_ref[...]], axis=0).astype(bf)
        i = lax.broadcasted_iota(jnp.int32, (ATT_TQ, 3 * ATT_BLOCK), 0)
        j = lax.broadcasted_iota(jnp.int32, (ATT_TQ, 3 * ATT_BLOCK), 1)
        rel = j - ATT_BLOCK - i
        valid = ((jnp.abs(rel) <= ATT_WINDOW) & ((b > 0) | (j >= ATT_BLOCK))
                 & ((b < nb - 1) | (j < 2 * ATT_BLOCK)))
    for h in range(ATT_HEADS):
        hk = h // ATT_GROUP
        qh = q[:, h * HEAD_DIM:(h + 1) * HEAD_DIM]
        kv_cols = slice(hk * HEAD_DIM, (hk + 1) * HEAD_DIM)
        sink = sink_ref[h]
        s_ctx = _dot_nt(qh, kc[:, kv_cols]) * scale
        m = jnp.maximum(jnp.max(s_ctx, axis=-1, keepdims=True), sink)
        if local:
            s_loc = jnp.where(valid, _dot_nt(qh, kb[:, kv_cols]) * scale, -jnp.inf)
            m = jnp.maximum(m, jnp.max(s_loc, axis=-1, keepdims=True))
        p_ctx = jnp.exp(s_ctx - m)
        den = jnp.sum(p_ctx, axis=-1, keepdims=True) + jnp.exp(sink - m)
        acc = jnp.dot(p_ctx.astype(bf), vc[:, kv_cols], preferred_element_type=jnp.float32)
        if local:
            p_loc = jnp.exp(s_loc - m)
            den = den + jnp.sum(p_loc, axis=-1, keepdims=True)
            acc = acc + jnp.dot(p_loc.astype(bf), vb[:, kv_cols], preferred_element_type=jnp.float32)
        o_ref[:, h * HEAD_DIM:(h + 1) * HEAD_DIM] = acc / den


def _attention(qn, kn, p, kcn, pc, sink, local):
    rows = qn.shape[0]
    nb = rows // ATT_TQ
    n_ctx = kcn.shape[0]
    specs = [pl.BlockSpec((ATT_TQ, ATT_Q_W), lambda b, s: (b, 0)),
             pl.BlockSpec((n_ctx, ATT_KV_W), lambda b, s: (0, 0)),
             pl.BlockSpec((n_ctx, ATT_KV_W), lambda b, s: (0, EV_V_BLK))]
    args = [qn, kcn, pc]
    if local:
        prev = lambda b, s: jnp.maximum(b - 1, 0)
        nxt = lambda b, s: jnp.minimum(b + 1, nb - 1)
        for col, arr in ((0, kn), (EV_V_BLK, p)):
            specs += [pl.BlockSpec((ATT_BLOCK, ATT_KV_W), lambda b, s, col=col: (prev(b, s), col)),
                      pl.BlockSpec((ATT_BLOCK, ATT_KV_W), lambda b, s, col=col: (b, col)),
                      pl.BlockSpec((ATT_BLOCK, ATT_KV_W), lambda b, s, col=col: (nxt(b, s), col))]
            args += [arr, arr, arr]
    return pl.pallas_call(
        functools.partial(_att_body, local=local),
        grid_spec=pltpu.PrefetchScalarGridSpec(
            num_scalar_prefetch=1, grid=(nb,), in_specs=specs,
            out_specs=pl.BlockSpec((ATT_TQ, ATT_Q_W), lambda b, s: (b, 0))),
        out_shape=jax.ShapeDtypeStruct((rows, ATT_Q_W), jnp.float32),
        name="window_attention" if local else "context_attention",
    )(sink.astype(jnp.float32), *args)


def _even_merge_body(hy_ref, att_ref, x_ref, gt_ref, w_ref, o_ref):
    bf = jnp.bfloat16
    m = (jnp.dot(hy_ref[...].astype(bf), w_ref[:HY_W, :].astype(bf), preferred_element_type=jnp.float32)
         + jnp.dot(att_ref[...].astype(bf), w_ref[HY_W:, :].astype(bf), preferred_element_type=jnp.float32))
    o_ref[...] = x_ref[...] + gt_ref[...] * m


def _even_merge(hy, att, x2d, gt, w_out, tm):
    rows, d = x2d.shape
    half = pl.BlockSpec((tm, HY_W), lambda i: (i, 0))
    return pl.pallas_call(
        _even_merge_body,
        grid=(rows // tm,),
        in_specs=[half, half, pl.BlockSpec((tm, d), lambda i: (i, 0)), pl.BlockSpec((1, d), lambda i: (0, 0)),
                  pl.BlockSpec((MIX_W, d), lambda i: (0, 0))],
        out_specs=pl.BlockSpec((tm, d), lambda i: (i, 0)),
        out_shape=jax.ShapeDtypeStruct((rows, d), jnp.float32),
        name="even_merge",
    )(hy, att, x2d, gt.reshape(1, -1), w_out)


def _hy_short_body(e_ref, o_ref, sc_ref, u_ref, bias_ref, cf_ref, sf_ref, out_ref):
    bf = jnp.bfloat16
    cf, sf = cf_ref[...], sf_ref[...]
    dot = lambda a, b: jnp.dot(a, b.astype(bf), preferred_element_type=jnp.float32)
    z = u_ref[0]
    for order in range(HY_ORDER):
        cols = slice(order * HY_W, (order + 1) * HY_W)
        h_re = dot(cf, e_ref[:, cols]) * sc_ref[:, cols]
        h_im = dot(sf, o_ref[:, cols]) * sc_ref[:, cols]
        x_re, x_im = dot(cf, z), dot(sf, z)
        y_re = (x_re * h_re - x_im * h_im).astype(bf)
        y_im = (x_re * h_im + x_im * h_re).astype(bf)
        conv = _dot_tn(cf, y_re) + _dot_tn(sf, y_im)
        z = u_ref[1 + order] * (conv + z * bias_ref[order:order + 1, :])
    out_ref[...] = z


def hyena_short(pc, f_w1, f_b1, f_w2, f_b2, f_w3, f_b3, f_w4, f_freq, f_bias, conv_w, conv_b):
    import numpy as np
    rows = pc.shape[0]
    e, od, nrm = _hy_filter_time(rows, f_w1, f_b1, f_w2, f_b2, f_w3, f_b3, f_w4, f_freq, rows)
    u = _hy_conv(pc, conv_w, conv_b, rows)
    ang = 2 * np.pi * np.outer(np.arange(2 * rows), np.arange(rows)) / (2 * rows)
    cf, sf = jnp.asarray(np.cos(ang), jnp.bfloat16), jnp.asarray(-np.sin(ang), jnp.bfloat16)
    scale = 1.0 / (nrm * (2.0 * rows))
    return pl.pallas_call(
        _hy_short_body,
        out_shape=jax.ShapeDtypeStruct((rows, HY_W), jnp.float32),
        name="hy_short",
    )(e, od, scale, u, f_bias.astype(jnp.float32), cf, sf)


def even_layer(x, xc, h, hc, w_in, conv_w, conv_b, f_w1, f_b1, f_w2, f_b2, f_w3, f_b3, f_w4, f_freq, f_bias,
               q_norm, k_norm, sink, gt, gt_c, w_out):
    f32 = jnp.float32
    length, n_ctx = h.shape[1], hc.shape[1]
    filt = (f_w1, f_b1, f_w2, f_b2, f_w3, f_b3, f_w4, f_freq, f_bias, conv_w, conv_b)
    o_q = 2 * ATT_KV_W
    o_hy = o_q + ATT_Q_W
    w_perm = jnp.concatenate([w_in[:, o_hy:], w_in[:, o_q:o_hy], w_in[:, :o_q]], axis=1)
    p = pmm(h[0], w_perm)
    pc = pmm(hc[0], w_perm)
    rope_c, rope_s = _rope_tables(length)
    qn, kn = _qk_prep(p, rope_c, rope_s, q_norm, k_norm, EVEN_TM)
    qcn, kcn = _qk_prep(pc, jnp.ones((n_ctx, HEAD_DIM), f32), jnp.zeros((n_ctx, HEAD_DIM), f32), q_norm, k_norm,
                        n_ctx)
    att = _attention(qn, kn, p, kcn, pc, sink, True)
    att_c = _attention(qcn, None, None, kcn, pc, sink, False)
    hy = hyena_long(p, *filt)
    hy_c = hyena_short(pc, *filt)
    x_new = _even_merge(hy, att, x[0], gt, w_out, EVEN_TM)
    xc_new = _even_merge(hy_c, att_c, xc[0], gt_c, w_out, n_ctx)
    return x_new[None], xc_new[None]


def odd_mixer(h, hc, lb, w_in, conv_w, conv_b, dt_bias, a_log, d_skip, ssd_norm, hg_norm, ctx_out):
    f32 = jnp.float32
    bsz, length, _ = h.shape
    n_ctx = hc.shape[1]
    p = pmm3(h, w_in).astype(f32)
    pc = pmm3(hc, w_in if ctx_out else w_in[:, :ODD_STATE_COLS]).astype(f32)
    o_dt = SSD_CONV_CH
    o_f = SSD_CONV_CH + 2 * SSD_HEADS
    o_i = o_f + 2 * HG_W
    o_z = ODD_STATE_COLS
    o_q = o_z + SSD_W
    o_g = o_q + HG_W
    gn = SSD_GROUPS * SSD_STATE

    def streams(pp):
        n = pp.shape[1]
        xbc = jax.nn.silu(dwconv_centred(pp[..., :SSD_CONV_CH], conv_w, conv_b))
        xs = xbc[..., :SSD_W].reshape(bsz, n, SSD_HEADS, SSD_HEAD_DIM)
        bm = xbc[..., SSD_W:SSD_W + gn].reshape(bsz, n, SSD_GROUPS, SSD_STATE)
        cm = xbc[..., SSD_W + gn:].reshape(bsz, n, SSD_GROUPS, SSD_STATE)
        dt_raw = pp[..., o_dt:o_f].reshape(bsz, n, 2, SSD_HEADS)
        f_raw = pp[..., o_f:o_i].reshape(bsz, n, 2, HG_HEADS, HG_EXPAND)
        iv = pp[..., o_i:o_i + HG_W].reshape(bsz, n, HG_HEADS, HG_VDIM)
        return xs, bm, cm, dt_raw, f_raw, iv

    xs, bm, cm, dt_raw, f_raw, iv = streams(p)
    xs_c, bm_c, cm_c, dt_raw_c, f_raw_c, iv_c = streams(pc)
    q = jax.nn.silu(p[..., o_q:o_g]).reshape(bsz, length, HG_HEADS, HG_EXPAND)
    q_c = jax.nn.silu(pc[..., o_q:o_g]).reshape(bsz, n_ctx, HG_HEADS, HG_EXPAND) if ctx_out else None
    lb = lb.astype(f32).reshape(HG_HEADS, HG_EXPAND)
    ssd0 = jnp.zeros((bsz, SSD_GROUPS, SSD_HEADS // SSD_GROUPS, SSD_HEAD_DIM, SSD_STATE), f32)
    hg0 = jnp.zeros((bsz, HG_HEADS, HG_EXPAND, HG_VDIM), f32)
    y_dirs, o_dirs, yc_dirs, oc_dirs = [], [], [], []
    for d in range(2):
        fl = (lambda a: jnp.flip(a, axis=1)) if d == 1 else (lambda a: a)
        a = -jnp.exp(a_log[d].astype(f32))
        dsk = d_skip[d].astype(f32)
        dtb = dt_bias[d].astype(f32)
        dt_l = jax.nn.softplus(dt_raw[:, :, d] + dtb)
        dt_c = jax.nn.softplus(dt_raw_c[:, :, d] + dtb)
        yc, s_ctx = ssd_scan(fl(xs_c), fl(dt_c), a, fl(bm_c), fl(cm_c), dsk, ssd0, ctx_out)
        yl, _ = ssd_scan(fl(xs), fl(dt_l), a, fl(bm), fl(cm), dsk, s_ctx, True)
        y_dirs.append(fl(yl))
        f_l = lb + (1.0 - lb) * jax.nn.sigmoid(f_raw[:, :, d])
        f_c = lb + (1.0 - lb) * jax.nn.sigmoid(f_raw_c[:, :, d])
        oc, s_hg = hgrn2_scan(fl(q_c) if ctx_out else None, fl(1.0 - f_c), fl(iv_c), fl(jnp.log(f_c)), hg0, ctx_out)
        ol, _ = hgrn2_scan(fl(q), fl(1.0 - f_l), fl(iv), fl(jnp.log(f_l)), s_hg, True)
        o_dirs.append(fl(ol))
        if ctx_out:
            yc_dirs.append(fl(yc))
            oc_dirs.append(fl(oc))

    def merge(yy, oo, pp, n):
        z = pp[..., o_z:o_q]
        g = pp[..., o_g:]
        ys = (yy.reshape(bsz, n, SSD_W) * jax.nn.silu(z)).reshape(bsz, n, SSD_GROUPS, SSD_W // SSD_GROUPS)
        ys = rms_norm(ys, ssd_norm.reshape(SSD_GROUPS, SSD_W // SSD_GROUPS)).reshape(bsz, n, SSD_W)
        hs = rms_norm(oo, hg_norm.reshape(HG_HEADS, HG_VDIM)).reshape(bsz, n, HG_W) * jax.nn.silu(g)
        return jnp.concatenate([ys, hs], axis=-1)

    out = merge(y_dirs[0] + y_dirs[1], o_dirs[0] + o_dirs[1], p, length).astype(h.dtype)
    if not ctx_out:
        return out, None
    out_c = merge(yc_dirs[0] + yc_dirs[1], oc_dirs[0] + oc_dirs[1], pc, n_ctx).astype(hc.dtype)
    return out, out_c


SCAN_Q = 128
SCAN_LEVELS = 7
ODD_COLS = SSD_CONV_CH + 8 * 512
HALO = SUBLANE


def _scan_constants():
    import numpy as np
    q = SCAN_Q
    d_hg, d_ssd, pairs, laters = [], [], [], []
    for direction in (0, 1):
        pos = np.arange(q) if direction == 0 else q - 1 - np.arange(q)
        pj, pt = pos[None, :], pos[:, None]
        top = pj <= pt
        end = pj > pt
        ones = np.ones((SUBLANE, q), bool)
        lv, pr, lt = [], [], []
        for level in range(SCAN_LEVELS):
            b = 2 ** level
            start = (pos // (2 * b)) * (2 * b)
            mid = (start + b)[:, None]
            later = pos >= start + b
            lv.append(np.where(later[:, None], (pj >= mid) & (pj <= pt), (pj > pt) & (pj < mid)))
            pr.append((start[:, None] == start[None, :]) & later[:, None] & ~later[None, :])
            lt.append(np.broadcast_to(later[:, None], (q, LANE)))
        pr.append(np.eye(q, dtype=bool))
        pr.append(top)
        d_hg.append(np.concatenate([top, end] + lv + [ones], axis=0))
        d_ssd.append(np.concatenate([top, end, ones], axis=0))
        pairs.append(np.stack(pr))
        laters.append(np.stack(lt))
    f = np.float32
    return (np.stack(d_hg).astype(f), np.stack(d_ssd).astype(f), np.stack(pairs).astype(f),
            np.stack(laters).astype(f))


def _split_dot(m_bf16, v):
    hi = v.astype(jnp.bfloat16)
    lo = (v - hi.astype(jnp.float32)).astype(jnp.bfloat16)
    return (jnp.dot(m_bf16, hi, preferred_element_type=jnp.float32)
            + jnp.dot(m_bf16, lo, preferred_element_type=jnp.float32))


def _dot_nt(a, b):
    return lax.dot_general(a, b, (((1,), (1,)), ((), ())), preferred_element_type=jnp.float32)


def _dot_tn(a, b):
    return lax.dot_general(a, b, (((0,), (0,)), ((), ())), preferred_element_type=jnp.float32)


def _softplus(x):
    return jnp.maximum(x, 0.0) + jnp.log1p(jnp.exp(-jnp.abs(x)))


def _scan_body(xbc_ref, prev_ref, next_ref, f_ref, iv_ref, q_ref, dt_ref,
               cw_ref, cb_ref, dtb_ref, a_ref, dsk_ref, lb_ref,
               dhg_ref, dssd_ref, pair_ref, later_ref, sel_ref,
               out_ref, s_ssd, s_hg, ydiag_ref, *, n_ctx_chunks, n_chunks):
    d = pl.program_id(0)
    j = pl.program_id(1)
    q_rows = SCAN_Q
    bf = jnp.bfloat16

    @pl.when(j == 0)
    def _():
        s_ssd[...] = jnp.zeros_like(s_ssd)
        s_hg[...] = jnp.zeros_like(s_hg)

    c = jnp.where(d == 0, j, jnp.where(j < n_ctx_chunks, n_ctx_chunks - 1 - j, n_chunks - 1 + n_ctx_chunks - j))
    first = (c == 0) | (c == n_ctx_chunks)
    last = (c == n_ctx_chunks - 1) | (c == n_chunks - 1)

    u = xbc_ref[...]
    row = lax.broadcasted_iota(jnp.int32, u.shape, 0)
    before = jnp.where(first, 0.0, prev_ref[HALO - 1:HALO, :])
    after = jnp.where(last, 0.0, next_ref[0:1, :])
    up = jnp.where(row == 0, before, pltpu.roll(u, 1, 0))
    un = jnp.where(row == q_rows - 1, after, pltpu.roll(u, q_rows - 1, 0))
    xbc = cw_ref[0:1, :] * up + cw_ref[1:2, :] * u + cw_ref[2:3, :] * un + cb_ref[...]
    xbc = xbc * jax.nn.sigmoid(xbc)
    x = xbc[:, :SSD_W]

    dt = _softplus(dt_ref[...] + dtb_ref[...])
    da = dt * a_ref[...]
    r = _split_dot(dssd_ref[...], da)
    cs, to_end, total = r[:q_rows], r[q_rows:2 * q_rows], r[2 * q_rows:2 * q_rows + 1]
    xdt = x * dt
    cs_hi = cs.astype(bf)
    cs_lo = (cs - cs_hi.astype(jnp.float32)).astype(bf)
    cs_rows = _dot_nt(sel_ref[...], cs_hi) + _dot_nt(sel_ref[...], cs_lo)
    l_mask = pair_ref[SCAN_LEVELS + 1]
    decay_in = jnp.exp(cs)
    w_end = (jnp.exp(to_end) * xdt).astype(bf)
    gn = SSD_GROUPS * SSD_STATE
    hpg = SSD_HEADS // SSD_GROUPS
    gw = hpg * SSD_HEAD_DIM
    for g in range(SSD_GROUPS):
        b_g = xbc[:, SSD_W + g * SSD_STATE:SSD_W + (g + 1) * SSD_STATE].astype(bf)
        c_g = xbc[:, SSD_W + gn + g * SSD_STATE:SSD_W + gn + (g + 1) * SSD_STATE].astype(bf)
        scores = _dot_nt(c_g, b_g)
        y_off = jnp.dot(c_g, s_ssd[g].astype(bf), preferred_element_type=jnp.float32)
        for hh in range(hpg):
            h = g * hpg + hh
            lo = h * SSD_HEAD_DIM
            diff = cs[:, lo:lo + 1] - cs_rows[h:h + 1, :]
            decay = jnp.exp(jnp.minimum(diff, 0.0)) * l_mask
            ydiag_ref[:, lo:lo + SSD_HEAD_DIM] = jnp.dot((scores * decay).astype(bf),
                                                         xdt[:, lo:lo + SSD_HEAD_DIM].astype(bf),
                                                         preferred_element_type=jnp.float32)
        cols = slice(g * gw, (g + 1) * gw)
        out_ref[:, cols] = (ydiag_ref[:, cols] + decay_in[:, cols] * y_off + dsk_ref[:, cols] * x[:, cols])
        s_ssd[g] = jnp.exp(total[:, cols]) * s_ssd[g] + _dot_tn(b_g, w_end[:, cols])

    lb = lb_ref[...]
    f = lb + (1.0 - lb) * jax.nn.sigmoid(f_ref[...])
    k_in = 1.0 - f
    qv = q_ref[...]
    qv = qv * jax.nn.sigmoid(qv)
    v_bf = iv_ref[...].astype(bf)
    e = jnp.exp(_split_dot(dhg_ref[...], jnp.log(f)))
    e_top, e_end = e[:q_rows], e[q_rows:2 * q_rows]
    e_tot = e[(2 + SCAN_LEVELS) * q_rows:(2 + SCAN_LEVELS) * q_rows + 1]
    for h in range(HG_HEADS):
        cols = slice(h * HG_EXPAND, (h + 1) * HG_EXPAND)
        q_h, k_h = qv[:, cols], k_in[:, cols]
        att = pair_ref[SCAN_LEVELS] * _dot_nt(q_h.astype(bf), k_h.astype(bf))
        for level in range(SCAN_LEVELS):
            e_l = e[(2 + level) * q_rows:(3 + level) * q_rows, cols]
            later = later_ref[level]
            q_l = (q_h * e_l * later).astype(bf)
            k_l = (k_h * e_l * (1.0 - later)).astype(bf)
            att = att + pair_ref[level] * _dot_nt(q_l, k_l)
        o = jnp.dot(att.astype(bf), v_bf[:, cols], preferred_element_type=jnp.float32)
        o = o + _dot_nt((q_h * e_top[:, cols]).astype(bf), s_hg[h].astype(bf))
        out_ref[:, SSD_W + h * HG_VDIM:SSD_W + (h + 1) * HG_VDIM] = o
        s_hg[h] = e_tot[:, cols] * s_hg[h] + _dot_tn(v_bf[:, cols], (k_h * e_end[:, cols]).astype(bf))


def _odd_scan(p_all, conv_w, conv_b, dtb, a_cols, dsk, lb, n_ctx):
    n_rows = p_all.shape[0]
    n_chunks = n_rows // SCAN_Q
    ncc = n_ctx // SCAN_Q
    d_hg, d_ssd, pairs, laters = _scan_constants()
    bf = jnp.bfloat16
    import numpy as np
    sel = np.zeros((LANE, SSD_W), np.float32)
    sel[np.arange(SSD_HEADS), np.arange(SSD_HEADS) * SSD_HEAD_DIM] = 1.0

    def chunk(d, j):
        return jnp.where(d == 0, j, jnp.where(j < ncc, ncc - 1 - j, n_chunks - 1 + ncc - j))

    per = SCAN_Q // HALO
    last_halo = n_rows // HALO - 1
    col512 = lambda blk: (lambda d, j: (chunk(d, j), blk))
    const2 = lambda shape: pl.BlockSpec(shape, lambda d, j: (0,) * len(shape))
    dirc = lambda shape: pl.BlockSpec((None,) + shape, lambda d, j: (d,) + (0,) * len(shape))
    body = functools.partial(_scan_body, n_ctx_chunks=ncc, n_chunks=n_chunks)
    return pl.pallas_call(
        body,
        grid=(2, n_chunks),
        in_specs=[
            pl.BlockSpec((SCAN_Q, SSD_CONV_CH), lambda d, j: (chunk(d, j), 0)),
            pl.BlockSpec((HALO, SSD_CONV_CH), lambda d, j: (jnp.maximum(chunk(d, j) * per - 1, 0), 0)),
            pl.BlockSpec((HALO, SSD_CONV_CH), lambda d, j: (jnp.minimum((chunk(d, j) + 1) * per, last_halo), 0)),
            pl.BlockSpec((SCAN_Q, 512), lambda d, j: (chunk(d, j), 2 + d)),
            pl.BlockSpec((SCAN_Q, 512), col512(4)),
            pl.BlockSpec((SCAN_Q, 512), col512(6)),
            pl.BlockSpec((SCAN_Q, 512), lambda d, j: (chunk(d, j), 8 + d)),
            const2((CONV_W, SSD_CONV_CH)), const2((1, SSD_CONV_CH)),
            dirc((1, SSD_W)), dirc((1, SSD_W)), dirc((1, SSD_W)), const2((1, HG_W)),
            dirc(d_hg.shape[1:]), dirc(d_ssd.shape[1:]), dirc(pairs.shape[1:]), dirc(laters.shape[1:]),
            const2((LANE, SSD_W)),
        ],
        out_specs=pl.BlockSpec((None, SCAN_Q, MIX_W), lambda d, j: (d, chunk(d, j), 0)),
        out_shape=jax.ShapeDtypeStruct((2, n_rows, MIX_W), jnp.float32),
        scratch_shapes=[pltpu.VMEM((SSD_GROUPS, SSD_STATE, SSD_W // SSD_GROUPS), jnp.float32),
                        pltpu.VMEM((HG_HEADS, HG_VDIM, HG_EXPAND), jnp.float32),
                        pltpu.VMEM((SCAN_Q, SSD_W), jnp.float32)],
        compiler_params=pltpu.CompilerParams(dimension_semantics=("arbitrary", "arbitrary"),
                                             vmem_limit_bytes=MOE_VMEM_LIMIT),
        name="odd_scan",
    )(p_all, p_all, p_all, p_all, p_all, p_all, p_all,
      conv_w, conv_b.reshape(1, -1), dtb, a_cols, dsk, lb.reshape(1, -1),
      jnp.asarray(d_hg, bf), jnp.asarray(d_ssd, bf), jnp.asarray(pairs), jnp.asarray(laters),
      jnp.asarray(sel, bf))


def _group_rms(v, width):
    parts = []
    for lo in range(0, v.shape[1], width):
        seg = v[:, lo:lo + width]
        parts.append(seg * lax.rsqrt(jnp.mean(seg * seg, axis=-1, keepdims=True) + EPS))
    return jnp.concatenate(parts, axis=1)


def _odd_merge_body(yo_ref, z_ref, g_ref, x_ref, sn_ref, hn_ref, gt_ref, w_ref, o_ref):
    yo = yo_ref[0] + yo_ref[1]
    z = z_ref[...]
    g = g_ref[...]
    ys = _group_rms(yo[:, :SSD_W] * (z * jax.nn.sigmoid(z)), SSD_W // SSD_GROUPS) * sn_ref[...]
    hs = _group_rms(yo[:, SSD_W:], HG_VDIM) * hn_ref[...] * (g * jax.nn.sigmoid(g))
    m = jnp.concatenate([ys, hs], axis=1).astype(jnp.bfloat16)
    o_ref[...] = x_ref[...] + gt_ref[...] * jnp.dot(m, w_ref[...].astype(jnp.bfloat16),
                                                    preferred_element_type=jnp.float32)


ODD_TM = 256


def _odd_merge(yo, p_all, x2d, ssd_norm, hg_norm, gt, w_out, n_ctx):
    n_lat, d = x2d.shape
    skip = n_ctx // ODD_TM
    return pl.pallas_call(
        _odd_merge_body,
        grid=(n_lat // ODD_TM,),
        in_specs=[pl.BlockSpec((2, ODD_TM, MIX_W), lambda i: (0, i + skip, 0)),
                  pl.BlockSpec((ODD_TM, 512), lambda i: (i + skip, 5)),
                  pl.BlockSpec((ODD_TM, 512), lambda i: (i + skip, 7)),
                  pl.BlockSpec((ODD_TM, d), lambda i: (i, 0)),
                  pl.BlockSpec((1, SSD_W), lambda i: (0, 0)),
                  pl.BlockSpec((1, HG_W), lambda i: (0, 0)),
                  pl.BlockSpec((1, d), lambda i: (0, 0)),
                  pl.BlockSpec((MIX_W, d), lambda i: (0, 0))],
        out_specs=pl.BlockSpec((ODD_TM, d), lambda i: (i, 0)),
        out_shape=jax.ShapeDtypeStruct((n_lat, d), jnp.float32),
        compiler_params=pltpu.CompilerParams(dimension_semantics=("arbitrary",)),
        name="odd_merge",
    )(yo, p_all, p_all, x2d, ssd_norm.reshape(1, -1), hg_norm.reshape(1, -1), gt.reshape(1, -1), w_out)


def odd_layer(x, h, hc, lb, w_in, conv_w, conv_b, dt_bias, a_log, d_skip, ssd_norm, hg_norm, gt, w_out):
    f32 = jnp.float32
    n_ctx = hc.shape[1]
    o_dt = SSD_CONV_CH
    o_f = o_dt + 2 * SSD_HEADS
    rep = lambda v: jnp.repeat(v, SSD_HEAD_DIM, axis=-1)
    w_perm = jnp.concatenate([w_in[:, :o_dt], w_in[:, o_f:],
                              rep(w_in[:, o_dt:o_dt + SSD_HEADS]), rep(w_in[:, o_dt + SSD_HEADS:o_f])], axis=1)
    h_all = jnp.concatenate([hc[0], h[0]], axis=0)
    p_all = pmm(h_all, w_perm)
    dtb = rep(dt_bias.astype(f32)).reshape(2, 1, SSD_W)
    a_cols = rep(-jnp.exp(a_log.astype(f32))).reshape(2, 1, SSD_W)
    dsk = rep(d_skip.astype(f32)).reshape(2, 1, SSD_W)
    yo = _odd_scan(p_all, conv_w, conv_b, dtb, a_cols, dsk, lb.astype(f32), n_ctx)
    return _odd_merge(yo, p_all, x[0], ssd_norm, hg_norm, gt, w_out, n_ctx)[None]


MOE_TM = 256
MOE_BM = 256
NEG_BIG = -1e30
MOE_VMEM_LIMIT = 52 * 1024 * 1024


def _route_body(t_ref, rw_ref, rb_ref, idx_ref, gate_ref, rank_ref, cnt_ref, run_ref):
    i = pl.program_id(0)

    @pl.when(i == 0)
    def _():
        run_ref[...] = jnp.zeros_like(run_ref)

    tm = t_ref.shape[0]
    logits = jnp.dot(t_ref[...].astype(jnp.bfloat16), rw_ref[...].astype(jnp.bfloat16),
                     preferred_element_type=jnp.float32) + rb_ref[...]
    lane = lax.broadcasted_iota(jnp.int32, (tm, LANE), 1)
    lane_f = lane.astype(jnp.float32)
    work = logits
    vals, sels, hots = [], [], []
    for _ in range(TOP_K):
        m = jnp.max(work, axis=-1, keepdims=True)
        sel = jnp.min(jnp.where(work == m, lane_f, float(LANE)), axis=-1, keepdims=True)
        hot = lane_f == sel
        vals.append(m)
        sels.append(sel.astype(jnp.int32))
        hots.append(hot)
        work = jnp.where(hot, -jnp.inf, work)
    exps = [jnp.exp(v - vals[0]) for v in vals]
    denom = exps[0] + exps[1] + exps[2] + exps[3]
    chosen = jnp.zeros((tm, LANE), jnp.float32)
    for hot in hots:
        chosen = chosen + hot.astype(jnp.float32)
    row = lax.broadcasted_iota(jnp.int32, (tm, tm), 0)
    col = lax.broadcasted_iota(jnp.int32, (tm, tm), 1)
    tri = (row > col).astype(jnp.bfloat16)
    before = jnp.dot(tri, chosen.astype(jnp.bfloat16), preferred_element_type=jnp.float32) + run_ref[0:1, :]
    idx_out = jnp.zeros((tm, LANE), jnp.int32)
    gate_out = jnp.zeros((tm, LANE), jnp.float32)
    rank_out = jnp.zeros((tm, LANE), jnp.int32)
    for k in range(TOP_K):
        rank_k = jnp.sum(jnp.where(hots[k], before, 0.0), axis=-1, keepdims=True).astype(jnp.int32)
        idx_out = jnp.where(lane == k, sels[k], idx_out)
        gate_out = jnp.where(lane == k, exps[k] / denom, gate_out)
        rank_out = jnp.where(lane == k, rank_k, rank_out)
    idx_ref[...] = idx_out
    gate_ref[...] = gate_out
    rank_ref[...] = rank_out
    run_new = run_ref[0:1, :] + jnp.sum(chosen, axis=0, keepdims=True)
    run_ref[...] = jnp.broadcast_to(run_new, run_ref.shape)
    cnt_ref[...] = jnp.broadcast_to(run_new, cnt_ref.shape)


def _moe_route(t, router_w, router_b):
    n_tok, d = t.shape
    rw = jnp.pad(router_w, ((0, 0), (0, LANE - N_EXPERTS)))
    rb = jnp.pad(router_b.astype(jnp.float32), (0, LANE - N_EXPERTS), constant_values=NEG_BIG).reshape(1, LANE)
    tile = pl.BlockSpec((MOE_TM, LANE), lambda i: (i, 0))
    idx, gate, rank, cnt = pl.pallas_call(
        _route_body,
        grid=(n_tok // MOE_TM,),
        in_specs=[pl.BlockSpec((MOE_TM, d), lambda i: (i, 0)),
                  pl.BlockSpec((d, LANE), lambda i: (0, 0)),
                  pl.BlockSpec((1, LANE), lambda i: (0, 0))],
        out_specs=[tile, tile, tile, pl.BlockSpec((SUBLANE, LANE), lambda i: (0, 0))],
        out_shape=[jax.ShapeDtypeStruct((n_tok, LANE), jnp.int32),
                   jax.ShapeDtypeStruct((n_tok, LANE), jnp.float32),
                   jax.ShapeDtypeStruct((n_tok, LANE), jnp.int32),
                   jax.ShapeDtypeStruct((SUBLANE, LANE), jnp.float32)],
        scratch_shapes=[pltpu.VMEM((SUBLANE, LANE), jnp.float32)],
        compiler_params=pltpu.CompilerParams(dimension_semantics=("arbitrary",)),
        name="moe_route",
    )(t, rw, rb)
    return idx[:, :TOP_K], gate, rank[:, :TOP_K], cnt[0, :N_EXPERTS].astype(jnp.int32)


def _row_copy(src_ref, src_row, dst_ref, dst_row, sem):
    return pltpu.make_async_copy(src_ref.at[pl.ds(src_row, 1)], dst_ref.at[pl.ds(dst_row, 1)], sem)


def _scatter_body(off_ref, pad_ref, dest_ref, t_ref, xs_ref, zero_ref, sem):
    i = pl.program_id(0)
    tm = t_ref.shape[0]

    @pl.when(i == 0)
    def _():
        zero_ref[...] = jnp.zeros_like(zero_ref)
        used = off_ref[N_EXPERTS - 1] + pad_ref[N_EXPERTS - 1]
        n_rows = xs_ref.shape[0]

        def zero_block(start):
            return pltpu.make_async_copy(zero_ref, xs_ref.at[pl.ds(pl.multiple_of(start, MOE_BM), MOE_BM)], sem)

        for e in range(N_EXPERTS):
            tail = n_rows - (e + 1) * MOE_BM

            @pl.when(pad_ref[e] > 0)
            def _():
                zero_block(off_ref[e] + pad_ref[e] - MOE_BM).start()

            @pl.when(tail >= used)
            def _():
                zero_block(tail).start()
        for e in range(N_EXPERTS):
            tail = n_rows - (e + 1) * MOE_BM

            @pl.when(pad_ref[e] > 0)
            def _():
                zero_block(0).wait()

            @pl.when(tail >= used)
            def _():
                zero_block(0).wait()

    def issue(t, carry):
        for k in range(TOP_K):
            _row_copy(t_ref, t, xs_ref, dest_ref[TOP_K * t + k], sem).start()
        return carry

    lax.fori_loop(0, tm, issue, 0)
    for _ in range(TOP_K):
        pltpu.make_async_copy(t_ref, xs_ref.at[pl.ds(0, tm)], sem).wait()


def _moe_scatter(t, dest_flat, off, padded, n_rows):
    n_tok, d = t.shape
    return pl.pallas_call(
        _scatter_body,
        grid_spec=pltpu.PrefetchScalarGridSpec(
            num_scalar_prefetch=2,
            grid=(n_tok // MOE_TM,),
            in_specs=[pl.BlockSpec((TOP_K * MOE_TM,), lambda i, off, pad: (i,), memory_space=pltpu.SMEM),
                      pl.BlockSpec((MOE_TM, d), lambda i, off, pad: (i, 0))],
            out_specs=pl.BlockSpec(memory_space=pl.ANY),
            scratch_shapes=[pltpu.VMEM((MOE_BM, d), jnp.float32), pltpu.SemaphoreType.DMA],
        ),
        out_shape=jax.ShapeDtypeStruct((n_rows, d), jnp.float32),
        compiler_params=pltpu.CompilerParams(dimension_semantics=("arbitrary",)),
        name="moe_scatter",
    )(off, padded, dest_flat, t)


def _expert_body(blk_e_ref, n_act_ref, x_ref, wgu_ref, bgu_ref, wdn_ref, bdn_ref, y_ref, wgu_bf, wdn_bf):
    i = pl.program_id(0)

    @pl.when(i < n_act_ref[0])
    def _():
        prev = blk_e_ref[jnp.maximum(i - 1, 0)]

        @pl.when((i == 0) | (blk_e_ref[i] != prev))
        def _():
            wgu_bf[...] = wgu_ref[...].astype(jnp.bfloat16)
            wdn_bf[...] = wdn_ref[...].astype(jnp.bfloat16)

        x = x_ref[...].astype(jnp.bfloat16)
        gu = jnp.dot(x, wgu_bf[...], preferred_element_type=jnp.float32) + bgu_ref[...]
        gate = jnp.minimum(gu[:, :D_EXPERT], SWIGLU_LIMIT)
        up = jnp.clip(gu[:, D_EXPERT:], -SWIGLU_LIMIT, SWIGLU_LIMIT)
        act = (up + 1.0) * gate * jax.nn.sigmoid(SWIGLU_ALPHA * gate)
        y_ref[...] = jnp.dot(act.astype(jnp.bfloat16), wdn_bf[...],
                             preferred_element_type=jnp.float32) + bdn_ref[...]

    @pl.when(i >= n_act_ref[0])
    def _():
        y_ref[...] = jnp.zeros_like(y_ref)


def _moe_experts(xs, blk_e, n_act, w_gu, b_gu, w_dn, b_dn):
    n_rows, d = xs.shape
    n_blk = n_rows // MOE_BM

    def blk(i, be, na):
        return jnp.minimum(i, na[0] - 1)

    return pl.pallas_call(
        _expert_body,
        grid_spec=pltpu.PrefetchScalarGridSpec(
            num_scalar_prefetch=2,
            grid=(n_blk,),
            in_specs=[pl.BlockSpec((MOE_BM, d), lambda i, be, na: (blk(i, be, na), 0)),
                      pl.BlockSpec((None, d, 2 * D_EXPERT), lambda i, be, na: (be[blk(i, be, na)], 0, 0)),
                      pl.BlockSpec((None, 1, 2 * D_EXPERT), lambda i, be, na: (be[blk(i, be, na)], 0, 0)),
                      pl.BlockSpec((None, D_EXPERT, d), lambda i, be, na: (be[blk(i, be, na)], 0, 0)),
                      pl.BlockSpec((None, 1, d), lambda i, be, na: (be[blk(i, be, na)], 0, 0))],
            out_specs=pl.BlockSpec((MOE_BM, d), lambda i, be, na: (i, 0)),
            scratch_shapes=[pltpu.VMEM((d, 2 * D_EXPERT), jnp.bfloat16),
                            pltpu.VMEM((D_EXPERT, d), jnp.bfloat16)],
        ),
        out_shape=jax.ShapeDtypeStruct((n_rows, d), jnp.float32),
        compiler_params=pltpu.CompilerParams(dimension_semantics=("arbitrary",),
                                             vmem_limit_bytes=MOE_VMEM_LIMIT),
        name="moe_experts",
    )(blk_e, n_act, xs, w_gu, b_gu.reshape(N_EXPERTS, 1, -1), w_dn, b_dn.reshape(N_EXPERTS, 1, -1))


def _combine_body(dest_ref, dest_nxt_ref, gate_ref, ys_ref, y_ref, buf_ref, sems):
    i = pl.program_id(0)
    n = pl.num_programs(0)
    tm = y_ref.shape[0]

    def fetch(d_ref, slot):
        def issue(t, carry):
            for k in range(TOP_K):
                _row_copy(ys_ref, d_ref[TOP_K * t + k], buf_ref.at[slot, k], t, sems.at[slot]).start()
            return carry
        lax.fori_loop(0, tm, issue, 0)

    @pl.when(i == 0)
    def _():
        fetch(dest_ref, 0)

    @pl.when(i + 1 < n)
    def _():
        fetch(dest_nxt_ref, (i + 1) % 2)

    slot = i % 2
    for k in range(TOP_K):
        pltpu.make_async_copy(ys_ref.at[pl.ds(0, tm)], buf_ref.at[slot, k], sems.at[slot]).wait()
    g = gate_ref[...]
    acc = g[:, 0:1] * buf_ref[slot, 0]
    for k in range(1, TOP_K):
        acc = acc + g[:, k:k + 1] * buf_ref[slot, k]
    y_ref[...] = acc


def _moe_combine(ys, dest_flat, gate, n_tok):
    d = ys.shape[1]
    n_tiles = n_tok // MOE_TM
    return pl.pallas_call(
        _combine_body,
        grid=(n_tiles,),
        in_specs=[pl.BlockSpec((TOP_K * MOE_TM,), lambda i: (i,), memory_space=pltpu.SMEM),
                  pl.BlockSpec((TOP_K * MOE_TM,), lambda i: (jnp.minimum(i + 1, n_tiles - 1),),
                               memory_space=pltpu.SMEM),
                  pl.BlockSpec((MOE_TM, LANE), lambda i: (i, 0)),
                  pl.BlockSpec(memory_space=pl.ANY)],
        out_specs=pl.BlockSpec((MOE_TM, d), lambda i: (i, 0)),
        out_shape=jax.ShapeDtypeStruct((n_tok, d), jnp.float32),
        scratch_shapes=[pltpu.VMEM((2, TOP_K, MOE_TM, d), jnp.float32), pltpu.SemaphoreType.DMA((2,))],
        compiler_params=pltpu.CompilerParams(dimension_semantics=("arbitrary",)),
        name="moe_combine",
    )(dest_flat, dest_flat, gate, ys)


def moe_ffn(t, router_w, router_b, w_gu, b_gu, w_dn, b_dn):
    n_tok, d = t.shape
    assert n_tok % MOE_TM == 0
    idx, gate, rank, counts = _moe_route(t, router_w, router_b)
    padded = (counts + MOE_BM - 1) // MOE_BM * MOE_BM
    pad_end = jnp.cumsum(padded)
    off = pad_end - padded
    experts = jnp.arange(N_EXPERTS, dtype=jnp.int32)
    dest = rank + jnp.sum(jnp.where(idx[..., None] == experts, off, 0), axis=-1)
    dest_flat = dest.reshape(-1).astype(jnp.int32)
    n_blk = -(-(n_tok * TOP_K) // MOE_BM) + N_EXPERTS
    blk_e = jnp.minimum(jnp.searchsorted(pad_end, jnp.arange(n_blk) * MOE_BM, side='right'),
                        N_EXPERTS - 1).astype(jnp.int32)
    n_act = (pad_end[-1:] // MOE_BM).astype(jnp.int32)
    xs = _moe_scatter(t, dest_flat, off.astype(jnp.int32), padded.astype(jnp.int32), n_blk * MOE_BM)
    ys = _moe_experts(xs, blk_e, n_act, w_gu, b_gu, w_dn, b_dn)
    return _moe_combine(ys, dest_flat, gate, n_tok)


def kernel(x, c, ctx, c_ctx, norm_g, ada_w, ada_b, w_out, w_in_even, hy_conv_w, hy_conv_b,
           hy_w1, hy_b1, hy_w2, hy_b2, hy_w3, hy_b3, hy_w4, hy_freq, hy_filter_bias,
           att_q_norm, att_k_norm, att_sink, w_in_odd, ssd_conv_w, ssd_conv_b, ssd_dt_bias,
           ssd_A_log, ssd_D, ssd_norm, hg_lower_bounds, hg_norm, router_w, router_b,
           moe_w_gu, moe_b_gu, moe_w_dn, moe_b_dn):
    lbs = jax.nn.softmax(hg_lower_bounds.astype(jnp.float32), axis=0)
    lbs = jnp.cumsum(lbs, axis=0) - lbs[0]
    xc = ctx
    for layer in range(DEPTH):
        ctx_out = layer < DEPTH - 1
        i = layer // 2
        sh, sc, gt = adaln(c, ada_w[layer], ada_b[layer], 0)
        sh_c, sc_c, gt_c = adaln(c_ctx, ada_w[layer], ada_b[layer], 0)
        h = modulate(rms_norm(x, norm_g[layer, 0]), sh[:, None], sc[:, None])
        hc = modulate(rms_norm(xc, norm_g[layer, 0]), sh_c, sc_c)
        if layer % 2 == 0 and ctx_out:
            x, xc = even_layer(x, xc, h, hc, w_in_even[i], hy_conv_w[i], hy_conv_b[i], hy_w1[i], hy_b1[i],
                               hy_w2[i], hy_b2[i], hy_w3[i], hy_b3[i], hy_w4[i], hy_freq[i],
                               hy_filter_bias[i], att_q_norm[i], att_k_norm[i], att_sink[i], gt, gt_c,
                               w_out[layer])
        elif layer % 2 == 0:
            m, m_c = even_mixer(h, hc, w_in_even[i], hy_conv_w[i], hy_conv_b[i], hy_w1[i], hy_b1[i],
                                hy_w2[i], hy_b2[i], hy_w3[i], hy_b3[i], hy_w4[i], hy_freq[i],
                                hy_filter_bias[i], att_q_norm[i], att_k_norm[i], att_sink[i], ctx_out)
            x = x + gt[:, None] * pmm3(m, w_out[layer])
        elif not ctx_out:
            x = odd_layer(x, h, hc, lbs[layer], w_in_odd[i], ssd_conv_w[i], ssd_conv_b[i], ssd_dt_bias[i],
                          ssd_A_log[i], ssd_D[i], ssd_norm[i], hg_norm[i], gt, w_out[layer])
        else:
            m, m_c = odd_mixer(h, hc, lbs[layer], w_in_odd[i], ssd_conv_w[i], ssd_conv_b[i], ssd_dt_bias[i],
                               ssd_A_log[i], ssd_D[i], ssd_norm[i], hg_norm[i], ctx_out)
            x = x + gt[:, None] * pmm3(m, w_out[layer])
            xc = xc + gt_c * pmm3(m_c, w_out[layer])
        sh, sc, gt = adaln(c, ada_w[layer], ada_b[layer], 1)
        h = modulate(rms_norm(x, norm_g[layer, 1]), sh[:, None], sc[:, None])
        moe_params = (router_w[layer], router_b[layer], moe_w_gu[layer], moe_b_gu[layer],
                      moe_w_dn[layer], moe_b_dn[layer])
        n_lat = x.shape[0] * x.shape[1]
        if ctx_out:
            sh_c, sc_c, gt_c = adaln(c_ctx, ada_w[layer], ada_b[layer], 1)
            hc = modulate(rms_norm(xc, norm_g[layer, 1]), sh_c, sc_c)
            y = moe_ffn(jnp.concatenate([h.reshape(n_lat, D_MODEL), hc.reshape(-1, D_MODEL)], axis=0), *moe_params)
            x = x + gt[:, None] * y[:n_lat].reshape(x.shape)
            xc = xc + gt_c * y[n_lat:].reshape(xc.shape)
        else:
            x = x + gt[:, None] * moe_ffn(h.reshape(n_lat, D_MODEL), *moe_params).reshape(x.shape)
    return x
```

```python
import functools
import math

import jax
import jax.numpy as jnp
from jax import lax
from jax.experimental import pallas as pl
from jax.experimental.pallas import tpu as pltpu

D_MODEL = 1024
DEPTH = 2
GRID_W = 64
MIX_W = D_MODEL
EPS = 1e-6
CONV_W = 3

HY_W = MIX_W // 2
HY_ORDER = 2
HY_EMB = 33
HY_FFN = 64
HY_TARGET = 1e-2
HY_SHORT_PCT = 0.3
HY_LONG_PCT = 1.5

HEAD_DIM = 64
ATT_HEADS = (MIX_W // 2) // HEAD_DIM
ATT_KV_HEADS = 2
ATT_GROUP = ATT_HEADS // ATT_KV_HEADS
ATT_WINDOW = 128
ATT_BLOCK = 128
ROPE_BASE = 10000.0
ATT_Q_W = ATT_HEADS * HEAD_DIM
ATT_KV_W = ATT_KV_HEADS * HEAD_DIM

SSD_W = MIX_W // 2
SSD_HEAD_DIM = 64
SSD_HEADS = SSD_W // SSD_HEAD_DIM
SSD_GROUPS = 2
SSD_STATE = 128
SSD_CHUNK = 128
SSD_CONV_CH = SSD_W + 2 * SSD_GROUPS * SSD_STATE

HG_W = MIX_W // 2
HG_EXPAND = 128
HG_HEADS = HG_W // HG_EXPAND
HG_VDIM = HG_W // HG_HEADS
HG_CHUNK = 64

N_EXPERTS = 32
TOP_K = 4
D_EXPERT = D_MODEL
SWIGLU_ALPHA = 1.702
SWIGLU_LIMIT = 7.0
MOE_BLOCK = 128

EVEN_IN = 2 * ATT_KV_W + ATT_Q_W + 3 * HY_W
ODD_STATE_COLS = SSD_CONV_CH + 2 * SSD_HEADS + 3 * HG_W
ODD_IN = ODD_STATE_COLS + SSD_W + 2 * HG_W

LANE = 128
SUBLANE = 8


def _mm_body(a_ref, b_ref, o_ref):
    a = a_ref[...].astype(jnp.bfloat16)
    b = b_ref[...].astype(jnp.bfloat16)
    o_ref[...] = jnp.dot(a, b, preferred_element_type=jnp.float32)


def _pick_tile(n, candidates):
    for c in candidates:
        if n % c == 0:
            return c
    return n


def pmm(a, b):
    m, k = a.shape
    n = b.shape[1]
    n_pad = -(-n // LANE) * LANE
    if n_pad != n:
        b = jnp.pad(b, ((0, 0), (0, n_pad - n)))
    m_pad = -(-m // SUBLANE) * SUBLANE
    if m_pad != m:
        a = jnp.pad(a, ((0, m_pad - m), (0, 0)))
    tm = _pick_tile(m_pad, (512, 256, 128, 64, 32, 16, 8))
    tn = _pick_tile(n_pad, (512, 384, 256, 128))
    out = pl.pallas_call(
        _mm_body,
        grid=(m_pad // tm, n_pad // tn),
        in_specs=[pl.BlockSpec((tm, k), lambda i, j: (i, 0)),
                  pl.BlockSpec((k, tn), lambda i, j: (0, j))],
        out_specs=pl.BlockSpec((tm, tn), lambda i, j: (i, j)),
        out_shape=jax.ShapeDtypeStruct((m_pad, n_pad), jnp.float32),
        name="dense_mm",
    )(a, b)
    return out[:m, :n]


PROJ_TM = 256
PROJ_VMEM_LIMIT = 56 * 1024 * 1024


def _norm_proj_body(x_ref, g_ref, sh_ref, sc_ref, w_ref, o_ref):
    x = x_ref[...]
    y = x * lax.rsqrt(jnp.mean(x * x, axis=-1, keepdims=True) + EPS) * g_ref[...]
    h = y * (1.0 + sc_ref[...]) + sh_ref[...]
    o_ref[...] = jnp.dot(h.astype(jnp.bfloat16), w_ref[...], preferred_element_type=jnp.float32)


def norm_proj(x2d, g, shift, scale, w_bf16, n_first=0):
    rows, d = x2d.shape
    n = w_bf16.shape[1]
    assert rows % PROJ_TM == 0 and n_first % PROJ_TM == 0 and n % LANE == 0
    first_tiles = n_first // PROJ_TM
    mod = pl.BlockSpec((None, 1, d), lambda i: (jnp.where(i < first_tiles, 0, 1), 0, 0))
    return pl.pallas_call(
        _norm_proj_body,
        grid=(rows // PROJ_TM,),
        in_specs=[pl.BlockSpec((PROJ_TM, d), lambda i: (i, 0)),
                  pl.BlockSpec((1, d), lambda i: (0, 0)), mod, mod,
                  pl.BlockSpec((d, n), lambda i: (0, 0))],
        out_specs=pl.BlockSpec((PROJ_TM, n), lambda i: (i, 0)),
        out_shape=jax.ShapeDtypeStruct((rows, n), jnp.float32),
        compiler_params=pltpu.CompilerParams(dimension_semantics=("arbitrary",),
                                             vmem_limit_bytes=PROJ_VMEM_LIMIT),
        name="norm_proj",
    )(x2d, g.astype(jnp.float32).reshape(1, d), shift.astype(jnp.float32).reshape(2, 1, d),
      scale.astype(jnp.float32).reshape(2, 1, d), w_bf16)


def pmm3(a, b):
    lead = a.shape[:-1]
    return pmm(a.reshape(-1, a.shape[-1]), b).reshape(*lead, b.shape[-1])


def rms_norm(x, g):
    xf = x.astype(jnp.float32)
    y = xf * lax.rsqrt(jnp.mean(xf * xf, axis=-1, keepdims=True) + EPS)
    return (y * g.astype(jnp.float32)).astype(x.dtype)


def modulate(h, shift, scale):
    return h * (1.0 + scale) + shift


def adaln(cond, w, b, j):
    lo, hi = 3 * j * D_MODEL, 3 * (j + 1) * D_MODEL
    m = jax.nn.silu(cond) @ w[:, lo:hi] + b[lo:hi]
    return jnp.split(m, 3, axis=-1)


def dwconv_centred(u, w, b):
    ch = u.shape[-1]
    y = lax.conv_general_dilated(u, w[:, None, :].astype(u.dtype), window_strides=(1,),
                                 padding=[(CONV_W // 2, CONV_W // 2)],
                                 dimension_numbers=('NWC', 'WIO', 'NWC'), feature_group_count=ch)
    return y + b.astype(u.dtype)


def axial_rope_tables(length):
    rows = length // GRID_W
    n_pairs = HEAD_DIM // 4
    inv = ROPE_BASE ** (-jnp.arange(n_pairs, dtype=jnp.float32) / n_pairs)
    row_ang = jnp.arange(rows, dtype=jnp.float32)[:, None] * inv
    col_ang = jnp.arange(GRID_W, dtype=jnp.float32)[:, None] * inv
    ang_r = jnp.broadcast_to(row_ang[:, None], (rows, GRID_W, n_pairs)).reshape(length, n_pairs)
    ang_c = jnp.broadcast_to(col_ang[None], (rows, GRID_W, n_pairs)).reshape(length, n_pairs)
    return jnp.cos(ang_r), jnp.sin(ang_r), jnp.cos(ang_c), jnp.sin(ang_c)


def _rotate(u, cos, sin):
    n = u.shape[-1] // 2
    u1, u2 = u[..., :n], u[..., n:]
    cos = cos[None, :, None, :]
    sin = sin[None, :, None, :]
    return jnp.concatenate([u1 * cos - u2 * sin, u1 * sin + u2 * cos], axis=-1)


def apply_axial_rope(u, tables):
    cr, sr, cc, sc = tables
    half = HEAD_DIM // 2
    return jnp.concatenate([_rotate(u[..., :half], cr, sr), _rotate(u[..., half:], cc, sc)], axis=-1)


def hyena_filters(length, w1, b1, w2, b2, w3, b3, w4, freq):
    f32 = jnp.float32
    t = jnp.linspace(0.0, 1.0, length, dtype=f32)[:, None]
    bands = (HY_EMB - 1) // 2
    w_ang = 2.0 * math.pi * jnp.arange(length, dtype=f32)[:, None] / length
    fr = jnp.linspace(1e-4, bands - 1, bands, dtype=f32)[None]
    z = jnp.concatenate([t, jnp.cos(fr * w_ang), -jnp.sin(fr * w_ang)], axis=-1)
    fq = freq.astype(f32)
    hdn = jnp.sin(fq * (z @ w1.astype(f32) + b1.astype(f32)))
    hdn = jnp.sin(fq * (hdn @ w2.astype(f32) + b2.astype(f32)))
    hdn = jnp.sin(fq * (hdn @ w3.astype(f32) + b3.astype(f32)))
    h = (hdn @ w4.astype(f32)).reshape(length, HY_ORDER, 2, HY_W)
    max_decay = math.log(HY_TARGET) / HY_SHORT_PCT
    min_decay = math.log(HY_TARGET) / HY_LONG_PCT
    deltas = jnp.abs(jnp.linspace(min_decay, max_decay, HY_W, dtype=f32))
    h = h * jnp.exp(-t * deltas)[:, None, None, :]
    h2 = jnp.concatenate([h[:, :, 0], jnp.zeros((1, HY_ORDER, HY_W), f32), h[:0:-1, :, 1]], axis=0)
    h2 = h2 / jnp.sum(jnp.abs(h2), axis=0, keepdims=True)
    return jnp.fft.rfft(h2, axis=0)


def hyena_mix(u, hf, filter_bias, conv_w, conv_b):
    length = u.shape[1]
    u = dwconv_centred(u.astype(jnp.float32), conv_w, conv_b)
    v, x1, x2 = jnp.split(u, 3, axis=-1)
    z = v
    for o, gate in enumerate((x1, x2)):
        zf = jnp.fft.rfft(z, n=2 * length, axis=1)
        zc = jnp.fft.irfft(zf * hf[None, :, o], n=2 * length, axis=1)[:, :length]
        z = gate * (zc + z * filter_bias[o].astype(jnp.float32))
    return z


def window_attention(q, k, v, k_c, v_c, sink):
    bsz, length = q.shape[:2]
    nb = length // ATT_BLOCK
    scale = HEAD_DIM ** -0.5
    qb = q.reshape(bsz, nb, ATT_BLOCK, ATT_KV_HEADS, ATT_GROUP, HEAD_DIM)
    pad = ((0, 0), (ATT_BLOCK, ATT_BLOCK), (0, 0), (0, 0))

    def band(a):
        ap = jnp.pad(a, pad).reshape(bsz, nb + 2, ATT_BLOCK, ATT_KV_HEADS, HEAD_DIM)
        return jnp.concatenate([ap[:, :-2], ap[:, 1:-1], ap[:, 2:]], axis=2)

    kw, vw = band(k), band(v)
    s_loc = jnp.einsum('bnqhgd,bnkhd->bnhgqk', qb, kw) * scale
    s_ctx = jnp.einsum('bnqhgd,bchd->bnhgqc', qb, k_c) * scale
    qpos = jnp.arange(nb)[:, None] * ATT_BLOCK + jnp.arange(ATT_BLOCK)[None]
    kpos = (jnp.arange(nb)[:, None] - 1) * ATT_BLOCK + jnp.arange(3 * ATT_BLOCK)[None]
    rel = kpos[:, None, :] - qpos[:, :, None]
    valid = (jnp.abs(rel) <= ATT_WINDOW) & (kpos[:, None, :] >= 0) & (kpos[:, None, :] < length)
    s_loc = jnp.where(valid[None, :, None, None], s_loc, -jnp.inf)
    sink_l = jnp.broadcast_to(sink.astype(jnp.float32).reshape(1, 1, ATT_KV_HEADS, ATT_GROUP, 1, 1),
                              s_loc.shape[:-1] + (1,))
    p = jax.nn.softmax(jnp.concatenate([s_loc, s_ctx, sink_l], axis=-1), axis=-1)
    n_loc = 3 * ATT_BLOCK
    n_ctx = k_c.shape[1]
    o = (jnp.einsum('bnhgqk,bnkhd->bnqhgd', p[..., :n_loc], vw)
         + jnp.einsum('bnhgqc,bchd->bnqhgd', p[..., n_loc:n_loc + n_ctx], v_c))
    return o.reshape(bsz, length, ATT_Q_W)


def context_attention(q_c, k_c, v_c, sink):
    bsz, n_ctx = q_c.shape[:2]
    s = jnp.einsum('bqhgd,bkhd->bhgqk', q_c, k_c) * HEAD_DIM ** -0.5
    sink_l = jnp.broadcast_to(sink.astype(jnp.float32).reshape(1, ATT_KV_HEADS, ATT_GROUP, 1, 1),
                              s.shape[:-1] + (1,))
    p = jax.nn.softmax(jnp.concatenate([s, sink_l], axis=-1), axis=-1)[..., :-1]
    return jnp.einsum('bhgqk,bkhd->bqhgd', p, v_c).reshape(bsz, n_ctx, ATT_Q_W)


def ssd_scan(x, dt, a, bm, cm, d_skip, init, need_y):
    bsz, length, n_heads, hd = x.shape
    nc = length // SSD_CHUNK
    hpg = n_heads // SSD_GROUPS
    da = (dt * a).reshape(bsz, nc, SSD_CHUNK, SSD_GROUPS, hpg)
    cs = jnp.cumsum(da, axis=2)
    xdt = (x * dt[..., None]).reshape(bsz, nc, SSD_CHUNK, SSD_GROUPS, hpg, hd)
    bc = bm.reshape(bsz, nc, SSD_CHUNK, SSD_GROUPS, SSD_STATE)
    cc = cm.reshape(bsz, nc, SSD_CHUNK, SSD_GROUPS, SSD_STATE)
    to_end = jnp.exp(cs[:, :, -1:] - cs)
    states = jnp.einsum('bcsgn,bcsgh,bcsghp->bcghpn', bc, to_end, xdt)
    chunk_decay = jnp.exp(cs[:, :, -1])

    def step(s, inp):
        st, dec = inp
        return s * dec[..., None, None] + st, s

    s_final, s_in = lax.scan(step, init, (jnp.moveaxis(states, 1, 0), jnp.moveaxis(chunk_decay, 1, 0)))
    if not need_y:
        return None, s_final
    s_in = jnp.moveaxis(s_in, 0, 1)
    cs_t = jnp.moveaxis(cs, 2, -1)
    diff = cs_t[..., :, None] - cs_t[..., None, :]
    lower = jnp.tril(jnp.ones((SSD_CHUNK, SSD_CHUNK), bool))
    decay = jnp.where(lower, jnp.exp(jnp.where(lower, diff, 0.0)), 0.0)
    scores = jnp.einsum('bclgn,bcsgn->bcgls', cc, bc)
    y_diag = jnp.einsum('bcgls,bcghls,bcsghp->bclghp', scores, decay, xdt)
    y_off = jnp.einsum('bclgn,bcghpn,bclgh->bclghp', cc, s_in, jnp.exp(cs))
    y = (y_diag + y_off).reshape(bsz, length, n_heads, hd) + d_skip[:, None] * x
    return y, s_final


def hgrn2_scan(q, k, v, g, init, need_o):
    bsz, length, n_heads, _ = k.shape
    nc = length // HG_CHUNK

    def chunks(a):
        return a.reshape(bsz, nc, HG_CHUNK, n_heads, a.shape[-1]).transpose(1, 0, 3, 2, 4)

    lower = jnp.tril(jnp.ones((HG_CHUNK, HG_CHUNK), bool))[:, :, None]

    def update(s, kc, vc, cum):
        last = cum[:, :, -1]
        return (s * jnp.exp(last)[..., None]
                + jnp.einsum('bhsk,bhsv->bhkv', kc * jnp.exp(last[:, :, None] - cum), vc))

    if not need_o:
        def step_state(s, inp):
            kc, vc, gc = inp
            return update(s, kc, vc, jnp.cumsum(gc, axis=2)), None
        s_final, _ = lax.scan(step_state, init, (chunks(k), chunks(v), chunks(g)))
        return None, s_final

    def step(s, inp):
        qc, kc, vc, gc = inp
        cum = jnp.cumsum(gc, axis=2)
        diff = cum[:, :, :, None, :] - cum[:, :, None, :, :]
        decay = jnp.where(lower, jnp.exp(jnp.where(lower, diff, 0.0)), 0.0)
        att = jnp.einsum('bhtk,bhsk,bhtsk->bhts', qc, kc, decay)
        o = (jnp.einsum('bhtk,bhkv->bhtv', qc * jnp.exp(cum), s)
             + jnp.einsum('bhts,bhsv->bhtv', att, vc))
        return update(s, kc, vc, cum), o

    s_final, o = lax.scan(step, init, (chunks(q), chunks(k), chunks(v), chunks(g)))
    return o.transpose(1, 0, 3, 2, 4).reshape(bsz, length, n_heads, v.shape[-1]), s_final


HY_N1 = 128
HY_N2 = 256
HY_NB = 8
HY_TM = 512


def _hy_dft_constants(length):
    import numpy as np
    n = 2 * length
    assert n == HY_N1 * HY_N2
    half = HY_N1 // 2
    k1 = np.arange(HY_N1)[:, None]
    n1 = np.arange(half)[None, :]
    n2 = np.arange(HY_N2)[:, None, None]
    ang = 2 * np.pi * (k1 * n1 / HY_N1)[None] + 2 * np.pi * n2 * k1[None] / n
    fwd = np.concatenate([np.cos(ang), -np.sin(ang)], axis=1)
    inv = np.concatenate([np.cos(ang), -np.sin(ang)], axis=1).transpose(0, 2, 1)
    a2 = 2 * np.pi * np.outer(np.arange(HY_N2), np.arange(HY_N2)) / HY_N2
    return fwd.astype(np.float32), inv.astype(np.float32), np.cos(a2).astype(np.float32), (-np.sin(a2)).astype(np.float32)


def _hy_conv_body(p_ref, prev_ref, next_ref, w_ref, b_ref, o_ref):
    i = pl.program_id(1)
    n = pl.num_programs(1)
    u = p_ref[...]
    rows = u.shape[0]
    row = lax.broadcasted_iota(jnp.int32, u.shape, 0)
    before = jnp.where(i == 0, 0.0, prev_ref[HALO - 1:HALO, :])
    after = jnp.where(i == n - 1, 0.0, next_ref[0:1, :])
    up = jnp.where(row == 0, before, pltpu.roll(u, 1, 0))
    un = jnp.where(row == rows - 1, after, pltpu.roll(u, rows - 1, 0))
    o_ref[...] = w_ref[0:1, :] * up + w_ref[1:2, :] * u + w_ref[2:3, :] * un + b_ref[...]


def _hy_conv(p, conv_w, conv_b, tm):
    length = p.shape[0]
    per = tm // HALO
    last = length // HALO - 1
    return pl.pallas_call(
        _hy_conv_body,
        grid=(3, length // tm),
        in_specs=[pl.BlockSpec((tm, HY_W), lambda c, i: (i, c)),
                  pl.BlockSpec((HALO, HY_W), lambda c, i: (jnp.maximum(i * per - 1, 0), c)),
                  pl.BlockSpec((HALO, HY_W), lambda c, i: (jnp.minimum((i + 1) * per, last), c)),
                  pl.BlockSpec((CONV_W, HY_W), lambda c, i: (0, c)),
                  pl.BlockSpec((1, HY_W), lambda c, i: (0, c))],
        out_specs=pl.BlockSpec((None, tm, HY_W), lambda c, i: (c, i, 0)),
        out_shape=jax.ShapeDtypeStruct((3, length, HY_W), jnp.float32),
        name="hy_conv",
    )(p, p, p, conv_w, conv_b.reshape(1, -1))


def _hy_filter_body(z_ref, w1_ref, b1_ref, w2_ref, b2_ref, w3_ref, b3_ref, w4_ref, fq_ref, dl_ref,
                    e_ref, o_ref, nrm_ref):
    i = pl.program_id(0)
    bf = jnp.bfloat16
    z = z_ref[...]
    fq = fq_ref[...]

    def layer(a, w_ref, b_ref):
        return jnp.sin(fq * (jnp.dot(a.astype(bf), w_ref[...].astype(bf), preferred_element_type=jnp.float32)
                             + b_ref[...]))

    hdn = layer(layer(layer(z, w1_ref, b1_ref), w2_ref, b2_ref), w3_ref, b3_ref)
    h = jnp.dot(hdn.astype(bf), w4_ref[...].astype(bf), preferred_element_type=jnp.float32)
    decay = jnp.exp(-z[:, 0:1] * dl_ref[...])
    first = (lax.broadcasted_iota(jnp.int32, decay.shape, 0) == 0) & (i == 0)
    acc = []
    for order in range(HY_ORDER):
        lo = order * 2 * HY_W
        h0 = h[:, lo:lo + HY_W] * decay
        h1 = jnp.where(first, 0.0, h[:, lo + HY_W:lo + 2 * HY_W] * decay)
        e_ref[:, order * HY_W:(order + 1) * HY_W] = h0 + h1
        o_ref[:, order * HY_W:(order + 1) * HY_W] = h0 - h1
        acc.append(jnp.sum(jnp.abs(h0) + jnp.abs(h1), axis=0, keepdims=True))
    part = jnp.concatenate(acc, axis=1)

    @pl.when(i == 0)
    def _():
        nrm_ref[...] = jnp.zeros_like(nrm_ref)

    nrm_ref[...] = nrm_ref[...] + part


def _hy_filter_time(length, w1, b1, w2, b2, w3, b3, w4, freq, tm):
    f32 = jnp.float32
    t = jnp.linspace(0.0, 1.0, length, dtype=f32)[:, None]
    bands = (HY_EMB - 1) // 2
    w_ang = 2.0 * math.pi * jnp.arange(length, dtype=f32)[:, None] / length
    fr = jnp.linspace(1e-4, bands - 1, bands, dtype=f32)[None]
    z = jnp.concatenate([t, jnp.cos(fr * w_ang), -jnp.sin(fr * w_ang)], axis=-1)
    z = jnp.pad(z, ((0, 0), (0, LANE - HY_EMB)))
    w1p = jnp.pad(w1.astype(f32), ((0, LANE - HY_EMB), (0, 0)))
    max_decay = math.log(HY_TARGET) / HY_SHORT_PCT
    min_decay = math.log(HY_TARGET) / HY_LONG_PCT
    deltas = jnp.abs(jnp.linspace(min_decay, max_decay, HY_W, dtype=f32)).reshape(1, HY_W)
    full = lambda a: pl.BlockSpec(a.shape, lambda i: (0,) * a.ndim)
    row = lambda v: v.astype(f32).reshape(1, -1)
    args = (z, w1p, row(b1), w2.astype(f32), row(b2), w3.astype(f32), row(b3), w4.astype(f32), row(freq), deltas)
    ow = HY_ORDER * HY_W
    return pl.pallas_call(
        _hy_filter_body,
        grid=(length // tm,),
        in_specs=[pl.BlockSpec((tm, LANE), lambda i: (i, 0))] + [full(a) for a in args[1:]],
        out_specs=[pl.BlockSpec((tm, ow), lambda i: (i, 0)), pl.BlockSpec((tm, ow), lambda i: (i, 0)),
                   pl.BlockSpec((1, ow), lambda i: (0, 0))],
        out_shape=[jax.ShapeDtypeStruct((length, ow), f32), jax.ShapeDtypeStruct((length, ow), f32),
                   jax.ShapeDtypeStruct((1, ow), f32)],
        compiler_params=pltpu.CompilerParams(dimension_semantics=("arbitrary",)),
        name="hy_filter_time",
    )(*args)


def _pack_complex(re, im):
    hi = lax.bitcast_convert_type(re.astype(jnp.bfloat16).astype(jnp.float32), jnp.uint32)
    lo = lax.bitcast_convert_type(im.astype(jnp.bfloat16).astype(jnp.float32), jnp.uint32)
    return hi | (lo >> 16)


def _unpack_complex(w):
    re = lax.bitcast_convert_type(w & jnp.uint32(0xFFFF0000), jnp.float32).astype(jnp.bfloat16)
    im = lax.bitcast_convert_type(w << 16, jnp.float32).astype(jnp.bfloat16)
    return re, im


def _hy_s1_body(x_ref, f_ref, a_ref):
    for i in range(HY_NB):
        acc = jnp.dot(f_ref[i], x_ref[:, i, :].astype(jnp.bfloat16), preferred_element_type=jnp.float32)
        a_ref[:, i, :] = _pack_complex(acc[:HY_N1], acc[HY_N1:])


def _hy_s1(x4, sel, fwd):
    _, half, _, width = x4.shape
    return pl.pallas_call(
        _hy_s1_body,
        grid=(HY_N2 // HY_NB,),
        in_specs=[pl.BlockSpec((None, half, HY_NB, width), lambda j: (sel, 0, j, 0)),
                  pl.BlockSpec((HY_NB, 2 * HY_N1, half), lambda j: (j, 0, 0))],
        out_specs=pl.BlockSpec((HY_N1, HY_NB, width), lambda j: (0, j, 0)),
        out_shape=jax.ShapeDtypeStruct((HY_N1, HY_N2, width), jnp.uint32),
        name="hy_stage1",
    )(x4, fwd)


def _hy_s2f_body(ae_ref, ao_ref, c_ref, s_ref, sc_ref, hr_ref, hi_ref):
    c, s = c_ref[...], s_ref[...]
    e_re, e_im = _unpack_complex(ae_ref[...])
    o_re, o_im = _unpack_complex(ao_ref[...])
    dot = lambda a, b: jnp.dot(a, b, preferred_element_type=jnp.float32)
    hr_ref[...] = ((dot(c, e_re) - dot(s, e_im)) * sc_ref[...]).astype(hr_ref.dtype)
    hi_ref[...] = ((dot(c, o_im) + dot(s, o_re)) * sc_ref[...]).astype(hi_ref.dtype)


def _hy_s2f(a_e, a_o, c2, s2, scale):
    width = scale.shape[1]
    slab = pl.BlockSpec((None, HY_N2, width), lambda k: (k, 0, 0))
    mat = pl.BlockSpec((HY_N2, HY_N2), lambda k: (0, 0))
    out = jax.ShapeDtypeStruct((HY_N1, HY_N2, width), jnp.bfloat16)
    return pl.pallas_call(
        _hy_s2f_body,
        grid=(HY_N1,),
        in_specs=[slab, slab, mat, mat, pl.BlockSpec((1, width), lambda k: (0, 0))],
        out_specs=[slab, slab],
        out_shape=[out, out],
        name="hy_filter_stage2",
    )(a_e, a_o, c2, s2, scale)


def _hy_s2_body(a_ref, hr_ref, hi_ref, cs_ref, b_ref):
    bf = jnp.bfloat16
    cs = cs_ref[...]
    a_re, a_im = _unpack_complex(a_ref[...])
    u = jnp.dot(cs, a_re, preferred_element_type=jnp.float32)
    w = jnp.dot(cs, a_im, preferred_element_type=jnp.float32)
    xr = u[:HY_N2] - w[HY_N2:]
    xi = w[:HY_N2] + u[HY_N2:]
    hr = hr_ref[...].astype(jnp.float32)
    hi = hi_ref[...].astype(jnp.float32)
    yr = (xr * hr - xi * hi).astype(bf)
    yi = (xr * hi + xi * hr).astype(bf)
    u = jnp.dot(cs, yr, preferred_element_type=jnp.float32)
    w = jnp.dot(cs, yi, preferred_element_type=jnp.float32)
    b_ref[...] = _pack_complex(u[:HY_N2] + w[HY_N2:], w[:HY_N2] - u[HY_N2:])


def _hy_s2(a, h_re, h_im, cs2, order):
    slab = pl.BlockSpec((None, HY_N2, HY_W), lambda k: (k, 0, 0))
    hslab = pl.BlockSpec((None, HY_N2, HY_W), lambda k: (k, 0, order))
    return pl.pallas_call(
        _hy_s2_body,
        grid=(HY_N1,),
        in_specs=[slab, hslab, hslab, pl.BlockSpec((2 * HY_N2, HY_N2), lambda k: (0, 0))],
        out_specs=slab,
        out_shape=jax.ShapeDtypeStruct((HY_N1, HY_N2, HY_W), jnp.uint32),
        name="hy_stage2",
    )(a, h_re, h_im, cs2)


def _hy_is1_body(b_ref, g_ref, z_ref, gate_ref, bias_ref, o_ref):
    for i in range(HY_NB):
        g = g_ref[i]
        b_re, b_im = _unpack_complex(b_ref[:, i, :])
        conv = (jnp.dot(g[:, :HY_N1], b_re, preferred_element_type=jnp.float32)
                + jnp.dot(g[:, HY_N1:], b_im, preferred_element_type=jnp.float32))
        o_ref[:, i, :] = gate_ref[:, i, :] * (conv + z_ref[:, i, :] * bias_ref[...])


def _hy_is1(b, inv, z4, z_sel, gate4, gate_sel, bias):
    _, half, _, width = z4.shape
    real = lambda sel: pl.BlockSpec((None, half, HY_NB, width), lambda j: (sel, 0, j, 0))
    return pl.pallas_call(
        _hy_is1_body,
        grid=(HY_N2 // HY_NB,),
        in_specs=[pl.BlockSpec((HY_N1, HY_NB, width), lambda j: (0, j, 0)),
                  pl.BlockSpec((HY_NB, half, 2 * HY_N1), lambda j: (j, 0, 0)),
                  real(z_sel), real(gate_sel), pl.BlockSpec((1, width), lambda j: (0, 0))],
        out_specs=pl.BlockSpec((half, HY_NB, width), lambda j: (0, j, 0)),
        out_shape=jax.ShapeDtypeStruct((half, HY_N2, width), jnp.float32),
        name="hy_inv_stage1",
    )(b, inv, z4, gate4, bias)


def hyena_long(p_hy, f_w1, f_b1, f_w2, f_b2, f_w3, f_b3, f_w4, f_freq, f_bias, conv_w, conv_b):
    length = p_hy.shape[0]
    bf = jnp.bfloat16
    half = HY_N1 // 2
    fwd, inv, c2, s2 = _hy_dft_constants(length)
    fwd, inv = jnp.asarray(fwd, bf), jnp.asarray(inv, bf)
    cs2 = jnp.asarray(jnp.concatenate([c2, s2], axis=0), bf)
    e, od, nrm = _hy_filter_time(length, f_w1, f_b1, f_w2, f_b2, f_w3, f_b3, f_w4, f_freq, HY_TM)
    ow = HY_ORDER * HY_W
    a_e = _hy_s1(e.reshape(1, half, HY_N2, ow), 0, fwd)
    a_o = _hy_s1(od.reshape(1, half, HY_N2, ow), 0, fwd)
    scale = 1.0 / (nrm * (2.0 * length))
    h_re, h_im = _hy_s2f(a_e, a_o, jnp.asarray(c2, bf), jnp.asarray(s2, bf), scale)
    u = _hy_conv(p_hy, conv_w, conv_b, HY_TM).reshape(3, half, HY_N2, HY_W)
    z, z_sel = u, 0
    for order in range(HY_ORDER):
        b = _hy_s2(_hy_s1(z, z_sel, fwd), h_re, h_im, cs2, order)
        z = _hy_is1(b, inv, z, z_sel, u, 1 + order, f_bias[order].astype(jnp.float32).reshape(1, HY_W))[None]
        z_sel = 0
    return z.reshape(length, HY_W)


def even_mixer(h, hc, w_in, conv_w, conv_b, f_w1, f_b1, f_w2, f_b2, f_w3, f_b3, f_w4, f_freq, f_bias,
               q_norm, k_norm, sink, ctx_out):
    f32 = jnp.float32
    bsz, length, _ = h.shape
    n_ctx = hc.shape[1]
    filt = (f_w1, f_b1, f_w2, f_b2, f_w3, f_b3, f_w4, f_freq)
    o_v = ATT_KV_W
    o_q = 2 * ATT_KV_W
    o_hy = o_q + ATT_Q_W
    w_perm = jnp.concatenate([w_in[:, o_hy:], w_in[:, o_q:o_hy], w_in[:, :o_q]], axis=1)
    hy_w = 3 * HY_W
    p_hy = pmm3(h, w_perm).astype(f32)
    p = jnp.concatenate([p_hy[..., hy_w + ATT_Q_W:], p_hy[..., hy_w:hy_w + ATT_Q_W]], axis=-1)
    pc = pmm3(hc, w_in if ctx_out else w_in[:, :o_q]).astype(f32)

    def heads(a, n_heads):
        return a.reshape(a.shape[0], a.shape[1], n_heads, HEAD_DIM)

    k_c = rms_norm(heads(pc[..., :o_v], ATT_KV_HEADS), k_norm)
    v_c = heads(pc[..., o_v:o_q], ATT_KV_HEADS)
    rope = axial_rope_tables(length)
    k = apply_axial_rope(rms_norm(heads(p[..., :o_v], ATT_KV_HEADS), k_norm), rope)
    v = heads(p[..., o_v:o_q], ATT_KV_HEADS)
    q = apply_axial_rope(rms_norm(heads(p[..., o_q:o_hy], ATT_HEADS), q_norm), rope)
    att = window_attention(q.reshape(bsz, length, ATT_KV_HEADS, ATT_GROUP, HEAD_DIM), k, v, k_c, v_c, sink)
    hy = hyena_long(p_hy[0], *filt, f_bias, conv_w, conv_b)[None]
    out = jnp.concatenate([hy, att], axis=-1).astype(h.dtype)
    if not ctx_out:
        return out, None
    q_c = rms_norm(heads(pc[..., o_q:o_hy], ATT_HEADS), q_norm).reshape(bsz, n_ctx, ATT_KV_HEADS, ATT_GROUP, HEAD_DIM)
    att_c = context_attention(q_c, k_c, v_c, sink)
    hy_c = hyena_mix(pc[..., o_hy:], hyena_filters(n_ctx, *filt), f_bias, conv_w, conv_b)
    return out, jnp.concatenate([hy_c, att_c], axis=-1).astype(hc.dtype)


ATT_TQ = ATT_BLOCK
EVEN_TM = 256
EV_Q_BLK = 3 * HY_W // ATT_Q_W
EV_K_BLK = (3 * HY_W + ATT_Q_W) // ATT_KV_W
EV_V_BLK = EV_K_BLK + 1


def _rope_tables(length):
    cr, sr, cc, sc = axial_rope_tables(length)
    return jnp.concatenate([cr, cr, cc, cc], axis=-1), jnp.concatenate([-sr, sr, -sc, sc], axis=-1)


def _head_norm_rope(x, g_row, c, s, seg):
    sq = x * x
    hi = sq.astype(jnp.bfloat16)
    lo = (sq - hi.astype(jnp.float32)).astype(jnp.bfloat16)
    ms = (jnp.dot(hi, seg, preferred_element_type=jnp.float32) + jnp.dot(lo, seg, preferred_element_type=jnp.float32))
    y = x * lax.rsqrt(ms + EPS) * g_row
    width = x.shape[1]
    quarter = HEAD_DIM // 4
    lane = lax.broadcasted_iota(jnp.int32, x.shape, 1)
    partner = jnp.where((lane & quarter) == 0, pltpu.roll(y, width - quarter, 1), pltpu.roll(y, quarter, 1))
    return y * c + partner * s


def _qk_prep_body(q_ref, k_ref, c_ref, s_ref, qn_ref, kn_ref, segq_ref, segk_ref, qo_ref, ko_ref):
    c, s = c_ref[...], s_ref[...]
    tile = lambda t, n: jnp.concatenate([t] * n, axis=1)
    qo_ref[...] = _head_norm_rope(q_ref[...], tile(qn_ref[...], ATT_HEADS), tile(c, ATT_HEADS), tile(s, ATT_HEADS),
                                  segq_ref[...]).astype(qo_ref.dtype)
    ko_ref[...] = _head_norm_rope(k_ref[...], tile(kn_ref[...], ATT_KV_HEADS), tile(c, ATT_KV_HEADS),
                                  tile(s, ATT_KV_HEADS), segk_ref[...]).astype(ko_ref.dtype)


def _qk_prep(p, rope_c, rope_s, q_norm, k_norm, tm):
    import numpy as np
    rows = p.shape[0]
    seg = lambda w: jnp.asarray(np.kron(np.eye(w // HEAD_DIM), np.full((HEAD_DIM, HEAD_DIM), 1.0 / HEAD_DIM)),
                                jnp.bfloat16)
    const = lambda shape: pl.BlockSpec(shape, lambda i: (0, 0))
    return pl.pallas_call(
        _qk_prep_body,
        grid=(rows // tm,),
        in_specs=[pl.BlockSpec((tm, ATT_Q_W), lambda i: (i, EV_Q_BLK)),
                  pl.BlockSpec((tm, ATT_KV_W), lambda i: (i, EV_K_BLK)),
                  pl.BlockSpec((tm, HEAD_DIM), lambda i: (i, 0)), pl.BlockSpec((tm, HEAD_DIM), lambda i: (i, 0)),
                  const((1, HEAD_DIM)), const((1, HEAD_DIM)),
                  const((ATT_Q_W, ATT_Q_W)), const((ATT_KV_W, ATT_KV_W))],
        out_specs=[pl.BlockSpec((tm, ATT_Q_W), lambda i: (i, 0)), pl.BlockSpec((tm, ATT_KV_W), lambda i: (i, 0))],
        out_shape=[jax.ShapeDtypeStruct((rows, ATT_Q_W), jnp.bfloat16),
                   jax.ShapeDtypeStruct((rows, ATT_KV_W), jnp.bfloat16)],
        name="qk_prep",
    )(p, p, rope_c, rope_s, q_norm.astype(jnp.float32).reshape(1, -1), k_norm.astype(jnp.float32).reshape(1, -1),
      seg(ATT_Q_W), seg(ATT_KV_W))


def _att_body(sink_ref, q_ref, kc_ref, vc_ref, *rest, local):
    if local:
        kp_ref, k0_ref, kn_ref, vp_ref, v0_ref, vn_ref, o_ref = rest
    else:
        (o_ref,) = rest
    b = pl.program_id(0)
    nb = pl.num_programs(0)
    bf = jnp.bfloat16
    scale = HEAD_DIM ** -0.5
    q = q_ref[...]
    kc = kc_ref[...]
    vc = vc_ref[...].astype(bf)
    if local:
        kb = jnp.concatenate([kp_ref[...], k0_ref[...], kn_ref[...]], axis=0)
        vb = jnp.concatenate([vp_ref[...], v0_ref[...], vn_ref[...]], axis=0).astype(bf)
        i = lax.broadcasted_iota(jnp.int32, (ATT_TQ, 3 * ATT_BLOCK), 0)
        j = lax.broadcasted_iota(jnp.int32, (ATT_TQ, 3 * ATT_BLOCK), 1)
        rel = j - ATT_BLOCK - i
        valid = ((jnp.abs(rel) <= ATT_WINDOW) & ((b > 0) | (j >= ATT_BLOCK))
                 & ((b < nb - 1) | (j < 2 * ATT_BLOCK)))
    for h in range(ATT_HEADS):
        hk = h // ATT_GROUP
        qh = q[:, h * HEAD_DIM:(h + 1) * HEAD_DIM]
        kv_cols = slice(hk * HEAD_DIM, (hk + 1) * HEAD_DIM)
        sink = sink_ref[h]
        s_ctx = _dot_nt(qh, kc[:, kv_cols]) * scale
        m = jnp.maximum(jnp.max(s_ctx, axis=-1, keepdims=True), sink)
        if local:
            s_loc = jnp.where(valid, _dot_nt(qh, kb[:, kv_cols]) * scale, -jnp.inf)
            m = jnp.maximum(m, jnp.max(s_loc, axis=-1, keepdims=True))
        p_ctx = jnp.exp(s_ctx - m)
        den = jnp.sum(p_ctx, axis=-1, keepdims=True) + jnp.exp(sink - m)
        acc = jnp.dot(p_ctx.astype(bf), vc[:, kv_cols], preferred_element_type=jnp.float32)
        if local:
            p_loc = jnp.exp(s_loc - m)
            den = den + jnp.sum(p_loc, axis=-1, keepdims=True)
            acc = acc + jnp.dot(p_loc.astype(bf), vb[:, kv_cols], preferred_element_type=jnp.float32)
        o_ref[:, h * HEAD_DIM:(h + 1) * HEAD_DIM] = acc / den


def _attention(qn, kn, p, kcn, pc, sink, local):
    rows = qn.shape[0]
    nb = rows // ATT_TQ
    n_ctx = kcn.shape[0]
    specs = [pl.BlockSpec((ATT_TQ, ATT_Q_W), lambda b, s: (b, 0)),
             pl.BlockSpec((n_ctx, ATT_KV_W), lambda b, s: (0, 0)),
             pl.BlockSpec((n_ctx, ATT_KV_W), lambda b, s: (0, EV_V_BLK))]
    args = [qn, kcn, pc]
    if local:
        prev = lambda b, s: jnp.maximum(b - 1, 0)
        nxt = lambda b, s: jnp.minimum(b + 1, nb - 1)
        for col, arr in ((0, kn), (EV_V_BLK, p)):
            specs += [pl.BlockSpec((ATT_BLOCK, ATT_KV_W), lambda b, s, col=col: (prev(b, s), col)),
                      pl.BlockSpec((ATT_BLOCK, ATT_KV_W), lambda b, s, col=col: (b, col)),
                      pl.BlockSpec((ATT_BLOCK, ATT_KV_W), lambda b, s, col=col: (nxt(b, s), col))]
            args += [arr, arr, arr]
    return pl.pallas_call(
        functools.partial(_att_body, local=local),
        grid_spec=pltpu.PrefetchScalarGridSpec(
            num_scalar_prefetch=1, grid=(nb,), in_specs=specs,
            out_specs=pl.BlockSpec((ATT_TQ, ATT_Q_W), lambda b, s: (b, 0))),
        out_shape=jax.ShapeDtypeStruct((rows, ATT_Q_W), jnp.float32),
        name="window_attention" if local else "context_attention",
    )(sink.astype(jnp.float32), *args)


def _even_merge_body(hy_ref, att_ref, x_ref, gt_ref, w_ref, o_ref):
    bf = jnp.bfloat16
    m = (jnp.dot(hy_ref[...].astype(bf), w_ref[:HY_W, :].astype(bf), preferred_element_type=jnp.float32)
         + jnp.dot(att_ref[...].astype(bf), w_ref[HY_W:, :].astype(bf), preferred_element_type=jnp.float32))
    o_ref[...] = x_ref[...] + gt_ref[...] * m


def _even_merge(hy, att, x2d, gt, w_out, tm):
    rows, d = x2d.shape
    half = pl.BlockSpec((tm, HY_W), lambda i: (i, 0))
    return pl.pallas_call(
        _even_merge_body,
        grid=(rows // tm,),
        in_specs=[half, half, pl.BlockSpec((tm, d), lambda i: (i, 0)), pl.BlockSpec((1, d), lambda i: (0, 0)),
                  pl.BlockSpec((MIX_W, d), lambda i: (0, 0))],
        out_specs=pl.BlockSpec((tm, d), lambda i: (i, 0)),
        out_shape=jax.ShapeDtypeStruct((rows, d), jnp.float32),
        name="even_merge",
    )(hy, att, x2d, gt.reshape(1, -1), w_out)


def _hy_short_body(e_ref, o_ref, sc_ref, u_ref, bias_ref, cf_ref, sf_ref, out_ref):
    bf = jnp.bfloat16
    cf, sf = cf_ref[...], sf_ref[...]
    dot = lambda a, b: jnp.dot(a, b.astype(bf), preferred_element_type=jnp.float32)
    z = u_ref[0]
    for order in range(HY_ORDER):
        cols = slice(order * HY_W, (order + 1) * HY_W)
        h_re = dot(cf, e_ref[:, cols]) * sc_ref[:, cols]
        h_im = dot(sf, o_ref[:, cols]) * sc_ref[:, cols]
        x_re, x_im = dot(cf, z), dot(sf, z)
        y_re = (x_re * h_re - x_im * h_im).astype(bf)
        y_im = (x_re * h_im + x_im * h_re).astype(bf)
        conv = _dot_tn(cf, y_re) + _dot_tn(sf, y_im)
        z = u_ref[1 + order] * (conv + z * bias_ref[order:order + 1, :])
    out_ref[...] = z


def hyena_short(pc, f_w1, f_b1, f_w2, f_b2, f_w3, f_b3, f_w4, f_freq, f_bias, conv_w, conv_b):
    import numpy as np
    rows = pc.shape[0]
    e, od, nrm = _hy_filter_time(rows, f_w1, f_b1, f_w2, f_b2, f_w3, f_b3, f_w4, f_freq, rows)
    u = _hy_conv(pc, conv_w, conv_b, rows)
    ang = 2 * np.pi * np.outer(np.arange(2 * rows), np.arange(rows)) / (2 * rows)
    cf, sf = jnp.asarray(np.cos(ang), jnp.bfloat16), jnp.asarray(-np.sin(ang), jnp.bfloat16)
    scale = 1.0 / (nrm * (2.0 * rows))
    return pl.pallas_call(
        _hy_short_body,
        out_shape=jax.ShapeDtypeStruct((rows, HY_W), jnp.float32),
        name="hy_short",
    )(e, od, scale, u, f_bias.astype(jnp.float32), cf, sf)


def even_layer(x, xc, g, mod, mod_c, w_in, conv_w, conv_b, f_w1, f_b1, f_w2, f_b2, f_w3, f_b3, f_w4, f_freq, f_bias,
               q_norm, k_norm, sink, gt, gt_c, w_out):
    f32 = jnp.float32
    length, n_ctx = x.shape[1], xc.shape[1]
    filt = (f_w1, f_b1, f_w2, f_b2, f_w3, f_b3, f_w4, f_freq, f_bias, conv_w, conv_b)
    o_q = 2 * ATT_KV_W
    o_hy = o_q + ATT_Q_W
    w_perm = jnp.concatenate([w_in[:, o_hy:], w_in[:, o_q:o_hy], w_in[:, :o_q]], axis=1).astype(jnp.bfloat16)
    twice = lambda v: jnp.stack([v.reshape(-1), v.reshape(-1)])
    p = norm_proj(x[0], g, twice(mod[0]), twice(mod[1]), w_perm)
    pc = norm_proj(xc[0], g, twice(mod_c[0]), twice(mod_c[1]), w_perm)
    rope_c, rope_s = _rope_tables(length)
    qn, kn = _qk_prep(p, rope_c, rope_s, q_norm, k_norm, EVEN_TM)
    qcn, kcn = _qk_prep(pc, jnp.ones((n_ctx, HEAD_DIM), f32), jnp.zeros((n_ctx, HEAD_DIM), f32), q_norm, k_norm,
                        n_ctx)
    att = _attention(qn, kn, p, kcn, pc, sink, True)
    att_c = _attention(qcn, None, None, kcn, pc, sink, False)
    hy = hyena_long(p, *filt)
    hy_c = hyena_short(pc, *filt)
    x_new = _even_merge(hy, att, x[0], gt, w_out, EVEN_TM)
    xc_new = _even_merge(hy_c, att_c, xc[0], gt_c, w_out, n_ctx)
    return x_new[None], xc_new[None]


def odd_mixer(h, hc, lb, w_in, conv_w, conv_b, dt_bias, a_log, d_skip, ssd_norm, hg_norm, ctx_out):
    f32 = jnp.float32
    bsz, length, _ = h.shape
    n_ctx = hc.shape[1]
    p = pmm3(h, w_in).astype(f32)
    pc = pmm3(hc, w_in if ctx_out else w_in[:, :ODD_STATE_COLS]).astype(f32)
    o_dt = SSD_CONV_CH
    o_f = SSD_CONV_CH + 2 * SSD_HEADS
    o_i = o_f + 2 * HG_W
    o_z = ODD_STATE_COLS
    o_q = o_z + SSD_W
    o_g = o_q + HG_W
    gn = SSD_GROUPS * SSD_STATE

    def streams(pp):
        n = pp.shape[1]
        xbc = jax.nn.silu(dwconv_centred(pp[..., :SSD_CONV_CH], conv_w, conv_b))
        xs = xbc[..., :SSD_W].reshape(bsz, n, SSD_HEADS, SSD_HEAD_DIM)
        bm = xbc[..., SSD_W:SSD_W + gn].reshape(bsz, n, SSD_GROUPS, SSD_STATE)
        cm = xbc[..., SSD_W + gn:].reshape(bsz, n, SSD_GROUPS, SSD_STATE)
        dt_raw = pp[..., o_dt:o_f].reshape(bsz, n, 2, SSD_HEADS)
        f_raw = pp[..., o_f:o_i].reshape(bsz, n, 2, HG_HEADS, HG_EXPAND)
        iv = pp[..., o_i:o_i + HG_W].reshape(bsz, n, HG_HEADS, HG_VDIM)
        return xs, bm, cm, dt_raw, f_raw, iv

    xs, bm, cm, dt_raw, f_raw, iv = streams(p)
    xs_c, bm_c, cm_c, dt_raw_c, f_raw_c, iv_c = streams(pc)
    q = jax.nn.silu(p[..., o_q:o_g]).reshape(bsz, length, HG_HEADS, HG_EXPAND)
    q_c = jax.nn.silu(pc[..., o_q:o_g]).reshape(bsz, n_ctx, HG_HEADS, HG_EXPAND) if ctx_out else None
    lb = lb.astype(f32).reshape(HG_HEADS, HG_EXPAND)
    ssd0 = jnp.zeros((bsz, SSD_GROUPS, SSD_HEADS // SSD_GROUPS, SSD_HEAD_DIM, SSD_STATE), f32)
    hg0 = jnp.zeros((bsz, HG_HEADS, HG_EXPAND, HG_VDIM), f32)
    y_dirs, o_dirs, yc_dirs, oc_dirs = [], [], [], []
    for d in range(2):
        fl = (lambda a: jnp.flip(a, axis=1)) if d == 1 else (lambda a: a)
        a = -jnp.exp(a_log[d].astype(f32))
        dsk = d_skip[d].astype(f32)
        dtb = dt_bias[d].astype(f32)
        dt_l = jax.nn.softplus(dt_raw[:, :, d] + dtb)
        dt_c = jax.nn.softplus(dt_raw_c[:, :, d] + dtb)
        yc, s_ctx = ssd_scan(fl(xs_c), fl(dt_c), a, fl(bm_c), fl(cm_c), dsk, ssd0, ctx_out)
        yl, _ = ssd_scan(fl(xs), fl(dt_l), a, fl(bm), fl(cm), dsk, s_ctx, True)
        y_dirs.append(fl(yl))
        f_l = lb + (1.0 - lb) * jax.nn.sigmoid(f_raw[:, :, d])
        f_c = lb + (1.0 - lb) * jax.nn.sigmoid(f_raw_c[:, :, d])
        oc, s_hg = hgrn2_scan(fl(q_c) if ctx_out else None, fl(1.0 - f_c), fl(iv_c), fl(jnp.log(f_c)), hg0, ctx_out)
        ol, _ = hgrn2_scan(fl(q), fl(1.0 - f_l), fl(iv), fl(jnp.log(f_l)), s_hg, True)
        o_dirs.append(fl(ol))
        if ctx_out:
            yc_dirs.append(fl(yc))
            oc_dirs.append(fl(oc))

    def merge(yy, oo, pp, n):
        z = pp[..., o_z:o_q]
        g = pp[..., o_g:]
        ys = (yy.reshape(bsz, n, SSD_W) * jax.nn.silu(z)).reshape(bsz, n, SSD_GROUPS, SSD_W // SSD_GROUPS)
        ys = rms_norm(ys, ssd_norm.reshape(SSD_GROUPS, SSD_W // SSD_GROUPS)).reshape(bsz, n, SSD_W)
        hs = rms_norm(oo, hg_norm.reshape(HG_HEADS, HG_VDIM)).reshape(bsz, n, HG_W) * jax.nn.silu(g)
        return jnp.concatenate([ys, hs], axis=-1)

    out = merge(y_dirs[0] + y_dirs[1], o_dirs[0] + o_dirs[1], p, length).astype(h.dtype)
    if not ctx_out:
        return out, None
    out_c = merge(yc_dirs[0] + yc_dirs[1], oc_dirs[0] + oc_dirs[1], pc, n_ctx).astype(hc.dtype)
    return out, out_c


SCAN_Q = 128
SCAN_LEVELS = 7
ODD_COLS = SSD_CONV_CH + 8 * 512
HALO = SUBLANE


def _scan_constants():
    import numpy as np
    q = SCAN_Q
    d_hg, d_ssd, pairs, laters = [], [], [], []
    for direction in (0, 1):
        pos = np.arange(q) if direction == 0 else q - 1 - np.arange(q)
        pj, pt = pos[None, :], pos[:, None]
        top = pj <= pt
        end = pj > pt
        ones = np.ones((SUBLANE, q), bool)
        lv, pr, lt = [], [], []
        for level in range(SCAN_LEVELS):
            b = 2 ** level
            start = (pos // (2 * b)) * (2 * b)
            mid = (start + b)[:, None]
            later = pos >= start + b
            lv.append(np.where(later[:, None], (pj >= mid) & (pj <= pt), (pj > pt) & (pj < mid)))
            pr.append((start[:, None] == start[None, :]) & later[:, None] & ~later[None, :])
            lt.append(np.broadcast_to(later[:, None], (q, LANE)))
        pr.append(np.eye(q, dtype=bool))
        pr.append(top)
        d_hg.append(np.concatenate([top, end] + lv + [ones], axis=0))
        d_ssd.append(np.concatenate([top, end, ones], axis=0))
        pairs.append(np.stack(pr))
        laters.append(np.stack(lt))
    f = np.float32
    return (np.stack(d_hg).astype(f), np.stack(d_ssd).astype(f), np.stack(pairs).astype(f),
            np.stack(laters).astype(f))


def _split_dot(m_bf16, v):
    hi = v.astype(jnp.bfloat16)
    lo = (v - hi.astype(jnp.float32)).astype(jnp.bfloat16)
    return (jnp.dot(m_bf16, hi, preferred_element_type=jnp.float32)
            + jnp.dot(m_bf16, lo, preferred_element_type=jnp.float32))


def _dot_nt(a, b):
    return lax.dot_general(a, b, (((1,), (1,)), ((), ())), preferred_element_type=jnp.float32)


def _dot_tn(a, b):
    return lax.dot_general(a, b, (((0,), (0,)), ((), ())), preferred_element_type=jnp.float32)


def _softplus(x):
    return jnp.maximum(x, 0.0) + jnp.log1p(jnp.exp(-jnp.abs(x)))


def _scan_body(xbc_ref, prev_ref, next_ref, f_ref, iv_ref, q_ref, dt_ref,
               cw_ref, cb_ref, dtb_ref, a_ref, dsk_ref, lb_ref,
               dhg_ref, dssd_ref, pair_ref, later_ref, sel_ref,
               out_ref, s_ssd, s_hg, ydiag_ref, *, n_ctx_chunks, n_chunks):
    d = pl.program_id(0)
    j = pl.program_id(1)
    q_rows = SCAN_Q
    bf = jnp.bfloat16

    @pl.when(j == 0)
    def _():
        s_ssd[...] = jnp.zeros_like(s_ssd)
        s_hg[...] = jnp.zeros_like(s_hg)

    c = jnp.where(d == 0, j, jnp.where(j < n_ctx_chunks, n_ctx_chunks - 1 - j, n_chunks - 1 + n_ctx_chunks - j))
    first = (c == 0) | (c == n_ctx_chunks)
    last = (c == n_ctx_chunks - 1) | (c == n_chunks - 1)

    u = xbc_ref[...]
    row = lax.broadcasted_iota(jnp.int32, u.shape, 0)
    before = jnp.where(first, 0.0, prev_ref[HALO - 1:HALO, :])
    after = jnp.where(last, 0.0, next_ref[0:1, :])
    up = jnp.where(row == 0, before, pltpu.roll(u, 1, 0))
    un = jnp.where(row == q_rows - 1, after, pltpu.roll(u, q_rows - 1, 0))
    xbc = cw_ref[0:1, :] * up + cw_ref[1:2, :] * u + cw_ref[2:3, :] * un + cb_ref[...]
    xbc = xbc * jax.nn.sigmoid(xbc)
    x = xbc[:, :SSD_W]

    dt = _softplus(dt_ref[...] + dtb_ref[...])
    da = dt * a_ref[...]
    r = _split_dot(dssd_ref[...], da)
    cs, to_end, total = r[:q_rows], r[q_rows:2 * q_rows], r[2 * q_rows:2 * q_rows + 1]
    xdt = x * dt
    cs_hi = cs.astype(bf)
    cs_lo = (cs - cs_hi.astype(jnp.float32)).astype(bf)
    cs_rows = _dot_nt(sel_ref[...], cs_hi) + _dot_nt(sel_ref[...], cs_lo)
    l_mask = pair_ref[SCAN_LEVELS + 1]
    decay_in = jnp.exp(cs)
    w_end = (jnp.exp(to_end) * xdt).astype(bf)
    gn = SSD_GROUPS * SSD_STATE
    hpg = SSD_HEADS // SSD_GROUPS
    gw = hpg * SSD_HEAD_DIM
    for g in range(SSD_GROUPS):
        b_g = xbc[:, SSD_W + g * SSD_STATE:SSD_W + (g + 1) * SSD_STATE].astype(bf)
        c_g = xbc[:, SSD_W + gn + g * SSD_STATE:SSD_W + gn + (g + 1) * SSD_STATE].astype(bf)
        scores = _dot_nt(c_g, b_g)
        y_off = jnp.dot(c_g, s_ssd[g].astype(bf), preferred_element_type=jnp.float32)
        for hh in range(hpg):
            h = g * hpg + hh
            lo = h * SSD_HEAD_DIM
            diff = cs[:, lo:lo + 1] - cs_rows[h:h + 1, :]
            decay = jnp.exp(jnp.minimum(diff, 0.0)) * l_mask
            ydiag_ref[:, lo:lo + SSD_HEAD_DIM] = jnp.dot((scores * decay).astype(bf),
                                                         xdt[:, lo:lo + SSD_HEAD_DIM].astype(bf),
                                                         preferred_element_type=jnp.float32)
        cols = slice(g * gw, (g + 1) * gw)
        out_ref[:, cols] = (ydiag_ref[:, cols] + decay_in[:, cols] * y_off + dsk_ref[:, cols] * x[:, cols])
        s_ssd[g] = jnp.exp(total[:, cols]) * s_ssd[g] + _dot_tn(b_g, w_end[:, cols])

    lb = lb_ref[...]
    f = lb + (1.0 - lb) * jax.nn.sigmoid(f_ref[...])
    k_in = 1.0 - f
    qv = q_ref[...]
    qv = qv * jax.nn.sigmoid(qv)
    v_bf = iv_ref[...].astype(bf)
    e = jnp.exp(_split_dot(dhg_ref[...], jnp.log(f)))
    e_top, e_end = e[:q_rows], e[q_rows:2 * q_rows]
    e_tot = e[(2 + SCAN_LEVELS) * q_rows:(2 + SCAN_LEVELS) * q_rows + 1]
    for h in range(HG_HEADS):
        cols = slice(h * HG_EXPAND, (h + 1) * HG_EXPAND)
        q_h, k_h = qv[:, cols], k_in[:, cols]
        att = pair_ref[SCAN_LEVELS] * _dot_nt(q_h.astype(bf), k_h.astype(bf))
        for level in range(SCAN_LEVELS):
            e_l = e[(2 + level) * q_rows:(3 + level) * q_rows, cols]
            later = later_ref[level]
            q_l = (q_h * e_l * later).astype(bf)
            k_l = (k_h * e_l * (1.0 - later)).astype(bf)
            att = att + pair_ref[level] * _dot_nt(q_l, k_l)
        o = jnp.dot(att.astype(bf), v_bf[:, cols], preferred_element_type=jnp.float32)
        o = o + _dot_nt((q_h * e_top[:, cols]).astype(bf), s_hg[h].astype(bf))
        out_ref[:, SSD_W + h * HG_VDIM:SSD_W + (h + 1) * HG_VDIM] = o
        s_hg[h] = e_tot[:, cols] * s_hg[h] + _dot_tn(v_bf[:, cols], (k_h * e_end[:, cols]).astype(bf))


def _odd_scan(p_all, conv_w, conv_b, dtb, a_cols, dsk, lb, n_ctx):
    n_rows = p_all.shape[0]
    n_chunks = n_rows // SCAN_Q
    ncc = n_ctx // SCAN_Q
    d_hg, d_ssd, pairs, laters = _scan_constants()
    bf = jnp.bfloat16
    import numpy as np
    sel = np.zeros((LANE, SSD_W), np.float32)
    sel[np.arange(SSD_HEADS), np.arange(SSD_HEADS) * SSD_HEAD_DIM] = 1.0

    def chunk(d, j):
        return jnp.where(d == 0, j, jnp.where(j < ncc, ncc - 1 - j, n_chunks - 1 + ncc - j))

    per = SCAN_Q // HALO
    last_halo = n_rows // HALO - 1
    col512 = lambda blk: (lambda d, j: (chunk(d, j), blk))
    const2 = lambda shape: pl.BlockSpec(shape, lambda d, j: (0,) * len(shape))
    dirc = lambda shape: pl.BlockSpec((None,) + shape, lambda d, j: (d,) + (0,) * len(shape))
    body = functools.partial(_scan_body, n_ctx_chunks=ncc, n_chunks=n_chunks)
    return pl.pallas_call(
        body,
        grid=(2, n_chunks),
        in_specs=[
            pl.BlockSpec((SCAN_Q, SSD_CONV_CH), lambda d, j: (chunk(d, j), 0)),
            pl.BlockSpec((HALO, SSD_CONV_CH), lambda d, j: (jnp.maximum(chunk(d, j) * per - 1, 0), 0)),
            pl.BlockSpec((HALO, SSD_CONV_CH), lambda d, j: (jnp.minimum((chunk(d, j) + 1) * per, last_halo), 0)),
            pl.BlockSpec((SCAN_Q, 512), lambda d, j: (chunk(d, j), 2 + d)),
            pl.BlockSpec((SCAN_Q, 512), col512(4)),
            pl.BlockSpec((SCAN_Q, 512), col512(6)),
            pl.BlockSpec((SCAN_Q, 512), lambda d, j: (chunk(d, j), 8 + d)),
            const2((CONV_W, SSD_CONV_CH)), const2((1, SSD_CONV_CH)),
            dirc((1, SSD_W)), dirc((1, SSD_W)), dirc((1, SSD_W)), const2((1, HG_W)),
            dirc(d_hg.shape[1:]), dirc(d_ssd.shape[1:]), dirc(pairs.shape[1:]), dirc(laters.shape[1:]),
            const2((LANE, SSD_W)),
        ],
        out_specs=pl.BlockSpec((None, SCAN_Q, MIX_W), lambda d, j: (d, chunk(d, j), 0)),
        out_shape=jax.ShapeDtypeStruct((2, n_rows, MIX_W), jnp.float32),
        scratch_shapes=[pltpu.VMEM((SSD_GROUPS, SSD_STATE, SSD_W // SSD_GROUPS), jnp.float32),
                        pltpu.VMEM((HG_HEADS, HG_VDIM, HG_EXPAND), jnp.float32),
                        pltpu.VMEM((SCAN_Q, SSD_W), jnp.float32)],
        compiler_params=pltpu.CompilerParams(dimension_semantics=("arbitrary", "arbitrary"),
                                             vmem_limit_bytes=MOE_VMEM_LIMIT),
        name="odd_scan",
    )(p_all, p_all, p_all, p_all, p_all, p_all, p_all,
      conv_w, conv_b.reshape(1, -1), dtb, a_cols, dsk, lb.reshape(1, -1),
      jnp.asarray(d_hg, bf), jnp.asarray(d_ssd, bf), jnp.asarray(pairs), jnp.asarray(laters),
      jnp.asarray(sel, bf))


def _group_rms(v, width):
    parts = []
    for lo in range(0, v.shape[1], width):
        seg = v[:, lo:lo + width]
        parts.append(seg * lax.rsqrt(jnp.mean(seg * seg, axis=-1, keepdims=True) + EPS))
    return jnp.concatenate(parts, axis=1)


def _odd_merge_body(yo_ref, z_ref, g_ref, x_ref, sn_ref, hn_ref, gt_ref, w_ref, o_ref):
    yo = yo_ref[0] + yo_ref[1]
    z = z_ref[...]
    g = g_ref[...]
    ys = _group_rms(yo[:, :SSD_W] * (z * jax.nn.sigmoid(z)), SSD_W // SSD_GROUPS) * sn_ref[...]
    hs = _group_rms(yo[:, SSD_W:], HG_VDIM) * hn_ref[...] * (g * jax.nn.sigmoid(g))
    m = jnp.concatenate([ys, hs], axis=1).astype(jnp.bfloat16)
    o_ref[...] = x_ref[...] + gt_ref[...] * jnp.dot(m, w_ref[...].astype(jnp.bfloat16),
                                                    preferred_element_type=jnp.float32)


ODD_TM = 256


def _odd_merge(yo, p_all, x2d, ssd_norm, hg_norm, gt, w_out, n_ctx):
    n_lat, d = x2d.shape
    skip = n_ctx // ODD_TM
    return pl.pallas_call(
        _odd_merge_body,
        grid=(n_lat // ODD_TM,),
        in_specs=[pl.BlockSpec((2, ODD_TM, MIX_W), lambda i: (0, i + skip, 0)),
                  pl.BlockSpec((ODD_TM, 512), lambda i: (i + skip, 5)),
                  pl.BlockSpec((ODD_TM, 512), lambda i: (i + skip, 7)),
                  pl.BlockSpec((ODD_TM, d), lambda i: (i, 0)),
                  pl.BlockSpec((1, SSD_W), lambda i: (0, 0)),
                  pl.BlockSpec((1, HG_W), lambda i: (0, 0)),
                  pl.BlockSpec((1, d), lambda i: (0, 0)),
                  pl.BlockSpec((MIX_W, d), lambda i: (0, 0))],
        out_specs=pl.BlockSpec((ODD_TM, d), lambda i: (i, 0)),
        out_shape=jax.ShapeDtypeStruct((n_lat, d), jnp.float32),
        compiler_params=pltpu.CompilerParams(dimension_semantics=("arbitrary",)),
        name="odd_merge",
    )(yo, p_all, p_all, x2d, ssd_norm.reshape(1, -1), hg_norm.reshape(1, -1), gt.reshape(1, -1), w_out)


def odd_layer(x, xc, g, mod, mod_c, lb, w_in, conv_w, conv_b, dt_bias, a_log, d_skip, ssd_norm, hg_norm, gt, w_out):
    f32 = jnp.float32
    n_ctx = xc.shape[1]
    o_dt = SSD_CONV_CH
    o_f = o_dt + 2 * SSD_HEADS
    rep = lambda v: jnp.repeat(v, SSD_HEAD_DIM, axis=-1)
    w_perm = jnp.concatenate([w_in[:, :o_dt], w_in[:, o_f:], rep(w_in[:, o_dt:o_dt + SSD_HEADS]),
                              rep(w_in[:, o_dt + SSD_HEADS:o_f])], axis=1).astype(jnp.bfloat16)
    x_all = jnp.concatenate([xc[0], x[0]], axis=0)
    both = lambda a, b: jnp.stack([a.reshape(-1), b.reshape(-1)])
    p_all = norm_proj(x_all, g, both(mod_c[0], mod[0]), both(mod_c[1], mod[1]), w_perm, n_first=n_ctx)
    dtb = rep(dt_bias.astype(f32)).reshape(2, 1, SSD_W)
    a_cols = rep(-jnp.exp(a_log.astype(f32))).reshape(2, 1, SSD_W)
    dsk = rep(d_skip.astype(f32)).reshape(2, 1, SSD_W)
    yo = _odd_scan(p_all, conv_w, conv_b, dtb, a_cols, dsk, lb.astype(f32), n_ctx)
    return _odd_merge(yo, p_all, x[0], ssd_norm, hg_norm, gt, w_out, n_ctx)[None]


MOE_TM = 256
MOE_BM = 256
NEG_BIG = -1e30
MOE_ISSUE_UNROLL = 4
MOE_VMEM_LIMIT = 52 * 1024 * 1024


def _route_body(x_ref, g_ref, sh_ref, sc_ref, rw_ref, rb_ref, cnt0_ref,
                h_ref, idx_ref, gate_ref, rank_ref, cnt_ref, run_ref):
    i = pl.program_id(0)

    @pl.when(i == 0)
    def _():
        run_ref[...] = cnt0_ref[...]

    tm = x_ref.shape[0]
    x = x_ref[...]
    t = (x * lax.rsqrt(jnp.mean(x * x, axis=-1, keepdims=True) + EPS) * g_ref[...]) * (1.0 + sc_ref[...]) + sh_ref[...]
    h_ref[...] = t
    logits = jnp.dot(t.astype(jnp.bfloat16), rw_ref[...].astype(jnp.bfloat16),
                     preferred_element_type=jnp.float32) + rb_ref[...]
    lane = lax.broadcasted_iota(jnp.int32, (tm, LANE), 1)
    lane_f = lane.astype(jnp.float32)
    work = logits
    vals, sels, hots = [], [], []
    for _ in range(TOP_K):
        m = jnp.max(work, axis=-1, keepdims=True)
        sel = jnp.min(jnp.where(work == m, lane_f, float(LANE)), axis=-1, keepdims=True)
        hot = lane_f == sel
        vals.append(m)
        sels.append(sel.astype(jnp.int32))
        hots.append(hot)
        work = jnp.where(hot, -jnp.inf, work)
    exps = [jnp.exp(v - vals[0]) for v in vals]
    denom = exps[0] + exps[1] + exps[2] + exps[3]
    chosen = jnp.zeros((tm, LANE), jnp.float32)
    for hot in hots:
        chosen = chosen + hot.astype(jnp.float32)
    row = lax.broadcasted_iota(jnp.int32, (tm, tm), 0)
    col = lax.broadcasted_iota(jnp.int32, (tm, tm), 1)
    tri = (row > col).astype(jnp.bfloat16)
    before = jnp.dot(tri, chosen.astype(jnp.bfloat16), preferred_element_type=jnp.float32) + run_ref[0:1, :]
    idx_out = jnp.zeros((tm, LANE), jnp.int32)
    gate_out = jnp.zeros((tm, LANE), jnp.float32)
    rank_out = jnp.zeros((tm, LANE), jnp.int32)
    for k in range(TOP_K):
        rank_k = jnp.sum(jnp.where(hots[k], before, 0.0), axis=-1, keepdims=True).astype(jnp.int32)
        idx_out = jnp.where(lane == k, sels[k], idx_out)
        gate_out = jnp.where(lane == k, exps[k] / denom, gate_out)
        rank_out = jnp.where(lane == k, rank_k, rank_out)
    idx_ref[...] = idx_out
    gate_ref[...] = gate_out
    rank_ref[...] = rank_out
    run_new = run_ref[0:1, :] + jnp.sum(chosen, axis=0, keepdims=True)
    run_ref[...] = jnp.broadcast_to(run_new, run_ref.shape)
    cnt_ref[...] = jnp.broadcast_to(run_new, cnt_ref.shape)


def _moe_route(x2d, g, shift, scale, router_w, router_b, cnt0):
    n_tok, d = x2d.shape
    f32 = jnp.float32
    rw = jnp.pad(router_w, ((0, 0), (0, LANE - N_EXPERTS)))
    rb = jnp.pad(router_b.astype(f32), (0, LANE - N_EXPERTS), constant_values=NEG_BIG).reshape(1, LANE)
    tile = pl.BlockSpec((MOE_TM, LANE), lambda i: (i, 0))
    wide = pl.BlockSpec((MOE_TM, d), lambda i: (i, 0))
    vec = pl.BlockSpec((1, d), lambda i: (0, 0))
    small = pl.BlockSpec((SUBLANE, LANE), lambda i: (0, 0))
    row = lambda v: v.astype(f32).reshape(1, d)
    h, idx, gate, rank, cnt = pl.pallas_call(
        _route_body,
        grid=(n_tok // MOE_TM,),
        in_specs=[wide, vec, vec, vec, pl.BlockSpec((d, LANE), lambda i: (0, 0)),
                  pl.BlockSpec((1, LANE), lambda i: (0, 0)), small],
        out_specs=[wide, tile, tile, tile, small],
        out_shape=[jax.ShapeDtypeStruct((n_tok, d), f32),
                   jax.ShapeDtypeStruct((n_tok, LANE), jnp.int32),
                   jax.ShapeDtypeStruct((n_tok, LANE), f32),
                   jax.ShapeDtypeStruct((n_tok, LANE), jnp.int32),
                   jax.ShapeDtypeStruct((SUBLANE, LANE), f32)],
        scratch_shapes=[pltpu.VMEM((SUBLANE, LANE), f32)],
        compiler_params=pltpu.CompilerParams(dimension_semantics=("arbitrary",)),
        name="moe_route",
    )(x2d, row(g), row(shift), row(scale), rw, rb, cnt0)
    return h, idx[:, :TOP_K], gate, rank[:, :TOP_K], cnt


def _row_copy(src_ref, src_row, dst_ref, dst_row, sem):
    return pltpu.make_async_copy(src_ref.at[pl.ds(src_row, 1)], dst_ref.at[pl.ds(dst_row, 1)], sem)


def _scatter_body(off_ref, pad_ref, dest_ref, *rest, tile_starts):
    t_refs = rest[:-3]
    xs_ref, zero_ref, sem = rest[-3:]
    i = pl.program_id(0)
    tm = t_refs[0].shape[0]

    @pl.when(i == 0)
    def _():
        zero_ref[...] = jnp.zeros_like(zero_ref)
        used = off_ref[N_EXPERTS - 1] + pad_ref[N_EXPERTS - 1]
        n_rows = xs_ref.shape[0]

        def zero_block(start):
            return pltpu.make_async_copy(zero_ref, xs_ref.at[pl.ds(pl.multiple_of(start, MOE_BM), MOE_BM)], sem)

        for e in range(N_EXPERTS):
            tail = n_rows - (e + 1) * MOE_BM

            @pl.when(pad_ref[e] > 0)
            def _():
                zero_block(off_ref[e] + pad_ref[e] - MOE_BM).start()

            @pl.when(tail >= used)
            def _():
                zero_block(tail).start()
        for e in range(N_EXPERTS):
            tail = n_rows - (e + 1) * MOE_BM

            @pl.when(pad_ref[e] > 0)
            def _():
                zero_block(0).wait()

            @pl.when(tail >= used)
            def _():
                zero_block(0).wait()

    for s, t_ref in enumerate(t_refs):
        @pl.when((i >= tile_starts[s]) & (i < tile_starts[s + 1]))
        def _():
            def issue(t, carry):
                for k in range(TOP_K):
                    _row_copy(t_ref, t, xs_ref, dest_ref[TOP_K * t + k], sem).start(priority=k % 2)
                return carry

            lax.fori_loop(0, tm, issue, 0, unroll=MOE_ISSUE_UNROLL)
    for _ in range(TOP_K):
        pltpu.make_async_copy(t_refs[0], xs_ref.at[pl.ds(0, tm)], sem).wait()


def _stream_tiles(streams):
    starts = [0]
    for t in streams:
        starts.append(starts[-1] + t.shape[0] // MOE_TM)
    return tuple(starts)


def _stream_spec(width, starts, s, extra=0):
    lo, n = starts[s], starts[s + 1] - starts[s]
    return pl.BlockSpec((MOE_TM, width), lambda i, *_: (jnp.clip(i + extra - lo, 0, n - 1), 0))


def _moe_scatter(hs, dest_flat, off, padded, n_rows):
    d = hs[0].shape[1]
    starts = _stream_tiles(hs)
    in_specs = [pl.BlockSpec((TOP_K * MOE_TM,), lambda i, off, pad: (i,), memory_space=pltpu.SMEM)]
    in_specs += [_stream_spec(d, starts, s) for s in range(len(hs))]
    return pl.pallas_call(
        functools.partial(_scatter_body, tile_starts=starts),
        grid_spec=pltpu.PrefetchScalarGridSpec(
            num_scalar_prefetch=2,
            grid=(starts[-1],),
            in_specs=in_specs,
            out_specs=pl.BlockSpec(memory_space=pl.ANY),
            scratch_shapes=[pltpu.VMEM((MOE_BM, d), jnp.float32), pltpu.SemaphoreType.DMA],
        ),
        out_shape=jax.ShapeDtypeStruct((n_rows, d), jnp.float32),
        compiler_params=pltpu.CompilerParams(dimension_semantics=("arbitrary",)),
        name="moe_scatter",
    )(off, padded, dest_flat, *hs)


def _expert_body(blk_e_ref, n_act_ref, x_ref, wgu_ref, bgu_ref, wdn_ref, bdn_ref, y_ref, wgu_bf, wdn_bf):
    i = pl.program_id(0)

    @pl.when(i < n_act_ref[0])
    def _():
        prev = blk_e_ref[jnp.maximum(i - 1, 0)]

        @pl.when((i == 0) | (blk_e_ref[i] != prev))
        def _():
            wgu_bf[...] = wgu_ref[...].astype(jnp.bfloat16)
            wdn_bf[...] = wdn_ref[...].astype(jnp.bfloat16)

        x = x_ref[...].astype(jnp.bfloat16)
        gu = jnp.dot(x, wgu_bf[...], preferred_element_type=jnp.float32) + bgu_ref[...]
        gate = jnp.minimum(gu[:, :D_EXPERT], SWIGLU_LIMIT)
        up = jnp.clip(gu[:, D_EXPERT:], -SWIGLU_LIMIT, SWIGLU_LIMIT)
        act = (up + 1.0) * gate * jax.nn.sigmoid(SWIGLU_ALPHA * gate)
        y_ref[...] = jnp.dot(act.astype(jnp.bfloat16), wdn_bf[...],
                             preferred_element_type=jnp.float32) + bdn_ref[...]

    @pl.when(i >= n_act_ref[0])
    def _():
        y_ref[...] = jnp.zeros_like(y_ref)


def _moe_experts(xs, blk_e, n_act, w_gu, b_gu, w_dn, b_dn, layer):
    n_rows, d = xs.shape
    n_blk = n_rows // MOE_BM

    def blk(i, be, na):
        return jnp.minimum(i, na[0] - 1)

    def expert(i, be, na):
        return (layer, be[blk(i, be, na)], 0, 0)

    return pl.pallas_call(
        _expert_body,
        grid_spec=pltpu.PrefetchScalarGridSpec(
            num_scalar_prefetch=2,
            grid=(n_blk,),
            in_specs=[pl.BlockSpec((MOE_BM, d), lambda i, be, na: (blk(i, be, na), 0)),
                      pl.BlockSpec((None, None, d, 2 * D_EXPERT), expert),
                      pl.BlockSpec((None, 1, 2 * D_EXPERT), lambda i, be, na: (be[blk(i, be, na)], 0, 0)),
                      pl.BlockSpec((None, None, D_EXPERT, d), expert),
                      pl.BlockSpec((None, 1, d), lambda i, be, na: (be[blk(i, be, na)], 0, 0))],
            out_specs=pl.BlockSpec((MOE_BM, d), lambda i, be, na: (i, 0)),
            scratch_shapes=[pltpu.VMEM((d, 2 * D_EXPERT), jnp.bfloat16),
                            pltpu.VMEM((D_EXPERT, d), jnp.bfloat16)],
        ),
        out_shape=jax.ShapeDtypeStruct((n_rows, d), jnp.float32),
        compiler_params=pltpu.CompilerParams(dimension_semantics=("arbitrary",),
                                             vmem_limit_bytes=MOE_VMEM_LIMIT),
        name="moe_experts",
    )(blk_e, n_act, xs, w_gu, b_gu.reshape(N_EXPERTS, 1, -1), w_dn, b_dn.reshape(N_EXPERTS, 1, -1))


def _combine_body(dest_ref, dest_nxt_ref, *rest, tile_starts):
    n_streams = len(tile_starts) - 1
    ins, ys_ref = rest[:3 * n_streams], rest[3 * n_streams]
    y_refs = rest[3 * n_streams + 1:4 * n_streams + 1]
    buf_ref, sems = rest[4 * n_streams + 1:]
    i = pl.program_id(0)
    n = pl.num_programs(0)
    tm = y_refs[0].shape[0]

    def fetch(d_ref, slot):
        def issue(t, carry):
            for k in range(TOP_K):
                _row_copy(ys_ref, d_ref[TOP_K * t + k], buf_ref.at[slot, k], t,
                          sems.at[slot]).start(priority=k % 2)
            return carry
        lax.fori_loop(0, tm, issue, 0, unroll=MOE_ISSUE_UNROLL)

    @pl.when(i == 0)
    def _():
        fetch(dest_ref, 0)

    @pl.when(i + 1 < n)
    def _():
        fetch(dest_nxt_ref, (i + 1) % 2)

    slot = i % 2
    for k in range(TOP_K):
        pltpu.make_async_copy(ys_ref.at[pl.ds(0, tm)], buf_ref.at[slot, k], sems.at[slot]).wait()
    for s in range(n_streams):
        gate_ref, x_ref, gt_ref = ins[3 * s:3 * s + 3]

        @pl.when((i >= tile_starts[s]) & (i < tile_starts[s + 1]))
        def _():
            g = gate_ref[...]
            acc = g[:, 0:1] * buf_ref[slot, 0]
            for k in range(1, TOP_K):
                acc = acc + g[:, k:k + 1] * buf_ref[slot, k]
            y_refs[s][...] = x_ref[...] + gt_ref[...] * acc


def _moe_combine(ys, dest_flat, gates, xs2d, gts):
    d = ys.shape[1]
    starts = _stream_tiles(xs2d)
    n_tiles = starts[-1]
    in_specs = [pl.BlockSpec((TOP_K * MOE_TM,), lambda i: (i,), memory_space=pltpu.SMEM),
                pl.BlockSpec((TOP_K * MOE_TM,), lambda i: (jnp.minimum(i + 1, n_tiles - 1),),
                             memory_space=pltpu.SMEM)]
    args = [dest_flat, dest_flat]
    for s, (gate, x2d, gt) in enumerate(zip(gates, xs2d, gts)):
        in_specs += [_stream_spec(LANE, starts, s), _stream_spec(d, starts, s), pl.BlockSpec((1, d), lambda i: (0, 0))]
        args += [gate, x2d, gt.astype(jnp.float32).reshape(1, d)]
    in_specs.append(pl.BlockSpec(memory_space=pl.ANY))
    args.append(ys)
    return pl.pallas_call(
        functools.partial(_combine_body, tile_starts=starts),
        grid=(n_tiles,),
        in_specs=in_specs,
        out_specs=[_stream_spec(d, starts, s) for s in range(len(xs2d))],
        out_shape=[jax.ShapeDtypeStruct(x2d.shape, jnp.float32) for x2d in xs2d],
        scratch_shapes=[pltpu.VMEM((2, TOP_K, MOE_TM, d), jnp.float32), pltpu.SemaphoreType.DMA((2,))],
        compiler_params=pltpu.CompilerParams(dimension_semantics=("arbitrary",)),
        name="moe_combine",
    )(*args)


def moe_layer(streams, g, router_w, router_b, w_gu, b_gu, w_dn, b_dn, layer):
    i32 = jnp.int32
    routed = []
    cnt = jnp.zeros((SUBLANE, LANE), jnp.float32)
    for x2d, shift, scale, _ in streams:
        assert x2d.shape[0] % MOE_TM == 0
        h, idx, gate, rank, cnt = _moe_route(x2d, g, shift, scale, router_w, router_b, cnt)
        routed.append((h, idx, gate, rank))
    counts = cnt[0, :N_EXPERTS].astype(i32)
    padded = (counts + MOE_BM - 1) // MOE_BM * MOE_BM
    pad_end = jnp.cumsum(padded)
    off = (pad_end - padded).astype(i32)
    padded = padded.astype(i32)
    n_tok = sum(s[0].shape[0] for s in streams)
    n_blk = -(-(n_tok * TOP_K) // MOE_BM) + N_EXPERTS
    blk_e = jnp.minimum(jnp.sum(jnp.arange(n_blk)[:, None] * MOE_BM >= pad_end[None, :], axis=1),
                        N_EXPERTS - 1).astype(i32)
    n_act = (pad_end[-1:] // MOE_BM).astype(i32)
    experts = jnp.arange(N_EXPERTS, dtype=i32)
    dests = []
    for _, idx, _, rank in routed:
        dest = rank + jnp.sum(jnp.where(idx[..., None] == experts, off, 0), axis=-1)
        dests.append(dest.reshape(-1).astype(i32))
    dest_flat = jnp.concatenate(dests)
    xs = _moe_scatter([r[0] for r in routed], dest_flat, off, padded, n_blk * MOE_BM)
    ys = _moe_experts(xs, blk_e, n_act, w_gu, b_gu, w_dn, b_dn, layer)
    return _moe_combine(ys, dest_flat, [r[2] for r in routed], [s[0] for s in streams], [s[3] for s in streams])


def kernel(x, c, ctx, c_ctx, norm_g, ada_w, ada_b, w_out, w_in_even, hy_conv_w, hy_conv_b,
           hy_w1, hy_b1, hy_w2, hy_b2, hy_w3, hy_b3, hy_w4, hy_freq, hy_filter_bias,
           att_q_norm, att_k_norm, att_sink, w_in_odd, ssd_conv_w, ssd_conv_b, ssd_dt_bias,
           ssd_A_log, ssd_D, ssd_norm, hg_lower_bounds, hg_norm, router_w, router_b,
           moe_w_gu, moe_b_gu, moe_w_dn, moe_b_dn):
    lbs = jax.nn.softmax(hg_lower_bounds.astype(jnp.float32), axis=0)
    lbs = jnp.cumsum(lbs, axis=0) - lbs[0]
    xc = ctx
    for layer in range(DEPTH):
        ctx_out = layer < DEPTH - 1
        i = layer // 2
        sh, sc, gt = adaln(c, ada_w[layer], ada_b[layer], 0)
        sh_c, sc_c, gt_c = adaln(c_ctx, ada_w[layer], ada_b[layer], 0)
        assert (layer % 2 == 0) == ctx_out
        if layer % 2 == 0:
            x, xc = even_layer(x, xc, norm_g[layer, 0], (sh, sc), (sh_c, sc_c), w_in_even[i], hy_conv_w[i],
                               hy_conv_b[i], hy_w1[i], hy_b1[i], hy_w2[i], hy_b2[i], hy_w3[i], hy_b3[i], hy_w4[i],
                               hy_freq[i], hy_filter_bias[i], att_q_norm[i], att_k_norm[i], att_sink[i], gt, gt_c,
                               w_out[layer])
        else:
            x = odd_layer(x, xc, norm_g[layer, 0], (sh, sc), (sh_c, sc_c), lbs[layer], w_in_odd[i], ssd_conv_w[i],
                          ssd_conv_b[i], ssd_dt_bias[i], ssd_A_log[i], ssd_D[i], ssd_norm[i], hg_norm[i], gt,
                          w_out[layer])
        sh, sc, gt = adaln(c, ada_w[layer], ada_b[layer], 1)
        streams = [(x[0], sh, sc, gt)]
        if ctx_out:
            sh_c, sc_c, gt_c = adaln(c_ctx, ada_w[layer], ada_b[layer], 1)
            streams.append((xc[0], sh_c, sc_c, gt_c))
        outs = moe_layer(streams, norm_g[layer, 1], router_w[layer], router_b[layer], moe_w_gu, moe_b_gu[layer],
                         moe_w_dn, moe_b_dn[layer], layer)
        x = outs[0][None]
        if ctx_out:
            xc = outs[1][None]
    return x
```

```python
import functools
import math

import jax
import jax.numpy as jnp
from jax import lax
from jax.experimental import pallas as pl
from jax.experimental.pallas import tpu as pltpu

D_MODEL = 1024
DEPTH = 2
GRID_W = 64
MIX_W = D_MODEL
EPS = 1e-6
CONV_W = 3

HY_W = MIX_W // 2
HY_ORDER = 2
HY_EMB = 33
HY_FFN = 64
HY_TARGET = 1e-2
HY_SHORT_PCT = 0.3
HY_LONG_PCT = 1.5

HEAD_DIM = 64
ATT_HEADS = (MIX_W // 2) // HEAD_DIM
ATT_KV_HEADS = 2
ATT_GROUP = ATT_HEADS // ATT_KV_HEADS
ATT_WINDOW = 128
ATT_BLOCK = 128
ROPE_BASE = 10000.0
ATT_Q_W = ATT_HEADS * HEAD_DIM
ATT_KV_W = ATT_KV_HEADS * HEAD_DIM

SSD_W = MIX_W // 2
SSD_HEAD_DIM = 64
SSD_HEADS = SSD_W // SSD_HEAD_DIM
SSD_GROUPS = 2
SSD_STATE = 128
SSD_CHUNK = 128
SSD_CONV_CH = SSD_W + 2 * SSD_GROUPS * SSD_STATE

HG_W = MIX_W // 2
HG_EXPAND = 128
HG_HEADS = HG_W // HG_EXPAND
HG_VDIM = HG_W // HG_HEADS
HG_CHUNK = 64

N_EXPERTS = 32
TOP_K = 4
D_EXPERT = D_MODEL
SWIGLU_ALPHA = 1.702
SWIGLU_LIMIT = 7.0
MOE_BLOCK = 128

EVEN_IN = 2 * ATT_KV_W + ATT_Q_W + 3 * HY_W
ODD_STATE_COLS = SSD_CONV_CH + 2 * SSD_HEADS + 3 * HG_W
ODD_IN = ODD_STATE_COLS + SSD_W + 2 * HG_W

LANE = 128
SUBLANE = 8


def _mm_body(a_ref, b_ref, o_ref):
    a = a_ref[...].astype(jnp.bfloat16)
    b = b_ref[...].astype(jnp.bfloat16)
    o_ref[...] = jnp.dot(a, b, preferred_element_type=jnp.float32)


def _pick_tile(n, candidates):
    for c in candidates:
        if n % c == 0:
            return c
    return n


def pmm(a, b):
    m, k = a.shape
    n = b.shape[1]
    n_pad = -(-n // LANE) * LANE
    if n_pad != n:
        b = jnp.pad(b, ((0, 0), (0, n_pad - n)))
    m_pad = -(-m // SUBLANE) * SUBLANE
    if m_pad != m:
        a = jnp.pad(a, ((0, m_pad - m), (0, 0)))
    tm = _pick_tile(m_pad, (512, 256, 128, 64, 32, 16, 8))
    tn = _pick_tile(n_pad, (512, 384, 256, 128))
    out = pl.pallas_call(
        _mm_body,
        grid=(m_pad // tm, n_pad // tn),
        in_specs=[pl.BlockSpec((tm, k), lambda i, j: (i, 0)),
                  pl.BlockSpec((k, tn), lambda i, j: (0, j))],
        out_specs=pl.BlockSpec((tm, tn), lambda i, j: (i, j)),
        out_shape=jax.ShapeDtypeStruct((m_pad, n_pad), jnp.float32),
        name="dense_mm",
    )(a, b)
    return out[:m, :n]


PROJ_TM = 256
PROJ_VMEM_LIMIT = 56 * 1024 * 1024


def _norm_proj_body(x_ref, g_ref, sh_ref, sc_ref, w_ref, o_ref):
    x = x_ref[...]
    y = x * lax.rsqrt(jnp.mean(x * x, axis=-1, keepdims=True) + EPS) * g_ref[...]
    h = y * (1.0 + sc_ref[...]) + sh_ref[...]
    o_ref[...] = jnp.dot(h.astype(jnp.bfloat16), w_ref[...], preferred_element_type=jnp.float32)


def norm_proj(x2d, g, shift, scale, w_bf16, n_first=0):
    rows, d = x2d.shape
    n = w_bf16.shape[1]
    assert rows % PROJ_TM == 0 and n_first % PROJ_TM == 0 and n % LANE == 0
    first_tiles = n_first // PROJ_TM
    mod = pl.BlockSpec((None, 1, d), lambda i: (jnp.where(i < first_tiles, 0, 1), 0, 0))
    return pl.pallas_call(
        _norm_proj_body,
        grid=(rows // PROJ_TM,),
        in_specs=[pl.BlockSpec((PROJ_TM, d), lambda i: (i, 0)),
                  pl.BlockSpec((1, d), lambda i: (0, 0)), mod, mod,
                  pl.BlockSpec((d, n), lambda i: (0, 0))],
        out_specs=pl.BlockSpec((PROJ_TM, n), lambda i: (i, 0)),
        out_shape=jax.ShapeDtypeStruct((rows, n), jnp.float32),
        compiler_params=pltpu.CompilerParams(dimension_semantics=("arbitrary",),
                                             vmem_limit_bytes=PROJ_VMEM_LIMIT),
        name="norm_proj",
    )(x2d, g.astype(jnp.float32).reshape(1, d), shift.astype(jnp.float32).reshape(2, 1, d),
      scale.astype(jnp.float32).reshape(2, 1, d), w_bf16)


def pmm3(a, b):
    lead = a.shape[:-1]
    return pmm(a.reshape(-1, a.shape[-1]), b).reshape(*lead, b.shape[-1])


def rms_norm(x, g):
    xf = x.astype(jnp.float32)
    y = xf * lax.rsqrt(jnp.mean(xf * xf, axis=-1, keepdims=True) + EPS)
    return (y * g.astype(jnp.float32)).astype(x.dtype)


def modulate(h, shift, scale):
    return h * (1.0 + scale) + shift


def adaln(cond, w, b, j):
    lo, hi = 3 * j * D_MODEL, 3 * (j + 1) * D_MODEL
    m = jax.nn.silu(cond) @ w[:, lo:hi] + b[lo:hi]
    return jnp.split(m, 3, axis=-1)


def dwconv_centred(u, w, b):
    ch = u.shape[-1]
    y = lax.conv_general_dilated(u, w[:, None, :].astype(u.dtype), window_strides=(1,),
                                 padding=[(CONV_W // 2, CONV_W // 2)],
                                 dimension_numbers=('NWC', 'WIO', 'NWC'), feature_group_count=ch)
    return y + b.astype(u.dtype)


def axial_rope_tables(length):
    rows = length // GRID_W
    n_pairs = HEAD_DIM // 4
    inv = ROPE_BASE ** (-jnp.arange(n_pairs, dtype=jnp.float32) / n_pairs)
    row_ang = jnp.arange(rows, dtype=jnp.float32)[:, None] * inv
    col_ang = jnp.arange(GRID_W, dtype=jnp.float32)[:, None] * inv
    ang_r = jnp.broadcast_to(row_ang[:, None], (rows, GRID_W, n_pairs)).reshape(length, n_pairs)
    ang_c = jnp.broadcast_to(col_ang[None], (rows, GRID_W, n_pairs)).reshape(length, n_pairs)
    return jnp.cos(ang_r), jnp.sin(ang_r), jnp.cos(ang_c), jnp.sin(ang_c)


def _rotate(u, cos, sin):
    n = u.shape[-1] // 2
    u1, u2 = u[..., :n], u[..., n:]
    cos = cos[None, :, None, :]
    sin = sin[None, :, None, :]
    return jnp.concatenate([u1 * cos - u2 * sin, u1 * sin + u2 * cos], axis=-1)


def apply_axial_rope(u, tables):
    cr, sr, cc, sc = tables
    half = HEAD_DIM // 2
    return jnp.concatenate([_rotate(u[..., :half], cr, sr), _rotate(u[..., half:], cc, sc)], axis=-1)


def hyena_filters(length, w1, b1, w2, b2, w3, b3, w4, freq):
    f32 = jnp.float32
    t = jnp.linspace(0.0, 1.0, length, dtype=f32)[:, None]
    bands = (HY_EMB - 1) // 2
    w_ang = 2.0 * math.pi * jnp.arange(length, dtype=f32)[:, None] / length
    fr = jnp.linspace(1e-4, bands - 1, bands, dtype=f32)[None]
    z = jnp.concatenate([t, jnp.cos(fr * w_ang), -jnp.sin(fr * w_ang)], axis=-1)
    fq = freq.astype(f32)
    hdn = jnp.sin(fq * (z @ w1.astype(f32) + b1.astype(f32)))
    hdn = jnp.sin(fq * (hdn @ w2.astype(f32) + b2.astype(f32)))
    hdn = jnp.sin(fq * (hdn @ w3.astype(f32) + b3.astype(f32)))
    h = (hdn @ w4.astype(f32)).reshape(length, HY_ORDER, 2, HY_W)
    max_decay = math.log(HY_TARGET) / HY_SHORT_PCT
    min_decay = math.log(HY_TARGET) / HY_LONG_PCT
    deltas = jnp.abs(jnp.linspace(min_decay, max_decay, HY_W, dtype=f32))
    h = h * jnp.exp(-t * deltas)[:, None, None, :]
    h2 = jnp.concatenate([h[:, :, 0], jnp.zeros((1, HY_ORDER, HY_W), f32), h[:0:-1, :, 1]], axis=0)
    h2 = h2 / jnp.sum(jnp.abs(h2), axis=0, keepdims=True)
    return jnp.fft.rfft(h2, axis=0)


def hyena_mix(u, hf, filter_bias, conv_w, conv_b):
    length = u.shape[1]
    u = dwconv_centred(u.astype(jnp.float32), conv_w, conv_b)
    v, x1, x2 = jnp.split(u, 3, axis=-1)
    z = v
    for o, gate in enumerate((x1, x2)):
        zf = jnp.fft.rfft(z, n=2 * length, axis=1)
        zc = jnp.fft.irfft(zf * hf[None, :, o], n=2 * length, axis=1)[:, :length]
        z = gate * (zc + z * filter_bias[o].astype(jnp.float32))
    return z


def window_attention(q, k, v, k_c, v_c, sink):
    bsz, length = q.shape[:2]
    nb = length // ATT_BLOCK
    scale = HEAD_DIM ** -0.5
    qb = q.reshape(bsz, nb, ATT_BLOCK, ATT_KV_HEADS, ATT_GROUP, HEAD_DIM)
    pad = ((0, 0), (ATT_BLOCK, ATT_BLOCK), (0, 0), (0, 0))

    def band(a):
        ap = jnp.pad(a, pad).reshape(bsz, nb + 2, ATT_BLOCK, ATT_KV_HEADS, HEAD_DIM)
        return jnp.concatenate([ap[:, :-2], ap[:, 1:-1], ap[:, 2:]], axis=2)

    kw, vw = band(k), band(v)
    s_loc = jnp.einsum('bnqhgd,bnkhd->bnhgqk', qb, kw) * scale
    s_ctx = jnp.einsum('bnqhgd,bchd->bnhgqc', qb, k_c) * scale
    qpos = jnp.arange(nb)[:, None] * ATT_BLOCK + jnp.arange(ATT_BLOCK)[None]
    kpos = (jnp.arange(nb)[:, None] - 1) * ATT_BLOCK + jnp.arange(3 * ATT_BLOCK)[None]
    rel = kpos[:, None, :] - qpos[:, :, None]
    valid = (jnp.abs(rel) <= ATT_WINDOW) & (kpos[:, None, :] >= 0) & (kpos[:, None, :] < length)
    s_loc = jnp.where(valid[None, :, None, None], s_loc, -jnp.inf)
    sink_l = jnp.broadcast_to(sink.astype(jnp.float32).reshape(1, 1, ATT_KV_HEADS, ATT_GROUP, 1, 1),
                              s_loc.shape[:-1] + (1,))
    p = jax.nn.softmax(jnp.concatenate([s_loc, s_ctx, sink_l], axis=-1), axis=-1)
    n_loc = 3 * ATT_BLOCK
    n_ctx = k_c.shape[1]
    o = (jnp.einsum('bnhgqk,bnkhd->bnqhgd', p[..., :n_loc], vw)
         + jnp.einsum('bnhgqc,bchd->bnqhgd', p[..., n_loc:n_loc + n_ctx], v_c))
    return o.reshape(bsz, length, ATT_Q_W)


def context_attention(q_c, k_c, v_c, sink):
    bsz, n_ctx = q_c.shape[:2]
    s = jnp.einsum('bqhgd,bkhd->bhgqk', q_c, k_c) * HEAD_DIM ** -0.5
    sink_l = jnp.broadcast_to(sink.astype(jnp.float32).reshape(1, ATT_KV_HEADS, ATT_GROUP, 1, 1),
                              s.shape[:-1] + (1,))
    p = jax.nn.softmax(jnp.concatenate([s, sink_l], axis=-1), axis=-1)[..., :-1]
    return jnp.einsum('bhgqk,bkhd->bqhgd', p, v_c).reshape(bsz, n_ctx, ATT_Q_W)


def ssd_scan(x, dt, a, bm, cm, d_skip, init, need_y):
    bsz, length, n_heads, hd = x.shape
    nc = length // SSD_CHUNK
    hpg = n_heads // SSD_GROUPS
    da = (dt * a).reshape(bsz, nc, SSD_CHUNK, SSD_GROUPS, hpg)
    cs = jnp.cumsum(da, axis=2)
    xdt = (x * dt[..., None]).reshape(bsz, nc, SSD_CHUNK, SSD_GROUPS, hpg, hd)
    bc = bm.reshape(bsz, nc, SSD_CHUNK, SSD_GROUPS, SSD_STATE)
    cc = cm.reshape(bsz, nc, SSD_CHUNK, SSD_GROUPS, SSD_STATE)
    to_end = jnp.exp(cs[:, :, -1:] - cs)
    states = jnp.einsum('bcsgn,bcsgh,bcsghp->bcghpn', bc, to_end, xdt)
    chunk_decay = jnp.exp(cs[:, :, -1])

    def step(s, inp):
        st, dec = inp
        return s * dec[..., None, None] + st, s

    s_final, s_in = lax.scan(step, init, (jnp.moveaxis(states, 1, 0), jnp.moveaxis(chunk_decay, 1, 0)))
    if not need_y:
        return None, s_final
    s_in = jnp.moveaxis(s_in, 0, 1)
    cs_t = jnp.moveaxis(cs, 2, -1)
    diff = cs_t[..., :, None] - cs_t[..., None, :]
    lower = jnp.tril(jnp.ones((SSD_CHUNK, SSD_CHUNK), bool))
    decay = jnp.where(lower, jnp.exp(jnp.where(lower, diff, 0.0)), 0.0)
    scores = jnp.einsum('bclgn,bcsgn->bcgls', cc, bc)
    y_diag = jnp.einsum('bcgls,bcghls,bcsghp->bclghp', scores, decay, xdt)
    y_off = jnp.einsum('bclgn,bcghpn,bclgh->bclghp', cc, s_in, jnp.exp(cs))
    y = (y_diag + y_off).reshape(bsz, length, n_heads, hd) + d_skip[:, None] * x
    return y, s_final


def hgrn2_scan(q, k, v, g, init, need_o):
    bsz, length, n_heads, _ = k.shape
    nc = length // HG_CHUNK

    def chunks(a):
        return a.reshape(bsz, nc, HG_CHUNK, n_heads, a.shape[-1]).transpose(1, 0, 3, 2, 4)

    lower = jnp.tril(jnp.ones((HG_CHUNK, HG_CHUNK), bool))[:, :, None]

    def update(s, kc, vc, cum):
        last = cum[:, :, -1]
        return (s * jnp.exp(last)[..., None]
                + jnp.einsum('bhsk,bhsv->bhkv', kc * jnp.exp(last[:, :, None] - cum), vc))

    if not need_o:
        def step_state(s, inp):
            kc, vc, gc = inp
            return update(s, kc, vc, jnp.cumsum(gc, axis=2)), None
        s_final, _ = lax.scan(step_state, init, (chunks(k), chunks(v), chunks(g)))
        return None, s_final

    def step(s, inp):
        qc, kc, vc, gc = inp
        cum = jnp.cumsum(gc, axis=2)
        diff = cum[:, :, :, None, :] - cum[:, :, None, :, :]
        decay = jnp.where(lower, jnp.exp(jnp.where(lower, diff, 0.0)), 0.0)
        att = jnp.einsum('bhtk,bhsk,bhtsk->bhts', qc, kc, decay)
        o = (jnp.einsum('bhtk,bhkv->bhtv', qc * jnp.exp(cum), s)
             + jnp.einsum('bhts,bhsv->bhtv', att, vc))
        return update(s, kc, vc, cum), o

    s_final, o = lax.scan(step, init, (chunks(q), chunks(k), chunks(v), chunks(g)))
    return o.transpose(1, 0, 3, 2, 4).reshape(bsz, length, n_heads, v.shape[-1]), s_final


HY_N1 = 128
HY_N2 = 256
HY_NB = 8
HY_TM = 512


def _hy_dft_constants(length):
    import numpy as np
    n = 2 * length
    assert n == HY_N1 * HY_N2
    half = HY_N1 // 2
    k1 = np.arange(HY_N1)[:, None]
    n1 = np.arange(half)[None, :]
    n2 = np.arange(HY_N2)[:, None, None]
    ang = 2 * np.pi * (k1 * n1 / HY_N1)[None] + 2 * np.pi * n2 * k1[None] / n
    fwd = np.stack([np.cos(ang), -np.sin(ang)], axis=2).reshape(HY_N2, 2 * HY_N1, half)
    inv = fwd.transpose(0, 2, 1)
    a2 = 2 * np.pi * np.outer(np.arange(HY_N2), np.arange(HY_N2)) / HY_N2
    c, s = np.cos(a2), -np.sin(a2)
    m_fwd = np.concatenate([np.stack([c, -s], axis=2).reshape(HY_N2, 2 * HY_N2),
                            np.stack([s, c], axis=2).reshape(HY_N2, 2 * HY_N2)], axis=0)
    m_inv = np.stack([np.concatenate([c, s], axis=1), np.concatenate([-s, c], axis=1)],
                     axis=1).reshape(2 * HY_N2, 2 * HY_N2)
    f = np.float32
    return fwd.astype(f), inv.astype(f), m_fwd.astype(f), m_inv.astype(f)


def _hy_conv_body(p_ref, prev_ref, next_ref, w_ref, b_ref, o_ref):
    i = pl.program_id(1)
    n = pl.num_programs(1)
    u = p_ref[...]
    rows = u.shape[0]
    row = lax.broadcasted_iota(jnp.int32, u.shape, 0)
    before = jnp.where(i == 0, 0.0, prev_ref[HALO - 1:HALO, :])
    after = jnp.where(i == n - 1, 0.0, next_ref[0:1, :])
    up = jnp.where(row == 0, before, pltpu.roll(u, 1, 0))
    un = jnp.where(row == rows - 1, after, pltpu.roll(u, rows - 1, 0))
    o_ref[...] = w_ref[0:1, :] * up + w_ref[1:2, :] * u + w_ref[2:3, :] * un + b_ref[...]


def _hy_conv(p, conv_w, conv_b, tm):
    length = p.shape[0]
    per = tm // HALO
    last = length // HALO - 1
    return pl.pallas_call(
        _hy_conv_body,
        grid=(3, length // tm),
        in_specs=[pl.BlockSpec((tm, HY_W), lambda c, i: (i, c)),
                  pl.BlockSpec((HALO, HY_W), lambda c, i: (jnp.maximum(i * per - 1, 0), c)),
                  pl.BlockSpec((HALO, HY_W), lambda c, i: (jnp.minimum((i + 1) * per, last), c)),
                  pl.BlockSpec((CONV_W, HY_W), lambda c, i: (0, c)),
                  pl.BlockSpec((1, HY_W), lambda c, i: (0, c))],
        out_specs=pl.BlockSpec((None, tm, HY_W), lambda c, i: (c, i, 0)),
        out_shape=jax.ShapeDtypeStruct((3, length, HY_W), jnp.float32),
        name="hy_conv",
    )(p, p, p, conv_w, conv_b.reshape(1, -1))


def _hy_filter_body(z_ref, w1_ref, b1_ref, w2_ref, b2_ref, w3_ref, b3_ref, w4_ref, fq_ref, dl_ref,
                    e_ref, o_ref, nrm_ref):
    i = pl.program_id(0)
    bf = jnp.bfloat16
    z = z_ref[...]
    fq = fq_ref[...]

    def layer(a, w_ref, b_ref):
        return jnp.sin(fq * (jnp.dot(a.astype(bf), w_ref[...].astype(bf), preferred_element_type=jnp.float32)
                             + b_ref[...]))

    hdn = layer(layer(layer(z, w1_ref, b1_ref), w2_ref, b2_ref), w3_ref, b3_ref)
    h = jnp.dot(hdn.astype(bf), w4_ref[...].astype(bf), preferred_element_type=jnp.float32)
    decay = jnp.exp(-z[:, 0:1] * dl_ref[...])
    first = (lax.broadcasted_iota(jnp.int32, decay.shape, 0) == 0) & (i == 0)
    acc = []
    for order in range(HY_ORDER):
        lo = order * 2 * HY_W
        h0 = h[:, lo:lo + HY_W] * decay
        h1 = jnp.where(first, 0.0, h[:, lo + HY_W:lo + 2 * HY_W] * decay)
        e_ref[:, order * HY_W:(order + 1) * HY_W] = h0 + h1
        o_ref[:, order * HY_W:(order + 1) * HY_W] = h0 - h1
        acc.append(jnp.sum(jnp.abs(h0) + jnp.abs(h1), axis=0, keepdims=True))
    part = jnp.concatenate(acc, axis=1)

    @pl.when(i == 0)
    def _():
        nrm_ref[...] = jnp.zeros_like(nrm_ref)

    nrm_ref[...] = nrm_ref[...] + part


def _hy_filter_time(length, w1, b1, w2, b2, w3, b3, w4, freq, tm):
    f32 = jnp.float32
    t = jnp.linspace(0.0, 1.0, length, dtype=f32)[:, None]
    bands = (HY_EMB - 1) // 2
    w_ang = 2.0 * math.pi * jnp.arange(length, dtype=f32)[:, None] / length
    fr = jnp.linspace(1e-4, bands - 1, bands, dtype=f32)[None]
    z = jnp.concatenate([t, jnp.cos(fr * w_ang), -jnp.sin(fr * w_ang)], axis=-1)
    z = jnp.pad(z, ((0, 0), (0, LANE - HY_EMB)))
    w1p = jnp.pad(w1.astype(f32), ((0, LANE - HY_EMB), (0, 0)))
    max_decay = math.log(HY_TARGET) / HY_SHORT_PCT
    min_decay = math.log(HY_TARGET) / HY_LONG_PCT
    deltas = jnp.abs(jnp.linspace(min_decay, max_decay, HY_W, dtype=f32)).reshape(1, HY_W)
    full = lambda a: pl.BlockSpec(a.shape, lambda i: (0,) * a.ndim)
    row = lambda v: v.astype(f32).reshape(1, -1)
    args = (z, w1p, row(b1), w2.astype(f32), row(b2), w3.astype(f32), row(b3), w4.astype(f32), row(freq), deltas)
    ow = HY_ORDER * HY_W
    return pl.pallas_call(
        _hy_filter_body,
        grid=(length // tm,),
        in_specs=[pl.BlockSpec((tm, LANE), lambda i: (i, 0))] + [full(a) for a in args[1:]],
        out_specs=[pl.BlockSpec((tm, ow), lambda i: (i, 0)), pl.BlockSpec((tm, ow), lambda i: (i, 0)),
                   pl.BlockSpec((1, ow), lambda i: (0, 0))],
        out_shape=[jax.ShapeDtypeStruct((length, ow), f32), jax.ShapeDtypeStruct((length, ow), f32),
                   jax.ShapeDtypeStruct((1, ow), f32)],
        compiler_params=pltpu.CompilerParams(dimension_semantics=("arbitrary",)),
        name="hy_filter_time",
    )(*args)


def _words(x_bf16):
    return pltpu.bitcast(x_bf16, jnp.uint32)


def _halves(w_u32):
    return pltpu.bitcast(w_u32, jnp.bfloat16)


def _transpose8(parts):
    rows, cols = parts[0].shape
    parts = [p.reshape(rows // SUBLANE, SUBLANE, cols) for p in parts]
    row = lax.broadcasted_iota(jnp.int32, parts[0].shape, 1)
    for s in (1, 2, 4):
        keep = (row & s) == 0
        nxt = list(parts)
        for i in range(SUBLANE):
            if i & s == 0:
                a, b = parts[i], parts[i + s]
                nxt[i] = jnp.where(keep, a, pltpu.roll(b, s, 1))
                nxt[i + s] = jnp.where(keep, pltpu.roll(a, SUBLANE - s, 1), b)
        parts = nxt
    return [p.reshape(rows, cols) for p in parts]


def _gather_tiles(ref, j, n):
    return jnp.concatenate([ref[SUBLANE * g + j] for g in range(n)], axis=0)


def _hy_s1_body(x_ref, f_ref, a_ref):
    groups = x_ref.shape[0] // SUBLANE
    xs = _transpose8([_gather_tiles(x_ref, j, groups) for j in range(SUBLANE)])
    words = []
    for i in range(HY_NB):
        acc = jnp.dot(f_ref[i], xs[i].astype(jnp.bfloat16), preferred_element_type=jnp.float32)
        words.append(_words(acc.astype(jnp.bfloat16)))
    tiles = _transpose8(words)
    per = HY_N1 // SUBLANE
    for j in range(SUBLANE):
        for g in range(per):
            a_ref[per * j + g] = tiles[j][SUBLANE * g:SUBLANE * (g + 1)]


def _hy_s1(x4, sel, fwd):
    assert HY_NB == SUBLANE
    _, half, _, width = x4.shape
    return pl.pallas_call(
        _hy_s1_body,
        grid=(HY_N2 // HY_NB,),
        in_specs=[pl.BlockSpec((None, half, HY_NB, width), lambda j: (sel, 0, j, 0)),
                  pl.BlockSpec((HY_NB, 2 * HY_N1, half), lambda j: (j, 0, 0))],
        out_specs=pl.BlockSpec((HY_N1, HY_NB, width), lambda j: (0, j, 0)),
        out_shape=jax.ShapeDtypeStruct((HY_N1, HY_N2, width), jnp.uint32),
        name="hy_stage1",
    )(x4, fwd)


def _hy_s2f_body(ae_ref, ao_ref, m_ref, sc_ref, hr_ref, hi_ref):
    dot = lambda a, b: jnp.dot(a, b, preferred_element_type=jnp.float32)
    hr_ref[...] = (dot(m_ref[:HY_N2, :], _halves(ae_ref[...])) * sc_ref[...]).astype(hr_ref.dtype)
    hi_ref[...] = (dot(m_ref[HY_N2:, :], _halves(ao_ref[...])) * sc_ref[...]).astype(hi_ref.dtype)


def _hy_s2f(a_e, a_o, m_fwd, scale):
    width = scale.shape[1]
    slab = pl.BlockSpec((None, HY_N2, width), lambda k: (k, 0, 0))
    out = jax.ShapeDtypeStruct((HY_N1, HY_N2, width), jnp.bfloat16)
    return pl.pallas_call(
        _hy_s2f_body,
        grid=(HY_N1,),
        in_specs=[slab, slab, pl.BlockSpec((2 * HY_N2, 2 * HY_N2), lambda k: (0, 0)),
                  pl.BlockSpec((1, width), lambda k: (0, 0))],
        out_specs=[slab, slab],
        out_shape=[out, out],
        name="hy_filter_stage2",
    )(a_e, a_o, m_fwd, scale)


def _hy_s2_body(a_ref, hr_ref, hi_ref, mf_ref, mi_ref, b_ref):
    bf = jnp.bfloat16
    x = jnp.dot(mf_ref[...], _halves(a_ref[...]), preferred_element_type=jnp.float32)
    xr, xi = x[:HY_N2], x[HY_N2:]
    hr = hr_ref[...].astype(jnp.float32)
    hi = hi_ref[...].astype(jnp.float32)
    y = jnp.concatenate([(xr * hr - xi * hi).astype(bf), (xr * hi + xi * hr).astype(bf)], axis=0)
    b = jnp.dot(mi_ref[...], y, preferred_element_type=jnp.float32)
    b_ref[...] = _words(b.astype(bf))


def _hy_s2(a, h_re, h_im, m_fwd, m_inv, order):
    slab = pl.BlockSpec((None, HY_N2, HY_W), lambda k: (k, 0, 0))
    hslab = pl.BlockSpec((None, HY_N2, HY_W), lambda k: (k, 0, order))
    mat = pl.BlockSpec((2 * HY_N2, 2 * HY_N2), lambda k: (0, 0))
    return pl.pallas_call(
        _hy_s2_body,
        grid=(HY_N1,),
        in_specs=[slab, hslab, hslab, mat, mat],
        out_specs=slab,
        out_shape=jax.ShapeDtypeStruct((HY_N1, HY_N2, HY_W), jnp.uint32),
        name="hy_stage2",
    )(a, h_re, h_im, m_fwd, m_inv)


def _hy_is1_body(b_ref, g_ref, z_ref, gate_ref, bias_ref, o_ref):
    per = HY_N1 // SUBLANE
    spectra = _transpose8([jnp.concatenate([b_ref[per * j + g] for g in range(per)], axis=0)
                           for j in range(SUBLANE)])
    convs = [jnp.dot(g_ref[i], _halves(spectra[i]), preferred_element_type=jnp.float32) for i in range(HY_NB)]
    tiles = _transpose8(convs)
    for j in range(SUBLANE):
        for g in range(o_ref.shape[0] // SUBLANE):
            n1 = SUBLANE * g + j
            o_ref[n1] = gate_ref[n1] * (tiles[j][SUBLANE * g:SUBLANE * (g + 1)] + z_ref[n1] * bias_ref[...])


def _hy_is1(b, inv, z4, z_sel, gate4, gate_sel, bias):
    _, half, _, width = z4.shape
    real = lambda sel: pl.BlockSpec((None, half, HY_NB, width), lambda j: (sel, 0, j, 0))
    return pl.pallas_call(
        _hy_is1_body,
        grid=(HY_N2 // HY_NB,),
        in_specs=[pl.BlockSpec((HY_N1, HY_NB, width), lambda j: (0, j, 0)),
                  pl.BlockSpec((HY_NB, half, 2 * HY_N1), lambda j: (j, 0, 0)),
                  real(z_sel), real(gate_sel), pl.BlockSpec((1, width), lambda j: (0, 0))],
        out_specs=pl.BlockSpec((half, HY_NB, width), lambda j: (0, j, 0)),
        out_shape=jax.ShapeDtypeStruct((half, HY_N2, width), jnp.float32),
        name="hy_inv_stage1",
    )(b, inv, z4, gate4, bias)


def hyena_long(p_hy, f_w1, f_b1, f_w2, f_b2, f_w3, f_b3, f_w4, f_freq, f_bias, conv_w, conv_b):
    length = p_hy.shape[0]
    bf = jnp.bfloat16
    half = HY_N1 // 2
    fwd, inv, m_fwd, m_inv = (jnp.asarray(m, bf) for m in _hy_dft_constants(length))
    e, od, nrm = _hy_filter_time(length, f_w1, f_b1, f_w2, f_b2, f_w3, f_b3, f_w4, f_freq, HY_TM)
    ow = HY_ORDER * HY_W
    a_e = _hy_s1(e.reshape(1, half, HY_N2, ow), 0, fwd)
    a_o = _hy_s1(od.reshape(1, half, HY_N2, ow), 0, fwd)
    scale = 1.0 / (nrm * (2.0 * length))
    h_re, h_im = _hy_s2f(a_e, a_o, m_fwd, scale)
    u = _hy_conv(p_hy, conv_w, conv_b, HY_TM).reshape(3, half, HY_N2, HY_W)
    z, z_sel = u, 0
    for order in range(HY_ORDER):
        b = _hy_s2(_hy_s1(z, z_sel, fwd), h_re, h_im, m_fwd, m_inv, order)
        z = _hy_is1(b, inv, z, z_sel, u, 1 + order, f_bias[order].astype(jnp.float32).reshape(1, HY_W))[None]
        z_sel = 0
    return z.reshape(length, HY_W)


def even_mixer(h, hc, w_in, conv_w, conv_b, f_w1, f_b1, f_w2, f_b2, f_w3, f_b3, f_w4, f_freq, f_bias,
               q_norm, k_norm, sink, ctx_out):
    f32 = jnp.float32
    bsz, length, _ = h.shape
    n_ctx = hc.shape[1]
    filt = (f_w1, f_b1, f_w2, f_b2, f_w3, f_b3, f_w4, f_freq)
    o_v = ATT_KV_W
    o_q = 2 * ATT_KV_W
    o_hy = o_q + ATT_Q_W
    w_perm = jnp.concatenate([w_in[:, o_hy:], w_in[:, o_q:o_hy], w_in[:, :o_q]], axis=1)
    hy_w = 3 * HY_W
    p_hy = pmm3(h, w_perm).astype(f32)
    p = jnp.concatenate([p_hy[..., hy_w + ATT_Q_W:], p_hy[..., hy_w:hy_w + ATT_Q_W]], axis=-1)
    pc = pmm3(hc, w_in if ctx_out else w_in[:, :o_q]).astype(f32)

    def heads(a, n_heads):
        return a.reshape(a.shape[0], a.shape[1], n_heads, HEAD_DIM)

    k_c = rms_norm(heads(pc[..., :o_v], ATT_KV_HEADS), k_norm)
    v_c = heads(pc[..., o_v:o_q], ATT_KV_HEADS)
    rope = axial_rope_tables(length)
    k = apply_axial_rope(rms_norm(heads(p[..., :o_v], ATT_KV_HEADS), k_norm), rope)
    v = heads(p[..., o_v:o_q], ATT_KV_HEADS)
    q = apply_axial_rope(rms_norm(heads(p[..., o_q:o_hy], ATT_HEADS), q_norm), rope)
    att = window_attention(q.reshape(bsz, length, ATT_KV_HEADS, ATT_GROUP, HEAD_DIM), k, v, k_c, v_c, sink)
    hy = hyena_long(p_hy[0], *filt, f_bias, conv_w, conv_b)[None]
    out = jnp.concatenate([hy, att], axis=-1).astype(h.dtype)
    if not ctx_out:
        return out, None
    q_c = rms_norm(heads(pc[..., o_q:o_hy], ATT_HEADS), q_norm).reshape(bsz, n_ctx, ATT_KV_HEADS, ATT_GROUP, HEAD_DIM)
    att_c = context_attention(q_c, k_c, v_c, sink)
    hy_c = hyena_mix(pc[..., o_hy:], hyena_filters(n_ctx, *filt), f_bias, conv_w, conv_b)
    return out, jnp.concatenate([hy_c, att_c], axis=-1).astype(hc.dtype)


ATT_TQ = ATT_BLOCK
EVEN_TM = 256
EV_Q_BLK = 3 * HY_W // ATT_Q_W
EV_K_BLK = (3 * HY_W + ATT_Q_W) // ATT_KV_W
EV_V_BLK = EV_K_BLK + 1


def _rope_tables(length):
    cr, sr, cc, sc = axial_rope_tables(length)
    return jnp.concatenate([cr, cr, cc, cc], axis=-1), jnp.concatenate([-sr, sr, -sc, sc], axis=-1)


def _head_norm_rope(x, g_row, c, s, seg):
    sq = x * x
    hi = sq.astype(jnp.bfloat16)
    lo = (sq - hi.astype(jnp.float32)).astype(jnp.bfloat16)
    ms = (jnp.dot(hi, seg, preferred_element_type=jnp.float32) + jnp.dot(lo, seg, preferred_element_type=jnp.float32))
    y = x * lax.rsqrt(ms + EPS) * g_row
    width = x.shape[1]
    quarter = HEAD_DIM // 4
    lane = lax.broadcasted_iota(jnp.int32, x.shape, 1)
    partner = jnp.where((lane & quarter) == 0, pltpu.roll(y, width - quarter, 1), pltpu.roll(y, quarter, 1))
    return y * c + partner * s


def _qk_prep_body(q_ref, k_ref, c_ref, s_ref, qn_ref, kn_ref, segq_ref, segk_ref, qo_ref, ko_ref):
    c, s = c_ref[...], s_ref[...]
    tile = lambda t, n: jnp.concatenate([t] * n, axis=1)
    qo_ref[...] = _head_norm_rope(q_ref[...], tile(qn_ref[...], ATT_HEADS), tile(c, ATT_HEADS), tile(s, ATT_HEADS),
                                  segq_ref[...]).astype(qo_ref.dtype)
    ko_ref[...] = _head_norm_rope(k_ref[...], tile(kn_ref[...], ATT_KV_HEADS), tile(c, ATT_KV_HEADS),
                                  tile(s, ATT_KV_HEADS), segk_ref[...]).astype(ko_ref.dtype)


def _qk_prep(p, rope_c, rope_s, q_norm, k_norm, tm):
    import numpy as np
    rows = p.shape[0]
    seg = lambda w: jnp.asarray(np.kron(np.eye(w // HEAD_DIM), np.full((HEAD_DIM, HEAD_DIM), 1.0 / HEAD_DIM)),
                                jnp.bfloat16)
    const = lambda shape: pl.BlockSpec(shape, lambda i: (0, 0))
    return pl.pallas_call(
        _qk_prep_body,
        grid=(rows // tm,),
        in_specs=[pl.BlockSpec((tm, ATT_Q_W), lambda i: (i, EV_Q_BLK)),
                  pl.BlockSpec((tm, ATT_KV_W), lambda i: (i, EV_K_BLK)),
                  pl.BlockSpec((tm, HEAD_DIM), lambda i: (i, 0)), pl.BlockSpec((tm, HEAD_DIM), lambda i: (i, 0)),
                  const((1, HEAD_DIM)), const((1, HEAD_DIM)),
                  const((ATT_Q_W, ATT_Q_W)), const((ATT_KV_W, ATT_KV_W))],
        out_specs=[pl.BlockSpec((tm, ATT_Q_W), lambda i: (i, 0)), pl.BlockSpec((tm, ATT_KV_W), lambda i: (i, 0))],
        out_shape=[jax.ShapeDtypeStruct((rows, ATT_Q_W), jnp.bfloat16),
                   jax.ShapeDtypeStruct((rows, ATT_KV_W), jnp.bfloat16)],
        name="qk_prep",
    )(p, p, rope_c, rope_s, q_norm.astype(jnp.float32).reshape(1, -1), k_norm.astype(jnp.float32).reshape(1, -1),
      seg(ATT_Q_W), seg(ATT_KV_W))


def _att_body(sink_ref, q_ref, kc_ref, vc_ref, *rest, local):
    if local:
        kp_ref, k0_ref, kn_ref, vp_ref, v0_ref, vn_ref, o_ref = rest
    else:
        (o_ref,) = rest
    b = pl.program_id(0)
    nb = pl.num_programs(0)
    bf = jnp.bfloat16
    scale = HEAD_DIM ** -0.5
    q = q_ref[...]
    kc = kc_ref[...]
    vc = vc_ref[...].astype(bf)
    if local:
        kb = jnp.concatenate([kp_ref[...], k0_ref[...], kn_ref[...]], axis=0)
        vb = jnp.concatenate([vp_ref[...], v0_ref[...], vn_ref[...]], axis=0).astype(bf)
        i = lax.broadcasted_iota(jnp.int32, (ATT_TQ, 3 * ATT_BLOCK), 0)
        j = lax.broadcasted_iota(jnp.int32, (ATT_TQ, 3 * ATT_BLOCK), 1)
        rel = j - ATT_BLOCK - i
        valid = ((jnp.abs(rel) <= ATT_WINDOW) & ((b > 0) | (j >= ATT_BLOCK))
                 & ((b < nb - 1) | (j < 2 * ATT_BLOCK)))
    for h in range(ATT_HEADS):
        hk = h // ATT_GROUP
        qh = q[:, h * HEAD_DIM:(h + 1) * HEAD_DIM]
        kv_cols = slice(hk * HEAD_DIM, (hk + 1) * HEAD_DIM)
        sink = sink_ref[h]
        s_ctx = _dot_nt(qh, kc[:, kv_cols]) * scale
        m = jnp.maximum(jnp.max(s_ctx, axis=-1, keepdims=True), sink)
        if local:
            s_loc = jnp.where(valid, _dot_nt(qh, kb[:, kv_cols]) * scale, -jnp.inf)
            m = jnp.maximum(m, jnp.max(s_loc, axis=-1, keepdims=True))
        p_ctx = jnp.exp(s_ctx - m)
        den = jnp.sum(p_ctx, axis=-1, keepdims=True) + jnp.exp(sink - m)
        acc = jnp.dot(p_ctx.astype(bf), vc[:, kv_cols], preferred_element_type=jnp.float32)
        if local:
            p_loc = jnp.exp(s_loc - m)
            den = den + jnp.sum(p_loc, axis=-1, keepdims=True)
            acc = acc + jnp.dot(p_loc.astype(bf), vb[:, kv_cols], preferred_element_type=jnp.float32)
        o_ref[:, h * HEAD_DIM:(h + 1) * HEAD_DIM] = acc / den


def _attention(qn, kn, p, kcn, pc, sink, local):
    rows = qn.shape[0]
    nb = rows // ATT_TQ
    n_ctx = kcn.shape[0]
    specs = [pl.BlockSpec((ATT_TQ, ATT_Q_W), lambda b, s: (b, 0)),
             pl.BlockSpec((n_ctx, ATT_KV_W), lambda b, s: (0, 0)),
             pl.BlockSpec((n_ctx, ATT_KV_W), lambda b, s: (0, EV_V_BLK))]
    args = [qn, kcn, pc]
    if local:
        prev = lambda b, s: jnp.maximum(b - 1, 0)
        nxt = lambda b, s: jnp.minimum(b + 1, nb - 1)
        for col, arr in ((0, kn), (EV_V_BLK, p)):
            specs += [pl.BlockSpec((ATT_BLOCK, ATT_KV_W), lambda b, s, col=col: (prev(b, s), col)),
                      pl.BlockSpec((ATT_BLOCK, ATT_KV_W), lambda b, s, col=col: (b, col)),
                      pl.BlockSpec((ATT_BLOCK, ATT_KV_W), lambda b, s, col=col: (nxt(b, s), col))]
            args += [arr, arr, arr]
    return pl.pallas_call(
        functools.partial(_att_body, local=local),
        grid_spec=pltpu.PrefetchScalarGridSpec(
            num_scalar_prefetch=1, grid=(nb,), in_specs=specs,
            out_specs=pl.BlockSpec((ATT_TQ, ATT_Q_W), lambda b, s: (b, 0))),
        out_shape=jax.ShapeDtypeStruct((rows, ATT_Q_W), jnp.float32),
        name="window_attention" if local else "context_attention",
    )(sink.astype(jnp.float32), *args)


def _even_merge_body(hy_ref, att_ref, x_ref, gt_ref, w_ref, o_ref):
    bf = jnp.bfloat16
    m = (jnp.dot(hy_ref[...].astype(bf), w_ref[:HY_W, :].astype(bf), preferred_element_type=jnp.float32)
         + jnp.dot(att_ref[...].astype(bf), w_ref[HY_W:, :].astype(bf), preferred_element_type=jnp.float32))
    o_ref[...] = x_ref[...] + gt_ref[...] * m


def _even_merge(hy, att, x2d, gt, w_out, tm):
    rows, d = x2d.shape
    half = pl.BlockSpec((tm, HY_W), lambda i: (i, 0))
    return pl.pallas_call(
        _even_merge_body,
        grid=(rows // tm,),
        in_specs=[half, half, pl.BlockSpec((tm, d), lambda i: (i, 0)), pl.BlockSpec((1, d), lambda i: (0, 0)),
                  pl.BlockSpec((MIX_W, d), lambda i: (0, 0))],
        out_specs=pl.BlockSpec((tm, d), lambda i: (i, 0)),
        out_shape=jax.ShapeDtypeStruct((rows, d), jnp.float32),
        name="even_merge",
    )(hy, att, x2d, gt.reshape(1, -1), w_out)


def _hy_short_body(e_ref, o_ref, sc_ref, u_ref, bias_ref, cf_ref, sf_ref, out_ref):
    bf = jnp.bfloat16
    cf, sf = cf_ref[...], sf_ref[...]
    dot = lambda a, b: jnp.dot(a, b.astype(bf), preferred_element_type=jnp.float32)
    z = u_ref[0]
    for order in range(HY_ORDER):
        cols = slice(order * HY_W, (order + 1) * HY_W)
        h_re = dot(cf, e_ref[:, cols]) * sc_ref[:, cols]
        h_im = dot(sf, o_ref[:, cols]) * sc_ref[:, cols]
        x_re, x_im = dot(cf, z), dot(sf, z)
        y_re = (x_re * h_re - x_im * h_im).astype(bf)
        y_im = (x_re * h_im + x_im * h_re).astype(bf)
        conv = _dot_tn(cf, y_re) + _dot_tn(sf, y_im)
        z = u_ref[1 + order] * (conv + z * bias_ref[order:order + 1, :])
    out_ref[...] = z


def hyena_short(pc, f_w1, f_b1, f_w2, f_b2, f_w3, f_b3, f_w4, f_freq, f_bias, conv_w, conv_b):
    import numpy as np
    rows = pc.shape[0]
    e, od, nrm = _hy_filter_time(rows, f_w1, f_b1, f_w2, f_b2, f_w3, f_b3, f_w4, f_freq, rows)
    u = _hy_conv(pc, conv_w, conv_b, rows)
    ang = 2 * np.pi * np.outer(np.arange(2 * rows), np.arange(rows)) / (2 * rows)
    cf, sf = jnp.asarray(np.cos(ang), jnp.bfloat16), jnp.asarray(-np.sin(ang), jnp.bfloat16)
    scale = 1.0 / (nrm * (2.0 * rows))
    return pl.pallas_call(
        _hy_short_body,
        out_shape=jax.ShapeDtypeStruct((rows, HY_W), jnp.float32),
        name="hy_short",
    )(e, od, scale, u, f_bias.astype(jnp.float32), cf, sf)


def even_layer(x, xc, g, mod, mod_c, w_in, conv_w, conv_b, f_w1, f_b1, f_w2, f_b2, f_w3, f_b3, f_w4, f_freq, f_bias,
               q_norm, k_norm, sink, gt, gt_c, w_out):
    f32 = jnp.float32
    length, n_ctx = x.shape[1], xc.shape[1]
    filt = (f_w1, f_b1, f_w2, f_b2, f_w3, f_b3, f_w4, f_freq, f_bias, conv_w, conv_b)
    o_q = 2 * ATT_KV_W
    o_hy = o_q + ATT_Q_W
    w_perm = jnp.concatenate([w_in[:, o_hy:], w_in[:, o_q:o_hy], w_in[:, :o_q]], axis=1).astype(jnp.bfloat16)
    twice = lambda v: jnp.stack([v.reshape(-1), v.reshape(-1)])
    p = norm_proj(x[0], g, twice(mod[0]), twice(mod[1]), w_perm)
    pc = norm_proj(xc[0], g, twice(mod_c[0]), twice(mod_c[1]), w_perm)
    rope_c, rope_s = _rope_tables(length)
    qn, kn = _qk_prep(p, rope_c, rope_s, q_norm, k_norm, EVEN_TM)
    qcn, kcn = _qk_prep(pc, jnp.ones((n_ctx, HEAD_DIM), f32), jnp.zeros((n_ctx, HEAD_DIM), f32), q_norm, k_norm,
                        n_ctx)
    att = _attention(qn, kn, p, kcn, pc, sink, True)
    att_c = _attention(qcn, None, None, kcn, pc, sink, False)
    hy = hyena_long(p, *filt)
    hy_c = hyena_short(pc, *filt)
    x_new = _even_merge(hy, att, x[0], gt, w_out, EVEN_TM)
    xc_new = _even_merge(hy_c, att_c, xc[0], gt_c, w_out, n_ctx)
    return x_new[None], xc_new[None]


def odd_mixer(h, hc, lb, w_in, conv_w, conv_b, dt_bias, a_log, d_skip, ssd_norm, hg_norm, ctx_out):
    f32 = jnp.float32
    bsz, length, _ = h.shape
    n_ctx = hc.shape[1]
    p = pmm3(h, w_in).astype(f32)
    pc = pmm3(hc, w_in if ctx_out else w_in[:, :ODD_STATE_COLS]).astype(f32)
    o_dt = SSD_CONV_CH
    o_f = SSD_CONV_CH + 2 * SSD_HEADS
    o_i = o_f + 2 * HG_W
    o_z = ODD_STATE_COLS
    o_q = o_z + SSD_W
    o_g = o_q + HG_W
    gn = SSD_GROUPS * SSD_STATE

    def streams(pp):
        n = pp.shape[1]
        xbc = jax.nn.silu(dwconv_centred(pp[..., :SSD_CONV_CH], conv_w, conv_b))
        xs = xbc[..., :SSD_W].reshape(bsz, n, SSD_HEADS, SSD_HEAD_DIM)
        bm = xbc[..., SSD_W:SSD_W + gn].reshape(bsz, n, SSD_GROUPS, SSD_STATE)
        cm = xbc[..., SSD_W + gn:].reshape(bsz, n, SSD_GROUPS, SSD_STATE)
        dt_raw = pp[..., o_dt:o_f].reshape(bsz, n, 2, SSD_HEADS)
        f_raw = pp[..., o_f:o_i].reshape(bsz, n, 2, HG_HEADS, HG_EXPAND)
        iv = pp[..., o_i:o_i + HG_W].reshape(bsz, n, HG_HEADS, HG_VDIM)
        return xs, bm, cm, dt_raw, f_raw, iv

    xs, bm, cm, dt_raw, f_raw, iv = streams(p)
    xs_c, bm_c, cm_c, dt_raw_c, f_raw_c, iv_c = streams(pc)
    q = jax.nn.silu(p[..., o_q:o_g]).reshape(bsz, length, HG_HEADS, HG_EXPAND)
    q_c = jax.nn.silu(pc[..., o_q:o_g]).reshape(bsz, n_ctx, HG_HEADS, HG_EXPAND) if ctx_out else None
    lb = lb.astype(f32).reshape(HG_HEADS, HG_EXPAND)
    ssd0 = jnp.zeros((bsz, SSD_GROUPS, SSD_HEADS // SSD_GROUPS, SSD_HEAD_DIM, SSD_STATE), f32)
    hg0 = jnp.zeros((bsz, HG_HEADS, HG_EXPAND, HG_VDIM), f32)
    y_dirs, o_dirs, yc_dirs, oc_dirs = [], [], [], []
    for d in range(2):
        fl = (lambda a: jnp.flip(a, axis=1)) if d == 1 else (lambda a: a)
        a = -jnp.exp(a_log[d].astype(f32))
        dsk = d_skip[d].astype(f32)
        dtb = dt_bias[d].astype(f32)
        dt_l = jax.nn.softplus(dt_raw[:, :, d] + dtb)
        dt_c = jax.nn.softplus(dt_raw_c[:, :, d] + dtb)
        yc, s_ctx = ssd_scan(fl(xs_c), fl(dt_c), a, fl(bm_c), fl(cm_c), dsk, ssd0, ctx_out)
        yl, _ = ssd_scan(fl(xs), fl(dt_l), a, fl(bm), fl(cm), dsk, s_ctx, True)
        y_dirs.append(fl(yl))
        f_l = lb + (1.0 - lb) * jax.nn.sigmoid(f_raw[:, :, d])
        f_c = lb + (1.0 - lb) * jax.nn.sigmoid(f_raw_c[:, :, d])
        oc, s_hg = hgrn2_scan(fl(q_c) if ctx_out else None, fl(1.0 - f_c), fl(iv_c), fl(jnp.log(f_c)), hg0, ctx_out)
        ol, _ = hgrn2_scan(fl(q), fl(1.0 - f_l), fl(iv), fl(jnp.log(f_l)), s_hg, True)
        o_dirs.append(fl(ol))
        if ctx_out:
            yc_dirs.append(fl(yc))
            oc_dirs.append(fl(oc))

    def merge(yy, oo, pp, n):
        z = pp[..., o_z:o_q]
        g = pp[..., o_g:]
        ys = (yy.reshape(bsz, n, SSD_W) * jax.nn.silu(z)).reshape(bsz, n, SSD_GROUPS, SSD_W // SSD_GROUPS)
        ys = rms_norm(ys, ssd_norm.reshape(SSD_GROUPS, SSD_W // SSD_GROUPS)).reshape(bsz, n, SSD_W)
        hs = rms_norm(oo, hg_norm.reshape(HG_HEADS, HG_VDIM)).reshape(bsz, n, HG_W) * jax.nn.silu(g)
        return jnp.concatenate([ys, hs], axis=-1)

    out = merge(y_dirs[0] + y_dirs[1], o_dirs[0] + o_dirs[1], p, length).astype(h.dtype)
    if not ctx_out:
        return out, None
    out_c = merge(yc_dirs[0] + yc_dirs[1], oc_dirs[0] + oc_dirs[1], pc, n_ctx).astype(hc.dtype)
    return out, out_c


SCAN_Q = 128
SCAN_LEVELS = 7
ODD_COLS = SSD_CONV_CH + 8 * 512
HALO = SUBLANE


def _scan_constants():
    import numpy as np
    q = SCAN_Q
    d_hg, d_ssd, pairs, laters = [], [], [], []
    for direction in (0, 1):
        pos = np.arange(q) if direction == 0 else q - 1 - np.arange(q)
        pj, pt = pos[None, :], pos[:, None]
        top = pj <= pt
        end = pj > pt
        ones = np.ones((SUBLANE, q), bool)
        lv, pr, lt = [], [], []
        for level in range(SCAN_LEVELS):
            b = 2 ** level
            start = (pos // (2 * b)) * (2 * b)
            mid = (start + b)[:, None]
            later = pos >= start + b
            lv.append(np.where(later[:, None], (pj >= mid) & (pj <= pt), (pj > pt) & (pj < mid)))
            pr.append((start[:, None] == start[None, :]) & later[:, None] & ~later[None, :])
            lt.append(np.broadcast_to(later[:, None], (q, LANE)))
        pr.append(np.eye(q, dtype=bool))
        pr.append(top)
        d_hg.append(np.concatenate([top, end] + lv + [ones], axis=0))
        d_ssd.append(np.concatenate([top, end, ones], axis=0))
        pairs.append(np.stack(pr))
        laters.append(np.stack(lt))
    f = np.float32
    twice = lambda m: np.concatenate([m, m], axis=-1)
    return (twice(np.stack(d_hg)).astype(f), twice(np.stack(d_ssd)).astype(f), np.stack(pairs).astype(f),
            np.stack(laters).astype(f))


def _split_dot(mm_bf16, v):
    hi = v.astype(jnp.bfloat16)
    lo = (v - hi.astype(jnp.float32)).astype(jnp.bfloat16)
    return jnp.dot(mm_bf16, jnp.concatenate([hi, lo], axis=0), preferred_element_type=jnp.float32)


def _dot_nt(a, b):
    return lax.dot_general(a, b, (((1,), (1,)), ((), ())), preferred_element_type=jnp.float32)


def _dot_tn(a, b):
    return lax.dot_general(a, b, (((0,), (0,)), ((), ())), preferred_element_type=jnp.float32)


def _softplus(x):
    return jnp.maximum(x, 0.0) + jnp.log1p(jnp.exp(-jnp.abs(x)))


def _scan_body(xbc_ref, prev_ref, next_ref, f_ref, iv_ref, q_ref, dt_ref,
               cw_ref, cb_ref, dtb_ref, a_ref, dsk_ref, lb_ref,
               dhg_ref, dssd_ref, pair_ref, later_ref, sel_ref,
               out_ref, s_ssd, s_hg, ydiag_ref, *, n_ctx_chunks, n_chunks):
    d = pl.program_id(0)
    j = pl.program_id(1)
    q_rows = SCAN_Q
    bf = jnp.bfloat16

    @pl.when(j == 0)
    def _():
        s_ssd[...] = jnp.zeros_like(s_ssd)
        s_hg[...] = jnp.zeros_like(s_hg)

    c = jnp.where(d == 0, j, jnp.where(j < n_ctx_chunks, n_ctx_chunks - 1 - j, n_chunks - 1 + n_ctx_chunks - j))
    first = (c == 0) | (c == n_ctx_chunks)
    last = (c == n_ctx_chunks - 1) | (c == n_chunks - 1)

    u = xbc_ref[...]
    row = lax.broadcasted_iota(jnp.int32, u.shape, 0)
    before = jnp.where(first, 0.0, prev_ref[HALO - 1:HALO, :])
    after = jnp.where(last, 0.0, next_ref[0:1, :])
    up = jnp.where(row == 0, before, pltpu.roll(u, 1, 0))
    un = jnp.where(row == q_rows - 1, after, pltpu.roll(u, q_rows - 1, 0))
    xbc = cw_ref[0:1, :] * up + cw_ref[1:2, :] * u + cw_ref[2:3, :] * un + cb_ref[...]
    xbc = xbc * jax.nn.sigmoid(xbc)
    x = xbc[:, :SSD_W]

    dt = _softplus(dt_ref[...] + dtb_ref[...])
    da = dt * a_ref[...]
    r = _split_dot(dssd_ref[...], da)
    cs, to_end, total = r[:q_rows], r[q_rows:2 * q_rows], r[2 * q_rows:2 * q_rows + 1]
    xdt = x * dt
    cs_hi = cs.astype(bf)
    cs_lo = (cs - cs_hi.astype(jnp.float32)).astype(bf)
    cs_rows = _dot_nt(sel_ref[...], cs_hi) + _dot_nt(sel_ref[...], cs_lo)
    l_mask = pair_ref[SCAN_LEVELS + 1]
    decay_in = jnp.exp(cs)
    w_end = (jnp.exp(to_end) * xdt).astype(bf)
    gn = SSD_GROUPS * SSD_STATE
    hpg = SSD_HEADS // SSD_GROUPS
    gw = hpg * SSD_HEAD_DIM
    for g in range(SSD_GROUPS):
        b_g = xbc[:, SSD_W + g * SSD_STATE:SSD_W + (g + 1) * SSD_STATE].astype(bf)
        c_g = xbc[:, SSD_W + gn + g * SSD_STATE:SSD_W + gn + (g + 1) * SSD_STATE].astype(bf)
        scores = _dot_nt(c_g, b_g)
        y_off = jnp.dot(c_g, s_ssd[g].astype(bf), preferred_element_type=jnp.float32)
        for hh in range(hpg):
            h = g * hpg + hh
            lo = h * SSD_HEAD_DIM
            diff = cs[:, lo:lo + 1] - cs_rows[h:h + 1, :]
            decay = jnp.exp(jnp.minimum(diff, 0.0)) * l_mask
            ydiag_ref[:, lo:lo + SSD_HEAD_DIM] = jnp.dot((scores * decay).astype(bf),
                                                         xdt[:, lo:lo + SSD_HEAD_DIM].astype(bf),
                                                         preferred_element_type=jnp.float32)
        cols = slice(g * gw, (g + 1) * gw)
        out_ref[:, cols] = (ydiag_ref[:, cols] + decay_in[:, cols] * y_off + dsk_ref[:, cols] * x[:, cols])
        s_ssd[g] = jnp.exp(total[:, cols]) * s_ssd[g] + _dot_tn(b_g, w_end[:, cols])

    lb = lb_ref[...]
    f = lb + (1.0 - lb) * jax.nn.sigmoid(f_ref[...])
    k_in = 1.0 - f
    qv = q_ref[...]
    qv = qv * jax.nn.sigmoid(qv)
    v_bf = iv_ref[...].astype(bf)
    e = jnp.exp(_split_dot(dhg_ref[...], jnp.log(f)))
    e_top, e_end = e[:q_rows], e[q_rows:2 * q_rows]
    e_tot = e[(2 + SCAN_LEVELS) * q_rows:(2 + SCAN_LEVELS) * q_rows + 1]
    for h in range(HG_HEADS):
        cols = slice(h * HG_EXPAND, (h + 1) * HG_EXPAND)
        q_h, k_h = qv[:, cols], k_in[:, cols]
        att = pair_ref[SCAN_LEVELS] * _dot_nt(q_h.astype(bf), k_h.astype(bf))
        for level in range(SCAN_LEVELS):
            e_l = e[(2 + level) * q_rows:(3 + level) * q_rows, cols]
            later = later_ref[level]
            q_l = (q_h * e_l * later).astype(bf)
            k_l = (k_h * e_l * (1.0 - later)).astype(bf)
            att = att + pair_ref[level] * _dot_nt(q_l, k_l)
        o = jnp.dot(att.astype(bf), v_bf[:, cols], preferred_element_type=jnp.float32)
        o = o + _dot_nt((q_h * e_top[:, cols]).astype(bf), s_hg[h].astype(bf))
        out_ref[:, SSD_W + h * HG_VDIM:SSD_W + (h + 1) * HG_VDIM] = o
        s_hg[h] = e_tot[:, cols] * s_hg[h] + _dot_tn(v_bf[:, cols], (k_h * e_end[:, cols]).astype(bf))


def _odd_scan(p_all, conv_w, conv_b, dtb, a_cols, dsk, lb, n_ctx):
    n_rows = p_all.shape[0]
    n_chunks = n_rows // SCAN_Q
    ncc = n_ctx // SCAN_Q
    d_hg, d_ssd, pairs, laters = _scan_constants()
    bf = jnp.bfloat16
    import numpy as np
    sel = np.zeros((LANE, SSD_W), np.float32)
    sel[np.arange(SSD_HEADS), np.arange(SSD_HEADS) * SSD_HEAD_DIM] = 1.0

    def chunk(d, j):
        return jnp.where(d == 0, j, jnp.where(j < ncc, ncc - 1 - j, n_chunks - 1 + ncc - j))

    per = SCAN_Q // HALO
    last_halo = n_rows // HALO - 1
    col512 = lambda blk: (lambda d, j: (chunk(d, j), blk))
    const2 = lambda shape: pl.BlockSpec(shape, lambda d, j: (0,) * len(shape))
    dirc = lambda shape: pl.BlockSpec((None,) + shape, lambda d, j: (d,) + (0,) * len(shape))
    body = functools.partial(_scan_body, n_ctx_chunks=ncc, n_chunks=n_chunks)
    return pl.pallas_call(
        body,
        grid=(2, n_chunks),
        in_specs=[
            pl.BlockSpec((SCAN_Q, SSD_CONV_CH), lambda d, j: (chunk(d, j), 0)),
            pl.BlockSpec((HALO, SSD_CONV_CH), lambda d, j: (jnp.maximum(chunk(d, j) * per - 1, 0), 0)),
            pl.BlockSpec((HALO, SSD_CONV_CH), lambda d, j: (jnp.minimum((chunk(d, j) + 1) * per, last_halo), 0)),
            pl.BlockSpec((SCAN_Q, 512), lambda d, j: (chunk(d, j), 2 + d)),
            pl.BlockSpec((SCAN_Q, 512), col512(4)),
            pl.BlockSpec((SCAN_Q, 512), col512(6)),
            pl.BlockSpec((SCAN_Q, 512), lambda d, j: (chunk(d, j), 8 + d)),
            const2((CONV_W, SSD_CONV_CH)), const2((1, SSD_CONV_CH)),
            dirc((1, SSD_W)), dirc((1, SSD_W)), dirc((1, SSD_W)), const2((1, HG_W)),
            dirc(d_hg.shape[1:]), dirc(d_ssd.shape[1:]), dirc(pairs.shape[1:]), dirc(laters.shape[1:]),
            const2((LANE, SSD_W)),
        ],
        out_specs=pl.BlockSpec((None, SCAN_Q, MIX_W), lambda d, j: (d, chunk(d, j), 0)),
        out_shape=jax.ShapeDtypeStruct((2, n_rows, MIX_W), jnp.float32),
        scratch_shapes=[pltpu.VMEM((SSD_GROUPS, SSD_STATE, SSD_W // SSD_GROUPS), jnp.float32),
                        pltpu.VMEM((HG_HEADS, HG_VDIM, HG_EXPAND), jnp.float32),
                        pltpu.VMEM((SCAN_Q, SSD_W), jnp.float32)],
        compiler_params=pltpu.CompilerParams(dimension_semantics=("arbitrary", "arbitrary"),
                                             vmem_limit_bytes=MOE_VMEM_LIMIT),
        name="odd_scan",
    )(p_all, p_all, p_all, p_all, p_all, p_all, p_all,
      conv_w, conv_b.reshape(1, -1), dtb, a_cols, dsk, lb.reshape(1, -1),
      jnp.asarray(d_hg, bf), jnp.asarray(d_ssd, bf), jnp.asarray(pairs), jnp.asarray(laters),
      jnp.asarray(sel, bf))


def _group_rms(v, width):
    parts = []
    for lo in range(0, v.shape[1], width):
        seg = v[:, lo:lo + width]
        parts.append(seg * lax.rsqrt(jnp.mean(seg * seg, axis=-1, keepdims=True) + EPS))
    return jnp.concatenate(parts, axis=1)


def _odd_merge_body(yo_ref, z_ref, g_ref, x_ref, sn_ref, hn_ref, gt_ref, w_ref, o_ref):
    yo = yo_ref[0] + yo_ref[1]
    z = z_ref[...]
    g = g_ref[...]
    ys = _group_rms(yo[:, :SSD_W] * (z * jax.nn.sigmoid(z)), SSD_W // SSD_GROUPS) * sn_ref[...]
    hs = _group_rms(yo[:, SSD_W:], HG_VDIM) * hn_ref[...] * (g * jax.nn.sigmoid(g))
    m = jnp.concatenate([ys, hs], axis=1).astype(jnp.bfloat16)
    o_ref[...] = x_ref[...] + gt_ref[...] * jnp.dot(m, w_ref[...].astype(jnp.bfloat16),
                                                    preferred_element_type=jnp.float32)


ODD_TM = 256


def _odd_merge(yo, p_all, x2d, ssd_norm, hg_norm, gt, w_out, n_ctx):
    n_lat, d = x2d.shape
    skip = n_ctx // ODD_TM
    return pl.pallas_call(
        _odd_merge_body,
        grid=(n_lat // ODD_TM,),
        in_specs=[pl.BlockSpec((2, ODD_TM, MIX_W), lambda i: (0, i + skip, 0)),
                  pl.BlockSpec((ODD_TM, 512), lambda i: (i + skip, 5)),
                  pl.BlockSpec((ODD_TM, 512), lambda i: (i + skip, 7)),
                  pl.BlockSpec((ODD_TM, d), lambda i: (i, 0)),
                  pl.BlockSpec((1, SSD_W), lambda i: (0, 0)),
                  pl.BlockSpec((1, HG_W), lambda i: (0, 0)),
                  pl.BlockSpec((1, d), lambda i: (0, 0)),
                  pl.BlockSpec((MIX_W, d), lambda i: (0, 0))],
        out_specs=pl.BlockSpec((ODD_TM, d), lambda i: (i, 0)),
        out_shape=jax.ShapeDtypeStruct((n_lat, d), jnp.float32),
        compiler_params=pltpu.CompilerParams(dimension_semantics=("arbitrary",)),
        name="odd_merge",
    )(yo, p_all, p_all, x2d, ssd_norm.reshape(1, -1), hg_norm.reshape(1, -1), gt.reshape(1, -1), w_out)


def odd_layer(x, xc, g, mod, mod_c, lb, w_in, conv_w, conv_b, dt_bias, a_log, d_skip, ssd_norm, hg_norm, gt, w_out):
    f32 = jnp.float32
    n_ctx = xc.shape[1]
    o_dt = SSD_CONV_CH
    o_f = o_dt + 2 * SSD_HEADS
    rep = lambda v: jnp.repeat(v, SSD_HEAD_DIM, axis=-1)
    w_perm = jnp.concatenate([w_in[:, :o_dt], w_in[:, o_f:], rep(w_in[:, o_dt:o_dt + SSD_HEADS]),
                              rep(w_in[:, o_dt + SSD_HEADS:o_f])], axis=1).astype(jnp.bfloat16)
    x_all = jnp.concatenate([xc[0], x[0]], axis=0)
    both = lambda a, b: jnp.stack([a.reshape(-1), b.reshape(-1)])
    p_all = norm_proj(x_all, g, both(mod_c[0], mod[0]), both(mod_c[1], mod[1]), w_perm, n_first=n_ctx)
    dtb = rep(dt_bias.astype(f32)).reshape(2, 1, SSD_W)
    a_cols = rep(-jnp.exp(a_log.astype(f32))).reshape(2, 1, SSD_W)
    dsk = rep(d_skip.astype(f32)).reshape(2, 1, SSD_W)
    yo = _odd_scan(p_all, conv_w, conv_b, dtb, a_cols, dsk, lb.astype(f32), n_ctx)
    return _odd_merge(yo, p_all, x[0], ssd_norm, hg_norm, gt, w_out, n_ctx)[None]


MOE_TM = 256
MOE_BM = 256
NEG_BIG = -1e30
MOE_ISSUE_UNROLL = 4
MOE_VMEM_LIMIT = 52 * 1024 * 1024


def _route_body(x_ref, g_ref, sh_ref, sc_ref, rw_ref, rb_ref, cnt0_ref,
                h_ref, idx_ref, gate_ref, rank_ref, cnt_ref, run_ref):
    i = pl.program_id(0)

    @pl.when(i == 0)
    def _():
        run_ref[...] = cnt0_ref[...]

    tm = x_ref.shape[0]
    x = x_ref[...]
    t = (x * lax.rsqrt(jnp.mean(x * x, axis=-1, keepdims=True) + EPS) * g_ref[...]) * (1.0 + sc_ref[...]) + sh_ref[...]
    h_ref[...] = t
    logits = jnp.dot(t.astype(jnp.bfloat16), rw_ref[...].astype(jnp.bfloat16),
                     preferred_element_type=jnp.float32) + rb_ref[...]
    lane = lax.broadcasted_iota(jnp.int32, (tm, LANE), 1)
    lane_f = lane.astype(jnp.float32)
    work = logits
    vals, sels, hots = [], [], []
    for _ in range(TOP_K):
        m = jnp.max(work, axis=-1, keepdims=True)
        sel = jnp.min(jnp.where(work == m, lane_f, float(LANE)), axis=-1, keepdims=True)
        hot = lane_f == sel
        vals.append(m)
        sels.append(sel.astype(jnp.int32))
        hots.append(hot)
        work = jnp.where(hot, -jnp.inf, work)
    exps = [jnp.exp(v - vals[0]) for v in vals]
    denom = exps[0] + exps[1] + exps[2] + exps[3]
    chosen = jnp.zeros((tm, LANE), jnp.float32)
    for hot in hots:
        chosen = chosen + hot.astype(jnp.float32)
    row = lax.broadcasted_iota(jnp.int32, (tm, tm), 0)
    col = lax.broadcasted_iota(jnp.int32, (tm, tm), 1)
    tri = (row > col).astype(jnp.bfloat16)
    before = jnp.dot(tri, chosen.astype(jnp.bfloat16), preferred_element_type=jnp.float32) + run_ref[0:1, :]
    idx_out = jnp.zeros((tm, LANE), jnp.int32)
    gate_out = jnp.zeros((tm, LANE), jnp.float32)
    rank_out = jnp.zeros((tm, LANE), jnp.int32)
    for k in range(TOP_K):
        rank_k = jnp.sum(jnp.where(hots[k], before, 0.0), axis=-1, keepdims=True).astype(jnp.int32)
        idx_out = jnp.where(lane == k, sels[k], idx_out)
        gate_out = jnp.where(lane == k, exps[k] / denom, gate_out)
        rank_out = jnp.where(lane == k, rank_k, rank_out)
    idx_ref[...] = idx_out
    gate_ref[...] = gate_out
    rank_ref[...] = rank_out
    run_new = run_ref[0:1, :] + jnp.sum(chosen, axis=0, keepdims=True)
    run_ref[...] = jnp.broadcast_to(run_new, run_ref.shape)
    cnt_ref[...] = jnp.broadcast_to(run_new, cnt_ref.shape)


def _moe_route(x2d, g, shift, scale, router_w, router_b, cnt0):
    n_tok, d = x2d.shape
    f32 = jnp.float32
    rw = jnp.pad(router_w, ((0, 0), (0, LANE - N_EXPERTS)))
    rb = jnp.pad(router_b.astype(f32), (0, LANE - N_EXPERTS), constant_values=NEG_BIG).reshape(1, LANE)
    tile = pl.BlockSpec((MOE_TM, LANE), lambda i: (i, 0))
    wide = pl.BlockSpec((MOE_TM, d), lambda i: (i, 0))
    vec = pl.BlockSpec((1, d), lambda i: (0, 0))
    small = pl.BlockSpec((SUBLANE, LANE), lambda i: (0, 0))
    row = lambda v: v.astype(f32).reshape(1, d)
    h, idx, gate, rank, cnt = pl.pallas_call(
        _route_body,
        grid=(n_tok // MOE_TM,),
        in_specs=[wide, vec, vec, vec, pl.BlockSpec((d, LANE), lambda i: (0, 0)),
                  pl.BlockSpec((1, LANE), lambda i: (0, 0)), small],
        out_specs=[wide, tile, tile, tile, small],
        out_shape=[jax.ShapeDtypeStruct((n_tok, d), f32),
                   jax.ShapeDtypeStruct((n_tok, LANE), jnp.int32),
                   jax.ShapeDtypeStruct((n_tok, LANE), f32),
                   jax.ShapeDtypeStruct((n_tok, LANE), jnp.int32),
                   jax.ShapeDtypeStruct((SUBLANE, LANE), f32)],
        scratch_shapes=[pltpu.VMEM((SUBLANE, LANE), f32)],
        compiler_params=pltpu.CompilerParams(dimension_semantics=("arbitrary",)),
        name="moe_route",
    )(x2d, row(g), row(shift), row(scale), rw, rb, cnt0)
    return h, idx[:, :TOP_K], gate, rank[:, :TOP_K], cnt


def _row_copy(src_ref, src_row, dst_ref, dst_row, sem):
    return pltpu.make_async_copy(src_ref.at[pl.ds(src_row, 1)], dst_ref.at[pl.ds(dst_row, 1)], sem)


def _scatter_body(off_ref, pad_ref, dest_ref, *rest, tile_starts):
    t_refs = rest[:-3]
    xs_ref, zero_ref, sem = rest[-3:]
    i = pl.program_id(0)
    tm = t_refs[0].shape[0]

    @pl.when(i == 0)
    def _():
        zero_ref[...] = jnp.zeros_like(zero_ref)
        used = off_ref[N_EXPERTS - 1] + pad_ref[N_EXPERTS - 1]
        n_rows = xs_ref.shape[0]

        def zero_block(start):
            return pltpu.make_async_copy(zero_ref, xs_ref.at[pl.ds(pl.multiple_of(start, MOE_BM), MOE_BM)], sem)

        for e in range(N_EXPERTS):
            tail = n_rows - (e + 1) * MOE_BM

            @pl.when(pad_ref[e] > 0)
            def _():
                zero_block(off_ref[e] + pad_ref[e] - MOE_BM).start()

            @pl.when(tail >= used)
            def _():
                zero_block(tail).start()
        for e in range(N_EXPERTS):
            tail = n_rows - (e + 1) * MOE_BM

            @pl.when(pad_ref[e] > 0)
            def _():
                zero_block(0).wait()

            @pl.when(tail >= used)
            def _():
                zero_block(0).wait()

    for s, t_ref in enumerate(t_refs):
        @pl.when((i >= tile_starts[s]) & (i < tile_starts[s + 1]))
        def _():
            def issue(t, carry):
                for k in range(TOP_K):
                    _row_copy(t_ref, t, xs_ref, dest_ref[TOP_K * t + k], sem).start(priority=k % 2)
                return carry

            lax.fori_loop(0, tm, issue, 0, unroll=MOE_ISSUE_UNROLL)
    for _ in range(TOP_K):
        pltpu.make_async_copy(t_refs[0], xs_ref.at[pl.ds(0, tm)], sem).wait()


def _stream_tiles(streams):
    starts = [0]
    for t in streams:
        starts.append(starts[-1] + t.shape[0] // MOE_TM)
    return tuple(starts)


def _stream_spec(width, starts, s, extra=0):
    lo, n = starts[s], starts[s + 1] - starts[s]
    return pl.BlockSpec((MOE_TM, width), lambda i, *_: (jnp.clip(i + extra - lo, 0, n - 1), 0))


def _moe_scatter(hs, dest_flat, off, padded, n_rows):
    d = hs[0].shape[1]
    starts = _stream_tiles(hs)
    in_specs = [pl.BlockSpec((TOP_K * MOE_TM,), lambda i, off, pad: (i,), memory_space=pltpu.SMEM)]
    in_specs += [_stream_spec(d, starts, s) for s in range(len(hs))]
    return pl.pallas_call(
        functools.partial(_scatter_body, tile_starts=starts),
        grid_spec=pltpu.PrefetchScalarGridSpec(
            num_scalar_prefetch=2,
            grid=(starts[-1],),
            in_specs=in_specs,
            out_specs=pl.BlockSpec(memory_space=pl.ANY),
            scratch_shapes=[pltpu.VMEM((MOE_BM, d), jnp.float32), pltpu.SemaphoreType.DMA],
        ),
        out_shape=jax.ShapeDtypeStruct((n_rows, d), jnp.float32),
        compiler_params=pltpu.CompilerParams(dimension_semantics=("arbitrary",)),
        name="moe_scatter",
    )(off, padded, dest_flat, *hs)


def _expert_body(blk_e_ref, n_act_ref, x_ref, wgu_ref, bgu_ref, wdn_ref, bdn_ref, y_ref, wgu_bf, wdn_bf):
    i = pl.program_id(0)

    @pl.when(i < n_act_ref[0])
    def _():
        prev = blk_e_ref[jnp.maximum(i - 1, 0)]

        @pl.when((i == 0) | (blk_e_ref[i] != prev))
        def _():
            wgu_bf[...] = wgu_ref[...].astype(jnp.bfloat16)
            wdn_bf[...] = wdn_ref[...].astype(jnp.bfloat16)

        x = x_ref[...].astype(jnp.bfloat16)
        gu = jnp.dot(x, wgu_bf[...], preferred_element_type=jnp.float32) + bgu_ref[...]
        gate = jnp.minimum(gu[:, :D_EXPERT], SWIGLU_LIMIT)
        up = jnp.clip(gu[:, D_EXPERT:], -SWIGLU_LIMIT, SWIGLU_LIMIT)
        act = (up + 1.0) * gate * jax.nn.sigmoid(SWIGLU_ALPHA * gate)
        y_ref[...] = jnp.dot(act.astype(jnp.bfloat16), wdn_bf[...],
                             preferred_element_type=jnp.float32) + bdn_ref[...]

    @pl.when(i >= n_act_ref[0])
    def _():
        y_ref[...] = jnp.zeros_like(y_ref)


def _moe_experts(xs, blk_e, n_act, w_gu, b_gu, w_dn, b_dn, layer):
    n_rows, d = xs.shape
    n_blk = n_rows // MOE_BM

    def blk(i, be, na):
        return jnp.minimum(i, na[0] - 1)

    def expert(i, be, na):
        return (layer, be[blk(i, be, na)], 0, 0)

    return pl.pallas_call(
        _expert_body,
        grid_spec=pltpu.PrefetchScalarGridSpec(
            num_scalar_prefetch=2,
            grid=(n_blk,),
            in_specs=[pl.BlockSpec((MOE_BM, d), lambda i, be, na: (blk(i, be, na), 0)),
                      pl.BlockSpec((None, None, d, 2 * D_EXPERT), expert),
                      pl.BlockSpec((None, 1, 2 * D_EXPERT), lambda i, be, na: (be[blk(i, be, na)], 0, 0)),
                      pl.BlockSpec((None, None, D_EXPERT, d), expert),
                      pl.BlockSpec((None, 1, d), lambda i, be, na: (be[blk(i, be, na)], 0, 0))],
            out_specs=pl.BlockSpec((MOE_BM, d), lambda i, be, na: (i, 0)),
            scratch_shapes=[pltpu.VMEM((d, 2 * D_EXPERT), jnp.bfloat16),
                            pltpu.VMEM((D_EXPERT, d), jnp.bfloat16)],
        ),
        out_shape=jax.ShapeDtypeStruct((n_rows, d), jnp.float32),
        compiler_params=pltpu.CompilerParams(dimension_semantics=("arbitrary",),
                                             vmem_limit_bytes=MOE_VMEM_LIMIT),
        name="moe_experts",
    )(blk_e, n_act, xs, w_gu, b_gu.reshape(N_EXPERTS, 1, -1), w_dn, b_dn.reshape(N_EXPERTS, 1, -1))


def _combine_body(dest_ref, dest_nxt_ref, *rest, tile_starts):
    n_streams = len(tile_starts) - 1
    ins, ys_ref = rest[:3 * n_streams], rest[3 * n_streams]
    y_refs = rest[3 * n_streams + 1:4 * n_streams + 1]
    buf_ref, sems = rest[4 * n_streams + 1:]
    i = pl.program_id(0)
    n = pl.num_programs(0)
    tm = y_refs[0].shape[0]

    def fetch(d_ref, slot):
        def issue(t, carry):
            for k in range(TOP_K):
                _row_copy(ys_ref, d_ref[TOP_K * t + k], buf_ref.at[slot, k], t,
                          sems.at[slot]).start(priority=k % 2)
            return carry
        lax.fori_loop(0, tm, issue, 0, unroll=MOE_ISSUE_UNROLL)

    @pl.when(i == 0)
    def _():
        fetch(dest_ref, 0)

    @pl.when(i + 1 < n)
    def _():
        fetch(dest_nxt_ref, (i + 1) % 2)

    slot = i % 2
    for k in range(TOP_K):
        pltpu.make_async_copy(ys_ref.at[pl.ds(0, tm)], buf_ref.at[slot, k], sems.at[slot]).wait()
    for s in range(n_streams):
        gate_ref, x_ref, gt_ref = ins[3 * s:3 * s + 3]

        @pl.when((i >= tile_starts[s]) & (i < tile_starts[s + 1]))
        def _():
            g = gate_ref[...]
            acc = g[:, 0:1] * buf_ref[slot, 0]
            for k in range(1, TOP_K):
                acc = acc + g[:, k:k + 1] * buf_ref[slot, k]
            y_refs[s][...] = x_ref[...] + gt_ref[...] * acc


def _moe_combine(ys, dest_flat, gates, xs2d, gts):
    d = ys.shape[1]
    starts = _stream_tiles(xs2d)
    n_tiles = starts[-1]
    in_specs = [pl.BlockSpec((TOP_K * MOE_TM,), lambda i: (i,), memory_space=pltpu.SMEM),
                pl.BlockSpec((TOP_K * MOE_TM,), lambda i: (jnp.minimum(i + 1, n_tiles - 1),),
                             memory_space=pltpu.SMEM)]
    args = [dest_flat, dest_flat]
    for s, (gate, x2d, gt) in enumerate(zip(gates, xs2d, gts)):
        in_specs += [_stream_spec(LANE, starts, s), _stream_spec(d, starts, s), pl.BlockSpec((1, d), lambda i: (0, 0))]
        args += [gate, x2d, gt.astype(jnp.float32).reshape(1, d)]
    in_specs.append(pl.BlockSpec(memory_space=pl.ANY))
    args.append(ys)
    return pl.pallas_call(
        functools.partial(_combine_body, tile_starts=starts),
        grid=(n_tiles,),
        in_specs=in_specs,
        out_specs=[_stream_spec(d, starts, s) for s in range(len(xs2d))],
        out_shape=[jax.ShapeDtypeStruct(x2d.shape, jnp.float32) for x2d in xs2d],
        scratch_shapes=[pltpu.VMEM((2, TOP_K, MOE_TM, d), jnp.float32), pltpu.SemaphoreType.DMA((2,))],
        compiler_params=pltpu.CompilerParams(dimension_semantics=("arbitrary",)),
        name="moe_combine",
    )(*args)


def moe_layer(streams, g, router_w, router_b, w_gu, b_gu, w_dn, b_dn, layer):
    i32 = jnp.int32
    routed = []
    cnt = jnp.zeros((SUBLANE, LANE), jnp.float32)
    for x2d, shift, scale, _ in streams:
        assert x2d.shape[0] % MOE_TM == 0
        h, idx, gate, rank, cnt = _moe_route(x2d, g, shift, scale, router_w, router_b, cnt)
        routed.append((h, idx, gate, rank))
    counts = cnt[0, :N_EXPERTS].astype(i32)
    padded = (counts + MOE_BM - 1) // MOE_BM * MOE_BM
    pad_end = jnp.cumsum(padded)
    off = (pad_end - padded).astype(i32)
    padded = padded.astype(i32)
    n_tok = sum(s[0].shape[0] for s in streams)
    n_blk = -(-(n_tok * TOP_K) // MOE_BM) + N_EXPERTS
    blk_e = jnp.minimum(jnp.sum(jnp.arange(n_blk)[:, None] * MOE_BM >= pad_end[None, :], axis=1),
                        N_EXPERTS - 1).astype(i32)
    n_act = (pad_end[-1:] // MOE_BM).astype(i32)
    experts = jnp.arange(N_EXPERTS, dtype=i32)
    dests = []
    for _, idx, _, rank in routed:
        dest = rank + jnp.sum(jnp.where(idx[..., None] == experts, off, 0), axis=-1)
        dests.append(dest.reshape(-1).astype(i32))
    dest_flat = jnp.concatenate(dests)
    xs = _moe_scatter([r[0] for r in routed], dest_flat, off, padded, n_blk * MOE_BM)
    ys = _moe_experts(xs, blk_e, n_act, w_gu, b_gu, w_dn, b_dn, layer)
    return _moe_combine(ys, dest_flat, [r[2] for r in routed], [s[0] for s in streams], [s[3] for s in streams])


def kernel(x, c, ctx, c_ctx, norm_g, ada_w, ada_b, w_out, w_in_even, hy_conv_w, hy_conv_b,
           hy_w1, hy_b1, hy_w2, hy_b2, hy_w3, hy_b3, hy_w4, hy_freq, hy_filter_bias,
           att_q_norm, att_k_norm, att_sink, w_in_odd, ssd_conv_w, ssd_conv_b, ssd_dt_bias,
           ssd_A_log, ssd_D, ssd_norm, hg_lower_bounds, hg_norm, router_w, router_b,
           moe_w_gu, moe_b_gu, moe_w_dn, moe_b_dn):
    lbs = jax.nn.softmax(hg_lower_bounds.astype(jnp.float32), axis=0)
    lbs = jnp.cumsum(lbs, axis=0) - lbs[0]
    xc = ctx
    for layer in range(DEPTH):
        ctx_out = layer < DEPTH - 1
        i = layer // 2
        sh, sc, gt = adaln(c, ada_w[layer], ada_b[layer], 0)
        sh_c, sc_c, gt_c = adaln(c_ctx, ada_w[layer], ada_b[layer], 0)
        assert (layer % 2 == 0) == ctx_out
        if layer % 2 == 0:
            x, xc = even_layer(x, xc, norm_g[layer, 0], (sh, sc), (sh_c, sc_c), w_in_even[i], hy_conv_w[i],
                               hy_conv_b[i], hy_w1[i], hy_b1[i], hy_w2[i], hy_b2[i], hy_w3[i], hy_b3[i], hy_w4[i],
                               hy_freq[i], hy_filter_bias[i], att_q_norm[i], att_k_norm[i], att_sink[i], gt, gt_c,
                               w_out[layer])
        else:
            x = odd_layer(x, xc, norm_g[layer, 0], (sh, sc), (sh_c, sc_c), lbs[layer], w_in_odd[i], ssd_conv_w[i],
                          ssd_conv_b[i], ssd_dt_bias[i], ssd_A_log[i], ssd_D[i], ssd_norm[i], hg_norm[i], gt,
                          w_out[layer])
        sh, sc, gt = adaln(c, ada_w[layer], ada_b[layer], 1)
        streams = [(x[0], sh, sc, gt)]
        if ctx_out:
            sh_c, sc_c, gt_c = adaln(c_ctx, ada_w[layer], ada_b[layer], 1)
            streams.append((xc[0], sh_c, sc_c, gt_c))
        outs = moe_layer(streams, norm_g[layer, 1], router_w[layer], router_b[layer], moe_w_gu, moe_b_gu[layer],
                         moe_w_dn, moe_b_dn[layer], layer)
        x = outs[0][None]
        if ctx_out:
            xc = outs[1][None]
    return x
```

```python
import functools
import math

import jax
import jax.numpy as jnp
from jax import lax
from jax.experimental import pallas as pl
from jax.experimental.pallas import tpu as pltpu

D_MODEL = 1024
DEPTH = 2
GRID_W = 64
MIX_W = D_MODEL
EPS = 1e-6
CONV_W = 3

HY_W = MIX_W // 2
HY_ORDER = 2
HY_EMB = 33
HY_FFN = 64
HY_TARGET = 1e-2
HY_SHORT_PCT = 0.3
HY_LONG_PCT = 1.5

HEAD_DIM = 64
ATT_HEADS = (MIX_W // 2) // HEAD_DIM
ATT_KV_HEADS = 2
ATT_GROUP = ATT_HEADS // ATT_KV_HEADS
ATT_WINDOW = 128
ATT_BLOCK = 128
ROPE_BASE = 10000.0
ATT_Q_W = ATT_HEADS * HEAD_DIM
ATT_KV_W = ATT_KV_HEADS * HEAD_DIM

SSD_W = MIX_W // 2
SSD_HEAD_DIM = 64
SSD_HEADS = SSD_W // SSD_HEAD_DIM
SSD_GROUPS = 2
SSD_STATE = 128
SSD_CHUNK = 128
SSD_CONV_CH = SSD_W + 2 * SSD_GROUPS * SSD_STATE

HG_W = MIX_W // 2
HG_EXPAND = 128
HG_HEADS = HG_W // HG_EXPAND
HG_VDIM = HG_W // HG_HEADS
HG_CHUNK = 64

N_EXPERTS = 32
TOP_K = 4
D_EXPERT = D_MODEL
SWIGLU_ALPHA = 1.702
SWIGLU_LIMIT = 7.0
MOE_BLOCK = 128

EVEN_IN = 2 * ATT_KV_W + ATT_Q_W + 3 * HY_W
ODD_STATE_COLS = SSD_CONV_CH + 2 * SSD_HEADS + 3 * HG_W
ODD_IN = ODD_STATE_COLS + SSD_W + 2 * HG_W

LANE = 128
SUBLANE = 8


def _mm_body(a_ref, b_ref, o_ref):
    a = a_ref[...].astype(jnp.bfloat16)
    b = b_ref[...].astype(jnp.bfloat16)
    o_ref[...] = jnp.dot(a, b, preferred_element_type=jnp.float32)


def _pick_tile(n, candidates):
    for c in candidates:
        if n % c == 0:
            return c
    return n


def pmm(a, b):
    m, k = a.shape
    n = b.shape[1]
    n_pad = -(-n // LANE) * LANE
    if n_pad != n:
        b = jnp.pad(b, ((0, 0), (0, n_pad - n)))
    m_pad = -(-m // SUBLANE) * SUBLANE
    if m_pad != m:
        a = jnp.pad(a, ((0, m_pad - m), (0, 0)))
    tm = _pick_tile(m_pad, (512, 256, 128, 64, 32, 16, 8))
    tn = _pick_tile(n_pad, (512, 384, 256, 128))
    out = pl.pallas_call(
        _mm_body,
        grid=(m_pad // tm, n_pad // tn),
        in_specs=[pl.BlockSpec((tm, k), lambda i, j: (i, 0)),
                  pl.BlockSpec((k, tn), lambda i, j: (0, j))],
        out_specs=pl.BlockSpec((tm, tn), lambda i, j: (i, j)),
        out_shape=jax.ShapeDtypeStruct((m_pad, n_pad), jnp.float32),
        name="dense_mm",
    )(a, b)
    return out[:m, :n]


PROJ_TM = 256
PROJ_VMEM_LIMIT = 56 * 1024 * 1024


def _norm_proj_body(x_ref, g_ref, sh_ref, sc_ref, w_ref, o_ref):
    x = x_ref[...]
    y = x * lax.rsqrt(jnp.mean(x * x, axis=-1, keepdims=True) + EPS) * g_ref[...]
    h = y * (1.0 + sc_ref[...]) + sh_ref[...]
    o_ref[...] = jnp.dot(h.astype(jnp.bfloat16), w_ref[...], preferred_element_type=jnp.float32)


def norm_proj(x2d, g, shift, scale, w_bf16, n_first=0):
    rows, d = x2d.shape
    n = w_bf16.shape[1]
    assert rows % PROJ_TM == 0 and n_first % PROJ_TM == 0 and n % LANE == 0
    first_tiles = n_first // PROJ_TM
    mod = pl.BlockSpec((None, 1, d), lambda i: (jnp.where(i < first_tiles, 0, 1), 0, 0))
    return pl.pallas_call(
        _norm_proj_body,
        grid=(rows // PROJ_TM,),
        in_specs=[pl.BlockSpec((PROJ_TM, d), lambda i: (i, 0)),
                  pl.BlockSpec((1, d), lambda i: (0, 0)), mod, mod,
                  pl.BlockSpec((d, n), lambda i: (0, 0))],
        out_specs=pl.BlockSpec((PROJ_TM, n), lambda i: (i, 0)),
        out_shape=jax.ShapeDtypeStruct((rows, n), jnp.float32),
        compiler_params=pltpu.CompilerParams(dimension_semantics=("arbitrary",),
                                             vmem_limit_bytes=PROJ_VMEM_LIMIT),
        name="norm_proj",
    )(x2d, g.astype(jnp.float32).reshape(1, d), shift.astype(jnp.float32).reshape(2, 1, d),
      scale.astype(jnp.float32).reshape(2, 1, d), w_bf16)


def pmm3(a, b):
    lead = a.shape[:-1]
    return pmm(a.reshape(-1, a.shape[-1]), b).reshape(*lead, b.shape[-1])


def rms_norm(x, g):
    xf = x.astype(jnp.float32)
    y = xf * lax.rsqrt(jnp.mean(xf * xf, axis=-1, keepdims=True) + EPS)
    return (y * g.astype(jnp.float32)).astype(x.dtype)


def modulate(h, shift, scale):
    return h * (1.0 + scale) + shift


def adaln(cond, w, b, j):
    lo, hi = 3 * j * D_MODEL, 3 * (j + 1) * D_MODEL
    m = jax.nn.silu(cond) @ w[:, lo:hi] + b[lo:hi]
    return jnp.split(m, 3, axis=-1)


def dwconv_centred(u, w, b):
    ch = u.shape[-1]
    y = lax.conv_general_dilated(u, w[:, None, :].astype(u.dtype), window_strides=(1,),
                                 padding=[(CONV_W // 2, CONV_W // 2)],
                                 dimension_numbers=('NWC', 'WIO', 'NWC'), feature_group_count=ch)
    return y + b.astype(u.dtype)


def axial_rope_tables(length):
    rows = length // GRID_W
    n_pairs = HEAD_DIM // 4
    inv = ROPE_BASE ** (-jnp.arange(n_pairs, dtype=jnp.float32) / n_pairs)
    row_ang = jnp.arange(rows, dtype=jnp.float32)[:, None] * inv
    col_ang = jnp.arange(GRID_W, dtype=jnp.float32)[:, None] * inv
    ang_r = jnp.broadcast_to(row_ang[:, None], (rows, GRID_W, n_pairs)).reshape(length, n_pairs)
    ang_c = jnp.broadcast_to(col_ang[None], (rows, GRID_W, n_pairs)).reshape(length, n_pairs)
    return jnp.cos(ang_r), jnp.sin(ang_r), jnp.cos(ang_c), jnp.sin(ang_c)


def _rotate(u, cos, sin):
    n = u.shape[-1] // 2
    u1, u2 = u[..., :n], u[..., n:]
    cos = cos[None, :, None, :]
    sin = sin[None, :, None, :]
    return jnp.concatenate([u1 * cos - u2 * sin, u1 * sin + u2 * cos], axis=-1)


def apply_axial_rope(u, tables):
    cr, sr, cc, sc = tables
    half = HEAD_DIM // 2
    return jnp.concatenate([_rotate(u[..., :half], cr, sr), _rotate(u[..., half:], cc, sc)], axis=-1)


def hyena_filters(length, w1, b1, w2, b2, w3, b3, w4, freq):
    f32 = jnp.float32
    t = jnp.linspace(0.0, 1.0, length, dtype=f32)[:, None]
    bands = (HY_EMB - 1) // 2
    w_ang = 2.0 * math.pi * jnp.arange(length, dtype=f32)[:, None] / length
    fr = jnp.linspace(1e-4, bands - 1, bands, dtype=f32)[None]
    z = jnp.concatenate([t, jnp.cos(fr * w_ang), -jnp.sin(fr * w_ang)], axis=-1)
    fq = freq.astype(f32)
    hdn = jnp.sin(fq * (z @ w1.astype(f32) + b1.astype(f32)))
    hdn = jnp.sin(fq * (hdn @ w2.astype(f32) + b2.astype(f32)))
    hdn = jnp.sin(fq * (hdn @ w3.astype(f32) + b3.astype(f32)))
    h = (hdn @ w4.astype(f32)).reshape(length, HY_ORDER, 2, HY_W)
    max_decay = math.log(HY_TARGET) / HY_SHORT_PCT
    min_decay = math.log(HY_TARGET) / HY_LONG_PCT
    deltas = jnp.abs(jnp.linspace(min_decay, max_decay, HY_W, dtype=f32))
    h = h * jnp.exp(-t * deltas)[:, None, None, :]
    h2 = jnp.concatenate([h[:, :, 0], jnp.zeros((1, HY_ORDER, HY_W), f32), h[:0:-1, :, 1]], axis=0)
    h2 = h2 / jnp.sum(jnp.abs(h2), axis=0, keepdims=True)
    return jnp.fft.rfft(h2, axis=0)


def hyena_mix(u, hf, filter_bias, conv_w, conv_b):
    length = u.shape[1]
    u = dwconv_centred(u.astype(jnp.float32), conv_w, conv_b)
    v, x1, x2 = jnp.split(u, 3, axis=-1)
    z = v
    for o, gate in enumerate((x1, x2)):
        zf = jnp.fft.rfft(z, n=2 * length, axis=1)
        zc = jnp.fft.irfft(zf * hf[None, :, o], n=2 * length, axis=1)[:, :length]
        z = gate * (zc + z * filter_bias[o].astype(jnp.float32))
    return z


def window_attention(q, k, v, k_c, v_c, sink):
    bsz, length = q.shape[:2]
    nb = length // ATT_BLOCK
    scale = HEAD_DIM ** -0.5
    qb = q.reshape(bsz, nb, ATT_BLOCK, ATT_KV_HEADS, ATT_GROUP, HEAD_DIM)
    pad = ((0, 0), (ATT_BLOCK, ATT_BLOCK), (0, 0), (0, 0))

    def band(a):
        ap = jnp.pad(a, pad).reshape(bsz, nb + 2, ATT_BLOCK, ATT_KV_HEADS, HEAD_DIM)
        return jnp.concatenate([ap[:, :-2], ap[:, 1:-1], ap[:, 2:]], axis=2)

    kw, vw = band(k), band(v)
    s_loc = jnp.einsum('bnqhgd,bnkhd->bnhgqk', qb, kw) * scale
    s_ctx = jnp.einsum('bnqhgd,bchd->bnhgqc', qb, k_c) * scale
    qpos = jnp.arange(nb)[:, None] * ATT_BLOCK + jnp.arange(ATT_BLOCK)[None]
    kpos = (jnp.arange(nb)[:, None] - 1) * ATT_BLOCK + jnp.arange(3 * ATT_BLOCK)[None]
    rel = kpos[:, None, :] - qpos[:, :, None]
    valid = (jnp.abs(rel) <= ATT_WINDOW) & (kpos[:, None, :] >= 0) & (kpos[:, None, :] < length)
    s_loc = jnp.where(valid[None, :, None, None], s_loc, -jnp.inf)
    sink_l = jnp.broadcast_to(sink.astype(jnp.float32).reshape(1, 1, ATT_KV_HEADS, ATT_GROUP, 1, 1),
                              s_loc.shape[:-1] + (1,))
    p = jax.nn.softmax(jnp.concatenate([s_loc, s_ctx, sink_l], axis=-1), axis=-1)
    n_loc = 3 * ATT_BLOCK
    n_ctx = k_c.shape[1]
    o = (jnp.einsum('bnhgqk,bnkhd->bnqhgd', p[..., :n_loc], vw)
         + jnp.einsum('bnhgqc,bchd->bnqhgd', p[..., n_loc:n_loc + n_ctx], v_c))
    return o.reshape(bsz, length, ATT_Q_W)


def context_attention(q_c, k_c, v_c, sink):
    bsz, n_ctx = q_c.shape[:2]
    s = jnp.einsum('bqhgd,bkhd->bhgqk', q_c, k_c) * HEAD_DIM ** -0.5
    sink_l = jnp.broadcast_to(sink.astype(jnp.float32).reshape(1, ATT_KV_HEADS, ATT_GROUP, 1, 1),
                              s.shape[:-1] + (1,))
    p = jax.nn.softmax(jnp.concatenate([s, sink_l], axis=-1), axis=-1)[..., :-1]
    return jnp.einsum('bhgqk,bkhd->bqhgd', p, v_c).reshape(bsz, n_ctx, ATT_Q_W)


def ssd_scan(x, dt, a, bm, cm, d_skip, init, need_y):
    bsz, length, n_heads, hd = x.shape
    nc = length // SSD_CHUNK
    hpg = n_heads // SSD_GROUPS
    da = (dt * a).reshape(bsz, nc, SSD_CHUNK, SSD_GROUPS, hpg)
    cs = jnp.cumsum(da, axis=2)
    xdt = (x * dt[..., None]).reshape(bsz, nc, SSD_CHUNK, SSD_GROUPS, hpg, hd)
    bc = bm.reshape(bsz, nc, SSD_CHUNK, SSD_GROUPS, SSD_STATE)
    cc = cm.reshape(bsz, nc, SSD_CHUNK, SSD_GROUPS, SSD_STATE)
    to_end = jnp.exp(cs[:, :, -1:] - cs)
    states = jnp.einsum('bcsgn,bcsgh,bcsghp->bcghpn', bc, to_end, xdt)
    chunk_decay = jnp.exp(cs[:, :, -1])

    def step(s, inp):
        st, dec = inp
        return s * dec[..., None, None] + st, s

    s_final, s_in = lax.scan(step, init, (jnp.moveaxis(states, 1, 0), jnp.moveaxis(chunk_decay, 1, 0)))
    if not need_y:
        return None, s_final
    s_in = jnp.moveaxis(s_in, 0, 1)
    cs_t = jnp.moveaxis(cs, 2, -1)
    diff = cs_t[..., :, None] - cs_t[..., None, :]
    lower = jnp.tril(jnp.ones((SSD_CHUNK, SSD_CHUNK), bool))
    decay = jnp.where(lower, jnp.exp(jnp.where(lower, diff, 0.0)), 0.0)
    scores = jnp.einsum('bclgn,bcsgn->bcgls', cc, bc)
    y_diag = jnp.einsum('bcgls,bcghls,bcsghp->bclghp', scores, decay, xdt)
    y_off = jnp.einsum('bclgn,bcghpn,bclgh->bclghp', cc, s_in, jnp.exp(cs))
    y = (y_diag + y_off).reshape(bsz, length, n_heads, hd) + d_skip[:, None] * x
    return y, s_final


def hgrn2_scan(q, k, v, g, init, need_o):
    bsz, length, n_heads, _ = k.shape
    nc = length // HG_CHUNK

    def chunks(a):
        return a.reshape(bsz, nc, HG_CHUNK, n_heads, a.shape[-1]).transpose(1, 0, 3, 2, 4)

    lower = jnp.tril(jnp.ones((HG_CHUNK, HG_CHUNK), bool))[:, :, None]

    def update(s, kc, vc, cum):
        last = cum[:, :, -1]
        return (s * jnp.exp(last)[..., None]
                + jnp.einsum('bhsk,bhsv->bhkv', kc * jnp.exp(last[:, :, None] - cum), vc))

    if not need_o:
        def step_state(s, inp):
            kc, vc, gc = inp
            return update(s, kc, vc, jnp.cumsum(gc, axis=2)), None
        s_final, _ = lax.scan(step_state, init, (chunks(k), chunks(v), chunks(g)))
        return None, s_final

    def step(s, inp):
        qc, kc, vc, gc = inp
        cum = jnp.cumsum(gc, axis=2)
        diff = cum[:, :, :, None, :] - cum[:, :, None, :, :]
        decay = jnp.where(lower, jnp.exp(jnp.where(lower, diff, 0.0)), 0.0)
        att = jnp.einsum('bhtk,bhsk,bhtsk->bhts', qc, kc, decay)
        o = (jnp.einsum('bhtk,bhkv->bhtv', qc * jnp.exp(cum), s)
             + jnp.einsum('bhts,bhsv->bhtv', att, vc))
        return update(s, kc, vc, cum), o

    s_final, o = lax.scan(step, init, (chunks(q), chunks(k), chunks(v), chunks(g)))
    return o.transpose(1, 0, 3, 2, 4).reshape(bsz, length, n_heads, v.shape[-1]), s_final


HY_N1 = 128
HY_N2 = 256
HY_NB = 8
HY_TM = 512


def _hy_dft_constants(length):
    import numpy as np
    n = 2 * length
    assert n == HY_N1 * HY_N2
    half = HY_N1 // 2
    k1 = np.arange(HY_N1)[:, None]
    n1 = np.arange(half)[None, :]
    n2 = np.arange(HY_N2)[:, None, None]
    ang = 2 * np.pi * (k1 * n1 / HY_N1)[None] + 2 * np.pi * n2 * k1[None] / n
    fwd = np.stack([np.cos(ang), -np.sin(ang)], axis=2).reshape(HY_N2, 2 * HY_N1, half)
    inv = fwd.transpose(0, 2, 1)
    a2 = 2 * np.pi * np.outer(np.arange(HY_N2), np.arange(HY_N2)) / HY_N2
    c, s = np.cos(a2), -np.sin(a2)
    m_fwd = np.concatenate([np.stack([c, -s], axis=2).reshape(HY_N2, 2 * HY_N2),
                            np.stack([s, c], axis=2).reshape(HY_N2, 2 * HY_N2)], axis=0)
    m_inv = np.stack([np.concatenate([c, s], axis=1), np.concatenate([-s, c], axis=1)],
                     axis=1).reshape(2 * HY_N2, 2 * HY_N2)
    f = np.float32
    return fwd.astype(f), inv.astype(f), m_fwd.astype(f), m_inv.astype(f)


def _hy_conv_body(p_ref, prev_ref, next_ref, w_ref, b_ref, o_ref):
    i = pl.program_id(1)
    n = pl.num_programs(1)
    u = p_ref[...]
    rows = u.shape[0]
    row = lax.broadcasted_iota(jnp.int32, u.shape, 0)
    before = jnp.where(i == 0, 0.0, prev_ref[HALO - 1:HALO, :])
    after = jnp.where(i == n - 1, 0.0, next_ref[0:1, :])
    up = jnp.where(row == 0, before, pltpu.roll(u, 1, 0))
    un = jnp.where(row == rows - 1, after, pltpu.roll(u, rows - 1, 0))
    o_ref[...] = w_ref[0:1, :] * up + w_ref[1:2, :] * u + w_ref[2:3, :] * un + b_ref[...]


def _hy_conv(p, conv_w, conv_b, tm):
    length = p.shape[0]
    per = tm // HALO
    last = length // HALO - 1
    return pl.pallas_call(
        _hy_conv_body,
        grid=(3, length // tm),
        in_specs=[pl.BlockSpec((tm, HY_W), lambda c, i: (i, c)),
                  pl.BlockSpec((HALO, HY_W), lambda c, i: (jnp.maximum(i * per - 1, 0), c)),
                  pl.BlockSpec((HALO, HY_W), lambda c, i: (jnp.minimum((i + 1) * per, last), c)),
                  pl.BlockSpec((CONV_W, HY_W), lambda c, i: (0, c)),
                  pl.BlockSpec((1, HY_W), lambda c, i: (0, c))],
        out_specs=pl.BlockSpec((None, tm, HY_W), lambda c, i: (c, i, 0)),
        out_shape=jax.ShapeDtypeStruct((3, length, HY_W), jnp.float32),
        name="hy_conv",
    )(p, p, p, conv_w, conv_b.reshape(1, -1))


def _hy_filter_body(z_ref, w1_ref, b1_ref, w2_ref, b2_ref, w3_ref, b3_ref, w4_ref, fq_ref, dl_ref,
                    e_ref, o_ref, nrm_ref):
    i = pl.program_id(0)
    bf = jnp.bfloat16
    z = z_ref[...]
    fq = fq_ref[...]

    def layer(a, w_ref, b_ref):
        return jnp.sin(fq * (jnp.dot(a.astype(bf), w_ref[...].astype(bf), preferred_element_type=jnp.float32)
                             + b_ref[...]))

    hdn = layer(layer(layer(z, w1_ref, b1_ref), w2_ref, b2_ref), w3_ref, b3_ref)
    h = jnp.dot(hdn.astype(bf), w4_ref[...].astype(bf), preferred_element_type=jnp.float32)
    decay = jnp.exp(-z[:, 0:1] * dl_ref[...])
    first = (lax.broadcasted_iota(jnp.int32, decay.shape, 0) == 0) & (i == 0)
    acc = []
    for order in range(HY_ORDER):
        lo = order * 2 * HY_W
        h0 = h[:, lo:lo + HY_W] * decay
        h1 = jnp.where(first, 0.0, h[:, lo + HY_W:lo + 2 * HY_W] * decay)
        e_ref[:, order * HY_W:(order + 1) * HY_W] = h0 + h1
        o_ref[:, order * HY_W:(order + 1) * HY_W] = h0 - h1
        acc.append(jnp.sum(jnp.abs(h0) + jnp.abs(h1), axis=0, keepdims=True))
    part = jnp.concatenate(acc, axis=1)

    @pl.when(i == 0)
    def _():
        nrm_ref[...] = jnp.zeros_like(nrm_ref)

    nrm_ref[...] = nrm_ref[...] + part


def _hy_filter_time(length, w1, b1, w2, b2, w3, b3, w4, freq, tm):
    f32 = jnp.float32
    t = jnp.linspace(0.0, 1.0, length, dtype=f32)[:, None]
    bands = (HY_EMB - 1) // 2
    w_ang = 2.0 * math.pi * jnp.arange(length, dtype=f32)[:, None] / length
    fr = jnp.linspace(1e-4, bands - 1, bands, dtype=f32)[None]
    z = jnp.concatenate([t, jnp.cos(fr * w_ang), -jnp.sin(fr * w_ang)], axis=-1)
    z = jnp.pad(z, ((0, 0), (0, LANE - HY_EMB)))
    w1p = jnp.pad(w1.astype(f32), ((0, LANE - HY_EMB), (0, 0)))
    max_decay = math.log(HY_TARGET) / HY_SHORT_PCT
    min_decay = math.log(HY_TARGET) / HY_LONG_PCT
    deltas = jnp.abs(jnp.linspace(min_decay, max_decay, HY_W, dtype=f32)).reshape(1, HY_W)
    full = lambda a: pl.BlockSpec(a.shape, lambda i: (0,) * a.ndim)
    row = lambda v: v.astype(f32).reshape(1, -1)
    args = (z, w1p, row(b1), w2.astype(f32), row(b2), w3.astype(f32), row(b3), w4.astype(f32), row(freq), deltas)
    ow = HY_ORDER * HY_W
    return pl.pallas_call(
        _hy_filter_body,
        grid=(length // tm,),
        in_specs=[pl.BlockSpec((tm, LANE), lambda i: (i, 0))] + [full(a) for a in args[1:]],
        out_specs=[pl.BlockSpec((tm, ow), lambda i: (i, 0)), pl.BlockSpec((tm, ow), lambda i: (i, 0)),
                   pl.BlockSpec((1, ow), lambda i: (0, 0))],
        out_shape=[jax.ShapeDtypeStruct((length, ow), f32), jax.ShapeDtypeStruct((length, ow), f32),
                   jax.ShapeDtypeStruct((1, ow), f32)],
        compiler_params=pltpu.CompilerParams(dimension_semantics=("arbitrary",)),
        name="hy_filter_time",
    )(*args)


def _words(x_bf16):
    return pltpu.bitcast(x_bf16, jnp.uint32)


def _halves(w_u32):
    return pltpu.bitcast(w_u32, jnp.bfloat16)


def _transpose8(parts):
    rows, cols = parts[0].shape
    parts = [p.reshape(rows // SUBLANE, SUBLANE, cols) for p in parts]
    row = lax.broadcasted_iota(jnp.int32, parts[0].shape, 1)
    for s in (1, 2, 4):
        keep = (row & s) == 0
        nxt = list(parts)
        for i in range(SUBLANE):
            if i & s == 0:
                a, b = parts[i], parts[i + s]
                nxt[i] = jnp.where(keep, a, pltpu.roll(b, s, 1))
                nxt[i + s] = jnp.where(keep, pltpu.roll(a, SUBLANE - s, 1), b)
        parts = nxt
    return [p.reshape(rows, cols) for p in parts]


def _gather_tiles(ref, j, n):
    return jnp.concatenate([ref[SUBLANE * g + j] for g in range(n)], axis=0)


def _hy_s1_body(x_ref, f_ref, a_ref):
    groups = x_ref.shape[0] // SUBLANE
    xs = _transpose8([_gather_tiles(x_ref, j, groups) for j in range(SUBLANE)])
    words = []
    for i in range(HY_NB):
        acc = jnp.dot(f_ref[i], xs[i].astype(jnp.bfloat16), preferred_element_type=jnp.float32)
        words.append(_words(acc.astype(jnp.bfloat16)))
    tiles = _transpose8(words)
    per = HY_N1 // SUBLANE
    for j in range(SUBLANE):
        for g in range(per):
            a_ref[per * j + g] = tiles[j][SUBLANE * g:SUBLANE * (g + 1)]


def _hy_s1(x4, sel, fwd):
    assert HY_NB == SUBLANE
    _, half, _, width = x4.shape
    return pl.pallas_call(
        _hy_s1_body,
        grid=(HY_N2 // HY_NB,),
        in_specs=[pl.BlockSpec((None, half, HY_NB, width), lambda j: (sel, 0, j, 0)),
                  pl.BlockSpec((HY_NB, 2 * HY_N1, half), lambda j: (j, 0, 0))],
        out_specs=pl.BlockSpec((HY_N1, HY_NB, width), lambda j: (0, j, 0)),
        out_shape=jax.ShapeDtypeStruct((HY_N1, HY_N2, width), jnp.uint32),
        name="hy_stage1",
    )(x4, fwd)


def _hy_s2f_body(ae_ref, ao_ref, m_ref, sc_ref, hr_ref, hi_ref):
    dot = lambda a, b: jnp.dot(a, b, preferred_element_type=jnp.float32)
    hr_ref[...] = (dot(m_ref[:HY_N2, :], _halves(ae_ref[...])) * sc_ref[...]).astype(hr_ref.dtype)
    hi_ref[...] = (dot(m_ref[HY_N2:, :], _halves(ao_ref[...])) * sc_ref[...]).astype(hi_ref.dtype)


def _hy_s2f(a_e, a_o, m_fwd, scale):
    width = scale.shape[1]
    slab = pl.BlockSpec((None, HY_N2, width), lambda k: (k, 0, 0))
    out = jax.ShapeDtypeStruct((HY_N1, HY_N2, width), jnp.bfloat16)
    return pl.pallas_call(
        _hy_s2f_body,
        grid=(HY_N1,),
        in_specs=[slab, slab, pl.BlockSpec((2 * HY_N2, 2 * HY_N2), lambda k: (0, 0)),
                  pl.BlockSpec((1, width), lambda k: (0, 0))],
        out_specs=[slab, slab],
        out_shape=[out, out],
        name="hy_filter_stage2",
    )(a_e, a_o, m_fwd, scale)


def _hy_s2_body(a_ref, hr_ref, hi_ref, mf_ref, mi_ref, b_ref):
    bf = jnp.bfloat16
    x = jnp.dot(mf_ref[...], _halves(a_ref[...]), preferred_element_type=jnp.float32)
    xr, xi = x[:HY_N2], x[HY_N2:]
    hr = hr_ref[...].astype(jnp.float32)
    hi = hi_ref[...].astype(jnp.float32)
    y = jnp.concatenate([(xr * hr - xi * hi).astype(bf), (xr * hi + xi * hr).astype(bf)], axis=0)
    b = jnp.dot(mi_ref[...], y, preferred_element_type=jnp.float32)
    b_ref[...] = _words(b.astype(bf))


def _hy_s2(a, h_re, h_im, m_fwd, m_inv, order):
    slab = pl.BlockSpec((None, HY_N2, HY_W), lambda k: (k, 0, 0))
    hslab = pl.BlockSpec((None, HY_N2, HY_W), lambda k: (k, 0, order))
    mat = pl.BlockSpec((2 * HY_N2, 2 * HY_N2), lambda k: (0, 0))
    return pl.pallas_call(
        _hy_s2_body,
        grid=(HY_N1,),
        in_specs=[slab, hslab, hslab, mat, mat],
        out_specs=slab,
        out_shape=jax.ShapeDtypeStruct((HY_N1, HY_N2, HY_W), jnp.uint32),
        name="hy_stage2",
    )(a, h_re, h_im, m_fwd, m_inv)


def _hy_is1_body(b_ref, g_ref, z_ref, gate_ref, bias_ref, o_ref):
    per = HY_N1 // SUBLANE
    spectra = _transpose8([jnp.concatenate([b_ref[per * j + g] for g in range(per)], axis=0)
                           for j in range(SUBLANE)])
    convs = [jnp.dot(g_ref[i], _halves(spectra[i]), preferred_element_type=jnp.float32) for i in range(HY_NB)]
    tiles = _transpose8(convs)
    for j in range(SUBLANE):
        for g in range(o_ref.shape[0] // SUBLANE):
            n1 = SUBLANE * g + j
            o_ref[n1] = gate_ref[n1] * (tiles[j][SUBLANE * g:SUBLANE * (g + 1)] + z_ref[n1] * bias_ref[...])


def _hy_is1(b, inv, z4, z_sel, gate4, gate_sel, bias):
    _, half, _, width = z4.shape
    real = lambda sel: pl.BlockSpec((None, half, HY_NB, width), lambda j: (sel, 0, j, 0))
    return pl.pallas_call(
        _hy_is1_body,
        grid=(HY_N2 // HY_NB,),
        in_specs=[pl.BlockSpec((HY_N1, HY_NB, width), lambda j: (0, j, 0)),
                  pl.BlockSpec((HY_NB, half, 2 * HY_N1), lambda j: (j, 0, 0)),
                  real(z_sel), real(gate_sel), pl.BlockSpec((1, width), lambda j: (0, 0))],
        out_specs=pl.BlockSpec((half, HY_NB, width), lambda j: (0, j, 0)),
        out_shape=jax.ShapeDtypeStruct((half, HY_N2, width), jnp.float32),
        name="hy_inv_stage1",
    )(b, inv, z4, gate4, bias)


def hyena_long(p_hy, f_w1, f_b1, f_w2, f_b2, f_w3, f_b3, f_w4, f_freq, f_bias, conv_w, conv_b):
    length = p_hy.shape[0]
    bf = jnp.bfloat16
    half = HY_N1 // 2
    fwd, inv, m_fwd, m_inv = (jnp.asarray(m, bf) for m in _hy_dft_constants(length))
    e, od, nrm = _hy_filter_time(length, f_w1, f_b1, f_w2, f_b2, f_w3, f_b3, f_w4, f_freq, HY_TM)
    ow = HY_ORDER * HY_W
    a_e = _hy_s1(e.reshape(1, half, HY_N2, ow), 0, fwd)
    a_o = _hy_s1(od.reshape(1, half, HY_N2, ow), 0, fwd)
    scale = 1.0 / (nrm * (2.0 * length))
    h_re, h_im = _hy_s2f(a_e, a_o, m_fwd, scale)
    u = _hy_conv(p_hy, conv_w, conv_b, HY_TM).reshape(3, half, HY_N2, HY_W)
    z, z_sel = u, 0
    for order in range(HY_ORDER):
        b = _hy_s2(_hy_s1(z, z_sel, fwd), h_re, h_im, m_fwd, m_inv, order)
        z = _hy_is1(b, inv, z, z_sel, u, 1 + order, f_bias[order].astype(jnp.float32).reshape(1, HY_W))[None]
        z_sel = 0
    return z.reshape(length, HY_W)


def even_mixer(h, hc, w_in, conv_w, conv_b, f_w1, f_b1, f_w2, f_b2, f_w3, f_b3, f_w4, f_freq, f_bias,
               q_norm, k_norm, sink, ctx_out):
    f32 = jnp.float32
    bsz, length, _ = h.shape
    n_ctx = hc.shape[1]
    filt = (f_w1, f_b1, f_w2, f_b2, f_w3, f_b3, f_w4, f_freq)
    o_v = ATT_KV_W
    o_q = 2 * ATT_KV_W
    o_hy = o_q + ATT_Q_W
    w_perm = jnp.concatenate([w_in[:, o_hy:], w_in[:, o_q:o_hy], w_in[:, :o_q]], axis=1)
    hy_w = 3 * HY_W
    p_hy = pmm3(h, w_perm).astype(f32)
    p = jnp.concatenate([p_hy[..., hy_w + ATT_Q_W:], p_hy[..., hy_w:hy_w + ATT_Q_W]], axis=-1)
    pc = pmm3(hc, w_in if ctx_out else w_in[:, :o_q]).astype(f32)

    def heads(a, n_heads):
        return a.reshape(a.shape[0], a.shape[1], n_heads, HEAD_DIM)

    k_c = rms_norm(heads(pc[..., :o_v], ATT_KV_HEADS), k_norm)
    v_c = heads(pc[..., o_v:o_q], ATT_KV_HEADS)
    rope = axial_rope_tables(length)
    k = apply_axial_rope(rms_norm(heads(p[..., :o_v], ATT_KV_HEADS), k_norm), rope)
    v = heads(p[..., o_v:o_q], ATT_KV_HEADS)
    q = apply_axial_rope(rms_norm(heads(p[..., o_q:o_hy], ATT_HEADS), q_norm), rope)
    att = window_attention(q.reshape(bsz, length, ATT_KV_HEADS, ATT_GROUP, HEAD_DIM), k, v, k_c, v_c, sink)
    hy = hyena_long(p_hy[0], *filt, f_bias, conv_w, conv_b)[None]
    out = jnp.concatenate([hy, att], axis=-1).astype(h.dtype)
    if not ctx_out:
        return out, None
    q_c = rms_norm(heads(pc[..., o_q:o_hy], ATT_HEADS), q_norm).reshape(bsz, n_ctx, ATT_KV_HEADS, ATT_GROUP, HEAD_DIM)
    att_c = context_attention(q_c, k_c, v_c, sink)
    hy_c = hyena_mix(pc[..., o_hy:], hyena_filters(n_ctx, *filt), f_bias, conv_w, conv_b)
    return out, jnp.concatenate([hy_c, att_c], axis=-1).astype(hc.dtype)


ATT_TQ = ATT_BLOCK
EVEN_TM = 256
EV_Q_BLK = 3 * HY_W // ATT_Q_W
EV_K_BLK = (3 * HY_W + ATT_Q_W) // ATT_KV_W
EV_V_BLK = EV_K_BLK + 1


def _rope_tables(length):
    cr, sr, cc, sc = axial_rope_tables(length)
    return jnp.concatenate([cr, cr, cc, cc], axis=-1), jnp.concatenate([-sr, sr, -sc, sc], axis=-1)


def _head_norm_rope(x, g_row, c, s, seg):
    sq = x * x
    hi = sq.astype(jnp.bfloat16)
    lo = (sq - hi.astype(jnp.float32)).astype(jnp.bfloat16)
    ms = (jnp.dot(hi, seg, preferred_element_type=jnp.float32) + jnp.dot(lo, seg, preferred_element_type=jnp.float32))
    y = x * lax.rsqrt(ms + EPS) * g_row
    width = x.shape[1]
    quarter = HEAD_DIM // 4
    lane = lax.broadcasted_iota(jnp.int32, x.shape, 1)
    partner = jnp.where((lane & quarter) == 0, pltpu.roll(y, width - quarter, 1), pltpu.roll(y, quarter, 1))
    return y * c + partner * s


def _qk_prep_body(q_ref, k_ref, c_ref, s_ref, qn_ref, kn_ref, segq_ref, segk_ref, qo_ref, ko_ref):
    c, s = c_ref[...], s_ref[...]
    tile = lambda t, n: jnp.concatenate([t] * n, axis=1)
    qo_ref[...] = _head_norm_rope(q_ref[...], tile(qn_ref[...], ATT_HEADS), tile(c, ATT_HEADS), tile(s, ATT_HEADS),
                                  segq_ref[...]).astype(qo_ref.dtype)
    ko_ref[...] = _head_norm_rope(k_ref[...], tile(kn_ref[...], ATT_KV_HEADS), tile(c, ATT_KV_HEADS),
                                  tile(s, ATT_KV_HEADS), segk_ref[...]).astype(ko_ref.dtype)


def _qk_prep(p, rope_c, rope_s, q_norm, k_norm, tm):
    import numpy as np
    rows = p.shape[0]
    seg = lambda w: jnp.asarray(np.kron(np.eye(w // HEAD_DIM), np.full((HEAD_DIM, HEAD_DIM), 1.0 / HEAD_DIM)),
                                jnp.bfloat16)
    const = lambda shape: pl.BlockSpec(shape, lambda i: (0, 0))
    return pl.pallas_call(
        _qk_prep_body,
        grid=(rows // tm,),
        in_specs=[pl.BlockSpec((tm, ATT_Q_W), lambda i: (i, EV_Q_BLK)),
                  pl.BlockSpec((tm, ATT_KV_W), lambda i: (i, EV_K_BLK)),
                  pl.BlockSpec((tm, HEAD_DIM), lambda i: (i, 0)), pl.BlockSpec((tm, HEAD_DIM), lambda i: (i, 0)),
                  const((1, HEAD_DIM)), const((1, HEAD_DIM)),
                  const((ATT_Q_W, ATT_Q_W)), const((ATT_KV_W, ATT_KV_W))],
        out_specs=[pl.BlockSpec((tm, ATT_Q_W), lambda i: (i, 0)), pl.BlockSpec((tm, ATT_KV_W), lambda i: (i, 0))],
        out_shape=[jax.ShapeDtypeStruct((rows, ATT_Q_W), jnp.bfloat16),
                   jax.ShapeDtypeStruct((rows, ATT_KV_W), jnp.bfloat16)],
        name="qk_prep",
    )(p, p, rope_c, rope_s, q_norm.astype(jnp.float32).reshape(1, -1), k_norm.astype(jnp.float32).reshape(1, -1),
      seg(ATT_Q_W), seg(ATT_KV_W))


def _att_body(sink_ref, q_ref, kc_ref, vc_ref, *rest, local):
    if local:
        kp_ref, k0_ref, kn_ref, vp_ref, v0_ref, vn_ref, o_ref = rest
    else:
        (o_ref,) = rest
    b = pl.program_id(0)
    nb = pl.num_programs(0)
    bf = jnp.bfloat16
    scale = HEAD_DIM ** -0.5
    q = q_ref[...]
    kc = kc_ref[...]
    vc = vc_ref[...].astype(bf)
    if local:
        kb = jnp.concatenate([kp_ref[...], k0_ref[...], kn_ref[...]], axis=0)
        vb = jnp.concatenate([vp_ref[...], v0_ref[...], vn_ref[...]], axis=0).astype(bf)
        i = lax.broadcasted_iota(jnp.int32, (ATT_TQ, 3 * ATT_BLOCK), 0)
        j = lax.broadcasted_iota(jnp.int32, (ATT_TQ, 3 * ATT_BLOCK), 1)
        rel = j - ATT_BLOCK - i
        valid = ((jnp.abs(rel) <= ATT_WINDOW) & ((b > 0) | (j >= ATT_BLOCK))
                 & ((b < nb - 1) | (j < 2 * ATT_BLOCK)))
    for h in range(ATT_HEADS):
        hk = h // ATT_GROUP
        qh = q[:, h * HEAD_DIM:(h + 1) * HEAD_DIM]
        kv_cols = slice(hk * HEAD_DIM, (hk + 1) * HEAD_DIM)
        sink = sink_ref[h]
        s_ctx = _dot_nt(qh, kc[:, kv_cols]) * scale
        m = jnp.maximum(jnp.max(s_ctx, axis=-1, keepdims=True), sink)
        if local:
            s_loc = jnp.where(valid, _dot_nt(qh, kb[:, kv_cols]) * scale, -jnp.inf)
            m = jnp.maximum(m, jnp.max(s_loc, axis=-1, keepdims=True))
        p_ctx = jnp.exp(s_ctx - m)
        den = jnp.sum(p_ctx, axis=-1, keepdims=True) + jnp.exp(sink - m)
        acc = jnp.dot(p_ctx.astype(bf), vc[:, kv_cols], preferred_element_type=jnp.float32)
        if local:
            p_loc = jnp.exp(s_loc - m)
            den = den + jnp.sum(p_loc, axis=-1, keepdims=True)
            acc = acc + jnp.dot(p_loc.astype(bf), vb[:, kv_cols], preferred_element_type=jnp.float32)
        o_ref[:, h * HEAD_DIM:(h + 1) * HEAD_DIM] = acc / den


def _attention(qn, kn, p, kcn, pc, sink, local):
    rows = qn.shape[0]
    nb = rows // ATT_TQ
    n_ctx = kcn.shape[0]
    specs = [pl.BlockSpec((ATT_TQ, ATT_Q_W), lambda b, s: (b, 0)),
             pl.BlockSpec((n_ctx, ATT_KV_W), lambda b, s: (0, 0)),
             pl.BlockSpec((n_ctx, ATT_KV_W), lambda b, s: (0, EV_V_BLK))]
    args = [qn, kcn, pc]
    if local:
        prev = lambda b, s: jnp.maximum(b - 1, 0)
        nxt = lambda b, s: jnp.minimum(b + 1, nb - 1)
        for col, arr in ((0, kn), (EV_V_BLK, p)):
            specs += [pl.BlockSpec((ATT_BLOCK, ATT_KV_W), lambda b, s, col=col: (prev(b, s), col)),
                      pl.BlockSpec((ATT_BLOCK, ATT_KV_W), lambda b, s, col=col: (b, col)),
                      pl.BlockSpec((ATT_BLOCK, ATT_KV_W), lambda b, s, col=col: (nxt(b, s), col))]
            args += [arr, arr, arr]
    return pl.pallas_call(
        functools.partial(_att_body, local=local),
        grid_spec=pltpu.PrefetchScalarGridSpec(
            num_scalar_prefetch=1, grid=(nb,), in_specs=specs,
            out_specs=pl.BlockSpec((ATT_TQ, ATT_Q_W), lambda b, s: (b, 0))),
        out_shape=jax.ShapeDtypeStruct((rows, ATT_Q_W), jnp.float32),
        name="window_attention" if local else "context_attention",
    )(sink.astype(jnp.float32), *args)


def _even_merge_body(hy_ref, att_ref, x_ref, gt_ref, w_ref, o_ref):
    bf = jnp.bfloat16
    m = (jnp.dot(hy_ref[...].astype(bf), w_ref[:HY_W, :].astype(bf), preferred_element_type=jnp.float32)
         + jnp.dot(att_ref[...].astype(bf), w_ref[HY_W:, :].astype(bf), preferred_element_type=jnp.float32))
    o_ref[...] = x_ref[...] + gt_ref[...] * m


def _even_merge(hy, att, x2d, gt, w_out, tm):
    rows, d = x2d.shape
    half = pl.BlockSpec((tm, HY_W), lambda i: (i, 0))
    return pl.pallas_call(
        _even_merge_body,
        grid=(rows // tm,),
        in_specs=[half, half, pl.BlockSpec((tm, d), lambda i: (i, 0)), pl.BlockSpec((1, d), lambda i: (0, 0)),
                  pl.BlockSpec((MIX_W, d), lambda i: (0, 0))],
        out_specs=pl.BlockSpec((tm, d), lambda i: (i, 0)),
        out_shape=jax.ShapeDtypeStruct((rows, d), jnp.float32),
        name="even_merge",
    )(hy, att, x2d, gt.reshape(1, -1), w_out)


def _hy_short_body(e_ref, o_ref, sc_ref, u_ref, bias_ref, cf_ref, sf_ref, out_ref):
    bf = jnp.bfloat16
    cf, sf = cf_ref[...], sf_ref[...]
    dot = lambda a, b: jnp.dot(a, b.astype(bf), preferred_element_type=jnp.float32)
    z = u_ref[0]
    for order in range(HY_ORDER):
        cols = slice(order * HY_W, (order + 1) * HY_W)
        h_re = dot(cf, e_ref[:, cols]) * sc_ref[:, cols]
        h_im = dot(sf, o_ref[:, cols]) * sc_ref[:, cols]
        x_re, x_im = dot(cf, z), dot(sf, z)
        y_re = (x_re * h_re - x_im * h_im).astype(bf)
        y_im = (x_re * h_im + x_im * h_re).astype(bf)
        conv = _dot_tn(cf, y_re) + _dot_tn(sf, y_im)
        z = u_ref[1 + order] * (conv + z * bias_ref[order:order + 1, :])
    out_ref[...] = z


def hyena_short(pc, f_w1, f_b1, f_w2, f_b2, f_w3, f_b3, f_w4, f_freq, f_bias, conv_w, conv_b):
    import numpy as np
    rows = pc.shape[0]
    e, od, nrm = _hy_filter_time(rows, f_w1, f_b1, f_w2, f_b2, f_w3, f_b3, f_w4, f_freq, rows)
    u = _hy_conv(pc, conv_w, conv_b, rows)
    ang = 2 * np.pi * np.outer(np.arange(2 * rows), np.arange(rows)) / (2 * rows)
    cf, sf = jnp.asarray(np.cos(ang), jnp.bfloat16), jnp.asarray(-np.sin(ang), jnp.bfloat16)
    scale = 1.0 / (nrm * (2.0 * rows))
    return pl.pallas_call(
        _hy_short_body,
        out_shape=jax.ShapeDtypeStruct((rows, HY_W), jnp.float32),
        name="hy_short",
    )(e, od, scale, u, f_bias.astype(jnp.float32), cf, sf)


def even_layer(x, xc, g, mod, mod_c, w_in, conv_w, conv_b, f_w1, f_b1, f_w2, f_b2, f_w3, f_b3, f_w4, f_freq, f_bias,
               q_norm, k_norm, sink, gt, gt_c, w_out):
    f32 = jnp.float32
    length, n_ctx = x.shape[1], xc.shape[1]
    filt = (f_w1, f_b1, f_w2, f_b2, f_w3, f_b3, f_w4, f_freq, f_bias, conv_w, conv_b)
    o_q = 2 * ATT_KV_W
    o_hy = o_q + ATT_Q_W
    w_perm = jnp.concatenate([w_in[:, o_hy:], w_in[:, o_q:o_hy], w_in[:, :o_q]], axis=1).astype(jnp.bfloat16)
    twice = lambda v: jnp.stack([v.reshape(-1), v.reshape(-1)])
    p = norm_proj(x[0], g, twice(mod[0]), twice(mod[1]), w_perm)
    pc = norm_proj(xc[0], g, twice(mod_c[0]), twice(mod_c[1]), w_perm)
    rope_c, rope_s = _rope_tables(length)
    qn, kn = _qk_prep(p, rope_c, rope_s, q_norm, k_norm, EVEN_TM)
    qcn, kcn = _qk_prep(pc, jnp.ones((n_ctx, HEAD_DIM), f32), jnp.zeros((n_ctx, HEAD_DIM), f32), q_norm, k_norm,
                        n_ctx)
    att = _attention(qn, kn, p, kcn, pc, sink, True)
    att_c = _attention(qcn, None, None, kcn, pc, sink, False)
    hy = hyena_long(p, *filt)
    hy_c = hyena_short(pc, *filt)
    x_new = _even_merge(hy, att, x[0], gt, w_out, EVEN_TM)
    xc_new = _even_merge(hy_c, att_c, xc[0], gt_c, w_out, n_ctx)
    return x_new[None], xc_new[None]


def odd_mixer(h, hc, lb, w_in, conv_w, conv_b, dt_bias, a_log, d_skip, ssd_norm, hg_norm, ctx_out):
    f32 = jnp.float32
    bsz, length, _ = h.shape
    n_ctx = hc.shape[1]
    p = pmm3(h, w_in).astype(f32)
    pc = pmm3(hc, w_in if ctx_out else w_in[:, :ODD_STATE_COLS]).astype(f32)
    o_dt = SSD_CONV_CH
    o_f = SSD_CONV_CH + 2 * SSD_HEADS
    o_i = o_f + 2 * HG_W
    o_z = ODD_STATE_COLS
    o_q = o_z + SSD_W
    o_g = o_q + HG_W
    gn = SSD_GROUPS * SSD_STATE

    def streams(pp):
        n = pp.shape[1]
        xbc = jax.nn.silu(dwconv_centred(pp[..., :SSD_CONV_CH], conv_w, conv_b))
        xs = xbc[..., :SSD_W].reshape(bsz, n, SSD_HEADS, SSD_HEAD_DIM)
        bm = xbc[..., SSD_W:SSD_W + gn].reshape(bsz, n, SSD_GROUPS, SSD_STATE)
        cm = xbc[..., SSD_W + gn:].reshape(bsz, n, SSD_GROUPS, SSD_STATE)
        dt_raw = pp[..., o_dt:o_f].reshape(bsz, n, 2, SSD_HEADS)
        f_raw = pp[..., o_f:o_i].reshape(bsz, n, 2, HG_HEADS, HG_EXPAND)
        iv = pp[..., o_i:o_i + HG_W].reshape(bsz, n, HG_HEADS, HG_VDIM)
        return xs, bm, cm, dt_raw, f_raw, iv

    xs, bm, cm, dt_raw, f_raw, iv = streams(p)
    xs_c, bm_c, cm_c, dt_raw_c, f_raw_c, iv_c = streams(pc)
    q = jax.nn.silu(p[..., o_q:o_g]).reshape(bsz, length, HG_HEADS, HG_EXPAND)
    q_c = jax.nn.silu(pc[..., o_q:o_g]).reshape(bsz, n_ctx, HG_HEADS, HG_EXPAND) if ctx_out else None
    lb = lb.astype(f32).reshape(HG_HEADS, HG_EXPAND)
    ssd0 = jnp.zeros((bsz, SSD_GROUPS, SSD_HEADS // SSD_GROUPS, SSD_HEAD_DIM, SSD_STATE), f32)
    hg0 = jnp.zeros((bsz, HG_HEADS, HG_EXPAND, HG_VDIM), f32)
    y_dirs, o_dirs, yc_dirs, oc_dirs = [], [], [], []
    for d in range(2):
        fl = (lambda a: jnp.flip(a, axis=1)) if d == 1 else (lambda a: a)
        a = -jnp.exp(a_log[d].astype(f32))
        dsk = d_skip[d].astype(f32)
        dtb = dt_bias[d].astype(f32)
        dt_l = jax.nn.softplus(dt_raw[:, :, d] + dtb)
        dt_c = jax.nn.softplus(dt_raw_c[:, :, d] + dtb)
        yc, s_ctx = ssd_scan(fl(xs_c), fl(dt_c), a, fl(bm_c), fl(cm_c), dsk, ssd0, ctx_out)
        yl, _ = ssd_scan(fl(xs), fl(dt_l), a, fl(bm), fl(cm), dsk, s_ctx, True)
        y_dirs.append(fl(yl))
        f_l = lb + (1.0 - lb) * jax.nn.sigmoid(f_raw[:, :, d])
        f_c = lb + (1.0 - lb) * jax.nn.sigmoid(f_raw_c[:, :, d])
        oc, s_hg = hgrn2_scan(fl(q_c) if ctx_out else None, fl(1.0 - f_c), fl(iv_c), fl(jnp.log(f_c)), hg0, ctx_out)
        ol, _ = hgrn2_scan(fl(q), fl(1.0 - f_l), fl(iv), fl(jnp.log(f_l)), s_hg, True)
        o_dirs.append(fl(ol))
        if ctx_out:
            yc_dirs.append(fl(yc))
            oc_dirs.append(fl(oc))

    def merge(yy, oo, pp, n):
        z = pp[..., o_z:o_q]
        g = pp[..., o_g:]
        ys = (yy.reshape(bsz, n, SSD_W) * jax.nn.silu(z)).reshape(bsz, n, SSD_GROUPS, SSD_W // SSD_GROUPS)
        ys = rms_norm(ys, ssd_norm.reshape(SSD_GROUPS, SSD_W // SSD_GROUPS)).reshape(bsz, n, SSD_W)
        hs = rms_norm(oo, hg_norm.reshape(HG_HEADS, HG_VDIM)).reshape(bsz, n, HG_W) * jax.nn.silu(g)
        return jnp.concatenate([ys, hs], axis=-1)

    out = merge(y_dirs[0] + y_dirs[1], o_dirs[0] + o_dirs[1], p, length).astype(h.dtype)
    if not ctx_out:
        return out, None
    out_c = merge(yc_dirs[0] + yc_dirs[1], oc_dirs[0] + oc_dirs[1], pc, n_ctx).astype(hc.dtype)
    return out, out_c


SCAN_Q = 128
SCAN_LEVELS = 7
ODD_COLS = SSD_CONV_CH + 8 * 512
HALO = SUBLANE


def _scan_constants():
    import numpy as np
    q = SCAN_Q
    d_hg, d_ssd, pairs, laters = [], [], [], []
    for direction in (0, 1):
        pos = np.arange(q) if direction == 0 else q - 1 - np.arange(q)
        pj, pt = pos[None, :], pos[:, None]
        top = pj <= pt
        end = pj > pt
        ones = np.ones((SUBLANE, q), bool)
        lv, pr, lt = [], [], []
        for level in range(SCAN_LEVELS):
            b = 2 ** level
            start = (pos // (2 * b)) * (2 * b)
            mid = (start + b)[:, None]
            later = pos >= start + b
            lv.append(np.where(later[:, None], (pj >= mid) & (pj <= pt), (pj > pt) & (pj < mid)))
            pr.append((start[:, None] == start[None, :]) & later[:, None] & ~later[None, :])
            lt.append(np.broadcast_to(later[:, None], (q, LANE)))
        pr.append(np.eye(q, dtype=bool))
        pr.append(top)
        d_hg.append(np.concatenate([top, end] + lv + [ones], axis=0))
        d_ssd.append(np.concatenate([top, end, ones], axis=0))
        pairs.append(np.stack(pr))
        laters.append(np.stack(lt))
    f = np.float32
    twice = lambda m: np.concatenate([m, m], axis=-1)
    return (twice(np.stack(d_hg)).astype(f), twice(np.stack(d_ssd)).astype(f), np.stack(pairs).astype(f),
            np.stack(laters).astype(f))


def _split_dot(mm_bf16, v):
    hi = v.astype(jnp.bfloat16)
    lo = (v - hi.astype(jnp.float32)).astype(jnp.bfloat16)
    return jnp.dot(mm_bf16, jnp.concatenate([hi, lo], axis=0), preferred_element_type=jnp.float32)


def _dot_nt(a, b):
    return lax.dot_general(a, b, (((1,), (1,)), ((), ())), preferred_element_type=jnp.float32)


def _dot_tn(a, b):
    return lax.dot_general(a, b, (((0,), (0,)), ((), ())), preferred_element_type=jnp.float32)


def _softplus(x):
    return jnp.maximum(x, 0.0) + jnp.log1p(jnp.exp(-jnp.abs(x)))


def _scan_body(xbc_ref, prev_ref, next_ref, f_ref, iv_ref, q_ref, dt_ref,
               cw_ref, cb_ref, dtb_ref, a_ref, dsk_ref, lb_ref,
               dhg_ref, dssd_ref, pair_ref, later_ref, sel_ref,
               out_ref, s_ssd, s_hg, ydiag_ref, *, n_ctx_chunks, n_chunks):
    d = pl.program_id(0)
    j = pl.program_id(1)
    q_rows = SCAN_Q
    bf = jnp.bfloat16

    @pl.when(j == 0)
    def _():
        s_ssd[...] = jnp.zeros_like(s_ssd)
        s_hg[...] = jnp.zeros_like(s_hg)

    c = jnp.where(d == 0, j, jnp.where(j < n_ctx_chunks, n_ctx_chunks - 1 - j, n_chunks - 1 + n_ctx_chunks - j))
    first = (c == 0) | (c == n_ctx_chunks)
    last = (c == n_ctx_chunks - 1) | (c == n_chunks - 1)

    u = xbc_ref[...]
    row = lax.broadcasted_iota(jnp.int32, u.shape, 0)
    before = jnp.where(first, 0.0, prev_ref[HALO - 1:HALO, :])
    after = jnp.where(last, 0.0, next_ref[0:1, :])
    up = jnp.where(row == 0, before, pltpu.roll(u, 1, 0))
    un = jnp.where(row == q_rows - 1, after, pltpu.roll(u, q_rows - 1, 0))
    xbc = cw_ref[0:1, :] * up + cw_ref[1:2, :] * u + cw_ref[2:3, :] * un + cb_ref[...]
    xbc = xbc * jax.nn.sigmoid(xbc)
    x = xbc[:, :SSD_W]

    dt = _softplus(dt_ref[...] + dtb_ref[...])
    da = dt * a_ref[...]
    r = _split_dot(dssd_ref[...], da)
    cs, to_end, total = r[:q_rows], r[q_rows:2 * q_rows], r[2 * q_rows:2 * q_rows + 1]
    xdt = x * dt
    cs_hi = cs.astype(bf)
    cs_lo = (cs - cs_hi.astype(jnp.float32)).astype(bf)
    cs_rows = _dot_nt(sel_ref[...], cs_hi) + _dot_nt(sel_ref[...], cs_lo)
    l_mask = pair_ref[SCAN_LEVELS + 1]
    decay_in = jnp.exp2(cs)
    w_end = (jnp.exp2(to_end) * xdt).astype(bf)
    gn = SSD_GROUPS * SSD_STATE
    hpg = SSD_HEADS // SSD_GROUPS
    gw = hpg * SSD_HEAD_DIM
    for g in range(SSD_GROUPS):
        b_g = xbc[:, SSD_W + g * SSD_STATE:SSD_W + (g + 1) * SSD_STATE].astype(bf)
        c_g = xbc[:, SSD_W + gn + g * SSD_STATE:SSD_W + gn + (g + 1) * SSD_STATE].astype(bf)
        scores = _dot_nt(c_g, b_g)
        y_off = jnp.dot(c_g, s_ssd[g].astype(bf), preferred_element_type=jnp.float32)
        for hh in range(hpg):
            h = g * hpg + hh
            lo = h * SSD_HEAD_DIM
            diff = cs[:, lo:lo + 1] - cs_rows[h:h + 1, :]
            decay = jnp.exp2(jnp.minimum(diff, 0.0)) * l_mask
            ydiag_ref[:, lo:lo + SSD_HEAD_DIM] = jnp.dot((scores * decay).astype(bf),
                                                         xdt[:, lo:lo + SSD_HEAD_DIM].astype(bf),
                                                         preferred_element_type=jnp.float32)
        cols = slice(g * gw, (g + 1) * gw)
        out_ref[:, cols] = (ydiag_ref[:, cols] + decay_in[:, cols] * y_off + dsk_ref[:, cols] * x[:, cols])
        s_ssd[g] = jnp.exp2(total[:, cols]) * s_ssd[g] + _dot_tn(b_g, w_end[:, cols])

    lb = lb_ref[...]
    f = lb + (1.0 - lb) * jax.nn.sigmoid(f_ref[...])
    k_in = 1.0 - f
    qv = q_ref[...]
    qv = qv * jax.nn.sigmoid(qv)
    v_bf = iv_ref[...].astype(bf)
    e = jnp.exp2(_split_dot(dhg_ref[...], jnp.log2(f)))
    e_top, e_end = e[:q_rows], e[q_rows:2 * q_rows]
    e_tot = e[(2 + SCAN_LEVELS) * q_rows:(2 + SCAN_LEVELS) * q_rows + 1]
    for h in range(HG_HEADS):
        cols = slice(h * HG_EXPAND, (h + 1) * HG_EXPAND)
        q_h, k_h = qv[:, cols], k_in[:, cols]
        att = pair_ref[SCAN_LEVELS] * _dot_nt(q_h.astype(bf), k_h.astype(bf))
        for level in range(SCAN_LEVELS):
            e_l = e[(2 + level) * q_rows:(3 + level) * q_rows, cols]
            w_l = (jnp.where(later_ref[level] > 0.0, q_h, k_h) * e_l).astype(bf)
            att = att + pair_ref[level] * _dot_nt(w_l, w_l)
        o = jnp.dot(att.astype(bf), v_bf[:, cols], preferred_element_type=jnp.float32)
        o = o + _dot_nt((q_h * e_top[:, cols]).astype(bf), s_hg[h].astype(bf))
        out_ref[:, SSD_W + h * HG_VDIM:SSD_W + (h + 1) * HG_VDIM] = o
        s_hg[h] = e_tot[:, cols] * s_hg[h] + _dot_tn(v_bf[:, cols], (k_h * e_end[:, cols]).astype(bf))


def _odd_scan(p_all, conv_w, conv_b, dtb, a_cols, dsk, lb, n_ctx):
    n_rows = p_all.shape[0]
    n_chunks = n_rows // SCAN_Q
    ncc = n_ctx // SCAN_Q
    d_hg, d_ssd, pairs, laters = _scan_constants()
    bf = jnp.bfloat16
    import numpy as np
    sel = np.zeros((LANE, SSD_W), np.float32)
    sel[np.arange(SSD_HEADS), np.arange(SSD_HEADS) * SSD_HEAD_DIM] = 1.0

    def chunk(d, j):
        return jnp.where(d == 0, j, jnp.where(j < ncc, ncc - 1 - j, n_chunks - 1 + ncc - j))

    per = SCAN_Q // HALO
    last_halo = n_rows // HALO - 1
    col512 = lambda blk: (lambda d, j: (chunk(d, j), blk))
    const2 = lambda shape: pl.BlockSpec(shape, lambda d, j: (0,) * len(shape))
    dirc = lambda shape: pl.BlockSpec((None,) + shape, lambda d, j: (d,) + (0,) * len(shape))
    body = functools.partial(_scan_body, n_ctx_chunks=ncc, n_chunks=n_chunks)
    return pl.pallas_call(
        body,
        grid=(2, n_chunks),
        in_specs=[
            pl.BlockSpec((SCAN_Q, SSD_CONV_CH), lambda d, j: (chunk(d, j), 0)),
            pl.BlockSpec((HALO, SSD_CONV_CH), lambda d, j: (jnp.maximum(chunk(d, j) * per - 1, 0), 0)),
            pl.BlockSpec((HALO, SSD_CONV_CH), lambda d, j: (jnp.minimum((chunk(d, j) + 1) * per, last_halo), 0)),
            pl.BlockSpec((SCAN_Q, 512), lambda d, j: (chunk(d, j), 2 + d)),
            pl.BlockSpec((SCAN_Q, 512), col512(4)),
            pl.BlockSpec((SCAN_Q, 512), col512(6)),
            pl.BlockSpec((SCAN_Q, 512), lambda d, j: (chunk(d, j), 8 + d)),
            const2((CONV_W, SSD_CONV_CH)), const2((1, SSD_CONV_CH)),
            dirc((1, SSD_W)), dirc((1, SSD_W)), dirc((1, SSD_W)), const2((1, HG_W)),
            dirc(d_hg.shape[1:]), dirc(d_ssd.shape[1:]), dirc(pairs.shape[1:]), dirc(laters.shape[1:]),
            const2((LANE, SSD_W)),
        ],
        out_specs=pl.BlockSpec((None, SCAN_Q, MIX_W), lambda d, j: (d, chunk(d, j), 0)),
        out_shape=jax.ShapeDtypeStruct((2, n_rows, MIX_W), jnp.float32),
        scratch_shapes=[pltpu.VMEM((SSD_GROUPS, SSD_STATE, SSD_W // SSD_GROUPS), jnp.float32),
                        pltpu.VMEM((HG_HEADS, HG_VDIM, HG_EXPAND), jnp.float32),
                        pltpu.VMEM((SCAN_Q, SSD_W), jnp.float32)],
        compiler_params=pltpu.CompilerParams(dimension_semantics=("arbitrary", "arbitrary"),
                                             vmem_limit_bytes=MOE_VMEM_LIMIT),
        name="odd_scan",
    )(p_all, p_all, p_all, p_all, p_all, p_all, p_all,
      conv_w, conv_b.reshape(1, -1), dtb, a_cols, dsk, lb.reshape(1, -1),
      jnp.asarray(d_hg, bf), jnp.asarray(d_ssd, bf), jnp.asarray(pairs), jnp.asarray(laters),
      jnp.asarray(sel, bf))


def _group_rms(v, width):
    parts = []
    for lo in range(0, v.shape[1], width):
        seg = v[:, lo:lo + width]
        parts.append(seg * lax.rsqrt(jnp.mean(seg * seg, axis=-1, keepdims=True) + EPS))
    return jnp.concatenate(parts, axis=1)


def _odd_merge_body(yo_ref, z_ref, g_ref, x_ref, sn_ref, hn_ref, gt_ref, w_ref, o_ref):
    yo = yo_ref[0] + yo_ref[1]
    z = z_ref[...]
    g = g_ref[...]
    ys = _group_rms(yo[:, :SSD_W] * (z * jax.nn.sigmoid(z)), SSD_W // SSD_GROUPS) * sn_ref[...]
    hs = _group_rms(yo[:, SSD_W:], HG_VDIM) * hn_ref[...] * (g * jax.nn.sigmoid(g))
    m = jnp.concatenate([ys, hs], axis=1).astype(jnp.bfloat16)
    o_ref[...] = x_ref[...] + gt_ref[...] * jnp.dot(m, w_ref[...].astype(jnp.bfloat16),
                                                    preferred_element_type=jnp.float32)


ODD_TM = 256


def _odd_merge(yo, p_all, x_all, ssd_norm, hg_norm, gt, w_out, n_ctx):
    n_all, d = x_all.shape
    n_lat = n_all - n_ctx
    skip = n_ctx // ODD_TM
    return pl.pallas_call(
        _odd_merge_body,
        grid=(n_lat // ODD_TM,),
        in_specs=[pl.BlockSpec((2, ODD_TM, MIX_W), lambda i: (0, i + skip, 0)),
                  pl.BlockSpec((ODD_TM, 512), lambda i: (i + skip, 5)),
                  pl.BlockSpec((ODD_TM, 512), lambda i: (i + skip, 7)),
                  pl.BlockSpec((ODD_TM, d), lambda i: (i + skip, 0)),
                  pl.BlockSpec((1, SSD_W), lambda i: (0, 0)),
                  pl.BlockSpec((1, HG_W), lambda i: (0, 0)),
                  pl.BlockSpec((1, d), lambda i: (0, 0)),
                  pl.BlockSpec((MIX_W, d), lambda i: (0, 0))],
        out_specs=pl.BlockSpec((ODD_TM, d), lambda i: (i, 0)),
        out_shape=jax.ShapeDtypeStruct((n_lat, d), jnp.float32),
        compiler_params=pltpu.CompilerParams(dimension_semantics=("arbitrary",)),
        name="odd_merge",
    )(yo, p_all, p_all, x_all, ssd_norm.reshape(1, -1), hg_norm.reshape(1, -1), gt.reshape(1, -1), w_out)


def odd_layer(x_all, n_ctx, g, mod, mod_c, lb, w_in, conv_w, conv_b, dt_bias, a_log, d_skip, ssd_norm, hg_norm, gt,
              w_out):
    f32 = jnp.float32
    o_dt = SSD_CONV_CH
    o_f = o_dt + 2 * SSD_HEADS
    rep = lambda v: jnp.repeat(v, SSD_HEAD_DIM, axis=-1)
    w_perm = jnp.concatenate([w_in[:, :o_dt], w_in[:, o_f:], rep(w_in[:, o_dt:o_dt + SSD_HEADS]),
                              rep(w_in[:, o_dt + SSD_HEADS:o_f])], axis=1).astype(jnp.bfloat16)
    both = lambda a, b: jnp.stack([a.reshape(-1), b.reshape(-1)])
    p_all = norm_proj(x_all, g, both(mod_c[0], mod[0]), both(mod_c[1], mod[1]), w_perm, n_first=n_ctx)
    dtb = rep(dt_bias.astype(f32)).reshape(2, 1, SSD_W)
    a_cols = rep(-jnp.exp(a_log.astype(f32)) * math.log2(math.e)).reshape(2, 1, SSD_W)
    dsk = rep(d_skip.astype(f32)).reshape(2, 1, SSD_W)
    yo = _odd_scan(p_all, conv_w, conv_b, dtb, a_cols, dsk, lb.astype(f32), n_ctx)
    return _odd_merge(yo, p_all, x_all, ssd_norm, hg_norm, gt, w_out, n_ctx)


MOE_TM = 256
MOE_BM = 256
NEG_BIG = -1e30
MOE_ISSUE_UNROLL = 4
MOE_VMEM_LIMIT = 52 * 1024 * 1024


def _route_body(x_ref, g_ref, sh_ref, sc_ref, rw_ref, rb_ref, cnt0_ref,
                h_ref, idx_ref, gate_ref, rank_ref, cnt_ref, run_ref):
    i = pl.program_id(0)

    @pl.when(i == 0)
    def _():
        run_ref[...] = cnt0_ref[...]

    tm = x_ref.shape[0]
    x = x_ref[...]
    t = (x * lax.rsqrt(jnp.mean(x * x, axis=-1, keepdims=True) + EPS) * g_ref[...]) * (1.0 + sc_ref[...]) + sh_ref[...]
    h_ref[...] = t
    logits = jnp.dot(t.astype(jnp.bfloat16), rw_ref[...].astype(jnp.bfloat16),
                     preferred_element_type=jnp.float32) + rb_ref[...]
    lane = lax.broadcasted_iota(jnp.int32, (tm, LANE), 1)
    lane_f = lane.astype(jnp.float32)
    work = logits
    vals, sels, hots = [], [], []
    for _ in range(TOP_K):
        m = jnp.max(work, axis=-1, keepdims=True)
        sel = jnp.min(jnp.where(work == m, lane_f, float(LANE)), axis=-1, keepdims=True)
        hot = lane_f == sel
        vals.append(m)
        sels.append(sel.astype(jnp.int32))
        hots.append(hot)
        work = jnp.where(hot, -jnp.inf, work)
    exps = [jnp.exp(v - vals[0]) for v in vals]
    denom = exps[0] + exps[1] + exps[2] + exps[3]
    chosen = jnp.zeros((tm, LANE), jnp.float32)
    for hot in hots:
        chosen = chosen + hot.astype(jnp.float32)
    row = lax.broadcasted_iota(jnp.int32, (tm, tm), 0)
    col = lax.broadcasted_iota(jnp.int32, (tm, tm), 1)
    tri = (row > col).astype(jnp.bfloat16)
    before = jnp.dot(tri, chosen.astype(jnp.bfloat16), preferred_element_type=jnp.float32) + run_ref[0:1, :]
    idx_out = jnp.zeros((tm, LANE), jnp.int32)
    gate_out = jnp.zeros((tm, LANE), jnp.float32)
    rank_out = jnp.zeros((tm, LANE), jnp.int32)
    for k in range(TOP_K):
        rank_k = jnp.sum(jnp.where(hots[k], before, 0.0), axis=-1, keepdims=True).astype(jnp.int32)
        idx_out = jnp.where(lane == k, sels[k], idx_out)
        gate_out = jnp.where(lane == k, exps[k] / denom, gate_out)
        rank_out = jnp.where(lane == k, rank_k, rank_out)
    idx_ref[...] = idx_out
    gate_ref[...] = gate_out
    rank_ref[...] = rank_out
    run_new = run_ref[0:1, :] + jnp.sum(chosen, axis=0, keepdims=True)
    run_ref[...] = jnp.broadcast_to(run_new, run_ref.shape)
    cnt_ref[...] = jnp.broadcast_to(run_new, cnt_ref.shape)


def _moe_route(x2d, g, shift, scale, router_w, router_b, cnt0):
    n_tok, d = x2d.shape
    f32 = jnp.float32
    rw = jnp.pad(router_w, ((0, 0), (0, LANE - N_EXPERTS)))
    rb = jnp.pad(router_b.astype(f32), (0, LANE - N_EXPERTS), constant_values=NEG_BIG).reshape(1, LANE)
    tile = pl.BlockSpec((MOE_TM, LANE), lambda i: (i, 0))
    wide = pl.BlockSpec((MOE_TM, d), lambda i: (i, 0))
    vec = pl.BlockSpec((1, d), lambda i: (0, 0))
    small = pl.BlockSpec((SUBLANE, LANE), lambda i: (0, 0))
    row = lambda v: v.astype(f32).reshape(1, d)
    h, idx, gate, rank, cnt = pl.pallas_call(
        _route_body,
        grid=(n_tok // MOE_TM,),
        in_specs=[wide, vec, vec, vec, pl.BlockSpec((d, LANE), lambda i: (0, 0)),
                  pl.BlockSpec((1, LANE), lambda i: (0, 0)), small],
        out_specs=[wide, tile, tile, tile, small],
        out_shape=[jax.ShapeDtypeStruct((n_tok, d), f32),
                   jax.ShapeDtypeStruct((n_tok, LANE), jnp.int32),
                   jax.ShapeDtypeStruct((n_tok, LANE), f32),
                   jax.ShapeDtypeStruct((n_tok, LANE), jnp.int32),
                   jax.ShapeDtypeStruct((SUBLANE, LANE), f32)],
        scratch_shapes=[pltpu.VMEM((SUBLANE, LANE), f32)],
        compiler_params=pltpu.CompilerParams(dimension_semantics=("arbitrary",)),
        name="moe_route",
    )(x2d, row(g), row(shift), row(scale), rw, rb, cnt0)
    return h, idx[:, :TOP_K], gate, rank[:, :TOP_K], cnt


def _row_copy(src_ref, src_row, dst_ref, dst_row, sem):
    return pltpu.make_async_copy(src_ref.at[pl.ds(src_row, 1)], dst_ref.at[pl.ds(dst_row, 1)], sem)


def _scatter_body(off_ref, pad_ref, dest_ref, *rest, tile_starts):
    t_refs = rest[:-3]
    xs_ref, zero_ref, sem = rest[-3:]
    i = pl.program_id(0)
    tm = t_refs[0].shape[0]

    @pl.when(i == 0)
    def _():
        zero_ref[...] = jnp.zeros_like(zero_ref)
        used = off_ref[N_EXPERTS - 1] + pad_ref[N_EXPERTS - 1]
        n_rows = xs_ref.shape[0]

        def zero_block(start):
            return pltpu.make_async_copy(zero_ref, xs_ref.at[pl.ds(pl.multiple_of(start, MOE_BM), MOE_BM)], sem)

        for e in range(N_EXPERTS):
            tail = n_rows - (e + 1) * MOE_BM

            @pl.when(pad_ref[e] > 0)
            def _():
                zero_block(off_ref[e] + pad_ref[e] - MOE_BM).start()

            @pl.when(tail >= used)
            def _():
                zero_block(tail).start()
        for e in range(N_EXPERTS):
            tail = n_rows - (e + 1) * MOE_BM

            @pl.when(pad_ref[e] > 0)
            def _():
                zero_block(0).wait()

            @pl.when(tail >= used)
            def _():
                zero_block(0).wait()

    for s, t_ref in enumerate(t_refs):
        @pl.when((i >= tile_starts[s]) & (i < tile_starts[s + 1]))
        def _():
            def issue(t, carry):
                for k in range(TOP_K):
                    _row_copy(t_ref, t, xs_ref, dest_ref[TOP_K * t + k], sem).start(priority=k % 2)
                return carry

            lax.fori_loop(0, tm, issue, 0, unroll=MOE_ISSUE_UNROLL)
    for _ in range(TOP_K):
        pltpu.make_async_copy(t_refs[0], xs_ref.at[pl.ds(0, tm)], sem).wait()


def _stream_tiles(streams):
    starts = [0]
    for t in streams:
        starts.append(starts[-1] + t.shape[0] // MOE_TM)
    return tuple(starts)


def _stream_spec(width, starts, s, extra=0):
    lo, n = starts[s], starts[s + 1] - starts[s]
    return pl.BlockSpec((MOE_TM, width), lambda i, *_: (jnp.clip(i + extra - lo, 0, n - 1), 0))


def _moe_scatter(hs, dest_flat, off, padded, n_rows):
    d = hs[0].shape[1]
    starts = _stream_tiles(hs)
    in_specs = [pl.BlockSpec((TOP_K * MOE_TM,), lambda i, off, pad: (i,), memory_space=pltpu.SMEM)]
    in_specs += [_stream_spec(d, starts, s) for s in range(len(hs))]
    return pl.pallas_call(
        functools.partial(_scatter_body, tile_starts=starts),
        grid_spec=pltpu.PrefetchScalarGridSpec(
            num_scalar_prefetch=2,
            grid=(starts[-1],),
            in_specs=in_specs,
            out_specs=pl.BlockSpec(memory_space=pl.ANY),
            scratch_shapes=[pltpu.VMEM((MOE_BM, d), jnp.float32), pltpu.SemaphoreType.DMA],
        ),
        out_shape=jax.ShapeDtypeStruct((n_rows, d), jnp.float32),
        compiler_params=pltpu.CompilerParams(dimension_semantics=("arbitrary",)),
        name="moe_scatter",
    )(off, padded, dest_flat, *hs)


def _expert_body(blk_e_ref, n_act_ref, nxt_e_ref, x_ref, wgu_hbm, bgu_ref, wdn_hbm, bdn_ref, y_ref,
                 wgu_f32, wdn_f32, wgu_bf, wdn_bf, sems, slot_ref, *, layer):
    i = pl.program_id(0)

    def weight_copies(e, slot):
        return (pltpu.make_async_copy(wgu_hbm.at[layer, e], wgu_f32.at[slot], sems.at[slot, 0]),
                pltpu.make_async_copy(wdn_hbm.at[layer, e], wdn_f32.at[slot], sems.at[slot, 1]))

    @pl.when(i < n_act_ref[0])
    def _():
        e = blk_e_ref[i]
        prev = blk_e_ref[jnp.maximum(i - 1, 0)]

        @pl.when((i == 0) | (e != prev))
        def _():
            @pl.when(i == 0)
            def _():
                slot_ref[0] = 0
                for c in weight_copies(e, 0):
                    c.start()

            slot = slot_ref[0]
            for c in weight_copies(e, slot):
                c.wait()
            wgu_bf[...] = wgu_f32[slot].astype(jnp.bfloat16)
            wdn_bf[...] = wdn_f32[slot].astype(jnp.bfloat16)
            nxt = nxt_e_ref[e]

            @pl.when(nxt >= 0)
            def _():
                for c in weight_copies(nxt, 1 - slot):
                    c.start()

            slot_ref[0] = 1 - slot

        x = x_ref[...].astype(jnp.bfloat16)
        gu = jnp.dot(x, wgu_bf[...], preferred_element_type=jnp.float32) + bgu_ref[...]
        gate = jnp.minimum(gu[:, :D_EXPERT], SWIGLU_LIMIT)
        up = jnp.clip(gu[:, D_EXPERT:], -SWIGLU_LIMIT, SWIGLU_LIMIT)
        act = (up + 1.0) * gate * jax.nn.sigmoid(SWIGLU_ALPHA * gate)
        y_ref[...] = jnp.dot(act.astype(jnp.bfloat16), wdn_bf[...],
                             preferred_element_type=jnp.float32) + bdn_ref[...]

    @pl.when(i >= n_act_ref[0])
    def _():
        y_ref[...] = jnp.zeros_like(y_ref)


def _moe_experts(xs, blk_e, n_act, nxt_e, w_gu, b_gu, w_dn, b_dn, layer):
    n_rows, d = xs.shape
    n_blk = n_rows // MOE_BM

    def blk(i, be, na, nx):
        return jnp.minimum(i, na[0] - 1)

    return pl.pallas_call(
        functools.partial(_expert_body, layer=layer),
        grid_spec=pltpu.PrefetchScalarGridSpec(
            num_scalar_prefetch=3,
            grid=(n_blk,),
            in_specs=[pl.BlockSpec((MOE_BM, d), lambda i, be, na, nx: (blk(i, be, na, nx), 0)),
                      pl.BlockSpec(memory_space=pl.ANY),
                      pl.BlockSpec((None, 1, 2 * D_EXPERT), lambda i, be, na, nx: (be[blk(i, be, na, nx)], 0, 0)),
                      pl.BlockSpec(memory_space=pl.ANY),
                      pl.BlockSpec((None, 1, d), lambda i, be, na, nx: (be[blk(i, be, na, nx)], 0, 0))],
            out_specs=pl.BlockSpec((MOE_BM, d), lambda i, be, na, nx: (i, 0)),
            scratch_shapes=[pltpu.VMEM((2, d, 2 * D_EXPERT), jnp.float32),
                            pltpu.VMEM((2, D_EXPERT, d), jnp.float32),
                            pltpu.VMEM((d, 2 * D_EXPERT), jnp.bfloat16),
                            pltpu.VMEM((D_EXPERT, d), jnp.bfloat16),
                            pltpu.SemaphoreType.DMA((2, 2)),
                            pltpu.SMEM((1,), jnp.int32)],
        ),
        out_shape=jax.ShapeDtypeStruct((n_rows, d), jnp.float32),
        compiler_params=pltpu.CompilerParams(dimension_semantics=("arbitrary",),
                                             vmem_limit_bytes=MOE_VMEM_LIMIT),
        name="moe_experts",
    )(blk_e, n_act, nxt_e, xs, w_gu, b_gu.reshape(N_EXPERTS, 1, -1), w_dn, b_dn.reshape(N_EXPERTS, 1, -1))


def _combine_body(dest_ref, dest_nxt_ref, *rest, tile_starts):
    n_streams = len(tile_starts) - 1
    ins, ys_ref = rest[:3 * n_streams], rest[3 * n_streams]
    y_refs = rest[3 * n_streams + 1:-2]
    if len(y_refs) == 1:
        y_refs = y_refs * n_streams
    buf_ref, sems = rest[-2:]
    i = pl.program_id(0)
    n = pl.num_programs(0)
    tm = y_refs[0].shape[0]

    def fetch(d_ref, slot):
        def issue(t, carry):
            for k in range(TOP_K):
                _row_copy(ys_ref, d_ref[TOP_K * t + k], buf_ref.at[slot, k], t,
                          sems.at[slot]).start(priority=k % 2)
            return carry
        lax.fori_loop(0, tm, issue, 0, unroll=MOE_ISSUE_UNROLL)

    @pl.when(i == 0)
    def _():
        fetch(dest_ref, 0)

    @pl.when(i + 1 < n)
    def _():
        fetch(dest_nxt_ref, (i + 1) % 2)

    slot = i % 2
    for k in range(TOP_K):
        pltpu.make_async_copy(ys_ref.at[pl.ds(0, tm)], buf_ref.at[slot, k], sems.at[slot]).wait()
    for s in range(n_streams):
        gate_ref, x_ref, gt_ref = ins[3 * s:3 * s + 3]

        @pl.when((i >= tile_starts[s]) & (i < tile_starts[s + 1]))
        def _():
            g = gate_ref[...]
            acc = g[:, 0:1] * buf_ref[slot, 0]
            for k in range(1, TOP_K):
                acc = acc + g[:, k:k + 1] * buf_ref[slot, k]
            y_refs[s][...] = x_ref[...] + gt_ref[...] * acc


def _moe_combine(ys, dest_flat, gates, xs2d, gts, joint):
    d = ys.shape[1]
    starts = _stream_tiles(xs2d)
    n_tiles = starts[-1]
    in_specs = [pl.BlockSpec((TOP_K * MOE_TM,), lambda i: (i,), memory_space=pltpu.SMEM),
                pl.BlockSpec((TOP_K * MOE_TM,), lambda i: (jnp.minimum(i + 1, n_tiles - 1),),
                             memory_space=pltpu.SMEM)]
    args = [dest_flat, dest_flat]
    for s, (gate, x2d, gt) in enumerate(zip(gates, xs2d, gts)):
        in_specs += [_stream_spec(LANE, starts, s), _stream_spec(d, starts, s), pl.BlockSpec((1, d), lambda i: (0, 0))]
        args += [gate, x2d, gt.astype(jnp.float32).reshape(1, d)]
    in_specs.append(pl.BlockSpec(memory_space=pl.ANY))
    args.append(ys)
    if joint:
        out_specs = [pl.BlockSpec((MOE_TM, d), lambda i: (i, 0))]
        out_shape = [jax.ShapeDtypeStruct((n_tiles * MOE_TM, d), jnp.float32)]
    else:
        out_specs = [_stream_spec(d, starts, s) for s in range(len(xs2d))]
        out_shape = [jax.ShapeDtypeStruct(x2d.shape, jnp.float32) for x2d in xs2d]
    return pl.pallas_call(
        functools.partial(_combine_body, tile_starts=starts),
        grid=(n_tiles,),
        in_specs=in_specs,
        out_specs=out_specs,
        out_shape=out_shape,
        scratch_shapes=[pltpu.VMEM((2, TOP_K, MOE_TM, d), jnp.float32), pltpu.SemaphoreType.DMA((2,))],
        compiler_params=pltpu.CompilerParams(dimension_semantics=("arbitrary",)),
        name="moe_combine",
    )(*args)


def moe_layer(streams, g, router_w, router_b, w_gu, b_gu, w_dn, b_dn, layer, joint=False):
    i32 = jnp.int32
    routed = []
    cnt = jnp.zeros((SUBLANE, LANE), jnp.float32)
    for x2d, shift, scale, _ in streams:
        assert x2d.shape[0] % MOE_TM == 0
        h, idx, gate, rank, cnt = _moe_route(x2d, g, shift, scale, router_w, router_b, cnt)
        routed.append((h, idx, gate, rank))
    counts = cnt[0, :N_EXPERTS].astype(i32)
    padded = (counts + MOE_BM - 1) // MOE_BM * MOE_BM
    pad_end = jnp.cumsum(padded)
    off = (pad_end - padded).astype(i32)
    padded = padded.astype(i32)
    n_tok = sum(s[0].shape[0] for s in streams)
    n_blk = -(-(n_tok * TOP_K) // MOE_BM) + N_EXPERTS
    blk_e = jnp.minimum(jnp.sum(jnp.arange(n_blk)[:, None] * MOE_BM >= pad_end[None, :], axis=1),
                        N_EXPERTS - 1).astype(i32)
    n_act = (pad_end[-1:] // MOE_BM).astype(i32)
    experts = jnp.arange(N_EXPERTS, dtype=i32)
    later_with_rows = (padded > 0)[None, :] & (experts[None, :] > experts[:, None])
    nxt_e = jnp.min(jnp.where(later_with_rows, experts[None, :], N_EXPERTS), axis=1)
    nxt_e = jnp.where(nxt_e == N_EXPERTS, -1, nxt_e).astype(i32)
    dests = []
    for _, idx, _, rank in routed:
        dest = rank + jnp.sum(jnp.where(idx[..., None] == experts, off, 0), axis=-1)
        dests.append(dest.reshape(-1).astype(i32))
    dest_flat = jnp.concatenate(dests)
    xs = _moe_scatter([r[0] for r in routed], dest_flat, off, padded, n_blk * MOE_BM)
    ys = _moe_experts(xs, blk_e, n_act, nxt_e, w_gu, b_gu, w_dn, b_dn, layer)
    return _moe_combine(ys, dest_flat, [r[2] for r in routed], [s[0] for s in streams], [s[3] for s in streams],
                        joint)


def kernel(x, c, ctx, c_ctx, norm_g, ada_w, ada_b, w_out, w_in_even, hy_conv_w, hy_conv_b,
           hy_w1, hy_b1, hy_w2, hy_b2, hy_w3, hy_b3, hy_w4, hy_freq, hy_filter_bias,
           att_q_norm, att_k_norm, att_sink, w_in_odd, ssd_conv_w, ssd_conv_b, ssd_dt_bias,
           ssd_A_log, ssd_D, ssd_norm, hg_lower_bounds, hg_norm, router_w, router_b,
           moe_w_gu, moe_b_gu, moe_w_dn, moe_b_dn):
    lbs = jax.nn.softmax(hg_lower_bounds.astype(jnp.float32), axis=0)
    lbs = jnp.cumsum(lbs, axis=0) - lbs[0]
    xc = ctx
    n_ctx = ctx.shape[1]
    for layer in range(DEPTH):
        ctx_out = layer < DEPTH - 1
        i = layer // 2
        sh, sc, gt = adaln(c, ada_w[layer], ada_b[layer], 0)
        sh_c, sc_c, gt_c = adaln(c_ctx, ada_w[layer], ada_b[layer], 0)
        assert (layer % 2 == 0) == ctx_out
        moe = functools.partial(moe_layer, g=norm_g[layer, 1], router_w=router_w[layer], router_b=router_b[layer],
                                w_gu=moe_w_gu, b_gu=moe_b_gu[layer], w_dn=moe_w_dn, b_dn=moe_b_dn[layer], layer=layer)
        sh2, sc2, gt2 = adaln(c, ada_w[layer], ada_b[layer], 1)
        if layer % 2 == 0:
            x, xc = even_layer(x, xc, norm_g[layer, 0], (sh, sc), (sh_c, sc_c), w_in_even[i], hy_conv_w[i],
                               hy_conv_b[i], hy_w1[i], hy_b1[i], hy_w2[i], hy_b2[i], hy_w3[i], hy_b3[i], hy_w4[i],
                               hy_freq[i], hy_filter_bias[i], att_q_norm[i], att_k_norm[i], att_sink[i], gt, gt_c,
                               w_out[layer])
            sh2_c, sc2_c, gt2_c = adaln(c_ctx, ada_w[layer], ada_b[layer], 1)
            x_all = moe([(xc[0], sh2_c, sc2_c, gt2_c), (x[0], sh2, sc2, gt2)], joint=True)[0]
        else:
            x = odd_layer(x_all, n_ctx, norm_g[layer, 0], (sh, sc), (sh_c, sc_c), lbs[layer], w_in_odd[i],
                          ssd_conv_w[i], ssd_conv_b[i], ssd_dt_bias[i], ssd_A_log[i], ssd_D[i], ssd_norm[i],
                          hg_norm[i], gt, w_out[layer])
            x = moe([(x, sh2, sc2, gt2)])[0][None]
    return x
```

```python
import functools
import math

import jax
import jax.numpy as jnp
from jax import lax
from jax.experimental import pallas as pl
from jax.experimental.pallas import tpu as pltpu

D_MODEL = 1024
DEPTH = 2
GRID_W = 64
MIX_W = D_MODEL
EPS = 1e-6
CONV_W = 3

HY_W = MIX_W // 2
HY_ORDER = 2
HY_EMB = 33
HY_FFN = 64
HY_TARGET = 1e-2
HY_SHORT_PCT = 0.3
HY_LONG_PCT = 1.5

HEAD_DIM = 64
ATT_HEADS = (MIX_W // 2) // HEAD_DIM
ATT_KV_HEADS = 2
ATT_GROUP = ATT_HEADS // ATT_KV_HEADS
ATT_WINDOW = 128
ATT_BLOCK = 128
ROPE_BASE = 10000.0
ATT_Q_W = ATT_HEADS * HEAD_DIM
ATT_KV_W = ATT_KV_HEADS * HEAD_DIM

SSD_W = MIX_W // 2
SSD_HEAD_DIM = 64
SSD_HEADS = SSD_W // SSD_HEAD_DIM
SSD_GROUPS = 2
SSD_STATE = 128
SSD_CHUNK = 128
SSD_CONV_CH = SSD_W + 2 * SSD_GROUPS * SSD_STATE

HG_W = MIX_W // 2
HG_EXPAND = 128
HG_HEADS = HG_W // HG_EXPAND
HG_VDIM = HG_W // HG_HEADS
HG_CHUNK = 64

N_EXPERTS = 32
TOP_K = 4
D_EXPERT = D_MODEL
SWIGLU_ALPHA = 1.702
SWIGLU_LIMIT = 7.0
MOE_BLOCK = 128

EVEN_IN = 2 * ATT_KV_W + ATT_Q_W + 3 * HY_W
ODD_STATE_COLS = SSD_CONV_CH + 2 * SSD_HEADS + 3 * HG_W
ODD_IN = ODD_STATE_COLS + SSD_W + 2 * HG_W

LANE = 128
SUBLANE = 8


def _mm_body(a_ref, b_ref, o_ref):
    a = a_ref[...].astype(jnp.bfloat16)
    b = b_ref[...].astype(jnp.bfloat16)
    o_ref[...] = jnp.dot(a, b, preferred_element_type=jnp.float32)


def _pick_tile(n, candidates):
    for c in candidates:
        if n % c == 0:
            return c
    return n


def pmm(a, b):
    m, k = a.shape
    n = b.shape[1]
    n_pad = -(-n // LANE) * LANE
    if n_pad != n:
        b = jnp.pad(b, ((0, 0), (0, n_pad - n)))
    m_pad = -(-m // SUBLANE) * SUBLANE
    if m_pad != m:
        a = jnp.pad(a, ((0, m_pad - m), (0, 0)))
    tm = _pick_tile(m_pad, (512, 256, 128, 64, 32, 16, 8))
    tn = _pick_tile(n_pad, (512, 384, 256, 128))
    out = pl.pallas_call(
        _mm_body,
        grid=(m_pad // tm, n_pad // tn),
        in_specs=[pl.BlockSpec((tm, k), lambda i, j: (i, 0)),
                  pl.BlockSpec((k, tn), lambda i, j: (0, j))],
        out_specs=pl.BlockSpec((tm, tn), lambda i, j: (i, j)),
        out_shape=jax.ShapeDtypeStruct((m_pad, n_pad), jnp.float32),
        name="dense_mm",
    )(a, b)
    return out[:m, :n]


PROJ_TM = 256
PROJ_VMEM_LIMIT = 56 * 1024 * 1024


def _norm_proj_body(x_ref, g_ref, sh_ref, sc_ref, w_ref, o_ref):
    x = x_ref[...]
    y = x * lax.rsqrt(jnp.mean(x * x, axis=-1, keepdims=True) + EPS) * g_ref[...]
    h = y * (1.0 + sc_ref[...]) + sh_ref[...]
    o_ref[...] = jnp.dot(h.astype(jnp.bfloat16), w_ref[...], preferred_element_type=jnp.float32)


def norm_proj(x2d, g, shift, scale, w_bf16, n_first=0):
    rows, d = x2d.shape
    n = w_bf16.shape[1]
    assert rows % PROJ_TM == 0 and n_first % PROJ_TM == 0 and n % LANE == 0
    first_tiles = n_first // PROJ_TM
    mod = pl.BlockSpec((None, 1, d), lambda i: (jnp.where(i < first_tiles, 0, 1), 0, 0))
    return pl.pallas_call(
        _norm_proj_body,
        grid=(rows // PROJ_TM,),
        in_specs=[pl.BlockSpec((PROJ_TM, d), lambda i: (i, 0)),
                  pl.BlockSpec((1, d), lambda i: (0, 0)), mod, mod,
                  pl.BlockSpec((d, n), lambda i: (0, 0))],
        out_specs=pl.BlockSpec((PROJ_TM, n), lambda i: (i, 0)),
        out_shape=jax.ShapeDtypeStruct((rows, n), jnp.float32),
        compiler_params=pltpu.CompilerParams(dimension_semantics=("arbitrary",),
                                             vmem_limit_bytes=PROJ_VMEM_LIMIT),
        name="norm_proj",
    )(x2d, g.astype(jnp.float32).reshape(1, d), shift.astype(jnp.float32).reshape(2, 1, d),
      scale.astype(jnp.float32).reshape(2, 1, d), w_bf16)


def pmm3(a, b):
    lead = a.shape[:-1]
    return pmm(a.reshape(-1, a.shape[-1]), b).reshape(*lead, b.shape[-1])


def rms_norm(x, g):
    xf = x.astype(jnp.float32)
    y = xf * lax.rsqrt(jnp.mean(xf * xf, axis=-1, keepdims=True) + EPS)
    return (y * g.astype(jnp.float32)).astype(x.dtype)


def modulate(h, shift, scale):
    return h * (1.0 + scale) + shift


def adaln(cond, w, b, j):
    lo, hi = 3 * j * D_MODEL, 3 * (j + 1) * D_MODEL
    m = jax.nn.silu(cond) @ w[:, lo:hi] + b[lo:hi]
    return jnp.split(m, 3, axis=-1)


def dwconv_centred(u, w, b):
    ch = u.shape[-1]
    y = lax.conv_general_dilated(u, w[:, None, :].astype(u.dtype), window_strides=(1,),
                                 padding=[(CONV_W // 2, CONV_W // 2)],
                                 dimension_numbers=('NWC', 'WIO', 'NWC'), feature_group_count=ch)
    return y + b.astype(u.dtype)


def axial_rope_tables(length):
    rows = length // GRID_W
    n_pairs = HEAD_DIM // 4
    inv = ROPE_BASE ** (-jnp.arange(n_pairs, dtype=jnp.float32) / n_pairs)
    row_ang = jnp.arange(rows, dtype=jnp.float32)[:, None] * inv
    col_ang = jnp.arange(GRID_W, dtype=jnp.float32)[:, None] * inv
    ang_r = jnp.broadcast_to(row_ang[:, None], (rows, GRID_W, n_pairs)).reshape(length, n_pairs)
    ang_c = jnp.broadcast_to(col_ang[None], (rows, GRID_W, n_pairs)).reshape(length, n_pairs)
    return jnp.cos(ang_r), jnp.sin(ang_r), jnp.cos(ang_c), jnp.sin(ang_c)


def _rotate(u, cos, sin):
    n = u.shape[-1] // 2
    u1, u2 = u[..., :n], u[..., n:]
    cos = cos[None, :, None, :]
    sin = sin[None, :, None, :]
    return jnp.concatenate([u1 * cos - u2 * sin, u1 * sin + u2 * cos], axis=-1)


def apply_axial_rope(u, tables):
    cr, sr, cc, sc = tables
    half = HEAD_DIM // 2
    return jnp.concatenate([_rotate(u[..., :half], cr, sr), _rotate(u[..., half:], cc, sc)], axis=-1)


def hyena_filters(length, w1, b1, w2, b2, w3, b3, w4, freq):
    f32 = jnp.float32
    t = jnp.linspace(0.0, 1.0, length, dtype=f32)[:, None]
    bands = (HY_EMB - 1) // 2
    w_ang = 2.0 * math.pi * jnp.arange(length, dtype=f32)[:, None] / length
    fr = jnp.linspace(1e-4, bands - 1, bands, dtype=f32)[None]
    z = jnp.concatenate([t, jnp.cos(fr * w_ang), -jnp.sin(fr * w_ang)], axis=-1)
    fq = freq.astype(f32)
    hdn = jnp.sin(fq * (z @ w1.astype(f32) + b1.astype(f32)))
    hdn = jnp.sin(fq * (hdn @ w2.astype(f32) + b2.astype(f32)))
    hdn = jnp.sin(fq * (hdn @ w3.astype(f32) + b3.astype(f32)))
    h = (hdn @ w4.astype(f32)).reshape(length, HY_ORDER, 2, HY_W)
    max_decay = math.log(HY_TARGET) / HY_SHORT_PCT
    min_decay = math.log(HY_TARGET) / HY_LONG_PCT
    deltas = jnp.abs(jnp.linspace(min_decay, max_decay, HY_W, dtype=f32))
    h = h * jnp.exp(-t * deltas)[:, None, None, :]
    h2 = jnp.concatenate([h[:, :, 0], jnp.zeros((1, HY_ORDER, HY_W), f32), h[:0:-1, :, 1]], axis=0)
    h2 = h2 / jnp.sum(jnp.abs(h2), axis=0, keepdims=True)
    return jnp.fft.rfft(h2, axis=0)


def hyena_mix(u, hf, filter_bias, conv_w, conv_b):
    length = u.shape[1]
    u = dwconv_centred(u.astype(jnp.float32), conv_w, conv_b)
    v, x1, x2 = jnp.split(u, 3, axis=-1)
    z = v
    for o, gate in enumerate((x1, x2)):
        zf = jnp.fft.rfft(z, n=2 * length, axis=1)
        zc = jnp.fft.irfft(zf * hf[None, :, o], n=2 * length, axis=1)[:, :length]
        z = gate * (zc + z * filter_bias[o].astype(jnp.float32))
    return z


def window_attention(q, k, v, k_c, v_c, sink):
    bsz, length = q.shape[:2]
    nb = length // ATT_BLOCK
    scale = HEAD_DIM ** -0.5
    qb = q.reshape(bsz, nb, ATT_BLOCK, ATT_KV_HEADS, ATT_GROUP, HEAD_DIM)
    pad = ((0, 0), (ATT_BLOCK, ATT_BLOCK), (0, 0), (0, 0))

    def band(a):
        ap = jnp.pad(a, pad).reshape(bsz, nb + 2, ATT_BLOCK, ATT_KV_HEADS, HEAD_DIM)
        return jnp.concatenate([ap[:, :-2], ap[:, 1:-1], ap[:, 2:]], axis=2)

    kw, vw = band(k), band(v)
    s_loc = jnp.einsum('bnqhgd,bnkhd->bnhgqk', qb, kw) * scale
    s_ctx = jnp.einsum('bnqhgd,bchd->bnhgqc', qb, k_c) * scale
    qpos = jnp.arange(nb)[:, None] * ATT_BLOCK + jnp.arange(ATT_BLOCK)[None]
    kpos = (jnp.arange(nb)[:, None] - 1) * ATT_BLOCK + jnp.arange(3 * ATT_BLOCK)[None]
    rel = kpos[:, None, :] - qpos[:, :, None]
    valid = (jnp.abs(rel) <= ATT_WINDOW) & (kpos[:, None, :] >= 0) & (kpos[:, None, :] < length)
    s_loc = jnp.where(valid[None, :, None, None], s_loc, -jnp.inf)
    sink_l = jnp.broadcast_to(sink.astype(jnp.float32).reshape(1, 1, ATT_KV_HEADS, ATT_GROUP, 1, 1),
                              s_loc.shape[:-1] + (1,))
    p = jax.nn.softmax(jnp.concatenate([s_loc, s_ctx, sink_l], axis=-1), axis=-1)
    n_loc = 3 * ATT_BLOCK
    n_ctx = k_c.shape[1]
    o = (jnp.einsum('bnhgqk,bnkhd->bnqhgd', p[..., :n_loc], vw)
         + jnp.einsum('bnhgqc,bchd->bnqhgd', p[..., n_loc:n_loc + n_ctx], v_c))
    return o.reshape(bsz, length, ATT_Q_W)


def context_attention(q_c, k_c, v_c, sink):
    bsz, n_ctx = q_c.shape[:2]
    s = jnp.einsum('bqhgd,bkhd->bhgqk', q_c, k_c) * HEAD_DIM ** -0.5
    sink_l = jnp.broadcast_to(sink.astype(jnp.float32).reshape(1, ATT_KV_HEADS, ATT_GROUP, 1, 1),
                              s.shape[:-1] + (1,))
    p = jax.nn.softmax(jnp.concatenate([s, sink_l], axis=-1), axis=-1)[..., :-1]
    return jnp.einsum('bhgqk,bkhd->bqhgd', p, v_c).reshape(bsz, n_ctx, ATT_Q_W)


def ssd_scan(x, dt, a, bm, cm, d_skip, init, need_y):
    bsz, length, n_heads, hd = x.shape
    nc = length // SSD_CHUNK
    hpg = n_heads // SSD_GROUPS
    da = (dt * a).reshape(bsz, nc, SSD_CHUNK, SSD_GROUPS, hpg)
    cs = jnp.cumsum(da, axis=2)
    xdt = (x * dt[..., None]).reshape(bsz, nc, SSD_CHUNK, SSD_GROUPS, hpg, hd)
    bc = bm.reshape(bsz, nc, SSD_CHUNK, SSD_GROUPS, SSD_STATE)
    cc = cm.reshape(bsz, nc, SSD_CHUNK, SSD_GROUPS, SSD_STATE)
    to_end = jnp.exp(cs[:, :, -1:] - cs)
    states = jnp.einsum('bcsgn,bcsgh,bcsghp->bcghpn', bc, to_end, xdt)
    chunk_decay = jnp.exp(cs[:, :, -1])

    def step(s, inp):
        st, dec = inp
        return s * dec[..., None, None] + st, s

    s_final, s_in = lax.scan(step, init, (jnp.moveaxis(states, 1, 0), jnp.moveaxis(chunk_decay, 1, 0)))
    if not need_y:
        return None, s_final
    s_in = jnp.moveaxis(s_in, 0, 1)
    cs_t = jnp.moveaxis(cs, 2, -1)
    diff = cs_t[..., :, None] - cs_t[..., None, :]
    lower = jnp.tril(jnp.ones((SSD_CHUNK, SSD_CHUNK), bool))
    decay = jnp.where(lower, jnp.exp(jnp.where(lower, diff, 0.0)), 0.0)
    scores = jnp.einsum('bclgn,bcsgn->bcgls', cc, bc)
    y_diag = jnp.einsum('bcgls,bcghls,bcsghp->bclghp', scores, decay, xdt)
    y_off = jnp.einsum('bclgn,bcghpn,bclgh->bclghp', cc, s_in, jnp.exp(cs))
    y = (y_diag + y_off).reshape(bsz, length, n_heads, hd) + d_skip[:, None] * x
    return y, s_final


def hgrn2_scan(q, k, v, g, init, need_o):
    bsz, length, n_heads, _ = k.shape
    nc = length // HG_CHUNK

    def chunks(a):
        return a.reshape(bsz, nc, HG_CHUNK, n_heads, a.shape[-1]).transpose(1, 0, 3, 2, 4)

    lower = jnp.tril(jnp.ones((HG_CHUNK, HG_CHUNK), bool))[:, :, None]

    def update(s, kc, vc, cum):
        last = cum[:, :, -1]
        return (s * jnp.exp(last)[..., None]
                + jnp.einsum('bhsk,bhsv->bhkv', kc * jnp.exp(last[:, :, None] - cum), vc))

    if not need_o:
        def step_state(s, inp):
            kc, vc, gc = inp
            return update(s, kc, vc, jnp.cumsum(gc, axis=2)), None
        s_final, _ = lax.scan(step_state, init, (chunks(k), chunks(v), chunks(g)))
        return None, s_final

    def step(s, inp):
        qc, kc, vc, gc = inp
        cum = jnp.cumsum(gc, axis=2)
        diff = cum[:, :, :, None, :] - cum[:, :, None, :, :]
        decay = jnp.where(lower, jnp.exp(jnp.where(lower, diff, 0.0)), 0.0)
        att = jnp.einsum('bhtk,bhsk,bhtsk->bhts', qc, kc, decay)
        o = (jnp.einsum('bhtk,bhkv->bhtv', qc * jnp.exp(cum), s)
             + jnp.einsum('bhts,bhsv->bhtv', att, vc))
        return update(s, kc, vc, cum), o

    s_final, o = lax.scan(step, init, (chunks(q), chunks(k), chunks(v), chunks(g)))
    return o.transpose(1, 0, 3, 2, 4).reshape(bsz, length, n_heads, v.shape[-1]), s_final


HY_N1 = 128
HY_N2 = 256
HY_NB = 8
HY_TM = 512


def _hy_dft_constants(length):
    import numpy as np
    n = 2 * length
    assert n == HY_N1 * HY_N2
    half = HY_N1 // 2
    k1 = np.arange(HY_N1)[:, None]
    n1 = np.arange(half)[None, :]
    n2 = np.arange(HY_N2)[:, None, None]
    ang = 2 * np.pi * (k1 * n1 / HY_N1)[None] + 2 * np.pi * n2 * k1[None] / n
    fwd = np.stack([np.cos(ang), -np.sin(ang)], axis=2).reshape(HY_N2, 2 * HY_N1, half)
    inv = fwd.transpose(0, 2, 1)
    a2 = 2 * np.pi * np.outer(np.arange(HY_N2), np.arange(HY_N2)) / HY_N2
    c, s = np.cos(a2), -np.sin(a2)
    m_fwd = np.concatenate([np.stack([c, -s], axis=2).reshape(HY_N2, 2 * HY_N2),
                            np.stack([s, c], axis=2).reshape(HY_N2, 2 * HY_N2)], axis=0)
    m_inv = np.stack([np.concatenate([c, s], axis=1), np.concatenate([-s, c], axis=1)],
                     axis=1).reshape(2 * HY_N2, 2 * HY_N2)
    f = np.float32
    return fwd.astype(f), inv.astype(f), m_fwd.astype(f), m_inv.astype(f)


def _hy_conv_body(p_ref, prev_ref, next_ref, w_ref, b_ref, o_ref):
    i = pl.program_id(1)
    n = pl.num_programs(1)
    u = p_ref[...]
    rows = u.shape[0]
    row = lax.broadcasted_iota(jnp.int32, u.shape, 0)
    before = jnp.where(i == 0, 0.0, prev_ref[HALO - 1:HALO, :])
    after = jnp.where(i == n - 1, 0.0, next_ref[0:1, :])
    up = jnp.where(row == 0, before, pltpu.roll(u, 1, 0))
    un = jnp.where(row == rows - 1, after, pltpu.roll(u, rows - 1, 0))
    o_ref[...] = w_ref[0:1, :] * up + w_ref[1:2, :] * u + w_ref[2:3, :] * un + b_ref[...]


def _hy_conv(p, conv_w, conv_b, tm):
    length = p.shape[0]
    per = tm // HALO
    last = length // HALO - 1
    return pl.pallas_call(
        _hy_conv_body,
        grid=(3, length // tm),
        in_specs=[pl.BlockSpec((tm, HY_W), lambda c, i: (i, c)),
                  pl.BlockSpec((HALO, HY_W), lambda c, i: (jnp.maximum(i * per - 1, 0), c)),
                  pl.BlockSpec((HALO, HY_W), lambda c, i: (jnp.minimum((i + 1) * per, last), c)),
                  pl.BlockSpec((CONV_W, HY_W), lambda c, i: (0, c)),
                  pl.BlockSpec((1, HY_W), lambda c, i: (0, c))],
        out_specs=pl.BlockSpec((None, tm, HY_W), lambda c, i: (c, i, 0)),
        out_shape=jax.ShapeDtypeStruct((3, length, HY_W), jnp.float32),
        name="hy_conv",
    )(p, p, p, conv_w, conv_b.reshape(1, -1))


def _hy_filter_body(z_ref, w1_ref, b1_ref, w2_ref, b2_ref, w3_ref, b3_ref, w4_ref, fq_ref, dl_ref,
                    e_ref, o_ref, nrm_ref):
    i = pl.program_id(0)
    bf = jnp.bfloat16
    z = z_ref[...]
    fq = fq_ref[...]

    def layer(a, w_ref, b_ref):
        return jnp.sin(fq * (jnp.dot(a.astype(bf), w_ref[...].astype(bf), preferred_element_type=jnp.float32)
                             + b_ref[...]))

    hdn = layer(layer(layer(z, w1_ref, b1_ref), w2_ref, b2_ref), w3_ref, b3_ref)
    h = jnp.dot(hdn.astype(bf), w4_ref[...].astype(bf), preferred_element_type=jnp.float32)
    decay = jnp.exp(-z[:, 0:1] * dl_ref[...])
    first = (lax.broadcasted_iota(jnp.int32, decay.shape, 0) == 0) & (i == 0)
    acc = []
    for order in range(HY_ORDER):
        lo = order * 2 * HY_W
        h0 = h[:, lo:lo + HY_W] * decay
        h1 = jnp.where(first, 0.0, h[:, lo + HY_W:lo + 2 * HY_W] * decay)
        e_ref[:, order * HY_W:(order + 1) * HY_W] = h0 + h1
        o_ref[:, order * HY_W:(order + 1) * HY_W] = h0 - h1
        acc.append(jnp.sum(jnp.abs(h0) + jnp.abs(h1), axis=0, keepdims=True))
    part = jnp.concatenate(acc, axis=1)

    @pl.when(i == 0)
    def _():
        nrm_ref[...] = jnp.zeros_like(nrm_ref)

    nrm_ref[...] = nrm_ref[...] + part


def _hy_filter_time(length, w1, b1, w2, b2, w3, b3, w4, freq, tm):
    f32 = jnp.float32
    t = jnp.linspace(0.0, 1.0, length, dtype=f32)[:, None]
    bands = (HY_EMB - 1) // 2
    w_ang = 2.0 * math.pi * jnp.arange(length, dtype=f32)[:, None] / length
    fr = jnp.linspace(1e-4, bands - 1, bands, dtype=f32)[None]
    z = jnp.concatenate([t, jnp.cos(fr * w_ang), -jnp.sin(fr * w_ang)], axis=-1)
    z = jnp.pad(z, ((0, 0), (0, LANE - HY_EMB)))
    w1p = jnp.pad(w1.astype(f32), ((0, LANE - HY_EMB), (0, 0)))
    max_decay = math.log(HY_TARGET) / HY_SHORT_PCT
    min_decay = math.log(HY_TARGET) / HY_LONG_PCT
    deltas = jnp.abs(jnp.linspace(min_decay, max_decay, HY_W, dtype=f32)).reshape(1, HY_W)
    full = lambda a: pl.BlockSpec(a.shape, lambda i: (0,) * a.ndim)
    row = lambda v: v.astype(f32).reshape(1, -1)
    args = (z, w1p, row(b1), w2.astype(f32), row(b2), w3.astype(f32), row(b3), w4.astype(f32), row(freq), deltas)
    ow = HY_ORDER * HY_W
    return pl.pallas_call(
        _hy_filter_body,
        grid=(length // tm,),
        in_specs=[pl.BlockSpec((tm, LANE), lambda i: (i, 0))] + [full(a) for a in args[1:]],
        out_specs=[pl.BlockSpec((tm, ow), lambda i: (i, 0)), pl.BlockSpec((tm, ow), lambda i: (i, 0)),
                   pl.BlockSpec((1, ow), lambda i: (0, 0))],
        out_shape=[jax.ShapeDtypeStruct((length, ow), f32), jax.ShapeDtypeStruct((length, ow), f32),
                   jax.ShapeDtypeStruct((1, ow), f32)],
        compiler_params=pltpu.CompilerParams(dimension_semantics=("arbitrary",)),
        name="hy_filter_time",
    )(*args)


def _words(x_bf16):
    return pltpu.bitcast(x_bf16, jnp.uint32)


def _halves(w_u32):
    return pltpu.bitcast(w_u32, jnp.bfloat16)


def _transpose8(parts):
    rows, cols = parts[0].shape
    parts = [p.reshape(rows // SUBLANE, SUBLANE, cols) for p in parts]
    row = lax.broadcasted_iota(jnp.int32, parts[0].shape, 1)
    for s in (1, 2, 4):
        keep = (row & s) == 0
        nxt = list(parts)
        for i in range(SUBLANE):
            if i & s == 0:
                a, b = parts[i], parts[i + s]
                nxt[i] = jnp.where(keep, a, pltpu.roll(b, s, 1))
                nxt[i + s] = jnp.where(keep, pltpu.roll(a, SUBLANE - s, 1), b)
        parts = nxt
    return [p.reshape(rows, cols) for p in parts]


def _gather_tiles(ref, j, n):
    return jnp.concatenate([ref[SUBLANE * g + j] for g in range(n)], axis=0)


def _hy_s1_body(x_ref, f_ref, a_ref):
    groups = x_ref.shape[0] // SUBLANE
    xs = _transpose8([_gather_tiles(x_ref, j, groups) for j in range(SUBLANE)])
    words = []
    for i in range(HY_NB):
        acc = jnp.dot(f_ref[i], xs[i].astype(jnp.bfloat16), preferred_element_type=jnp.float32)
        words.append(_words(acc.astype(jnp.bfloat16)))
    tiles = _transpose8(words)
    per = HY_N1 // SUBLANE
    for j in range(SUBLANE):
        for g in range(per):
            a_ref[per * j + g] = tiles[j][SUBLANE * g:SUBLANE * (g + 1)]


def _hy_s1(x4, sel, fwd):
    assert HY_NB == SUBLANE
    _, half, _, width = x4.shape
    return pl.pallas_call(
        _hy_s1_body,
        grid=(HY_N2 // HY_NB,),
        in_specs=[pl.BlockSpec((None, half, HY_NB, width), lambda j: (sel, 0, j, 0)),
                  pl.BlockSpec((HY_NB, 2 * HY_N1, half), lambda j: (j, 0, 0))],
        out_specs=pl.BlockSpec((HY_N1, HY_NB, width), lambda j: (0, j, 0)),
        out_shape=jax.ShapeDtypeStruct((HY_N1, HY_N2, width), jnp.uint32),
        name="hy_stage1",
    )(x4, fwd)


def _hy_s2f_body(ae_ref, ao_ref, m_ref, sc_ref, hr_ref, hi_ref):
    dot = lambda a, b: jnp.dot(a, b, preferred_element_type=jnp.float32)
    hr_ref[...] = (dot(m_ref[:HY_N2, :], _halves(ae_ref[...])) * sc_ref[...]).astype(hr_ref.dtype)
    hi_ref[...] = (dot(m_ref[HY_N2:, :], _halves(ao_ref[...])) * sc_ref[...]).astype(hi_ref.dtype)


def _hy_s2f(a_e, a_o, m_fwd, scale):
    width = scale.shape[1]
    slab = pl.BlockSpec((None, HY_N2, width), lambda k: (k, 0, 0))
    out = jax.ShapeDtypeStruct((HY_N1, HY_N2, width), jnp.bfloat16)
    return pl.pallas_call(
        _hy_s2f_body,
        grid=(HY_N1,),
        in_specs=[slab, slab, pl.BlockSpec((2 * HY_N2, 2 * HY_N2), lambda k: (0, 0)),
                  pl.BlockSpec((1, width), lambda k: (0, 0))],
        out_specs=[slab, slab],
        out_shape=[out, out],
        name="hy_filter_stage2",
    )(a_e, a_o, m_fwd, scale)


def _hy_s2_body(a_ref, hr_ref, hi_ref, mf_ref, mi_ref, b_ref):
    bf = jnp.bfloat16
    x = jnp.dot(mf_ref[...], _halves(a_ref[...]), preferred_element_type=jnp.float32)
    xr, xi = x[:HY_N2], x[HY_N2:]
    hr = hr_ref[...].astype(jnp.float32)
    hi = hi_ref[...].astype(jnp.float32)
    y = jnp.concatenate([(xr * hr - xi * hi).astype(bf), (xr * hi + xi * hr).astype(bf)], axis=0)
    b = jnp.dot(mi_ref[...], y, preferred_element_type=jnp.float32)
    b_ref[...] = _words(b.astype(bf))


def _hy_s2(a, h_re, h_im, m_fwd, m_inv, order):
    slab = pl.BlockSpec((None, HY_N2, HY_W), lambda k: (k, 0, 0))
    hslab = pl.BlockSpec((None, HY_N2, HY_W), lambda k: (k, 0, order))
    mat = pl.BlockSpec((2 * HY_N2, 2 * HY_N2), lambda k: (0, 0))
    return pl.pallas_call(
        _hy_s2_body,
        grid=(HY_N1,),
        in_specs=[slab, hslab, hslab, mat, mat],
        out_specs=slab,
        out_shape=jax.ShapeDtypeStruct((HY_N1, HY_N2, HY_W), jnp.uint32),
        name="hy_stage2",
    )(a, h_re, h_im, m_fwd, m_inv)


def _hy_is1_body(b_ref, g_ref, z_ref, gate_ref, bias_ref, o_ref):
    per = HY_N1 // SUBLANE
    spectra = _transpose8([jnp.concatenate([b_ref[per * j + g] for g in range(per)], axis=0)
                           for j in range(SUBLANE)])
    convs = [jnp.dot(g_ref[i], _halves(spectra[i]), preferred_element_type=jnp.float32) for i in range(HY_NB)]
    tiles = _transpose8(convs)
    for j in range(SUBLANE):
        for g in range(o_ref.shape[0] // SUBLANE):
            n1 = SUBLANE * g + j
            o_ref[n1] = gate_ref[n1] * (tiles[j][SUBLANE * g:SUBLANE * (g + 1)] + z_ref[n1] * bias_ref[...])


def _hy_is1(b, inv, z4, z_sel, gate4, gate_sel, bias):
    _, half, _, width = z4.shape
    real = lambda sel: pl.BlockSpec((None, half, HY_NB, width), lambda j: (sel, 0, j, 0))
    return pl.pallas_call(
        _hy_is1_body,
        grid=(HY_N2 // HY_NB,),
        in_specs=[pl.BlockSpec((HY_N1, HY_NB, width), lambda j: (0, j, 0)),
                  pl.BlockSpec((HY_NB, half, 2 * HY_N1), lambda j: (j, 0, 0)),
                  real(z_sel), real(gate_sel), pl.BlockSpec((1, width), lambda j: (0, 0))],
        out_specs=pl.BlockSpec((half, HY_NB, width), lambda j: (0, j, 0)),
        out_shape=jax.ShapeDtypeStruct((half, HY_N2, width), jnp.float32),
        name="hy_inv_stage1",
    )(b, inv, z4, gate4, bias)


def hyena_long(p_hy, f_w1, f_b1, f_w2, f_b2, f_w3, f_b3, f_w4, f_freq, f_bias, conv_w, conv_b):
    length = p_hy.shape[0]
    bf = jnp.bfloat16
    half = HY_N1 // 2
    fwd, inv, m_fwd, m_inv = (jnp.asarray(m, bf) for m in _hy_dft_constants(length))
    e, od, nrm = _hy_filter_time(length, f_w1, f_b1, f_w2, f_b2, f_w3, f_b3, f_w4, f_freq, HY_TM)
    ow = HY_ORDER * HY_W
    a_e = _hy_s1(e.reshape(1, half, HY_N2, ow), 0, fwd)
    a_o = _hy_s1(od.reshape(1, half, HY_N2, ow), 0, fwd)
    scale = 1.0 / (nrm * (2.0 * length))
    h_re, h_im = _hy_s2f(a_e, a_o, m_fwd, scale)
    u = _hy_conv(p_hy, conv_w, conv_b, HY_TM).reshape(3, half, HY_N2, HY_W)
    z, z_sel = u, 0
    for order in range(HY_ORDER):
        b = _hy_s2(_hy_s1(z, z_sel, fwd), h_re, h_im, m_fwd, m_inv, order)
        z = _hy_is1(b, inv, z, z_sel, u, 1 + order, f_bias[order].astype(jnp.float32).reshape(1, HY_W))[None]
        z_sel = 0
    return z.reshape(length, HY_W)


def even_mixer(h, hc, w_in, conv_w, conv_b, f_w1, f_b1, f_w2, f_b2, f_w3, f_b3, f_w4, f_freq, f_bias,
               q_norm, k_norm, sink, ctx_out):
    f32 = jnp.float32
    bsz, length, _ = h.shape
    n_ctx = hc.shape[1]
    filt = (f_w1, f_b1, f_w2, f_b2, f_w3, f_b3, f_w4, f_freq)
    o_v = ATT_KV_W
    o_q = 2 * ATT_KV_W
    o_hy = o_q + ATT_Q_W
    w_perm = jnp.concatenate([w_in[:, o_hy:], w_in[:, o_q:o_hy], w_in[:, :o_q]], axis=1)
    hy_w = 3 * HY_W
    p_hy = pmm3(h, w_perm).astype(f32)
    p = jnp.concatenate([p_hy[..., hy_w + ATT_Q_W:], p_hy[..., hy_w:hy_w + ATT_Q_W]], axis=-1)
    pc = pmm3(hc, w_in if ctx_out else w_in[:, :o_q]).astype(f32)

    def heads(a, n_heads):
        return a.reshape(a.shape[0], a.shape[1], n_heads, HEAD_DIM)

    k_c = rms_norm(heads(pc[..., :o_v], ATT_KV_HEADS), k_norm)
    v_c = heads(pc[..., o_v:o_q], ATT_KV_HEADS)
    rope = axial_rope_tables(length)
    k = apply_axial_rope(rms_norm(heads(p[..., :o_v], ATT_KV_HEADS), k_norm), rope)
    v = heads(p[..., o_v:o_q], ATT_KV_HEADS)
    q = apply_axial_rope(rms_norm(heads(p[..., o_q:o_hy], ATT_HEADS), q_norm), rope)
    att = window_attention(q.reshape(bsz, length, ATT_KV_HEADS, ATT_GROUP, HEAD_DIM), k, v, k_c, v_c, sink)
    hy = hyena_long(p_hy[0], *filt, f_bias, conv_w, conv_b)[None]
    out = jnp.concatenate([hy, att], axis=-1).astype(h.dtype)
    if not ctx_out:
        return out, None
    q_c = rms_norm(heads(pc[..., o_q:o_hy], ATT_HEADS), q_norm).reshape(bsz, n_ctx, ATT_KV_HEADS, ATT_GROUP, HEAD_DIM)
    att_c = context_attention(q_c, k_c, v_c, sink)
    hy_c = hyena_mix(pc[..., o_hy:], hyena_filters(n_ctx, *filt), f_bias, conv_w, conv_b)
    return out, jnp.concatenate([hy_c, att_c], axis=-1).astype(hc.dtype)


ATT_TQ = ATT_BLOCK
EVEN_TM = 256
EV_Q_BLK = 3 * HY_W // ATT_Q_W
EV_K_BLK = (3 * HY_W + ATT_Q_W) // ATT_KV_W
EV_V_BLK = EV_K_BLK + 1


def _rope_tables(length):
    cr, sr, cc, sc = axial_rope_tables(length)
    return jnp.concatenate([cr, cr, cc, cc], axis=-1), jnp.concatenate([-sr, sr, -sc, sc], axis=-1)


def _head_norm_rope(x, g_row, c, s, seg):
    sq = x * x
    hi = sq.astype(jnp.bfloat16)
    lo = (sq - hi.astype(jnp.float32)).astype(jnp.bfloat16)
    ms = (jnp.dot(hi, seg, preferred_element_type=jnp.float32) + jnp.dot(lo, seg, preferred_element_type=jnp.float32))
    y = x * lax.rsqrt(ms + EPS) * g_row
    width = x.shape[1]
    quarter = HEAD_DIM // 4
    lane = lax.broadcasted_iota(jnp.int32, x.shape, 1)
    partner = jnp.where((lane & quarter) == 0, pltpu.roll(y, width - quarter, 1), pltpu.roll(y, quarter, 1))
    return y * c + partner * s


def _qk_prep_body(q_ref, k_ref, c_ref, s_ref, qn_ref, kn_ref, segq_ref, segk_ref, qo_ref, ko_ref):
    c, s = c_ref[...], s_ref[...]
    tile = lambda t, n: jnp.concatenate([t] * n, axis=1)
    qo_ref[...] = _head_norm_rope(q_ref[...], tile(qn_ref[...], ATT_HEADS), tile(c, ATT_HEADS), tile(s, ATT_HEADS),
                                  segq_ref[...]).astype(qo_ref.dtype)
    ko_ref[...] = _head_norm_rope(k_ref[...], tile(kn_ref[...], ATT_KV_HEADS), tile(c, ATT_KV_HEADS),
                                  tile(s, ATT_KV_HEADS), segk_ref[...]).astype(ko_ref.dtype)


def _qk_prep(p, rope_c, rope_s, q_norm, k_norm, tm):
    import numpy as np
    rows = p.shape[0]
    seg = lambda w: jnp.asarray(np.kron(np.eye(w // HEAD_DIM), np.full((HEAD_DIM, HEAD_DIM), 1.0 / HEAD_DIM)),
                                jnp.bfloat16)
    const = lambda shape: pl.BlockSpec(shape, lambda i: (0, 0))
    return pl.pallas_call(
        _qk_prep_body,
        grid=(rows // tm,),
        in_specs=[pl.BlockSpec((tm, ATT_Q_W), lambda i: (i, EV_Q_BLK)),
                  pl.BlockSpec((tm, ATT_KV_W), lambda i: (i, EV_K_BLK)),
                  pl.BlockSpec((tm, HEAD_DIM), lambda i: (i, 0)), pl.BlockSpec((tm, HEAD_DIM), lambda i: (i, 0)),
                  const((1, HEAD_DIM)), const((1, HEAD_DIM)),
                  const((ATT_Q_W, ATT_Q_W)), const((ATT_KV_W, ATT_KV_W))],
        out_specs=[pl.BlockSpec((tm, ATT_Q_W), lambda i: (i, 0)), pl.BlockSpec((tm, ATT_KV_W), lambda i: (i, 0))],
        out_shape=[jax.ShapeDtypeStruct((rows, ATT_Q_W), jnp.bfloat16),
                   jax.ShapeDtypeStruct((rows, ATT_KV_W), jnp.bfloat16)],
        name="qk_prep",
    )(p, p, rope_c, rope_s, q_norm.astype(jnp.float32).reshape(1, -1), k_norm.astype(jnp.float32).reshape(1, -1),
      seg(ATT_Q_W), seg(ATT_KV_W))


def _att_body(sink_ref, q_ref, kc_ref, vc_ref, *rest, local):
    if local:
        kp_ref, k0_ref, kn_ref, vp_ref, v0_ref, vn_ref, o_ref = rest
    else:
        (o_ref,) = rest
    b = pl.program_id(0)
    nb = pl.num_programs(0)
    bf = jnp.bfloat16
    scale = HEAD_DIM ** -0.5
    q = q_ref[...]
    kc = kc_ref[...]
    vc = vc_ref[...].astype(bf)
    if local:
        kb = jnp.concatenate([kp_ref[...], k0_ref[...], kn_ref[...]], axis=0)
        vb = jnp.concatenate([vp_ref[...], v0_ref[...], vn_ref[...]], axis=0).astype(bf)
        i = lax.broadcasted_iota(jnp.int32, (ATT_TQ, 3 * ATT_BLOCK), 0)
        j = lax.broadcasted_iota(jnp.int32, (ATT_TQ, 3 * ATT_BLOCK), 1)
        rel = j - ATT_BLOCK - i
        valid = ((jnp.abs(rel) <= ATT_WINDOW) & ((b > 0) | (j >= ATT_BLOCK))
                 & ((b < nb - 1) | (j < 2 * ATT_BLOCK)))
    for h in range(ATT_HEADS):
        hk = h // ATT_GROUP
        qh = q[:, h * HEAD_DIM:(h + 1) * HEAD_DIM]
        kv_cols = slice(hk * HEAD_DIM, (hk + 1) * HEAD_DIM)
        sink = sink_ref[h]
        s_ctx = _dot_nt(qh, kc[:, kv_cols]) * scale
        m = jnp.maximum(jnp.max(s_ctx, axis=-1, keepdims=True), sink)
        if local:
            s_loc = jnp.where(valid, _dot_nt(qh, kb[:, kv_cols]) * scale, -jnp.inf)
            m = jnp.maximum(m, jnp.max(s_loc, axis=-1, keepdims=True))
        p_ctx = jnp.exp(s_ctx - m)
        den = jnp.sum(p_ctx, axis=-1, keepdims=True) + jnp.exp(sink - m)
        acc = jnp.dot(p_ctx.astype(bf), vc[:, kv_cols], preferred_element_type=jnp.float32)
        if local:
            p_loc = jnp.exp(s_loc - m)
            den = den + jnp.sum(p_loc, axis=-1, keepdims=True)
            acc = acc + jnp.dot(p_loc.astype(bf), vb[:, kv_cols], preferred_element_type=jnp.float32)
        o_ref[:, h * HEAD_DIM:(h + 1) * HEAD_DIM] = acc / den


def _attention(qn, kn, p, kcn, pc, sink, local):
    rows = qn.shape[0]
    nb = rows // ATT_TQ
    n_ctx = kcn.shape[0]
    specs = [pl.BlockSpec((ATT_TQ, ATT_Q_W), lambda b, s: (b, 0)),
             pl.BlockSpec((n_ctx, ATT_KV_W), lambda b, s: (0, 0)),
             pl.BlockSpec((n_ctx, ATT_KV_W), lambda b, s: (0, EV_V_BLK))]
    args = [qn, kcn, pc]
    if local:
        prev = lambda b, s: jnp.maximum(b - 1, 0)
        nxt = lambda b, s: jnp.minimum(b + 1, nb - 1)
        for col, arr in ((0, kn), (EV_V_BLK, p)):
            specs += [pl.BlockSpec((ATT_BLOCK, ATT_KV_W), lambda b, s, col=col: (prev(b, s), col)),
                      pl.BlockSpec((ATT_BLOCK, ATT_KV_W), lambda b, s, col=col: (b, col)),
                      pl.BlockSpec((ATT_BLOCK, ATT_KV_W), lambda b, s, col=col: (nxt(b, s), col))]
            args += [arr, arr, arr]
    return pl.pallas_call(
        functools.partial(_att_body, local=local),
        grid_spec=pltpu.PrefetchScalarGridSpec(
            num_scalar_prefetch=1, grid=(nb,), in_specs=specs,
            out_specs=pl.BlockSpec((ATT_TQ, ATT_Q_W), lambda b, s: (b, 0))),
        out_shape=jax.ShapeDtypeStruct((rows, ATT_Q_W), jnp.float32),
        name="window_attention" if local else "context_attention",
    )(sink.astype(jnp.float32), *args)


def _even_merge_body(hy_ref, att_ref, x_ref, gt_ref, w_ref, o_ref):
    bf = jnp.bfloat16
    m = (jnp.dot(hy_ref[...].astype(bf), w_ref[:HY_W, :].astype(bf), preferred_element_type=jnp.float32)
         + jnp.dot(att_ref[...].astype(bf), w_ref[HY_W:, :].astype(bf), preferred_element_type=jnp.float32))
    o_ref[...] = x_ref[...] + gt_ref[...] * m


def _even_merge(hy, att, x2d, gt, w_out, tm):
    rows, d = x2d.shape
    half = pl.BlockSpec((tm, HY_W), lambda i: (i, 0))
    return pl.pallas_call(
        _even_merge_body,
        grid=(rows // tm,),
        in_specs=[half, half, pl.BlockSpec((tm, d), lambda i: (i, 0)), pl.BlockSpec((1, d), lambda i: (0, 0)),
                  pl.BlockSpec((MIX_W, d), lambda i: (0, 0))],
        out_specs=pl.BlockSpec((tm, d), lambda i: (i, 0)),
        out_shape=jax.ShapeDtypeStruct((rows, d), jnp.float32),
        name="even_merge",
    )(hy, att, x2d, gt.reshape(1, -1), w_out)


def _hy_short_body(e_ref, o_ref, sc_ref, u_ref, bias_ref, cf_ref, sf_ref, out_ref):
    bf = jnp.bfloat16
    cf, sf = cf_ref[...], sf_ref[...]
    dot = lambda a, b: jnp.dot(a, b.astype(bf), preferred_element_type=jnp.float32)
    z = u_ref[0]
    for order in range(HY_ORDER):
        cols = slice(order * HY_W, (order + 1) * HY_W)
        h_re = dot(cf, e_ref[:, cols]) * sc_ref[:, cols]
        h_im = dot(sf, o_ref[:, cols]) * sc_ref[:, cols]
        x_re, x_im = dot(cf, z), dot(sf, z)
        y_re = (x_re * h_re - x_im * h_im).astype(bf)
        y_im = (x_re * h_im + x_im * h_re).astype(bf)
        conv = _dot_tn(cf, y_re) + _dot_tn(sf, y_im)
        z = u_ref[1 + order] * (conv + z * bias_ref[order:order + 1, :])
    out_ref[...] = z


def hyena_short(pc, f_w1, f_b1, f_w2, f_b2, f_w3, f_b3, f_w4, f_freq, f_bias, conv_w, conv_b):
    import numpy as np
    rows = pc.shape[0]
    e, od, nrm = _hy_filter_time(rows, f_w1, f_b1, f_w2, f_b2, f_w3, f_b3, f_w4, f_freq, rows)
    u = _hy_conv(pc, conv_w, conv_b, rows)
    ang = 2 * np.pi * np.outer(np.arange(2 * rows), np.arange(rows)) / (2 * rows)
    cf, sf = jnp.asarray(np.cos(ang), jnp.bfloat16), jnp.asarray(-np.sin(ang), jnp.bfloat16)
    scale = 1.0 / (nrm * (2.0 * rows))
    return pl.pallas_call(
        _hy_short_body,
        out_shape=jax.ShapeDtypeStruct((rows, HY_W), jnp.float32),
        name="hy_short",
    )(e, od, scale, u, f_bias.astype(jnp.float32), cf, sf)


def even_layer(x, xc, g, mod, mod_c, w_in, conv_w, conv_b, f_w1, f_b1, f_w2, f_b2, f_w3, f_b3, f_w4, f_freq, f_bias,
               q_norm, k_norm, sink, gt, gt_c, w_out):
    f32 = jnp.float32
    length, n_ctx = x.shape[1], xc.shape[1]
    filt = (f_w1, f_b1, f_w2, f_b2, f_w3, f_b3, f_w4, f_freq, f_bias, conv_w, conv_b)
    o_q = 2 * ATT_KV_W
    o_hy = o_q + ATT_Q_W
    w_perm = jnp.concatenate([w_in[:, o_hy:], w_in[:, o_q:o_hy], w_in[:, :o_q]], axis=1).astype(jnp.bfloat16)
    twice = lambda v: jnp.stack([v.reshape(-1), v.reshape(-1)])
    p = norm_proj(x[0], g, twice(mod[0]), twice(mod[1]), w_perm)
    pc = norm_proj(xc[0], g, twice(mod_c[0]), twice(mod_c[1]), w_perm)
    rope_c, rope_s = _rope_tables(length)
    qn, kn = _qk_prep(p, rope_c, rope_s, q_norm, k_norm, EVEN_TM)
    qcn, kcn = _qk_prep(pc, jnp.ones((n_ctx, HEAD_DIM), f32), jnp.zeros((n_ctx, HEAD_DIM), f32), q_norm, k_norm,
                        n_ctx)
    att = _attention(qn, kn, p, kcn, pc, sink, True)
    att_c = _attention(qcn, None, None, kcn, pc, sink, False)
    hy = hyena_long(p, *filt)
    hy_c = hyena_short(pc, *filt)
    x_new = _even_merge(hy, att, x[0], gt, w_out, EVEN_TM)
    xc_new = _even_merge(hy_c, att_c, xc[0], gt_c, w_out, n_ctx)
    return x_new[None], xc_new[None]


def odd_mixer(h, hc, lb, w_in, conv_w, conv_b, dt_bias, a_log, d_skip, ssd_norm, hg_norm, ctx_out):
    f32 = jnp.float32
    bsz, length, _ = h.shape
    n_ctx = hc.shape[1]
    p = pmm3(h, w_in).astype(f32)
    pc = pmm3(hc, w_in if ctx_out else w_in[:, :ODD_STATE_COLS]).astype(f32)
    o_dt = SSD_CONV_CH
    o_f = SSD_CONV_CH + 2 * SSD_HEADS
    o_i = o_f + 2 * HG_W
    o_z = ODD_STATE_COLS
    o_q = o_z + SSD_W
    o_g = o_q + HG_W
    gn = SSD_GROUPS * SSD_STATE

    def streams(pp):
        n = pp.shape[1]
        xbc = jax.nn.silu(dwconv_centred(pp[..., :SSD_CONV_CH], conv_w, conv_b))
        xs = xbc[..., :SSD_W].reshape(bsz, n, SSD_HEADS, SSD_HEAD_DIM)
        bm = xbc[..., SSD_W:SSD_W + gn].reshape(bsz, n, SSD_GROUPS, SSD_STATE)
        cm = xbc[..., SSD_W + gn:].reshape(bsz, n, SSD_GROUPS, SSD_STATE)
        dt_raw = pp[..., o_dt:o_f].reshape(bsz, n, 2, SSD_HEADS)
        f_raw = pp[..., o_f:o_i].reshape(bsz, n, 2, HG_HEADS, HG_EXPAND)
        iv = pp[..., o_i:o_i + HG_W].reshape(bsz, n, HG_HEADS, HG_VDIM)
        return xs, bm, cm, dt_raw, f_raw, iv

    xs, bm, cm, dt_raw, f_raw, iv = streams(p)
    xs_c, bm_c, cm_c, dt_raw_c, f_raw_c, iv_c = streams(pc)
    q = jax.nn.silu(p[..., o_q:o_g]).reshape(bsz, length, HG_HEADS, HG_EXPAND)
    q_c = jax.nn.silu(pc[..., o_q:o_g]).reshape(bsz, n_ctx, HG_HEADS, HG_EXPAND) if ctx_out else None
    lb = lb.astype(f32).reshape(HG_HEADS, HG_EXPAND)
    ssd0 = jnp.zeros((bsz, SSD_GROUPS, SSD_HEADS // SSD_GROUPS, SSD_HEAD_DIM, SSD_STATE), f32)
    hg0 = jnp.zeros((bsz, HG_HEADS, HG_EXPAND, HG_VDIM), f32)
    y_dirs, o_dirs, yc_dirs, oc_dirs = [], [], [], []
    for d in range(2):
        fl = (lambda a: jnp.flip(a, axis=1)) if d == 1 else (lambda a: a)
        a = -jnp.exp(a_log[d].astype(f32))
        dsk = d_skip[d].astype(f32)
        dtb = dt_bias[d].astype(f32)
        dt_l = jax.nn.softplus(dt_raw[:, :, d] + dtb)
        dt_c = jax.nn.softplus(dt_raw_c[:, :, d] + dtb)
        yc, s_ctx = ssd_scan(fl(xs_c), fl(dt_c), a, fl(bm_c), fl(cm_c), dsk, ssd0, ctx_out)
        yl, _ = ssd_scan(fl(xs), fl(dt_l), a, fl(bm), fl(cm), dsk, s_ctx, True)
        y_dirs.append(fl(yl))
        f_l = lb + (1.0 - lb) * jax.nn.sigmoid(f_raw[:, :, d])
        f_c = lb + (1.0 - lb) * jax.nn.sigmoid(f_raw_c[:, :, d])
        oc, s_hg = hgrn2_scan(fl(q_c) if ctx_out else None, fl(1.0 - f_c), fl(iv_c), fl(jnp.log(f_c)), hg0, ctx_out)
        ol, _ = hgrn2_scan(fl(q), fl(1.0 - f_l), fl(iv), fl(jnp.log(f_l)), s_hg, True)
        o_dirs.append(fl(ol))
        if ctx_out:
            yc_dirs.append(fl(yc))
            oc_dirs.append(fl(oc))

    def merge(yy, oo, pp, n):
        z = pp[..., o_z:o_q]
        g = pp[..., o_g:]
        ys = (yy.reshape(bsz, n, SSD_W) * jax.nn.silu(z)).reshape(bsz, n, SSD_GROUPS, SSD_W // SSD_GROUPS)
        ys = rms_norm(ys, ssd_norm.reshape(SSD_GROUPS, SSD_W // SSD_GROUPS)).reshape(bsz, n, SSD_W)
        hs = rms_norm(oo, hg_norm.reshape(HG_HEADS, HG_VDIM)).reshape(bsz, n, HG_W) * jax.nn.silu(g)
        return jnp.concatenate([ys, hs], axis=-1)

    out = merge(y_dirs[0] + y_dirs[1], o_dirs[0] + o_dirs[1], p, length).astype(h.dtype)
    if not ctx_out:
        return out, None
    out_c = merge(yc_dirs[0] + yc_dirs[1], oc_dirs[0] + oc_dirs[1], pc, n_ctx).astype(hc.dtype)
    return out, out_c


SCAN_Q = 128
SCAN_LEVELS = 7
ODD_COLS = SSD_CONV_CH + 8 * 512
HALO = SUBLANE


def _scan_constants():
    import numpy as np
    q = SCAN_Q
    d_hg, d_ssd, pairs, laters = [], [], [], []
    for direction in (0, 1):
        pos = np.arange(q) if direction == 0 else q - 1 - np.arange(q)
        pj, pt = pos[None, :], pos[:, None]
        top = pj <= pt
        end = pj > pt
        ones = np.ones((SUBLANE, q), bool)
        lv, pr, lt = [], [], []
        for level in range(SCAN_LEVELS):
            b = 2 ** level
            start = (pos // (2 * b)) * (2 * b)
            mid = (start + b)[:, None]
            later = pos >= start + b
            lv.append(np.where(later[:, None], (pj >= mid) & (pj <= pt), (pj > pt) & (pj < mid)))
            pr.append((start[:, None] == start[None, :]) & later[:, None] & ~later[None, :])
            lt.append(np.broadcast_to(later[:, None], (q, LANE)))
        pr.append(np.eye(q, dtype=bool))
        pr.append(top)
        d_hg.append(np.concatenate([top, end] + lv + [ones], axis=0))
        d_ssd.append(np.concatenate([top, end, ones], axis=0))
        pairs.append(np.stack(pr))
        laters.append(np.stack(lt))
    f = np.float32
    twice = lambda m: np.concatenate([m, m], axis=-1)
    return (twice(np.stack(d_hg)).astype(f), twice(np.stack(d_ssd)).astype(f), np.stack(pairs).astype(f),
            np.stack(laters).astype(f))


def _split_dot(mm_bf16, v):
    hi = v.astype(jnp.bfloat16)
    lo = (v - hi.astype(jnp.float32)).astype(jnp.bfloat16)
    return jnp.dot(mm_bf16, jnp.concatenate([hi, lo], axis=0), preferred_element_type=jnp.float32)


def _dot_nt(a, b):
    return lax.dot_general(a, b, (((1,), (1,)), ((), ())), preferred_element_type=jnp.float32)


def _dot_tn(a, b):
    return lax.dot_general(a, b, (((0,), (0,)), ((), ())), preferred_element_type=jnp.float32)


def _softplus(x):
    return jnp.maximum(x, 0.0) + jnp.log1p(jnp.exp(-jnp.abs(x)))


def _scan_body(xbc_ref, prev_ref, next_ref, f_ref, iv_ref, q_ref, dt_ref,
               cw_ref, cb_ref, dtb_ref, a_ref, dsk_ref, lb_ref,
               dhg_ref, dssd_ref, pair_ref, later_ref, sel_ref,
               out_ref, s_ssd, s_hg, ydiag_ref, *, n_ctx_chunks, n_chunks):
    d = pl.program_id(0)
    j = pl.program_id(1)
    q_rows = SCAN_Q
    bf = jnp.bfloat16

    @pl.when(j == 0)
    def _():
        s_ssd[...] = jnp.zeros_like(s_ssd)
        s_hg[...] = jnp.zeros_like(s_hg)

    c = jnp.where(d == 0, j, jnp.where(j < n_ctx_chunks, n_ctx_chunks - 1 - j, n_chunks - 1 + n_ctx_chunks - j))
    first = (c == 0) | (c == n_ctx_chunks)
    last = (c == n_ctx_chunks - 1) | (c == n_chunks - 1)

    u = xbc_ref[...]
    row = lax.broadcasted_iota(jnp.int32, u.shape, 0)
    before = jnp.where(first, 0.0, prev_ref[HALO - 1:HALO, :])
    after = jnp.where(last, 0.0, next_ref[0:1, :])
    up = jnp.where(row == 0, before, pltpu.roll(u, 1, 0))
    un = jnp.where(row == q_rows - 1, after, pltpu.roll(u, q_rows - 1, 0))
    xbc = cw_ref[0:1, :] * up + cw_ref[1:2, :] * u + cw_ref[2:3, :] * un + cb_ref[...]
    xbc = xbc * jax.nn.sigmoid(xbc)
    x = xbc[:, :SSD_W]

    dt = _softplus(dt_ref[...] + dtb_ref[...])
    da = dt * a_ref[...]
    r = _split_dot(dssd_ref[...], da)
    cs, to_end, total = r[:q_rows], r[q_rows:2 * q_rows], r[2 * q_rows:2 * q_rows + 1]
    xdt = x * dt
    cs_hi = cs.astype(bf)
    cs_lo = (cs - cs_hi.astype(jnp.float32)).astype(bf)
    cs_rows = _dot_nt(sel_ref[...], cs_hi) + _dot_nt(sel_ref[...], cs_lo)
    l_mask = pair_ref[SCAN_LEVELS + 1]
    decay_in = jnp.exp2(cs)
    w_end = (jnp.exp2(to_end) * xdt).astype(bf)
    gn = SSD_GROUPS * SSD_STATE
    hpg = SSD_HEADS // SSD_GROUPS
    gw = hpg * SSD_HEAD_DIM
    for g in range(SSD_GROUPS):
        b_g = xbc[:, SSD_W + g * SSD_STATE:SSD_W + (g + 1) * SSD_STATE].astype(bf)
        c_g = xbc[:, SSD_W + gn + g * SSD_STATE:SSD_W + gn + (g + 1) * SSD_STATE].astype(bf)
        scores = _dot_nt(c_g, b_g)
        y_off = jnp.dot(c_g, s_ssd[g].astype(bf), preferred_element_type=jnp.float32)
        for hh in range(hpg):
            h = g * hpg + hh
            lo = h * SSD_HEAD_DIM
            diff = cs[:, lo:lo + 1] - cs_rows[h:h + 1, :]
            decay = jnp.exp2(jnp.minimum(diff, 0.0)) * l_mask
            ydiag_ref[:, lo:lo + SSD_HEAD_DIM] = jnp.dot((scores * decay).astype(bf),
                                                         xdt[:, lo:lo + SSD_HEAD_DIM].astype(bf),
                                                         preferred_element_type=jnp.float32)
        cols = slice(g * gw, (g + 1) * gw)
        out_ref[:, cols] = (ydiag_ref[:, cols] + decay_in[:, cols] * y_off + dsk_ref[:, cols] * x[:, cols])
        s_ssd[g] = jnp.exp2(total[:, cols]) * s_ssd[g] + _dot_tn(b_g, w_end[:, cols])

    lb = lb_ref[...]
    f = lb + (1.0 - lb) * jax.nn.sigmoid(f_ref[...])
    k_in = 1.0 - f
    qv = q_ref[...]
    qv = qv * jax.nn.sigmoid(qv)
    v_bf = iv_ref[...].astype(bf)
    e = jnp.exp2(_split_dot(dhg_ref[...], jnp.log2(f)))
    e_top, e_end = e[:q_rows], e[q_rows:2 * q_rows]
    e_tot = e[(2 + SCAN_LEVELS) * q_rows:(2 + SCAN_LEVELS) * q_rows + 1]
    for h in range(HG_HEADS):
        cols = slice(h * HG_EXPAND, (h + 1) * HG_EXPAND)
        q_h, k_h = qv[:, cols], k_in[:, cols]
        att = pair_ref[SCAN_LEVELS] * _dot_nt(q_h.astype(bf), k_h.astype(bf))
        for level in range(SCAN_LEVELS):
            e_l = e[(2 + level) * q_rows:(3 + level) * q_rows, cols]
            w_l = (jnp.where(later_ref[level] > 0.0, q_h, k_h) * e_l).astype(bf)
            att = att + pair_ref[level] * _dot_nt(w_l, w_l)
        o = jnp.dot(att.astype(bf), v_bf[:, cols], preferred_element_type=jnp.float32)
        o = o + _dot_nt((q_h * e_top[:, cols]).astype(bf), s_hg[h].astype(bf))
        out_ref[:, SSD_W + h * HG_VDIM:SSD_W + (h + 1) * HG_VDIM] = o
        s_hg[h] = e_tot[:, cols] * s_hg[h] + _dot_tn(v_bf[:, cols], (k_h * e_end[:, cols]).astype(bf))


def _odd_scan(p_all, conv_w, conv_b, dtb, a_cols, dsk, lb, n_ctx):
    n_rows = p_all.shape[0]
    n_chunks = n_rows // SCAN_Q
    ncc = n_ctx // SCAN_Q
    d_hg, d_ssd, pairs, laters = _scan_constants()
    bf = jnp.bfloat16
    import numpy as np
    sel = np.zeros((LANE, SSD_W), np.float32)
    sel[np.arange(SSD_HEADS), np.arange(SSD_HEADS) * SSD_HEAD_DIM] = 1.0

    def chunk(d, j):
        return jnp.where(d == 0, j, jnp.where(j < ncc, ncc - 1 - j, n_chunks - 1 + ncc - j))

    per = SCAN_Q // HALO
    last_halo = n_rows // HALO - 1
    col512 = lambda blk: (lambda d, j: (chunk(d, j), blk))
    const2 = lambda shape: pl.BlockSpec(shape, lambda d, j: (0,) * len(shape))
    dirc = lambda shape: pl.BlockSpec((None,) + shape, lambda d, j: (d,) + (0,) * len(shape))
    body = functools.partial(_scan_body, n_ctx_chunks=ncc, n_chunks=n_chunks)
    return pl.pallas_call(
        body,
        grid=(2, n_chunks),
        in_specs=[
            pl.BlockSpec((SCAN_Q, SSD_CONV_CH), lambda d, j: (chunk(d, j), 0)),
            pl.BlockSpec((HALO, SSD_CONV_CH), lambda d, j: (jnp.maximum(chunk(d, j) * per - 1, 0), 0)),
            pl.BlockSpec((HALO, SSD_CONV_CH), lambda d, j: (jnp.minimum((chunk(d, j) + 1) * per, last_halo), 0)),
            pl.BlockSpec((SCAN_Q, 512), lambda d, j: (chunk(d, j), 2 + d)),
            pl.BlockSpec((SCAN_Q, 512), col512(4)),
            pl.BlockSpec((SCAN_Q, 512), col512(6)),
            pl.BlockSpec((SCAN_Q, 512), lambda d, j: (chunk(d, j), 8 + d)),
            const2((CONV_W, SSD_CONV_CH)), const2((1, SSD_CONV_CH)),
            dirc((1, SSD_W)), dirc((1, SSD_W)), dirc((1, SSD_W)), const2((1, HG_W)),
            dirc(d_hg.shape[1:]), dirc(d_ssd.shape[1:]), dirc(pairs.shape[1:]), dirc(laters.shape[1:]),
            const2((LANE, SSD_W)),
        ],
        out_specs=pl.BlockSpec((None, SCAN_Q, MIX_W), lambda d, j: (d, chunk(d, j), 0)),
        out_shape=jax.ShapeDtypeStruct((2, n_rows, MIX_W), jnp.float32),
        scratch_shapes=[pltpu.VMEM((SSD_GROUPS, SSD_STATE, SSD_W // SSD_GROUPS), jnp.float32),
                        pltpu.VMEM((HG_HEADS, HG_VDIM, HG_EXPAND), jnp.float32),
                        pltpu.VMEM((SCAN_Q, SSD_W), jnp.float32)],
        compiler_params=pltpu.CompilerParams(dimension_semantics=("arbitrary", "arbitrary"),
                                             vmem_limit_bytes=MOE_VMEM_LIMIT),
        name="odd_scan",
    )(p_all, p_all, p_all, p_all, p_all, p_all, p_all,
      conv_w, conv_b.reshape(1, -1), dtb, a_cols, dsk, lb.reshape(1, -1),
      jnp.asarray(d_hg, bf), jnp.asarray(d_ssd, bf), jnp.asarray(pairs), jnp.asarray(laters),
      jnp.asarray(sel, bf))


def _group_rms(v, width):
    parts = []
    for lo in range(0, v.shape[1], width):
        seg = v[:, lo:lo + width]
        parts.append(seg * lax.rsqrt(jnp.mean(seg * seg, axis=-1, keepdims=True) + EPS))
    return jnp.concatenate(parts, axis=1)


def _odd_merge_body(yo_ref, z_ref, g_ref, x_ref, sn_ref, hn_ref, gt_ref, w_ref, o_ref):
    yo = yo_ref[0] + yo_ref[1]
    z = z_ref[...]
    g = g_ref[...]
    ys = _group_rms(yo[:, :SSD_W] * (z * jax.nn.sigmoid(z)), SSD_W // SSD_GROUPS) * sn_ref[...]
    hs = _group_rms(yo[:, SSD_W:], HG_VDIM) * hn_ref[...] * (g * jax.nn.sigmoid(g))
    m = jnp.concatenate([ys, hs], axis=1).astype(jnp.bfloat16)
    o_ref[...] = x_ref[...] + gt_ref[...] * jnp.dot(m, w_ref[...].astype(jnp.bfloat16),
                                                    preferred_element_type=jnp.float32)


ODD_TM = 256


def _odd_merge(yo, p_all, x_all, ssd_norm, hg_norm, gt, w_out, n_ctx):
    n_all, d = x_all.shape
    n_lat = n_all - n_ctx
    skip = n_ctx // ODD_TM
    return pl.pallas_call(
        _odd_merge_body,
        grid=(n_lat // ODD_TM,),
        in_specs=[pl.BlockSpec((2, ODD_TM, MIX_W), lambda i: (0, i + skip, 0)),
                  pl.BlockSpec((ODD_TM, 512), lambda i: (i + skip, 5)),
                  pl.BlockSpec((ODD_TM, 512), lambda i: (i + skip, 7)),
                  pl.BlockSpec((ODD_TM, d), lambda i: (i + skip, 0)),
                  pl.BlockSpec((1, SSD_W), lambda i: (0, 0)),
                  pl.BlockSpec((1, HG_W), lambda i: (0, 0)),
                  pl.BlockSpec((1, d), lambda i: (0, 0)),
                  pl.BlockSpec((MIX_W, d), lambda i: (0, 0))],
        out_specs=pl.BlockSpec((ODD_TM, d), lambda i: (i, 0)),
        out_shape=jax.ShapeDtypeStruct((n_lat, d), jnp.float32),
        compiler_params=pltpu.CompilerParams(dimension_semantics=("arbitrary",)),
        name="odd_merge",
    )(yo, p_all, p_all, x_all, ssd_norm.reshape(1, -1), hg_norm.reshape(1, -1), gt.reshape(1, -1), w_out)


def odd_layer(x_all, n_ctx, g, mod, mod_c, lb, w_in, conv_w, conv_b, dt_bias, a_log, d_skip, ssd_norm, hg_norm, gt,
              w_out):
    f32 = jnp.float32
    o_dt = SSD_CONV_CH
    o_f = o_dt + 2 * SSD_HEADS
    rep = lambda v: jnp.repeat(v, SSD_HEAD_DIM, axis=-1)
    w_perm = jnp.concatenate([w_in[:, :o_dt], w_in[:, o_f:], rep(w_in[:, o_dt:o_dt + SSD_HEADS]),
                              rep(w_in[:, o_dt + SSD_HEADS:o_f])], axis=1).astype(jnp.bfloat16)
    both = lambda a, b: jnp.stack([a.reshape(-1), b.reshape(-1)])
    p_all = norm_proj(x_all, g, both(mod_c[0], mod[0]), both(mod_c[1], mod[1]), w_perm, n_first=n_ctx)
    dtb = rep(dt_bias.astype(f32)).reshape(2, 1, SSD_W)
    a_cols = rep(-jnp.exp(a_log.astype(f32)) * math.log2(math.e)).reshape(2, 1, SSD_W)
    dsk = rep(d_skip.astype(f32)).reshape(2, 1, SSD_W)
    yo = _odd_scan(p_all, conv_w, conv_b, dtb, a_cols, dsk, lb.astype(f32), n_ctx)
    return _odd_merge(yo, p_all, x_all, ssd_norm, hg_norm, gt, w_out, n_ctx)


MOE_TM = 256
MOE_BM = 256
NEG_BIG = -1e30
MOE_ISSUE_UNROLL = 4
MOE_VMEM_LIMIT = 52 * 1024 * 1024


def _route_body(x_ref, g_ref, sh_ref, sc_ref, rw_ref, rb_ref, cnt0_ref,
                h_ref, idx_ref, gate_ref, rank_ref, cnt_ref, run_ref):
    i = pl.program_id(0)

    @pl.when(i == 0)
    def _():
        run_ref[...] = cnt0_ref[...]

    tm = x_ref.shape[0]
    x = x_ref[...]
    t = (x * lax.rsqrt(jnp.mean(x * x, axis=-1, keepdims=True) + EPS) * g_ref[...]) * (1.0 + sc_ref[...]) + sh_ref[...]
    h_ref[...] = t
    logits = jnp.dot(t.astype(jnp.bfloat16), rw_ref[...].astype(jnp.bfloat16),
                     preferred_element_type=jnp.float32) + rb_ref[...]
    lane = lax.broadcasted_iota(jnp.int32, (tm, LANE), 1)
    lane_f = lane.astype(jnp.float32)
    work = logits
    vals, sels, hots = [], [], []
    for _ in range(TOP_K):
        m = jnp.max(work, axis=-1, keepdims=True)
        sel = jnp.min(jnp.where(work == m, lane_f, float(LANE)), axis=-1, keepdims=True)
        hot = lane_f == sel
        vals.append(m)
        sels.append(sel.astype(jnp.int32))
        hots.append(hot)
        work = jnp.where(hot, -jnp.inf, work)
    exps = [jnp.exp(v - vals[0]) for v in vals]
    denom = exps[0] + exps[1] + exps[2] + exps[3]
    chosen = jnp.zeros((tm, LANE), jnp.float32)
    for hot in hots:
        chosen = chosen + hot.astype(jnp.float32)
    row = lax.broadcasted_iota(jnp.int32, (tm, tm), 0)
    col = lax.broadcasted_iota(jnp.int32, (tm, tm), 1)
    tri = (row > col).astype(jnp.bfloat16)
    before = jnp.dot(tri, chosen.astype(jnp.bfloat16), preferred_element_type=jnp.float32) + run_ref[0:1, :]
    idx_out = jnp.zeros((tm, LANE), jnp.int32)
    gate_out = jnp.zeros((tm, LANE), jnp.float32)
    rank_out = jnp.zeros((tm, LANE), jnp.int32)
    for k in range(TOP_K):
        rank_k = jnp.sum(jnp.where(hots[k], before, 0.0), axis=-1, keepdims=True).astype(jnp.int32)
        idx_out = jnp.where(lane == k, sels[k], idx_out)
        gate_out = jnp.where(lane == k, exps[k] / denom, gate_out)
        rank_out = jnp.where(lane == k, rank_k, rank_out)
    idx_ref[...] = idx_out
    gate_ref[...] = gate_out
    rank_ref[...] = rank_out
    run_new = run_ref[0:1, :] + jnp.sum(chosen, axis=0, keepdims=True)
    run_ref[...] = jnp.broadcast_to(run_new, run_ref.shape)
    cnt_ref[...] = jnp.broadcast_to(run_new, cnt_ref.shape)


def _moe_route(x2d, g, shift, scale, router_w, router_b, cnt0):
    n_tok, d = x2d.shape
    f32 = jnp.float32
    rw = jnp.pad(router_w, ((0, 0), (0, LANE - N_EXPERTS)))
    rb = jnp.pad(router_b.astype(f32), (0, LANE - N_EXPERTS), constant_values=NEG_BIG).reshape(1, LANE)
    tile = pl.BlockSpec((MOE_TM, LANE), lambda i: (i, 0))
    wide = pl.BlockSpec((MOE_TM, d), lambda i: (i, 0))
    vec = pl.BlockSpec((1, d), lambda i: (0, 0))
    small = pl.BlockSpec((SUBLANE, LANE), lambda i: (0, 0))
    row = lambda v: v.astype(f32).reshape(1, d)
    h, idx, gate, rank, cnt = pl.pallas_call(
        _route_body,
        grid=(n_tok // MOE_TM,),
        in_specs=[wide, vec, vec, vec, pl.BlockSpec((d, LANE), lambda i: (0, 0)),
                  pl.BlockSpec((1, LANE), lambda i: (0, 0)), small],
        out_specs=[wide, tile, tile, tile, small],
        out_shape=[jax.ShapeDtypeStruct((n_tok, d), f32),
                   jax.ShapeDtypeStruct((n_tok, LANE), jnp.int32),
                   jax.ShapeDtypeStruct((n_tok, LANE), f32),
                   jax.ShapeDtypeStruct((n_tok, LANE), jnp.int32),
                   jax.ShapeDtypeStruct((SUBLANE, LANE), f32)],
        scratch_shapes=[pltpu.VMEM((SUBLANE, LANE), f32)],
        compiler_params=pltpu.CompilerParams(dimension_semantics=("arbitrary",)),
        name="moe_route",
    )(x2d, row(g), row(shift), row(scale), rw, rb, cnt0)
    return h, idx[:, :TOP_K], gate, rank[:, :TOP_K], cnt


def _rows_to_tiles(x):
    return _transpose8([x[:, LANE * s:LANE * (s + 1)] for s in range(SUBLANE)])


def _store_tiles(ref, tiles):
    groups = ref.shape[0] // SUBLANE
    for j in range(SUBLANE):
        ref[pl.ds(j, groups, stride=SUBLANE)] = tiles[j].reshape(groups, SUBLANE, LANE)


def _load_rows(ref):
    groups = ref.shape[0] // SUBLANE
    tiles = [ref[pl.ds(j, groups, stride=SUBLANE)].reshape(groups * SUBLANE, LANE) for j in range(SUBLANE)]
    return jnp.concatenate(_transpose8(tiles), axis=1)


def _scatter_body(off_ref, pad_ref, dest_ref, *rest, tile_starts):
    t_refs = rest[:-4]
    xs_ref, rows_ref, zero_ref, sem = rest[-4:]
    i = pl.program_id(0)
    tm = t_refs[0].shape[0]

    @pl.when(i == 0)
    def _():
        zero_ref[...] = jnp.zeros_like(zero_ref)
        used = off_ref[N_EXPERTS - 1] + pad_ref[N_EXPERTS - 1]
        n_rows = xs_ref.shape[0]

        def zero_block(start):
            return pltpu.make_async_copy(zero_ref, xs_ref.at[pl.ds(start, MOE_BM)], sem)

        for e in range(N_EXPERTS):
            tail = n_rows - (e + 1) * MOE_BM

            @pl.when(pad_ref[e] > 0)
            def _():
                zero_block(off_ref[e] + pad_ref[e] - MOE_BM).start()

            @pl.when(tail >= used)
            def _():
                zero_block(tail).start()
        for e in range(N_EXPERTS):
            tail = n_rows - (e + 1) * MOE_BM

            @pl.when(pad_ref[e] > 0)
            def _():
                zero_block(0).wait()

            @pl.when(tail >= used)
            def _():
                zero_block(0).wait()

    for s, t_ref in enumerate(t_refs):
        @pl.when((i >= tile_starts[s]) & (i < tile_starts[s + 1]))
        def _():
            _store_tiles(rows_ref, _rows_to_tiles(t_ref[...]))

    def issue(t, carry):
        for k in range(TOP_K):
            pltpu.make_async_copy(rows_ref.at[t], xs_ref.at[dest_ref[TOP_K * t + k]], sem).start(priority=k % 2)
        return carry

    lax.fori_loop(0, tm, issue, 0, unroll=MOE_ISSUE_UNROLL)
    for _ in range(TOP_K):
        pltpu.make_async_copy(rows_ref, xs_ref.at[pl.ds(0, tm)], sem).wait()


def _stream_tiles(streams):
    starts = [0]
    for t in streams:
        starts.append(starts[-1] + t.shape[0] // MOE_TM)
    return tuple(starts)


def _stream_spec(width, starts, s, extra=0):
    lo, n = starts[s], starts[s + 1] - starts[s]
    return pl.BlockSpec((MOE_TM, width), lambda i, *_: (jnp.clip(i + extra - lo, 0, n - 1), 0))


def _moe_scatter(hs, dest_flat, off, padded, n_rows):
    d = hs[0].shape[1]
    assert d == SUBLANE * LANE
    starts = _stream_tiles(hs)
    in_specs = [pl.BlockSpec((TOP_K * MOE_TM,), lambda i, off, pad: (i,), memory_space=pltpu.SMEM)]
    in_specs += [_stream_spec(d, starts, s) for s in range(len(hs))]
    return pl.pallas_call(
        functools.partial(_scatter_body, tile_starts=starts),
        grid_spec=pltpu.PrefetchScalarGridSpec(
            num_scalar_prefetch=2,
            grid=(starts[-1],),
            in_specs=in_specs,
            out_specs=pl.BlockSpec(memory_space=pl.ANY),
            scratch_shapes=[pltpu.VMEM((MOE_TM, SUBLANE, LANE), jnp.float32),
                            pltpu.VMEM((MOE_BM, SUBLANE, LANE), jnp.float32), pltpu.SemaphoreType.DMA],
        ),
        out_shape=jax.ShapeDtypeStruct((n_rows, SUBLANE, LANE), jnp.float32),
        compiler_params=pltpu.CompilerParams(dimension_semantics=("arbitrary",)),
        name="moe_scatter",
    )(off, padded, dest_flat, *hs)


def _expert_body(blk_e_ref, n_act_ref, nxt_e_ref, x_ref, wgu_hbm, bgu_ref, wdn_hbm, bdn_ref, y_ref,
                 wgu_f32, wdn_f32, wgu_bf, wdn_bf, sems, slot_ref, *, layer):
    i = pl.program_id(0)

    def weight_copies(e, slot):
        return (pltpu.make_async_copy(wgu_hbm.at[layer, e], wgu_f32.at[slot], sems.at[slot, 0]),
                pltpu.make_async_copy(wdn_hbm.at[layer, e], wdn_f32.at[slot], sems.at[slot, 1]))

    @pl.when(i < n_act_ref[0])
    def _():
        e = blk_e_ref[i]
        prev = blk_e_ref[jnp.maximum(i - 1, 0)]

        @pl.when((i == 0) | (e != prev))
        def _():
            @pl.when(i == 0)
            def _():
                slot_ref[0] = 0
                for c in weight_copies(e, 0):
                    c.start()

            slot = slot_ref[0]
            for c in weight_copies(e, slot):
                c.wait()
            wgu_bf[...] = wgu_f32[slot].astype(jnp.bfloat16)
            wdn_bf[...] = wdn_f32[slot].astype(jnp.bfloat16)
            nxt = nxt_e_ref[e]

            @pl.when(nxt >= 0)
            def _():
                for c in weight_copies(nxt, 1 - slot):
                    c.start()

            slot_ref[0] = 1 - slot

        x = _load_rows(x_ref).astype(jnp.bfloat16)
        gu = jnp.dot(x, wgu_bf[...], preferred_element_type=jnp.float32) + bgu_ref[...]
        gate = jnp.minimum(gu[:, :D_EXPERT], SWIGLU_LIMIT)
        up = jnp.clip(gu[:, D_EXPERT:], -SWIGLU_LIMIT, SWIGLU_LIMIT)
        act = (up + 1.0) * gate * jax.nn.sigmoid(SWIGLU_ALPHA * gate)
        y = jnp.dot(act.astype(jnp.bfloat16), wdn_bf[...], preferred_element_type=jnp.float32) + bdn_ref[...]
        _store_tiles(y_ref, _rows_to_tiles(y))

    @pl.when(i >= n_act_ref[0])
    def _():
        y_ref[...] = jnp.zeros_like(y_ref)


def _moe_experts(xs, blk_e, n_act, nxt_e, w_gu, b_gu, w_dn, b_dn, layer):
    n_rows = xs.shape[0]
    d = SUBLANE * LANE
    n_blk = n_rows // MOE_BM

    def blk(i, be, na, nx):
        return jnp.minimum(i, na[0] - 1)

    return pl.pallas_call(
        functools.partial(_expert_body, layer=layer),
        grid_spec=pltpu.PrefetchScalarGridSpec(
            num_scalar_prefetch=3,
            grid=(n_blk,),
            in_specs=[pl.BlockSpec((MOE_BM, SUBLANE, LANE), lambda i, be, na, nx: (blk(i, be, na, nx), 0, 0)),
                      pl.BlockSpec(memory_space=pl.ANY),
                      pl.BlockSpec((None, 1, 2 * D_EXPERT), lambda i, be, na, nx: (be[blk(i, be, na, nx)], 0, 0)),
                      pl.BlockSpec(memory_space=pl.ANY),
                      pl.BlockSpec((None, 1, d), lambda i, be, na, nx: (be[blk(i, be, na, nx)], 0, 0))],
            out_specs=pl.BlockSpec((MOE_BM, SUBLANE, LANE), lambda i, be, na, nx: (i, 0, 0)),
            scratch_shapes=[pltpu.VMEM((2, d, 2 * D_EXPERT), jnp.float32),
                            pltpu.VMEM((2, D_EXPERT, d), jnp.float32),
                            pltpu.VMEM((d, 2 * D_EXPERT), jnp.bfloat16),
                            pltpu.VMEM((D_EXPERT, d), jnp.bfloat16),
                            pltpu.SemaphoreType.DMA((2, 2)),
                            pltpu.SMEM((1,), jnp.int32)],
        ),
        out_shape=jax.ShapeDtypeStruct((n_rows, SUBLANE, LANE), jnp.float32),
        compiler_params=pltpu.CompilerParams(dimension_semantics=("arbitrary",),
                                             vmem_limit_bytes=MOE_VMEM_LIMIT),
        name="moe_experts",
    )(blk_e, n_act, nxt_e, xs, w_gu, b_gu.reshape(N_EXPERTS, 1, -1), w_dn, b_dn.reshape(N_EXPERTS, 1, -1))


def _combine_body(dest_ref, dest_nxt_ref, *rest, tile_starts):
    n_streams = len(tile_starts) - 1
    ins, ys_ref = rest[:3 * n_streams], rest[3 * n_streams]
    y_refs = rest[3 * n_streams + 1:-2]
    if len(y_refs) == 1:
        y_refs = y_refs * n_streams
    buf_ref, sems = rest[-2:]
    i = pl.program_id(0)
    n = pl.num_programs(0)
    tm = y_refs[0].shape[0]

    def fetch(d_ref, slot):
        def issue(t, carry):
            for k in range(TOP_K):
                pltpu.make_async_copy(ys_ref.at[d_ref[TOP_K * t + k]], buf_ref.at[slot, k, t],
                                      sems.at[slot]).start(priority=k % 2)
            return carry
        lax.fori_loop(0, tm, issue, 0, unroll=MOE_ISSUE_UNROLL)

    @pl.when(i == 0)
    def _():
        fetch(dest_ref, 0)

    @pl.when(i + 1 < n)
    def _():
        fetch(dest_nxt_ref, (i + 1) % 2)

    slot = i % 2
    for k in range(TOP_K):
        pltpu.make_async_copy(ys_ref.at[pl.ds(0, tm)], buf_ref.at[slot, k], sems.at[slot]).wait()
    for s in range(n_streams):
        gate_ref, x_ref, gt_ref = ins[3 * s:3 * s + 3]

        @pl.when((i >= tile_starts[s]) & (i < tile_starts[s + 1]))
        def _():
            g = gate_ref[...]
            groups = tm // SUBLANE
            gk = [jnp.broadcast_to(g[:, k:k + 1], (tm, LANE)).reshape(groups, SUBLANE, LANE) for k in range(TOP_K)]
            accs = []
            for j in range(SUBLANE):
                acc = None
                for k in range(TOP_K):
                    splat = jnp.broadcast_to(gk[k][:, j:j + 1, :], (groups, SUBLANE, LANE))
                    term = splat * buf_ref.at[slot, k][pl.ds(j, groups, stride=SUBLANE)]
                    acc = term if acc is None else acc + term
                accs.append(acc.reshape(tm, LANE))
            y_refs[s][...] = x_ref[...] + gt_ref[...] * jnp.concatenate(_transpose8(accs), axis=1)


def _moe_combine(ys, dest_flat, gates, xs2d, gts, joint):
    d = xs2d[0].shape[1]
    starts = _stream_tiles(xs2d)
    n_tiles = starts[-1]
    in_specs = [pl.BlockSpec((TOP_K * MOE_TM,), lambda i: (i,), memory_space=pltpu.SMEM),
                pl.BlockSpec((TOP_K * MOE_TM,), lambda i: (jnp.minimum(i + 1, n_tiles - 1),),
                             memory_space=pltpu.SMEM)]
    args = [dest_flat, dest_flat]
    for s, (gate, x2d, gt) in enumerate(zip(gates, xs2d, gts)):
        in_specs += [_stream_spec(LANE, starts, s), _stream_spec(d, starts, s), pl.BlockSpec((1, d), lambda i: (0, 0))]
        args += [gate, x2d, gt.astype(jnp.float32).reshape(1, d)]
    in_specs.append(pl.BlockSpec(memory_space=pl.ANY))
    args.append(ys)
    if joint:
        out_specs = [pl.BlockSpec((MOE_TM, d), lambda i: (i, 0))]
        out_shape = [jax.ShapeDtypeStruct((n_tiles * MOE_TM, d), jnp.float32)]
    else:
        out_specs = [_stream_spec(d, starts, s) for s in range(len(xs2d))]
        out_shape = [jax.ShapeDtypeStruct(x2d.shape, jnp.float32) for x2d in xs2d]
    return pl.pallas_call(
        functools.partial(_combine_body, tile_starts=starts),
        grid=(n_tiles,),
        in_specs=in_specs,
        out_specs=out_specs,
        out_shape=out_shape,
        scratch_shapes=[pltpu.VMEM((2, TOP_K, MOE_TM, SUBLANE, LANE), jnp.float32),
                        pltpu.SemaphoreType.DMA((2,))],
        compiler_params=pltpu.CompilerParams(dimension_semantics=("arbitrary",)),
        name="moe_combine",
    )(*args)


def moe_layer(streams, g, router_w, router_b, w_gu, b_gu, w_dn, b_dn, layer, joint=False):
    i32 = jnp.int32
    routed = []
    cnt = jnp.zeros((SUBLANE, LANE), jnp.float32)
    for x2d, shift, scale, _ in streams:
        assert x2d.shape[0] % MOE_TM == 0
        h, idx, gate, rank, cnt = _moe_route(x2d, g, shift, scale, router_w, router_b, cnt)
        routed.append((h, idx, gate, rank))
    counts = cnt[0, :N_EXPERTS].astype(i32)
    padded = (counts + MOE_BM - 1) // MOE_BM * MOE_BM
    pad_end = jnp.cumsum(padded)
    off = (pad_end - padded).astype(i32)
    padded = padded.astype(i32)
    n_tok = sum(s[0].shape[0] for s in streams)
    n_blk = -(-(n_tok * TOP_K) // MOE_BM) + N_EXPERTS
    blk_e = jnp.minimum(jnp.sum(jnp.arange(n_blk)[:, None] * MOE_BM >= pad_end[None, :], axis=1),
                        N_EXPERTS - 1).astype(i32)
    n_act = (pad_end[-1:] // MOE_BM).astype(i32)
    experts = jnp.arange(N_EXPERTS, dtype=i32)
    later_with_rows = (padded > 0)[None, :] & (experts[None, :] > experts[:, None])
    nxt_e = jnp.min(jnp.where(later_with_rows, experts[None, :], N_EXPERTS), axis=1)
    nxt_e = jnp.where(nxt_e == N_EXPERTS, -1, nxt_e).astype(i32)
    dests = []
    for _, idx, _, rank in routed:
        dest = rank + jnp.sum(jnp.where(idx[..., None] == experts, off, 0), axis=-1)
        dests.append(dest.reshape(-1).astype(i32))
    dest_flat = jnp.concatenate(dests)
    xs = _moe_scatter([r[0] for r in routed], dest_flat, off, padded, n_blk * MOE_BM)
    ys = _moe_experts(xs, blk_e, n_act, nxt_e, w_gu, b_gu, w_dn, b_dn, layer)
    return _moe_combine(ys, dest_flat, [r[2] for r in routed], [s[0] for s in streams], [s[3] for s in streams],
                        joint)


def kernel(x, c, ctx, c_ctx, norm_g, ada_w, ada_b, w_out, w_in_even, hy_conv_w, hy_conv_b,
           hy_w1, hy_b1, hy_w2, hy_b2, hy_w3, hy_b3, hy_w4, hy_freq, hy_filter_bias,
           att_q_norm, att_k_norm, att_sink, w_in_odd, ssd_conv_w, ssd_conv_b, ssd_dt_bias,
           ssd_A_log, ssd_D, ssd_norm, hg_lower_bounds, hg_norm, router_w, router_b,
           moe_w_gu, moe_b_gu, moe_w_dn, moe_b_dn):
    lbs = jax.nn.softmax(hg_lower_bounds.astype(jnp.float32), axis=0)
    lbs = jnp.cumsum(lbs, axis=0) - lbs[0]
    xc = ctx
    n_ctx = ctx.shape[1]
    for layer in range(DEPTH):
        ctx_out = layer < DEPTH - 1
        i = layer // 2
        sh, sc, gt = adaln(c, ada_w[layer], ada_b[layer], 0)
        sh_c, sc_c, gt_c = adaln(c_ctx, ada_w[layer], ada_b[layer], 0)
        assert (layer % 2 == 0) == ctx_out
        moe = functools.partial(moe_layer, g=norm_g[layer, 1], router_w=router_w[layer], router_b=router_b[layer],
                                w_gu=moe_w_gu, b_gu=moe_b_gu[layer], w_dn=moe_w_dn, b_dn=moe_b_dn[layer], layer=layer)
        sh2, sc2, gt2 = adaln(c, ada_w[layer], ada_b[layer], 1)
        if layer % 2 == 0:
            x, xc = even_layer(x, xc, norm_g[layer, 0], (sh, sc), (sh_c, sc_c), w_in_even[i], hy_conv_w[i],
                               hy_conv_b[i], hy_w1[i], hy_b1[i], hy_w2[i], hy_b2[i], hy_w3[i], hy_b3[i], hy_w4[i],
                               hy_freq[i], hy_filter_bias[i], att_q_norm[i], att_k_norm[i], att_sink[i], gt, gt_c,
                               w_out[layer])
            sh2_c, sc2_c, gt2_c = adaln(c_ctx, ada_w[layer], ada_b[layer], 1)
            x_all = moe([(xc[0], sh2_c, sc2_c, gt2_c), (x[0], sh2, sc2, gt2)], joint=True)[0]
        else:
            x = odd_layer(x_all, n_ctx, norm_g[layer, 0], (sh, sc), (sh_c, sc_c), lbs[layer], w_in_odd[i],
                          ssd_conv_w[i], ssd_conv_b[i], ssd_dt_bias[i], ssd_A_log[i], ssd_D[i], ssd_norm[i],
                          hg_norm[i], gt, w_out[layer])
            x = moe([(x, sh2, sc2, gt2)])[0][None]
    return x
```

```python
import functools
import math

import jax
import jax.numpy as jnp
from jax import lax
from jax.experimental import pallas as pl
from jax.experimental.pallas import tpu as pltpu

D_MODEL = 1024
DEPTH = 2
GRID_W = 64
MIX_W = D_MODEL
EPS = 1e-6
CONV_W = 3

HY_W = MIX_W // 2
HY_ORDER = 2
HY_EMB = 33
HY_FFN = 64
HY_TARGET = 1e-2
HY_SHORT_PCT = 0.3
HY_LONG_PCT = 1.5

HEAD_DIM = 64
ATT_HEADS = (MIX_W // 2) // HEAD_DIM
ATT_KV_HEADS = 2
ATT_GROUP = ATT_HEADS // ATT_KV_HEADS
ATT_WINDOW = 128
ATT_BLOCK = 128
ROPE_BASE = 10000.0
ATT_Q_W = ATT_HEADS * HEAD_DIM
ATT_KV_W = ATT_KV_HEADS * HEAD_DIM

SSD_W = MIX_W // 2
SSD_HEAD_DIM = 64
SSD_HEADS = SSD_W // SSD_HEAD_DIM
SSD_GROUPS = 2
SSD_STATE = 128
SSD_CHUNK = 128
SSD_CONV_CH = SSD_W + 2 * SSD_GROUPS * SSD_STATE

HG_W = MIX_W // 2
HG_EXPAND = 128
HG_HEADS = HG_W // HG_EXPAND
HG_VDIM = HG_W // HG_HEADS
HG_CHUNK = 64

N_EXPERTS = 32
TOP_K = 4
D_EXPERT = D_MODEL
SWIGLU_ALPHA = 1.702
SWIGLU_LIMIT = 7.0
MOE_BLOCK = 128

EVEN_IN = 2 * ATT_KV_W + ATT_Q_W + 3 * HY_W
ODD_STATE_COLS = SSD_CONV_CH + 2 * SSD_HEADS + 3 * HG_W
ODD_IN = ODD_STATE_COLS + SSD_W + 2 * HG_W

LANE = 128
SUBLANE = 8


def _mm_body(a_ref, b_ref, o_ref):
    a = a_ref[...].astype(jnp.bfloat16)
    b = b_ref[...].astype(jnp.bfloat16)
    o_ref[...] = jnp.dot(a, b, preferred_element_type=jnp.float32)


def _pick_tile(n, candidates):
    for c in candidates:
        if n % c == 0:
            return c
    return n


def pmm(a, b):
    m, k = a.shape
    n = b.shape[1]
    n_pad = -(-n // LANE) * LANE
    if n_pad != n:
        b = jnp.pad(b, ((0, 0), (0, n_pad - n)))
    m_pad = -(-m // SUBLANE) * SUBLANE
    if m_pad != m:
        a = jnp.pad(a, ((0, m_pad - m), (0, 0)))
    tm = _pick_tile(m_pad, (512, 256, 128, 64, 32, 16, 8))
    tn = _pick_tile(n_pad, (512, 384, 256, 128))
    out = pl.pallas_call(
        _mm_body,
        grid=(m_pad // tm, n_pad // tn),
        in_specs=[pl.BlockSpec((tm, k), lambda i, j: (i, 0)),
                  pl.BlockSpec((k, tn), lambda i, j: (0, j))],
        out_specs=pl.BlockSpec((tm, tn), lambda i, j: (i, j)),
        out_shape=jax.ShapeDtypeStruct((m_pad, n_pad), jnp.float32),
        name="dense_mm",
    )(a, b)
    return out[:m, :n]


PROJ_TM = 256
PROJ_VMEM_LIMIT = 56 * 1024 * 1024


def _norm_proj_body(x_ref, g_ref, sh_ref, sc_ref, w_ref, o_ref):
    x = x_ref[...]
    y = x * lax.rsqrt(jnp.mean(x * x, axis=-1, keepdims=True) + EPS) * g_ref[...]
    h = y * (1.0 + sc_ref[...]) + sh_ref[...]
    o_ref[...] = jnp.dot(h.astype(jnp.bfloat16), w_ref[...], preferred_element_type=jnp.float32)


def norm_proj(x2d, g, shift, scale, w_bf16, n_first=0):
    rows, d = x2d.shape
    n = w_bf16.shape[1]
    assert rows % PROJ_TM == 0 and n_first % PROJ_TM == 0 and n % LANE == 0
    first_tiles = n_first // PROJ_TM
    mod = pl.BlockSpec((None, 1, d), lambda i: (jnp.where(i < first_tiles, 0, 1), 0, 0))
    return pl.pallas_call(
        _norm_proj_body,
        grid=(rows // PROJ_TM,),
        in_specs=[pl.BlockSpec((PROJ_TM, d), lambda i: (i, 0)),
                  pl.BlockSpec((1, d), lambda i: (0, 0)), mod, mod,
                  pl.BlockSpec((d, n), lambda i: (0, 0))],
        out_specs=pl.BlockSpec((PROJ_TM, n), lambda i: (i, 0)),
        out_shape=jax.ShapeDtypeStruct((rows, n), jnp.float32),
        compiler_params=pltpu.CompilerParams(dimension_semantics=("arbitrary",),
                                             vmem_limit_bytes=PROJ_VMEM_LIMIT),
        name="norm_proj",
    )(x2d, g.astype(jnp.float32).reshape(1, d), shift.astype(jnp.float32).reshape(2, 1, d),
      scale.astype(jnp.float32).reshape(2, 1, d), w_bf16)


def pmm3(a, b):
    lead = a.shape[:-1]
    return pmm(a.reshape(-1, a.shape[-1]), b).reshape(*lead, b.shape[-1])


def rms_norm(x, g):
    xf = x.astype(jnp.float32)
    y = xf * lax.rsqrt(jnp.mean(xf * xf, axis=-1, keepdims=True) + EPS)
    return (y * g.astype(jnp.float32)).astype(x.dtype)


def modulate(h, shift, scale):
    return h * (1.0 + scale) + shift


def adaln(cond, w, b, j):
    lo, hi = 3 * j * D_MODEL, 3 * (j + 1) * D_MODEL
    m = jax.nn.silu(cond) @ w[:, lo:hi] + b[lo:hi]
    return jnp.split(m, 3, axis=-1)


def dwconv_centred(u, w, b):
    ch = u.shape[-1]
    y = lax.conv_general_dilated(u, w[:, None, :].astype(u.dtype), window_strides=(1,),
                                 padding=[(CONV_W // 2, CONV_W // 2)],
                                 dimension_numbers=('NWC', 'WIO', 'NWC'), feature_group_count=ch)
    return y + b.astype(u.dtype)


def axial_rope_tables(length):
    rows = length // GRID_W
    n_pairs = HEAD_DIM // 4
    inv = ROPE_BASE ** (-jnp.arange(n_pairs, dtype=jnp.float32) / n_pairs)
    row_ang = jnp.arange(rows, dtype=jnp.float32)[:, None] * inv
    col_ang = jnp.arange(GRID_W, dtype=jnp.float32)[:, None] * inv
    ang_r = jnp.broadcast_to(row_ang[:, None], (rows, GRID_W, n_pairs)).reshape(length, n_pairs)
    ang_c = jnp.broadcast_to(col_ang[None], (rows, GRID_W, n_pairs)).reshape(length, n_pairs)
    return jnp.cos(ang_r), jnp.sin(ang_r), jnp.cos(ang_c), jnp.sin(ang_c)


def _rotate(u, cos, sin):
    n = u.shape[-1] // 2
    u1, u2 = u[..., :n], u[..., n:]
    cos = cos[None, :, None, :]
    sin = sin[None, :, None, :]
    return jnp.concatenate([u1 * cos - u2 * sin, u1 * sin + u2 * cos], axis=-1)


def apply_axial_rope(u, tables):
    cr, sr, cc, sc = tables
    half = HEAD_DIM // 2
    return jnp.concatenate([_rotate(u[..., :half], cr, sr), _rotate(u[..., half:], cc, sc)], axis=-1)


def hyena_filters(length, w1, b1, w2, b2, w3, b3, w4, freq):
    f32 = jnp.float32
    t = jnp.linspace(0.0, 1.0, length, dtype=f32)[:, None]
    bands = (HY_EMB - 1) // 2
    w_ang = 2.0 * math.pi * jnp.arange(length, dtype=f32)[:, None] / length
    fr = jnp.linspace(1e-4, bands - 1, bands, dtype=f32)[None]
    z = jnp.concatenate([t, jnp.cos(fr * w_ang), -jnp.sin(fr * w_ang)], axis=-1)
    fq = freq.astype(f32)
    hdn = jnp.sin(fq * (z @ w1.astype(f32) + b1.astype(f32)))
    hdn = jnp.sin(fq * (hdn @ w2.astype(f32) + b2.astype(f32)))
    hdn = jnp.sin(fq * (hdn @ w3.astype(f32) + b3.astype(f32)))
    h = (hdn @ w4.astype(f32)).reshape(length, HY_ORDER, 2, HY_W)
    max_decay = math.log(HY_TARGET) / HY_SHORT_PCT
    min_decay = math.log(HY_TARGET) / HY_LONG_PCT
    deltas = jnp.abs(jnp.linspace(min_decay, max_decay, HY_W, dtype=f32))
    h = h * jnp.exp(-t * deltas)[:, None, None, :]
    h2 = jnp.concatenate([h[:, :, 0], jnp.zeros((1, HY_ORDER, HY_W), f32), h[:0:-1, :, 1]], axis=0)
    h2 = h2 / jnp.sum(jnp.abs(h2), axis=0, keepdims=True)
    return jnp.fft.rfft(h2, axis=0)


def hyena_mix(u, hf, filter_bias, conv_w, conv_b):
    length = u.shape[1]
    u = dwconv_centred(u.astype(jnp.float32), conv_w, conv_b)
    v, x1, x2 = jnp.split(u, 3, axis=-1)
    z = v
    for o, gate in enumerate((x1, x2)):
        zf = jnp.fft.rfft(z, n=2 * length, axis=1)
        zc = jnp.fft.irfft(zf * hf[None, :, o], n=2 * length, axis=1)[:, :length]
        z = gate * (zc + z * filter_bias[o].astype(jnp.float32))
    return z


def window_attention(q, k, v, k_c, v_c, sink):
    bsz, length = q.shape[:2]
    nb = length // ATT_BLOCK
    scale = HEAD_DIM ** -0.5
    qb = q.reshape(bsz, nb, ATT_BLOCK, ATT_KV_HEADS, ATT_GROUP, HEAD_DIM)
    pad = ((0, 0), (ATT_BLOCK, ATT_BLOCK), (0, 0), (0, 0))

    def band(a):
        ap = jnp.pad(a, pad).reshape(bsz, nb + 2, ATT_BLOCK, ATT_KV_HEADS, HEAD_DIM)
        return jnp.concatenate([ap[:, :-2], ap[:, 1:-1], ap[:, 2:]], axis=2)

    kw, vw = band(k), band(v)
    s_loc = jnp.einsum('bnqhgd,bnkhd->bnhgqk', qb, kw) * scale
    s_ctx = jnp.einsum('bnqhgd,bchd->bnhgqc', qb, k_c) * scale
    qpos = jnp.arange(nb)[:, None] * ATT_BLOCK + jnp.arange(ATT_BLOCK)[None]
    kpos = (jnp.arange(nb)[:, None] - 1) * ATT_BLOCK + jnp.arange(3 * ATT_BLOCK)[None]
    rel = kpos[:, None, :] - qpos[:, :, None]
    valid = (jnp.abs(rel) <= ATT_WINDOW) & (kpos[:, None, :] >= 0) & (kpos[:, None, :] < length)
    s_loc = jnp.where(valid[None, :, None, None], s_loc, -jnp.inf)
    sink_l = jnp.broadcast_to(sink.astype(jnp.float32).reshape(1, 1, ATT_KV_HEADS, ATT_GROUP, 1, 1),
                              s_loc.shape[:-1] + (1,))
    p = jax.nn.softmax(jnp.concatenate([s_loc, s_ctx, sink_l], axis=-1), axis=-1)
    n_loc = 3 * ATT_BLOCK
    n_ctx = k_c.shape[1]
    o = (jnp.einsum('bnhgqk,bnkhd->bnqhgd', p[..., :n_loc], vw)
         + jnp.einsum('bnhgqc,bchd->bnqhgd', p[..., n_loc:n_loc + n_ctx], v_c))
    return o.reshape(bsz, length, ATT_Q_W)


def context_attention(q_c, k_c, v_c, sink):
    bsz, n_ctx = q_c.shape[:2]
    s = jnp.einsum('bqhgd,bkhd->bhgqk', q_c, k_c) * HEAD_DIM ** -0.5
    sink_l = jnp.broadcast_to(sink.astype(jnp.float32).reshape(1, ATT_KV_HEADS, ATT_GROUP, 1, 1),
                              s.shape[:-1] + (1,))
    p = jax.nn.softmax(jnp.concatenate([s, sink_l], axis=-1), axis=-1)[..., :-1]
    return jnp.einsum('bhgqk,bkhd->bqhgd', p, v_c).reshape(bsz, n_ctx, ATT_Q_W)


def ssd_scan(x, dt, a, bm, cm, d_skip, init, need_y):
    bsz, length, n_heads, hd = x.shape
    nc = length // SSD_CHUNK
    hpg = n_heads // SSD_GROUPS
    da = (dt * a).reshape(bsz, nc, SSD_CHUNK, SSD_GROUPS, hpg)
    cs = jnp.cumsum(da, axis=2)
    xdt = (x * dt[..., None]).reshape(bsz, nc, SSD_CHUNK, SSD_GROUPS, hpg, hd)
    bc = bm.reshape(bsz, nc, SSD_CHUNK, SSD_GROUPS, SSD_STATE)
    cc = cm.reshape(bsz, nc, SSD_CHUNK, SSD_GROUPS, SSD_STATE)
    to_end = jnp.exp(cs[:, :, -1:] - cs)
    states = jnp.einsum('bcsgn,bcsgh,bcsghp->bcghpn', bc, to_end, xdt)
    chunk_decay = jnp.exp(cs[:, :, -1])

    def step(s, inp):
        st, dec = inp
        return s * dec[..., None, None] + st, s

    s_final, s_in = lax.scan(step, init, (jnp.moveaxis(states, 1, 0), jnp.moveaxis(chunk_decay, 1, 0)))
    if not need_y:
        return None, s_final
    s_in = jnp.moveaxis(s_in, 0, 1)
    cs_t = jnp.moveaxis(cs, 2, -1)
    diff = cs_t[..., :, None] - cs_t[..., None, :]
    lower = jnp.tril(jnp.ones((SSD_CHUNK, SSD_CHUNK), bool))
    decay = jnp.where(lower, jnp.exp(jnp.where(lower, diff, 0.0)), 0.0)
    scores = jnp.einsum('bclgn,bcsgn->bcgls', cc, bc)
    y_diag = jnp.einsum('bcgls,bcghls,bcsghp->bclghp', scores, decay, xdt)
    y_off = jnp.einsum('bclgn,bcghpn,bclgh->bclghp', cc, s_in, jnp.exp(cs))
    y = (y_diag + y_off).reshape(bsz, length, n_heads, hd) + d_skip[:, None] * x
    return y, s_final


def hgrn2_scan(q, k, v, g, init, need_o):
    bsz, length, n_heads, _ = k.shape
    nc = length // HG_CHUNK

    def chunks(a):
        return a.reshape(bsz, nc, HG_CHUNK, n_heads, a.shape[-1]).transpose(1, 0, 3, 2, 4)

    lower = jnp.tril(jnp.ones((HG_CHUNK, HG_CHUNK), bool))[:, :, None]

    def update(s, kc, vc, cum):
        last = cum[:, :, -1]
        return (s * jnp.exp(last)[..., None]
                + jnp.einsum('bhsk,bhsv->bhkv', kc * jnp.exp(last[:, :, None] - cum), vc))

    if not need_o:
        def step_state(s, inp):
            kc, vc, gc = inp
            return update(s, kc, vc, jnp.cumsum(gc, axis=2)), None
        s_final, _ = lax.scan(step_state, init, (chunks(k), chunks(v), chunks(g)))
        return None, s_final

    def step(s, inp):
        qc, kc, vc, gc = inp
        cum = jnp.cumsum(gc, axis=2)
        diff = cum[:, :, :, None, :] - cum[:, :, None, :, :]
        decay = jnp.where(lower, jnp.exp(jnp.where(lower, diff, 0.0)), 0.0)
        att = jnp.einsum('bhtk,bhsk,bhtsk->bhts', qc, kc, decay)
        o = (jnp.einsum('bhtk,bhkv->bhtv', qc * jnp.exp(cum), s)
             + jnp.einsum('bhts,bhsv->bhtv', att, vc))
        return update(s, kc, vc, cum), o

    s_final, o = lax.scan(step, init, (chunks(q), chunks(k), chunks(v), chunks(g)))
    return o.transpose(1, 0, 3, 2, 4).reshape(bsz, length, n_heads, v.shape[-1]), s_final


HY_N1 = 128
HY_N2 = 256
HY_NB = 8
HY_TM = 512


def _hy_dft_constants(length):
    import numpy as np
    n = 2 * length
    assert n == HY_N1 * HY_N2
    half = HY_N1 // 2
    k1 = np.arange(HY_N1)[:, None]
    n1 = np.arange(half)[None, :]
    n2 = np.arange(HY_N2)[:, None, None]
    ang = 2 * np.pi * (k1 * n1 / HY_N1)[None] + 2 * np.pi * n2 * k1[None] / n
    fwd = np.stack([np.cos(ang), -np.sin(ang)], axis=2).reshape(HY_N2, 2 * HY_N1, half)
    inv = fwd.transpose(0, 2, 1)
    a2 = 2 * np.pi * np.outer(np.arange(HY_N2), np.arange(HY_N2)) / HY_N2
    c, s = np.cos(a2), -np.sin(a2)
    m_fwd = np.concatenate([np.stack([c, -s], axis=2).reshape(HY_N2, 2 * HY_N2),
                            np.stack([s, c], axis=2).reshape(HY_N2, 2 * HY_N2)], axis=0)
    m_inv = np.stack([np.concatenate([c, s], axis=1), np.concatenate([-s, c], axis=1)],
                     axis=1).reshape(2 * HY_N2, 2 * HY_N2)
    f = np.float32
    return fwd.astype(f), inv.astype(f), m_fwd.astype(f), m_inv.astype(f)


def _hy_conv_body(p_ref, prev_ref, next_ref, w_ref, b_ref, o_ref):
    i = pl.program_id(1)
    n = pl.num_programs(1)
    u = p_ref[...]
    rows = u.shape[0]
    row = lax.broadcasted_iota(jnp.int32, u.shape, 0)
    before = jnp.where(i == 0, 0.0, prev_ref[HALO - 1:HALO, :])
    after = jnp.where(i == n - 1, 0.0, next_ref[0:1, :])
    up = jnp.where(row == 0, before, pltpu.roll(u, 1, 0))
    un = jnp.where(row == rows - 1, after, pltpu.roll(u, rows - 1, 0))
    o_ref[...] = w_ref[0:1, :] * up + w_ref[1:2, :] * u + w_ref[2:3, :] * un + b_ref[...]


def _hy_conv(p, conv_w, conv_b, tm):
    length = p.shape[0]
    per = tm // HALO
    last = length // HALO - 1
    return pl.pallas_call(
        _hy_conv_body,
        grid=(3, length // tm),
        in_specs=[pl.BlockSpec((tm, HY_W), lambda c, i: (i, c)),
                  pl.BlockSpec((HALO, HY_W), lambda c, i: (jnp.maximum(i * per - 1, 0), c)),
                  pl.BlockSpec((HALO, HY_W), lambda c, i: (jnp.minimum((i + 1) * per, last), c)),
                  pl.BlockSpec((CONV_W, HY_W), lambda c, i: (0, c)),
                  pl.BlockSpec((1, HY_W), lambda c, i: (0, c))],
        out_specs=pl.BlockSpec((None, tm, HY_W), lambda c, i: (c, i, 0)),
        out_shape=jax.ShapeDtypeStruct((3, length, HY_W), jnp.float32),
        name="hy_conv",
    )(p, p, p, conv_w, conv_b.reshape(1, -1))


def _hy_filter_body(z_ref, w1_ref, b1_ref, w2_ref, b2_ref, w3_ref, b3_ref, w4_ref, fq_ref, dl_ref,
                    e_ref, o_ref, nrm_ref):
    i = pl.program_id(0)
    bf = jnp.bfloat16
    z = z_ref[...]
    fq = fq_ref[...]

    def layer(a, w_ref, b_ref):
        return jnp.sin(fq * (jnp.dot(a.astype(bf), w_ref[...].astype(bf), preferred_element_type=jnp.float32)
                             + b_ref[...]))

    half = z.shape[0] // 2
    hdn = layer(layer(layer(jnp.concatenate([z[:half], z[half:]], axis=1), w1_ref, b1_ref), w2_ref, b2_ref),
                w3_ref, b3_ref).astype(bf)
    w4 = w4_ref[...].astype(bf)
    h = jnp.concatenate([jnp.dot(hdn[:, :HY_FFN], w4, preferred_element_type=jnp.float32),
                         jnp.dot(hdn[:, HY_FFN:], w4, preferred_element_type=jnp.float32)], axis=0)
    decay = jnp.exp(-z[:, 0:1] * dl_ref[...])
    first = (lax.broadcasted_iota(jnp.int32, decay.shape, 0) == 0) & (i == 0)
    acc = []
    for order in range(HY_ORDER):
        lo = order * 2 * HY_W
        h0 = h[:, lo:lo + HY_W] * decay
        h1 = jnp.where(first, 0.0, h[:, lo + HY_W:lo + 2 * HY_W] * decay)
        e_ref[:, order * HY_W:(order + 1) * HY_W] = h0 + h1
        o_ref[:, order * HY_W:(order + 1) * HY_W] = h0 - h1
        acc.append(jnp.sum(jnp.abs(h0) + jnp.abs(h1), axis=0, keepdims=True))
    part = jnp.concatenate(acc, axis=1)

    @pl.when(i == 0)
    def _():
        nrm_ref[...] = jnp.zeros_like(nrm_ref)

    nrm_ref[...] = nrm_ref[...] + part


def _hy_filter_time(length, w1, b1, w2, b2, w3, b3, w4, freq, tm):
    f32 = jnp.float32
    t = jnp.linspace(0.0, 1.0, length, dtype=f32)[:, None]
    bands = (HY_EMB - 1) // 2
    w_ang = 2.0 * math.pi * jnp.arange(length, dtype=f32)[:, None] / length
    fr = jnp.linspace(1e-4, bands - 1, bands, dtype=f32)[None]
    z = jnp.concatenate([t, jnp.cos(fr * w_ang), -jnp.sin(fr * w_ang)], axis=-1)
    z = jnp.pad(z, ((0, 0), (0, LANE - HY_EMB)))
    w1p = jnp.pad(w1.astype(f32), ((0, LANE - HY_EMB), (0, 0)))
    max_decay = math.log(HY_TARGET) / HY_SHORT_PCT
    min_decay = math.log(HY_TARGET) / HY_LONG_PCT
    deltas = jnp.abs(jnp.linspace(min_decay, max_decay, HY_W, dtype=f32)).reshape(1, HY_W)
    full = lambda a: pl.BlockSpec(a.shape, lambda i: (0,) * a.ndim)
    row2 = lambda v: jnp.tile(v.astype(f32).reshape(1, -1), (1, 2))
    diag2 = lambda w: jnp.kron(jnp.eye(2, dtype=f32), w.astype(f32))
    args = (z, diag2(w1p), row2(b1), diag2(w2), row2(b2), diag2(w3), row2(b3), w4.astype(f32), row2(freq), deltas)
    ow = HY_ORDER * HY_W
    return pl.pallas_call(
        _hy_filter_body,
        grid=(length // tm,),
        in_specs=[pl.BlockSpec((tm, LANE), lambda i: (i, 0))] + [full(a) for a in args[1:]],
        out_specs=[pl.BlockSpec((tm, ow), lambda i: (i, 0)), pl.BlockSpec((tm, ow), lambda i: (i, 0)),
                   pl.BlockSpec((1, ow), lambda i: (0, 0))],
        out_shape=[jax.ShapeDtypeStruct((length, ow), f32), jax.ShapeDtypeStruct((length, ow), f32),
                   jax.ShapeDtypeStruct((1, ow), f32)],
        compiler_params=pltpu.CompilerParams(dimension_semantics=("arbitrary",)),
        name="hy_filter_time",
    )(*args)


def _words(x_bf16):
    return pltpu.bitcast(x_bf16, jnp.uint32)


def _halves(w_u32):
    return pltpu.bitcast(w_u32, jnp.bfloat16)


def _transpose8(parts):
    rows, cols = parts[0].shape
    parts = [p.reshape(rows // SUBLANE, SUBLANE, cols) for p in parts]
    row = lax.broadcasted_iota(jnp.int32, parts[0].shape, 1)
    for s in (1, 2, 4):
        keep = (row & s) == 0
        nxt = list(parts)
        for i in range(SUBLANE):
            if i & s == 0:
                a, b = parts[i], parts[i + s]
                nxt[i] = jnp.where(keep, a, pltpu.roll(b, s, 1))
                nxt[i + s] = jnp.where(keep, pltpu.roll(a, SUBLANE - s, 1), b)
        parts = nxt
    return [p.reshape(rows, cols) for p in parts]


def _gather_tiles(ref, j, n):
    return jnp.concatenate([ref[SUBLANE * g + j] for g in range(n)], axis=0)


def _hy_s1_body(x_ref, f_ref, a_ref):
    groups = x_ref.shape[0] // SUBLANE
    xs = _transpose8([_gather_tiles(x_ref, j, groups) for j in range(SUBLANE)])
    words = []
    for i in range(HY_NB):
        acc = jnp.dot(f_ref[i], xs[i].astype(jnp.bfloat16), preferred_element_type=jnp.float32)
        words.append(_words(acc.astype(jnp.bfloat16)))
    tiles = _transpose8(words)
    per = HY_N1 // SUBLANE
    for j in range(SUBLANE):
        for g in range(per):
            a_ref[per * j + g] = tiles[j][SUBLANE * g:SUBLANE * (g + 1)]


def _hy_s1(x4, sel, fwd):
    assert HY_NB == SUBLANE
    _, half, _, width = x4.shape
    return pl.pallas_call(
        _hy_s1_body,
        grid=(HY_N2 // HY_NB,),
        in_specs=[pl.BlockSpec((None, half, HY_NB, width), lambda j: (sel, 0, j, 0)),
                  pl.BlockSpec((HY_NB, 2 * HY_N1, half), lambda j: (j, 0, 0))],
        out_specs=pl.BlockSpec((HY_N1, HY_NB, width), lambda j: (0, j, 0)),
        out_shape=jax.ShapeDtypeStruct((HY_N1, HY_N2, width), jnp.uint32),
        name="hy_stage1",
    )(x4, fwd)


HY_SLABS = 2


def _hy_s2f_body(ae_ref, ao_ref, m_ref, sc_ref, hr_ref, hi_ref):
    dot = lambda a, b: jnp.dot(a, b, preferred_element_type=jnp.float32)
    for s in range(HY_SLABS):
        hr_ref[s] = (dot(m_ref[:HY_N2, :], _halves(ae_ref[s])) * sc_ref[...]).astype(hr_ref.dtype)
        hi_ref[s] = (dot(m_ref[HY_N2:, :], _halves(ao_ref[s])) * sc_ref[...]).astype(hi_ref.dtype)


def _hy_s2f(a_e, a_o, m_fwd, scale):
    width = scale.shape[1]
    slab = pl.BlockSpec((HY_SLABS, HY_N2, width), lambda k: (k, 0, 0))
    out = jax.ShapeDtypeStruct((HY_N1, HY_N2, width), jnp.bfloat16)
    return pl.pallas_call(
        _hy_s2f_body,
        grid=(HY_N1 // HY_SLABS,),
        in_specs=[slab, slab, pl.BlockSpec((2 * HY_N2, 2 * HY_N2), lambda k: (0, 0)),
                  pl.BlockSpec((1, width), lambda k: (0, 0))],
        out_specs=[slab, slab],
        out_shape=[out, out],
        name="hy_filter_stage2",
    )(a_e, a_o, m_fwd, scale)


def _hy_s2_body(a_ref, hr_ref, hi_ref, mf_ref, mi_ref, b_ref):
    bf = jnp.bfloat16
    for s in range(HY_SLABS):
        x = jnp.dot(mf_ref[...], _halves(a_ref[s]), preferred_element_type=jnp.float32)
        xr, xi = x[:HY_N2], x[HY_N2:]
        hr = hr_ref[s].astype(jnp.float32)
        hi = hi_ref[s].astype(jnp.float32)
        y = jnp.concatenate([(xr * hr - xi * hi).astype(bf), (xr * hi + xi * hr).astype(bf)], axis=0)
        b = jnp.dot(mi_ref[...], y, preferred_element_type=jnp.float32)
        b_ref[s] = _words(b.astype(bf))


def _hy_s2(a, h_re, h_im, m_fwd, m_inv, order):
    slab = pl.BlockSpec((HY_SLABS, HY_N2, HY_W), lambda k: (k, 0, 0))
    hslab = pl.BlockSpec((HY_SLABS, HY_N2, HY_W), lambda k: (k, 0, order))
    mat = pl.BlockSpec((2 * HY_N2, 2 * HY_N2), lambda k: (0, 0))
    return pl.pallas_call(
        _hy_s2_body,
        grid=(HY_N1 // HY_SLABS,),
        in_specs=[slab, hslab, hslab, mat, mat],
        out_specs=slab,
        out_shape=jax.ShapeDtypeStruct((HY_N1, HY_N2, HY_W), jnp.uint32),
        name="hy_stage2",
    )(a, h_re, h_im, m_fwd, m_inv)


def _hy_is1_body(b_ref, g_ref, z_ref, gate_ref, bias_ref, o_ref):
    per = HY_N1 // SUBLANE
    spectra = _transpose8([jnp.concatenate([b_ref[per * j + g] for g in range(per)], axis=0)
                           for j in range(SUBLANE)])
    convs = [jnp.dot(g_ref[i], _halves(spectra[i]), preferred_element_type=jnp.float32) for i in range(HY_NB)]
    tiles = _transpose8(convs)
    for j in range(SUBLANE):
        for g in range(o_ref.shape[0] // SUBLANE):
            n1 = SUBLANE * g + j
            o_ref[n1] = gate_ref[n1] * (tiles[j][SUBLANE * g:SUBLANE * (g + 1)] + z_ref[n1] * bias_ref[...])


def _hy_is1(b, inv, z4, z_sel, gate4, gate_sel, bias):
    _, half, _, width = z4.shape
    real = lambda sel: pl.BlockSpec((None, half, HY_NB, width), lambda j: (sel, 0, j, 0))
    return pl.pallas_call(
        _hy_is1_body,
        grid=(HY_N2 // HY_NB,),
        in_specs=[pl.BlockSpec((HY_N1, HY_NB, width), lambda j: (0, j, 0)),
                  pl.BlockSpec((HY_NB, half, 2 * HY_N1), lambda j: (j, 0, 0)),
                  real(z_sel), real(gate_sel), pl.BlockSpec((1, width), lambda j: (0, 0))],
        out_specs=pl.BlockSpec((half, HY_NB, width), lambda j: (0, j, 0)),
        out_shape=jax.ShapeDtypeStruct((half, HY_N2, width), jnp.float32),
        name="hy_inv_stage1",
    )(b, inv, z4, gate4, bias)


def hyena_long(p_hy, f_w1, f_b1, f_w2, f_b2, f_w3, f_b3, f_w4, f_freq, f_bias, conv_w, conv_b):
    length = p_hy.shape[0]
    bf = jnp.bfloat16
    half = HY_N1 // 2
    fwd, inv, m_fwd, m_inv = (jnp.asarray(m, bf) for m in _hy_dft_constants(length))
    e, od, nrm = _hy_filter_time(length, f_w1, f_b1, f_w2, f_b2, f_w3, f_b3, f_w4, f_freq, HY_TM)
    ow = HY_ORDER * HY_W
    a_e = _hy_s1(e.reshape(1, half, HY_N2, ow), 0, fwd)
    a_o = _hy_s1(od.reshape(1, half, HY_N2, ow), 0, fwd)
    scale = 1.0 / (nrm * (2.0 * length))
    h_re, h_im = _hy_s2f(a_e, a_o, m_fwd, scale)
    u = _hy_conv(p_hy, conv_w, conv_b, HY_TM).reshape(3, half, HY_N2, HY_W)
    z, z_sel = u, 0
    for order in range(HY_ORDER):
        b = _hy_s2(_hy_s1(z, z_sel, fwd), h_re, h_im, m_fwd, m_inv, order)
        z = _hy_is1(b, inv, z, z_sel, u, 1 + order, f_bias[order].astype(jnp.float32).reshape(1, HY_W))[None]
        z_sel = 0
    return z.reshape(length, HY_W)


def even_mixer(h, hc, w_in, conv_w, conv_b, f_w1, f_b1, f_w2, f_b2, f_w3, f_b3, f_w4, f_freq, f_bias,
               q_norm, k_norm, sink, ctx_out):
    f32 = jnp.float32
    bsz, length, _ = h.shape
    n_ctx = hc.shape[1]
    filt = (f_w1, f_b1, f_w2, f_b2, f_w3, f_b3, f_w4, f_freq)
    o_v = ATT_KV_W
    o_q = 2 * ATT_KV_W
    o_hy = o_q + ATT_Q_W
    w_perm = jnp.concatenate([w_in[:, o_hy:], w_in[:, o_q:o_hy], w_in[:, :o_q]], axis=1)
    hy_w = 3 * HY_W
    p_hy = pmm3(h, w_perm).astype(f32)
    p = jnp.concatenate([p_hy[..., hy_w + ATT_Q_W:], p_hy[..., hy_w:hy_w + ATT_Q_W]], axis=-1)
    pc = pmm3(hc, w_in if ctx_out else w_in[:, :o_q]).astype(f32)

    def heads(a, n_heads):
        return a.reshape(a.shape[0], a.shape[1], n_heads, HEAD_DIM)

    k_c = rms_norm(heads(pc[..., :o_v], ATT_KV_HEADS), k_norm)
    v_c = heads(pc[..., o_v:o_q], ATT_KV_HEADS)
    rope = axial_rope_tables(length)
    k = apply_axial_rope(rms_norm(heads(p[..., :o_v], ATT_KV_HEADS), k_norm), rope)
    v = heads(p[..., o_v:o_q], ATT_KV_HEADS)
    q = apply_axial_rope(rms_norm(heads(p[..., o_q:o_hy], ATT_HEADS), q_norm), rope)
    att = window_attention(q.reshape(bsz, length, ATT_KV_HEADS, ATT_GROUP, HEAD_DIM), k, v, k_c, v_c, sink)
    hy = hyena_long(p_hy[0], *filt, f_bias, conv_w, conv_b)[None]
    out = jnp.concatenate([hy, att], axis=-1).astype(h.dtype)
    if not ctx_out:
        return out, None
    q_c = rms_norm(heads(pc[..., o_q:o_hy], ATT_HEADS), q_norm).reshape(bsz, n_ctx, ATT_KV_HEADS, ATT_GROUP, HEAD_DIM)
    att_c = context_attention(q_c, k_c, v_c, sink)
    hy_c = hyena_mix(pc[..., o_hy:], hyena_filters(n_ctx, *filt), f_bias, conv_w, conv_b)
    return out, jnp.concatenate([hy_c, att_c], axis=-1).astype(hc.dtype)


ATT_TQ = ATT_BLOCK
EVEN_TM = 256
EV_Q_BLK = 3 * HY_W // ATT_Q_W
EV_K_BLK = (3 * HY_W + ATT_Q_W) // ATT_KV_W
EV_V_BLK = EV_K_BLK + 1


def _rope_tables(length):
    cr, sr, cc, sc = axial_rope_tables(length)
    return jnp.concatenate([cr, cr, cc, cc], axis=-1), jnp.concatenate([-sr, sr, -sc, sc], axis=-1)


def _head_norm_rope(x, g_row, c, s, seg):
    sq = x * x
    hi = sq.astype(jnp.bfloat16)
    lo = (sq - hi.astype(jnp.float32)).astype(jnp.bfloat16)
    ms = (jnp.dot(hi, seg, preferred_element_type=jnp.float32) + jnp.dot(lo, seg, preferred_element_type=jnp.float32))
    y = x * lax.rsqrt(ms + EPS) * g_row
    width = x.shape[1]
    quarter = HEAD_DIM // 4
    lane = lax.broadcasted_iota(jnp.int32, x.shape, 1)
    partner = jnp.where((lane & quarter) == 0, pltpu.roll(y, width - quarter, 1), pltpu.roll(y, quarter, 1))
    return y * c + partner * s


def _qk_prep_body(q_ref, k_ref, c_ref, s_ref, qn_ref, kn_ref, segq_ref, segk_ref, qo_ref, ko_ref):
    c, s = c_ref[...], s_ref[...]
    tile = lambda t, n: jnp.concatenate([t] * n, axis=1)
    qo_ref[...] = _head_norm_rope(q_ref[...], tile(qn_ref[...], ATT_HEADS), tile(c, ATT_HEADS), tile(s, ATT_HEADS),
                                  segq_ref[...]).astype(qo_ref.dtype)
    ko_ref[...] = _head_norm_rope(k_ref[...], tile(kn_ref[...], ATT_KV_HEADS), tile(c, ATT_KV_HEADS),
                                  tile(s, ATT_KV_HEADS), segk_ref[...]).astype(ko_ref.dtype)


def _qk_prep(p, rope_c, rope_s, q_norm, k_norm, tm):
    import numpy as np
    rows = p.shape[0]
    seg = lambda w: jnp.asarray(np.kron(np.eye(w // HEAD_DIM), np.full((HEAD_DIM, HEAD_DIM), 1.0 / HEAD_DIM)),
                                jnp.bfloat16)
    const = lambda shape: pl.BlockSpec(shape, lambda i: (0, 0))
    return pl.pallas_call(
        _qk_prep_body,
        grid=(rows // tm,),
        in_specs=[pl.BlockSpec((tm, ATT_Q_W), lambda i: (i, EV_Q_BLK)),
                  pl.BlockSpec((tm, ATT_KV_W), lambda i: (i, EV_K_BLK)),
                  pl.BlockSpec((tm, HEAD_DIM), lambda i: (i, 0)), pl.BlockSpec((tm, HEAD_DIM), lambda i: (i, 0)),
                  const((1, HEAD_DIM)), const((1, HEAD_DIM)),
                  const((ATT_Q_W, ATT_Q_W)), const((ATT_KV_W, ATT_KV_W))],
        out_specs=[pl.BlockSpec((tm, ATT_Q_W), lambda i: (i, 0)), pl.BlockSpec((tm, ATT_KV_W), lambda i: (i, 0))],
        out_shape=[jax.ShapeDtypeStruct((rows, ATT_Q_W), jnp.bfloat16),
                   jax.ShapeDtypeStruct((rows, ATT_KV_W), jnp.bfloat16)],
        name="qk_prep",
    )(p, p, rope_c, rope_s, q_norm.astype(jnp.float32).reshape(1, -1), k_norm.astype(jnp.float32).reshape(1, -1),
      seg(ATT_Q_W), seg(ATT_KV_W))


def _att_body(sink_ref, q_ref, kc_ref, vc_ref, *rest, local):
    if local:
        kp_ref, k0_ref, kn_ref, vp_ref, v0_ref, vn_ref, o_ref = rest
    else:
        (o_ref,) = rest
    b = pl.program_id(0)
    nb = pl.num_programs(0)
    bf = jnp.bfloat16
    scale = HEAD_DIM ** -0.5
    q = q_ref[...]
    kc = kc_ref[...]
    vc = vc_ref[...].astype(bf)
    if local:
        kb = jnp.concatenate([kp_ref[...], k0_ref[...], kn_ref[...]], axis=0)
        vb = jnp.concatenate([vp_ref[...], v0_ref[...], vn_ref[...]], axis=0).astype(bf)
        i = lax.broadcasted_iota(jnp.int32, (ATT_TQ, 3 * ATT_BLOCK), 0)
        j = lax.broadcasted_iota(jnp.int32, (ATT_TQ, 3 * ATT_BLOCK), 1)
        rel = j - ATT_BLOCK - i
        valid = ((jnp.abs(rel) <= ATT_WINDOW) & ((b > 0) | (j >= ATT_BLOCK))
                 & ((b < nb - 1) | (j < 2 * ATT_BLOCK)))
    rows = ATT_GROUP * ATT_TQ
    head_of_row = lax.broadcasted_iota(jnp.int32, (rows, 1), 0) // ATT_TQ
    for hk in range(ATT_KV_HEADS):
        heads = range(hk * ATT_GROUP, (hk + 1) * ATT_GROUP)
        qs = jnp.concatenate([q[:, h * HEAD_DIM:(h + 1) * HEAD_DIM] for h in heads], axis=0)
        kv_cols = slice(hk * HEAD_DIM, (hk + 1) * HEAD_DIM)
        sink = jnp.zeros((rows, 1), jnp.float32)
        for g, h in enumerate(heads):
            sink = jnp.where(head_of_row == g, sink_ref[h], sink)
        s_ctx = _dot_nt(qs, kc[:, kv_cols]) * scale
        m = jnp.maximum(jnp.max(s_ctx, axis=-1, keepdims=True), sink)
        if local:
            valid4 = jnp.concatenate([valid] * ATT_GROUP, axis=0)
            s_loc = jnp.where(valid4, _dot_nt(qs, kb[:, kv_cols]) * scale, -jnp.inf)
            m = jnp.maximum(m, jnp.max(s_loc, axis=-1, keepdims=True))
        p_ctx = jnp.exp(s_ctx - m)
        den = jnp.sum(p_ctx, axis=-1, keepdims=True) + jnp.exp(sink - m)
        acc = jnp.dot(p_ctx.astype(bf), vc[:, kv_cols], preferred_element_type=jnp.float32)
        if local:
            p_loc = jnp.exp(s_loc - m)
            den = den + jnp.sum(p_loc, axis=-1, keepdims=True)
            acc = acc + jnp.dot(p_loc.astype(bf), vb[:, kv_cols], preferred_element_type=jnp.float32)
        out = acc / den
        for g, h in enumerate(heads):
            o_ref[:, h * HEAD_DIM:(h + 1) * HEAD_DIM] = out[g * ATT_TQ:(g + 1) * ATT_TQ]


def _attention(qn, kn, p, kcn, pc, sink, local):
    rows = qn.shape[0]
    nb = rows // ATT_TQ
    n_ctx = kcn.shape[0]
    specs = [pl.BlockSpec((ATT_TQ, ATT_Q_W), lambda b, s: (b, 0)),
             pl.BlockSpec((n_ctx, ATT_KV_W), lambda b, s: (0, 0)),
             pl.BlockSpec((n_ctx, ATT_KV_W), lambda b, s: (0, EV_V_BLK))]
    args = [qn, kcn, pc]
    if local:
        prev = lambda b, s: jnp.maximum(b - 1, 0)
        nxt = lambda b, s: jnp.minimum(b + 1, nb - 1)
        for col, arr in ((0, kn), (EV_V_BLK, p)):
            specs += [pl.BlockSpec((ATT_BLOCK, ATT_KV_W), lambda b, s, col=col: (prev(b, s), col)),
                      pl.BlockSpec((ATT_BLOCK, ATT_KV_W), lambda b, s, col=col: (b, col)),
                      pl.BlockSpec((ATT_BLOCK, ATT_KV_W), lambda b, s, col=col: (nxt(b, s), col))]
            args += [arr, arr, arr]
    return pl.pallas_call(
        functools.partial(_att_body, local=local),
        grid_spec=pltpu.PrefetchScalarGridSpec(
            num_scalar_prefetch=1, grid=(nb,), in_specs=specs,
            out_specs=pl.BlockSpec((ATT_TQ, ATT_Q_W), lambda b, s: (b, 0))),
        out_shape=jax.ShapeDtypeStruct((rows, ATT_Q_W), jnp.float32),
        name="window_attention" if local else "context_attention",
    )(sink.astype(jnp.float32), *args)


def _even_merge_body(hy_ref, att_ref, x_ref, gt_ref, w_ref, o_ref):
    bf = jnp.bfloat16
    m = (jnp.dot(hy_ref[...].astype(bf), w_ref[:HY_W, :].astype(bf), preferred_element_type=jnp.float32)
         + jnp.dot(att_ref[...].astype(bf), w_ref[HY_W:, :].astype(bf), preferred_element_type=jnp.float32))
    o_ref[...] = x_ref[...] + gt_ref[...] * m


def _even_merge(hy, att, x2d, gt, w_out, tm):
    rows, d = x2d.shape
    half = pl.BlockSpec((tm, HY_W), lambda i: (i, 0))
    return pl.pallas_call(
        _even_merge_body,
        grid=(rows // tm,),
        in_specs=[half, half, pl.BlockSpec((tm, d), lambda i: (i, 0)), pl.BlockSpec((1, d), lambda i: (0, 0)),
                  pl.BlockSpec((MIX_W, d), lambda i: (0, 0))],
        out_specs=pl.BlockSpec((tm, d), lambda i: (i, 0)),
        out_shape=jax.ShapeDtypeStruct((rows, d), jnp.float32),
        name="even_merge",
    )(hy, att, x2d, gt.reshape(1, -1), w_out)


def _hy_short_body(e_ref, o_ref, sc_ref, u_ref, bias_ref, cf_ref, sf_ref, out_ref):
    bf = jnp.bfloat16
    cf, sf = cf_ref[...], sf_ref[...]
    dot = lambda a, b: jnp.dot(a, b.astype(bf), preferred_element_type=jnp.float32)
    z = u_ref[0]
    for order in range(HY_ORDER):
        cols = slice(order * HY_W, (order + 1) * HY_W)
        h_re = dot(cf, e_ref[:, cols]) * sc_ref[:, cols]
        h_im = dot(sf, o_ref[:, cols]) * sc_ref[:, cols]
        x_re, x_im = dot(cf, z), dot(sf, z)
        y_re = (x_re * h_re - x_im * h_im).astype(bf)
        y_im = (x_re * h_im + x_im * h_re).astype(bf)
        conv = _dot_tn(cf, y_re) + _dot_tn(sf, y_im)
        z = u_ref[1 + order] * (conv + z * bias_ref[order:order + 1, :])
    out_ref[...] = z


def hyena_short(pc, f_w1, f_b1, f_w2, f_b2, f_w3, f_b3, f_w4, f_freq, f_bias, conv_w, conv_b):
    import numpy as np
    rows = pc.shape[0]
    e, od, nrm = _hy_filter_time(rows, f_w1, f_b1, f_w2, f_b2, f_w3, f_b3, f_w4, f_freq, rows)
    u = _hy_conv(pc, conv_w, conv_b, rows)
    ang = 2 * np.pi * np.outer(np.arange(2 * rows), np.arange(rows)) / (2 * rows)
    cf, sf = jnp.asarray(np.cos(ang), jnp.bfloat16), jnp.asarray(-np.sin(ang), jnp.bfloat16)
    scale = 1.0 / (nrm * (2.0 * rows))
    return pl.pallas_call(
        _hy_short_body,
        out_shape=jax.ShapeDtypeStruct((rows, HY_W), jnp.float32),
        name="hy_short",
    )(e, od, scale, u, f_bias.astype(jnp.float32), cf, sf)


def even_layer(x, xc, g, mod, mod_c, w_in, conv_w, conv_b, f_w1, f_b1, f_w2, f_b2, f_w3, f_b3, f_w4, f_freq, f_bias,
               q_norm, k_norm, sink, gt, gt_c, w_out):
    f32 = jnp.float32
    length, n_ctx = x.shape[1], xc.shape[1]
    filt = (f_w1, f_b1, f_w2, f_b2, f_w3, f_b3, f_w4, f_freq, f_bias, conv_w, conv_b)
    o_q = 2 * ATT_KV_W
    o_hy = o_q + ATT_Q_W
    w_perm = jnp.concatenate([w_in[:, o_hy:], w_in[:, o_q:o_hy], w_in[:, :o_q]], axis=1).astype(jnp.bfloat16)
    twice = lambda v: jnp.stack([v.reshape(-1), v.reshape(-1)])
    p = norm_proj(x[0], g, twice(mod[0]), twice(mod[1]), w_perm)
    pc = norm_proj(xc[0], g, twice(mod_c[0]), twice(mod_c[1]), w_perm)
    rope_c, rope_s = _rope_tables(length)
    qn, kn = _qk_prep(p, rope_c, rope_s, q_norm, k_norm, EVEN_TM)
    qcn, kcn = _qk_prep(pc, jnp.ones((n_ctx, HEAD_DIM), f32), jnp.zeros((n_ctx, HEAD_DIM), f32), q_norm, k_norm,
                        n_ctx)
    att = _attention(qn, kn, p, kcn, pc, sink, True)
    att_c = _attention(qcn, None, None, kcn, pc, sink, False)
    hy = hyena_long(p, *filt)
    hy_c = hyena_short(pc, *filt)
    x_new = _even_merge(hy, att, x[0], gt, w_out, EVEN_TM)
    xc_new = _even_merge(hy_c, att_c, xc[0], gt_c, w_out, n_ctx)
    return x_new[None], xc_new[None]


def odd_mixer(h, hc, lb, w_in, conv_w, conv_b, dt_bias, a_log, d_skip, ssd_norm, hg_norm, ctx_out):
    f32 = jnp.float32
    bsz, length, _ = h.shape
    n_ctx = hc.shape[1]
    p = pmm3(h, w_in).astype(f32)
    pc = pmm3(hc, w_in if ctx_out else w_in[:, :ODD_STATE_COLS]).astype(f32)
    o_dt = SSD_CONV_CH
    o_f = SSD_CONV_CH + 2 * SSD_HEADS
    o_i = o_f + 2 * HG_W
    o_z = ODD_STATE_COLS
    o_q = o_z + SSD_W
    o_g = o_q + HG_W
    gn = SSD_GROUPS * SSD_STATE

    def streams(pp):
        n = pp.shape[1]
        xbc = jax.nn.silu(dwconv_centred(pp[..., :SSD_CONV_CH], conv_w, conv_b))
        xs = xbc[..., :SSD_W].reshape(bsz, n, SSD_HEADS, SSD_HEAD_DIM)
        bm = xbc[..., SSD_W:SSD_W + gn].reshape(bsz, n, SSD_GROUPS, SSD_STATE)
        cm = xbc[..., SSD_W + gn:].reshape(bsz, n, SSD_GROUPS, SSD_STATE)
        dt_raw = pp[..., o_dt:o_f].reshape(bsz, n, 2, SSD_HEADS)
        f_raw = pp[..., o_f:o_i].reshape(bsz, n, 2, HG_HEADS, HG_EXPAND)
        iv = pp[..., o_i:o_i + HG_W].reshape(bsz, n, HG_HEADS, HG_VDIM)
        return xs, bm, cm, dt_raw, f_raw, iv

    xs, bm, cm, dt_raw, f_raw, iv = streams(p)
    xs_c, bm_c, cm_c, dt_raw_c, f_raw_c, iv_c = streams(pc)
    q = jax.nn.silu(p[..., o_q:o_g]).reshape(bsz, length, HG_HEADS, HG_EXPAND)
    q_c = jax.nn.silu(pc[..., o_q:o_g]).reshape(bsz, n_ctx, HG_HEADS, HG_EXPAND) if ctx_out else None
    lb = lb.astype(f32).reshape(HG_HEADS, HG_EXPAND)
    ssd0 = jnp.zeros((bsz, SSD_GROUPS, SSD_HEADS // SSD_GROUPS, SSD_HEAD_DIM, SSD_STATE), f32)
    hg0 = jnp.zeros((bsz, HG_HEADS, HG_EXPAND, HG_VDIM), f32)
    y_dirs, o_dirs, yc_dirs, oc_dirs = [], [], [], []
    for d in range(2):
        fl = (lambda a: jnp.flip(a, axis=1)) if d == 1 else (lambda a: a)
        a = -jnp.exp(a_log[d].astype(f32))
        dsk = d_skip[d].astype(f32)
        dtb = dt_bias[d].astype(f32)
        dt_l = jax.nn.softplus(dt_raw[:, :, d] + dtb)
        dt_c = jax.nn.softplus(dt_raw_c[:, :, d] + dtb)
        yc, s_ctx = ssd_scan(fl(xs_c), fl(dt_c), a, fl(bm_c), fl(cm_c), dsk, ssd0, ctx_out)
        yl, _ = ssd_scan(fl(xs), fl(dt_l), a, fl(bm), fl(cm), dsk, s_ctx, True)
        y_dirs.append(fl(yl))
        f_l = lb + (1.0 - lb) * jax.nn.sigmoid(f_raw[:, :, d])
        f_c = lb + (1.0 - lb) * jax.nn.sigmoid(f_raw_c[:, :, d])
        oc, s_hg = hgrn2_scan(fl(q_c) if ctx_out else None, fl(1.0 - f_c), fl(iv_c), fl(jnp.log(f_c)), hg0, ctx_out)
        ol, _ = hgrn2_scan(fl(q), fl(1.0 - f_l), fl(iv), fl(jnp.log(f_l)), s_hg, True)
        o_dirs.append(fl(ol))
        if ctx_out:
            yc_dirs.append(fl(yc))
            oc_dirs.append(fl(oc))

    def merge(yy, oo, pp, n):
        z = pp[..., o_z:o_q]
        g = pp[..., o_g:]
        ys = (yy.reshape(bsz, n, SSD_W) * jax.nn.silu(z)).reshape(bsz, n, SSD_GROUPS, SSD_W // SSD_GROUPS)
        ys = rms_norm(ys, ssd_norm.reshape(SSD_GROUPS, SSD_W // SSD_GROUPS)).reshape(bsz, n, SSD_W)
        hs = rms_norm(oo, hg_norm.reshape(HG_HEADS, HG_VDIM)).reshape(bsz, n, HG_W) * jax.nn.silu(g)
        return jnp.concatenate([ys, hs], axis=-1)

    out = merge(y_dirs[0] + y_dirs[1], o_dirs[0] + o_dirs[1], p, length).astype(h.dtype)
    if not ctx_out:
        return out, None
    out_c = merge(yc_dirs[0] + yc_dirs[1], oc_dirs[0] + oc_dirs[1], pc, n_ctx).astype(hc.dtype)
    return out, out_c


SCAN_Q = 128
SCAN_LEVELS = 7
ODD_COLS = SSD_CONV_CH + 8 * 512
HALO = SUBLANE


def _scan_constants():
    import numpy as np
    q = SCAN_Q
    d_hg, d_ssd, pairs, laters = [], [], [], []
    for direction in (0, 1):
        pos = np.arange(q) if direction == 0 else q - 1 - np.arange(q)
        pj, pt = pos[None, :], pos[:, None]
        top = pj <= pt
        end = pj > pt
        ones = np.ones((SUBLANE, q), bool)
        lv, pr, lt = [], [], []
        for level in range(SCAN_LEVELS):
            b = 2 ** level
            start = (pos // (2 * b)) * (2 * b)
            mid = (start + b)[:, None]
            later = pos >= start + b
            lv.append(np.where(later[:, None], (pj >= mid) & (pj <= pt), (pj > pt) & (pj < mid)))
            pr.append((start[:, None] == start[None, :]) & later[:, None] & ~later[None, :])
            lt.append(np.broadcast_to(later[:, None], (q, LANE)))
        pr.append(np.eye(q, dtype=bool))
        pr.append(top)
        d_hg.append(np.concatenate([top, end] + lv + [ones], axis=0))
        d_ssd.append(np.concatenate([top, end, ones], axis=0))
        pairs.append(np.stack(pr))
        laters.append(np.stack(lt))
    f = np.float32
    twice = lambda m: np.concatenate([m, m], axis=-1)
    return (twice(np.stack(d_hg)).astype(f), twice(np.stack(d_ssd)).astype(f), np.stack(pairs).astype(f),
            np.stack(laters).astype(f))


def _split_dot(mm_bf16, v):
    hi = v.astype(jnp.bfloat16)
    lo = (v - hi.astype(jnp.float32)).astype(jnp.bfloat16)
    return jnp.dot(mm_bf16, jnp.concatenate([hi, lo], axis=0), preferred_element_type=jnp.float32)


def _dot_nt(a, b):
    return lax.dot_general(a, b, (((1,), (1,)), ((), ())), preferred_element_type=jnp.float32)


def _dot_tn(a, b):
    return lax.dot_general(a, b, (((0,), (0,)), ((), ())), preferred_element_type=jnp.float32)


def _softplus(x):
    return jnp.maximum(x, 0.0) + jnp.log1p(jnp.exp(-jnp.abs(x)))


def _scan_body(xbc_ref, prev_ref, next_ref, f_ref, iv_ref, q_ref, dt_ref,
               cw_ref, cb_ref, dtb_ref, a_ref, dsk_ref, lb_ref,
               dhg_ref, dssd_ref, pair_ref, later_ref, sel_ref,
               out_ref, s_ssd, s_hg, ydiag_ref, *, n_ctx_chunks, n_chunks):
    d = pl.program_id(0)
    j = pl.program_id(1)
    q_rows = SCAN_Q
    bf = jnp.bfloat16

    @pl.when(j == 0)
    def _():
        s_ssd[...] = jnp.zeros_like(s_ssd)
        s_hg[...] = jnp.zeros_like(s_hg)

    c = jnp.where(d == 0, j, jnp.where(j < n_ctx_chunks, n_ctx_chunks - 1 - j, n_chunks - 1 + n_ctx_chunks - j))
    first = (c == 0) | (c == n_ctx_chunks)
    last = (c == n_ctx_chunks - 1) | (c == n_chunks - 1)

    u = xbc_ref[...]
    row = lax.broadcasted_iota(jnp.int32, u.shape, 0)
    before = jnp.where(first, 0.0, prev_ref[HALO - 1:HALO, :])
    after = jnp.where(last, 0.0, next_ref[0:1, :])
    up = jnp.where(row == 0, before, pltpu.roll(u, 1, 0))
    un = jnp.where(row == q_rows - 1, after, pltpu.roll(u, q_rows - 1, 0))
    xbc = cw_ref[0:1, :] * up + cw_ref[1:2, :] * u + cw_ref[2:3, :] * un + cb_ref[...]
    xbc = xbc * jax.nn.sigmoid(xbc)
    x = xbc[:, :SSD_W]

    dt = _softplus(dt_ref[...] + dtb_ref[...])
    da = dt * a_ref[...]
    r = _split_dot(dssd_ref[...], da)
    cs, to_end, total = r[:q_rows], r[q_rows:2 * q_rows], r[2 * q_rows:2 * q_rows + 1]
    xdt = x * dt
    cs_hi = cs.astype(bf)
    cs_lo = (cs - cs_hi.astype(jnp.float32)).astype(bf)
    cs_rows = _dot_nt(sel_ref[...], cs_hi) + _dot_nt(sel_ref[...], cs_lo)
    l_mask = pair_ref[SCAN_LEVELS + 1]
    decay_in = jnp.exp2(cs)
    w_end = (jnp.exp2(to_end) * xdt).astype(bf)
    gn = SSD_GROUPS * SSD_STATE
    hpg = SSD_HEADS // SSD_GROUPS
    gw = hpg * SSD_HEAD_DIM
    for g in range(SSD_GROUPS):
        b_g = xbc[:, SSD_W + g * SSD_STATE:SSD_W + (g + 1) * SSD_STATE].astype(bf)
        c_g = xbc[:, SSD_W + gn + g * SSD_STATE:SSD_W + gn + (g + 1) * SSD_STATE].astype(bf)
        scores = _dot_nt(c_g, b_g)
        y_off = jnp.dot(c_g, s_ssd[g].astype(bf), preferred_element_type=jnp.float32)
        for hh in range(hpg):
            h = g * hpg + hh
            lo = h * SSD_HEAD_DIM
            diff = cs[:, lo:lo + 1] - cs_rows[h:h + 1, :]
            decay = jnp.exp2(jnp.minimum(diff, 0.0)) * l_mask
            ydiag_ref[:, lo:lo + SSD_HEAD_DIM] = jnp.dot((scores * decay).astype(bf),
                                                         xdt[:, lo:lo + SSD_HEAD_DIM].astype(bf),
                                                         preferred_element_type=jnp.float32)
        cols = slice(g * gw, (g + 1) * gw)
        out_ref[:, cols] = (ydiag_ref[:, cols] + decay_in[:, cols] * y_off + dsk_ref[:, cols] * x[:, cols])
        s_ssd[g] = jnp.exp2(total[:, cols]) * s_ssd[g] + _dot_tn(b_g, w_end[:, cols])

    lb = lb_ref[...]
    f = lb + (1.0 - lb) * jax.nn.sigmoid(f_ref[...])
    k_in = 1.0 - f
    qv = q_ref[...]
    qv = qv * jax.nn.sigmoid(qv)
    v_bf = iv_ref[...].astype(bf)
    e = jnp.exp2(_split_dot(dhg_ref[...], jnp.log2(f)))
    e_top, e_end = e[:q_rows], e[q_rows:2 * q_rows]
    e_tot = e[(2 + SCAN_LEVELS) * q_rows:(2 + SCAN_LEVELS) * q_rows + 1]
    for h in range(HG_HEADS):
        cols = slice(h * HG_EXPAND, (h + 1) * HG_EXPAND)
        q_h, k_h = qv[:, cols], k_in[:, cols]
        att = pair_ref[SCAN_LEVELS] * _dot_nt(q_h.astype(bf), k_h.astype(bf))
        for level in range(SCAN_LEVELS):
            e_l = e[(2 + level) * q_rows:(3 + level) * q_rows, cols]
            w_l = (jnp.where(later_ref[level] > 0.0, q_h, k_h) * e_l).astype(bf)
            att = att + pair_ref[level] * _dot_nt(w_l, w_l)
        o = jnp.dot(att.astype(bf), v_bf[:, cols], preferred_element_type=jnp.float32)
        o = o + _dot_nt((q_h * e_top[:, cols]).astype(bf), s_hg[h].astype(bf))
        out_ref[:, SSD_W + h * HG_VDIM:SSD_W + (h + 1) * HG_VDIM] = o
        s_hg[h] = e_tot[:, cols] * s_hg[h] + _dot_tn(v_bf[:, cols], (k_h * e_end[:, cols]).astype(bf))


def _odd_scan(p_all, conv_w, conv_b, dtb, a_cols, dsk, lb, n_ctx):
    n_rows = p_all.shape[0]
    n_chunks = n_rows // SCAN_Q
    ncc = n_ctx // SCAN_Q
    d_hg, d_ssd, pairs, laters = _scan_constants()
    bf = jnp.bfloat16
    import numpy as np
    sel = np.zeros((LANE, SSD_W), np.float32)
    sel[np.arange(SSD_HEADS), np.arange(SSD_HEADS) * SSD_HEAD_DIM] = 1.0

    def chunk(d, j):
        return jnp.where(d == 0, j, jnp.where(j < ncc, ncc - 1 - j, n_chunks - 1 + ncc - j))

    per = SCAN_Q // HALO
    last_halo = n_rows // HALO - 1
    col512 = lambda blk: (lambda d, j: (chunk(d, j), blk))
    const2 = lambda shape: pl.BlockSpec(shape, lambda d, j: (0,) * len(shape))
    dirc = lambda shape: pl.BlockSpec((None,) + shape, lambda d, j: (d,) + (0,) * len(shape))
    body = functools.partial(_scan_body, n_ctx_chunks=ncc, n_chunks=n_chunks)
    return pl.pallas_call(
        body,
        grid=(2, n_chunks),
        in_specs=[
            pl.BlockSpec((SCAN_Q, SSD_CONV_CH), lambda d, j: (chunk(d, j), 0)),
            pl.BlockSpec((HALO, SSD_CONV_CH), lambda d, j: (jnp.maximum(chunk(d, j) * per - 1, 0), 0)),
            pl.BlockSpec((HALO, SSD_CONV_CH), lambda d, j: (jnp.minimum((chunk(d, j) + 1) * per, last_halo), 0)),
            pl.BlockSpec((SCAN_Q, 512), lambda d, j: (chunk(d, j), 2 + d)),
            pl.BlockSpec((SCAN_Q, 512), col512(4)),
            pl.BlockSpec((SCAN_Q, 512), col512(6)),
            pl.BlockSpec((SCAN_Q, 512), lambda d, j: (chunk(d, j), 8 + d)),
            const2((CONV_W, SSD_CONV_CH)), const2((1, SSD_CONV_CH)),
            dirc((1, SSD_W)), dirc((1, SSD_W)), dirc((1, SSD_W)), const2((1, HG_W)),
            dirc(d_hg.shape[1:]), dirc(d_ssd.shape[1:]), dirc(pairs.shape[1:]), dirc(laters.shape[1:]),
            const2((LANE, SSD_W)),
        ],
        out_specs=pl.BlockSpec((None, SCAN_Q, MIX_W), lambda d, j: (d, chunk(d, j), 0)),
        out_shape=jax.ShapeDtypeStruct((2, n_rows, MIX_W), jnp.float32),
        scratch_shapes=[pltpu.VMEM((SSD_GROUPS, SSD_STATE, SSD_W // SSD_GROUPS), jnp.float32),
                        pltpu.VMEM((HG_HEADS, HG_VDIM, HG_EXPAND), jnp.float32),
                        pltpu.VMEM((SCAN_Q, SSD_W), jnp.float32)],
        compiler_params=pltpu.CompilerParams(dimension_semantics=("arbitrary", "arbitrary"),
                                             vmem_limit_bytes=MOE_VMEM_LIMIT),
        name="odd_scan",
    )(p_all, p_all, p_all, p_all, p_all, p_all, p_all,
      conv_w, conv_b.reshape(1, -1), dtb, a_cols, dsk, lb.reshape(1, -1),
      jnp.asarray(d_hg, bf), jnp.asarray(d_ssd, bf), jnp.asarray(pairs), jnp.asarray(laters),
      jnp.asarray(sel, bf))


def _group_rms(v, width):
    parts = []
    for lo in range(0, v.shape[1], width):
        seg = v[:, lo:lo + width]
        parts.append(seg * lax.rsqrt(jnp.mean(seg * seg, axis=-1, keepdims=True) + EPS))
    return jnp.concatenate(parts, axis=1)


def _odd_merge_body(yo_ref, z_ref, g_ref, x_ref, sn_ref, hn_ref, gt_ref, w_ref, o_ref):
    yo = yo_ref[0] + yo_ref[1]
    z = z_ref[...]
    g = g_ref[...]
    ys = _group_rms(yo[:, :SSD_W] * (z * jax.nn.sigmoid(z)), SSD_W // SSD_GROUPS) * sn_ref[...]
    hs = _group_rms(yo[:, SSD_W:], HG_VDIM) * hn_ref[...] * (g * jax.nn.sigmoid(g))
    m = jnp.concatenate([ys, hs], axis=1).astype(jnp.bfloat16)
    o_ref[...] = x_ref[...] + gt_ref[...] * jnp.dot(m, w_ref[...].astype(jnp.bfloat16),
                                                    preferred_element_type=jnp.float32)


ODD_TM = 256


def _odd_merge(yo, p_all, x_all, ssd_norm, hg_norm, gt, w_out, n_ctx):
    n_all, d = x_all.shape
    n_lat = n_all - n_ctx
    skip = n_ctx // ODD_TM
    return pl.pallas_call(
        _odd_merge_body,
        grid=(n_lat // ODD_TM,),
        in_specs=[pl.BlockSpec((2, ODD_TM, MIX_W), lambda i: (0, i + skip, 0)),
                  pl.BlockSpec((ODD_TM, 512), lambda i: (i + skip, 5)),
                  pl.BlockSpec((ODD_TM, 512), lambda i: (i + skip, 7)),
                  pl.BlockSpec((ODD_TM, d), lambda i: (i + skip, 0)),
                  pl.BlockSpec((1, SSD_W), lambda i: (0, 0)),
                  pl.BlockSpec((1, HG_W), lambda i: (0, 0)),
                  pl.BlockSpec((1, d), lambda i: (0, 0)),
                  pl.BlockSpec((MIX_W, d), lambda i: (0, 0))],
        out_specs=pl.BlockSpec((ODD_TM, d), lambda i: (i, 0)),
        out_shape=jax.ShapeDtypeStruct((n_lat, d), jnp.float32),
        compiler_params=pltpu.CompilerParams(dimension_semantics=("arbitrary",)),
        name="odd_merge",
    )(yo, p_all, p_all, x_all, ssd_norm.reshape(1, -1), hg_norm.reshape(1, -1), gt.reshape(1, -1), w_out)


def odd_layer(x_all, n_ctx, g, mod, mod_c, lb, w_in, conv_w, conv_b, dt_bias, a_log, d_skip, ssd_norm, hg_norm, gt,
              w_out):
    f32 = jnp.float32
    o_dt = SSD_CONV_CH
    o_f = o_dt + 2 * SSD_HEADS
    rep = lambda v: jnp.repeat(v, SSD_HEAD_DIM, axis=-1)
    w_perm = jnp.concatenate([w_in[:, :o_dt], w_in[:, o_f:], rep(w_in[:, o_dt:o_dt + SSD_HEADS]),
                              rep(w_in[:, o_dt + SSD_HEADS:o_f])], axis=1).astype(jnp.bfloat16)
    both = lambda a, b: jnp.stack([a.reshape(-1), b.reshape(-1)])
    p_all = norm_proj(x_all, g, both(mod_c[0], mod[0]), both(mod_c[1], mod[1]), w_perm, n_first=n_ctx)
    dtb = rep(dt_bias.astype(f32)).reshape(2, 1, SSD_W)
    a_cols = rep(-jnp.exp(a_log.astype(f32)) * math.log2(math.e)).reshape(2, 1, SSD_W)
    dsk = rep(d_skip.astype(f32)).reshape(2, 1, SSD_W)
    yo = _odd_scan(p_all, conv_w, conv_b, dtb, a_cols, dsk, lb.astype(f32), n_ctx)
    return _odd_merge(yo, p_all, x_all, ssd_norm, hg_norm, gt, w_out, n_ctx)


MOE_TM = 256
MOE_BM = 256
NEG_BIG = -1e30
MOE_ISSUE_UNROLL = 4
MOE_VMEM_LIMIT = 52 * 1024 * 1024


def _route_body(x_ref, g_ref, sh_ref, sc_ref, rw_ref, rb_ref, cnt0_ref,
                h_ref, idx_ref, gate_ref, rank_ref, cnt_ref, run_ref):
    i = pl.program_id(0)

    @pl.when(i == 0)
    def _():
        run_ref[...] = cnt0_ref[...]

    tm = x_ref.shape[0]
    x = x_ref[...]
    t = (x * lax.rsqrt(jnp.mean(x * x, axis=-1, keepdims=True) + EPS) * g_ref[...]) * (1.0 + sc_ref[...]) + sh_ref[...]
    h_ref[...] = t
    logits = jnp.dot(t.astype(jnp.bfloat16), rw_ref[...].astype(jnp.bfloat16),
                     preferred_element_type=jnp.float32) + rb_ref[...]
    lane = lax.broadcasted_iota(jnp.int32, (tm, LANE), 1)
    lane_f = lane.astype(jnp.float32)
    work = logits
    vals, sels, hots = [], [], []
    for _ in range(TOP_K):
        m = jnp.max(work, axis=-1, keepdims=True)
        sel = jnp.min(jnp.where(work == m, lane_f, float(LANE)), axis=-1, keepdims=True)
        hot = lane_f == sel
        vals.append(m)
        sels.append(sel.astype(jnp.int32))
        hots.append(hot)
        work = jnp.where(hot, -jnp.inf, work)
    exps = [jnp.exp(v - vals[0]) for v in vals]
    denom = exps[0] + exps[1] + exps[2] + exps[3]
    chosen = jnp.zeros((tm, LANE), jnp.float32)
    for hot in hots:
        chosen = chosen + hot.astype(jnp.float32)
    row = lax.broadcasted_iota(jnp.int32, (tm, tm), 0)
    col = lax.broadcasted_iota(jnp.int32, (tm, tm), 1)
    tri = (row > col).astype(jnp.bfloat16)
    before = jnp.dot(tri, chosen.astype(jnp.bfloat16), preferred_element_type=jnp.float32) + run_ref[0:1, :]
    idx_out = jnp.zeros((tm, LANE), jnp.int32)
    gate_out = jnp.zeros((tm, LANE), jnp.float32)
    rank_out = jnp.zeros((tm, LANE), jnp.int32)
    for k in range(TOP_K):
        rank_k = jnp.sum(jnp.where(hots[k], before, 0.0), axis=-1, keepdims=True).astype(jnp.int32)
        idx_out = jnp.where(lane == k, sels[k], idx_out)
        gate_out = jnp.where(lane == k, exps[k] / denom, gate_out)
        rank_out = jnp.where(lane == k, rank_k, rank_out)
    idx_ref[...] = idx_out
    gate_ref[...] = gate_out
    rank_ref[...] = rank_out
    run_new = run_ref[0:1, :] + jnp.sum(chosen, axis=0, keepdims=True)
    run_ref[...] = jnp.broadcast_to(run_new, run_ref.shape)
    cnt_ref[...] = jnp.broadcast_to(run_new, cnt_ref.shape)


def _moe_route(x2d, g, shift, scale, router_w, router_b, cnt0):
    n_tok, d = x2d.shape
    f32 = jnp.float32
    rw = jnp.pad(router_w, ((0, 0), (0, LANE - N_EXPERTS)))
    rb = jnp.pad(router_b.astype(f32), (0, LANE - N_EXPERTS), constant_values=NEG_BIG).reshape(1, LANE)
    tile = pl.BlockSpec((MOE_TM, LANE), lambda i: (i, 0))
    wide = pl.BlockSpec((MOE_TM, d), lambda i: (i, 0))
    vec = pl.BlockSpec((1, d), lambda i: (0, 0))
    small = pl.BlockSpec((SUBLANE, LANE), lambda i: (0, 0))
    row = lambda v: v.astype(f32).reshape(1, d)
    h, idx, gate, rank, cnt = pl.pallas_call(
        _route_body,
        grid=(n_tok // MOE_TM,),
        in_specs=[wide, vec, vec, vec, pl.BlockSpec((d, LANE), lambda i: (0, 0)),
                  pl.BlockSpec((1, LANE), lambda i: (0, 0)), small],
        out_specs=[wide, tile, tile, tile, small],
        out_shape=[jax.ShapeDtypeStruct((n_tok, d), f32),
                   jax.ShapeDtypeStruct((n_tok, LANE), jnp.int32),
                   jax.ShapeDtypeStruct((n_tok, LANE), f32),
                   jax.ShapeDtypeStruct((n_tok, LANE), jnp.int32),
                   jax.ShapeDtypeStruct((SUBLANE, LANE), f32)],
        scratch_shapes=[pltpu.VMEM((SUBLANE, LANE), f32)],
        compiler_params=pltpu.CompilerParams(dimension_semantics=("arbitrary",)),
        name="moe_route",
    )(x2d, row(g), row(shift), row(scale), rw, rb, cnt0)
    return h, idx[:, :TOP_K], gate, rank[:, :TOP_K], cnt


def _rows_to_tiles(x):
    return _transpose8([x[:, LANE * s:LANE * (s + 1)] for s in range(SUBLANE)])


def _store_tiles(ref, tiles):
    groups = ref.shape[0] // SUBLANE
    for j in range(SUBLANE):
        ref[pl.ds(j, groups, stride=SUBLANE)] = tiles[j].reshape(groups, SUBLANE, LANE)


def _load_rows(ref):
    groups = ref.shape[0] // SUBLANE
    tiles = [ref[pl.ds(j, groups, stride=SUBLANE)].reshape(groups * SUBLANE, LANE) for j in range(SUBLANE)]
    return jnp.concatenate(_transpose8(tiles), axis=1)


def _scatter_body(off_ref, pad_ref, dest_ref, *rest, tile_starts):
    t_refs = rest[:-4]
    xs_ref, rows_ref, zero_ref, sem = rest[-4:]
    i = pl.program_id(0)
    tm = t_refs[0].shape[0]

    @pl.when(i == 0)
    def _():
        zero_ref[...] = jnp.zeros_like(zero_ref)
        used = off_ref[N_EXPERTS - 1] + pad_ref[N_EXPERTS - 1]
        n_rows = xs_ref.shape[0]

        def zero_block(start):
            return pltpu.make_async_copy(zero_ref, xs_ref.at[pl.ds(start, MOE_BM)], sem)

        for e in range(N_EXPERTS):
            tail = n_rows - (e + 1) * MOE_BM

            @pl.when(pad_ref[e] > 0)
            def _():
                zero_block(off_ref[e] + pad_ref[e] - MOE_BM).start()

            @pl.when(tail >= used)
            def _():
                zero_block(tail).start()
        for e in range(N_EXPERTS):
            tail = n_rows - (e + 1) * MOE_BM

            @pl.when(pad_ref[e] > 0)
            def _():
                zero_block(0).wait()

            @pl.when(tail >= used)
            def _():
                zero_block(0).wait()

    for s, t_ref in enumerate(t_refs):
        @pl.when((i >= tile_starts[s]) & (i < tile_starts[s + 1]))
        def _():
            _store_tiles(rows_ref, _rows_to_tiles(t_ref[...]))

    def issue(t, carry):
        for k in range(TOP_K):
            pltpu.make_async_copy(rows_ref.at[t], xs_ref.at[dest_ref[TOP_K * t + k]], sem).start(priority=k % 2)
        return carry

    lax.fori_loop(0, tm, issue, 0, unroll=MOE_ISSUE_UNROLL)
    for _ in range(TOP_K):
        pltpu.make_async_copy(rows_ref, xs_ref.at[pl.ds(0, tm)], sem).wait()


def _stream_tiles(streams):
    starts = [0]
    for t in streams:
        starts.append(starts[-1] + t.shape[0] // MOE_TM)
    return tuple(starts)


def _stream_spec(width, starts, s, extra=0):
    lo, n = starts[s], starts[s + 1] - starts[s]
    return pl.BlockSpec((MOE_TM, width), lambda i, *_: (jnp.clip(i + extra - lo, 0, n - 1), 0))


def _moe_scatter(hs, dest_flat, off, padded, n_rows):
    d = hs[0].shape[1]
    assert d == SUBLANE * LANE
    starts = _stream_tiles(hs)
    in_specs = [pl.BlockSpec((TOP_K * MOE_TM,), lambda i, off, pad: (i,), memory_space=pltpu.SMEM)]
    in_specs += [_stream_spec(d, starts, s) for s in range(len(hs))]
    return pl.pallas_call(
        functools.partial(_scatter_body, tile_starts=starts),
        grid_spec=pltpu.PrefetchScalarGridSpec(
            num_scalar_prefetch=2,
            grid=(starts[-1],),
            in_specs=in_specs,
            out_specs=pl.BlockSpec(memory_space=pl.ANY),
            scratch_shapes=[pltpu.VMEM((MOE_TM, SUBLANE, LANE), jnp.float32),
                            pltpu.VMEM((MOE_BM, SUBLANE, LANE), jnp.float32), pltpu.SemaphoreType.DMA],
        ),
        out_shape=jax.ShapeDtypeStruct((n_rows, SUBLANE, LANE), jnp.float32),
        compiler_params=pltpu.CompilerParams(dimension_semantics=("arbitrary",)),
        name="moe_scatter",
    )(off, padded, dest_flat, *hs)


def _expert_body(blk_e_ref, n_act_ref, nxt_e_ref, x_ref, wgu_hbm, bgu_ref, wdn_hbm, bdn_ref, y_ref,
                 wgu_f32, wdn_f32, wgu_bf, wdn_bf, sems, slot_ref, *, layer):
    i = pl.program_id(0)

    def weight_copies(e, slot):
        return (pltpu.make_async_copy(wgu_hbm.at[layer, e], wgu_f32.at[slot], sems.at[slot, 0]),
                pltpu.make_async_copy(wdn_hbm.at[layer, e], wdn_f32.at[slot], sems.at[slot, 1]))

    @pl.when(i < n_act_ref[0])
    def _():
        e = blk_e_ref[i]
        prev = blk_e_ref[jnp.maximum(i - 1, 0)]

        @pl.when((i == 0) | (e != prev))
        def _():
            @pl.when(i == 0)
            def _():
                slot_ref[0] = 0
                for c in weight_copies(e, 0):
                    c.start()

            slot = slot_ref[0]
            for c in weight_copies(e, slot):
                c.wait()
            wgu_bf[...] = wgu_f32[slot].astype(jnp.bfloat16)
            wdn_bf[...] = wdn_f32[slot].astype(jnp.bfloat16)
            nxt = nxt_e_ref[e]

            @pl.when(nxt >= 0)
            def _():
                for c in weight_copies(nxt, 1 - slot):
                    c.start()

            slot_ref[0] = 1 - slot

        x = _load_rows(x_ref).astype(jnp.bfloat16)
        gu = jnp.dot(x, wgu_bf[...], preferred_element_type=jnp.float32) + bgu_ref[...]
        gate = jnp.minimum(gu[:, :D_EXPERT], SWIGLU_LIMIT)
        up = jnp.clip(gu[:, D_EXPERT:], -SWIGLU_LIMIT, SWIGLU_LIMIT)
        act = (up + 1.0) * gate * jax.nn.sigmoid(SWIGLU_ALPHA * gate)
        y = jnp.dot(act.astype(jnp.bfloat16), wdn_bf[...], preferred_element_type=jnp.float32) + bdn_ref[...]
        _store_tiles(y_ref, _rows_to_tiles(y))

    @pl.when(i >= n_act_ref[0])
    def _():
        y_ref[...] = jnp.zeros_like(y_ref)


def _moe_experts(xs, blk_e, n_act, nxt_e, w_gu, b_gu, w_dn, b_dn, layer):
    n_rows = xs.shape[0]
    d = SUBLANE * LANE
    n_blk = n_rows // MOE_BM

    def blk(i, be, na, nx):
        return jnp.minimum(i, na[0] - 1)

    return pl.pallas_call(
        functools.partial(_expert_body, layer=layer),
        grid_spec=pltpu.PrefetchScalarGridSpec(
            num_scalar_prefetch=3,
            grid=(n_blk,),
            in_specs=[pl.BlockSpec((MOE_BM, SUBLANE, LANE), lambda i, be, na, nx: (blk(i, be, na, nx), 0, 0)),
                      pl.BlockSpec(memory_space=pl.ANY),
                      pl.BlockSpec((None, 1, 2 * D_EXPERT), lambda i, be, na, nx: (be[blk(i, be, na, nx)], 0, 0)),
                      pl.BlockSpec(memory_space=pl.ANY),
                      pl.BlockSpec((None, 1, d), lambda i, be, na, nx: (be[blk(i, be, na, nx)], 0, 0))],
            out_specs=pl.BlockSpec((MOE_BM, SUBLANE, LANE), lambda i, be, na, nx: (i, 0, 0)),
            scratch_shapes=[pltpu.VMEM((2, d, 2 * D_EXPERT), jnp.float32),
                            pltpu.VMEM((2, D_EXPERT, d), jnp.float32),
                            pltpu.VMEM((d, 2 * D_EXPERT), jnp.bfloat16),
                            pltpu.VMEM((D_EXPERT, d), jnp.bfloat16),
                            pltpu.SemaphoreType.DMA((2, 2)),
                            pltpu.SMEM((1,), jnp.int32)],
        ),
        out_shape=jax.ShapeDtypeStruct((n_rows, SUBLANE, LANE), jnp.float32),
        compiler_params=pltpu.CompilerParams(dimension_semantics=("arbitrary",),
                                             vmem_limit_bytes=MOE_VMEM_LIMIT),
        name="moe_experts",
    )(blk_e, n_act, nxt_e, xs, w_gu, b_gu.reshape(N_EXPERTS, 1, -1), w_dn, b_dn.reshape(N_EXPERTS, 1, -1))


def _combine_body(dest_ref, dest_nxt_ref, *rest, tile_starts):
    n_streams = len(tile_starts) - 1
    ins, ys_ref = rest[:3 * n_streams], rest[3 * n_streams]
    y_refs = rest[3 * n_streams + 1:-2]
    if len(y_refs) == 1:
        y_refs = y_refs * n_streams
    buf_ref, sems = rest[-2:]
    i = pl.program_id(0)
    n = pl.num_programs(0)
    tm = y_refs[0].shape[0]

    def fetch(d_ref, slot):
        def issue(t, carry):
            for k in range(TOP_K):
                pltpu.make_async_copy(ys_ref.at[d_ref[TOP_K * t + k]], buf_ref.at[slot, k, t],
                                      sems.at[slot]).start(priority=k % 2)
            return carry
        lax.fori_loop(0, tm, issue, 0, unroll=MOE_ISSUE_UNROLL)

    @pl.when(i == 0)
    def _():
        fetch(dest_ref, 0)

    @pl.when(i + 1 < n)
    def _():
        fetch(dest_nxt_ref, (i + 1) % 2)

    slot = i % 2
    for k in range(TOP_K):
        pltpu.make_async_copy(ys_ref.at[pl.ds(0, tm)], buf_ref.at[slot, k], sems.at[slot]).wait()
    for s in range(n_streams):
        gate_ref, x_ref, gt_ref = ins[3 * s:3 * s + 3]

        @pl.when((i >= tile_starts[s]) & (i < tile_starts[s + 1]))
        def _():
            g = gate_ref[...]
            groups = tm // SUBLANE
            gk = [jnp.broadcast_to(g[:, k:k + 1], (tm, LANE)).reshape(groups, SUBLANE, LANE) for k in range(TOP_K)]
            accs = []
            for j in range(SUBLANE):
                acc = None
                for k in range(TOP_K):
                    splat = jnp.broadcast_to(gk[k][:, j:j + 1, :], (groups, SUBLANE, LANE))
                    term = splat * buf_ref.at[slot, k][pl.ds(j, groups, stride=SUBLANE)]
                    acc = term if acc is None else acc + term
                accs.append(acc.reshape(tm, LANE))
            y_refs[s][...] = x_ref[...] + gt_ref[...] * jnp.concatenate(_transpose8(accs), axis=1)


def _moe_combine(ys, dest_flat, gates, xs2d, gts, joint):
    d = xs2d[0].shape[1]
    starts = _stream_tiles(xs2d)
    n_tiles = starts[-1]
    in_specs = [pl.BlockSpec((TOP_K * MOE_TM,), lambda i: (i,), memory_space=pltpu.SMEM),
                pl.BlockSpec((TOP_K * MOE_TM,), lambda i: (jnp.minimum(i + 1, n_tiles - 1),),
                             memory_space=pltpu.SMEM)]
    args = [dest_flat, dest_flat]
    for s, (gate, x2d, gt) in enumerate(zip(gates, xs2d, gts)):
        in_specs += [_stream_spec(LANE, starts, s), _stream_spec(d, starts, s), pl.BlockSpec((1, d), lambda i: (0, 0))]
        args += [gate, x2d, gt.astype(jnp.float32).reshape(1, d)]
    in_specs.append(pl.BlockSpec(memory_space=pl.ANY))
    args.append(ys)
    if joint:
        out_specs = [pl.BlockSpec((MOE_TM, d), lambda i: (i, 0))]
        out_shape = [jax.ShapeDtypeStruct((n_tiles * MOE_TM, d), jnp.float32)]
    else:
        out_specs = [_stream_spec(d, starts, s) for s in range(len(xs2d))]
        out_shape = [jax.ShapeDtypeStruct(x2d.shape, jnp.float32) for x2d in xs2d]
    return pl.pallas_call(
        functools.partial(_combine_body, tile_starts=starts),
        grid=(n_tiles,),
        in_specs=in_specs,
        out_specs=out_specs,
        out_shape=out_shape,
        scratch_shapes=[pltpu.VMEM((2, TOP_K, MOE_TM, SUBLANE, LANE), jnp.float32),
                        pltpu.SemaphoreType.DMA((2,))],
        compiler_params=pltpu.CompilerParams(dimension_semantics=("arbitrary",)),
        name="moe_combine",
    )(*args)


def moe_layer(streams, g, router_w, router_b, w_gu, b_gu, w_dn, b_dn, layer, joint=False):
    i32 = jnp.int32
    routed = []
    cnt = jnp.zeros((SUBLANE, LANE), jnp.float32)
    for x2d, shift, scale, _ in streams:
        assert x2d.shape[0] % MOE_TM == 0
        h, idx, gate, rank, cnt = _moe_route(x2d, g, shift, scale, router_w, router_b, cnt)
        routed.append((h, idx, gate, rank))
    counts = cnt[0, :N_EXPERTS].astype(i32)
    padded = (counts + MOE_BM - 1) // MOE_BM * MOE_BM
    pad_end = jnp.cumsum(padded)
    off = (pad_end - padded).astype(i32)
    padded = padded.astype(i32)
    n_tok = sum(s[0].shape[0] for s in streams)
    n_blk = -(-(n_tok * TOP_K) // MOE_BM) + N_EXPERTS
    blk_e = jnp.minimum(jnp.sum(jnp.arange(n_blk)[:, None] * MOE_BM >= pad_end[None, :], axis=1),
                        N_EXPERTS - 1).astype(i32)
    n_act = (pad_end[-1:] // MOE_BM).astype(i32)
    experts = jnp.arange(N_EXPERTS, dtype=i32)
    later_with_rows = (padded > 0)[None, :] & (experts[None, :] > experts[:, None])
    nxt_e = jnp.min(jnp.where(later_with_rows, experts[None, :], N_EXPERTS), axis=1)
    nxt_e = jnp.where(nxt_e == N_EXPERTS, -1, nxt_e).astype(i32)
    dests = []
    for _, idx, _, rank in routed:
        dest = rank + jnp.sum(jnp.where(idx[..., None] == experts, off, 0), axis=-1)
        dests.append(dest.reshape(-1).astype(i32))
    dest_flat = jnp.concatenate(dests)
    xs = _moe_scatter([r[0] for r in routed], dest_flat, off, padded, n_blk * MOE_BM)
    ys = _moe_experts(xs, blk_e, n_act, nxt_e, w_gu, b_gu, w_dn, b_dn, layer)
    return _moe_combine(ys, dest_flat, [r[2] for r in routed], [s[0] for s in streams], [s[3] for s in streams],
                        joint)


def kernel(x, c, ctx, c_ctx, norm_g, ada_w, ada_b, w_out, w_in_even, hy_conv_w, hy_conv_b,
           hy_w1, hy_b1, hy_w2, hy_b2, hy_w3, hy_b3, hy_w4, hy_freq, hy_filter_bias,
           att_q_norm, att_k_norm, att_sink, w_in_odd, ssd_conv_w, ssd_conv_b, ssd_dt_bias,
           ssd_A_log, ssd_D, ssd_norm, hg_lower_bounds, hg_norm, router_w, router_b,
           moe_w_gu, moe_b_gu, moe_w_dn, moe_b_dn):
    lbs = jax.nn.softmax(hg_lower_bounds.astype(jnp.float32), axis=0)
    lbs = jnp.cumsum(lbs, axis=0) - lbs[0]
    xc = ctx
    n_ctx = ctx.shape[1]
    for layer in range(DEPTH):
        ctx_out = layer < DEPTH - 1
        i = layer // 2
        sh, sc, gt = adaln(c, ada_w[layer], ada_b[layer], 0)
        sh_c, sc_c, gt_c = adaln(c_ctx, ada_w[layer], ada_b[layer], 0)
        assert (layer % 2 == 0) == ctx_out
        moe = functools.partial(moe_layer, g=norm_g[layer, 1], router_w=router_w[layer], router_b=router_b[layer],
                                w_gu=moe_w_gu, b_gu=moe_b_gu[layer], w_dn=moe_w_dn, b_dn=moe_b_dn[layer], layer=layer)
        sh2, sc2, gt2 = adaln(c, ada_w[layer], ada_b[layer], 1)
        if layer % 2 == 0:
            x, xc = even_layer(x, xc, norm_g[layer, 0], (sh, sc), (sh_c, sc_c), w_in_even[i], hy_conv_w[i],
                               hy_conv_b[i], hy_w1[i], hy_b1[i], hy_w2[i], hy_b2[i], hy_w3[i], hy_b3[i], hy_w4[i],
                               hy_freq[i], hy_filter_bias[i], att_q_norm[i], att_k_norm[i], att_sink[i], gt, gt_c,
                               w_out[layer])
            sh2_c, sc2_c, gt2_c = adaln(c_ctx, ada_w[layer], ada_b[layer], 1)
            x_all = moe([(xc[0], sh2_c, sc2_c, gt2_c), (x[0], sh2, sc2, gt2)], joint=True)[0]
        else:
            x = odd_layer(x_all, n_ctx, norm_g[layer, 0], (sh, sc), (sh_c, sc_c), lbs[layer], w_in_odd[i],
                          ssd_conv_w[i], ssd_conv_b[i], ssd_dt_bias[i], ssd_A_log[i], ssd_D[i], ssd_norm[i],
                          hg_norm[i], gt, w_out[layer])
            x = moe([(x, sh2, sc2, gt2)])[0][None]
    return x
```

```python
import functools
import math

import jax
import jax.numpy as jnp
from jax import lax
from jax.experimental import pallas as pl
from jax.experimental.pallas import tpu as pltpu

D_MODEL = 1024
DEPTH = 2
GRID_W = 64
MIX_W = D_MODEL
EPS = 1e-6
CONV_W = 3

HY_W = MIX_W // 2
HY_ORDER = 2
HY_EMB = 33
HY_FFN = 64
HY_TARGET = 1e-2
HY_SHORT_PCT = 0.3
HY_LONG_PCT = 1.5

HEAD_DIM = 64
ATT_HEADS = (MIX_W // 2) // HEAD_DIM
ATT_KV_HEADS = 2
ATT_GROUP = ATT_HEADS // ATT_KV_HEADS
ATT_WINDOW = 128
ATT_BLOCK = 128
ROPE_BASE = 10000.0
ATT_Q_W = ATT_HEADS * HEAD_DIM
ATT_KV_W = ATT_KV_HEADS * HEAD_DIM

SSD_W = MIX_W // 2
SSD_HEAD_DIM = 64
SSD_HEADS = SSD_W // SSD_HEAD_DIM
SSD_GROUPS = 2
SSD_STATE = 128
SSD_CHUNK = 128
SSD_CONV_CH = SSD_W + 2 * SSD_GROUPS * SSD_STATE

HG_W = MIX_W // 2
HG_EXPAND = 128
HG_HEADS = HG_W // HG_EXPAND
HG_VDIM = HG_W // HG_HEADS
HG_CHUNK = 64

N_EXPERTS = 32
TOP_K = 4
D_EXPERT = D_MODEL
SWIGLU_ALPHA = 1.702
SWIGLU_LIMIT = 7.0
MOE_BLOCK = 128

EVEN_IN = 2 * ATT_KV_W + ATT_Q_W + 3 * HY_W
ODD_STATE_COLS = SSD_CONV_CH + 2 * SSD_HEADS + 3 * HG_W
ODD_IN = ODD_STATE_COLS + SSD_W + 2 * HG_W

LANE = 128
SUBLANE = 8


def _mm_body(a_ref, b_ref, o_ref):
    a = a_ref[...].astype(jnp.bfloat16)
    b = b_ref[...].astype(jnp.bfloat16)
    o_ref[...] = jnp.dot(a, b, preferred_element_type=jnp.float32)


def _pick_tile(n, candidates):
    for c in candidates:
        if n % c == 0:
            return c
    return n


def pmm(a, b):
    m, k = a.shape
    n = b.shape[1]
    n_pad = -(-n // LANE) * LANE
    if n_pad != n:
        b = jnp.pad(b, ((0, 0), (0, n_pad - n)))
    m_pad = -(-m // SUBLANE) * SUBLANE
    if m_pad != m:
        a = jnp.pad(a, ((0, m_pad - m), (0, 0)))
    tm = _pick_tile(m_pad, (512, 256, 128, 64, 32, 16, 8))
    tn = _pick_tile(n_pad, (512, 384, 256, 128))
    out = pl.pallas_call(
        _mm_body,
        grid=(m_pad // tm, n_pad // tn),
        in_specs=[pl.BlockSpec((tm, k), lambda i, j: (i, 0)),
                  pl.BlockSpec((k, tn), lambda i, j: (0, j))],
        out_specs=pl.BlockSpec((tm, tn), lambda i, j: (i, j)),
        out_shape=jax.ShapeDtypeStruct((m_pad, n_pad), jnp.float32),
        name="dense_mm",
    )(a, b)
    return out[:m, :n]


PROJ_TM = 256
PROJ_VMEM_LIMIT = 56 * 1024 * 1024


def _norm_proj_body(x_ref, g_ref, sh_ref, sc_ref, w_ref, o_ref):
    x = x_ref[...]
    y = x * lax.rsqrt(jnp.mean(x * x, axis=-1, keepdims=True) + EPS) * g_ref[...]
    h = y * (1.0 + sc_ref[...]) + sh_ref[...]
    o_ref[...] = jnp.dot(h.astype(jnp.bfloat16), w_ref[...], preferred_element_type=jnp.float32)


def norm_proj(x2d, g, shift, scale, w_bf16, n_first=0):
    rows, d = x2d.shape
    n = w_bf16.shape[1]
    assert rows % PROJ_TM == 0 and n_first % PROJ_TM == 0 and n % LANE == 0
    first_tiles = n_first // PROJ_TM
    mod = pl.BlockSpec((None, 1, d), lambda i: (jnp.where(i < first_tiles, 0, 1), 0, 0))
    return pl.pallas_call(
        _norm_proj_body,
        grid=(rows // PROJ_TM,),
        in_specs=[pl.BlockSpec((PROJ_TM, d), lambda i: (i, 0)),
                  pl.BlockSpec((1, d), lambda i: (0, 0)), mod, mod,
                  pl.BlockSpec((d, n), lambda i: (0, 0))],
        out_specs=pl.BlockSpec((PROJ_TM, n), lambda i: (i, 0)),
        out_shape=jax.ShapeDtypeStruct((rows, n), jnp.float32),
        compiler_params=pltpu.CompilerParams(dimension_semantics=("arbitrary",),
                                             vmem_limit_bytes=PROJ_VMEM_LIMIT),
        name="norm_proj",
    )(x2d, g.astype(jnp.float32).reshape(1, d), shift.astype(jnp.float32).reshape(2, 1, d),
      scale.astype(jnp.float32).reshape(2, 1, d), w_bf16)


def pmm3(a, b):
    lead = a.shape[:-1]
    return pmm(a.reshape(-1, a.shape[-1]), b).reshape(*lead, b.shape[-1])


def rms_norm(x, g):
    xf = x.astype(jnp.float32)
    y = xf * lax.rsqrt(jnp.mean(xf * xf, axis=-1, keepdims=True) + EPS)
    return (y * g.astype(jnp.float32)).astype(x.dtype)


def modulate(h, shift, scale):
    return h * (1.0 + scale) + shift


def adaln(cond, w, b, j):
    lo, hi = 3 * j * D_MODEL, 3 * (j + 1) * D_MODEL
    m = jax.nn.silu(cond) @ w[:, lo:hi] + b[lo:hi]
    return jnp.split(m, 3, axis=-1)


def dwconv_centred(u, w, b):
    ch = u.shape[-1]
    y = lax.conv_general_dilated(u, w[:, None, :].astype(u.dtype), window_strides=(1,),
                                 padding=[(CONV_W // 2, CONV_W // 2)],
                                 dimension_numbers=('NWC', 'WIO', 'NWC'), feature_group_count=ch)
    return y + b.astype(u.dtype)


def axial_rope_tables(length):
    rows = length // GRID_W
    n_pairs = HEAD_DIM // 4
    inv = ROPE_BASE ** (-jnp.arange(n_pairs, dtype=jnp.float32) / n_pairs)
    row_ang = jnp.arange(rows, dtype=jnp.float32)[:, None] * inv
    col_ang = jnp.arange(GRID_W, dtype=jnp.float32)[:, None] * inv
    ang_r = jnp.broadcast_to(row_ang[:, None], (rows, GRID_W, n_pairs)).reshape(length, n_pairs)
    ang_c = jnp.broadcast_to(col_ang[None], (rows, GRID_W, n_pairs)).reshape(length, n_pairs)
    return jnp.cos(ang_r), jnp.sin(ang_r), jnp.cos(ang_c), jnp.sin(ang_c)


def _rotate(u, cos, sin):
    n = u.shape[-1] // 2
    u1, u2 = u[..., :n], u[..., n:]
    cos = cos[None, :, None, :]
    sin = sin[None, :, None, :]
    return jnp.concatenate([u1 * cos - u2 * sin, u1 * sin + u2 * cos], axis=-1)


def apply_axial_rope(u, tables):
    cr, sr, cc, sc = tables
    half = HEAD_DIM // 2
    return jnp.concatenate([_rotate(u[..., :half], cr, sr), _rotate(u[..., half:], cc, sc)], axis=-1)


def hyena_filters(length, w1, b1, w2, b2, w3, b3, w4, freq):
    f32 = jnp.float32
    t = jnp.linspace(0.0, 1.0, length, dtype=f32)[:, None]
    bands = (HY_EMB - 1) // 2
    w_ang = 2.0 * math.pi * jnp.arange(length, dtype=f32)[:, None] / length
    fr = jnp.linspace(1e-4, bands - 1, bands, dtype=f32)[None]
    z = jnp.concatenate([t, jnp.cos(fr * w_ang), -jnp.sin(fr * w_ang)], axis=-1)
    fq = freq.astype(f32)
    hdn = jnp.sin(fq * (z @ w1.astype(f32) + b1.astype(f32)))
    hdn = jnp.sin(fq * (hdn @ w2.astype(f32) + b2.astype(f32)))
    hdn = jnp.sin(fq * (hdn @ w3.astype(f32) + b3.astype(f32)))
    h = (hdn @ w4.astype(f32)).reshape(length, HY_ORDER, 2, HY_W)
    max_decay = math.log(HY_TARGET) / HY_SHORT_PCT
    min_decay = math.log(HY_TARGET) / HY_LONG_PCT
    deltas = jnp.abs(jnp.linspace(min_decay, max_decay, HY_W, dtype=f32))
    h = h * jnp.exp(-t * deltas)[:, None, None, :]
    h2 = jnp.concatenate([h[:, :, 0], jnp.zeros((1, HY_ORDER, HY_W), f32), h[:0:-1, :, 1]], axis=0)
    h2 = h2 / jnp.sum(jnp.abs(h2), axis=0, keepdims=True)
    return jnp.fft.rfft(h2, axis=0)


def hyena_mix(u, hf, filter_bias, conv_w, conv_b):
    length = u.shape[1]
    u = dwconv_centred(u.astype(jnp.float32), conv_w, conv_b)
    v, x1, x2 = jnp.split(u, 3, axis=-1)
    z = v
    for o, gate in enumerate((x1, x2)):
        zf = jnp.fft.rfft(z, n=2 * length, axis=1)
        zc = jnp.fft.irfft(zf * hf[None, :, o], n=2 * length, axis=1)[:, :length]
        z = gate * (zc + z * filter_bias[o].astype(jnp.float32))
    return z


def window_attention(q, k, v, k_c, v_c, sink):
    bsz, length = q.shape[:2]
    nb = length // ATT_BLOCK
    scale = HEAD_DIM ** -0.5
    qb = q.reshape(bsz, nb, ATT_BLOCK, ATT_KV_HEADS, ATT_GROUP, HEAD_DIM)
    pad = ((0, 0), (ATT_BLOCK, ATT_BLOCK), (0, 0), (0, 0))

    def band(a):
        ap = jnp.pad(a, pad).reshape(bsz, nb + 2, ATT_BLOCK, ATT_KV_HEADS, HEAD_DIM)
        return jnp.concatenate([ap[:, :-2], ap[:, 1:-1], ap[:, 2:]], axis=2)

    kw, vw = band(k), band(v)
    s_loc = jnp.einsum('bnqhgd,bnkhd->bnhgqk', qb, kw) * scale
    s_ctx = jnp.einsum('bnqhgd,bchd->bnhgqc', qb, k_c) * scale
    qpos = jnp.arange(nb)[:, None] * ATT_BLOCK + jnp.arange(ATT_BLOCK)[None]
    kpos = (jnp.arange(nb)[:, None] - 1) * ATT_BLOCK + jnp.arange(3 * ATT_BLOCK)[None]
    rel = kpos[:, None, :] - qpos[:, :, None]
    valid = (jnp.abs(rel) <= ATT_WINDOW) & (kpos[:, None, :] >= 0) & (kpos[:, None, :] < length)
    s_loc = jnp.where(valid[None, :, None, None], s_loc, -jnp.inf)
    sink_l = jnp.broadcast_to(sink.astype(jnp.float32).reshape(1, 1, ATT_KV_HEADS, ATT_GROUP, 1, 1),
                              s_loc.shape[:-1] + (1,))
    p = jax.nn.softmax(jnp.concatenate([s_loc, s_ctx, sink_l], axis=-1), axis=-1)
    n_loc = 3 * ATT_BLOCK
    n_ctx = k_c.shape[1]
    o = (jnp.einsum('bnhgqk,bnkhd->bnqhgd', p[..., :n_loc], vw)
         + jnp.einsum('bnhgqc,bchd->bnqhgd', p[..., n_loc:n_loc + n_ctx], v_c))
    return o.reshape(bsz, length, ATT_Q_W)


def context_attention(q_c, k_c, v_c, sink):
    bsz, n_ctx = q_c.shape[:2]
    s = jnp.einsum('bqhgd,bkhd->bhgqk', q_c, k_c) * HEAD_DIM ** -0.5
    sink_l = jnp.broadcast_to(sink.astype(jnp.float32).reshape(1, ATT_KV_HEADS, ATT_GROUP, 1, 1),
                              s.shape[:-1] + (1,))
    p = jax.nn.softmax(jnp.concatenate([s, sink_l], axis=-1), axis=-1)[..., :-1]
    return jnp.einsum('bhgqk,bkhd->bqhgd', p, v_c).reshape(bsz, n_ctx, ATT_Q_W)


def ssd_scan(x, dt, a, bm, cm, d_skip, init, need_y):
    bsz, length, n_heads, hd = x.shape
    nc = length // SSD_CHUNK
    hpg = n_heads // SSD_GROUPS
    da = (dt * a).reshape(bsz, nc, SSD_CHUNK, SSD_GROUPS, hpg)
    cs = jnp.cumsum(da, axis=2)
    xdt = (x * dt[..., None]).reshape(bsz, nc, SSD_CHUNK, SSD_GROUPS, hpg, hd)
    bc = bm.reshape(bsz, nc, SSD_CHUNK, SSD_GROUPS, SSD_STATE)
    cc = cm.reshape(bsz, nc, SSD_CHUNK, SSD_GROUPS, SSD_STATE)
    to_end = jnp.exp(cs[:, :, -1:] - cs)
    states = jnp.einsum('bcsgn,bcsgh,bcsghp->bcghpn', bc, to_end, xdt)
    chunk_decay = jnp.exp(cs[:, :, -1])

    def step(s, inp):
        st, dec = inp
        return s * dec[..., None, None] + st, s

    s_final, s_in = lax.scan(step, init, (jnp.moveaxis(states, 1, 0), jnp.moveaxis(chunk_decay, 1, 0)))
    if not need_y:
        return None, s_final
    s_in = jnp.moveaxis(s_in, 0, 1)
    cs_t = jnp.moveaxis(cs, 2, -1)
    diff = cs_t[..., :, None] - cs_t[..., None, :]
    lower = jnp.tril(jnp.ones((SSD_CHUNK, SSD_CHUNK), bool))
    decay = jnp.where(lower, jnp.exp(jnp.where(lower, diff, 0.0)), 0.0)
    scores = jnp.einsum('bclgn,bcsgn->bcgls', cc, bc)
    y_diag = jnp.einsum('bcgls,bcghls,bcsghp->bclghp', scores, decay, xdt)
    y_off = jnp.einsum('bclgn,bcghpn,bclgh->bclghp', cc, s_in, jnp.exp(cs))
    y = (y_diag + y_off).reshape(bsz, length, n_heads, hd) + d_skip[:, None] * x
    return y, s_final


def hgrn2_scan(q, k, v, g, init, need_o):
    bsz, length, n_heads, _ = k.shape
    nc = length // HG_CHUNK

    def chunks(a):
        return a.reshape(bsz, nc, HG_CHUNK, n_heads, a.shape[-1]).transpose(1, 0, 3, 2, 4)

    lower = jnp.tril(jnp.ones((HG_CHUNK, HG_CHUNK), bool))[:, :, None]

    def update(s, kc, vc, cum):
        last = cum[:, :, -1]
        return (s * jnp.exp(last)[..., None]
                + jnp.einsum('bhsk,bhsv->bhkv', kc * jnp.exp(last[:, :, None] - cum), vc))

    if not need_o:
        def step_state(s, inp):
            kc, vc, gc = inp
            return update(s, kc, vc, jnp.cumsum(gc, axis=2)), None
        s_final, _ = lax.scan(step_state, init, (chunks(k), chunks(v), chunks(g)))
        return None, s_final

    def step(s, inp):
        qc, kc, vc, gc = inp
        cum = jnp.cumsum(gc, axis=2)
        diff = cum[:, :, :, None, :] - cum[:, :, None, :, :]
        decay = jnp.where(lower, jnp.exp(jnp.where(lower, diff, 0.0)), 0.0)
        att = jnp.einsum('bhtk,bhsk,bhtsk->bhts', qc, kc, decay)
        o = (jnp.einsum('bhtk,bhkv->bhtv', qc * jnp.exp(cum), s)
             + jnp.einsum('bhts,bhsv->bhtv', att, vc))
        return update(s, kc, vc, cum), o

    s_final, o = lax.scan(step, init, (chunks(q), chunks(k), chunks(v), chunks(g)))
    return o.transpose(1, 0, 3, 2, 4).reshape(bsz, length, n_heads, v.shape[-1]), s_final


HY_N1 = 128
HY_N2 = 256
HY_NB = 8
HY_TM = 512


def _hy_dft_constants(length):
    import numpy as np
    n = 2 * length
    assert n == HY_N1 * HY_N2
    half = HY_N1 // 2
    k1 = np.arange(HY_N1)[:, None]
    n1 = np.arange(half)[None, :]
    n2 = np.arange(HY_N2)[:, None, None]
    ang = 2 * np.pi * (k1 * n1 / HY_N1)[None] + 2 * np.pi * n2 * k1[None] / n
    fwd = np.stack([np.cos(ang), -np.sin(ang)], axis=2).reshape(HY_N2, 2 * HY_N1, half)
    inv = fwd.transpose(0, 2, 1)
    a2 = 2 * np.pi * np.outer(np.arange(HY_N2), np.arange(HY_N2)) / HY_N2
    c, s = np.cos(a2), -np.sin(a2)
    m_fwd = np.concatenate([np.stack([c, -s], axis=2).reshape(HY_N2, 2 * HY_N2),
                            np.stack([s, c], axis=2).reshape(HY_N2, 2 * HY_N2)], axis=0)
    m_inv = np.stack([np.concatenate([c, s], axis=1), np.concatenate([-s, c], axis=1)],
                     axis=1).reshape(2 * HY_N2, 2 * HY_N2)
    f = np.float32
    return fwd.astype(f), inv.astype(f), m_fwd.astype(f), m_inv.astype(f)


def _hy_conv_body(p_ref, prev_ref, next_ref, w_ref, b_ref, o_ref):
    i = pl.program_id(1)
    n = pl.num_programs(1)
    u = p_ref[...]
    rows = u.shape[0]
    row = lax.broadcasted_iota(jnp.int32, u.shape, 0)
    before = jnp.where(i == 0, 0.0, prev_ref[HALO - 1:HALO, :])
    after = jnp.where(i == n - 1, 0.0, next_ref[0:1, :])
    up = jnp.where(row == 0, before, pltpu.roll(u, 1, 0))
    un = jnp.where(row == rows - 1, after, pltpu.roll(u, rows - 1, 0))
    o_ref[...] = w_ref[0:1, :] * up + w_ref[1:2, :] * u + w_ref[2:3, :] * un + b_ref[...]


def _hy_conv(p, conv_w, conv_b, tm):
    length = p.shape[0]
    per = tm // HALO
    last = length // HALO - 1
    return pl.pallas_call(
        _hy_conv_body,
        grid=(3, length // tm),
        in_specs=[pl.BlockSpec((tm, HY_W), lambda c, i: (i, c)),
                  pl.BlockSpec((HALO, HY_W), lambda c, i: (jnp.maximum(i * per - 1, 0), c)),
                  pl.BlockSpec((HALO, HY_W), lambda c, i: (jnp.minimum((i + 1) * per, last), c)),
                  pl.BlockSpec((CONV_W, HY_W), lambda c, i: (0, c)),
                  pl.BlockSpec((1, HY_W), lambda c, i: (0, c))],
        out_specs=pl.BlockSpec((None, tm, HY_W), lambda c, i: (c, i, 0)),
        out_shape=jax.ShapeDtypeStruct((3, length, HY_W), jnp.float32),
        name="hy_conv",
    )(p, p, p, conv_w, conv_b.reshape(1, -1))


def _hy_filter_body(z_ref, w1_ref, b1_ref, w2_ref, b2_ref, w3_ref, b3_ref, w4_ref, fq_ref, dl_ref,
                    e_ref, o_ref, nrm_ref):
    i = pl.program_id(0)
    bf = jnp.bfloat16
    z = z_ref[...]
    fq = fq_ref[...]

    def layer(a, w_ref, b_ref):
        return jnp.sin(fq * (jnp.dot(a.astype(bf), w_ref[...].astype(bf), preferred_element_type=jnp.float32)
                             + b_ref[...]))

    half = z.shape[0] // 2
    hdn = layer(layer(layer(jnp.concatenate([z[:half], z[half:]], axis=1), w1_ref, b1_ref), w2_ref, b2_ref),
                w3_ref, b3_ref).astype(bf)
    w4 = w4_ref[...].astype(bf)
    h = jnp.concatenate([jnp.dot(hdn[:, :HY_FFN], w4, preferred_element_type=jnp.float32),
                         jnp.dot(hdn[:, HY_FFN:], w4, preferred_element_type=jnp.float32)], axis=0)
    decay = jnp.exp(-z[:, 0:1] * dl_ref[...])
    first = (lax.broadcasted_iota(jnp.int32, decay.shape, 0) == 0) & (i == 0)
    acc = []
    for order in range(HY_ORDER):
        lo = order * 2 * HY_W
        h0 = h[:, lo:lo + HY_W] * decay
        h1 = jnp.where(first, 0.0, h[:, lo + HY_W:lo + 2 * HY_W] * decay)
        e_ref[:, order * HY_W:(order + 1) * HY_W] = h0 + h1
        o_ref[:, order * HY_W:(order + 1) * HY_W] = h0 - h1
        acc.append(jnp.sum(jnp.abs(h0) + jnp.abs(h1), axis=0, keepdims=True))
    part = jnp.concatenate(acc, axis=1)

    @pl.when(i == 0)
    def _():
        nrm_ref[...] = jnp.zeros_like(nrm_ref)

    nrm_ref[...] = nrm_ref[...] + part


def _hy_filter_time(length, w1, b1, w2, b2, w3, b3, w4, freq, tm):
    f32 = jnp.float32
    t = jnp.linspace(0.0, 1.0, length, dtype=f32)[:, None]
    bands = (HY_EMB - 1) // 2
    w_ang = 2.0 * math.pi * jnp.arange(length, dtype=f32)[:, None] / length
    fr = jnp.linspace(1e-4, bands - 1, bands, dtype=f32)[None]
    z = jnp.concatenate([t, jnp.cos(fr * w_ang), -jnp.sin(fr * w_ang)], axis=-1)
    z = jnp.pad(z, ((0, 0), (0, LANE - HY_EMB)))
    w1p = jnp.pad(w1.astype(f32), ((0, LANE - HY_EMB), (0, 0)))
    max_decay = math.log(HY_TARGET) / HY_SHORT_PCT
    min_decay = math.log(HY_TARGET) / HY_LONG_PCT
    deltas = jnp.abs(jnp.linspace(min_decay, max_decay, HY_W, dtype=f32)).reshape(1, HY_W)
    full = lambda a: pl.BlockSpec(a.shape, lambda i: (0,) * a.ndim)
    row2 = lambda v: jnp.tile(v.astype(f32).reshape(1, -1), (1, 2))
    diag2 = lambda w: jnp.kron(jnp.eye(2, dtype=f32), w.astype(f32))
    args = (z, diag2(w1p), row2(b1), diag2(w2), row2(b2), diag2(w3), row2(b3), w4.astype(f32), row2(freq), deltas)
    ow = HY_ORDER * HY_W
    return pl.pallas_call(
        _hy_filter_body,
        grid=(length // tm,),
        in_specs=[pl.BlockSpec((tm, LANE), lambda i: (i, 0))] + [full(a) for a in args[1:]],
        out_specs=[pl.BlockSpec((tm, ow), lambda i: (i, 0)), pl.BlockSpec((tm, ow), lambda i: (i, 0)),
                   pl.BlockSpec((1, ow), lambda i: (0, 0))],
        out_shape=[jax.ShapeDtypeStruct((length, ow), f32), jax.ShapeDtypeStruct((length, ow), f32),
                   jax.ShapeDtypeStruct((1, ow), f32)],
        compiler_params=pltpu.CompilerParams(dimension_semantics=("arbitrary",)),
        name="hy_filter_time",
    )(*args)


def _words(x_bf16):
    return pltpu.bitcast(x_bf16, jnp.uint32)


def _halves(w_u32):
    return pltpu.bitcast(w_u32, jnp.bfloat16)


def _transpose8(parts):
    rows, cols = parts[0].shape
    parts = [p.reshape(rows // SUBLANE, SUBLANE, cols) for p in parts]
    row = lax.broadcasted_iota(jnp.int32, parts[0].shape, 1)
    for s in (1, 2, 4):
        keep = (row & s) == 0
        nxt = list(parts)
        for i in range(SUBLANE):
            if i & s == 0:
                a, b = parts[i], parts[i + s]
                nxt[i] = jnp.where(keep, a, pltpu.roll(b, s, 1))
                nxt[i + s] = jnp.where(keep, pltpu.roll(a, SUBLANE - s, 1), b)
        parts = nxt
    return [p.reshape(rows, cols) for p in parts]


def _gather_tiles(ref, j, n):
    return jnp.concatenate([ref[SUBLANE * g + j] for g in range(n)], axis=0)


def _hy_s1_body(x_ref, f_ref, a_ref):
    groups = x_ref.shape[0] // SUBLANE
    xs = _transpose8([_gather_tiles(x_ref, j, groups) for j in range(SUBLANE)])
    words = []
    for i in range(HY_NB):
        acc = jnp.dot(f_ref[i], xs[i].astype(jnp.bfloat16), preferred_element_type=jnp.float32)
        words.append(_words(acc.astype(jnp.bfloat16)))
    tiles = _transpose8(words)
    per = HY_N1 // SUBLANE
    for j in range(SUBLANE):
        for g in range(per):
            a_ref[per * j + g] = tiles[j][SUBLANE * g:SUBLANE * (g + 1)]


def _hy_s1(x4, sel, fwd):
    assert HY_NB == SUBLANE
    _, half, _, width = x4.shape
    return pl.pallas_call(
        _hy_s1_body,
        grid=(HY_N2 // HY_NB,),
        in_specs=[pl.BlockSpec((None, half, HY_NB, width), lambda j: (sel, 0, j, 0)),
                  pl.BlockSpec((HY_NB, 2 * HY_N1, half), lambda j: (j, 0, 0))],
        out_specs=pl.BlockSpec((HY_N1, HY_NB, width), lambda j: (0, j, 0)),
        out_shape=jax.ShapeDtypeStruct((HY_N1, HY_N2, width), jnp.uint32),
        name="hy_stage1",
    )(x4, fwd)


HY_SLABS = 2


def _hy_s2f_body(ae_ref, ao_ref, m_ref, sc_ref, hr_ref, hi_ref):
    dot = lambda a, b: jnp.dot(a, b, preferred_element_type=jnp.float32)
    for s in range(HY_SLABS):
        hr_ref[s] = (dot(m_ref[:HY_N2, :], _halves(ae_ref[s])) * sc_ref[...]).astype(hr_ref.dtype)
        hi_ref[s] = (dot(m_ref[HY_N2:, :], _halves(ao_ref[s])) * sc_ref[...]).astype(hi_ref.dtype)


def _hy_s2f(a_e, a_o, m_fwd, scale):
    width = scale.shape[1]
    slab = pl.BlockSpec((HY_SLABS, HY_N2, width), lambda k: (k, 0, 0))
    out = jax.ShapeDtypeStruct((HY_N1, HY_N2, width), jnp.bfloat16)
    return pl.pallas_call(
        _hy_s2f_body,
        grid=(HY_N1 // HY_SLABS,),
        in_specs=[slab, slab, pl.BlockSpec((2 * HY_N2, 2 * HY_N2), lambda k: (0, 0)),
                  pl.BlockSpec((1, width), lambda k: (0, 0))],
        out_specs=[slab, slab],
        out_shape=[out, out],
        name="hy_filter_stage2",
    )(a_e, a_o, m_fwd, scale)


def _hy_s2_body(a_ref, hr_ref, hi_ref, mf_ref, mi_ref, b_ref):
    bf = jnp.bfloat16
    for s in range(HY_SLABS):
        x = jnp.dot(mf_ref[...], _halves(a_ref[s]), preferred_element_type=jnp.float32)
        xr, xi = x[:HY_N2], x[HY_N2:]
        hr = hr_ref[s].astype(jnp.float32)
        hi = hi_ref[s].astype(jnp.float32)
        y = jnp.concatenate([(xr * hr - xi * hi).astype(bf), (xr * hi + xi * hr).astype(bf)], axis=0)
        b = jnp.dot(mi_ref[...], y, preferred_element_type=jnp.float32)
        b_ref[s] = _words(b.astype(bf))


def _hy_s2(a, h_re, h_im, m_fwd, m_inv, order):
    slab = pl.BlockSpec((HY_SLABS, HY_N2, HY_W), lambda k: (k, 0, 0))
    hslab = pl.BlockSpec((HY_SLABS, HY_N2, HY_W), lambda k: (k, 0, order))
    mat = pl.BlockSpec((2 * HY_N2, 2 * HY_N2), lambda k: (0, 0))
    return pl.pallas_call(
        _hy_s2_body,
        grid=(HY_N1 // HY_SLABS,),
        in_specs=[slab, hslab, hslab, mat, mat],
        out_specs=slab,
        out_shape=jax.ShapeDtypeStruct((HY_N1, HY_N2, HY_W), jnp.uint32),
        name="hy_stage2",
    )(a, h_re, h_im, m_fwd, m_inv)


def _hy_is1_body(b_ref, g_ref, z_ref, gate_ref, bias_ref, o_ref):
    per = HY_N1 // SUBLANE
    spectra = _transpose8([jnp.concatenate([b_ref[per * j + g] for g in range(per)], axis=0)
                           for j in range(SUBLANE)])
    convs = [jnp.dot(g_ref[i], _halves(spectra[i]), preferred_element_type=jnp.float32) for i in range(HY_NB)]
    tiles = _transpose8(convs)
    for j in range(SUBLANE):
        for g in range(o_ref.shape[0] // SUBLANE):
            n1 = SUBLANE * g + j
            o_ref[n1] = gate_ref[n1] * (tiles[j][SUBLANE * g:SUBLANE * (g + 1)] + z_ref[n1] * bias_ref[...])


def _hy_is1(b, inv, z4, z_sel, gate4, gate_sel, bias):
    _, half, _, width = z4.shape
    real = lambda sel: pl.BlockSpec((None, half, HY_NB, width), lambda j: (sel, 0, j, 0))
    return pl.pallas_call(
        _hy_is1_body,
        grid=(HY_N2 // HY_NB,),
        in_specs=[pl.BlockSpec((HY_N1, HY_NB, width), lambda j: (0, j, 0)),
                  pl.BlockSpec((HY_NB, half, 2 * HY_N1), lambda j: (j, 0, 0)),
                  real(z_sel), real(gate_sel), pl.BlockSpec((1, width), lambda j: (0, 0))],
        out_specs=pl.BlockSpec((half, HY_NB, width), lambda j: (0, j, 0)),
        out_shape=jax.ShapeDtypeStruct((half, HY_N2, width), jnp.float32),
        name="hy_inv_stage1",
    )(b, inv, z4, gate4, bias)


def hyena_long(p_hy, f_w1, f_b1, f_w2, f_b2, f_w3, f_b3, f_w4, f_freq, f_bias, conv_w, conv_b):
    length = p_hy.shape[0]
    bf = jnp.bfloat16
    half = HY_N1 // 2
    fwd, inv, m_fwd, m_inv = (jnp.asarray(m, bf) for m in _hy_dft_constants(length))
    e, od, nrm = _hy_filter_time(length, f_w1, f_b1, f_w2, f_b2, f_w3, f_b3, f_w4, f_freq, HY_TM)
    ow = HY_ORDER * HY_W
    a_e = _hy_s1(e.reshape(1, half, HY_N2, ow), 0, fwd)
    a_o = _hy_s1(od.reshape(1, half, HY_N2, ow), 0, fwd)
    scale = 1.0 / (nrm * (2.0 * length))
    h_re, h_im = _hy_s2f(a_e, a_o, m_fwd, scale)
    u = _hy_conv(p_hy, conv_w, conv_b, HY_TM).reshape(3, half, HY_N2, HY_W)
    z, z_sel = u, 0
    for order in range(HY_ORDER):
        b = _hy_s2(_hy_s1(z, z_sel, fwd), h_re, h_im, m_fwd, m_inv, order)
        z = _hy_is1(b, inv, z, z_sel, u, 1 + order, f_bias[order].astype(jnp.float32).reshape(1, HY_W))[None]
        z_sel = 0
    return z.reshape(length, HY_W)


def even_mixer(h, hc, w_in, conv_w, conv_b, f_w1, f_b1, f_w2, f_b2, f_w3, f_b3, f_w4, f_freq, f_bias,
               q_norm, k_norm, sink, ctx_out):
    f32 = jnp.float32
    bsz, length, _ = h.shape
    n_ctx = hc.shape[1]
    filt = (f_w1, f_b1, f_w2, f_b2, f_w3, f_b3, f_w4, f_freq)
    o_v = ATT_KV_W
    o_q = 2 * ATT_KV_W
    o_hy = o_q + ATT_Q_W
    w_perm = jnp.concatenate([w_in[:, o_hy:], w_in[:, o_q:o_hy], w_in[:, :o_q]], axis=1)
    hy_w = 3 * HY_W
    p_hy = pmm3(h, w_perm).astype(f32)
    p = jnp.concatenate([p_hy[..., hy_w + ATT_Q_W:], p_hy[..., hy_w:hy_w + ATT_Q_W]], axis=-1)
    pc = pmm3(hc, w_in if ctx_out else w_in[:, :o_q]).astype(f32)

    def heads(a, n_heads):
        return a.reshape(a.shape[0], a.shape[1], n_heads, HEAD_DIM)

    k_c = rms_norm(heads(pc[..., :o_v], ATT_KV_HEADS), k_norm)
    v_c = heads(pc[..., o_v:o_q], ATT_KV_HEADS)
    rope = axial_rope_tables(length)
    k = apply_axial_rope(rms_norm(heads(p[..., :o_v], ATT_KV_HEADS), k_norm), rope)
    v = heads(p[..., o_v:o_q], ATT_KV_HEADS)
    q = apply_axial_rope(rms_norm(heads(p[..., o_q:o_hy], ATT_HEADS), q_norm), rope)
    att = window_attention(q.reshape(bsz, length, ATT_KV_HEADS, ATT_GROUP, HEAD_DIM), k, v, k_c, v_c, sink)
    hy = hyena_long(p_hy[0], *filt, f_bias, conv_w, conv_b)[None]
    out = jnp.concatenate([hy, att], axis=-1).astype(h.dtype)
    if not ctx_out:
        return out, None
    q_c = rms_norm(heads(pc[..., o_q:o_hy], ATT_HEADS), q_norm).reshape(bsz, n_ctx, ATT_KV_HEADS, ATT_GROUP, HEAD_DIM)
    att_c = context_attention(q_c, k_c, v_c, sink)
    hy_c = hyena_mix(pc[..., o_hy:], hyena_filters(n_ctx, *filt), f_bias, conv_w, conv_b)
    return out, jnp.concatenate([hy_c, att_c], axis=-1).astype(hc.dtype)


ATT_TQ = ATT_BLOCK
EVEN_TM = 256
EV_Q_BLK = 3 * HY_W // ATT_Q_W
EV_K_BLK = (3 * HY_W + ATT_Q_W) // ATT_KV_W
EV_V_BLK = EV_K_BLK + 1


def _rope_tables(length):
    cr, sr, cc, sc = axial_rope_tables(length)
    return jnp.concatenate([cr, cr, cc, cc], axis=-1), jnp.concatenate([-sr, sr, -sc, sc], axis=-1)


def _head_norm_rope(x, g_row, c, s, seg):
    sq = x * x
    hi = sq.astype(jnp.bfloat16)
    lo = (sq - hi.astype(jnp.float32)).astype(jnp.bfloat16)
    ms = (jnp.dot(hi, seg, preferred_element_type=jnp.float32) + jnp.dot(lo, seg, preferred_element_type=jnp.float32))
    y = x * lax.rsqrt(ms + EPS) * g_row
    width = x.shape[1]
    quarter = HEAD_DIM // 4
    lane = lax.broadcasted_iota(jnp.int32, x.shape, 1)
    partner = jnp.where((lane & quarter) == 0, pltpu.roll(y, width - quarter, 1), pltpu.roll(y, quarter, 1))
    return y * c + partner * s


def _qk_prep_body(q_ref, k_ref, c_ref, s_ref, qn_ref, kn_ref, segq_ref, segk_ref, qo_ref, ko_ref):
    c, s = c_ref[...], s_ref[...]
    tile = lambda t, n: jnp.concatenate([t] * n, axis=1)
    qo_ref[...] = _head_norm_rope(q_ref[...], tile(qn_ref[...], ATT_HEADS), tile(c, ATT_HEADS), tile(s, ATT_HEADS),
                                  segq_ref[...]).astype(qo_ref.dtype)
    ko_ref[...] = _head_norm_rope(k_ref[...], tile(kn_ref[...], ATT_KV_HEADS), tile(c, ATT_KV_HEADS),
                                  tile(s, ATT_KV_HEADS), segk_ref[...]).astype(ko_ref.dtype)


def _qk_prep(p, rope_c, rope_s, q_norm, k_norm, tm):
    import numpy as np
    rows = p.shape[0]
    seg = lambda w: jnp.asarray(np.kron(np.eye(w // HEAD_DIM), np.full((HEAD_DIM, HEAD_DIM), 1.0 / HEAD_DIM)),
                                jnp.bfloat16)
    const = lambda shape: pl.BlockSpec(shape, lambda i: (0, 0))
    return pl.pallas_call(
        _qk_prep_body,
        grid=(rows // tm,),
        in_specs=[pl.BlockSpec((tm, ATT_Q_W), lambda i: (i, EV_Q_BLK)),
                  pl.BlockSpec((tm, ATT_KV_W), lambda i: (i, EV_K_BLK)),
                  pl.BlockSpec((tm, HEAD_DIM), lambda i: (i, 0)), pl.BlockSpec((tm, HEAD_DIM), lambda i: (i, 0)),
                  const((1, HEAD_DIM)), const((1, HEAD_DIM)),
                  const((ATT_Q_W, ATT_Q_W)), const((ATT_KV_W, ATT_KV_W))],
        out_specs=[pl.BlockSpec((tm, ATT_Q_W), lambda i: (i, 0)), pl.BlockSpec((tm, ATT_KV_W), lambda i: (i, 0))],
        out_shape=[jax.ShapeDtypeStruct((rows, ATT_Q_W), jnp.bfloat16),
                   jax.ShapeDtypeStruct((rows, ATT_KV_W), jnp.bfloat16)],
        name="qk_prep",
    )(p, p, rope_c, rope_s, q_norm.astype(jnp.float32).reshape(1, -1), k_norm.astype(jnp.float32).reshape(1, -1),
      seg(ATT_Q_W), seg(ATT_KV_W))


def _att_body(sink_ref, q_ref, kc_ref, vc_ref, *rest, local):
    if local:
        kp_ref, k0_ref, kn_ref, vp_ref, v0_ref, vn_ref, o_ref = rest
    else:
        (o_ref,) = rest
    b = pl.program_id(0)
    nb = pl.num_programs(0)
    bf = jnp.bfloat16
    scale = HEAD_DIM ** -0.5
    q = q_ref[...]
    kc = kc_ref[...]
    vc = vc_ref[...].astype(bf)
    if local:
        kb = jnp.concatenate([kp_ref[...], k0_ref[...], kn_ref[...]], axis=0)
        vb = jnp.concatenate([vp_ref[...], v0_ref[...], vn_ref[...]], axis=0).astype(bf)
        i = lax.broadcasted_iota(jnp.int32, (ATT_TQ, 3 * ATT_BLOCK), 0)
        j = lax.broadcasted_iota(jnp.int32, (ATT_TQ, 3 * ATT_BLOCK), 1)
        rel = j - ATT_BLOCK - i
        valid = ((jnp.abs(rel) <= ATT_WINDOW) & ((b > 0) | (j >= ATT_BLOCK))
                 & ((b < nb - 1) | (j < 2 * ATT_BLOCK)))
    rows = ATT_GROUP * ATT_TQ
    head_of_row = lax.broadcasted_iota(jnp.int32, (rows, 1), 0) // ATT_TQ
    for hk in range(ATT_KV_HEADS):
        heads = range(hk * ATT_GROUP, (hk + 1) * ATT_GROUP)
        qs = jnp.concatenate([q[:, h * HEAD_DIM:(h + 1) * HEAD_DIM] for h in heads], axis=0)
        kv_cols = slice(hk * HEAD_DIM, (hk + 1) * HEAD_DIM)
        sink = jnp.zeros((rows, 1), jnp.float32)
        for g, h in enumerate(heads):
            sink = jnp.where(head_of_row == g, sink_ref[h], sink)
        s_ctx = _dot_nt(qs, kc[:, kv_cols]) * scale
        m = jnp.maximum(jnp.max(s_ctx, axis=-1, keepdims=True), sink)
        if local:
            valid4 = jnp.concatenate([valid] * ATT_GROUP, axis=0)
            s_loc = jnp.where(valid4, _dot_nt(qs, kb[:, kv_cols]) * scale, -jnp.inf)
            m = jnp.maximum(m, jnp.max(s_loc, axis=-1, keepdims=True))
        p_ctx = jnp.exp(s_ctx - m)
        den = jnp.sum(p_ctx, axis=-1, keepdims=True) + jnp.exp(sink - m)
        acc = jnp.dot(p_ctx.astype(bf), vc[:, kv_cols], preferred_element_type=jnp.float32)
        if local:
            p_loc = jnp.exp(s_loc - m)
            den = den + jnp.sum(p_loc, axis=-1, keepdims=True)
            acc = acc + jnp.dot(p_loc.astype(bf), vb[:, kv_cols], preferred_element_type=jnp.float32)
        out = acc / den
        for g, h in enumerate(heads):
            o_ref[:, h * HEAD_DIM:(h + 1) * HEAD_DIM] = out[g * ATT_TQ:(g + 1) * ATT_TQ]


def _attention(qn, kn, p, kcn, pc, sink, local):
    rows = qn.shape[0]
    nb = rows // ATT_TQ
    n_ctx = kcn.shape[0]
    specs = [pl.BlockSpec((ATT_TQ, ATT_Q_W), lambda b, s: (b, 0)),
             pl.BlockSpec((n_ctx, ATT_KV_W), lambda b, s: (0, 0)),
             pl.BlockSpec((n_ctx, ATT_KV_W), lambda b, s: (0, EV_V_BLK))]
    args = [qn, kcn, pc]
    if local:
        prev = lambda b, s: jnp.maximum(b - 1, 0)
        nxt = lambda b, s: jnp.minimum(b + 1, nb - 1)
        for col, arr in ((0, kn), (EV_V_BLK, p)):
            specs += [pl.BlockSpec((ATT_BLOCK, ATT_KV_W), lambda b, s, col=col: (prev(b, s), col)),
                      pl.BlockSpec((ATT_BLOCK, ATT_KV_W), lambda b, s, col=col: (b, col)),
                      pl.BlockSpec((ATT_BLOCK, ATT_KV_W), lambda b, s, col=col: (nxt(b, s), col))]
            args += [arr, arr, arr]
    return pl.pallas_call(
        functools.partial(_att_body, local=local),
        grid_spec=pltpu.PrefetchScalarGridSpec(
            num_scalar_prefetch=1, grid=(nb,), in_specs=specs,
            out_specs=pl.BlockSpec((ATT_TQ, ATT_Q_W), lambda b, s: (b, 0))),
        out_shape=jax.ShapeDtypeStruct((rows, ATT_Q_W), jnp.float32),
        name="window_attention" if local else "context_attention",
    )(sink.astype(jnp.float32), *args)


def _even_merge_body(hy_ref, att_ref, x_ref, gt_ref, w_ref, o_ref):
    bf = jnp.bfloat16
    m = (jnp.dot(hy_ref[...].astype(bf), w_ref[:HY_W, :].astype(bf), preferred_element_type=jnp.float32)
         + jnp.dot(att_ref[...].astype(bf), w_ref[HY_W:, :].astype(bf), preferred_element_type=jnp.float32))
    o_ref[...] = x_ref[...] + gt_ref[...] * m


def _even_merge(hy, att, x2d, gt, w_out, tm):
    rows, d = x2d.shape
    half = pl.BlockSpec((tm, HY_W), lambda i: (i, 0))
    return pl.pallas_call(
        _even_merge_body,
        grid=(rows // tm,),
        in_specs=[half, half, pl.BlockSpec((tm, d), lambda i: (i, 0)), pl.BlockSpec((1, d), lambda i: (0, 0)),
                  pl.BlockSpec((MIX_W, d), lambda i: (0, 0))],
        out_specs=pl.BlockSpec((tm, d), lambda i: (i, 0)),
        out_shape=jax.ShapeDtypeStruct((rows, d), jnp.float32),
        name="even_merge",
    )(hy, att, x2d, gt.reshape(1, -1), w_out)


def _hy_short_body(e_ref, o_ref, sc_ref, u_ref, bias_ref, cf_ref, sf_ref, out_ref):
    bf = jnp.bfloat16
    cf, sf = cf_ref[...], sf_ref[...]
    dot = lambda a, b: jnp.dot(a, b.astype(bf), preferred_element_type=jnp.float32)
    z = u_ref[0]
    for order in range(HY_ORDER):
        cols = slice(order * HY_W, (order + 1) * HY_W)
        h_re = dot(cf, e_ref[:, cols]) * sc_ref[:, cols]
        h_im = dot(sf, o_ref[:, cols]) * sc_ref[:, cols]
        x_re, x_im = dot(cf, z), dot(sf, z)
        y_re = (x_re * h_re - x_im * h_im).astype(bf)
        y_im = (x_re * h_im + x_im * h_re).astype(bf)
        conv = _dot_tn(cf, y_re) + _dot_tn(sf, y_im)
        z = u_ref[1 + order] * (conv + z * bias_ref[order:order + 1, :])
    out_ref[...] = z


def hyena_short(pc, f_w1, f_b1, f_w2, f_b2, f_w3, f_b3, f_w4, f_freq, f_bias, conv_w, conv_b):
    import numpy as np
    rows = pc.shape[0]
    e, od, nrm = _hy_filter_time(rows, f_w1, f_b1, f_w2, f_b2, f_w3, f_b3, f_w4, f_freq, rows)
    u = _hy_conv(pc, conv_w, conv_b, rows)
    ang = 2 * np.pi * np.outer(np.arange(2 * rows), np.arange(rows)) / (2 * rows)
    cf, sf = jnp.asarray(np.cos(ang), jnp.bfloat16), jnp.asarray(-np.sin(ang), jnp.bfloat16)
    scale = 1.0 / (nrm * (2.0 * rows))
    return pl.pallas_call(
        _hy_short_body,
        out_shape=jax.ShapeDtypeStruct((rows, HY_W), jnp.float32),
        name="hy_short",
    )(e, od, scale, u, f_bias.astype(jnp.float32), cf, sf)


def even_layer(x, xc, g, mod, mod_c, w_in, conv_w, conv_b, f_w1, f_b1, f_w2, f_b2, f_w3, f_b3, f_w4, f_freq, f_bias,
               q_norm, k_norm, sink, gt, gt_c, w_out):
    f32 = jnp.float32
    length, n_ctx = x.shape[1], xc.shape[1]
    filt = (f_w1, f_b1, f_w2, f_b2, f_w3, f_b3, f_w4, f_freq, f_bias, conv_w, conv_b)
    o_q = 2 * ATT_KV_W
    o_hy = o_q + ATT_Q_W
    w_perm = jnp.concatenate([w_in[:, o_hy:], w_in[:, o_q:o_hy], w_in[:, :o_q]], axis=1).astype(jnp.bfloat16)
    twice = lambda v: jnp.stack([v.reshape(-1), v.reshape(-1)])
    p = norm_proj(x[0], g, twice(mod[0]), twice(mod[1]), w_perm)
    pc = norm_proj(xc[0], g, twice(mod_c[0]), twice(mod_c[1]), w_perm)
    rope_c, rope_s = _rope_tables(length)
    qn, kn = _qk_prep(p, rope_c, rope_s, q_norm, k_norm, EVEN_TM)
    qcn, kcn = _qk_prep(pc, jnp.ones((n_ctx, HEAD_DIM), f32), jnp.zeros((n_ctx, HEAD_DIM), f32), q_norm, k_norm,
                        n_ctx)
    att = _attention(qn, kn, p, kcn, pc, sink, True)
    att_c = _attention(qcn, None, None, kcn, pc, sink, False)
    hy = hyena_long(p, *filt)
    hy_c = hyena_short(pc, *filt)
    x_new = _even_merge(hy, att, x[0], gt, w_out, EVEN_TM)
    xc_new = _even_merge(hy_c, att_c, xc[0], gt_c, w_out, n_ctx)
    return x_new[None], xc_new[None]


def odd_mixer(h, hc, lb, w_in, conv_w, conv_b, dt_bias, a_log, d_skip, ssd_norm, hg_norm, ctx_out):
    f32 = jnp.float32
    bsz, length, _ = h.shape
    n_ctx = hc.shape[1]
    p = pmm3(h, w_in).astype(f32)
    pc = pmm3(hc, w_in if ctx_out else w_in[:, :ODD_STATE_COLS]).astype(f32)
    o_dt = SSD_CONV_CH
    o_f = SSD_CONV_CH + 2 * SSD_HEADS
    o_i = o_f + 2 * HG_W
    o_z = ODD_STATE_COLS
    o_q = o_z + SSD_W
    o_g = o_q + HG_W
    gn = SSD_GROUPS * SSD_STATE

    def streams(pp):
        n = pp.shape[1]
        xbc = jax.nn.silu(dwconv_centred(pp[..., :SSD_CONV_CH], conv_w, conv_b))
        xs = xbc[..., :SSD_W].reshape(bsz, n, SSD_HEADS, SSD_HEAD_DIM)
        bm = xbc[..., SSD_W:SSD_W + gn].reshape(bsz, n, SSD_GROUPS, SSD_STATE)
        cm = xbc[..., SSD_W + gn:].reshape(bsz, n, SSD_GROUPS, SSD_STATE)
        dt_raw = pp[..., o_dt:o_f].reshape(bsz, n, 2, SSD_HEADS)
        f_raw = pp[..., o_f:o_i].reshape(bsz, n, 2, HG_HEADS, HG_EXPAND)
        iv = pp[..., o_i:o_i + HG_W].reshape(bsz, n, HG_HEADS, HG_VDIM)
        return xs, bm, cm, dt_raw, f_raw, iv

    xs, bm, cm, dt_raw, f_raw, iv = streams(p)
    xs_c, bm_c, cm_c, dt_raw_c, f_raw_c, iv_c = streams(pc)
    q = jax.nn.silu(p[..., o_q:o_g]).reshape(bsz, length, HG_HEADS, HG_EXPAND)
    q_c = jax.nn.silu(pc[..., o_q:o_g]).reshape(bsz, n_ctx, HG_HEADS, HG_EXPAND) if ctx_out else None
    lb = lb.astype(f32).reshape(HG_HEADS, HG_EXPAND)
    ssd0 = jnp.zeros((bsz, SSD_GROUPS, SSD_HEADS // SSD_GROUPS, SSD_HEAD_DIM, SSD_STATE), f32)
    hg0 = jnp.zeros((bsz, HG_HEADS, HG_EXPAND, HG_VDIM), f32)
    y_dirs, o_dirs, yc_dirs, oc_dirs = [], [], [], []
    for d in range(2):
        fl = (lambda a: jnp.flip(a, axis=1)) if d == 1 else (lambda a: a)
        a = -jnp.exp(a_log[d].astype(f32))
        dsk = d_skip[d].astype(f32)
        dtb = dt_bias[d].astype(f32)
        dt_l = jax.nn.softplus(dt_raw[:, :, d] + dtb)
        dt_c = jax.nn.softplus(dt_raw_c[:, :, d] + dtb)
        yc, s_ctx = ssd_scan(fl(xs_c), fl(dt_c), a, fl(bm_c), fl(cm_c), dsk, ssd0, ctx_out)
        yl, _ = ssd_scan(fl(xs), fl(dt_l), a, fl(bm), fl(cm), dsk, s_ctx, True)
        y_dirs.append(fl(yl))
        f_l = lb + (1.0 - lb) * jax.nn.sigmoid(f_raw[:, :, d])
        f_c = lb + (1.0 - lb) * jax.nn.sigmoid(f_raw_c[:, :, d])
        oc, s_hg = hgrn2_scan(fl(q_c) if ctx_out else None, fl(1.0 - f_c), fl(iv_c), fl(jnp.log(f_c)), hg0, ctx_out)
        ol, _ = hgrn2_scan(fl(q), fl(1.0 - f_l), fl(iv), fl(jnp.log(f_l)), s_hg, True)
        o_dirs.append(fl(ol))
        if ctx_out:
            yc_dirs.append(fl(yc))
            oc_dirs.append(fl(oc))

    def merge(yy, oo, pp, n):
        z = pp[..., o_z:o_q]
        g = pp[..., o_g:]
        ys = (yy.reshape(bsz, n, SSD_W) * jax.nn.silu(z)).reshape(bsz, n, SSD_GROUPS, SSD_W // SSD_GROUPS)
        ys = rms_norm(ys, ssd_norm.reshape(SSD_GROUPS, SSD_W // SSD_GROUPS)).reshape(bsz, n, SSD_W)
        hs = rms_norm(oo, hg_norm.reshape(HG_HEADS, HG_VDIM)).reshape(bsz, n, HG_W) * jax.nn.silu(g)
        return jnp.concatenate([ys, hs], axis=-1)

    out = merge(y_dirs[0] + y_dirs[1], o_dirs[0] + o_dirs[1], p, length).astype(h.dtype)
    if not ctx_out:
        return out, None
    out_c = merge(yc_dirs[0] + yc_dirs[1], oc_dirs[0] + oc_dirs[1], pc, n_ctx).astype(hc.dtype)
    return out, out_c


SCAN_Q = 128
SCAN_LEVELS = 7
ODD_COLS = SSD_CONV_CH + 8 * 512
HALO = SUBLANE


def _scan_constants():
    import numpy as np
    q = SCAN_Q
    d_hg, d_ssd, pairs, laters = [], [], [], []
    for direction in (0, 1):
        pos = np.arange(q) if direction == 0 else q - 1 - np.arange(q)
        pj, pt = pos[None, :], pos[:, None]
        top = pj <= pt
        end = pj > pt
        ones = np.ones((SUBLANE, q), bool)
        lv, pr, lt = [], [], []
        for level in range(SCAN_LEVELS):
            b = 2 ** level
            start = (pos // (2 * b)) * (2 * b)
            mid = (start + b)[:, None]
            later = pos >= start + b
            lv.append(np.where(later[:, None], (pj >= mid) & (pj <= pt), (pj > pt) & (pj < mid)))
            pr.append((start[:, None] == start[None, :]) & later[:, None] & ~later[None, :])
            lt.append(np.broadcast_to(later[:, None], (q, LANE)))
        pr.append(np.eye(q, dtype=bool))
        pr.append(top)
        d_hg.append(np.concatenate([top, end] + lv + [ones], axis=0))
        d_ssd.append(np.concatenate([top, end, ones], axis=0))
        pairs.append(np.stack(pr))
        laters.append(np.stack(lt))
    f = np.float32
    twice = lambda m: np.concatenate([m, m], axis=-1)
    return (twice(np.stack(d_hg)).astype(f), twice(np.stack(d_ssd)).astype(f), np.stack(pairs).astype(f),
            np.stack(laters).astype(f))


def _split_dot(mm_bf16, v):
    hi = v.astype(jnp.bfloat16)
    lo = (v - hi.astype(jnp.float32)).astype(jnp.bfloat16)
    return jnp.dot(mm_bf16, jnp.concatenate([hi, lo], axis=0), preferred_element_type=jnp.float32)


def _dot_nt(a, b):
    return lax.dot_general(a, b, (((1,), (1,)), ((), ())), preferred_element_type=jnp.float32)


def _dot_tn(a, b):
    return lax.dot_general(a, b, (((0,), (0,)), ((), ())), preferred_element_type=jnp.float32)


def _softplus(x):
    return jnp.maximum(x, 0.0) + jnp.log1p(jnp.exp(-jnp.abs(x)))


def _scan_body(xbc_ref, prev_ref, next_ref, f_ref, iv_ref, q_ref, dt_ref,
               cw_ref, cb_ref, dtb_ref, a_ref, dsk_ref, lb_ref,
               dhg_ref, dssd_ref, pair_ref, later_ref, sel_ref,
               out_ref, s_ssd, s_hg, ydiag_ref, *, n_ctx_chunks, n_chunks):
    d = pl.program_id(0)
    j = pl.program_id(1)
    q_rows = SCAN_Q
    bf = jnp.bfloat16

    @pl.when(j == 0)
    def _():
        s_ssd[...] = jnp.zeros_like(s_ssd)
        s_hg[...] = jnp.zeros_like(s_hg)

    c = jnp.where(d == 0, j, jnp.where(j < n_ctx_chunks, n_ctx_chunks - 1 - j, n_chunks - 1 + n_ctx_chunks - j))
    first = (c == 0) | (c == n_ctx_chunks)
    last = (c == n_ctx_chunks - 1) | (c == n_chunks - 1)

    u = xbc_ref[...]
    row = lax.broadcasted_iota(jnp.int32, u.shape, 0)
    before = jnp.where(first, 0.0, prev_ref[HALO - 1:HALO, :])
    after = jnp.where(last, 0.0, next_ref[0:1, :])
    up = jnp.where(row == 0, before, pltpu.roll(u, 1, 0))
    un = jnp.where(row == q_rows - 1, after, pltpu.roll(u, q_rows - 1, 0))
    xbc = cw_ref[0:1, :] * up + cw_ref[1:2, :] * u + cw_ref[2:3, :] * un + cb_ref[...]
    xbc = xbc * jax.nn.sigmoid(xbc)
    x = xbc[:, :SSD_W]

    dt = _softplus(dt_ref[...] + dtb_ref[...])
    da = dt * a_ref[...]
    r = _split_dot(dssd_ref[...], da)
    cs, to_end, total = r[:q_rows], r[q_rows:2 * q_rows], r[2 * q_rows:2 * q_rows + 1]
    xdt = x * dt
    cs_hi = cs.astype(bf)
    cs_lo = (cs - cs_hi.astype(jnp.float32)).astype(bf)
    cs_rows = _dot_nt(sel_ref[...], cs_hi) + _dot_nt(sel_ref[...], cs_lo)
    l_mask = pair_ref[SCAN_LEVELS + 1]
    decay_in = jnp.exp2(cs)
    w_end = (jnp.exp2(to_end) * xdt).astype(bf)
    gn = SSD_GROUPS * SSD_STATE
    hpg = SSD_HEADS // SSD_GROUPS
    gw = hpg * SSD_HEAD_DIM
    for g in range(SSD_GROUPS):
        b_g = xbc[:, SSD_W + g * SSD_STATE:SSD_W + (g + 1) * SSD_STATE].astype(bf)
        c_g = xbc[:, SSD_W + gn + g * SSD_STATE:SSD_W + gn + (g + 1) * SSD_STATE].astype(bf)
        scores = _dot_nt(c_g, b_g)
        y_off = jnp.dot(c_g, s_ssd[g].astype(bf), preferred_element_type=jnp.float32)
        for hh in range(hpg):
            h = g * hpg + hh
            lo = h * SSD_HEAD_DIM
            diff = cs[:, lo:lo + 1] - cs_rows[h:h + 1, :]
            decay = jnp.exp2(jnp.minimum(diff, 0.0)) * l_mask
            ydiag_ref[:, lo:lo + SSD_HEAD_DIM] = jnp.dot((scores * decay).astype(bf),
                                                         xdt[:, lo:lo + SSD_HEAD_DIM].astype(bf),
                                                         preferred_element_type=jnp.float32)
        cols = slice(g * gw, (g + 1) * gw)
        out_ref[:, cols] = (ydiag_ref[:, cols] + decay_in[:, cols] * y_off + dsk_ref[:, cols] * x[:, cols])
        s_ssd[g] = jnp.exp2(total[:, cols]) * s_ssd[g] + _dot_tn(b_g, w_end[:, cols])

    lb = lb_ref[...]
    f = lb + (1.0 - lb) * jax.nn.sigmoid(f_ref[...])
    k_in = 1.0 - f
    qv = q_ref[...]
    qv = qv * jax.nn.sigmoid(qv)
    v_bf = iv_ref[...].astype(bf)
    e = jnp.exp2(_split_dot(dhg_ref[...], jnp.log2(f)))
    e_top, e_end = e[:q_rows], e[q_rows:2 * q_rows]
    e_tot = e[(2 + SCAN_LEVELS) * q_rows:(2 + SCAN_LEVELS) * q_rows + 1]
    for h in range(HG_HEADS):
        cols = slice(h * HG_EXPAND, (h + 1) * HG_EXPAND)
        q_h, k_h = qv[:, cols], k_in[:, cols]
        att = pair_ref[SCAN_LEVELS] * _dot_nt(q_h.astype(bf), k_h.astype(bf))
        for level in range(SCAN_LEVELS):
            e_l = e[(2 + level) * q_rows:(3 + level) * q_rows, cols]
            w_l = (jnp.where(later_ref[level] > 0.0, q_h, k_h) * e_l).astype(bf)
            att = att + pair_ref[level] * _dot_nt(w_l, w_l)
        o = jnp.dot(att.astype(bf), v_bf[:, cols], preferred_element_type=jnp.float32)
        o = o + _dot_nt((q_h * e_top[:, cols]).astype(bf), s_hg[h].astype(bf))
        out_ref[:, SSD_W + h * HG_VDIM:SSD_W + (h + 1) * HG_VDIM] = o
        s_hg[h] = e_tot[:, cols] * s_hg[h] + _dot_tn(v_bf[:, cols], (k_h * e_end[:, cols]).astype(bf))


def _odd_scan(p_all, conv_w, conv_b, dtb, a_cols, dsk, lb, n_ctx):
    n_rows = p_all.shape[0]
    n_chunks = n_rows // SCAN_Q
    ncc = n_ctx // SCAN_Q
    d_hg, d_ssd, pairs, laters = _scan_constants()
    bf = jnp.bfloat16
    import numpy as np
    sel = np.zeros((LANE, SSD_W), np.float32)
    sel[np.arange(SSD_HEADS), np.arange(SSD_HEADS) * SSD_HEAD_DIM] = 1.0

    def chunk(d, j):
        return jnp.where(d == 0, j, jnp.where(j < ncc, ncc - 1 - j, n_chunks - 1 + ncc - j))

    per = SCAN_Q // HALO
    last_halo = n_rows // HALO - 1
    col512 = lambda blk: (lambda d, j: (chunk(d, j), blk))
    const2 = lambda shape: pl.BlockSpec(shape, lambda d, j: (0,) * len(shape))
    dirc = lambda shape: pl.BlockSpec((None,) + shape, lambda d, j: (d,) + (0,) * len(shape))
    body = functools.partial(_scan_body, n_ctx_chunks=ncc, n_chunks=n_chunks)
    return pl.pallas_call(
        body,
        grid=(2, n_chunks),
        in_specs=[
            pl.BlockSpec((SCAN_Q, SSD_CONV_CH), lambda d, j: (chunk(d, j), 0)),
            pl.BlockSpec((HALO, SSD_CONV_CH), lambda d, j: (jnp.maximum(chunk(d, j) * per - 1, 0), 0)),
            pl.BlockSpec((HALO, SSD_CONV_CH), lambda d, j: (jnp.minimum((chunk(d, j) + 1) * per, last_halo), 0)),
            pl.BlockSpec((SCAN_Q, 512), lambda d, j: (chunk(d, j), 2 + d)),
            pl.BlockSpec((SCAN_Q, 512), col512(4)),
            pl.BlockSpec((SCAN_Q, 512), col512(6)),
            pl.BlockSpec((SCAN_Q, 512), lambda d, j: (chunk(d, j), 8 + d)),
            const2((CONV_W, SSD_CONV_CH)), const2((1, SSD_CONV_CH)),
            dirc((1, SSD_W)), dirc((1, SSD_W)), dirc((1, SSD_W)), const2((1, HG_W)),
            dirc(d_hg.shape[1:]), dirc(d_ssd.shape[1:]), dirc(pairs.shape[1:]), dirc(laters.shape[1:]),
            const2((LANE, SSD_W)),
        ],
        out_specs=pl.BlockSpec((None, SCAN_Q, MIX_W), lambda d, j: (d, chunk(d, j), 0)),
        out_shape=jax.ShapeDtypeStruct((2, n_rows, MIX_W), jnp.float32),
        scratch_shapes=[pltpu.VMEM((SSD_GROUPS, SSD_STATE, SSD_W // SSD_GROUPS), jnp.float32),
                        pltpu.VMEM((HG_HEADS, HG_VDIM, HG_EXPAND), jnp.float32),
                        pltpu.VMEM((SCAN_Q, SSD_W), jnp.float32)],
        compiler_params=pltpu.CompilerParams(dimension_semantics=("arbitrary", "arbitrary"),
                                             vmem_limit_bytes=MOE_VMEM_LIMIT),
        name="odd_scan",
    )(p_all, p_all, p_all, p_all, p_all, p_all, p_all,
      conv_w, conv_b.reshape(1, -1), dtb, a_cols, dsk, lb.reshape(1, -1),
      jnp.asarray(d_hg, bf), jnp.asarray(d_ssd, bf), jnp.asarray(pairs), jnp.asarray(laters),
      jnp.asarray(sel, bf))


def _group_rms(v, width):
    parts = []
    for lo in range(0, v.shape[1], width):
        seg = v[:, lo:lo + width]
        parts.append(seg * lax.rsqrt(jnp.mean(seg * seg, axis=-1, keepdims=True) + EPS))
    return jnp.concatenate(parts, axis=1)


def _odd_merge_body(yo_ref, z_ref, g_ref, x_ref, sn_ref, hn_ref, gt_ref, w_ref, o_ref):
    yo = yo_ref[0] + yo_ref[1]
    z = z_ref[...]
    g = g_ref[...]
    ys = _group_rms(yo[:, :SSD_W] * (z * jax.nn.sigmoid(z)), SSD_W // SSD_GROUPS) * sn_ref[...]
    hs = _group_rms(yo[:, SSD_W:], HG_VDIM) * hn_ref[...] * (g * jax.nn.sigmoid(g))
    m = jnp.concatenate([ys, hs], axis=1).astype(jnp.bfloat16)
    o_ref[...] = x_ref[...] + gt_ref[...] * jnp.dot(m, w_ref[...].astype(jnp.bfloat16),
                                                    preferred_element_type=jnp.float32)


ODD_TM = 256


def _odd_merge(yo, p_all, x_all, ssd_norm, hg_norm, gt, w_out, n_ctx):
    n_all, d = x_all.shape
    n_lat = n_all - n_ctx
    skip = n_ctx // ODD_TM
    return pl.pallas_call(
        _odd_merge_body,
        grid=(n_lat // ODD_TM,),
        in_specs=[pl.BlockSpec((2, ODD_TM, MIX_W), lambda i: (0, i + skip, 0)),
                  pl.BlockSpec((ODD_TM, 512), lambda i: (i + skip, 5)),
                  pl.BlockSpec((ODD_TM, 512), lambda i: (i + skip, 7)),
                  pl.BlockSpec((ODD_TM, d), lambda i: (i + skip, 0)),
                  pl.BlockSpec((1, SSD_W), lambda i: (0, 0)),
                  pl.BlockSpec((1, HG_W), lambda i: (0, 0)),
                  pl.BlockSpec((1, d), lambda i: (0, 0)),
                  pl.BlockSpec((MIX_W, d), lambda i: (0, 0))],
        out_specs=pl.BlockSpec((ODD_TM, d), lambda i: (i, 0)),
        out_shape=jax.ShapeDtypeStruct((n_lat, d), jnp.float32),
        compiler_params=pltpu.CompilerParams(dimension_semantics=("arbitrary",)),
        name="odd_merge",
    )(yo, p_all, p_all, x_all, ssd_norm.reshape(1, -1), hg_norm.reshape(1, -1), gt.reshape(1, -1), w_out)


def odd_layer(x_all, n_ctx, g, mod, mod_c, lb, w_in, conv_w, conv_b, dt_bias, a_log, d_skip, ssd_norm, hg_norm, gt,
              w_out):
    f32 = jnp.float32
    o_dt = SSD_CONV_CH
    o_f = o_dt + 2 * SSD_HEADS
    rep = lambda v: jnp.repeat(v, SSD_HEAD_DIM, axis=-1)
    w_perm = jnp.concatenate([w_in[:, :o_dt], w_in[:, o_f:], rep(w_in[:, o_dt:o_dt + SSD_HEADS]),
                              rep(w_in[:, o_dt + SSD_HEADS:o_f])], axis=1).astype(jnp.bfloat16)
    both = lambda a, b: jnp.stack([a.reshape(-1), b.reshape(-1)])
    p_all = norm_proj(x_all, g, both(mod_c[0], mod[0]), both(mod_c[1], mod[1]), w_perm, n_first=n_ctx)
    dtb = rep(dt_bias.astype(f32)).reshape(2, 1, SSD_W)
    a_cols = rep(-jnp.exp(a_log.astype(f32)) * math.log2(math.e)).reshape(2, 1, SSD_W)
    dsk = rep(d_skip.astype(f32)).reshape(2, 1, SSD_W)
    yo = _odd_scan(p_all, conv_w, conv_b, dtb, a_cols, dsk, lb.astype(f32), n_ctx)
    return _odd_merge(yo, p_all, x_all, ssd_norm, hg_norm, gt, w_out, n_ctx)


MOE_TM = 256
MOE_BM = 256
NEG_BIG = -1e30
MOE_ISSUE_UNROLL = 8
MOE_TILE_ROWS = 64
MOE_VMEM_LIMIT = 52 * 1024 * 1024


def _route_body(x_ref, g_ref, sh_ref, sc_ref, rw_ref, rb_ref, cnt0_ref,
                h_ref, idx_ref, gate_ref, rank_ref, cnt_ref, run_ref):
    i = pl.program_id(0)

    @pl.when(i == 0)
    def _():
        run_ref[...] = cnt0_ref[...]

    tm = x_ref.shape[0]
    x = x_ref[...]
    t = (x * lax.rsqrt(jnp.mean(x * x, axis=-1, keepdims=True) + EPS) * g_ref[...]) * (1.0 + sc_ref[...]) + sh_ref[...]
    h_ref[...] = t
    logits = jnp.dot(t.astype(jnp.bfloat16), rw_ref[...].astype(jnp.bfloat16),
                     preferred_element_type=jnp.float32) + rb_ref[...]
    lane = lax.broadcasted_iota(jnp.int32, (tm, LANE), 1)
    lane_f = lane.astype(jnp.float32)
    work = logits
    vals, sels, hots = [], [], []
    for _ in range(TOP_K):
        m = jnp.max(work, axis=-1, keepdims=True)
        sel = jnp.min(jnp.where(work == m, lane_f, float(LANE)), axis=-1, keepdims=True)
        hot = lane_f == sel
        vals.append(m)
        sels.append(sel.astype(jnp.int32))
        hots.append(hot)
        work = jnp.where(hot, -jnp.inf, work)
    exps = [jnp.exp(v - vals[0]) for v in vals]
    denom = exps[0] + exps[1] + exps[2] + exps[3]
    chosen = jnp.zeros((tm, LANE), jnp.float32)
    for hot in hots:
        chosen = chosen + hot.astype(jnp.float32)
    row = lax.broadcasted_iota(jnp.int32, (tm, tm), 0)
    col = lax.broadcasted_iota(jnp.int32, (tm, tm), 1)
    tri = (row > col).astype(jnp.bfloat16)
    before = jnp.dot(tri, chosen.astype(jnp.bfloat16), preferred_element_type=jnp.float32) + run_ref[0:1, :]
    idx_out = jnp.zeros((tm, LANE), jnp.int32)
    gate_out = jnp.zeros((tm, LANE), jnp.float32)
    rank_out = jnp.zeros((tm, LANE), jnp.int32)
    for k in range(TOP_K):
        rank_k = jnp.sum(jnp.where(hots[k], before, 0.0), axis=-1, keepdims=True).astype(jnp.int32)
        idx_out = jnp.where(lane == k, sels[k], idx_out)
        gate_out = jnp.where(lane == k, exps[k] / denom, gate_out)
        rank_out = jnp.where(lane == k, rank_k, rank_out)
    idx_ref[...] = idx_out
    gate_ref[...] = gate_out
    rank_ref[...] = rank_out
    run_new = run_ref[0:1, :] + jnp.sum(chosen, axis=0, keepdims=True)
    run_ref[...] = jnp.broadcast_to(run_new, run_ref.shape)
    cnt_ref[...] = jnp.broadcast_to(run_new, cnt_ref.shape)


def _moe_route(x2d, g, shift, scale, router_w, router_b, cnt0):
    n_tok, d = x2d.shape
    f32 = jnp.float32
    rw = jnp.pad(router_w, ((0, 0), (0, LANE - N_EXPERTS)))
    rb = jnp.pad(router_b.astype(f32), (0, LANE - N_EXPERTS), constant_values=NEG_BIG).reshape(1, LANE)
    tile = pl.BlockSpec((MOE_TM, LANE), lambda i: (i, 0))
    wide = pl.BlockSpec((MOE_TM, d), lambda i: (i, 0))
    vec = pl.BlockSpec((1, d), lambda i: (0, 0))
    small = pl.BlockSpec((SUBLANE, LANE), lambda i: (0, 0))
    row = lambda v: v.astype(f32).reshape(1, d)
    h, idx, gate, rank, cnt = pl.pallas_call(
        _route_body,
        grid=(n_tok // MOE_TM,),
        in_specs=[wide, vec, vec, vec, pl.BlockSpec((d, LANE), lambda i: (0, 0)),
                  pl.BlockSpec((1, LANE), lambda i: (0, 0)), small],
        out_specs=[wide, tile, tile, tile, small],
        out_shape=[jax.ShapeDtypeStruct((n_tok, d), f32),
                   jax.ShapeDtypeStruct((n_tok, LANE), jnp.int32),
                   jax.ShapeDtypeStruct((n_tok, LANE), f32),
                   jax.ShapeDtypeStruct((n_tok, LANE), jnp.int32),
                   jax.ShapeDtypeStruct((SUBLANE, LANE), f32)],
        scratch_shapes=[pltpu.VMEM((SUBLANE, LANE), f32)],
        compiler_params=pltpu.CompilerParams(dimension_semantics=("arbitrary",)),
        name="moe_route",
    )(x2d, row(g), row(shift), row(scale), rw, rb, cnt0)
    return h, idx[:, :TOP_K], gate, rank[:, :TOP_K], cnt


def _rows_to_tiles(x):
    return _transpose8([x[:, LANE * s:LANE * (s + 1)] for s in range(SUBLANE)])


def _store_tiles(ref, tiles):
    groups = ref.shape[0] // SUBLANE
    for j in range(SUBLANE):
        ref[pl.ds(j, groups, stride=SUBLANE)] = tiles[j].reshape(groups, SUBLANE, LANE)


def _load_rows(ref):
    groups = ref.shape[0] // SUBLANE
    tiles = [ref[pl.ds(j, groups, stride=SUBLANE)].reshape(groups * SUBLANE, LANE) for j in range(SUBLANE)]
    return jnp.concatenate(_transpose8(tiles), axis=1)


def _scatter_body(off_ref, pad_ref, dest_ref, *rest, tile_starts):
    t_refs = rest[:-4]
    xs_ref, rows_ref, zero_ref, sem = rest[-4:]
    i = pl.program_id(0)
    tm = t_refs[0].shape[0]

    @pl.when(i == 0)
    def _():
        zero_ref[...] = jnp.zeros_like(zero_ref)
        used = off_ref[N_EXPERTS - 1] + pad_ref[N_EXPERTS - 1]
        n_rows = xs_ref.shape[0]

        def zero_block(start):
            return pltpu.make_async_copy(zero_ref, xs_ref.at[pl.ds(start, MOE_BM)], sem)

        for e in range(N_EXPERTS):
            tail = n_rows - (e + 1) * MOE_BM

            @pl.when(pad_ref[e] > 0)
            def _():
                zero_block(off_ref[e] + pad_ref[e] - MOE_BM).start()

            @pl.when(tail >= used)
            def _():
                zero_block(tail).start()
        for e in range(N_EXPERTS):
            tail = n_rows - (e + 1) * MOE_BM

            @pl.when(pad_ref[e] > 0)
            def _():
                zero_block(0).wait()

            @pl.when(tail >= used)
            def _():
                zero_block(0).wait()

    for s, t_ref in enumerate(t_refs):
        @pl.when((i >= tile_starts[s]) & (i < tile_starts[s + 1]))
        def _():
            for r0 in range(0, tm, MOE_TILE_ROWS):
                _store_tiles(rows_ref.at[pl.ds(r0, MOE_TILE_ROWS)],
                             _rows_to_tiles(t_ref[pl.ds(r0, MOE_TILE_ROWS), :]))

    def issue(t, carry):
        for k in range(TOP_K):
            pltpu.make_async_copy(rows_ref.at[t], xs_ref.at[dest_ref[TOP_K * t + k]], sem).start(priority=k % 2)
        return carry

    lax.fori_loop(0, tm, issue, 0, unroll=MOE_ISSUE_UNROLL)
    for _ in range(TOP_K):
        pltpu.make_async_copy(rows_ref, xs_ref.at[pl.ds(0, tm)], sem).wait()


def _stream_tiles(streams):
    starts = [0]
    for t in streams:
        starts.append(starts[-1] + t.shape[0] // MOE_TM)
    return tuple(starts)


def _stream_spec(width, starts, s, extra=0):
    lo, n = starts[s], starts[s + 1] - starts[s]
    return pl.BlockSpec((MOE_TM, width), lambda i, *_: (jnp.clip(i + extra - lo, 0, n - 1), 0))


def _moe_scatter(hs, dest_flat, off, padded, n_rows):
    d = hs[0].shape[1]
    assert d == SUBLANE * LANE
    starts = _stream_tiles(hs)
    in_specs = [pl.BlockSpec((TOP_K * MOE_TM,), lambda i, off, pad: (i,), memory_space=pltpu.SMEM)]
    in_specs += [_stream_spec(d, starts, s) for s in range(len(hs))]
    return pl.pallas_call(
        functools.partial(_scatter_body, tile_starts=starts),
        grid_spec=pltpu.PrefetchScalarGridSpec(
            num_scalar_prefetch=2,
            grid=(starts[-1],),
            in_specs=in_specs,
            out_specs=pl.BlockSpec(memory_space=pl.ANY),
            scratch_shapes=[pltpu.VMEM((MOE_TM, SUBLANE, LANE), jnp.float32),
                            pltpu.VMEM((MOE_BM, SUBLANE, LANE), jnp.float32), pltpu.SemaphoreType.DMA],
        ),
        out_shape=jax.ShapeDtypeStruct((n_rows, SUBLANE, LANE), jnp.float32),
        compiler_params=pltpu.CompilerParams(dimension_semantics=("arbitrary",)),
        name="moe_scatter",
    )(off, padded, dest_flat, *hs)


def _expert_body(blk_e_ref, n_act_ref, nxt_e_ref, x_ref, wgu_hbm, bgu_ref, wdn_hbm, bdn_ref, y_ref,
                 wgu_f32, wdn_f32, wgu_bf, wdn_bf, x_bf, y_buf, sems, slot_ref, *, layer):
    step = pl.program_id(0)
    n_act = n_act_ref[0]
    i = step - 1

    def weight_copies(e, slot):
        return (pltpu.make_async_copy(wgu_hbm.at[layer, e], wgu_f32.at[slot], sems.at[slot, 0]),
                pltpu.make_async_copy(wdn_hbm.at[layer, e], wdn_f32.at[slot], sems.at[slot, 1]))

    @pl.when(step == 0)
    def _():
        x_bf[...] = jnp.zeros_like(x_bf)
        y_buf[...] = jnp.zeros_like(y_buf)
        wgu_bf[...] = jnp.zeros_like(wgu_bf)
        wdn_bf[...] = jnp.zeros_like(wdn_bf)

    @pl.when((i >= 0) & (i < n_act))
    def _():
        e = blk_e_ref[i]
        prev = blk_e_ref[jnp.maximum(i - 1, 0)]

        @pl.when((i == 0) | (e != prev))
        def _():
            @pl.when(i == 0)
            def _():
                slot_ref[0] = 0
                for c in weight_copies(e, 0):
                    c.start()

            slot = slot_ref[0]
            for c in weight_copies(e, slot):
                c.wait()
            wgu_bf[...] = wgu_f32[slot].astype(jnp.bfloat16)
            wdn_bf[...] = wdn_f32[slot].astype(jnp.bfloat16)
            nxt = nxt_e_ref[e]

            @pl.when(nxt >= 0)
            def _():
                for c in weight_copies(nxt, 1 - slot):
                    c.start()

            slot_ref[0] = 1 - slot

    cur, oth = step % 2, (step + 1) % 2
    for r0 in range(0, MOE_BM, MOE_TILE_ROWS):
        x_bf[cur, r0:r0 + MOE_TILE_ROWS] = _load_rows(x_ref.at[pl.ds(r0, MOE_TILE_ROWS)]).astype(jnp.bfloat16)
    gu = jnp.dot(x_bf[oth], wgu_bf[...], preferred_element_type=jnp.float32) + bgu_ref[...]
    gate = jnp.minimum(gu[:, :D_EXPERT], SWIGLU_LIMIT)
    up = jnp.clip(gu[:, D_EXPERT:], -SWIGLU_LIMIT, SWIGLU_LIMIT)
    act = (up + 1.0) * gate * jax.nn.sigmoid(SWIGLU_ALPHA * gate)
    y_buf[oth] = jnp.dot(act.astype(jnp.bfloat16), wdn_bf[...], preferred_element_type=jnp.float32) + bdn_ref[...]
    done = step - 2
    keep = (done >= 0) & (done < n_act)
    for r0 in range(0, MOE_BM, MOE_TILE_ROWS):
        tiles = _rows_to_tiles(y_buf[cur, r0:r0 + MOE_TILE_ROWS])
        _store_tiles(y_ref.at[pl.ds(r0, MOE_TILE_ROWS)], [jnp.where(keep, t, 0.0) for t in tiles])


def _moe_experts(xs, blk_e, n_act, nxt_e, w_gu, b_gu, w_dn, b_dn, layer):
    n_rows = xs.shape[0]
    d = SUBLANE * LANE
    n_blk = n_rows // MOE_BM

    def blk(i, be, na, nx):
        return jnp.clip(i, 0, na[0] - 1)

    return pl.pallas_call(
        functools.partial(_expert_body, layer=layer),
        grid_spec=pltpu.PrefetchScalarGridSpec(
            num_scalar_prefetch=3,
            grid=(n_blk + 2,),
            in_specs=[pl.BlockSpec((MOE_BM, SUBLANE, LANE), lambda s, be, na, nx: (blk(s, be, na, nx), 0, 0)),
                      pl.BlockSpec(memory_space=pl.ANY),
                      pl.BlockSpec((None, 1, 2 * D_EXPERT), lambda s, be, na, nx: (be[blk(s - 1, be, na, nx)], 0, 0)),
                      pl.BlockSpec(memory_space=pl.ANY),
                      pl.BlockSpec((None, 1, d), lambda s, be, na, nx: (be[blk(s - 1, be, na, nx)], 0, 0))],
            out_specs=pl.BlockSpec((MOE_BM, SUBLANE, LANE),
                                   lambda s, be, na, nx: (jnp.clip(s - 2, 0, n_blk - 1), 0, 0)),
            scratch_shapes=[pltpu.VMEM((2, d, 2 * D_EXPERT), jnp.float32),
                            pltpu.VMEM((2, D_EXPERT, d), jnp.float32),
                            pltpu.VMEM((d, 2 * D_EXPERT), jnp.bfloat16),
                            pltpu.VMEM((D_EXPERT, d), jnp.bfloat16),
                            pltpu.VMEM((2, MOE_BM, d), jnp.bfloat16),
                            pltpu.VMEM((2, MOE_BM, d), jnp.float32),
                            pltpu.SemaphoreType.DMA((2, 2)),
                            pltpu.SMEM((1,), jnp.int32)],
        ),
        out_shape=jax.ShapeDtypeStruct((n_rows, SUBLANE, LANE), jnp.float32),
        compiler_params=pltpu.CompilerParams(dimension_semantics=("arbitrary",),
                                             vmem_limit_bytes=MOE_VMEM_LIMIT),
        name="moe_experts",
    )(blk_e, n_act, nxt_e, xs, w_gu, b_gu.reshape(N_EXPERTS, 1, -1), w_dn, b_dn.reshape(N_EXPERTS, 1, -1))


def _combine_body(dest_ref, dest_nxt_ref, *rest, tile_starts):
    n_streams = len(tile_starts) - 1
    ins, ys_ref = rest[:3 * n_streams], rest[3 * n_streams]
    y_refs = rest[3 * n_streams + 1:-2]
    if len(y_refs) == 1:
        y_refs = y_refs * n_streams
    buf_ref, sems = rest[-2:]
    i = pl.program_id(0)
    n = pl.num_programs(0)
    tm = y_refs[0].shape[0]

    def fetch(d_ref, slot):
        def issue(t, carry):
            for k in range(TOP_K):
                pltpu.make_async_copy(ys_ref.at[d_ref[TOP_K * t + k]], buf_ref.at[slot, k, t],
                                      sems.at[slot]).start(priority=k % 2)
            return carry
        lax.fori_loop(0, tm, issue, 0, unroll=MOE_ISSUE_UNROLL)

    @pl.when(i == 0)
    def _():
        fetch(dest_ref, 0)

    @pl.when(i + 1 < n)
    def _():
        fetch(dest_nxt_ref, (i + 1) % 2)

    slot = i % 2
    for k in range(TOP_K):
        pltpu.make_async_copy(ys_ref.at[pl.ds(0, tm)], buf_ref.at[slot, k], sems.at[slot]).wait()
    for s in range(n_streams):
        gate_ref, x_ref, gt_ref = ins[3 * s:3 * s + 3]

        @pl.when((i >= tile_starts[s]) & (i < tile_starts[s + 1]))
        def _():
            g = gate_ref[...]
            groups = tm // SUBLANE
            gk = [jnp.broadcast_to(g[:, k:k + 1], (tm, LANE)).reshape(groups, SUBLANE, LANE) for k in range(TOP_K)]
            accs = []
            for j in range(SUBLANE):
                acc = None
                for k in range(TOP_K):
                    splat = jnp.broadcast_to(gk[k][:, j:j + 1, :], (groups, SUBLANE, LANE))
                    term = splat * buf_ref.at[slot, k][pl.ds(j, groups, stride=SUBLANE)]
                    acc = term if acc is None else acc + term
                accs.append(acc.reshape(tm, LANE))
            y_refs[s][...] = x_ref[...] + gt_ref[...] * jnp.concatenate(_transpose8(accs), axis=1)


def _moe_combine(ys, dest_flat, gates, xs2d, gts, joint):
    d = xs2d[0].shape[1]
    starts = _stream_tiles(xs2d)
    n_tiles = starts[-1]
    in_specs = [pl.BlockSpec((TOP_K * MOE_TM,), lambda i: (i,), memory_space=pltpu.SMEM),
                pl.BlockSpec((TOP_K * MOE_TM,), lambda i: (jnp.minimum(i + 1, n_tiles - 1),),
                             memory_space=pltpu.SMEM)]
    args = [dest_flat, dest_flat]
    for s, (gate, x2d, gt) in enumerate(zip(gates, xs2d, gts)):
        in_specs += [_stream_spec(LANE, starts, s), _stream_spec(d, starts, s), pl.BlockSpec((1, d), lambda i: (0, 0))]
        args += [gate, x2d, gt.astype(jnp.float32).reshape(1, d)]
    in_specs.append(pl.BlockSpec(memory_space=pl.ANY))
    args.append(ys)
    if joint:
        out_specs = [pl.BlockSpec((MOE_TM, d), lambda i: (i, 0))]
        out_shape = [jax.ShapeDtypeStruct((n_tiles * MOE_TM, d), jnp.float32)]
    else:
        out_specs = [_stream_spec(d, starts, s) for s in range(len(xs2d))]
        out_shape = [jax.ShapeDtypeStruct(x2d.shape, jnp.float32) for x2d in xs2d]
    return pl.pallas_call(
        functools.partial(_combine_body, tile_starts=starts),
        grid=(n_tiles,),
        in_specs=in_specs,
        out_specs=out_specs,
        out_shape=out_shape,
        scratch_shapes=[pltpu.VMEM((2, TOP_K, MOE_TM, SUBLANE, LANE), jnp.float32),
                        pltpu.SemaphoreType.DMA((2,))],
        compiler_params=pltpu.CompilerParams(dimension_semantics=("arbitrary",)),
        name="moe_combine",
    )(*args)


def moe_layer(streams, g, router_w, router_b, w_gu, b_gu, w_dn, b_dn, layer, joint=False):
    i32 = jnp.int32
    routed = []
    cnt = jnp.zeros((SUBLANE, LANE), jnp.float32)
    for x2d, shift, scale, _ in streams:
        assert x2d.shape[0] % MOE_TM == 0
        h, idx, gate, rank, cnt = _moe_route(x2d, g, shift, scale, router_w, router_b, cnt)
        routed.append((h, idx, gate, rank))
    counts = cnt[0, :N_EXPERTS].astype(i32)
    padded = (counts + MOE_BM - 1) // MOE_BM * MOE_BM
    pad_end = jnp.cumsum(padded)
    off = (pad_end - padded).astype(i32)
    padded = padded.astype(i32)
    n_tok = sum(s[0].shape[0] for s in streams)
    n_blk = -(-(n_tok * TOP_K) // MOE_BM) + N_EXPERTS
    blk_e = jnp.minimum(jnp.sum(jnp.arange(n_blk)[:, None] * MOE_BM >= pad_end[None, :], axis=1),
                        N_EXPERTS - 1).astype(i32)
    n_act = (pad_end[-1:] // MOE_BM).astype(i32)
    experts = jnp.arange(N_EXPERTS, dtype=i32)
    later_with_rows = (padded > 0)[None, :] & (experts[None, :] > experts[:, None])
    nxt_e = jnp.min(jnp.where(later_with_rows, experts[None, :], N_EXPERTS), axis=1)
    nxt_e = jnp.where(nxt_e == N_EXPERTS, -1, nxt_e).astype(i32)
    dests = []
    for _, idx, _, rank in routed:
        dest = rank + jnp.sum(jnp.where(idx[..., None] == experts, off, 0), axis=-1)
        dests.append(dest.reshape(-1).astype(i32))
    dest_flat = jnp.concatenate(dests)
    xs = _moe_scatter([r[0] for r in routed], dest_flat, off, padded, n_blk * MOE_BM)
    ys = _moe_experts(xs, blk_e, n_act, nxt_e, w_gu, b_gu, w_dn, b_dn, layer)
    return _moe_combine(ys, dest_flat, [r[2] for r in routed], [s[0] for s in streams], [s[3] for s in streams],
                        joint)


def kernel(x, c, ctx, c_ctx, norm_g, ada_w, ada_b, w_out, w_in_even, hy_conv_w, hy_conv_b,
           hy_w1, hy_b1, hy_w2, hy_b2, hy_w3, hy_b3, hy_w4, hy_freq, hy_filter_bias,
           att_q_norm, att_k_norm, att_sink, w_in_odd, ssd_conv_w, ssd_conv_b, ssd_dt_bias,
           ssd_A_log, ssd_D, ssd_norm, hg_lower_bounds, hg_norm, router_w, router_b,
           moe_w_gu, moe_b_gu, moe_w_dn, moe_b_dn):
    lbs = jax.nn.softmax(hg_lower_bounds.astype(jnp.float32), axis=0)
    lbs = jnp.cumsum(lbs, axis=0) - lbs[0]
    xc = ctx
    n_ctx = ctx.shape[1]
    for layer in range(DEPTH):
        ctx_out = layer < DEPTH - 1
        i = layer // 2
        sh, sc, gt = adaln(c, ada_w[layer], ada_b[layer], 0)
        sh_c, sc_c, gt_c = adaln(c_ctx, ada_w[layer], ada_b[layer], 0)
        assert (layer % 2 == 0) == ctx_out
        moe = functools.partial(moe_layer, g=norm_g[layer, 1], router_w=router_w[layer], router_b=router_b[layer],
                                w_gu=moe_w_gu, b_gu=moe_b_gu[layer], w_dn=moe_w_dn, b_dn=moe_b_dn[layer], layer=layer)
        sh2, sc2, gt2 = adaln(c, ada_w[layer], ada_b[layer], 1)
        if layer % 2 == 0:
            x, xc = even_layer(x, xc, norm_g[layer, 0], (sh, sc), (sh_c, sc_c), w_in_even[i], hy_conv_w[i],
                               hy_conv_b[i], hy_w1[i], hy_b1[i], hy_w2[i], hy_b2[i], hy_w3[i], hy_b3[i], hy_w4[i],
                               hy_freq[i], hy_filter_bias[i], att_q_norm[i], att_k_norm[i], att_sink[i], gt, gt_c,
                               w_out[layer])
            sh2_c, sc2_c, gt2_c = adaln(c_ctx, ada_w[layer], ada_b[layer], 1)
            x_all = moe([(xc[0], sh2_c, sc2_c, gt2_c), (x[0], sh2, sc2, gt2)], joint=True)[0]
        else:
            x = odd_layer(x_all, n_ctx, norm_g[layer, 0], (sh, sc), (sh_c, sc_c), lbs[layer], w_in_odd[i],
                          ssd_conv_w[i], ssd_conv_b[i], ssd_dt_bias[i], ssd_A_log[i], ssd_D[i], ssd_norm[i],
                          hg_norm[i], gt, w_out[layer])
            x = moe([(x, sh2, sc2, gt2)])[0][None]
    return x
```

```python
import functools
import math

import jax
import jax.numpy as jnp
from jax import lax
from jax.experimental import pallas as pl
from jax.experimental.pallas import tpu as pltpu

D_MODEL = 1024
DEPTH = 2
GRID_W = 64
MIX_W = D_MODEL
EPS = 1e-6
CONV_W = 3

HY_W = MIX_W // 2
HY_ORDER = 2
HY_EMB = 33
HY_FFN = 64
HY_TARGET = 1e-2
HY_SHORT_PCT = 0.3
HY_LONG_PCT = 1.5

HEAD_DIM = 64
ATT_HEADS = (MIX_W // 2) // HEAD_DIM
ATT_KV_HEADS = 2
ATT_GROUP = ATT_HEADS // ATT_KV_HEADS
ATT_WINDOW = 128
ATT_BLOCK = 128
ROPE_BASE = 10000.0
ATT_Q_W = ATT_HEADS * HEAD_DIM
ATT_KV_W = ATT_KV_HEADS * HEAD_DIM

SSD_W = MIX_W // 2
SSD_HEAD_DIM = 64
SSD_HEADS = SSD_W // SSD_HEAD_DIM
SSD_GROUPS = 2
SSD_STATE = 128
SSD_CHUNK = 128
SSD_CONV_CH = SSD_W + 2 * SSD_GROUPS * SSD_STATE

HG_W = MIX_W // 2
HG_EXPAND = 128
HG_HEADS = HG_W // HG_EXPAND
HG_VDIM = HG_W // HG_HEADS
HG_CHUNK = 64

N_EXPERTS = 32
TOP_K = 4
D_EXPERT = D_MODEL
SWIGLU_ALPHA = 1.702
SWIGLU_LIMIT = 7.0
MOE_BLOCK = 128

EVEN_IN = 2 * ATT_KV_W + ATT_Q_W + 3 * HY_W
ODD_STATE_COLS = SSD_CONV_CH + 2 * SSD_HEADS + 3 * HG_W
ODD_IN = ODD_STATE_COLS + SSD_W + 2 * HG_W

LANE = 128
SUBLANE = 8


PROJ_TM = 256
PROJ_VMEM_LIMIT = 56 * 1024 * 1024


def _norm_proj_body(x_ref, g_ref, sh_ref, sc_ref, w_ref, o_ref):
    x = x_ref[...]
    y = x * lax.rsqrt(jnp.mean(x * x, axis=-1, keepdims=True) + EPS) * g_ref[...]
    h = y * (1.0 + sc_ref[...]) + sh_ref[...]
    o_ref[...] = jnp.dot(h.astype(jnp.bfloat16), w_ref[...], preferred_element_type=jnp.float32)


def norm_proj(x2d, g, shift, scale, w_bf16, n_first=0):
    rows, d = x2d.shape
    n = w_bf16.shape[1]
    assert rows % PROJ_TM == 0 and n_first % PROJ_TM == 0 and n % LANE == 0
    first_tiles = n_first // PROJ_TM
    mod = pl.BlockSpec((None, 1, d), lambda i: (jnp.where(i < first_tiles, 0, 1), 0, 0))
    return pl.pallas_call(
        _norm_proj_body,
        grid=(rows // PROJ_TM,),
        in_specs=[pl.BlockSpec((PROJ_TM, d), lambda i: (i, 0)),
                  pl.BlockSpec((1, d), lambda i: (0, 0)), mod, mod,
                  pl.BlockSpec((d, n), lambda i: (0, 0))],
        out_specs=pl.BlockSpec((PROJ_TM, n), lambda i: (i, 0)),
        out_shape=jax.ShapeDtypeStruct((rows, n), jnp.float32),
        compiler_params=pltpu.CompilerParams(dimension_semantics=("arbitrary",),
                                             vmem_limit_bytes=PROJ_VMEM_LIMIT),
        name="norm_proj",
    )(x2d, g.astype(jnp.float32).reshape(1, d), shift.astype(jnp.float32).reshape(2, 1, d),
      scale.astype(jnp.float32).reshape(2, 1, d), w_bf16)


def adaln(cond, w, b, j):
    lo, hi = 3 * j * D_MODEL, 3 * (j + 1) * D_MODEL
    m = jax.nn.silu(cond) @ w[:, lo:hi] + b[lo:hi]
    return jnp.split(m, 3, axis=-1)


def axial_rope_tables(length):
    rows = length // GRID_W
    n_pairs = HEAD_DIM // 4
    inv = ROPE_BASE ** (-jnp.arange(n_pairs, dtype=jnp.float32) / n_pairs)
    row_ang = jnp.arange(rows, dtype=jnp.float32)[:, None] * inv
    col_ang = jnp.arange(GRID_W, dtype=jnp.float32)[:, None] * inv
    ang_r = jnp.broadcast_to(row_ang[:, None], (rows, GRID_W, n_pairs)).reshape(length, n_pairs)
    ang_c = jnp.broadcast_to(col_ang[None], (rows, GRID_W, n_pairs)).reshape(length, n_pairs)
    return jnp.cos(ang_r), jnp.sin(ang_r), jnp.cos(ang_c), jnp.sin(ang_c)


HY_N1 = 128
HY_N2 = 256
HY_NB = 8
HY_TM = 512


def _hy_dft_constants(length):
    import numpy as np
    n = 2 * length
    assert n == HY_N1 * HY_N2
    half = HY_N1 // 2
    k1 = np.arange(HY_N1)[:, None]
    n1 = np.arange(half)[None, :]
    n2 = np.arange(HY_N2)[:, None, None]
    ang = 2 * np.pi * (k1 * n1 / HY_N1)[None] + 2 * np.pi * n2 * k1[None] / n
    fwd = np.stack([np.cos(ang), -np.sin(ang)], axis=2).reshape(HY_N2, 2 * HY_N1, half)
    inv = fwd.transpose(0, 2, 1)
    a2 = 2 * np.pi * np.outer(np.arange(HY_N2), np.arange(HY_N2)) / HY_N2
    c, s = np.cos(a2), -np.sin(a2)
    m_fwd = np.concatenate([np.stack([c, -s], axis=2).reshape(HY_N2, 2 * HY_N2),
                            np.stack([s, c], axis=2).reshape(HY_N2, 2 * HY_N2)], axis=0)
    m_inv = np.stack([np.concatenate([c, s], axis=1), np.concatenate([-s, c], axis=1)],
                     axis=1).reshape(2 * HY_N2, 2 * HY_N2)
    f = np.float32
    return fwd.astype(f), inv.astype(f), m_fwd.astype(f), m_inv.astype(f)


def _hy_conv_body(p_ref, prev_ref, next_ref, w_ref, b_ref, o_ref):
    i = pl.program_id(1)
    n = pl.num_programs(1)
    u = p_ref[...]
    rows = u.shape[0]
    row = lax.broadcasted_iota(jnp.int32, u.shape, 0)
    before = jnp.where(i == 0, 0.0, prev_ref[HALO - 1:HALO, :])
    after = jnp.where(i == n - 1, 0.0, next_ref[0:1, :])
    up = jnp.where(row == 0, before, pltpu.roll(u, 1, 0))
    un = jnp.where(row == rows - 1, after, pltpu.roll(u, rows - 1, 0))
    o_ref[...] = w_ref[0:1, :] * up + w_ref[1:2, :] * u + w_ref[2:3, :] * un + b_ref[...]


def _hy_conv(p, conv_w, conv_b, tm):
    length = p.shape[0]
    per = tm // HALO
    last = length // HALO - 1
    return pl.pallas_call(
        _hy_conv_body,
        grid=(3, length // tm),
        in_specs=[pl.BlockSpec((tm, HY_W), lambda c, i: (i, c)),
                  pl.BlockSpec((HALO, HY_W), lambda c, i: (jnp.maximum(i * per - 1, 0), c)),
                  pl.BlockSpec((HALO, HY_W), lambda c, i: (jnp.minimum((i + 1) * per, last), c)),
                  pl.BlockSpec((CONV_W, HY_W), lambda c, i: (0, c)),
                  pl.BlockSpec((1, HY_W), lambda c, i: (0, c))],
        out_specs=pl.BlockSpec((None, tm, HY_W), lambda c, i: (c, i, 0)),
        out_shape=jax.ShapeDtypeStruct((3, length, HY_W), jnp.float32),
        name="hy_conv",
    )(p, p, p, conv_w, conv_b.reshape(1, -1))


def _hy_filter_body(z_ref, w1_ref, b1_ref, w2_ref, b2_ref, w3_ref, b3_ref, w4_ref, fq_ref, dl_ref,
                    e_ref, o_ref, nrm_ref):
    i = pl.program_id(0)
    bf = jnp.bfloat16
    z = z_ref[...]
    fq = fq_ref[...]

    def layer(a, w_ref, b_ref):
        return jnp.sin(fq * (jnp.dot(a.astype(bf), w_ref[...].astype(bf), preferred_element_type=jnp.float32)
                             + b_ref[...]))

    half = z.shape[0] // 2
    hdn = layer(layer(layer(jnp.concatenate([z[:half], z[half:]], axis=1), w1_ref, b1_ref), w2_ref, b2_ref),
                w3_ref, b3_ref).astype(bf)
    w4 = w4_ref[...].astype(bf)
    h = jnp.concatenate([jnp.dot(hdn[:, :HY_FFN], w4, preferred_element_type=jnp.float32),
                         jnp.dot(hdn[:, HY_FFN:], w4, preferred_element_type=jnp.float32)], axis=0)
    decay = jnp.exp(-z[:, 0:1] * dl_ref[...])
    first = (lax.broadcasted_iota(jnp.int32, decay.shape, 0) == 0) & (i == 0)
    acc = []
    for order in range(HY_ORDER):
        lo = order * 2 * HY_W
        h0 = h[:, lo:lo + HY_W] * decay
        h1 = jnp.where(first, 0.0, h[:, lo + HY_W:lo + 2 * HY_W] * decay)
        e_ref[:, order * HY_W:(order + 1) * HY_W] = h0 + h1
        o_ref[:, order * HY_W:(order + 1) * HY_W] = h0 - h1
        acc.append(jnp.sum(jnp.abs(h0) + jnp.abs(h1), axis=0, keepdims=True))
    part = jnp.concatenate(acc, axis=1)

    @pl.when(i == 0)
    def _():
        nrm_ref[...] = jnp.zeros_like(nrm_ref)

    nrm_ref[...] = nrm_ref[...] + part


def _hy_filter_time(length, w1, b1, w2, b2, w3, b3, w4, freq, tm):
    f32 = jnp.float32
    t = jnp.linspace(0.0, 1.0, length, dtype=f32)[:, None]
    bands = (HY_EMB - 1) // 2
    w_ang = 2.0 * math.pi * jnp.arange(length, dtype=f32)[:, None] / length
    fr = jnp.linspace(1e-4, bands - 1, bands, dtype=f32)[None]
    z = jnp.concatenate([t, jnp.cos(fr * w_ang), -jnp.sin(fr * w_ang)], axis=-1)
    z = jnp.pad(z, ((0, 0), (0, LANE - HY_EMB)))
    w1p = jnp.pad(w1.astype(f32), ((0, LANE - HY_EMB), (0, 0)))
    max_decay = math.log(HY_TARGET) / HY_SHORT_PCT
    min_decay = math.log(HY_TARGET) / HY_LONG_PCT
    deltas = jnp.abs(jnp.linspace(min_decay, max_decay, HY_W, dtype=f32)).reshape(1, HY_W)
    full = lambda a: pl.BlockSpec(a.shape, lambda i: (0,) * a.ndim)
    row2 = lambda v: jnp.tile(v.astype(f32).reshape(1, -1), (1, 2))
    diag2 = lambda w: jnp.kron(jnp.eye(2, dtype=f32), w.astype(f32))
    args = (z, diag2(w1p), row2(b1), diag2(w2), row2(b2), diag2(w3), row2(b3), w4.astype(f32), row2(freq), deltas)
    ow = HY_ORDER * HY_W
    return pl.pallas_call(
        _hy_filter_body,
        grid=(length // tm,),
        in_specs=[pl.BlockSpec((tm, LANE), lambda i: (i, 0))] + [full(a) for a in args[1:]],
        out_specs=[pl.BlockSpec((tm, ow), lambda i: (i, 0)), pl.BlockSpec((tm, ow), lambda i: (i, 0)),
                   pl.BlockSpec((1, ow), lambda i: (0, 0))],
        out_shape=[jax.ShapeDtypeStruct((length, ow), f32), jax.ShapeDtypeStruct((length, ow), f32),
                   jax.ShapeDtypeStruct((1, ow), f32)],
        compiler_params=pltpu.CompilerParams(dimension_semantics=("arbitrary",)),
        name="hy_filter_time",
    )(*args)


def _words(x_bf16):
    return pltpu.bitcast(x_bf16, jnp.uint32)


def _halves(w_u32):
    return pltpu.bitcast(w_u32, jnp.bfloat16)


def _transpose8(parts):
    rows, cols = parts[0].shape
    parts = [p.reshape(rows // SUBLANE, SUBLANE, cols) for p in parts]
    row = lax.broadcasted_iota(jnp.int32, parts[0].shape, 1)
    for s in (1, 2, 4):
        keep = (row & s) == 0
        nxt = list(parts)
        for i in range(SUBLANE):
            if i & s == 0:
                a, b = parts[i], parts[i + s]
                nxt[i] = jnp.where(keep, a, pltpu.roll(b, s, 1))
                nxt[i + s] = jnp.where(keep, pltpu.roll(a, SUBLANE - s, 1), b)
        parts = nxt
    return [p.reshape(rows, cols) for p in parts]


def _gather_tiles(ref, j, n):
    return jnp.concatenate([ref[SUBLANE * g + j] for g in range(n)], axis=0)


def _hy_s1_body(x_ref, f_ref, a_ref):
    groups = x_ref.shape[0] // SUBLANE
    xs = _transpose8([_gather_tiles(x_ref, j, groups) for j in range(SUBLANE)])
    words = []
    for i in range(HY_NB):
        acc = jnp.dot(f_ref[i], xs[i].astype(jnp.bfloat16), preferred_element_type=jnp.float32)
        words.append(_words(acc.astype(jnp.bfloat16)))
    tiles = _transpose8(words)
    per = HY_N1 // SUBLANE
    for j in range(SUBLANE):
        for g in range(per):
            a_ref[per * j + g] = tiles[j][SUBLANE * g:SUBLANE * (g + 1)]


def _hy_s1(x4, sel, fwd):
    assert HY_NB == SUBLANE
    _, half, _, width = x4.shape
    return pl.pallas_call(
        _hy_s1_body,
        grid=(HY_N2 // HY_NB,),
        in_specs=[pl.BlockSpec((None, half, HY_NB, width), lambda j: (sel, 0, j, 0)),
                  pl.BlockSpec((HY_NB, 2 * HY_N1, half), lambda j: (j, 0, 0))],
        out_specs=pl.BlockSpec((HY_N1, HY_NB, width), lambda j: (0, j, 0)),
        out_shape=jax.ShapeDtypeStruct((HY_N1, HY_N2, width), jnp.uint32),
        name="hy_stage1",
    )(x4, fwd)


HY_SLABS = 2


def _hy_s2f_body(ae_ref, ao_ref, m_ref, sc_ref, hr_ref, hi_ref):
    dot = lambda a, b: jnp.dot(a, b, preferred_element_type=jnp.float32)
    for s in range(HY_SLABS):
        hr_ref[s] = (dot(m_ref[:HY_N2, :], _halves(ae_ref[s])) * sc_ref[...]).astype(hr_ref.dtype)
        hi_ref[s] = (dot(m_ref[HY_N2:, :], _halves(ao_ref[s])) * sc_ref[...]).astype(hi_ref.dtype)


def _hy_s2f(a_e, a_o, m_fwd, scale):
    width = scale.shape[1]
    slab = pl.BlockSpec((HY_SLABS, HY_N2, width), lambda k: (k, 0, 0))
    out = jax.ShapeDtypeStruct((HY_N1, HY_N2, width), jnp.bfloat16)
    return pl.pallas_call(
        _hy_s2f_body,
        grid=(HY_N1 // HY_SLABS,),
        in_specs=[slab, slab, pl.BlockSpec((2 * HY_N2, 2 * HY_N2), lambda k: (0, 0)),
                  pl.BlockSpec((1, width), lambda k: (0, 0))],
        out_specs=[slab, slab],
        out_shape=[out, out],
        name="hy_filter_stage2",
    )(a_e, a_o, m_fwd, scale)


def _hy_s2_body(a_ref, hr_ref, hi_ref, mf_ref, mi_ref, b_ref):
    bf = jnp.bfloat16
    for s in range(HY_SLABS):
        x = jnp.dot(mf_ref[...], _halves(a_ref[s]), preferred_element_type=jnp.float32)
        xr, xi = x[:HY_N2], x[HY_N2:]
        hr = hr_ref[s].astype(jnp.float32)
        hi = hi_ref[s].astype(jnp.float32)
        y = jnp.concatenate([(xr * hr - xi * hi).astype(bf), (xr * hi + xi * hr).astype(bf)], axis=0)
        b = jnp.dot(mi_ref[...], y, preferred_element_type=jnp.float32)
        b_ref[s] = _words(b.astype(bf))


def _hy_s2(a, h_re, h_im, m_fwd, m_inv, order):
    slab = pl.BlockSpec((HY_SLABS, HY_N2, HY_W), lambda k: (k, 0, 0))
    hslab = pl.BlockSpec((HY_SLABS, HY_N2, HY_W), lambda k: (k, 0, order))
    mat = pl.BlockSpec((2 * HY_N2, 2 * HY_N2), lambda k: (0, 0))
    return pl.pallas_call(
        _hy_s2_body,
        grid=(HY_N1 // HY_SLABS,),
        in_specs=[slab, hslab, hslab, mat, mat],
        out_specs=slab,
        out_shape=jax.ShapeDtypeStruct((HY_N1, HY_N2, HY_W), jnp.uint32),
        name="hy_stage2",
    )(a, h_re, h_im, m_fwd, m_inv)


def _hy_is1_body(b_ref, g_ref, z_ref, gate_ref, bias_ref, o_ref):
    per = HY_N1 // SUBLANE
    spectra = _transpose8([jnp.concatenate([b_ref[per * j + g] for g in range(per)], axis=0)
                           for j in range(SUBLANE)])
    convs = [jnp.dot(g_ref[i], _halves(spectra[i]), preferred_element_type=jnp.float32) for i in range(HY_NB)]
    tiles = _transpose8(convs)
    for j in range(SUBLANE):
        for g in range(o_ref.shape[0] // SUBLANE):
            n1 = SUBLANE * g + j
            o_ref[n1] = gate_ref[n1] * (tiles[j][SUBLANE * g:SUBLANE * (g + 1)] + z_ref[n1] * bias_ref[...])


def _hy_is1(b, inv, z4, z_sel, gate4, gate_sel, bias):
    _, half, _, width = z4.shape
    real = lambda sel: pl.BlockSpec((None, half, HY_NB, width), lambda j: (sel, 0, j, 0))
    return pl.pallas_call(
        _hy_is1_body,
        grid=(HY_N2 // HY_NB,),
        in_specs=[pl.BlockSpec((HY_N1, HY_NB, width), lambda j: (0, j, 0)),
                  pl.BlockSpec((HY_NB, half, 2 * HY_N1), lambda j: (j, 0, 0)),
                  real(z_sel), real(gate_sel), pl.BlockSpec((1, width), lambda j: (0, 0))],
        out_specs=pl.BlockSpec((half, HY_NB, width), lambda j: (0, j, 0)),
        out_shape=jax.ShapeDtypeStruct((half, HY_N2, width), jnp.float32),
        name="hy_inv_stage1",
    )(b, inv, z4, gate4, bias)


def hyena_long(p_hy, f_w1, f_b1, f_w2, f_b2, f_w3, f_b3, f_w4, f_freq, f_bias, conv_w, conv_b):
    length = p_hy.shape[0]
    bf = jnp.bfloat16
    half = HY_N1 // 2
    fwd, inv, m_fwd, m_inv = (jnp.asarray(m, bf) for m in _hy_dft_constants(length))
    e, od, nrm = _hy_filter_time(length, f_w1, f_b1, f_w2, f_b2, f_w3, f_b3, f_w4, f_freq, HY_TM)
    ow = HY_ORDER * HY_W
    a_e = _hy_s1(e.reshape(1, half, HY_N2, ow), 0, fwd)
    a_o = _hy_s1(od.reshape(1, half, HY_N2, ow), 0, fwd)
    scale = 1.0 / (nrm * (2.0 * length))
    h_re, h_im = _hy_s2f(a_e, a_o, m_fwd, scale)
    u = _hy_conv(p_hy, conv_w, conv_b, HY_TM).reshape(3, half, HY_N2, HY_W)
    z, z_sel = u, 0
    for order in range(HY_ORDER):
        b = _hy_s2(_hy_s1(z, z_sel, fwd), h_re, h_im, m_fwd, m_inv, order)
        z = _hy_is1(b, inv, z, z_sel, u, 1 + order, f_bias[order].astype(jnp.float32).reshape(1, HY_W))[None]
        z_sel = 0
    return z.reshape(length, HY_W)


ATT_TQ = ATT_BLOCK
EVEN_TM = 256
EV_Q_BLK = 3 * HY_W // ATT_Q_W
EV_K_BLK = (3 * HY_W + ATT_Q_W) // ATT_KV_W
EV_V_BLK = EV_K_BLK + 1


def _rope_tables(length):
    cr, sr, cc, sc = axial_rope_tables(length)
    return jnp.concatenate([cr, cr, cc, cc], axis=-1), jnp.concatenate([-sr, sr, -sc, sc], axis=-1)


def _head_norm_rope(x, g_row, c, s, seg):
    sq = x * x
    hi = sq.astype(jnp.bfloat16)
    lo = (sq - hi.astype(jnp.float32)).astype(jnp.bfloat16)
    ms = (jnp.dot(hi, seg, preferred_element_type=jnp.float32) + jnp.dot(lo, seg, preferred_element_type=jnp.float32))
    y = x * lax.rsqrt(ms + EPS) * g_row
    width = x.shape[1]
    quarter = HEAD_DIM // 4
    lane = lax.broadcasted_iota(jnp.int32, x.shape, 1)
    partner = jnp.where((lane & quarter) == 0, pltpu.roll(y, width - quarter, 1), pltpu.roll(y, quarter, 1))
    return y * c + partner * s


def _qk_prep_body(q_ref, k_ref, c_ref, s_ref, qn_ref, kn_ref, segq_ref, segk_ref, qo_ref, ko_ref):
    c, s = c_ref[...], s_ref[...]
    tile = lambda t, n: jnp.concatenate([t] * n, axis=1)
    qo_ref[...] = _head_norm_rope(q_ref[...], tile(qn_ref[...], ATT_HEADS), tile(c, ATT_HEADS), tile(s, ATT_HEADS),
                                  segq_ref[...]).astype(qo_ref.dtype)
    ko_ref[...] = _head_norm_rope(k_ref[...], tile(kn_ref[...], ATT_KV_HEADS), tile(c, ATT_KV_HEADS),
                                  tile(s, ATT_KV_HEADS), segk_ref[...]).astype(ko_ref.dtype)


def _qk_prep(p, rope_c, rope_s, q_norm, k_norm, tm):
    import numpy as np
    rows = p.shape[0]
    seg = lambda w: jnp.asarray(np.kron(np.eye(w // HEAD_DIM), np.full((HEAD_DIM, HEAD_DIM), 1.0 / HEAD_DIM)),
                                jnp.bfloat16)
    const = lambda shape: pl.BlockSpec(shape, lambda i: (0, 0))
    return pl.pallas_call(
        _qk_prep_body,
        grid=(rows // tm,),
        in_specs=[pl.BlockSpec((tm, ATT_Q_W), lambda i: (i, EV_Q_BLK)),
                  pl.BlockSpec((tm, ATT_KV_W), lambda i: (i, EV_K_BLK)),
                  pl.BlockSpec((tm, HEAD_DIM), lambda i: (i, 0)), pl.BlockSpec((tm, HEAD_DIM), lambda i: (i, 0)),
                  const((1, HEAD_DIM)), const((1, HEAD_DIM)),
                  const((ATT_Q_W, ATT_Q_W)), const((ATT_KV_W, ATT_KV_W))],
        out_specs=[pl.BlockSpec((tm, ATT_Q_W), lambda i: (i, 0)), pl.BlockSpec((tm, ATT_KV_W), lambda i: (i, 0))],
        out_shape=[jax.ShapeDtypeStruct((rows, ATT_Q_W), jnp.bfloat16),
                   jax.ShapeDtypeStruct((rows, ATT_KV_W), jnp.bfloat16)],
        name="qk_prep",
    )(p, p, rope_c, rope_s, q_norm.astype(jnp.float32).reshape(1, -1), k_norm.astype(jnp.float32).reshape(1, -1),
      seg(ATT_Q_W), seg(ATT_KV_W))


def _att_body(sink_ref, q_ref, kc_ref, vc_ref, *rest, local):
    if local:
        kp_ref, k0_ref, kn_ref, vp_ref, v0_ref, vn_ref, o_ref = rest
    else:
        (o_ref,) = rest
    b = pl.program_id(0)
    nb = pl.num_programs(0)
    bf = jnp.bfloat16
    scale = HEAD_DIM ** -0.5
    q = q_ref[...]
    kc = kc_ref[...]
    vc = vc_ref[...].astype(bf)
    if local:
        kb = jnp.concatenate([kp_ref[...], k0_ref[...], kn_ref[...]], axis=0)
        vb = jnp.concatenate([vp_ref[...], v0_ref[...], vn_ref[...]], axis=0).astype(bf)
        i = lax.broadcasted_iota(jnp.int32, (ATT_TQ, 3 * ATT_BLOCK), 0)
        j = lax.broadcasted_iota(jnp.int32, (ATT_TQ, 3 * ATT_BLOCK), 1)
        rel = j - ATT_BLOCK - i
        valid = ((jnp.abs(rel) <= ATT_WINDOW) & ((b > 0) | (j >= ATT_BLOCK))
                 & ((b < nb - 1) | (j < 2 * ATT_BLOCK)))
    rows = ATT_GROUP * ATT_TQ
    head_of_row = lax.broadcasted_iota(jnp.int32, (rows, 1), 0) // ATT_TQ
    for hk in range(ATT_KV_HEADS):
        heads = range(hk * ATT_GROUP, (hk + 1) * ATT_GROUP)
        qs = jnp.concatenate([q[:, h * HEAD_DIM:(h + 1) * HEAD_DIM] for h in heads], axis=0)
        kv_cols = slice(hk * HEAD_DIM, (hk + 1) * HEAD_DIM)
        sink = jnp.zeros((rows, 1), jnp.float32)
        for g, h in enumerate(heads):
            sink = jnp.where(head_of_row == g, sink_ref[h], sink)
        s_ctx = _dot_nt(qs, kc[:, kv_cols]) * scale
        m = jnp.maximum(jnp.max(s_ctx, axis=-1, keepdims=True), sink)
        if local:
            valid4 = jnp.concatenate([valid] * ATT_GROUP, axis=0)
            s_loc = jnp.where(valid4, _dot_nt(qs, kb[:, kv_cols]) * scale, -jnp.inf)
            m = jnp.maximum(m, jnp.max(s_loc, axis=-1, keepdims=True))
        p_ctx = jnp.exp(s_ctx - m)
        den = jnp.sum(p_ctx, axis=-1, keepdims=True) + jnp.exp(sink - m)
        acc = jnp.dot(p_ctx.astype(bf), vc[:, kv_cols], preferred_element_type=jnp.float32)
        if local:
            p_loc = jnp.exp(s_loc - m)
            den = den + jnp.sum(p_loc, axis=-1, keepdims=True)
            acc = acc + jnp.dot(p_loc.astype(bf), vb[:, kv_cols], preferred_element_type=jnp.float32)
        out = acc / den
        for g, h in enumerate(heads):
            o_ref[:, h * HEAD_DIM:(h + 1) * HEAD_DIM] = out[g * ATT_TQ:(g + 1) * ATT_TQ]


def _attention(qn, kn, p, kcn, pc, sink, local):
    rows = qn.shape[0]
    nb = rows // ATT_TQ
    n_ctx = kcn.shape[0]
    specs = [pl.BlockSpec((ATT_TQ, ATT_Q_W), lambda b, s: (b, 0)),
             pl.BlockSpec((n_ctx, ATT_KV_W), lambda b, s: (0, 0)),
             pl.BlockSpec((n_ctx, ATT_KV_W), lambda b, s: (0, EV_V_BLK))]
    args = [qn, kcn, pc]
    if local:
        prev = lambda b, s: jnp.maximum(b - 1, 0)
        nxt = lambda b, s: jnp.minimum(b + 1, nb - 1)
        for col, arr in ((0, kn), (EV_V_BLK, p)):
            specs += [pl.BlockSpec((ATT_BLOCK, ATT_KV_W), lambda b, s, col=col: (prev(b, s), col)),
                      pl.BlockSpec((ATT_BLOCK, ATT_KV_W), lambda b, s, col=col: (b, col)),
                      pl.BlockSpec((ATT_BLOCK, ATT_KV_W), lambda b, s, col=col: (nxt(b, s), col))]
            args += [arr, arr, arr]
    return pl.pallas_call(
        functools.partial(_att_body, local=local),
        grid_spec=pltpu.PrefetchScalarGridSpec(
            num_scalar_prefetch=1, grid=(nb,), in_specs=specs,
            out_specs=pl.BlockSpec((ATT_TQ, ATT_Q_W), lambda b, s: (b, 0))),
        out_shape=jax.ShapeDtypeStruct((rows, ATT_Q_W), jnp.float32),
        name="window_attention" if local else "context_attention",
    )(sink.astype(jnp.float32), *args)


def _even_merge_body(hy_ref, att_ref, x_ref, gt_ref, w_ref, o_ref):
    bf = jnp.bfloat16
    m = (jnp.dot(hy_ref[...].astype(bf), w_ref[:HY_W, :].astype(bf), preferred_element_type=jnp.float32)
         + jnp.dot(att_ref[...].astype(bf), w_ref[HY_W:, :].astype(bf), preferred_element_type=jnp.float32))
    o_ref[...] = x_ref[...] + gt_ref[...] * m


def _even_merge(hy, att, x2d, gt, w_out, tm):
    rows, d = x2d.shape
    half = pl.BlockSpec((tm, HY_W), lambda i: (i, 0))
    return pl.pallas_call(
        _even_merge_body,
        grid=(rows // tm,),
        in_specs=[half, half, pl.BlockSpec((tm, d), lambda i: (i, 0)), pl.BlockSpec((1, d), lambda i: (0, 0)),
                  pl.BlockSpec((MIX_W, d), lambda i: (0, 0))],
        out_specs=pl.BlockSpec((tm, d), lambda i: (i, 0)),
        out_shape=jax.ShapeDtypeStruct((rows, d), jnp.float32),
        name="even_merge",
    )(hy, att, x2d, gt.reshape(1, -1), w_out)


def _hy_short_body(e_ref, o_ref, sc_ref, u_ref, bias_ref, cf_ref, sf_ref, out_ref):
    bf = jnp.bfloat16
    cf, sf = cf_ref[...], sf_ref[...]
    dot = lambda a, b: jnp.dot(a, b.astype(bf), preferred_element_type=jnp.float32)
    z = u_ref[0]
    for order in range(HY_ORDER):
        cols = slice(order * HY_W, (order + 1) * HY_W)
        h_re = dot(cf, e_ref[:, cols]) * sc_ref[:, cols]
        h_im = dot(sf, o_ref[:, cols]) * sc_ref[:, cols]
        x_re, x_im = dot(cf, z), dot(sf, z)
        y_re = (x_re * h_re - x_im * h_im).astype(bf)
        y_im = (x_re * h_im + x_im * h_re).astype(bf)
        conv = _dot_tn(cf, y_re) + _dot_tn(sf, y_im)
        z = u_ref[1 + order] * (conv + z * bias_ref[order:order + 1, :])
    out_ref[...] = z


def hyena_short(pc, f_w1, f_b1, f_w2, f_b2, f_w3, f_b3, f_w4, f_freq, f_bias, conv_w, conv_b):
    import numpy as np
    rows = pc.shape[0]
    e, od, nrm = _hy_filter_time(rows, f_w1, f_b1, f_w2, f_b2, f_w3, f_b3, f_w4, f_freq, rows)
    u = _hy_conv(pc, conv_w, conv_b, rows)
    ang = 2 * np.pi * np.outer(np.arange(2 * rows), np.arange(rows)) / (2 * rows)
    cf, sf = jnp.asarray(np.cos(ang), jnp.bfloat16), jnp.asarray(-np.sin(ang), jnp.bfloat16)
    scale = 1.0 / (nrm * (2.0 * rows))
    return pl.pallas_call(
        _hy_short_body,
        out_shape=jax.ShapeDtypeStruct((rows, HY_W), jnp.float32),
        name="hy_short",
    )(e, od, scale, u, f_bias.astype(jnp.float32), cf, sf)


def even_layer(x, xc, g, mod, mod_c, w_in, conv_w, conv_b, f_w1, f_b1, f_w2, f_b2, f_w3, f_b3, f_w4, f_freq, f_bias,
               q_norm, k_norm, sink, gt, gt_c, w_out):
    f32 = jnp.float32
    length, n_ctx = x.shape[1], xc.shape[1]
    filt = (f_w1, f_b1, f_w2, f_b2, f_w3, f_b3, f_w4, f_freq, f_bias, conv_w, conv_b)
    o_q = 2 * ATT_KV_W
    o_hy = o_q + ATT_Q_W
    w_perm = jnp.concatenate([w_in[:, o_hy:], w_in[:, o_q:o_hy], w_in[:, :o_q]], axis=1).astype(jnp.bfloat16)
    twice = lambda v: jnp.stack([v.reshape(-1), v.reshape(-1)])
    p = norm_proj(x[0], g, twice(mod[0]), twice(mod[1]), w_perm)
    pc = norm_proj(xc[0], g, twice(mod_c[0]), twice(mod_c[1]), w_perm)
    rope_c, rope_s = _rope_tables(length)
    qn, kn = _qk_prep(p, rope_c, rope_s, q_norm, k_norm, EVEN_TM)
    qcn, kcn = _qk_prep(pc, jnp.ones((n_ctx, HEAD_DIM), f32), jnp.zeros((n_ctx, HEAD_DIM), f32), q_norm, k_norm,
                        n_ctx)
    att = _attention(qn, kn, p, kcn, pc, sink, True)
    att_c = _attention(qcn, None, None, kcn, pc, sink, False)
    hy = hyena_long(p, *filt)
    hy_c = hyena_short(pc, *filt)
    x_new = _even_merge(hy, att, x[0], gt, w_out, EVEN_TM)
    xc_new = _even_merge(hy_c, att_c, xc[0], gt_c, w_out, n_ctx)
    return x_new[None], xc_new[None]


SCAN_Q = 128
SCAN_LEVELS = 7
ODD_COLS = SSD_CONV_CH + 8 * 512
HALO = SUBLANE


def _scan_constants():
    import numpy as np
    q = SCAN_Q
    d_hg, d_ssd, pairs, laters = [], [], [], []
    for direction in (0, 1):
        pos = np.arange(q) if direction == 0 else q - 1 - np.arange(q)
        pj, pt = pos[None, :], pos[:, None]
        top = pj <= pt
        end = pj > pt
        ones = np.ones((SUBLANE, q), bool)
        lv, pr, lt = [], [], []
        for level in range(SCAN_LEVELS):
            b = 2 ** level
            start = (pos // (2 * b)) * (2 * b)
            mid = (start + b)[:, None]
            later = pos >= start + b
            lv.append(np.where(later[:, None], (pj >= mid) & (pj <= pt), (pj > pt) & (pj < mid)))
            pr.append((start[:, None] == start[None, :]) & later[:, None] & ~later[None, :])
            lt.append(np.broadcast_to(later[:, None], (q, LANE)))
        pr.append(np.eye(q, dtype=bool))
        pr.append(top)
        d_hg.append(np.concatenate([top, end] + lv + [ones], axis=0))
        d_ssd.append(np.concatenate([top, end, ones], axis=0))
        pairs.append(np.stack(pr))
        laters.append(np.stack(lt))
    f = np.float32
    twice = lambda m: np.concatenate([m, m], axis=-1)
    return (twice(np.stack(d_hg)).astype(f), twice(np.stack(d_ssd)).astype(f), np.stack(pairs).astype(f),
            np.stack(laters).astype(f))


def _split_dot(mm_bf16, v):
    hi = v.astype(jnp.bfloat16)
    lo = (v - hi.astype(jnp.float32)).astype(jnp.bfloat16)
    return jnp.dot(mm_bf16, jnp.concatenate([hi, lo], axis=0), preferred_element_type=jnp.float32)


def _dot_nt(a, b):
    return lax.dot_general(a, b, (((1,), (1,)), ((), ())), preferred_element_type=jnp.float32)


def _dot_tn(a, b):
    return lax.dot_general(a, b, (((0,), (0,)), ((), ())), preferred_element_type=jnp.float32)


def _softplus(x):
    return jnp.maximum(x, 0.0) + jnp.log1p(jnp.exp(-jnp.abs(x)))


def _scan_body(xbc_ref, prev_ref, next_ref, f_ref, iv_ref, q_ref, dt_ref,
               cw_ref, cb_ref, dtb_ref, a_ref, dsk_ref, lb_ref,
               dhg_ref, dssd_ref, pair_ref, later_ref, sel_ref,
               out_ref, s_ssd, s_hg, ydiag_ref, *, n_ctx_chunks, n_chunks):
    d = pl.program_id(0)
    j = pl.program_id(1)
    q_rows = SCAN_Q
    bf = jnp.bfloat16

    @pl.when(j == 0)
    def _():
        s_ssd[...] = jnp.zeros_like(s_ssd)
        s_hg[...] = jnp.zeros_like(s_hg)

    c = jnp.where(d == 0, j, jnp.where(j < n_ctx_chunks, n_ctx_chunks - 1 - j, n_chunks - 1 + n_ctx_chunks - j))
    first = (c == 0) | (c == n_ctx_chunks)
    last = (c == n_ctx_chunks - 1) | (c == n_chunks - 1)

    u = xbc_ref[...]
    row = lax.broadcasted_iota(jnp.int32, u.shape, 0)
    before = jnp.where(first, 0.0, prev_ref[HALO - 1:HALO, :])
    after = jnp.where(last, 0.0, next_ref[0:1, :])
    up = jnp.where(row == 0, before, pltpu.roll(u, 1, 0))
    un = jnp.where(row == q_rows - 1, after, pltpu.roll(u, q_rows - 1, 0))
    xbc = cw_ref[0:1, :] * up + cw_ref[1:2, :] * u + cw_ref[2:3, :] * un + cb_ref[...]
    xbc = xbc * jax.nn.sigmoid(xbc)
    x = xbc[:, :SSD_W]

    dt = _softplus(dt_ref[...] + dtb_ref[...])
    da = dt * a_ref[...]
    r = _split_dot(dssd_ref[...], da)
    cs, to_end, total = r[:q_rows], r[q_rows:2 * q_rows], r[2 * q_rows:2 * q_rows + 1]
    xdt = x * dt
    cs_hi = cs.astype(bf)
    cs_lo = (cs - cs_hi.astype(jnp.float32)).astype(bf)
    cs_rows = _dot_nt(sel_ref[...], cs_hi) + _dot_nt(sel_ref[...], cs_lo)
    l_mask = pair_ref[SCAN_LEVELS + 1]
    decay_in = jnp.exp2(cs)
    w_end = (jnp.exp2(to_end) * xdt).astype(bf)
    gn = SSD_GROUPS * SSD_STATE
    hpg = SSD_HEADS // SSD_GROUPS
    gw = hpg * SSD_HEAD_DIM
    for g in range(SSD_GROUPS):
        b_g = xbc[:, SSD_W + g * SSD_STATE:SSD_W + (g + 1) * SSD_STATE].astype(bf)
        c_g = xbc[:, SSD_W + gn + g * SSD_STATE:SSD_W + gn + (g + 1) * SSD_STATE].astype(bf)
        scores = _dot_nt(c_g, b_g)
        y_off = jnp.dot(c_g, s_ssd[g].astype(bf), preferred_element_type=jnp.float32)
        for hh in range(hpg):
            h = g * hpg + hh
            lo = h * SSD_HEAD_DIM
            diff = cs[:, lo:lo + 1] - cs_rows[h:h + 1, :]
            decay = jnp.exp2(jnp.minimum(diff, 0.0)) * l_mask
            ydiag_ref[:, lo:lo + SSD_HEAD_DIM] = jnp.dot((scores * decay).astype(bf),
                                                         xdt[:, lo:lo + SSD_HEAD_DIM].astype(bf),
                                                         preferred_element_type=jnp.float32)
        cols = slice(g * gw, (g + 1) * gw)
        out_ref[:, cols] = (ydiag_ref[:, cols] + decay_in[:, cols] * y_off + dsk_ref[:, cols] * x[:, cols])
        s_ssd[g] = jnp.exp2(total[:, cols]) * s_ssd[g] + _dot_tn(b_g, w_end[:, cols])

    lb = lb_ref[...]
    f = lb + (1.0 - lb) * jax.nn.sigmoid(f_ref[...])
    k_in = 1.0 - f
    qv = q_ref[...]
    qv = qv * jax.nn.sigmoid(qv)
    v_bf = iv_ref[...].astype(bf)
    e = jnp.exp2(_split_dot(dhg_ref[...], jnp.log2(f)))
    e_top, e_end = e[:q_rows], e[q_rows:2 * q_rows]
    e_tot = e[(2 + SCAN_LEVELS) * q_rows:(2 + SCAN_LEVELS) * q_rows + 1]
    for h in range(HG_HEADS):
        cols = slice(h * HG_EXPAND, (h + 1) * HG_EXPAND)
        q_h, k_h = qv[:, cols], k_in[:, cols]
        att = pair_ref[SCAN_LEVELS] * _dot_nt(q_h.astype(bf), k_h.astype(bf))
        for level in range(SCAN_LEVELS):
            e_l = e[(2 + level) * q_rows:(3 + level) * q_rows, cols]
            w_l = (jnp.where(later_ref[level] > 0.0, q_h, k_h) * e_l).astype(bf)
            att = att + pair_ref[level] * _dot_nt(w_l, w_l)
        o = jnp.dot(att.astype(bf), v_bf[:, cols], preferred_element_type=jnp.float32)
        o = o + _dot_nt((q_h * e_top[:, cols]).astype(bf), s_hg[h].astype(bf))
        out_ref[:, SSD_W + h * HG_VDIM:SSD_W + (h + 1) * HG_VDIM] = o
        s_hg[h] = e_tot[:, cols] * s_hg[h] + _dot_tn(v_bf[:, cols], (k_h * e_end[:, cols]).astype(bf))


def _odd_scan(p_all, conv_w, conv_b, dtb, a_cols, dsk, lb, n_ctx):
    n_rows = p_all.shape[0]
    n_chunks = n_rows // SCAN_Q
    ncc = n_ctx // SCAN_Q
    d_hg, d_ssd, pairs, laters = _scan_constants()
    bf = jnp.bfloat16
    import numpy as np
    sel = np.zeros((LANE, SSD_W), np.float32)
    sel[np.arange(SSD_HEADS), np.arange(SSD_HEADS) * SSD_HEAD_DIM] = 1.0

    def chunk(d, j):
        return jnp.where(d == 0, j, jnp.where(j < ncc, ncc - 1 - j, n_chunks - 1 + ncc - j))

    per = SCAN_Q // HALO
    last_halo = n_rows // HALO - 1
    col512 = lambda blk: (lambda d, j: (chunk(d, j), blk))
    const2 = lambda shape: pl.BlockSpec(shape, lambda d, j: (0,) * len(shape))
    dirc = lambda shape: pl.BlockSpec((None,) + shape, lambda d, j: (d,) + (0,) * len(shape))
    body = functools.partial(_scan_body, n_ctx_chunks=ncc, n_chunks=n_chunks)
    return pl.pallas_call(
        body,
        grid=(2, n_chunks),
        in_specs=[
            pl.BlockSpec((SCAN_Q, SSD_CONV_CH), lambda d, j: (chunk(d, j), 0)),
            pl.BlockSpec((HALO, SSD_CONV_CH), lambda d, j: (jnp.maximum(chunk(d, j) * per - 1, 0), 0)),
            pl.BlockSpec((HALO, SSD_CONV_CH), lambda d, j: (jnp.minimum((chunk(d, j) + 1) * per, last_halo), 0)),
            pl.BlockSpec((SCAN_Q, 512), lambda d, j: (chunk(d, j), 2 + d)),
            pl.BlockSpec((SCAN_Q, 512), col512(4)),
            pl.BlockSpec((SCAN_Q, 512), col512(6)),
            pl.BlockSpec((SCAN_Q, 512), lambda d, j: (chunk(d, j), 8 + d)),
            const2((CONV_W, SSD_CONV_CH)), const2((1, SSD_CONV_CH)),
            dirc((1, SSD_W)), dirc((1, SSD_W)), dirc((1, SSD_W)), const2((1, HG_W)),
            dirc(d_hg.shape[1:]), dirc(d_ssd.shape[1:]), dirc(pairs.shape[1:]), dirc(laters.shape[1:]),
            const2((LANE, SSD_W)),
        ],
        out_specs=pl.BlockSpec((None, SCAN_Q, MIX_W), lambda d, j: (d, chunk(d, j), 0)),
        out_shape=jax.ShapeDtypeStruct((2, n_rows, MIX_W), jnp.float32),
        scratch_shapes=[pltpu.VMEM((SSD_GROUPS, SSD_STATE, SSD_W // SSD_GROUPS), jnp.float32),
                        pltpu.VMEM((HG_HEADS, HG_VDIM, HG_EXPAND), jnp.float32),
                        pltpu.VMEM((SCAN_Q, SSD_W), jnp.float32)],
        compiler_params=pltpu.CompilerParams(dimension_semantics=("arbitrary", "arbitrary"),
                                             vmem_limit_bytes=MOE_VMEM_LIMIT),
        name="odd_scan",
    )(p_all, p_all, p_all, p_all, p_all, p_all, p_all,
      conv_w, conv_b.reshape(1, -1), dtb, a_cols, dsk, lb.reshape(1, -1),
      jnp.asarray(d_hg, bf), jnp.asarray(d_ssd, bf), jnp.asarray(pairs), jnp.asarray(laters),
      jnp.asarray(sel, bf))


def _group_rms(v, width):
    parts = []
    for lo in range(0, v.shape[1], width):
        seg = v[:, lo:lo + width]
        parts.append(seg * lax.rsqrt(jnp.mean(seg * seg, axis=-1, keepdims=True) + EPS))
    return jnp.concatenate(parts, axis=1)


def _odd_merge_body(yo_ref, z_ref, g_ref, x_ref, sn_ref, hn_ref, gt_ref, w_ref, o_ref):
    yo = yo_ref[0] + yo_ref[1]
    z = z_ref[...]
    g = g_ref[...]
    ys = _group_rms(yo[:, :SSD_W] * (z * jax.nn.sigmoid(z)), SSD_W // SSD_GROUPS) * sn_ref[...]
    hs = _group_rms(yo[:, SSD_W:], HG_VDIM) * hn_ref[...] * (g * jax.nn.sigmoid(g))
    m = jnp.concatenate([ys, hs], axis=1).astype(jnp.bfloat16)
    o_ref[...] = x_ref[...] + gt_ref[...] * jnp.dot(m, w_ref[...].astype(jnp.bfloat16),
                                                    preferred_element_type=jnp.float32)


ODD_TM = 256


def _odd_merge(yo, p_all, x_all, ssd_norm, hg_norm, gt, w_out, n_ctx):
    n_all, d = x_all.shape
    n_lat = n_all - n_ctx
    skip = n_ctx // ODD_TM
    return pl.pallas_call(
        _odd_merge_body,
        grid=(n_lat // ODD_TM,),
        in_specs=[pl.BlockSpec((2, ODD_TM, MIX_W), lambda i: (0, i + skip, 0)),
                  pl.BlockSpec((ODD_TM, 512), lambda i: (i + skip, 5)),
                  pl.BlockSpec((ODD_TM, 512), lambda i: (i + skip, 7)),
                  pl.BlockSpec((ODD_TM, d), lambda i: (i + skip, 0)),
                  pl.BlockSpec((1, SSD_W), lambda i: (0, 0)),
                  pl.BlockSpec((1, HG_W), lambda i: (0, 0)),
                  pl.BlockSpec((1, d), lambda i: (0, 0)),
                  pl.BlockSpec((MIX_W, d), lambda i: (0, 0))],
        out_specs=pl.BlockSpec((ODD_TM, d), lambda i: (i, 0)),
        out_shape=jax.ShapeDtypeStruct((n_lat, d), jnp.float32),
        compiler_params=pltpu.CompilerParams(dimension_semantics=("arbitrary",)),
        name="odd_merge",
    )(yo, p_all, p_all, x_all, ssd_norm.reshape(1, -1), hg_norm.reshape(1, -1), gt.reshape(1, -1), w_out)


def odd_layer(x_all, n_ctx, g, mod, mod_c, lb, w_in, conv_w, conv_b, dt_bias, a_log, d_skip, ssd_norm, hg_norm, gt,
              w_out):
    f32 = jnp.float32
    o_dt = SSD_CONV_CH
    o_f = o_dt + 2 * SSD_HEADS
    rep = lambda v: jnp.repeat(v, SSD_HEAD_DIM, axis=-1)
    w_perm = jnp.concatenate([w_in[:, :o_dt], w_in[:, o_f:], rep(w_in[:, o_dt:o_dt + SSD_HEADS]),
                              rep(w_in[:, o_dt + SSD_HEADS:o_f])], axis=1).astype(jnp.bfloat16)
    both = lambda a, b: jnp.stack([a.reshape(-1), b.reshape(-1)])
    p_all = norm_proj(x_all, g, both(mod_c[0], mod[0]), both(mod_c[1], mod[1]), w_perm, n_first=n_ctx)
    dtb = rep(dt_bias.astype(f32)).reshape(2, 1, SSD_W)
    a_cols = rep(-jnp.exp(a_log.astype(f32)) * math.log2(math.e)).reshape(2, 1, SSD_W)
    dsk = rep(d_skip.astype(f32)).reshape(2, 1, SSD_W)
    yo = _odd_scan(p_all, conv_w, conv_b, dtb, a_cols, dsk, lb.astype(f32), n_ctx)
    return _odd_merge(yo, p_all, x_all, ssd_norm, hg_norm, gt, w_out, n_ctx)


MOE_TM = 256
MOE_BM = 256
NEG_BIG = -1e30
MOE_ISSUE_UNROLL = 8
MOE_TILE_ROWS = 64
MOE_VMEM_LIMIT = 52 * 1024 * 1024


def _route_body(x_ref, g_ref, sh_ref, sc_ref, rw_ref, rb_ref, cnt0_ref,
                h_ref, idx_ref, gate_ref, rank_ref, cnt_ref, run_ref):
    i = pl.program_id(0)

    @pl.when(i == 0)
    def _():
        run_ref[...] = cnt0_ref[...]

    tm = x_ref.shape[0]
    x = x_ref[...]
    t = (x * lax.rsqrt(jnp.mean(x * x, axis=-1, keepdims=True) + EPS) * g_ref[...]) * (1.0 + sc_ref[...]) + sh_ref[...]
    h_ref[...] = t
    logits = jnp.dot(t.astype(jnp.bfloat16), rw_ref[...].astype(jnp.bfloat16),
                     preferred_element_type=jnp.float32) + rb_ref[...]
    lane = lax.broadcasted_iota(jnp.int32, (tm, LANE), 1)
    lane_f = lane.astype(jnp.float32)
    work = logits
    vals, sels, hots = [], [], []
    for _ in range(TOP_K):
        m = jnp.max(work, axis=-1, keepdims=True)
        sel = jnp.min(jnp.where(work == m, lane_f, float(LANE)), axis=-1, keepdims=True)
        hot = lane_f == sel
        vals.append(m)
        sels.append(sel.astype(jnp.int32))
        hots.append(hot)
        work = jnp.where(hot, -jnp.inf, work)
    exps = [jnp.exp(v - vals[0]) for v in vals]
    denom = exps[0] + exps[1] + exps[2] + exps[3]
    chosen = jnp.zeros((tm, LANE), jnp.float32)
    for hot in hots:
        chosen = chosen + hot.astype(jnp.float32)
    row = lax.broadcasted_iota(jnp.int32, (tm, tm), 0)
    col = lax.broadcasted_iota(jnp.int32, (tm, tm), 1)
    tri = (row > col).astype(jnp.bfloat16)
    before = jnp.dot(tri, chosen.astype(jnp.bfloat16), preferred_element_type=jnp.float32) + run_ref[0:1, :]
    idx_out = jnp.zeros((tm, LANE), jnp.int32)
    gate_out = jnp.zeros((tm, LANE), jnp.float32)
    rank_out = jnp.zeros((tm, LANE), jnp.int32)
    for k in range(TOP_K):
        rank_k = jnp.sum(jnp.where(hots[k], before, 0.0), axis=-1, keepdims=True).astype(jnp.int32)
        idx_out = jnp.where(lane == k, sels[k], idx_out)
        gate_out = jnp.where(lane == k, exps[k] / denom, gate_out)
        rank_out = jnp.where(lane == k, rank_k, rank_out)
    idx_ref[...] = idx_out
    gate_ref[...] = gate_out
    rank_ref[...] = rank_out
    run_new = run_ref[0:1, :] + jnp.sum(chosen, axis=0, keepdims=True)
    run_ref[...] = jnp.broadcast_to(run_new, run_ref.shape)
    cnt_ref[...] = jnp.broadcast_to(run_new, cnt_ref.shape)


def _moe_route(x2d, g, shift, scale, router_w, router_b, cnt0):
    n_tok, d = x2d.shape
    f32 = jnp.float32
    rw = jnp.pad(router_w, ((0, 0), (0, LANE - N_EXPERTS)))
    rb = jnp.pad(router_b.astype(f32), (0, LANE - N_EXPERTS), constant_values=NEG_BIG).reshape(1, LANE)
    tile = pl.BlockSpec((MOE_TM, LANE), lambda i: (i, 0))
    wide = pl.BlockSpec((MOE_TM, d), lambda i: (i, 0))
    vec = pl.BlockSpec((1, d), lambda i: (0, 0))
    small = pl.BlockSpec((SUBLANE, LANE), lambda i: (0, 0))
    row = lambda v: v.astype(f32).reshape(1, d)
    h, idx, gate, rank, cnt = pl.pallas_call(
        _route_body,
        grid=(n_tok // MOE_TM,),
        in_specs=[wide, vec, vec, vec, pl.BlockSpec((d, LANE), lambda i: (0, 0)),
                  pl.BlockSpec((1, LANE), lambda i: (0, 0)), small],
        out_specs=[wide, tile, tile, tile, small],
        out_shape=[jax.ShapeDtypeStruct((n_tok, d), f32),
                   jax.ShapeDtypeStruct((n_tok, LANE), jnp.int32),
                   jax.ShapeDtypeStruct((n_tok, LANE), f32),
                   jax.ShapeDtypeStruct((n_tok, LANE), jnp.int32),
                   jax.ShapeDtypeStruct((SUBLANE, LANE), f32)],
        scratch_shapes=[pltpu.VMEM((SUBLANE, LANE), f32)],
        compiler_params=pltpu.CompilerParams(dimension_semantics=("arbitrary",)),
        name="moe_route",
    )(x2d, row(g), row(shift), row(scale), rw, rb, cnt0)
    return h, idx[:, :TOP_K], gate, rank[:, :TOP_K], cnt


def _rows_to_tiles(x):
    return _transpose8([x[:, LANE * s:LANE * (s + 1)] for s in range(SUBLANE)])


def _store_tiles(ref, tiles):
    groups = ref.shape[0] // SUBLANE
    for j in range(SUBLANE):
        ref[pl.ds(j, groups, stride=SUBLANE)] = tiles[j].reshape(groups, SUBLANE, LANE)


def _load_rows(ref):
    groups = ref.shape[0] // SUBLANE
    tiles = [ref[pl.ds(j, groups, stride=SUBLANE)].reshape(groups * SUBLANE, LANE) for j in range(SUBLANE)]
    return jnp.concatenate(_transpose8(tiles), axis=1)


def _scatter_body(off_ref, pad_ref, dest_ref, *rest, tile_starts):
    t_refs = rest[:-4]
    xs_ref, rows_ref, zero_ref, sem = rest[-4:]
    i = pl.program_id(0)
    tm = t_refs[0].shape[0]

    @pl.when(i == 0)
    def _():
        zero_ref[...] = jnp.zeros_like(zero_ref)
        used = off_ref[N_EXPERTS - 1] + pad_ref[N_EXPERTS - 1]
        n_rows = xs_ref.shape[0]

        def zero_block(start):
            return pltpu.make_async_copy(zero_ref, xs_ref.at[pl.ds(start, MOE_BM)], sem)

        for e in range(N_EXPERTS):
            tail = n_rows - (e + 1) * MOE_BM

            @pl.when(pad_ref[e] > 0)
            def _():
                zero_block(off_ref[e] + pad_ref[e] - MOE_BM).start()

            @pl.when(tail >= used)
            def _():
                zero_block(tail).start()
        for e in range(N_EXPERTS):
            tail = n_rows - (e + 1) * MOE_BM

            @pl.when(pad_ref[e] > 0)
            def _():
                zero_block(0).wait()

            @pl.when(tail >= used)
            def _():
                zero_block(0).wait()

    for s, t_ref in enumerate(t_refs):
        @pl.when((i >= tile_starts[s]) & (i < tile_starts[s + 1]))
        def _():
            for r0 in range(0, tm, MOE_TILE_ROWS):
                _store_tiles(rows_ref.at[pl.ds(r0, MOE_TILE_ROWS)],
                             _rows_to_tiles(t_ref[pl.ds(r0, MOE_TILE_ROWS), :]))

    def issue(t, carry):
        for k in range(TOP_K):
            pltpu.make_async_copy(rows_ref.at[t], xs_ref.at[dest_ref[TOP_K * t + k]], sem).start(priority=k % 2)
        return carry

    lax.fori_loop(0, tm, issue, 0, unroll=MOE_ISSUE_UNROLL)
    for _ in range(TOP_K):
        pltpu.make_async_copy(rows_ref, xs_ref.at[pl.ds(0, tm)], sem).wait()


def _stream_tiles(streams):
    starts = [0]
    for t in streams:
        starts.append(starts[-1] + t.shape[0] // MOE_TM)
    return tuple(starts)


def _stream_spec(width, starts, s, extra=0):
    lo, n = starts[s], starts[s + 1] - starts[s]
    return pl.BlockSpec((MOE_TM, width), lambda i, *_: (jnp.clip(i + extra - lo, 0, n - 1), 0))


def _moe_scatter(hs, dest_flat, off, padded, n_rows):
    d = hs[0].shape[1]
    assert d == SUBLANE * LANE
    starts = _stream_tiles(hs)
    in_specs = [pl.BlockSpec((TOP_K * MOE_TM,), lambda i, off, pad: (i,), memory_space=pltpu.SMEM)]
    in_specs += [_stream_spec(d, starts, s) for s in range(len(hs))]
    return pl.pallas_call(
        functools.partial(_scatter_body, tile_starts=starts),
        grid_spec=pltpu.PrefetchScalarGridSpec(
            num_scalar_prefetch=2,
            grid=(starts[-1],),
            in_specs=in_specs,
            out_specs=pl.BlockSpec(memory_space=pl.ANY),
            scratch_shapes=[pltpu.VMEM((MOE_TM, SUBLANE, LANE), jnp.float32),
                            pltpu.VMEM((MOE_BM, SUBLANE, LANE), jnp.float32), pltpu.SemaphoreType.DMA],
        ),
        out_shape=jax.ShapeDtypeStruct((n_rows, SUBLANE, LANE), jnp.float32),
        compiler_params=pltpu.CompilerParams(dimension_semantics=("arbitrary",)),
        name="moe_scatter",
    )(off, padded, dest_flat, *hs)


def _expert_body(blk_e_ref, n_act_ref, nxt_e_ref, x_ref, wgu_hbm, bgu_ref, wdn_hbm, bdn_ref, y_ref,
                 wgu_f32, wdn_f32, wgu_bf, wdn_bf, sems, slot_ref, *, layer):
    i = pl.program_id(0)

    def weight_copies(e, slot):
        return (pltpu.make_async_copy(wgu_hbm.at[layer, e], wgu_f32.at[slot], sems.at[slot, 0]),
                pltpu.make_async_copy(wdn_hbm.at[layer, e], wdn_f32.at[slot], sems.at[slot, 1]))

    @pl.when(i < n_act_ref[0])
    def _():
        e = blk_e_ref[i]
        prev = blk_e_ref[jnp.maximum(i - 1, 0)]

        @pl.when((i == 0) | (e != prev))
        def _():
            @pl.when(i == 0)
            def _():
                slot_ref[0] = 0
                for c in weight_copies(e, 0):
                    c.start()

            slot = slot_ref[0]
            for c in weight_copies(e, slot):
                c.wait()
            wgu_bf[...] = wgu_f32[slot].astype(jnp.bfloat16)
            wdn_bf[...] = wdn_f32[slot].astype(jnp.bfloat16)
            nxt = nxt_e_ref[e]

            @pl.when(nxt >= 0)
            def _():
                for c in weight_copies(nxt, 1 - slot):
                    c.start()

            slot_ref[0] = 1 - slot

        x = _load_rows(x_ref).astype(jnp.bfloat16)
        gu = jnp.dot(x, wgu_bf[...], preferred_element_type=jnp.float32) + bgu_ref[...]
        gate = jnp.minimum(gu[:, :D_EXPERT], SWIGLU_LIMIT)
        up = jnp.clip(gu[:, D_EXPERT:], -SWIGLU_LIMIT, SWIGLU_LIMIT)
        act = (up + 1.0) * gate * jax.nn.sigmoid(SWIGLU_ALPHA * gate)
        y = jnp.dot(act.astype(jnp.bfloat16), wdn_bf[...], preferred_element_type=jnp.float32) + bdn_ref[...]
        _store_tiles(y_ref, _rows_to_tiles(y))

    @pl.when(i >= n_act_ref[0])
    def _():
        y_ref[...] = jnp.zeros_like(y_ref)


def _moe_experts(xs, blk_e, n_act, nxt_e, w_gu, b_gu, w_dn, b_dn, layer):
    n_rows = xs.shape[0]
    d = SUBLANE * LANE
    n_blk = n_rows // MOE_BM

    def blk(i, be, na, nx):
        return jnp.minimum(i, na[0] - 1)

    return pl.pallas_call(
        functools.partial(_expert_body, layer=layer),
        grid_spec=pltpu.PrefetchScalarGridSpec(
            num_scalar_prefetch=3,
            grid=(n_blk,),
            in_specs=[pl.BlockSpec((MOE_BM, SUBLANE, LANE), lambda i, be, na, nx: (blk(i, be, na, nx), 0, 0)),
                      pl.BlockSpec(memory_space=pl.ANY),
                      pl.BlockSpec((None, 1, 2 * D_EXPERT), lambda i, be, na, nx: (be[blk(i, be, na, nx)], 0, 0)),
                      pl.BlockSpec(memory_space=pl.ANY),
                      pl.BlockSpec((None, 1, d), lambda i, be, na, nx: (be[blk(i, be, na, nx)], 0, 0))],
            out_specs=pl.BlockSpec((MOE_BM, SUBLANE, LANE), lambda i, be, na, nx: (i, 0, 0)),
            scratch_shapes=[pltpu.VMEM((2, d, 2 * D_EXPERT), jnp.float32),
                            pltpu.VMEM((2, D_EXPERT, d), jnp.float32),
                            pltpu.VMEM((d, 2 * D_EXPERT), jnp.bfloat16),
                            pltpu.VMEM((D_EXPERT, d), jnp.bfloat16),
                            pltpu.SemaphoreType.DMA((2, 2)),
                            pltpu.SMEM((1,), jnp.int32)],
        ),
        out_shape=jax.ShapeDtypeStruct((n_rows, SUBLANE, LANE), jnp.float32),
        compiler_params=pltpu.CompilerParams(dimension_semantics=("arbitrary",),
                                             vmem_limit_bytes=MOE_VMEM_LIMIT),
        name="moe_experts",
    )(blk_e, n_act, nxt_e, xs, w_gu, b_gu.reshape(N_EXPERTS, 1, -1), w_dn, b_dn.reshape(N_EXPERTS, 1, -1))


def _combine_body(dest_ref, dest_nxt_ref, *rest, tile_starts):
    n_streams = len(tile_starts) - 1
    ins, ys_ref = rest[:3 * n_streams], rest[3 * n_streams]
    y_refs = rest[3 * n_streams + 1:-2]
    if len(y_refs) == 1:
        y_refs = y_refs * n_streams
    buf_ref, sems = rest[-2:]
    i = pl.program_id(0)
    n = pl.num_programs(0)
    tm = y_refs[0].shape[0]

    def fetch(d_ref, slot):
        def issue(t, carry):
            for k in range(TOP_K):
                pltpu.make_async_copy(ys_ref.at[d_ref[TOP_K * t + k]], buf_ref.at[slot, k, t],
                                      sems.at[slot]).start(priority=k % 2)
            return carry
        lax.fori_loop(0, tm, issue, 0, unroll=MOE_ISSUE_UNROLL)

    @pl.when(i == 0)
    def _():
        fetch(dest_ref, 0)

    @pl.when(i + 1 < n)
    def _():
        fetch(dest_nxt_ref, (i + 1) % 2)

    slot = i % 2
    for k in range(TOP_K):
        pltpu.make_async_copy(ys_ref.at[pl.ds(0, tm)], buf_ref.at[slot, k], sems.at[slot]).wait()
    for s in range(n_streams):
        gate_ref, x_ref, gt_ref = ins[3 * s:3 * s + 3]

        @pl.when((i >= tile_starts[s]) & (i < tile_starts[s + 1]))
        def _():
            g = gate_ref[...]
            groups = tm // SUBLANE
            gk = [jnp.broadcast_to(g[:, k:k + 1], (tm, LANE)).reshape(groups, SUBLANE, LANE) for k in range(TOP_K)]
            accs = []
            for j in range(SUBLANE):
                acc = None
                for k in range(TOP_K):
                    splat = jnp.broadcast_to(gk[k][:, j:j + 1, :], (groups, SUBLANE, LANE))
                    term = splat * buf_ref.at[slot, k][pl.ds(j, groups, stride=SUBLANE)]
                    acc = term if acc is None else acc + term
                accs.append(acc.reshape(tm, LANE))
            y_refs[s][...] = x_ref[...] + gt_ref[...] * jnp.concatenate(_transpose8(accs), axis=1)


def _moe_combine(ys, dest_flat, gates, xs2d, gts, joint):
    d = xs2d[0].shape[1]
    starts = _stream_tiles(xs2d)
    n_tiles = starts[-1]
    in_specs = [pl.BlockSpec((TOP_K * MOE_TM,), lambda i: (i,), memory_space=pltpu.SMEM),
                pl.BlockSpec((TOP_K * MOE_TM,), lambda i: (jnp.minimum(i + 1, n_tiles - 1),),
                             memory_space=pltpu.SMEM)]
    args = [dest_flat, dest_flat]
    for s, (gate, x2d, gt) in enumerate(zip(gates, xs2d, gts)):
        in_specs += [_stream_spec(LANE, starts, s), _stream_spec(d, starts, s), pl.BlockSpec((1, d), lambda i: (0, 0))]
        args += [gate, x2d, gt.astype(jnp.float32).reshape(1, d)]
    in_specs.append(pl.BlockSpec(memory_space=pl.ANY))
    args.append(ys)
    if joint:
        out_specs = [pl.BlockSpec((MOE_TM, d), lambda i: (i, 0))]
        out_shape = [jax.ShapeDtypeStruct((n_tiles * MOE_TM, d), jnp.float32)]
    else:
        out_specs = [_stream_spec(d, starts, s) for s in range(len(xs2d))]
        out_shape = [jax.ShapeDtypeStruct(x2d.shape, jnp.float32) for x2d in xs2d]
    return pl.pallas_call(
        functools.partial(_combine_body, tile_starts=starts),
        grid=(n_tiles,),
        in_specs=in_specs,
        out_specs=out_specs,
        out_shape=out_shape,
        scratch_shapes=[pltpu.VMEM((2, TOP_K, MOE_TM, SUBLANE, LANE), jnp.float32),
                        pltpu.SemaphoreType.DMA((2,))],
        compiler_params=pltpu.CompilerParams(dimension_semantics=("arbitrary",)),
        name="moe_combine",
    )(*args)


def moe_layer(streams, g, router_w, router_b, w_gu, b_gu, w_dn, b_dn, layer, joint=False):
    i32 = jnp.int32
    routed = []
    cnt = jnp.zeros((SUBLANE, LANE), jnp.float32)
    for x2d, shift, scale, _ in streams:
        assert x2d.shape[0] % MOE_TM == 0
        h, idx, gate, rank, cnt = _moe_route(x2d, g, shift, scale, router_w, router_b, cnt)
        routed.append((h, idx, gate, rank))
    counts = cnt[0, :N_EXPERTS].astype(i32)
    padded = (counts + MOE_BM - 1) // MOE_BM * MOE_BM
    pad_end = jnp.cumsum(padded)
    off = (pad_end - padded).astype(i32)
    padded = padded.astype(i32)
    n_tok = sum(s[0].shape[0] for s in streams)
    n_blk = -(-(n_tok * TOP_K) // MOE_BM) + N_EXPERTS
    blk_e = jnp.minimum(jnp.sum(jnp.arange(n_blk)[:, None] * MOE_BM >= pad_end[None, :], axis=1),
                        N_EXPERTS - 1).astype(i32)
    n_act = (pad_end[-1:] // MOE_BM).astype(i32)
    experts = jnp.arange(N_EXPERTS, dtype=i32)
    later_with_rows = (padded > 0)[None, :] & (experts[None, :] > experts[:, None])
    nxt_e = jnp.min(jnp.where(later_with_rows, experts[None, :], N_EXPERTS), axis=1)
    nxt_e = jnp.where(nxt_e == N_EXPERTS, -1, nxt_e).astype(i32)
    dests = []
    for _, idx, _, rank in routed:
        dest = rank + jnp.sum(jnp.where(idx[..., None] == experts, off, 0), axis=-1)
        dests.append(dest.reshape(-1).astype(i32))
    dest_flat = jnp.concatenate(dests)
    xs = _moe_scatter([r[0] for r in routed], dest_flat, off, padded, n_blk * MOE_BM)
    ys = _moe_experts(xs, blk_e, n_act, nxt_e, w_gu, b_gu, w_dn, b_dn, layer)
    return _moe_combine(ys, dest_flat, [r[2] for r in routed], [s[0] for s in streams], [s[3] for s in streams],
                        joint)


def kernel(x, c, ctx, c_ctx, norm_g, ada_w, ada_b, w_out, w_in_even, hy_conv_w, hy_conv_b,
           hy_w1, hy_b1, hy_w2, hy_b2, hy_w3, hy_b3, hy_w4, hy_freq, hy_filter_bias,
           att_q_norm, att_k_norm, att_sink, w_in_odd, ssd_conv_w, ssd_conv_b, ssd_dt_bias,
           ssd_A_log, ssd_D, ssd_norm, hg_lower_bounds, hg_norm, router_w, router_b,
           moe_w_gu, moe_b_gu, moe_w_dn, moe_b_dn):
    lbs = jax.nn.softmax(hg_lower_bounds.astype(jnp.float32), axis=0)
    lbs = jnp.cumsum(lbs, axis=0) - lbs[0]
    xc = ctx
    n_ctx = ctx.shape[1]
    for layer in range(DEPTH):
        ctx_out = layer < DEPTH - 1
        i = layer // 2
        sh, sc, gt = adaln(c, ada_w[layer], ada_b[layer], 0)
        sh_c, sc_c, gt_c = adaln(c_ctx, ada_w[layer], ada_b[layer], 0)
        assert (layer % 2 == 0) == ctx_out
        moe = functools.partial(moe_layer, g=norm_g[layer, 1], router_w=router_w[layer], router_b=router_b[layer],
                                w_gu=moe_w_gu, b_gu=moe_b_gu[layer], w_dn=moe_w_dn, b_dn=moe_b_dn[layer], layer=layer)
        sh2, sc2, gt2 = adaln(c, ada_w[layer], ada_b[layer], 1)
        if layer % 2 == 0:
            x, xc = even_layer(x, xc, norm_g[layer, 0], (sh, sc), (sh_c, sc_c), w_in_even[i], hy_conv_w[i],
                               hy_conv_b[i], hy_w1[i], hy_b1[i], hy_w2[i], hy_b2[i], hy_w3[i], hy_b3[i], hy_w4[i],
                               hy_freq[i], hy_filter_bias[i], att_q_norm[i], att_k_norm[i], att_sink[i], gt, gt_c,
                               w_out[layer])
            sh2_c, sc2_c, gt2_c = adaln(c_ctx, ada_w[layer], ada_b[layer], 1)
            x_all = moe([(xc[0], sh2_c, sc2_c, gt2_c), (x[0], sh2, sc2, gt2)], joint=True)[0]
        else:
            x = odd_layer(x_all, n_ctx, norm_g[layer, 0], (sh, sc), (sh_c, sc_c), lbs[layer], w_in_odd[i],
                          ssd_conv_w[i], ssd_conv_b[i], ssd_dt_bias[i], ssd_A_log[i], ssd_D[i], ssd_norm[i],
                          hg_norm[i], gt, w_out[layer])
            x = moe([(x, sh2, sc2, gt2)])[0][None]
    return x
```

```python
import functools
import math

import jax
import jax.numpy as jnp
from jax import lax
from jax.experimental import pallas as pl
from jax.experimental.pallas import tpu as pltpu

D_MODEL = 1024
DEPTH = 2
GRID_W = 64
MIX_W = D_MODEL
EPS = 1e-6
CONV_W = 3

HY_W = MIX_W // 2
HY_ORDER = 2
HY_EMB = 33
HY_FFN = 64
HY_TARGET = 1e-2
HY_SHORT_PCT = 0.3
HY_LONG_PCT = 1.5

HEAD_DIM = 64
ATT_HEADS = (MIX_W // 2) // HEAD_DIM
ATT_KV_HEADS = 2
ATT_GROUP = ATT_HEADS // ATT_KV_HEADS
ATT_WINDOW = 128
ATT_BLOCK = 128
ROPE_BASE = 10000.0
ATT_Q_W = ATT_HEADS * HEAD_DIM
ATT_KV_W = ATT_KV_HEADS * HEAD_DIM

SSD_W = MIX_W // 2
SSD_HEAD_DIM = 64
SSD_HEADS = SSD_W // SSD_HEAD_DIM
SSD_GROUPS = 2
SSD_STATE = 128
SSD_CHUNK = 128
SSD_CONV_CH = SSD_W + 2 * SSD_GROUPS * SSD_STATE

HG_W = MIX_W // 2
HG_EXPAND = 128
HG_HEADS = HG_W // HG_EXPAND
HG_VDIM = HG_W // HG_HEADS
HG_CHUNK = 64

N_EXPERTS = 32
TOP_K = 4
D_EXPERT = D_MODEL
SWIGLU_ALPHA = 1.702
SWIGLU_LIMIT = 7.0
MOE_BLOCK = 128

EVEN_IN = 2 * ATT_KV_W + ATT_Q_W + 3 * HY_W
ODD_STATE_COLS = SSD_CONV_CH + 2 * SSD_HEADS + 3 * HG_W
ODD_IN = ODD_STATE_COLS + SSD_W + 2 * HG_W

LANE = 128
SUBLANE = 8


PROJ_TM = 256
PROJ_VMEM_LIMIT = 56 * 1024 * 1024


def _norm_proj_body(x_ref, g_ref, sh_ref, sc_ref, w_ref, o_ref):
    x = x_ref[...]
    y = x * lax.rsqrt(jnp.mean(x * x, axis=-1, keepdims=True) + EPS) * g_ref[...]
    h = y * (1.0 + sc_ref[...]) + sh_ref[...]
    o_ref[...] = jnp.dot(h.astype(jnp.bfloat16), w_ref[...], preferred_element_type=jnp.float32)


def norm_proj(x2d, g, shift, scale, w_bf16, n_first=0):
    rows, d = x2d.shape
    n = w_bf16.shape[1]
    assert rows % PROJ_TM == 0 and n_first % PROJ_TM == 0 and n % LANE == 0
    first_tiles = n_first // PROJ_TM
    mod = pl.BlockSpec((None, 1, d), lambda i: (jnp.where(i < first_tiles, 0, 1), 0, 0))
    return pl.pallas_call(
        _norm_proj_body,
        grid=(rows // PROJ_TM,),
        in_specs=[pl.BlockSpec((PROJ_TM, d), lambda i: (i, 0)),
                  pl.BlockSpec((1, d), lambda i: (0, 0)), mod, mod,
                  pl.BlockSpec((d, n), lambda i: (0, 0))],
        out_specs=pl.BlockSpec((PROJ_TM, n), lambda i: (i, 0)),
        out_shape=jax.ShapeDtypeStruct((rows, n), jnp.float32),
        compiler_params=pltpu.CompilerParams(dimension_semantics=("arbitrary",),
                                             vmem_limit_bytes=PROJ_VMEM_LIMIT),
        name="norm_proj",
    )(x2d, g.astype(jnp.float32).reshape(1, d), shift.astype(jnp.float32).reshape(2, 1, d),
      scale.astype(jnp.float32).reshape(2, 1, d), w_bf16)


def adaln(cond, w, b, j):
    lo, hi = 3 * j * D_MODEL, 3 * (j + 1) * D_MODEL
    m = jax.nn.silu(cond) @ w[:, lo:hi] + b[lo:hi]
    return jnp.split(m, 3, axis=-1)


def axial_rope_tables(length):
    rows = length // GRID_W
    n_pairs = HEAD_DIM // 4
    inv = ROPE_BASE ** (-jnp.arange(n_pairs, dtype=jnp.float32) / n_pairs)
    row_ang = jnp.arange(rows, dtype=jnp.float32)[:, None] * inv
    col_ang = jnp.arange(GRID_W, dtype=jnp.float32)[:, None] * inv
    ang_r = jnp.broadcast_to(row_ang[:, None], (rows, GRID_W, n_pairs)).reshape(length, n_pairs)
    ang_c = jnp.broadcast_to(col_ang[None], (rows, GRID_W, n_pairs)).reshape(length, n_pairs)
    return jnp.cos(ang_r), jnp.sin(ang_r), jnp.cos(ang_c), jnp.sin(ang_c)


HY_N1 = 128
HY_N2 = 256
HY_NB = 8
HY_TM = 512


def _hy_dft_constants(length):
    import numpy as np
    n = 2 * length
    assert n == HY_N1 * HY_N2
    half = HY_N1 // 2
    k1 = np.arange(HY_N1)[:, None]
    n1 = np.arange(half)[None, :]
    n2 = np.arange(HY_N2)[:, None, None]
    ang = 2 * np.pi * (k1 * n1 / HY_N1)[None] + 2 * np.pi * n2 * k1[None] / n
    fwd = np.stack([np.cos(ang), -np.sin(ang)], axis=2).reshape(HY_N2, 2 * HY_N1, half)
    inv = fwd.transpose(0, 2, 1)
    a2 = 2 * np.pi * np.outer(np.arange(HY_N2), np.arange(HY_N2)) / HY_N2
    c, s = np.cos(a2), -np.sin(a2)
    m_fwd = np.concatenate([np.stack([c, -s], axis=2).reshape(HY_N2, 2 * HY_N2),
                            np.stack([s, c], axis=2).reshape(HY_N2, 2 * HY_N2)], axis=0)
    m_inv = np.stack([np.concatenate([c, s], axis=1), np.concatenate([-s, c], axis=1)],
                     axis=1).reshape(2 * HY_N2, 2 * HY_N2)
    f = np.float32
    return fwd.astype(f), inv.astype(f), m_fwd.astype(f), m_inv.astype(f)


def _hy_conv_body(p_ref, prev_ref, next_ref, w_ref, b_ref, o_ref):
    i = pl.program_id(1)
    n = pl.num_programs(1)
    u = p_ref[...]
    rows = u.shape[0]
    row = lax.broadcasted_iota(jnp.int32, u.shape, 0)
    before = jnp.where(i == 0, 0.0, prev_ref[HALO - 1:HALO, :])
    after = jnp.where(i == n - 1, 0.0, next_ref[0:1, :])
    up = jnp.where(row == 0, before, pltpu.roll(u, 1, 0))
    un = jnp.where(row == rows - 1, after, pltpu.roll(u, rows - 1, 0))
    o_ref[...] = w_ref[0:1, :] * up + w_ref[1:2, :] * u + w_ref[2:3, :] * un + b_ref[...]


def _hy_conv(p, conv_w, conv_b, tm):
    length = p.shape[0]
    per = tm // HALO
    last = length // HALO - 1
    return pl.pallas_call(
        _hy_conv_body,
        grid=(3, length // tm),
        in_specs=[pl.BlockSpec((tm, HY_W), lambda c, i: (i, c)),
                  pl.BlockSpec((HALO, HY_W), lambda c, i: (jnp.maximum(i * per - 1, 0), c)),
                  pl.BlockSpec((HALO, HY_W), lambda c, i: (jnp.minimum((i + 1) * per, last), c)),
                  pl.BlockSpec((CONV_W, HY_W), lambda c, i: (0, c)),
                  pl.BlockSpec((1, HY_W), lambda c, i: (0, c))],
        out_specs=pl.BlockSpec((None, tm, HY_W), lambda c, i: (c, i, 0)),
        out_shape=jax.ShapeDtypeStruct((3, length, HY_W), jnp.float32),
        name="hy_conv",
    )(p, p, p, conv_w, conv_b.reshape(1, -1))


def _hy_filter_body(z_ref, w1_ref, b1_ref, w2_ref, b2_ref, w3_ref, b3_ref, w4_ref, fq_ref, dl_ref,
                    e_ref, o_ref, nrm_ref):
    i = pl.program_id(0)
    bf = jnp.bfloat16
    z = z_ref[...]
    fq = fq_ref[...]

    def layer(a, w_ref, b_ref):
        return jnp.sin(fq * (jnp.dot(a.astype(bf), w_ref[...].astype(bf), preferred_element_type=jnp.float32)
                             + b_ref[...]))

    half = z.shape[0] // 2
    hdn = layer(layer(layer(jnp.concatenate([z[:half], z[half:]], axis=1), w1_ref, b1_ref), w2_ref, b2_ref),
                w3_ref, b3_ref).astype(bf)
    w4 = w4_ref[...].astype(bf)
    h = jnp.concatenate([jnp.dot(hdn[:, :HY_FFN], w4, preferred_element_type=jnp.float32),
                         jnp.dot(hdn[:, HY_FFN:], w4, preferred_element_type=jnp.float32)], axis=0)
    decay = jnp.exp(-z[:, 0:1] * dl_ref[...])
    first = (lax.broadcasted_iota(jnp.int32, decay.shape, 0) == 0) & (i == 0)
    acc = []
    for order in range(HY_ORDER):
        lo = order * 2 * HY_W
        h0 = h[:, lo:lo + HY_W] * decay
        h1 = jnp.where(first, 0.0, h[:, lo + HY_W:lo + 2 * HY_W] * decay)
        e_ref[:, order * HY_W:(order + 1) * HY_W] = h0 + h1
        o_ref[:, order * HY_W:(order + 1) * HY_W] = h0 - h1
        acc.append(jnp.sum(jnp.abs(h0) + jnp.abs(h1), axis=0, keepdims=True))
    part = jnp.concatenate(acc, axis=1)

    @pl.when(i == 0)
    def _():
        nrm_ref[...] = jnp.zeros_like(nrm_ref)

    nrm_ref[...] = nrm_ref[...] + part


def _hy_filter_time(length, w1, b1, w2, b2, w3, b3, w4, freq, tm):
    f32 = jnp.float32
    t = jnp.linspace(0.0, 1.0, length, dtype=f32)[:, None]
    bands = (HY_EMB - 1) // 2
    w_ang = 2.0 * math.pi * jnp.arange(length, dtype=f32)[:, None] / length
    fr = jnp.linspace(1e-4, bands - 1, bands, dtype=f32)[None]
    z = jnp.concatenate([t, jnp.cos(fr * w_ang), -jnp.sin(fr * w_ang)], axis=-1)
    z = jnp.pad(z, ((0, 0), (0, LANE - HY_EMB)))
    w1p = jnp.pad(w1.astype(f32), ((0, LANE - HY_EMB), (0, 0)))
    max_decay = math.log(HY_TARGET) / HY_SHORT_PCT
    min_decay = math.log(HY_TARGET) / HY_LONG_PCT
    deltas = jnp.abs(jnp.linspace(min_decay, max_decay, HY_W, dtype=f32)).reshape(1, HY_W)
    full = lambda a: pl.BlockSpec(a.shape, lambda i: (0,) * a.ndim)
    row2 = lambda v: jnp.tile(v.astype(f32).reshape(1, -1), (1, 2))
    diag2 = lambda w: jnp.kron(jnp.eye(2, dtype=f32), w.astype(f32))
    args = (z, diag2(w1p), row2(b1), diag2(w2), row2(b2), diag2(w3), row2(b3), w4.astype(f32), row2(freq), deltas)
    ow = HY_ORDER * HY_W
    return pl.pallas_call(
        _hy_filter_body,
        grid=(length // tm,),
        in_specs=[pl.BlockSpec((tm, LANE), lambda i: (i, 0))] + [full(a) for a in args[1:]],
        out_specs=[pl.BlockSpec((tm, ow), lambda i: (i, 0)), pl.BlockSpec((tm, ow), lambda i: (i, 0)),
                   pl.BlockSpec((1, ow), lambda i: (0, 0))],
        out_shape=[jax.ShapeDtypeStruct((length, ow), f32), jax.ShapeDtypeStruct((length, ow), f32),
                   jax.ShapeDtypeStruct((1, ow), f32)],
        compiler_params=pltpu.CompilerParams(dimension_semantics=("arbitrary",)),
        name="hy_filter_time",
    )(*args)


def _words(x_bf16):
    return pltpu.bitcast(x_bf16, jnp.uint32)


def _halves(w_u32):
    return pltpu.bitcast(w_u32, jnp.bfloat16)


def _transpose8(parts):
    rows, cols = parts[0].shape
    parts = [p.reshape(rows // SUBLANE, SUBLANE, cols) for p in parts]
    row = lax.broadcasted_iota(jnp.int32, parts[0].shape, 1)
    for s in (1, 2, 4):
        keep = (row & s) == 0
        nxt = list(parts)
        for i in range(SUBLANE):
            if i & s == 0:
                a, b = parts[i], parts[i + s]
                nxt[i] = jnp.where(keep, a, pltpu.roll(b, s, 1))
                nxt[i + s] = jnp.where(keep, pltpu.roll(a, SUBLANE - s, 1), b)
        parts = nxt
    return [p.reshape(rows, cols) for p in parts]


def _gather_tiles(ref, j, n):
    return jnp.concatenate([ref[SUBLANE * g + j] for g in range(n)], axis=0)


def _hy_s1_body(x_ref, f_ref, a_ref):
    groups = x_ref.shape[0] // SUBLANE
    xs = _transpose8([_gather_tiles(x_ref, j, groups) for j in range(SUBLANE)])
    words = []
    for i in range(HY_NB):
        acc = jnp.dot(f_ref[i], xs[i].astype(jnp.bfloat16), preferred_element_type=jnp.float32)
        words.append(_words(acc.astype(jnp.bfloat16)))
    tiles = _transpose8(words)
    per = HY_N1 // SUBLANE
    for j in range(SUBLANE):
        for g in range(per):
            a_ref[per * j + g] = tiles[j][SUBLANE * g:SUBLANE * (g + 1)]


def _hy_s1(x4, sel, fwd):
    assert HY_NB == SUBLANE
    _, half, _, width = x4.shape
    return pl.pallas_call(
        _hy_s1_body,
        grid=(HY_N2 // HY_NB,),
        in_specs=[pl.BlockSpec((None, half, HY_NB, width), lambda j: (sel, 0, j, 0)),
                  pl.BlockSpec((HY_NB, 2 * HY_N1, half), lambda j: (j, 0, 0))],
        out_specs=pl.BlockSpec((HY_N1, HY_NB, width), lambda j: (0, j, 0)),
        out_shape=jax.ShapeDtypeStruct((HY_N1, HY_N2, width), jnp.uint32),
        name="hy_stage1",
    )(x4, fwd)


HY_SLABS = 4


def _hy_s2f_body(ae_ref, ao_ref, m_ref, sc_ref, hr_ref, hi_ref):
    dot = lambda a, b: jnp.dot(a, b, preferred_element_type=jnp.float32)
    for s in range(HY_SLABS):
        hr_ref[s] = (dot(m_ref[:HY_N2, :], _halves(ae_ref[s])) * sc_ref[...]).astype(hr_ref.dtype)
        hi_ref[s] = (dot(m_ref[HY_N2:, :], _halves(ao_ref[s])) * sc_ref[...]).astype(hi_ref.dtype)


def _hy_s2f(a_e, a_o, m_fwd, scale):
    width = scale.shape[1]
    slab = pl.BlockSpec((HY_SLABS, HY_N2, width), lambda k: (k, 0, 0))
    out = jax.ShapeDtypeStruct((HY_N1, HY_N2, width), jnp.bfloat16)
    return pl.pallas_call(
        _hy_s2f_body,
        grid=(HY_N1 // HY_SLABS,),
        in_specs=[slab, slab, pl.BlockSpec((2 * HY_N2, 2 * HY_N2), lambda k: (0, 0)),
                  pl.BlockSpec((1, width), lambda k: (0, 0))],
        out_specs=[slab, slab],
        out_shape=[out, out],
        name="hy_filter_stage2",
    )(a_e, a_o, m_fwd, scale)


def _hy_s2_body(a_ref, hr_ref, hi_ref, mf_ref, mi_ref, b_ref):
    bf = jnp.bfloat16
    for s in range(HY_SLABS):
        x = jnp.dot(mf_ref[...], _halves(a_ref[s]), preferred_element_type=jnp.float32)
        xr, xi = x[:HY_N2], x[HY_N2:]
        hr = hr_ref[s].astype(jnp.float32)
        hi = hi_ref[s].astype(jnp.float32)
        y = jnp.concatenate([(xr * hr - xi * hi).astype(bf), (xr * hi + xi * hr).astype(bf)], axis=0)
        b = jnp.dot(mi_ref[...], y, preferred_element_type=jnp.float32)
        b_ref[s] = _words(b.astype(bf))


def _hy_s2(a, h_re, h_im, m_fwd, m_inv, order):
    slab = pl.BlockSpec((HY_SLABS, HY_N2, HY_W), lambda k: (k, 0, 0))
    hslab = pl.BlockSpec((HY_SLABS, HY_N2, HY_W), lambda k: (k, 0, order))
    mat = pl.BlockSpec((2 * HY_N2, 2 * HY_N2), lambda k: (0, 0))
    return pl.pallas_call(
        _hy_s2_body,
        grid=(HY_N1 // HY_SLABS,),
        in_specs=[slab, hslab, hslab, mat, mat],
        out_specs=slab,
        out_shape=jax.ShapeDtypeStruct((HY_N1, HY_N2, HY_W), jnp.uint32),
        name="hy_stage2",
    )(a, h_re, h_im, m_fwd, m_inv)


def _hy_is1_body(b_ref, g_ref, z_ref, gate_ref, bias_ref, o_ref):
    per = HY_N1 // SUBLANE
    spectra = _transpose8([jnp.concatenate([b_ref[per * j + g] for g in range(per)], axis=0)
                           for j in range(SUBLANE)])
    convs = [jnp.dot(g_ref[i], _halves(spectra[i]), preferred_element_type=jnp.float32) for i in range(HY_NB)]
    tiles = _transpose8(convs)
    for j in range(SUBLANE):
        for g in range(o_ref.shape[0] // SUBLANE):
            n1 = SUBLANE * g + j
            o_ref[n1] = gate_ref[n1] * (tiles[j][SUBLANE * g:SUBLANE * (g + 1)] + z_ref[n1] * bias_ref[...])


def _hy_is1(b, inv, z4, z_sel, gate4, gate_sel, bias):
    _, half, _, width = z4.shape
    real = lambda sel: pl.BlockSpec((None, half, HY_NB, width), lambda j: (sel, 0, j, 0))
    return pl.pallas_call(
        _hy_is1_body,
        grid=(HY_N2 // HY_NB,),
        in_specs=[pl.BlockSpec((HY_N1, HY_NB, width), lambda j: (0, j, 0)),
                  pl.BlockSpec((HY_NB, half, 2 * HY_N1), lambda j: (j, 0, 0)),
                  real(z_sel), real(gate_sel), pl.BlockSpec((1, width), lambda j: (0, 0))],
        out_specs=pl.BlockSpec((half, HY_NB, width), lambda j: (0, j, 0)),
        out_shape=jax.ShapeDtypeStruct((half, HY_N2, width), jnp.float32),
        name="hy_inv_stage1",
    )(b, inv, z4, gate4, bias)


def hyena_long(p_hy, f_w1, f_b1, f_w2, f_b2, f_w3, f_b3, f_w4, f_freq, f_bias, conv_w, conv_b):
    length = p_hy.shape[0]
    bf = jnp.bfloat16
    half = HY_N1 // 2
    fwd, inv, m_fwd, m_inv = (jnp.asarray(m, bf) for m in _hy_dft_constants(length))
    e, od, nrm = _hy_filter_time(length, f_w1, f_b1, f_w2, f_b2, f_w3, f_b3, f_w4, f_freq, HY_TM)
    ow = HY_ORDER * HY_W
    a_e = _hy_s1(e.reshape(1, half, HY_N2, ow), 0, fwd)
    a_o = _hy_s1(od.reshape(1, half, HY_N2, ow), 0, fwd)
    scale = 1.0 / (nrm * (2.0 * length))
    h_re, h_im = _hy_s2f(a_e, a_o, m_fwd, scale)
    u = _hy_conv(p_hy, conv_w, conv_b, HY_TM).reshape(3, half, HY_N2, HY_W)
    z, z_sel = u, 0
    for order in range(HY_ORDER):
        b = _hy_s2(_hy_s1(z, z_sel, fwd), h_re, h_im, m_fwd, m_inv, order)
        z = _hy_is1(b, inv, z, z_sel, u, 1 + order, f_bias[order].astype(jnp.float32).reshape(1, HY_W))[None]
        z_sel = 0
    return z.reshape(length, HY_W)


ATT_TQ = ATT_BLOCK
EVEN_TM = 256
EV_Q_BLK = 3 * HY_W // ATT_Q_W
EV_K_BLK = (3 * HY_W + ATT_Q_W) // ATT_KV_W
EV_V_BLK = EV_K_BLK + 1


def _rope_tables(length):
    cr, sr, cc, sc = axial_rope_tables(length)
    return jnp.concatenate([cr, cr, cc, cc], axis=-1), jnp.concatenate([-sr, sr, -sc, sc], axis=-1)


def _head_norm_rope(x, g_row, c, s, seg):
    sq = x * x
    hi = sq.astype(jnp.bfloat16)
    lo = (sq - hi.astype(jnp.float32)).astype(jnp.bfloat16)
    ms = (jnp.dot(hi, seg, preferred_element_type=jnp.float32) + jnp.dot(lo, seg, preferred_element_type=jnp.float32))
    y = x * lax.rsqrt(ms + EPS) * g_row
    width = x.shape[1]
    quarter = HEAD_DIM // 4
    lane = lax.broadcasted_iota(jnp.int32, x.shape, 1)
    partner = jnp.where((lane & quarter) == 0, pltpu.roll(y, width - quarter, 1), pltpu.roll(y, quarter, 1))
    return y * c + partner * s


def _qk_prep_body(q_ref, k_ref, c_ref, s_ref, qn_ref, kn_ref, segq_ref, segk_ref, qo_ref, ko_ref):
    c, s = c_ref[...], s_ref[...]
    tile = lambda t, n: jnp.concatenate([t] * n, axis=1)
    qo_ref[...] = _head_norm_rope(q_ref[...], tile(qn_ref[...], ATT_HEADS), tile(c, ATT_HEADS), tile(s, ATT_HEADS),
                                  segq_ref[...]).astype(qo_ref.dtype)
    ko_ref[...] = _head_norm_rope(k_ref[...], tile(kn_ref[...], ATT_KV_HEADS), tile(c, ATT_KV_HEADS),
                                  tile(s, ATT_KV_HEADS), segk_ref[...]).astype(ko_ref.dtype)


def _qk_prep(p, rope_c, rope_s, q_norm, k_norm, tm):
    import numpy as np
    rows = p.shape[0]
    seg = lambda w: jnp.asarray(np.kron(np.eye(w // HEAD_DIM), np.full((HEAD_DIM, HEAD_DIM), 1.0 / HEAD_DIM)),
                                jnp.bfloat16)
    const = lambda shape: pl.BlockSpec(shape, lambda i: (0, 0))
    return pl.pallas_call(
        _qk_prep_body,
        grid=(rows // tm,),
        in_specs=[pl.BlockSpec((tm, ATT_Q_W), lambda i: (i, EV_Q_BLK)),
                  pl.BlockSpec((tm, ATT_KV_W), lambda i: (i, EV_K_BLK)),
                  pl.BlockSpec((tm, HEAD_DIM), lambda i: (i, 0)), pl.BlockSpec((tm, HEAD_DIM), lambda i: (i, 0)),
                  const((1, HEAD_DIM)), const((1, HEAD_DIM)),
                  const((ATT_Q_W, ATT_Q_W)), const((ATT_KV_W, ATT_KV_W))],
        out_specs=[pl.BlockSpec((tm, ATT_Q_W), lambda i: (i, 0)), pl.BlockSpec((tm, ATT_KV_W), lambda i: (i, 0))],
        out_shape=[jax.ShapeDtypeStruct((rows, ATT_Q_W), jnp.bfloat16),
                   jax.ShapeDtypeStruct((rows, ATT_KV_W), jnp.bfloat16)],
        name="qk_prep",
    )(p, p, rope_c, rope_s, q_norm.astype(jnp.float32).reshape(1, -1), k_norm.astype(jnp.float32).reshape(1, -1),
      seg(ATT_Q_W), seg(ATT_KV_W))


def _att_body(sink_ref, q_ref, kc_ref, vc_ref, *rest, local):
    if local:
        kp_ref, k0_ref, kn_ref, vp_ref, v0_ref, vn_ref, o_ref = rest
    else:
        (o_ref,) = rest
    b = pl.program_id(0)
    nb = pl.num_programs(0)
    bf = jnp.bfloat16
    scale = HEAD_DIM ** -0.5
    q = q_ref[...]
    kc = kc_ref[...]
    vc = vc_ref[...].astype(bf)
    if local:
        kb = jnp.concatenate([kp_ref[...], k0_ref[...], kn_ref[...]], axis=0)
        vb = jnp.concatenate([vp_ref[...], v0_ref[...], vn_ref[...]], axis=0).astype(bf)
        i = lax.broadcasted_iota(jnp.int32, (ATT_TQ, 3 * ATT_BLOCK), 0)
        j = lax.broadcasted_iota(jnp.int32, (ATT_TQ, 3 * ATT_BLOCK), 1)
        rel = j - ATT_BLOCK - i
        valid = ((jnp.abs(rel) <= ATT_WINDOW) & ((b > 0) | (j >= ATT_BLOCK))
                 & ((b < nb - 1) | (j < 2 * ATT_BLOCK)))
    rows = ATT_GROUP * ATT_TQ
    head_of_row = lax.broadcasted_iota(jnp.int32, (rows, 1), 0) // ATT_TQ
    for hk in range(ATT_KV_HEADS):
        heads = range(hk * ATT_GROUP, (hk + 1) * ATT_GROUP)
        qs = jnp.concatenate([q[:, h * HEAD_DIM:(h + 1) * HEAD_DIM] for h in heads], axis=0)
        kv_cols = slice(hk * HEAD_DIM, (hk + 1) * HEAD_DIM)
        sink = jnp.zeros((rows, 1), jnp.float32)
        for g, h in enumerate(heads):
            sink = jnp.where(head_of_row == g, sink_ref[h], sink)
        s_ctx = _dot_nt(qs, kc[:, kv_cols]) * scale
        m = jnp.maximum(jnp.max(s_ctx, axis=-1, keepdims=True), sink)
        if local:
            valid4 = jnp.concatenate([valid] * ATT_GROUP, axis=0)
            s_loc = jnp.where(valid4, _dot_nt(qs, kb[:, kv_cols]) * scale, -jnp.inf)
            m = jnp.maximum(m, jnp.max(s_loc, axis=-1, keepdims=True))
        p_ctx = jnp.exp(s_ctx - m)
        den = jnp.sum(p_ctx, axis=-1, keepdims=True) + jnp.exp(sink - m)
        acc = jnp.dot(p_ctx.astype(bf), vc[:, kv_cols], preferred_element_type=jnp.float32)
        if local:
            p_loc = jnp.exp(s_loc - m)
            den = den + jnp.sum(p_loc, axis=-1, keepdims=True)
            acc = acc + jnp.dot(p_loc.astype(bf), vb[:, kv_cols], preferred_element_type=jnp.float32)
        out = acc / den
        for g, h in enumerate(heads):
            o_ref[:, h * HEAD_DIM:(h + 1) * HEAD_DIM] = out[g * ATT_TQ:(g + 1) * ATT_TQ]


def _attention(qn, kn, p, kcn, pc, sink, local):
    rows = qn.shape[0]
    nb = rows // ATT_TQ
    n_ctx = kcn.shape[0]
    specs = [pl.BlockSpec((ATT_TQ, ATT_Q_W), lambda b, s: (b, 0)),
             pl.BlockSpec((n_ctx, ATT_KV_W), lambda b, s: (0, 0)),
             pl.BlockSpec((n_ctx, ATT_KV_W), lambda b, s: (0, EV_V_BLK))]
    args = [qn, kcn, pc]
    if local:
        prev = lambda b, s: jnp.maximum(b - 1, 0)
        nxt = lambda b, s: jnp.minimum(b + 1, nb - 1)
        for col, arr in ((0, kn), (EV_V_BLK, p)):
            specs += [pl.BlockSpec((ATT_BLOCK, ATT_KV_W), lambda b, s, col=col: (prev(b, s), col)),
                      pl.BlockSpec((ATT_BLOCK, ATT_KV_W), lambda b, s, col=col: (b, col)),
                      pl.BlockSpec((ATT_BLOCK, ATT_KV_W), lambda b, s, col=col: (nxt(b, s), col))]
            args += [arr, arr, arr]
    return pl.pallas_call(
        functools.partial(_att_body, local=local),
        grid_spec=pltpu.PrefetchScalarGridSpec(
            num_scalar_prefetch=1, grid=(nb,), in_specs=specs,
            out_specs=pl.BlockSpec((ATT_TQ, ATT_Q_W), lambda b, s: (b, 0))),
        out_shape=jax.ShapeDtypeStruct((rows, ATT_Q_W), jnp.float32),
        name="window_attention" if local else "context_attention",
    )(sink.astype(jnp.float32), *args)


def _even_merge_body(hy_ref, att_ref, x_ref, gt_ref, w_ref, o_ref):
    bf = jnp.bfloat16
    m = (jnp.dot(hy_ref[...].astype(bf), w_ref[:HY_W, :], preferred_element_type=jnp.float32)
         + jnp.dot(att_ref[...].astype(bf), w_ref[HY_W:, :], preferred_element_type=jnp.float32))
    o_ref[...] = x_ref[...] + gt_ref[...] * m


def _even_merge(hy, att, x2d, gt, w_out, tm):
    rows, d = x2d.shape
    half = pl.BlockSpec((tm, HY_W), lambda i: (i, 0))
    return pl.pallas_call(
        _even_merge_body,
        grid=(rows // tm,),
        in_specs=[half, half, pl.BlockSpec((tm, d), lambda i: (i, 0)), pl.BlockSpec((1, d), lambda i: (0, 0)),
                  pl.BlockSpec((MIX_W, d), lambda i: (0, 0))],
        out_specs=pl.BlockSpec((tm, d), lambda i: (i, 0)),
        out_shape=jax.ShapeDtypeStruct((rows, d), jnp.float32),
        name="even_merge",
    )(hy, att, x2d, gt.reshape(1, -1), w_out.astype(jnp.bfloat16))


def _hy_short_body(e_ref, o_ref, sc_ref, u_ref, bias_ref, cf_ref, sf_ref, out_ref):
    bf = jnp.bfloat16
    cf, sf = cf_ref[...], sf_ref[...]
    dot = lambda a, b: jnp.dot(a, b.astype(bf), preferred_element_type=jnp.float32)
    z = u_ref[0]
    for order in range(HY_ORDER):
        cols = slice(order * HY_W, (order + 1) * HY_W)
        h_re = dot(cf, e_ref[:, cols]) * sc_ref[:, cols]
        h_im = dot(sf, o_ref[:, cols]) * sc_ref[:, cols]
        x_re, x_im = dot(cf, z), dot(sf, z)
        y_re = (x_re * h_re - x_im * h_im).astype(bf)
        y_im = (x_re * h_im + x_im * h_re).astype(bf)
        conv = _dot_tn(cf, y_re) + _dot_tn(sf, y_im)
        z = u_ref[1 + order] * (conv + z * bias_ref[order:order + 1, :])
    out_ref[...] = z


def hyena_short(pc, f_w1, f_b1, f_w2, f_b2, f_w3, f_b3, f_w4, f_freq, f_bias, conv_w, conv_b):
    import numpy as np
    rows = pc.shape[0]
    e, od, nrm = _hy_filter_time(rows, f_w1, f_b1, f_w2, f_b2, f_w3, f_b3, f_w4, f_freq, rows)
    u = _hy_conv(pc, conv_w, conv_b, rows)
    ang = 2 * np.pi * np.outer(np.arange(2 * rows), np.arange(rows)) / (2 * rows)
    cf, sf = jnp.asarray(np.cos(ang), jnp.bfloat16), jnp.asarray(-np.sin(ang), jnp.bfloat16)
    scale = 1.0 / (nrm * (2.0 * rows))
    return pl.pallas_call(
        _hy_short_body,
        out_shape=jax.ShapeDtypeStruct((rows, HY_W), jnp.float32),
        name="hy_short",
    )(e, od, scale, u, f_bias.astype(jnp.float32), cf, sf)


def even_layer(x, xc, g, mod, mod_c, w_in, conv_w, conv_b, f_w1, f_b1, f_w2, f_b2, f_w3, f_b3, f_w4, f_freq, f_bias,
               q_norm, k_norm, sink, gt, gt_c, w_out):
    f32 = jnp.float32
    length, n_ctx = x.shape[1], xc.shape[1]
    filt = (f_w1, f_b1, f_w2, f_b2, f_w3, f_b3, f_w4, f_freq, f_bias, conv_w, conv_b)
    o_q = 2 * ATT_KV_W
    o_hy = o_q + ATT_Q_W
    w_perm = jnp.concatenate([w_in[:, o_hy:], w_in[:, o_q:o_hy], w_in[:, :o_q]], axis=1).astype(jnp.bfloat16)
    twice = lambda v: jnp.stack([v.reshape(-1), v.reshape(-1)])
    p = norm_proj(x[0], g, twice(mod[0]), twice(mod[1]), w_perm)
    pc = norm_proj(xc[0], g, twice(mod_c[0]), twice(mod_c[1]), w_perm)
    rope_c, rope_s = _rope_tables(length)
    qn, kn = _qk_prep(p, rope_c, rope_s, q_norm, k_norm, EVEN_TM)
    qcn, kcn = _qk_prep(pc, jnp.ones((n_ctx, HEAD_DIM), f32), jnp.zeros((n_ctx, HEAD_DIM), f32), q_norm, k_norm,
                        n_ctx)
    att = _attention(qn, kn, p, kcn, pc, sink, True)
    att_c = _attention(qcn, None, None, kcn, pc, sink, False)
    hy = hyena_long(p, *filt)
    hy_c = hyena_short(pc, *filt)
    x_new = _even_merge(hy, att, x[0], gt, w_out, EVEN_TM)
    xc_new = _even_merge(hy_c, att_c, xc[0], gt_c, w_out, n_ctx)
    return x_new[None], xc_new[None]


SCAN_Q = 128
SCAN_LEVELS = 7
ODD_COLS = SSD_CONV_CH + 8 * 512
HALO = SUBLANE


def _scan_constants():
    import numpy as np
    q = SCAN_Q
    d_hg, d_ssd, pairs, laters = [], [], [], []
    for direction in (0, 1):
        pos = np.arange(q) if direction == 0 else q - 1 - np.arange(q)
        pj, pt = pos[None, :], pos[:, None]
        top = pj <= pt
        end = pj > pt
        ones = np.ones((SUBLANE, q), bool)
        lv, pr, lt = [], [], []
        for level in range(SCAN_LEVELS):
            b = 2 ** level
            start = (pos // (2 * b)) * (2 * b)
            mid = (start + b)[:, None]
            later = pos >= start + b
            lv.append(np.where(later[:, None], (pj >= mid) & (pj <= pt), (pj > pt) & (pj < mid)))
            pr.append((start[:, None] == start[None, :]) & later[:, None] & ~later[None, :])
            lt.append(np.broadcast_to(later[:, None], (q, LANE)))
        pr.append(np.eye(q, dtype=bool))
        pr.append(top)
        d_hg.append(np.concatenate([top, end] + lv + [ones], axis=0))
        d_ssd.append(np.concatenate([top, end, ones], axis=0))
        pairs.append(np.stack(pr))
        laters.append(np.stack(lt))
    f = np.float32
    twice = lambda m: np.concatenate([m, m], axis=-1)
    return (twice(np.stack(d_hg)).astype(f), twice(np.stack(d_ssd)).astype(f), np.stack(pairs).astype(f),
            np.stack(laters).astype(f))


def _split_dot(mm_bf16, v):
    hi = v.astype(jnp.bfloat16)
    lo = (v - hi.astype(jnp.float32)).astype(jnp.bfloat16)
    return jnp.dot(mm_bf16, jnp.concatenate([hi, lo], axis=0), preferred_element_type=jnp.float32)


def _dot_nt(a, b):
    return lax.dot_general(a, b, (((1,), (1,)), ((), ())), preferred_element_type=jnp.float32)


def _dot_tn(a, b):
    return lax.dot_general(a, b, (((0,), (0,)), ((), ())), preferred_element_type=jnp.float32)


def _softplus(x):
    return jnp.maximum(x, 0.0) + jnp.log1p(jnp.exp(-jnp.abs(x)))


def _scan_body(xbc_ref, prev_ref, next_ref, f_ref, iv_ref, q_ref, dt_ref,
               cw_ref, cb_ref, dtb_ref, a_ref, dsk_ref, lb_ref,
               dhg_ref, dssd_ref, pair_ref, later_ref, sel_ref,
               out_ref, s_ssd, s_hg, ydiag_ref, *, n_ctx_chunks, n_chunks):
    d = pl.program_id(0)
    j = pl.program_id(1)
    q_rows = SCAN_Q
    bf = jnp.bfloat16

    @pl.when(j == 0)
    def _():
        s_ssd[...] = jnp.zeros_like(s_ssd)
        s_hg[...] = jnp.zeros_like(s_hg)

    c = jnp.where(d == 0, j, jnp.where(j < n_ctx_chunks, n_ctx_chunks - 1 - j, n_chunks - 1 + n_ctx_chunks - j))
    first = (c == 0) | (c == n_ctx_chunks)
    last = (c == n_ctx_chunks - 1) | (c == n_chunks - 1)

    u = xbc_ref[...]
    row = lax.broadcasted_iota(jnp.int32, u.shape, 0)
    before = jnp.where(first, 0.0, prev_ref[HALO - 1:HALO, :])
    after = jnp.where(last, 0.0, next_ref[0:1, :])
    up = jnp.where(row == 0, before, pltpu.roll(u, 1, 0))
    un = jnp.where(row == q_rows - 1, after, pltpu.roll(u, q_rows - 1, 0))
    xbc = cw_ref[0:1, :] * up + cw_ref[1:2, :] * u + cw_ref[2:3, :] * un + cb_ref[...]
    xbc = xbc * jax.nn.sigmoid(xbc)
    x = xbc[:, :SSD_W]

    dt = _softplus(dt_ref[...] + dtb_ref[...])
    da = dt * a_ref[...]
    r = _split_dot(dssd_ref[...], da)
    cs, to_end, total = r[:q_rows], r[q_rows:2 * q_rows], r[2 * q_rows:2 * q_rows + 1]
    xdt = x * dt
    cs_hi = cs.astype(bf)
    cs_lo = (cs - cs_hi.astype(jnp.float32)).astype(bf)
    cs_rows = _dot_nt(sel_ref[...], cs_hi) + _dot_nt(sel_ref[...], cs_lo)
    l_mask = pair_ref[SCAN_LEVELS + 1]
    decay_in = jnp.exp2(cs)
    w_end = (jnp.exp2(to_end) * xdt).astype(bf)
    gn = SSD_GROUPS * SSD_STATE
    hpg = SSD_HEADS // SSD_GROUPS
    gw = hpg * SSD_HEAD_DIM
    for g in range(SSD_GROUPS):
        b_g = xbc[:, SSD_W + g * SSD_STATE:SSD_W + (g + 1) * SSD_STATE].astype(bf)
        c_g = xbc[:, SSD_W + gn + g * SSD_STATE:SSD_W + gn + (g + 1) * SSD_STATE].astype(bf)
        scores = _dot_nt(c_g, b_g)
        y_off = jnp.dot(c_g, s_ssd[g].astype(bf), preferred_element_type=jnp.float32)
        for hh in range(hpg):
            h = g * hpg + hh
            lo = h * SSD_HEAD_DIM
            diff = cs[:, lo:lo + 1] - cs_rows[h:h + 1, :]
            decay = jnp.exp2(jnp.minimum(diff, 0.0)) * l_mask
            ydiag_ref[:, lo:lo + SSD_HEAD_DIM] = jnp.dot((scores * decay).astype(bf),
                                                         xdt[:, lo:lo + SSD_HEAD_DIM].astype(bf),
                                                         preferred_element_type=jnp.float32)
        cols = slice(g * gw, (g + 1) * gw)
        out_ref[:, cols] = (ydiag_ref[:, cols] + decay_in[:, cols] * y_off + dsk_ref[:, cols] * x[:, cols])
        s_ssd[g] = jnp.exp2(total[:, cols]) * s_ssd[g] + _dot_tn(b_g, w_end[:, cols])

    lb = lb_ref[...]
    f = lb + (1.0 - lb) * jax.nn.sigmoid(f_ref[...])
    k_in = 1.0 - f
    qv = q_ref[...]
    qv = qv * jax.nn.sigmoid(qv)
    v_bf = iv_ref[...].astype(bf)
    e = jnp.exp2(_split_dot(dhg_ref[...], jnp.log2(f)))
    e_top, e_end = e[:q_rows], e[q_rows:2 * q_rows]
    e_tot = e[(2 + SCAN_LEVELS) * q_rows:(2 + SCAN_LEVELS) * q_rows + 1]
    for h in range(HG_HEADS):
        cols = slice(h * HG_EXPAND, (h + 1) * HG_EXPAND)
        q_h, k_h = qv[:, cols], k_in[:, cols]
        att = pair_ref[SCAN_LEVELS] * _dot_nt(q_h.astype(bf), k_h.astype(bf))
        for level in range(SCAN_LEVELS):
            e_l = e[(2 + level) * q_rows:(3 + level) * q_rows, cols]
            w_l = (jnp.where(later_ref[level] > 0.0, q_h, k_h) * e_l).astype(bf)
            att = att + pair_ref[level] * _dot_nt(w_l, w_l)
        o = jnp.dot(att.astype(bf), v_bf[:, cols], preferred_element_type=jnp.float32)
        o = o + _dot_nt((q_h * e_top[:, cols]).astype(bf), s_hg[h].astype(bf))
        out_ref[:, SSD_W + h * HG_VDIM:SSD_W + (h + 1) * HG_VDIM] = o
        s_hg[h] = e_tot[:, cols] * s_hg[h] + _dot_tn(v_bf[:, cols], (k_h * e_end[:, cols]).astype(bf))


def _odd_scan(p_all, conv_w, conv_b, dtb, a_cols, dsk, lb, n_ctx):
    n_rows = p_all.shape[0]
    n_chunks = n_rows // SCAN_Q
    ncc = n_ctx // SCAN_Q
    d_hg, d_ssd, pairs, laters = _scan_constants()
    bf = jnp.bfloat16
    import numpy as np
    sel = np.zeros((LANE, SSD_W), np.float32)
    sel[np.arange(SSD_HEADS), np.arange(SSD_HEADS) * SSD_HEAD_DIM] = 1.0

    def chunk(d, j):
        return jnp.where(d == 0, j, jnp.where(j < ncc, ncc - 1 - j, n_chunks - 1 + ncc - j))

    per = SCAN_Q // HALO
    last_halo = n_rows // HALO - 1
    col512 = lambda blk: (lambda d, j: (chunk(d, j), blk))
    const2 = lambda shape: pl.BlockSpec(shape, lambda d, j: (0,) * len(shape))
    dirc = lambda shape: pl.BlockSpec((None,) + shape, lambda d, j: (d,) + (0,) * len(shape))
    body = functools.partial(_scan_body, n_ctx_chunks=ncc, n_chunks=n_chunks)
    return pl.pallas_call(
        body,
        grid=(2, n_chunks),
        in_specs=[
            pl.BlockSpec((SCAN_Q, SSD_CONV_CH), lambda d, j: (chunk(d, j), 0)),
            pl.BlockSpec((HALO, SSD_CONV_CH), lambda d, j: (jnp.maximum(chunk(d, j) * per - 1, 0), 0)),
            pl.BlockSpec((HALO, SSD_CONV_CH), lambda d, j: (jnp.minimum((chunk(d, j) + 1) * per, last_halo), 0)),
            pl.BlockSpec((SCAN_Q, 512), lambda d, j: (chunk(d, j), 2 + d)),
            pl.BlockSpec((SCAN_Q, 512), col512(4)),
            pl.BlockSpec((SCAN_Q, 512), col512(6)),
            pl.BlockSpec((SCAN_Q, 512), lambda d, j: (chunk(d, j), 8 + d)),
            const2((CONV_W, SSD_CONV_CH)), const2((1, SSD_CONV_CH)),
            dirc((1, SSD_W)), dirc((1, SSD_W)), dirc((1, SSD_W)), const2((1, HG_W)),
            dirc(d_hg.shape[1:]), dirc(d_ssd.shape[1:]), dirc(pairs.shape[1:]), dirc(laters.shape[1:]),
            const2((LANE, SSD_W)),
        ],
        out_specs=pl.BlockSpec((None, SCAN_Q, MIX_W), lambda d, j: (d, chunk(d, j), 0)),
        out_shape=jax.ShapeDtypeStruct((2, n_rows, MIX_W), jnp.float32),
        scratch_shapes=[pltpu.VMEM((SSD_GROUPS, SSD_STATE, SSD_W // SSD_GROUPS), jnp.float32),
                        pltpu.VMEM((HG_HEADS, HG_VDIM, HG_EXPAND), jnp.float32),
                        pltpu.VMEM((SCAN_Q, SSD_W), jnp.float32)],
        compiler_params=pltpu.CompilerParams(dimension_semantics=("arbitrary", "arbitrary"),
                                             vmem_limit_bytes=MOE_VMEM_LIMIT),
        name="odd_scan",
    )(p_all, p_all, p_all, p_all, p_all, p_all, p_all,
      conv_w, conv_b.reshape(1, -1), dtb, a_cols, dsk, lb.reshape(1, -1),
      jnp.asarray(d_hg, bf), jnp.asarray(d_ssd, bf), jnp.asarray(pairs), jnp.asarray(laters),
      jnp.asarray(sel, bf))


def _group_rms(v, width):
    parts = []
    for lo in range(0, v.shape[1], width):
        seg = v[:, lo:lo + width]
        parts.append(seg * lax.rsqrt(jnp.mean(seg * seg, axis=-1, keepdims=True) + EPS))
    return jnp.concatenate(parts, axis=1)


def _odd_merge_body(yo_ref, z_ref, g_ref, x_ref, sn_ref, hn_ref, gt_ref, w_ref, o_ref):
    yo = yo_ref[0] + yo_ref[1]
    z = z_ref[...]
    g = g_ref[...]
    ys = _group_rms(yo[:, :SSD_W] * (z * jax.nn.sigmoid(z)), SSD_W // SSD_GROUPS) * sn_ref[...]
    hs = _group_rms(yo[:, SSD_W:], HG_VDIM) * hn_ref[...] * (g * jax.nn.sigmoid(g))
    m = jnp.concatenate([ys, hs], axis=1).astype(jnp.bfloat16)
    o_ref[...] = x_ref[...] + gt_ref[...] * jnp.dot(m, w_ref[...], preferred_element_type=jnp.float32)


ODD_TM = 256


def _odd_merge(yo, p_all, x_all, ssd_norm, hg_norm, gt, w_out, n_ctx):
    n_all, d = x_all.shape
    n_lat = n_all - n_ctx
    skip = n_ctx // ODD_TM
    return pl.pallas_call(
        _odd_merge_body,
        grid=(n_lat // ODD_TM,),
        in_specs=[pl.BlockSpec((2, ODD_TM, MIX_W), lambda i: (0, i + skip, 0)),
                  pl.BlockSpec((ODD_TM, 512), lambda i: (i + skip, 5)),
                  pl.BlockSpec((ODD_TM, 512), lambda i: (i + skip, 7)),
                  pl.BlockSpec((ODD_TM, d), lambda i: (i + skip, 0)),
                  pl.BlockSpec((1, SSD_W), lambda i: (0, 0)),
                  pl.BlockSpec((1, HG_W), lambda i: (0, 0)),
                  pl.BlockSpec((1, d), lambda i: (0, 0)),
                  pl.BlockSpec((MIX_W, d), lambda i: (0, 0))],
        out_specs=pl.BlockSpec((ODD_TM, d), lambda i: (i, 0)),
        out_shape=jax.ShapeDtypeStruct((n_lat, d), jnp.float32),
        compiler_params=pltpu.CompilerParams(dimension_semantics=("arbitrary",)),
        name="odd_merge",
    )(yo, p_all, p_all, x_all, ssd_norm.reshape(1, -1), hg_norm.reshape(1, -1), gt.reshape(1, -1),
      w_out.astype(jnp.bfloat16))


def odd_layer(x_all, n_ctx, g, mod, mod_c, lb, w_in, conv_w, conv_b, dt_bias, a_log, d_skip, ssd_norm, hg_norm, gt,
              w_out):
    f32 = jnp.float32
    o_dt = SSD_CONV_CH
    o_f = o_dt + 2 * SSD_HEADS
    rep = lambda v: jnp.repeat(v, SSD_HEAD_DIM, axis=-1)
    w_perm = jnp.concatenate([w_in[:, :o_dt], w_in[:, o_f:], rep(w_in[:, o_dt:o_dt + SSD_HEADS]),
                              rep(w_in[:, o_dt + SSD_HEADS:o_f])], axis=1).astype(jnp.bfloat16)
    both = lambda a, b: jnp.stack([a.reshape(-1), b.reshape(-1)])
    p_all = norm_proj(x_all, g, both(mod_c[0], mod[0]), both(mod_c[1], mod[1]), w_perm, n_first=n_ctx)
    dtb = rep(dt_bias.astype(f32)).reshape(2, 1, SSD_W)
    a_cols = rep(-jnp.exp(a_log.astype(f32)) * math.log2(math.e)).reshape(2, 1, SSD_W)
    dsk = rep(d_skip.astype(f32)).reshape(2, 1, SSD_W)
    yo = _odd_scan(p_all, conv_w, conv_b, dtb, a_cols, dsk, lb.astype(f32), n_ctx)
    return _odd_merge(yo, p_all, x_all, ssd_norm, hg_norm, gt, w_out, n_ctx)


MOE_TM = 256
MOE_BM = 256
NEG_BIG = -1e30
MOE_ISSUE_UNROLL = 8
MOE_TILE_ROWS = 64
MOE_VMEM_LIMIT = 52 * 1024 * 1024


def _route_body(x_ref, g_ref, sh_ref, sc_ref, rw_ref, rb_ref, cnt0_ref,
                h_ref, idx_ref, gate_ref, rank_ref, cnt_ref, run_ref):
    i = pl.program_id(0)

    @pl.when(i == 0)
    def _():
        run_ref[...] = cnt0_ref[...]

    tm = x_ref.shape[0]
    x = x_ref[...]
    t = (x * lax.rsqrt(jnp.mean(x * x, axis=-1, keepdims=True) + EPS) * g_ref[...]) * (1.0 + sc_ref[...]) + sh_ref[...]
    h_ref[...] = t
    logits = jnp.dot(t.astype(jnp.bfloat16), rw_ref[...].astype(jnp.bfloat16),
                     preferred_element_type=jnp.float32) + rb_ref[...]
    lane = lax.broadcasted_iota(jnp.int32, (tm, LANE), 1)
    lane_f = lane.astype(jnp.float32)
    work = logits
    vals, sels, hots = [], [], []
    for _ in range(TOP_K):
        m = jnp.max(work, axis=-1, keepdims=True)
        sel = jnp.min(jnp.where(work == m, lane_f, float(LANE)), axis=-1, keepdims=True)
        hot = lane_f == sel
        vals.append(m)
        sels.append(sel.astype(jnp.int32))
        hots.append(hot)
        work = jnp.where(hot, -jnp.inf, work)
    exps = [jnp.exp(v - vals[0]) for v in vals]
    denom = exps[0] + exps[1] + exps[2] + exps[3]
    chosen = jnp.zeros((tm, LANE), jnp.float32)
    for hot in hots:
        chosen = chosen + hot.astype(jnp.float32)
    row = lax.broadcasted_iota(jnp.int32, (tm, tm), 0)
    col = lax.broadcasted_iota(jnp.int32, (tm, tm), 1)
    tri = (row > col).astype(jnp.bfloat16)
    before = jnp.dot(tri, chosen.astype(jnp.bfloat16), preferred_element_type=jnp.float32) + run_ref[0:1, :]
    idx_out = jnp.zeros((tm, LANE), jnp.int32)
    gate_out = jnp.zeros((tm, LANE), jnp.float32)
    rank_out = jnp.zeros((tm, LANE), jnp.int32)
    for k in range(TOP_K):
        rank_k = jnp.sum(jnp.where(hots[k], before, 0.0), axis=-1, keepdims=True).astype(jnp.int32)
        idx_out = jnp.where(lane == k, sels[k], idx_out)
        gate_out = jnp.where(lane == k, exps[k] / denom, gate_out)
        rank_out = jnp.where(lane == k, rank_k, rank_out)
    idx_ref[...] = idx_out
    gate_ref[...] = gate_out
    rank_ref[...] = rank_out
    run_new = run_ref[0:1, :] + jnp.sum(chosen, axis=0, keepdims=True)
    run_ref[...] = jnp.broadcast_to(run_new, run_ref.shape)
    cnt_ref[...] = jnp.broadcast_to(run_new, cnt_ref.shape)


def _moe_route(x2d, g, shift, scale, router_w, router_b, cnt0):
    n_tok, d = x2d.shape
    f32 = jnp.float32
    rw = jnp.pad(router_w, ((0, 0), (0, LANE - N_EXPERTS)))
    rb = jnp.pad(router_b.astype(f32), (0, LANE - N_EXPERTS), constant_values=NEG_BIG).reshape(1, LANE)
    tile = pl.BlockSpec((MOE_TM, LANE), lambda i: (i, 0))
    wide = pl.BlockSpec((MOE_TM, d), lambda i: (i, 0))
    vec = pl.BlockSpec((1, d), lambda i: (0, 0))
    small = pl.BlockSpec((SUBLANE, LANE), lambda i: (0, 0))
    row = lambda v: v.astype(f32).reshape(1, d)
    h, idx, gate, rank, cnt = pl.pallas_call(
        _route_body,
        grid=(n_tok // MOE_TM,),
        in_specs=[wide, vec, vec, vec, pl.BlockSpec((d, LANE), lambda i: (0, 0)),
                  pl.BlockSpec((1, LANE), lambda i: (0, 0)), small],
        out_specs=[wide, tile, tile, tile, small],
        out_shape=[jax.ShapeDtypeStruct((n_tok, d), f32),
                   jax.ShapeDtypeStruct((n_tok, LANE), jnp.int32),
                   jax.ShapeDtypeStruct((n_tok, LANE), f32),
                   jax.ShapeDtypeStruct((n_tok, LANE), jnp.int32),
                   jax.ShapeDtypeStruct((SUBLANE, LANE), f32)],
        scratch_shapes=[pltpu.VMEM((SUBLANE, LANE), f32)],
        compiler_params=pltpu.CompilerParams(dimension_semantics=("arbitrary",)),
        name="moe_route",
    )(x2d, row(g), row(shift), row(scale), rw, rb, cnt0)
    return h, idx[:, :TOP_K], gate, rank[:, :TOP_K], cnt


def _rows_to_tiles(x):
    return _transpose8([x[:, LANE * s:LANE * (s + 1)] for s in range(SUBLANE)])


def _store_tiles(ref, tiles):
    groups = ref.shape[0] // SUBLANE
    for j in range(SUBLANE):
        ref[pl.ds(j, groups, stride=SUBLANE)] = tiles[j].reshape(groups, SUBLANE, LANE)


def _load_rows(ref):
    groups = ref.shape[0] // SUBLANE
    tiles = [ref[pl.ds(j, groups, stride=SUBLANE)].reshape(groups * SUBLANE, LANE) for j in range(SUBLANE)]
    return jnp.concatenate(_transpose8(tiles), axis=1)


def _scatter_body(off_ref, pad_ref, dest_ref, *rest, tile_starts):
    t_refs = rest[:-4]
    xs_ref, rows_ref, zero_ref, sem = rest[-4:]
    i = pl.program_id(0)
    tm = t_refs[0].shape[0]

    @pl.when(i == 0)
    def _():
        zero_ref[...] = jnp.zeros_like(zero_ref)
        used = off_ref[N_EXPERTS - 1] + pad_ref[N_EXPERTS - 1]
        n_rows = xs_ref.shape[0]

        def zero_block(start):
            return pltpu.make_async_copy(zero_ref, xs_ref.at[pl.ds(start, MOE_BM)], sem)

        for e in range(N_EXPERTS):
            tail = n_rows - (e + 1) * MOE_BM

            @pl.when(pad_ref[e] > 0)
            def _():
                zero_block(off_ref[e] + pad_ref[e] - MOE_BM).start()

            @pl.when(tail >= used)
            def _():
                zero_block(tail).start()
        for e in range(N_EXPERTS):
            tail = n_rows - (e + 1) * MOE_BM

            @pl.when(pad_ref[e] > 0)
            def _():
                zero_block(0).wait()

            @pl.when(tail >= used)
            def _():
                zero_block(0).wait()

    for s, t_ref in enumerate(t_refs):
        @pl.when((i >= tile_starts[s]) & (i < tile_starts[s + 1]))
        def _():
            for r0 in range(0, tm, MOE_TILE_ROWS):
                _store_tiles(rows_ref.at[pl.ds(r0, MOE_TILE_ROWS)],
                             _rows_to_tiles(t_ref[pl.ds(r0, MOE_TILE_ROWS), :]))

    def issue(t, carry):
        for k in range(TOP_K):
            pltpu.make_async_copy(rows_ref.at[t], xs_ref.at[dest_ref[TOP_K * t + k]], sem).start(priority=k % 2)
        return carry

    lax.fori_loop(0, tm, issue, 0, unroll=MOE_ISSUE_UNROLL)
    for _ in range(TOP_K):
        pltpu.make_async_copy(rows_ref, xs_ref.at[pl.ds(0, tm)], sem).wait()


def _stream_tiles(streams):
    starts = [0]
    for t in streams:
        starts.append(starts[-1] + t.shape[0] // MOE_TM)
    return tuple(starts)


def _stream_spec(width, starts, s, extra=0):
    lo, n = starts[s], starts[s + 1] - starts[s]
    return pl.BlockSpec((MOE_TM, width), lambda i, *_: (jnp.clip(i + extra - lo, 0, n - 1), 0))


def _moe_scatter(hs, dest_flat, off, padded, n_rows):
    d = hs[0].shape[1]
    assert d == SUBLANE * LANE
    starts = _stream_tiles(hs)
    in_specs = [pl.BlockSpec((TOP_K * MOE_TM,), lambda i, off, pad: (i,), memory_space=pltpu.SMEM)]
    in_specs += [_stream_spec(d, starts, s) for s in range(len(hs))]
    return pl.pallas_call(
        functools.partial(_scatter_body, tile_starts=starts),
        grid_spec=pltpu.PrefetchScalarGridSpec(
            num_scalar_prefetch=2,
            grid=(starts[-1],),
            in_specs=in_specs,
            out_specs=pl.BlockSpec(memory_space=pl.ANY),
            scratch_shapes=[pltpu.VMEM((MOE_TM, SUBLANE, LANE), jnp.float32),
                            pltpu.VMEM((MOE_BM, SUBLANE, LANE), jnp.float32), pltpu.SemaphoreType.DMA],
        ),
        out_shape=jax.ShapeDtypeStruct((n_rows, SUBLANE, LANE), jnp.float32),
        compiler_params=pltpu.CompilerParams(dimension_semantics=("arbitrary",)),
        name="moe_scatter",
    )(off, padded, dest_flat, *hs)


def _expert_body(blk_e_ref, n_act_ref, nxt_e_ref, x_ref, wgu_hbm, bgu_ref, wdn_hbm, bdn_ref, y_ref,
                 wgu_f32, wdn_f32, wgu_bf, wdn_bf, sems, slot_ref, *, layer):
    i = pl.program_id(0)

    def weight_copies(e, slot):
        return (pltpu.make_async_copy(wgu_hbm.at[layer, e], wgu_f32.at[slot], sems.at[slot, 0]),
                pltpu.make_async_copy(wdn_hbm.at[layer, e], wdn_f32.at[slot], sems.at[slot, 1]))

    @pl.when(i < n_act_ref[0])
    def _():
        e = blk_e_ref[i]
        prev = blk_e_ref[jnp.maximum(i - 1, 0)]

        @pl.when((i == 0) | (e != prev))
        def _():
            @pl.when(i == 0)
            def _():
                slot_ref[0] = 0
                for c in weight_copies(e, 0):
                    c.start()

            slot = slot_ref[0]
            for c in weight_copies(e, slot):
                c.wait()
            wgu_bf[...] = wgu_f32[slot].astype(jnp.bfloat16)
            wdn_bf[...] = wdn_f32[slot].astype(jnp.bfloat16)
            nxt = nxt_e_ref[e]

            @pl.when(nxt >= 0)
            def _():
                for c in weight_copies(nxt, 1 - slot):
                    c.start()

            slot_ref[0] = 1 - slot

        x = _load_rows(x_ref).astype(jnp.bfloat16)
        gu = jnp.dot(x, wgu_bf[...], preferred_element_type=jnp.float32) + bgu_ref[...]
        gate = jnp.minimum(gu[:, :D_EXPERT], SWIGLU_LIMIT)
        up = jnp.clip(gu[:, D_EXPERT:], -SWIGLU_LIMIT, SWIGLU_LIMIT)
        act = (up + 1.0) * gate * jax.nn.sigmoid(SWIGLU_ALPHA * gate)
        y = jnp.dot(act.astype(jnp.bfloat16), wdn_bf[...], preferred_element_type=jnp.float32) + bdn_ref[...]
        _store_tiles(y_ref, _rows_to_tiles(y))

    @pl.when(i >= n_act_ref[0])
    def _():
        y_ref[...] = jnp.zeros_like(y_ref)


def _moe_experts(xs, blk_e, n_act, nxt_e, w_gu, b_gu, w_dn, b_dn, layer):
    n_rows = xs.shape[0]
    d = SUBLANE * LANE
    n_blk = n_rows // MOE_BM

    def blk(i, be, na, nx):
        return jnp.minimum(i, na[0] - 1)

    return pl.pallas_call(
        functools.partial(_expert_body, layer=layer),
        grid_spec=pltpu.PrefetchScalarGridSpec(
            num_scalar_prefetch=3,
            grid=(n_blk,),
            in_specs=[pl.BlockSpec((MOE_BM, SUBLANE, LANE), lambda i, be, na, nx: (blk(i, be, na, nx), 0, 0)),
                      pl.BlockSpec(memory_space=pl.ANY),
                      pl.BlockSpec((None, 1, 2 * D_EXPERT), lambda i, be, na, nx: (be[blk(i, be, na, nx)], 0, 0)),
                      pl.BlockSpec(memory_space=pl.ANY),
                      pl.BlockSpec((None, 1, d), lambda i, be, na, nx: (be[blk(i, be, na, nx)], 0, 0))],
            out_specs=pl.BlockSpec((MOE_BM, SUBLANE, LANE), lambda i, be, na, nx: (i, 0, 0)),
            scratch_shapes=[pltpu.VMEM((2, d, 2 * D_EXPERT), jnp.float32),
                            pltpu.VMEM((2, D_EXPERT, d), jnp.float32),
                            pltpu.VMEM((d, 2 * D_EXPERT), jnp.bfloat16),
                            pltpu.VMEM((D_EXPERT, d), jnp.bfloat16),
                            pltpu.SemaphoreType.DMA((2, 2)),
                            pltpu.SMEM((1,), jnp.int32)],
        ),
        out_shape=jax.ShapeDtypeStruct((n_rows, SUBLANE, LANE), jnp.float32),
        compiler_params=pltpu.CompilerParams(dimension_semantics=("arbitrary",),
                                             vmem_limit_bytes=MOE_VMEM_LIMIT),
        name="moe_experts",
    )(blk_e, n_act, nxt_e, xs, w_gu, b_gu.reshape(N_EXPERTS, 1, -1), w_dn, b_dn.reshape(N_EXPERTS, 1, -1))


def _combine_body(dest_ref, dest_nxt_ref, *rest, tile_starts):
    n_streams = len(tile_starts) - 1
    ins, ys_ref = rest[:3 * n_streams], rest[3 * n_streams]
    y_refs = rest[3 * n_streams + 1:-2]
    if len(y_refs) == 1:
        y_refs = y_refs * n_streams
    buf_ref, sems = rest[-2:]
    i = pl.program_id(0)
    n = pl.num_programs(0)
    tm = y_refs[0].shape[0]

    def fetch(d_ref, slot):
        def issue(t, carry):
            for k in range(TOP_K):
                pltpu.make_async_copy(ys_ref.at[d_ref[TOP_K * t + k]], buf_ref.at[slot, k, t],
                                      sems.at[slot]).start(priority=k % 2)
            return carry
        lax.fori_loop(0, tm, issue, 0, unroll=MOE_ISSUE_UNROLL)

    @pl.when(i == 0)
    def _():
        fetch(dest_ref, 0)

    @pl.when(i + 1 < n)
    def _():
        fetch(dest_nxt_ref, (i + 1) % 2)

    slot = i % 2
    for k in range(TOP_K):
        pltpu.make_async_copy(ys_ref.at[pl.ds(0, tm)], buf_ref.at[slot, k], sems.at[slot]).wait()
    for s in range(n_streams):
        gate_ref, x_ref, gt_ref = ins[3 * s:3 * s + 3]

        @pl.when((i >= tile_starts[s]) & (i < tile_starts[s + 1]))
        def _():
            g = gate_ref[...]
            groups = tm // SUBLANE
            gk = [jnp.broadcast_to(g[:, k:k + 1], (tm, LANE)).reshape(groups, SUBLANE, LANE) for k in range(TOP_K)]
            accs = []
            for j in range(SUBLANE):
                acc = None
                for k in range(TOP_K):
                    splat = jnp.broadcast_to(gk[k][:, j:j + 1, :], (groups, SUBLANE, LANE))
                    term = splat * buf_ref.at[slot, k][pl.ds(j, groups, stride=SUBLANE)]
                    acc = term if acc is None else acc + term
                accs.append(acc.reshape(tm, LANE))
            y_refs[s][...] = x_ref[...] + gt_ref[...] * jnp.concatenate(_transpose8(accs), axis=1)


def _moe_combine(ys, dest_flat, gates, xs2d, gts, joint):
    d = xs2d[0].shape[1]
    starts = _stream_tiles(xs2d)
    n_tiles = starts[-1]
    in_specs = [pl.BlockSpec((TOP_K * MOE_TM,), lambda i: (i,), memory_space=pltpu.SMEM),
                pl.BlockSpec((TOP_K * MOE_TM,), lambda i: (jnp.minimum(i + 1, n_tiles - 1),),
                             memory_space=pltpu.SMEM)]
    args = [dest_flat, dest_flat]
    for s, (gate, x2d, gt) in enumerate(zip(gates, xs2d, gts)):
        in_specs += [_stream_spec(LANE, starts, s), _stream_spec(d, starts, s), pl.BlockSpec((1, d), lambda i: (0, 0))]
        args += [gate, x2d, gt.astype(jnp.float32).reshape(1, d)]
    in_specs.append(pl.BlockSpec(memory_space=pl.ANY))
    args.append(ys)
    if joint:
        out_specs = [pl.BlockSpec((MOE_TM, d), lambda i: (i, 0))]
        out_shape = [jax.ShapeDtypeStruct((n_tiles * MOE_TM, d), jnp.float32)]
    else:
        out_specs = [_stream_spec(d, starts, s) for s in range(len(xs2d))]
        out_shape = [jax.ShapeDtypeStruct(x2d.shape, jnp.float32) for x2d in xs2d]
    return pl.pallas_call(
        functools.partial(_combine_body, tile_starts=starts),
        grid=(n_tiles,),
        in_specs=in_specs,
        out_specs=out_specs,
        out_shape=out_shape,
        scratch_shapes=[pltpu.VMEM((2, TOP_K, MOE_TM, SUBLANE, LANE), jnp.float32),
                        pltpu.SemaphoreType.DMA((2,))],
        compiler_params=pltpu.CompilerParams(dimension_semantics=("arbitrary",)),
        name="moe_combine",
    )(*args)


def moe_layer(streams, g, router_w, router_b, w_gu, b_gu, w_dn, b_dn, layer, joint=False):
    i32 = jnp.int32
    routed = []
    cnt = jnp.zeros((SUBLANE, LANE), jnp.float32)
    for x2d, shift, scale, _ in streams:
        assert x2d.shape[0] % MOE_TM == 0
        h, idx, gate, rank, cnt = _moe_route(x2d, g, shift, scale, router_w, router_b, cnt)
        routed.append((h, idx, gate, rank))
    counts = cnt[0, :N_EXPERTS].astype(i32)
    padded = (counts + MOE_BM - 1) // MOE_BM * MOE_BM
    pad_end = jnp.cumsum(padded)
    off = (pad_end - padded).astype(i32)
    padded = padded.astype(i32)
    n_tok = sum(s[0].shape[0] for s in streams)
    n_blk = -(-(n_tok * TOP_K) // MOE_BM) + N_EXPERTS
    blk_e = jnp.minimum(jnp.sum(jnp.arange(n_blk)[:, None] * MOE_BM >= pad_end[None, :], axis=1),
                        N_EXPERTS - 1).astype(i32)
    n_act = (pad_end[-1:] // MOE_BM).astype(i32)
    experts = jnp.arange(N_EXPERTS, dtype=i32)
    later_with_rows = (padded > 0)[None, :] & (experts[None, :] > experts[:, None])
    nxt_e = jnp.min(jnp.where(later_with_rows, experts[None, :], N_EXPERTS), axis=1)
    nxt_e = jnp.where(nxt_e == N_EXPERTS, -1, nxt_e).astype(i32)
    dests = []
    for _, idx, _, rank in routed:
        dest = rank + jnp.sum(jnp.where(idx[..., None] == experts, off, 0), axis=-1)
        dests.append(dest.reshape(-1).astype(i32))
    dest_flat = jnp.concatenate(dests)
    xs = _moe_scatter([r[0] for r in routed], dest_flat, off, padded, n_blk * MOE_BM)
    ys = _moe_experts(xs, blk_e, n_act, nxt_e, w_gu, b_gu, w_dn, b_dn, layer)
    return _moe_combine(ys, dest_flat, [r[2] for r in routed], [s[0] for s in streams], [s[3] for s in streams],
                        joint)


def kernel(x, c, ctx, c_ctx, norm_g, ada_w, ada_b, w_out, w_in_even, hy_conv_w, hy_conv_b,
           hy_w1, hy_b1, hy_w2, hy_b2, hy_w3, hy_b3, hy_w4, hy_freq, hy_filter_bias,
           att_q_norm, att_k_norm, att_sink, w_in_odd, ssd_conv_w, ssd_conv_b, ssd_dt_bias,
           ssd_A_log, ssd_D, ssd_norm, hg_lower_bounds, hg_norm, router_w, router_b,
           moe_w_gu, moe_b_gu, moe_w_dn, moe_b_dn):
    lbs = jax.nn.softmax(hg_lower_bounds.astype(jnp.float32), axis=0)
    lbs = jnp.cumsum(lbs, axis=0) - lbs[0]
    xc = ctx
    n_ctx = ctx.shape[1]
    for layer in range(DEPTH):
        ctx_out = layer < DEPTH - 1
        i = layer // 2
        sh, sc, gt = adaln(c, ada_w[layer], ada_b[layer], 0)
        sh_c, sc_c, gt_c = adaln(c_ctx, ada_w[layer], ada_b[layer], 0)
        assert (layer % 2 == 0) == ctx_out
        moe = functools.partial(moe_layer, g=norm_g[layer, 1], router_w=router_w[layer], router_b=router_b[layer],
                                w_gu=moe_w_gu, b_gu=moe_b_gu[layer], w_dn=moe_w_dn, b_dn=moe_b_dn[layer], layer=layer)
        sh2, sc2, gt2 = adaln(c, ada_w[layer], ada_b[layer], 1)
        if layer % 2 == 0:
            x, xc = even_layer(x, xc, norm_g[layer, 0], (sh, sc), (sh_c, sc_c), w_in_even[i], hy_conv_w[i],
                               hy_conv_b[i], hy_w1[i], hy_b1[i], hy_w2[i], hy_b2[i], hy_w3[i], hy_b3[i], hy_w4[i],
                               hy_freq[i], hy_filter_bias[i], att_q_norm[i], att_k_norm[i], att_sink[i], gt, gt_c,
                               w_out[layer])
            sh2_c, sc2_c, gt2_c = adaln(c_ctx, ada_w[layer], ada_b[layer], 1)
            x_all = moe([(xc[0], sh2_c, sc2_c, gt2_c), (x[0], sh2, sc2, gt2)], joint=True)[0]
        else:
            x = odd_layer(x_all, n_ctx, norm_g[layer, 0], (sh, sc), (sh_c, sc_c), lbs[layer], w_in_odd[i],
                          ssd_conv_w[i], ssd_conv_b[i], ssd_dt_bias[i], ssd_A_log[i], ssd_D[i], ssd_norm[i],
                          hg_norm[i], gt, w_out[layer])
            x = moe([(x, sh2, sc2, gt2)])[0][None]
    return x
```

```python
import functools
import math

import jax
import jax.numpy as jnp
from jax import lax
from jax.experimental import pallas as pl
from jax.experimental.pallas import tpu as pltpu

D_MODEL = 1024
DEPTH = 2
GRID_W = 64
MIX_W = D_MODEL
EPS = 1e-6
CONV_W = 3

HY_W = MIX_W // 2
HY_ORDER = 2
HY_EMB = 33
HY_FFN = 64
HY_TARGET = 1e-2
HY_SHORT_PCT = 0.3
HY_LONG_PCT = 1.5

HEAD_DIM = 64
ATT_HEADS = (MIX_W // 2) // HEAD_DIM
ATT_KV_HEADS = 2
ATT_GROUP = ATT_HEADS // ATT_KV_HEADS
ATT_WINDOW = 128
ATT_BLOCK = 128
ROPE_BASE = 10000.0
ATT_Q_W = ATT_HEADS * HEAD_DIM
ATT_KV_W = ATT_KV_HEADS * HEAD_DIM

SSD_W = MIX_W // 2
SSD_HEAD_DIM = 64
SSD_HEADS = SSD_W // SSD_HEAD_DIM
SSD_GROUPS = 2
SSD_STATE = 128
SSD_CHUNK = 128
SSD_CONV_CH = SSD_W + 2 * SSD_GROUPS * SSD_STATE

HG_W = MIX_W // 2
HG_EXPAND = 128
HG_HEADS = HG_W // HG_EXPAND
HG_VDIM = HG_W // HG_HEADS
HG_CHUNK = 64

N_EXPERTS = 32
TOP_K = 4
D_EXPERT = D_MODEL
SWIGLU_ALPHA = 1.702
SWIGLU_LIMIT = 7.0
MOE_BLOCK = 128

EVEN_IN = 2 * ATT_KV_W + ATT_Q_W + 3 * HY_W
ODD_STATE_COLS = SSD_CONV_CH + 2 * SSD_HEADS + 3 * HG_W
ODD_IN = ODD_STATE_COLS + SSD_W + 2 * HG_W

LANE = 128
SUBLANE = 8


PROJ_TM = 256
PROJ_VMEM_LIMIT = 56 * 1024 * 1024


def _norm_proj_body(x_ref, g_ref, sh_ref, sc_ref, w_ref, o_ref):
    x = x_ref[...]
    y = x * lax.rsqrt(jnp.mean(x * x, axis=-1, keepdims=True) + EPS) * g_ref[...]
    h = y * (1.0 + sc_ref[...]) + sh_ref[...]
    o_ref[...] = jnp.dot(h.astype(jnp.bfloat16), w_ref[...], preferred_element_type=jnp.float32)


def norm_proj(x2d, g, shift, scale, w_bf16, n_first=0):
    rows, d = x2d.shape
    n = w_bf16.shape[1]
    assert rows % PROJ_TM == 0 and n_first % PROJ_TM == 0 and n % LANE == 0
    first_tiles = n_first // PROJ_TM
    mod = pl.BlockSpec((None, 1, d), lambda i: (jnp.where(i < first_tiles, 0, 1), 0, 0))
    return pl.pallas_call(
        _norm_proj_body,
        grid=(rows // PROJ_TM,),
        in_specs=[pl.BlockSpec((PROJ_TM, d), lambda i: (i, 0)),
                  pl.BlockSpec((1, d), lambda i: (0, 0)), mod, mod,
                  pl.BlockSpec((d, n), lambda i: (0, 0))],
        out_specs=pl.BlockSpec((PROJ_TM, n), lambda i: (i, 0)),
        out_shape=jax.ShapeDtypeStruct((rows, n), jnp.float32),
        compiler_params=pltpu.CompilerParams(dimension_semantics=("arbitrary",),
                                             vmem_limit_bytes=PROJ_VMEM_LIMIT),
        name="norm_proj",
    )(x2d, g.astype(jnp.float32).reshape(1, d), shift.astype(jnp.float32).reshape(2, 1, d),
      scale.astype(jnp.float32).reshape(2, 1, d), w_bf16)


def adaln(cond, w, b, j):
    lo, hi = 3 * j * D_MODEL, 3 * (j + 1) * D_MODEL
    m = jax.nn.silu(cond) @ w[:, lo:hi] + b[lo:hi]
    return jnp.split(m, 3, axis=-1)


def axial_rope_tables(length):
    rows = length // GRID_W
    n_pairs = HEAD_DIM // 4
    inv = ROPE_BASE ** (-jnp.arange(n_pairs, dtype=jnp.float32) / n_pairs)
    row_ang = jnp.arange(rows, dtype=jnp.float32)[:, None] * inv
    col_ang = jnp.arange(GRID_W, dtype=jnp.float32)[:, None] * inv
    ang_r = jnp.broadcast_to(row_ang[:, None], (rows, GRID_W, n_pairs)).reshape(length, n_pairs)
    ang_c = jnp.broadcast_to(col_ang[None], (rows, GRID_W, n_pairs)).reshape(length, n_pairs)
    return jnp.cos(ang_r), jnp.sin(ang_r), jnp.cos(ang_c), jnp.sin(ang_c)


HY_N1 = 128
HY_N2 = 256
HY_NB = 8
HY_TM = 512


def _hy_dft_constants(length):
    import numpy as np
    n = 2 * length
    assert n == HY_N1 * HY_N2
    half = HY_N1 // 2
    k1 = np.arange(HY_N1)[:, None]
    n1 = np.arange(half)[None, :]
    n2 = np.arange(HY_N2)[:, None, None]
    ang = 2 * np.pi * (k1 * n1 / HY_N1)[None] + 2 * np.pi * n2 * k1[None] / n
    fwd = np.stack([np.cos(ang), -np.sin(ang)], axis=2).reshape(HY_N2, 2 * HY_N1, half)
    inv = fwd.transpose(0, 2, 1)
    a2 = 2 * np.pi * np.outer(np.arange(HY_N2), np.arange(HY_N2)) / HY_N2
    c, s = np.cos(a2), -np.sin(a2)
    m_fwd = np.concatenate([np.stack([c, -s], axis=2).reshape(HY_N2, 2 * HY_N2),
                            np.stack([s, c], axis=2).reshape(HY_N2, 2 * HY_N2)], axis=0)
    m_inv = np.stack([np.concatenate([c, s], axis=1), np.concatenate([-s, c], axis=1)],
                     axis=1).reshape(2 * HY_N2, 2 * HY_N2)
    f = np.float32
    return fwd.astype(f), inv.astype(f), m_fwd.astype(f), m_inv.astype(f)


def _hy_conv_body(p_ref, prev_ref, next_ref, w_ref, b_ref, o_ref):
    i = pl.program_id(1)
    n = pl.num_programs(1)
    u = p_ref[...]
    rows = u.shape[0]
    row = lax.broadcasted_iota(jnp.int32, u.shape, 0)
    before = jnp.where(i == 0, 0.0, prev_ref[HALO - 1:HALO, :])
    after = jnp.where(i == n - 1, 0.0, next_ref[0:1, :])
    up = jnp.where(row == 0, before, pltpu.roll(u, 1, 0))
    un = jnp.where(row == rows - 1, after, pltpu.roll(u, rows - 1, 0))
    o_ref[...] = w_ref[0:1, :] * up + w_ref[1:2, :] * u + w_ref[2:3, :] * un + b_ref[...]


def _hy_conv(p, conv_w, conv_b, tm):
    length = p.shape[0]
    per = tm // HALO
    last = length // HALO - 1
    return pl.pallas_call(
        _hy_conv_body,
        grid=(3, length // tm),
        in_specs=[pl.BlockSpec((tm, HY_W), lambda c, i: (i, c)),
                  pl.BlockSpec((HALO, HY_W), lambda c, i: (jnp.maximum(i * per - 1, 0), c)),
                  pl.BlockSpec((HALO, HY_W), lambda c, i: (jnp.minimum((i + 1) * per, last), c)),
                  pl.BlockSpec((CONV_W, HY_W), lambda c, i: (0, c)),
                  pl.BlockSpec((1, HY_W), lambda c, i: (0, c))],
        out_specs=pl.BlockSpec((None, tm, HY_W), lambda c, i: (c, i, 0)),
        out_shape=jax.ShapeDtypeStruct((3, length, HY_W), jnp.float32),
        name="hy_conv",
    )(p, p, p, conv_w, conv_b.reshape(1, -1))


def _hy_filter_body(z_ref, w1_ref, b1_ref, w2_ref, b2_ref, w3_ref, b3_ref, w4_ref, fq_ref, dl_ref,
                    e_ref, o_ref, nrm_ref):
    i = pl.program_id(0)
    bf = jnp.bfloat16
    z = z_ref[...]
    fq = fq_ref[...]

    def layer(a, w_ref, b_ref):
        return jnp.sin(fq * (jnp.dot(a.astype(bf), w_ref[...].astype(bf), preferred_element_type=jnp.float32)
                             + b_ref[...]))

    half = z.shape[0] // 2
    hdn = layer(layer(layer(jnp.concatenate([z[:half], z[half:]], axis=1), w1_ref, b1_ref), w2_ref, b2_ref),
                w3_ref, b3_ref).astype(bf)
    w4 = w4_ref[...].astype(bf)
    h = jnp.concatenate([jnp.dot(hdn[:, :HY_FFN], w4, preferred_element_type=jnp.float32),
                         jnp.dot(hdn[:, HY_FFN:], w4, preferred_element_type=jnp.float32)], axis=0)
    decay = jnp.exp(-z[:, 0:1] * dl_ref[...])
    first = (lax.broadcasted_iota(jnp.int32, decay.shape, 0) == 0) & (i == 0)
    acc = []
    for order in range(HY_ORDER):
        lo = order * 2 * HY_W
        h0 = h[:, lo:lo + HY_W] * decay
        h1 = jnp.where(first, 0.0, h[:, lo + HY_W:lo + 2 * HY_W] * decay)
        e_ref[:, order * HY_W:(order + 1) * HY_W] = h0 + h1
        o_ref[:, order * HY_W:(order + 1) * HY_W] = h0 - h1
        acc.append(jnp.sum(jnp.abs(h0) + jnp.abs(h1), axis=0, keepdims=True))
    part = jnp.concatenate(acc, axis=1)

    @pl.when(i == 0)
    def _():
        nrm_ref[...] = jnp.zeros_like(nrm_ref)

    nrm_ref[...] = nrm_ref[...] + part


def _hy_filter_time(length, w1, b1, w2, b2, w3, b3, w4, freq, tm):
    f32 = jnp.float32
    t = jnp.linspace(0.0, 1.0, length, dtype=f32)[:, None]
    bands = (HY_EMB - 1) // 2
    w_ang = 2.0 * math.pi * jnp.arange(length, dtype=f32)[:, None] / length
    fr = jnp.linspace(1e-4, bands - 1, bands, dtype=f32)[None]
    z = jnp.concatenate([t, jnp.cos(fr * w_ang), -jnp.sin(fr * w_ang)], axis=-1)
    z = jnp.pad(z, ((0, 0), (0, LANE - HY_EMB)))
    w1p = jnp.pad(w1.astype(f32), ((0, LANE - HY_EMB), (0, 0)))
    max_decay = math.log(HY_TARGET) / HY_SHORT_PCT
    min_decay = math.log(HY_TARGET) / HY_LONG_PCT
    deltas = jnp.abs(jnp.linspace(min_decay, max_decay, HY_W, dtype=f32)).reshape(1, HY_W)
    full = lambda a: pl.BlockSpec(a.shape, lambda i: (0,) * a.ndim)
    row2 = lambda v: jnp.tile(v.astype(f32).reshape(1, -1), (1, 2))
    diag2 = lambda w: jnp.kron(jnp.eye(2, dtype=f32), w.astype(f32))
    args = (z, diag2(w1p), row2(b1), diag2(w2), row2(b2), diag2(w3), row2(b3), w4.astype(f32), row2(freq), deltas)
    ow = HY_ORDER * HY_W
    return pl.pallas_call(
        _hy_filter_body,
        grid=(length // tm,),
        in_specs=[pl.BlockSpec((tm, LANE), lambda i: (i, 0))] + [full(a) for a in args[1:]],
        out_specs=[pl.BlockSpec((tm, ow), lambda i: (i, 0)), pl.BlockSpec((tm, ow), lambda i: (i, 0)),
                   pl.BlockSpec((1, ow), lambda i: (0, 0))],
        out_shape=[jax.ShapeDtypeStruct((length, ow), f32), jax.ShapeDtypeStruct((length, ow), f32),
                   jax.ShapeDtypeStruct((1, ow), f32)],
        compiler_params=pltpu.CompilerParams(dimension_semantics=("arbitrary",)),
        name="hy_filter_time",
    )(*args)


def _words(x_bf16):
    return pltpu.bitcast(x_bf16, jnp.uint32)


def _halves(w_u32):
    return pltpu.bitcast(w_u32, jnp.bfloat16)


def _transpose8(parts):
    rows, cols = parts[0].shape
    parts = [p.reshape(rows // SUBLANE, SUBLANE, cols) for p in parts]
    row = lax.broadcasted_iota(jnp.int32, parts[0].shape, 1)
    for s in (1, 2, 4):
        keep = (row & s) == 0
        nxt = list(parts)
        for i in range(SUBLANE):
            if i & s == 0:
                a, b = parts[i], parts[i + s]
                nxt[i] = jnp.where(keep, a, pltpu.roll(b, s, 1))
                nxt[i + s] = jnp.where(keep, pltpu.roll(a, SUBLANE - s, 1), b)
        parts = nxt
    return [p.reshape(rows, cols) for p in parts]


def _gather_tiles(ref, j, n):
    return jnp.concatenate([ref[SUBLANE * g + j] for g in range(n)], axis=0)


def _hy_s1_body(x_ref, f_ref, a_ref):
    groups = x_ref.shape[0] // SUBLANE
    xs = _transpose8([_gather_tiles(x_ref, j, groups) for j in range(SUBLANE)])
    words = []
    for i in range(HY_NB):
        acc = jnp.dot(f_ref[i], xs[i].astype(jnp.bfloat16), preferred_element_type=jnp.float32)
        words.append(_words(acc.astype(jnp.bfloat16)))
    tiles = _transpose8(words)
    per = HY_N1 // SUBLANE
    for j in range(SUBLANE):
        for g in range(per):
            a_ref[per * j + g] = tiles[j][SUBLANE * g:SUBLANE * (g + 1)]


def _hy_s1(x4, sel, fwd):
    assert HY_NB == SUBLANE
    _, half, _, width = x4.shape
    return pl.pallas_call(
        _hy_s1_body,
        grid=(HY_N2 // HY_NB,),
        in_specs=[pl.BlockSpec((None, half, HY_NB, width), lambda j: (sel, 0, j, 0)),
                  pl.BlockSpec((HY_NB, 2 * HY_N1, half), lambda j: (j, 0, 0))],
        out_specs=pl.BlockSpec((HY_N1, HY_NB, width), lambda j: (0, j, 0)),
        out_shape=jax.ShapeDtypeStruct((HY_N1, HY_N2, width), jnp.uint32),
        name="hy_stage1",
    )(x4, fwd)


HY_SLABS = 4


def _hy_s2f_body(ae_ref, ao_ref, m_ref, sc_ref, hr_ref, hi_ref):
    dot = lambda a, b: jnp.dot(a, b, preferred_element_type=jnp.float32)
    for s in range(HY_SLABS):
        hr_ref[s] = (dot(m_ref[:HY_N2, :], _halves(ae_ref[s])) * sc_ref[...]).astype(hr_ref.dtype)
        hi_ref[s] = (dot(m_ref[HY_N2:, :], _halves(ao_ref[s])) * sc_ref[...]).astype(hi_ref.dtype)


def _hy_s2f(a_e, a_o, m_fwd, scale):
    width = scale.shape[1]
    slab = pl.BlockSpec((HY_SLABS, HY_N2, width), lambda k: (k, 0, 0))
    out = jax.ShapeDtypeStruct((HY_N1, HY_N2, width), jnp.bfloat16)
    return pl.pallas_call(
        _hy_s2f_body,
        grid=(HY_N1 // HY_SLABS,),
        in_specs=[slab, slab, pl.BlockSpec((2 * HY_N2, 2 * HY_N2), lambda k: (0, 0)),
                  pl.BlockSpec((1, width), lambda k: (0, 0))],
        out_specs=[slab, slab],
        out_shape=[out, out],
        name="hy_filter_stage2",
    )(a_e, a_o, m_fwd, scale)


def _hy_s2_body(a_ref, hr_ref, hi_ref, mf_ref, mi_ref, b_ref):
    bf = jnp.bfloat16
    for s in range(HY_SLABS):
        x = jnp.dot(mf_ref[...], _halves(a_ref[s]), preferred_element_type=jnp.float32)
        xr, xi = x[:HY_N2], x[HY_N2:]
        hr = hr_ref[s].astype(jnp.float32)
        hi = hi_ref[s].astype(jnp.float32)
        y = jnp.concatenate([(xr * hr - xi * hi).astype(bf), (xr * hi + xi * hr).astype(bf)], axis=0)
        b = jnp.dot(mi_ref[...], y, preferred_element_type=jnp.float32)
        b_ref[s] = _words(b.astype(bf))


def _hy_s2(a, h_re, h_im, m_fwd, m_inv, order):
    slab = pl.BlockSpec((HY_SLABS, HY_N2, HY_W), lambda k: (k, 0, 0))
    hslab = pl.BlockSpec((HY_SLABS, HY_N2, HY_W), lambda k: (k, 0, order))
    mat = pl.BlockSpec((2 * HY_N2, 2 * HY_N2), lambda k: (0, 0))
    return pl.pallas_call(
        _hy_s2_body,
        grid=(HY_N1 // HY_SLABS,),
        in_specs=[slab, hslab, hslab, mat, mat],
        out_specs=slab,
        out_shape=jax.ShapeDtypeStruct((HY_N1, HY_N2, HY_W), jnp.uint32),
        name="hy_stage2",
    )(a, h_re, h_im, m_fwd, m_inv)


def _hy_is1_body(b_ref, g_ref, z_ref, gate_ref, bias_ref, o_ref):
    per = HY_N1 // SUBLANE
    spectra = _transpose8([jnp.concatenate([b_ref[per * j + g] for g in range(per)], axis=0)
                           for j in range(SUBLANE)])
    convs = [jnp.dot(g_ref[i], _halves(spectra[i]), preferred_element_type=jnp.float32) for i in range(HY_NB)]
    tiles = _transpose8(convs)
    for j in range(SUBLANE):
        for g in range(o_ref.shape[0] // SUBLANE):
            n1 = SUBLANE * g + j
            o_ref[n1] = gate_ref[n1] * (tiles[j][SUBLANE * g:SUBLANE * (g + 1)] + z_ref[n1] * bias_ref[...])


def _hy_is1(b, inv, z4, z_sel, gate4, gate_sel, bias):
    _, half, _, width = z4.shape
    real = lambda sel: pl.BlockSpec((None, half, HY_NB, width), lambda j: (sel, 0, j, 0))
    return pl.pallas_call(
        _hy_is1_body,
        grid=(HY_N2 // HY_NB,),
        in_specs=[pl.BlockSpec((HY_N1, HY_NB, width), lambda j: (0, j, 0)),
                  pl.BlockSpec((HY_NB, half, 2 * HY_N1), lambda j: (j, 0, 0)),
                  real(z_sel), real(gate_sel), pl.BlockSpec((1, width), lambda j: (0, 0))],
        out_specs=pl.BlockSpec((half, HY_NB, width), lambda j: (0, j, 0)),
        out_shape=jax.ShapeDtypeStruct((half, HY_N2, width), jnp.float32),
        name="hy_inv_stage1",
    )(b, inv, z4, gate4, bias)


def hyena_long(p_hy, f_w1, f_b1, f_w2, f_b2, f_w3, f_b3, f_w4, f_freq, f_bias, conv_w, conv_b):
    length = p_hy.shape[0]
    bf = jnp.bfloat16
    half = HY_N1 // 2
    fwd, inv, m_fwd, m_inv = (jnp.asarray(m, bf) for m in _hy_dft_constants(length))
    e, od, nrm = _hy_filter_time(length, f_w1, f_b1, f_w2, f_b2, f_w3, f_b3, f_w4, f_freq, HY_TM)
    ow = HY_ORDER * HY_W
    a_e = _hy_s1(e.reshape(1, half, HY_N2, ow), 0, fwd)
    a_o = _hy_s1(od.reshape(1, half, HY_N2, ow), 0, fwd)
    scale = 1.0 / (nrm * (2.0 * length))
    h_re, h_im = _hy_s2f(a_e, a_o, m_fwd, scale)
    u = _hy_conv(p_hy, conv_w, conv_b, HY_TM).reshape(3, half, HY_N2, HY_W)
    z, z_sel = u, 0
    for order in range(HY_ORDER):
        b = _hy_s2(_hy_s1(z, z_sel, fwd), h_re, h_im, m_fwd, m_inv, order)
        z = _hy_is1(b, inv, z, z_sel, u, 1 + order, f_bias[order].astype(jnp.float32).reshape(1, HY_W))[None]
        z_sel = 0
    return z.reshape(length, HY_W)


ATT_TQ = ATT_BLOCK
EVEN_TM = 512
EV_Q_BLK = 3 * HY_W // ATT_Q_W
EV_K_BLK = (3 * HY_W + ATT_Q_W) // ATT_KV_W
EV_V_BLK = EV_K_BLK + 1


def _rope_tables(length):
    cr, sr, cc, sc = axial_rope_tables(length)
    return jnp.concatenate([cr, cr, cc, cc], axis=-1), jnp.concatenate([-sr, sr, -sc, sc], axis=-1)


def _head_norm_rope(x, g_row, c, s, seg):
    sq = x * x
    hi = sq.astype(jnp.bfloat16)
    lo = (sq - hi.astype(jnp.float32)).astype(jnp.bfloat16)
    ms = (jnp.dot(hi, seg, preferred_element_type=jnp.float32) + jnp.dot(lo, seg, preferred_element_type=jnp.float32))
    y = x * lax.rsqrt(ms + EPS) * g_row
    width = x.shape[1]
    quarter = HEAD_DIM // 4
    lane = lax.broadcasted_iota(jnp.int32, x.shape, 1)
    partner = jnp.where((lane & quarter) == 0, pltpu.roll(y, width - quarter, 1), pltpu.roll(y, quarter, 1))
    return y * c + partner * s


def _qk_prep_body(q_ref, k_ref, c_ref, s_ref, qn_ref, kn_ref, segq_ref, segk_ref, qo_ref, ko_ref):
    c, s = c_ref[...], s_ref[...]
    tile = lambda t, n: jnp.concatenate([t] * n, axis=1)
    qo_ref[...] = _head_norm_rope(q_ref[...], tile(qn_ref[...], ATT_HEADS), tile(c, ATT_HEADS), tile(s, ATT_HEADS),
                                  segq_ref[...]).astype(qo_ref.dtype)
    ko_ref[...] = _head_norm_rope(k_ref[...], tile(kn_ref[...], ATT_KV_HEADS), tile(c, ATT_KV_HEADS),
                                  tile(s, ATT_KV_HEADS), segk_ref[...]).astype(ko_ref.dtype)


def _qk_prep(p, rope_c, rope_s, q_norm, k_norm, tm):
    import numpy as np
    rows = p.shape[0]
    seg = lambda w: jnp.asarray(np.kron(np.eye(w // HEAD_DIM), np.full((HEAD_DIM, HEAD_DIM), 1.0 / HEAD_DIM)),
                                jnp.bfloat16)
    const = lambda shape: pl.BlockSpec(shape, lambda i: (0, 0))
    return pl.pallas_call(
        _qk_prep_body,
        grid=(rows // tm,),
        in_specs=[pl.BlockSpec((tm, ATT_Q_W), lambda i: (i, EV_Q_BLK)),
                  pl.BlockSpec((tm, ATT_KV_W), lambda i: (i, EV_K_BLK)),
                  pl.BlockSpec((tm, HEAD_DIM), lambda i: (i, 0)), pl.BlockSpec((tm, HEAD_DIM), lambda i: (i, 0)),
                  const((1, HEAD_DIM)), const((1, HEAD_DIM)),
                  const((ATT_Q_W, ATT_Q_W)), const((ATT_KV_W, ATT_KV_W))],
        out_specs=[pl.BlockSpec((tm, ATT_Q_W), lambda i: (i, 0)), pl.BlockSpec((tm, ATT_KV_W), lambda i: (i, 0))],
        out_shape=[jax.ShapeDtypeStruct((rows, ATT_Q_W), jnp.bfloat16),
                   jax.ShapeDtypeStruct((rows, ATT_KV_W), jnp.bfloat16)],
        name="qk_prep",
    )(p, p, rope_c, rope_s, q_norm.astype(jnp.float32).reshape(1, -1), k_norm.astype(jnp.float32).reshape(1, -1),
      seg(ATT_Q_W), seg(ATT_KV_W))


def _att_body(sink_ref, q_ref, kc_ref, vc_ref, *rest, local):
    if local:
        kp_ref, k0_ref, kn_ref, vp_ref, v0_ref, vn_ref, o_ref = rest
    else:
        (o_ref,) = rest
    b = pl.program_id(0)
    nb = pl.num_programs(0)
    bf = jnp.bfloat16
    scale = HEAD_DIM ** -0.5
    q = q_ref[...]
    kc = kc_ref[...]
    vc = vc_ref[...].astype(bf)
    if local:
        kb = jnp.concatenate([kp_ref[...], k0_ref[...], kn_ref[...]], axis=0)
        vb = jnp.concatenate([vp_ref[...], v0_ref[...], vn_ref[...]], axis=0).astype(bf)
        i = lax.broadcasted_iota(jnp.int32, (ATT_TQ, 3 * ATT_BLOCK), 0)
        j = lax.broadcasted_iota(jnp.int32, (ATT_TQ, 3 * ATT_BLOCK), 1)
        rel = j - ATT_BLOCK - i
        valid = ((jnp.abs(rel) <= ATT_WINDOW) & ((b > 0) | (j >= ATT_BLOCK))
                 & ((b < nb - 1) | (j < 2 * ATT_BLOCK)))
    rows = ATT_GROUP * ATT_TQ
    head_of_row = lax.broadcasted_iota(jnp.int32, (rows, 1), 0) // ATT_TQ
    for hk in range(ATT_KV_HEADS):
        heads = range(hk * ATT_GROUP, (hk + 1) * ATT_GROUP)
        qs = jnp.concatenate([q[:, h * HEAD_DIM:(h + 1) * HEAD_DIM] for h in heads], axis=0)
        kv_cols = slice(hk * HEAD_DIM, (hk + 1) * HEAD_DIM)
        sink = jnp.zeros((rows, 1), jnp.float32)
        for g, h in enumerate(heads):
            sink = jnp.where(head_of_row == g, sink_ref[h], sink)
        s_ctx = _dot_nt(qs, kc[:, kv_cols]) * scale
        m = jnp.maximum(jnp.max(s_ctx, axis=-1, keepdims=True), sink)
        if local:
            valid4 = jnp.concatenate([valid] * ATT_GROUP, axis=0)
            s_loc = jnp.where(valid4, _dot_nt(qs, kb[:, kv_cols]) * scale, -jnp.inf)
            m = jnp.maximum(m, jnp.max(s_loc, axis=-1, keepdims=True))
        p_ctx = jnp.exp(s_ctx - m)
        den = jnp.sum(p_ctx, axis=-1, keepdims=True) + jnp.exp(sink - m)
        acc = jnp.dot(p_ctx.astype(bf), vc[:, kv_cols], preferred_element_type=jnp.float32)
        if local:
            p_loc = jnp.exp(s_loc - m)
            den = den + jnp.sum(p_loc, axis=-1, keepdims=True)
            acc = acc + jnp.dot(p_loc.astype(bf), vb[:, kv_cols], preferred_element_type=jnp.float32)
        out = acc / den
        for g, h in enumerate(heads):
            o_ref[:, h * HEAD_DIM:(h + 1) * HEAD_DIM] = out[g * ATT_TQ:(g + 1) * ATT_TQ]


def _attention(qn, kn, p, kcn, pc, sink, local):
    rows = qn.shape[0]
    nb = rows // ATT_TQ
    n_ctx = kcn.shape[0]
    specs = [pl.BlockSpec((ATT_TQ, ATT_Q_W), lambda b, s: (b, 0)),
             pl.BlockSpec((n_ctx, ATT_KV_W), lambda b, s: (0, 0)),
             pl.BlockSpec((n_ctx, ATT_KV_W), lambda b, s: (0, EV_V_BLK))]
    args = [qn, kcn, pc]
    if local:
        prev = lambda b, s: jnp.maximum(b - 1, 0)
        nxt = lambda b, s: jnp.minimum(b + 1, nb - 1)
        for col, arr in ((0, kn), (EV_V_BLK, p)):
            specs += [pl.BlockSpec((ATT_BLOCK, ATT_KV_W), lambda b, s, col=col: (prev(b, s), col)),
                      pl.BlockSpec((ATT_BLOCK, ATT_KV_W), lambda b, s, col=col: (b, col)),
                      pl.BlockSpec((ATT_BLOCK, ATT_KV_W), lambda b, s, col=col: (nxt(b, s), col))]
            args += [arr, arr, arr]
    return pl.pallas_call(
        functools.partial(_att_body, local=local),
        grid_spec=pltpu.PrefetchScalarGridSpec(
            num_scalar_prefetch=1, grid=(nb,), in_specs=specs,
            out_specs=pl.BlockSpec((ATT_TQ, ATT_Q_W), lambda b, s: (b, 0))),
        out_shape=jax.ShapeDtypeStruct((rows, ATT_Q_W), jnp.float32),
        name="window_attention" if local else "context_attention",
    )(sink.astype(jnp.float32), *args)


def _even_merge_body(hy_ref, att_ref, x_ref, gt_ref, w_ref, o_ref):
    bf = jnp.bfloat16
    m = (jnp.dot(hy_ref[...].astype(bf), w_ref[:HY_W, :], preferred_element_type=jnp.float32)
         + jnp.dot(att_ref[...].astype(bf), w_ref[HY_W:, :], preferred_element_type=jnp.float32))
    o_ref[...] = x_ref[...] + gt_ref[...] * m


def _even_merge(hy, att, x2d, gt, w_out, tm):
    rows, d = x2d.shape
    half = pl.BlockSpec((tm, HY_W), lambda i: (i, 0))
    return pl.pallas_call(
        _even_merge_body,
        grid=(rows // tm,),
        in_specs=[half, half, pl.BlockSpec((tm, d), lambda i: (i, 0)), pl.BlockSpec((1, d), lambda i: (0, 0)),
                  pl.BlockSpec((MIX_W, d), lambda i: (0, 0))],
        out_specs=pl.BlockSpec((tm, d), lambda i: (i, 0)),
        out_shape=jax.ShapeDtypeStruct((rows, d), jnp.float32),
        name="even_merge",
    )(hy, att, x2d, gt.reshape(1, -1), w_out.astype(jnp.bfloat16))


def _hy_short_body(e_ref, o_ref, sc_ref, u_ref, bias_ref, cf_ref, sf_ref, out_ref):
    bf = jnp.bfloat16
    cf, sf = cf_ref[...], sf_ref[...]
    dot = lambda a, b: jnp.dot(a, b.astype(bf), preferred_element_type=jnp.float32)
    z = u_ref[0]
    for order in range(HY_ORDER):
        cols = slice(order * HY_W, (order + 1) * HY_W)
        h_re = dot(cf, e_ref[:, cols]) * sc_ref[:, cols]
        h_im = dot(sf, o_ref[:, cols]) * sc_ref[:, cols]
        x_re, x_im = dot(cf, z), dot(sf, z)
        y_re = (x_re * h_re - x_im * h_im).astype(bf)
        y_im = (x_re * h_im + x_im * h_re).astype(bf)
        conv = _dot_tn(cf, y_re) + _dot_tn(sf, y_im)
        z = u_ref[1 + order] * (conv + z * bias_ref[order:order + 1, :])
    out_ref[...] = z


def hyena_short(pc, f_w1, f_b1, f_w2, f_b2, f_w3, f_b3, f_w4, f_freq, f_bias, conv_w, conv_b):
    import numpy as np
    rows = pc.shape[0]
    e, od, nrm = _hy_filter_time(rows, f_w1, f_b1, f_w2, f_b2, f_w3, f_b3, f_w4, f_freq, rows)
    u = _hy_conv(pc, conv_w, conv_b, rows)
    ang = 2 * np.pi * np.outer(np.arange(2 * rows), np.arange(rows)) / (2 * rows)
    cf, sf = jnp.asarray(np.cos(ang), jnp.bfloat16), jnp.asarray(-np.sin(ang), jnp.bfloat16)
    scale = 1.0 / (nrm * (2.0 * rows))
    return pl.pallas_call(
        _hy_short_body,
        out_shape=jax.ShapeDtypeStruct((rows, HY_W), jnp.float32),
        name="hy_short",
    )(e, od, scale, u, f_bias.astype(jnp.float32), cf, sf)


def even_layer(x, xc, g, mod, mod_c, w_in, conv_w, conv_b, f_w1, f_b1, f_w2, f_b2, f_w3, f_b3, f_w4, f_freq, f_bias,
               q_norm, k_norm, sink, gt, gt_c, w_out):
    f32 = jnp.float32
    length, n_ctx = x.shape[1], xc.shape[1]
    filt = (f_w1, f_b1, f_w2, f_b2, f_w3, f_b3, f_w4, f_freq, f_bias, conv_w, conv_b)
    o_q = 2 * ATT_KV_W
    o_hy = o_q + ATT_Q_W
    w_perm = jnp.concatenate([w_in[:, o_hy:], w_in[:, o_q:o_hy], w_in[:, :o_q]], axis=1).astype(jnp.bfloat16)
    twice = lambda v: jnp.stack([v.reshape(-1), v.reshape(-1)])
    p = norm_proj(x[0], g, twice(mod[0]), twice(mod[1]), w_perm)
    pc = norm_proj(xc[0], g, twice(mod_c[0]), twice(mod_c[1]), w_perm)
    rope_c, rope_s = _rope_tables(length)
    qn, kn = _qk_prep(p, rope_c, rope_s, q_norm, k_norm, EVEN_TM)
    qcn, kcn = _qk_prep(pc, jnp.ones((n_ctx, HEAD_DIM), f32), jnp.zeros((n_ctx, HEAD_DIM), f32), q_norm, k_norm,
                        n_ctx)
    att = _attention(qn, kn, p, kcn, pc, sink, True)
    att_c = _attention(qcn, None, None, kcn, pc, sink, False)
    hy = hyena_long(p, *filt)
    hy_c = hyena_short(pc, *filt)
    x_new = _even_merge(hy, att, x[0], gt, w_out, EVEN_TM)
    xc_new = _even_merge(hy_c, att_c, xc[0], gt_c, w_out, n_ctx)
    return x_new[None], xc_new[None]


SCAN_Q = 128
SCAN_LEVELS = 7
ODD_COLS = SSD_CONV_CH + 8 * 512
HALO = SUBLANE


def _scan_constants():
    import numpy as np
    q = SCAN_Q
    d_hg, d_ssd, pairs, laters = [], [], [], []
    for direction in (0, 1):
        pos = np.arange(q) if direction == 0 else q - 1 - np.arange(q)
        pj, pt = pos[None, :], pos[:, None]
        top = pj <= pt
        end = pj > pt
        ones = np.ones((SUBLANE, q), bool)
        lv, pr, lt = [], [], []
        for level in range(SCAN_LEVELS):
            b = 2 ** level
            start = (pos // (2 * b)) * (2 * b)
            mid = (start + b)[:, None]
            later = pos >= start + b
            lv.append(np.where(later[:, None], (pj >= mid) & (pj <= pt), (pj > pt) & (pj < mid)))
            pr.append((start[:, None] == start[None, :]) & later[:, None] & ~later[None, :])
            lt.append(np.broadcast_to(later[:, None], (q, LANE)))
        pr.append(np.eye(q, dtype=bool))
        pr.append(top)
        d_hg.append(np.concatenate([top, end] + lv + [ones], axis=0))
        d_ssd.append(np.concatenate([top, end, ones], axis=0))
        pairs.append(np.stack(pr))
        laters.append(np.stack(lt))
    f = np.float32
    twice = lambda m: np.concatenate([m, m], axis=-1)
    return (twice(np.stack(d_hg)).astype(f), twice(np.stack(d_ssd)).astype(f), np.stack(pairs).astype(f),
            np.stack(laters).astype(f))


def _split_dot(mm_bf16, v):
    hi = v.astype(jnp.bfloat16)
    lo = (v - hi.astype(jnp.float32)).astype(jnp.bfloat16)
    return jnp.dot(mm_bf16, jnp.concatenate([hi, lo], axis=0), preferred_element_type=jnp.float32)


def _dot_nt(a, b):
    return lax.dot_general(a, b, (((1,), (1,)), ((), ())), preferred_element_type=jnp.float32)


def _dot_tn(a, b):
    return lax.dot_general(a, b, (((0,), (0,)), ((), ())), preferred_element_type=jnp.float32)


def _softplus(x):
    return jnp.maximum(x, 0.0) + jnp.log1p(jnp.exp(-jnp.abs(x)))


def _scan_body(xbc_ref, prev_ref, next_ref, f_ref, iv_ref, q_ref, dt_ref,
               cw_ref, cb_ref, dtb_ref, a_ref, dsk_ref, lb_ref,
               dhg_ref, dssd_ref, pair_ref, later_ref, sel_ref,
               out_ref, s_ssd, s_hg, ydiag_ref, *, n_ctx_chunks, n_chunks):
    d = pl.program_id(0)
    j = pl.program_id(1)
    q_rows = SCAN_Q
    bf = jnp.bfloat16

    @pl.when(j == 0)
    def _():
        s_ssd[...] = jnp.zeros_like(s_ssd)
        s_hg[...] = jnp.zeros_like(s_hg)

    c = jnp.where(d == 0, j, jnp.where(j < n_ctx_chunks, n_ctx_chunks - 1 - j, n_chunks - 1 + n_ctx_chunks - j))
    first = (c == 0) | (c == n_ctx_chunks)
    last = (c == n_ctx_chunks - 1) | (c == n_chunks - 1)

    u = xbc_ref[...]
    row = lax.broadcasted_iota(jnp.int32, u.shape, 0)
    before = jnp.where(first, 0.0, prev_ref[HALO - 1:HALO, :])
    after = jnp.where(last, 0.0, next_ref[0:1, :])
    up = jnp.where(row == 0, before, pltpu.roll(u, 1, 0))
    un = jnp.where(row == q_rows - 1, after, pltpu.roll(u, q_rows - 1, 0))
    xbc = cw_ref[0:1, :] * up + cw_ref[1:2, :] * u + cw_ref[2:3, :] * un + cb_ref[...]
    xbc = xbc * jax.nn.sigmoid(xbc)
    x = xbc[:, :SSD_W]

    dt = _softplus(dt_ref[...] + dtb_ref[...])
    da = dt * a_ref[...]
    r = _split_dot(dssd_ref[...], da)
    cs, to_end, total = r[:q_rows], r[q_rows:2 * q_rows], r[2 * q_rows:2 * q_rows + 1]
    xdt = x * dt
    cs_hi = cs.astype(bf)
    cs_lo = (cs - cs_hi.astype(jnp.float32)).astype(bf)
    cs_rows = _dot_nt(sel_ref[...], cs_hi) + _dot_nt(sel_ref[...], cs_lo)
    l_mask = pair_ref[SCAN_LEVELS + 1]
    decay_in = jnp.exp2(cs)
    w_end = (jnp.exp2(to_end) * xdt).astype(bf)
    gn = SSD_GROUPS * SSD_STATE
    hpg = SSD_HEADS // SSD_GROUPS
    gw = hpg * SSD_HEAD_DIM
    for g in range(SSD_GROUPS):
        b_g = xbc[:, SSD_W + g * SSD_STATE:SSD_W + (g + 1) * SSD_STATE].astype(bf)
        c_g = xbc[:, SSD_W + gn + g * SSD_STATE:SSD_W + gn + (g + 1) * SSD_STATE].astype(bf)
        scores = _dot_nt(c_g, b_g)
        y_off = jnp.dot(c_g, s_ssd[g].astype(bf), preferred_element_type=jnp.float32)
        for hh in range(hpg):
            h = g * hpg + hh
            lo = h * SSD_HEAD_DIM
            diff = cs[:, lo:lo + 1] - cs_rows[h:h + 1, :]
            decay = jnp.exp2(jnp.minimum(diff, 0.0)) * l_mask
            ydiag_ref[:, lo:lo + SSD_HEAD_DIM] = jnp.dot((scores * decay).astype(bf),
                                                         xdt[:, lo:lo + SSD_HEAD_DIM].astype(bf),
                                                         preferred_element_type=jnp.float32)
        cols = slice(g * gw, (g + 1) * gw)
        out_ref[:, cols] = (ydiag_ref[:, cols] + decay_in[:, cols] * y_off + dsk_ref[:, cols] * x[:, cols])
        s_ssd[g] = jnp.exp2(total[:, cols]) * s_ssd[g] + _dot_tn(b_g, w_end[:, cols])

    lb = lb_ref[...]
    f = lb + (1.0 - lb) * jax.nn.sigmoid(f_ref[...])
    k_in = 1.0 - f
    qv = q_ref[...]
    qv = qv * jax.nn.sigmoid(qv)
    v_bf = iv_ref[...].astype(bf)
    e = jnp.exp2(_split_dot(dhg_ref[...], jnp.log2(f)))
    e_top, e_end = e[:q_rows], e[q_rows:2 * q_rows]
    e_tot = e[(2 + SCAN_LEVELS) * q_rows:(2 + SCAN_LEVELS) * q_rows + 1]
    for h in range(HG_HEADS):
        cols = slice(h * HG_EXPAND, (h + 1) * HG_EXPAND)
        q_h, k_h = qv[:, cols], k_in[:, cols]
        att = pair_ref[SCAN_LEVELS] * _dot_nt(q_h.astype(bf), k_h.astype(bf))
        for level in range(SCAN_LEVELS):
            e_l = e[(2 + level) * q_rows:(3 + level) * q_rows, cols]
            w_l = (jnp.where(later_ref[level] > 0.0, q_h, k_h) * e_l).astype(bf)
            att = att + pair_ref[level] * _dot_nt(w_l, w_l)
        o = jnp.dot(att.astype(bf), v_bf[:, cols], preferred_element_type=jnp.float32)
        o = o + _dot_nt((q_h * e_top[:, cols]).astype(bf), s_hg[h].astype(bf))
        out_ref[:, SSD_W + h * HG_VDIM:SSD_W + (h + 1) * HG_VDIM] = o
        s_hg[h] = e_tot[:, cols] * s_hg[h] + _dot_tn(v_bf[:, cols], (k_h * e_end[:, cols]).astype(bf))


def _odd_scan(p_all, conv_w, conv_b, dtb, a_cols, dsk, lb, n_ctx):
    n_rows = p_all.shape[0]
    n_chunks = n_rows // SCAN_Q
    ncc = n_ctx // SCAN_Q
    d_hg, d_ssd, pairs, laters = _scan_constants()
    bf = jnp.bfloat16
    import numpy as np
    sel = np.zeros((LANE, SSD_W), np.float32)
    sel[np.arange(SSD_HEADS), np.arange(SSD_HEADS) * SSD_HEAD_DIM] = 1.0

    def chunk(d, j):
        return jnp.where(d == 0, j, jnp.where(j < ncc, ncc - 1 - j, n_chunks - 1 + ncc - j))

    per = SCAN_Q // HALO
    last_halo = n_rows // HALO - 1
    col512 = lambda blk: (lambda d, j: (chunk(d, j), blk))
    const2 = lambda shape: pl.BlockSpec(shape, lambda d, j: (0,) * len(shape))
    dirc = lambda shape: pl.BlockSpec((None,) + shape, lambda d, j: (d,) + (0,) * len(shape))
    body = functools.partial(_scan_body, n_ctx_chunks=ncc, n_chunks=n_chunks)
    return pl.pallas_call(
        body,
        grid=(2, n_chunks),
        in_specs=[
            pl.BlockSpec((SCAN_Q, SSD_CONV_CH), lambda d, j: (chunk(d, j), 0)),
            pl.BlockSpec((HALO, SSD_CONV_CH), lambda d, j: (jnp.maximum(chunk(d, j) * per - 1, 0), 0)),
            pl.BlockSpec((HALO, SSD_CONV_CH), lambda d, j: (jnp.minimum((chunk(d, j) + 1) * per, last_halo), 0)),
            pl.BlockSpec((SCAN_Q, 512), lambda d, j: (chunk(d, j), 2 + d)),
            pl.BlockSpec((SCAN_Q, 512), col512(4)),
            pl.BlockSpec((SCAN_Q, 512), col512(6)),
            pl.BlockSpec((SCAN_Q, 512), lambda d, j: (chunk(d, j), 8 + d)),
            const2((CONV_W, SSD_CONV_CH)), const2((1, SSD_CONV_CH)),
            dirc((1, SSD_W)), dirc((1, SSD_W)), dirc((1, SSD_W)), const2((1, HG_W)),
            dirc(d_hg.shape[1:]), dirc(d_ssd.shape[1:]), dirc(pairs.shape[1:]), dirc(laters.shape[1:]),
            const2((LANE, SSD_W)),
        ],
        out_specs=pl.BlockSpec((None, SCAN_Q, MIX_W), lambda d, j: (d, chunk(d, j), 0)),
        out_shape=jax.ShapeDtypeStruct((2, n_rows, MIX_W), jnp.float32),
        scratch_shapes=[pltpu.VMEM((SSD_GROUPS, SSD_STATE, SSD_W // SSD_GROUPS), jnp.float32),
                        pltpu.VMEM((HG_HEADS, HG_VDIM, HG_EXPAND), jnp.float32),
                        pltpu.VMEM((SCAN_Q, SSD_W), jnp.float32)],
        compiler_params=pltpu.CompilerParams(dimension_semantics=("arbitrary", "arbitrary"),
                                             vmem_limit_bytes=MOE_VMEM_LIMIT),
        name="odd_scan",
    )(p_all, p_all, p_all, p_all, p_all, p_all, p_all,
      conv_w, conv_b.reshape(1, -1), dtb, a_cols, dsk, lb.reshape(1, -1),
      jnp.asarray(d_hg, bf), jnp.asarray(d_ssd, bf), jnp.asarray(pairs), jnp.asarray(laters),
      jnp.asarray(sel, bf))


def _group_rms(v, width):
    parts = []
    for lo in range(0, v.shape[1], width):
        seg = v[:, lo:lo + width]
        parts.append(seg * lax.rsqrt(jnp.mean(seg * seg, axis=-1, keepdims=True) + EPS))
    return jnp.concatenate(parts, axis=1)


def _odd_merge_body(yo_ref, z_ref, g_ref, x_ref, sn_ref, hn_ref, gt_ref, w_ref, o_ref):
    yo = yo_ref[0] + yo_ref[1]
    z = z_ref[...]
    g = g_ref[...]
    ys = _group_rms(yo[:, :SSD_W] * (z * jax.nn.sigmoid(z)), SSD_W // SSD_GROUPS) * sn_ref[...]
    hs = _group_rms(yo[:, SSD_W:], HG_VDIM) * hn_ref[...] * (g * jax.nn.sigmoid(g))
    m = jnp.concatenate([ys, hs], axis=1).astype(jnp.bfloat16)
    o_ref[...] = x_ref[...] + gt_ref[...] * jnp.dot(m, w_ref[...], preferred_element_type=jnp.float32)


ODD_TM = 256


def _odd_merge(yo, p_all, x_all, ssd_norm, hg_norm, gt, w_out, n_ctx):
    n_all, d = x_all.shape
    n_lat = n_all - n_ctx
    skip = n_ctx // ODD_TM
    return pl.pallas_call(
        _odd_merge_body,
        grid=(n_lat // ODD_TM,),
        in_specs=[pl.BlockSpec((2, ODD_TM, MIX_W), lambda i: (0, i + skip, 0)),
                  pl.BlockSpec((ODD_TM, 512), lambda i: (i + skip, 5)),
                  pl.BlockSpec((ODD_TM, 512), lambda i: (i + skip, 7)),
                  pl.BlockSpec((ODD_TM, d), lambda i: (i + skip, 0)),
                  pl.BlockSpec((1, SSD_W), lambda i: (0, 0)),
                  pl.BlockSpec((1, HG_W), lambda i: (0, 0)),
                  pl.BlockSpec((1, d), lambda i: (0, 0)),
                  pl.BlockSpec((MIX_W, d), lambda i: (0, 0))],
        out_specs=pl.BlockSpec((ODD_TM, d), lambda i: (i, 0)),
        out_shape=jax.ShapeDtypeStruct((n_lat, d), jnp.float32),
        compiler_params=pltpu.CompilerParams(dimension_semantics=("arbitrary",)),
        name="odd_merge",
    )(yo, p_all, p_all, x_all, ssd_norm.reshape(1, -1), hg_norm.reshape(1, -1), gt.reshape(1, -1),
      w_out.astype(jnp.bfloat16))


def odd_layer(x_all, n_ctx, g, mod, mod_c, lb, w_in, conv_w, conv_b, dt_bias, a_log, d_skip, ssd_norm, hg_norm, gt,
              w_out):
    f32 = jnp.float32
    o_dt = SSD_CONV_CH
    o_f = o_dt + 2 * SSD_HEADS
    rep = lambda v: jnp.repeat(v, SSD_HEAD_DIM, axis=-1)
    w_perm = jnp.concatenate([w_in[:, :o_dt], w_in[:, o_f:], rep(w_in[:, o_dt:o_dt + SSD_HEADS]),
                              rep(w_in[:, o_dt + SSD_HEADS:o_f])], axis=1).astype(jnp.bfloat16)
    both = lambda a, b: jnp.stack([a.reshape(-1), b.reshape(-1)])
    p_all = norm_proj(x_all, g, both(mod_c[0], mod[0]), both(mod_c[1], mod[1]), w_perm, n_first=n_ctx)
    dtb = rep(dt_bias.astype(f32)).reshape(2, 1, SSD_W)
    a_cols = rep(-jnp.exp(a_log.astype(f32)) * math.log2(math.e)).reshape(2, 1, SSD_W)
    dsk = rep(d_skip.astype(f32)).reshape(2, 1, SSD_W)
    yo = _odd_scan(p_all, conv_w, conv_b, dtb, a_cols, dsk, lb.astype(f32), n_ctx)
    return _odd_merge(yo, p_all, x_all, ssd_norm, hg_norm, gt, w_out, n_ctx)


MOE_TM = 256
MOE_BM = 256
NEG_BIG = -1e30
MOE_ISSUE_UNROLL = 16
MOE_TILE_ROWS = 64
MOE_VMEM_LIMIT = 52 * 1024 * 1024


def _route_body(x_ref, g_ref, sh_ref, sc_ref, rw_ref, rb_ref, cnt0_ref,
                h_ref, idx_ref, gate_ref, rank_ref, cnt_ref, run_ref):
    i = pl.program_id(0)

    @pl.when(i == 0)
    def _():
        run_ref[...] = cnt0_ref[...]

    tm = x_ref.shape[0]
    x = x_ref[...]
    t = (x * lax.rsqrt(jnp.mean(x * x, axis=-1, keepdims=True) + EPS) * g_ref[...]) * (1.0 + sc_ref[...]) + sh_ref[...]
    h_ref[...] = t
    logits = jnp.dot(t.astype(jnp.bfloat16), rw_ref[...].astype(jnp.bfloat16),
                     preferred_element_type=jnp.float32) + rb_ref[...]
    lane = lax.broadcasted_iota(jnp.int32, (tm, LANE), 1)
    lane_f = lane.astype(jnp.float32)
    work = logits
    vals, sels, hots = [], [], []
    for _ in range(TOP_K):
        m = jnp.max(work, axis=-1, keepdims=True)
        sel = jnp.min(jnp.where(work == m, lane_f, float(LANE)), axis=-1, keepdims=True)
        hot = lane_f == sel
        vals.append(m)
        sels.append(sel.astype(jnp.int32))
        hots.append(hot)
        work = jnp.where(hot, -jnp.inf, work)
    exps = [jnp.exp(v - vals[0]) for v in vals]
    denom = exps[0] + exps[1] + exps[2] + exps[3]
    chosen = jnp.zeros((tm, LANE), jnp.float32)
    for hot in hots:
        chosen = chosen + hot.astype(jnp.float32)
    row = lax.broadcasted_iota(jnp.int32, (tm, tm), 0)
    col = lax.broadcasted_iota(jnp.int32, (tm, tm), 1)
    tri = (row > col).astype(jnp.bfloat16)
    before = jnp.dot(tri, chosen.astype(jnp.bfloat16), preferred_element_type=jnp.float32) + run_ref[0:1, :]
    idx_out = jnp.zeros((tm, LANE), jnp.int32)
    gate_out = jnp.zeros((tm, LANE), jnp.float32)
    rank_out = jnp.zeros((tm, LANE), jnp.int32)
    for k in range(TOP_K):
        rank_k = jnp.sum(jnp.where(hots[k], before, 0.0), axis=-1, keepdims=True).astype(jnp.int32)
        idx_out = jnp.where(lane == k, sels[k], idx_out)
        gate_out = jnp.where(lane == k, exps[k] / denom, gate_out)
        rank_out = jnp.where(lane == k, rank_k, rank_out)
    idx_ref[...] = idx_out
    gate_ref[...] = gate_out
    rank_ref[...] = rank_out
    run_new = run_ref[0:1, :] + jnp.sum(chosen, axis=0, keepdims=True)
    run_ref[...] = jnp.broadcast_to(run_new, run_ref.shape)
    cnt_ref[...] = jnp.broadcast_to(run_new, cnt_ref.shape)


def _moe_route(x2d, g, shift, scale, router_w, router_b, cnt0):
    n_tok, d = x2d.shape
    f32 = jnp.float32
    rw = jnp.pad(router_w, ((0, 0), (0, LANE - N_EXPERTS)))
    rb = jnp.pad(router_b.astype(f32), (0, LANE - N_EXPERTS), constant_values=NEG_BIG).reshape(1, LANE)
    tile = pl.BlockSpec((MOE_TM, LANE), lambda i: (i, 0))
    wide = pl.BlockSpec((MOE_TM, d), lambda i: (i, 0))
    vec = pl.BlockSpec((1, d), lambda i: (0, 0))
    small = pl.BlockSpec((SUBLANE, LANE), lambda i: (0, 0))
    row = lambda v: v.astype(f32).reshape(1, d)
    h, idx, gate, rank, cnt = pl.pallas_call(
        _route_body,
        grid=(n_tok // MOE_TM,),
        in_specs=[wide, vec, vec, vec, pl.BlockSpec((d, LANE), lambda i: (0, 0)),
                  pl.BlockSpec((1, LANE), lambda i: (0, 0)), small],
        out_specs=[wide, tile, tile, tile, small],
        out_shape=[jax.ShapeDtypeStruct((n_tok, d), f32),
                   jax.ShapeDtypeStruct((n_tok, LANE), jnp.int32),
                   jax.ShapeDtypeStruct((n_tok, LANE), f32),
                   jax.ShapeDtypeStruct((n_tok, LANE), jnp.int32),
                   jax.ShapeDtypeStruct((SUBLANE, LANE), f32)],
        scratch_shapes=[pltpu.VMEM((SUBLANE, LANE), f32)],
        compiler_params=pltpu.CompilerParams(dimension_semantics=("arbitrary",)),
        name="moe_route",
    )(x2d, row(g), row(shift), row(scale), rw, rb, cnt0)
    return h, idx[:, :TOP_K], gate, rank[:, :TOP_K], cnt


def _rows_to_tiles(x):
    return _transpose8([x[:, LANE * s:LANE * (s + 1)] for s in range(SUBLANE)])


def _store_tiles(ref, tiles):
    groups = ref.shape[0] // SUBLANE
    for j in range(SUBLANE):
        ref[pl.ds(j, groups, stride=SUBLANE)] = tiles[j].reshape(groups, SUBLANE, LANE)


def _load_rows(ref):
    groups = ref.shape[0] // SUBLANE
    tiles = [ref[pl.ds(j, groups, stride=SUBLANE)].reshape(groups * SUBLANE, LANE) for j in range(SUBLANE)]
    return jnp.concatenate(_transpose8(tiles), axis=1)


def _scatter_body(off_ref, pad_ref, dest_ref, *rest, tile_starts):
    t_refs = rest[:-4]
    xs_ref, rows_ref, zero_ref, sem = rest[-4:]
    i = pl.program_id(0)
    tm = t_refs[0].shape[0]

    @pl.when(i == 0)
    def _():
        zero_ref[...] = jnp.zeros_like(zero_ref)
        used = off_ref[N_EXPERTS - 1] + pad_ref[N_EXPERTS - 1]
        n_rows = xs_ref.shape[0]

        def zero_block(start):
            return pltpu.make_async_copy(zero_ref, xs_ref.at[pl.ds(start, MOE_BM)], sem)

        for e in range(N_EXPERTS):
            tail = n_rows - (e + 1) * MOE_BM

            @pl.when(pad_ref[e] > 0)
            def _():
                zero_block(off_ref[e] + pad_ref[e] - MOE_BM).start()

            @pl.when(tail >= used)
            def _():
                zero_block(tail).start()
        for e in range(N_EXPERTS):
            tail = n_rows - (e + 1) * MOE_BM

            @pl.when(pad_ref[e] > 0)
            def _():
                zero_block(0).wait()

            @pl.when(tail >= used)
            def _():
                zero_block(0).wait()

    for s, t_ref in enumerate(t_refs):
        @pl.when((i >= tile_starts[s]) & (i < tile_starts[s + 1]))
        def _():
            for r0 in range(0, tm, MOE_TILE_ROWS):
                _store_tiles(rows_ref.at[pl.ds(r0, MOE_TILE_ROWS)],
                             _rows_to_tiles(t_ref[pl.ds(r0, MOE_TILE_ROWS), :]))

    def issue(t, carry):
        for k in range(TOP_K):
            pltpu.make_async_copy(rows_ref.at[t], xs_ref.at[dest_ref[TOP_K * t + k]], sem).start(priority=k % 2)
        return carry

    lax.fori_loop(0, tm, issue, 0, unroll=MOE_ISSUE_UNROLL)
    for _ in range(TOP_K):
        pltpu.make_async_copy(rows_ref, xs_ref.at[pl.ds(0, tm)], sem).wait()


def _stream_tiles(streams):
    starts = [0]
    for t in streams:
        starts.append(starts[-1] + t.shape[0] // MOE_TM)
    return tuple(starts)


def _stream_spec(width, starts, s, extra=0):
    lo, n = starts[s], starts[s + 1] - starts[s]
    return pl.BlockSpec((MOE_TM, width), lambda i, *_: (jnp.clip(i + extra - lo, 0, n - 1), 0))


def _moe_scatter(hs, dest_flat, off, padded, n_rows):
    d = hs[0].shape[1]
    assert d == SUBLANE * LANE
    starts = _stream_tiles(hs)
    in_specs = [pl.BlockSpec((TOP_K * MOE_TM,), lambda i, off, pad: (i,), memory_space=pltpu.SMEM)]
    in_specs += [_stream_spec(d, starts, s) for s in range(len(hs))]
    return pl.pallas_call(
        functools.partial(_scatter_body, tile_starts=starts),
        grid_spec=pltpu.PrefetchScalarGridSpec(
            num_scalar_prefetch=2,
            grid=(starts[-1],),
            in_specs=in_specs,
            out_specs=pl.BlockSpec(memory_space=pl.ANY),
            scratch_shapes=[pltpu.VMEM((MOE_TM, SUBLANE, LANE), jnp.float32),
                            pltpu.VMEM((MOE_BM, SUBLANE, LANE), jnp.float32), pltpu.SemaphoreType.DMA],
        ),
        out_shape=jax.ShapeDtypeStruct((n_rows, SUBLANE, LANE), jnp.float32),
        compiler_params=pltpu.CompilerParams(dimension_semantics=("arbitrary",)),
        name="moe_scatter",
    )(off, padded, dest_flat, *hs)


def _expert_body(blk_e_ref, n_act_ref, nxt_e_ref, x_ref, wgu_hbm, bgu_ref, wdn_hbm, bdn_ref, y_ref,
                 wgu_f32, wdn_f32, wgu_bf, wdn_bf, sems, slot_ref, *, layer):
    i = pl.program_id(0)

    def weight_copies(e, slot):
        return (pltpu.make_async_copy(wgu_hbm.at[layer, e], wgu_f32.at[slot], sems.at[slot, 0]),
                pltpu.make_async_copy(wdn_hbm.at[layer, e], wdn_f32.at[slot], sems.at[slot, 1]))

    @pl.when(i < n_act_ref[0])
    def _():
        e = blk_e_ref[i]
        prev = blk_e_ref[jnp.maximum(i - 1, 0)]

        @pl.when((i == 0) | (e != prev))
        def _():
            @pl.when(i == 0)
            def _():
                slot_ref[0] = 0
                for c in weight_copies(e, 0):
                    c.start()

            slot = slot_ref[0]
            for c in weight_copies(e, slot):
                c.wait()
            wgu_bf[...] = wgu_f32[slot].astype(jnp.bfloat16)
            wdn_bf[...] = wdn_f32[slot].astype(jnp.bfloat16)
            nxt = nxt_e_ref[e]

            @pl.when(nxt >= 0)
            def _():
                for c in weight_copies(nxt, 1 - slot):
                    c.start()

            slot_ref[0] = 1 - slot

        x = _load_rows(x_ref).astype(jnp.bfloat16)
        gu = jnp.dot(x, wgu_bf[...], preferred_element_type=jnp.float32) + bgu_ref[...]
        gate = jnp.minimum(gu[:, :D_EXPERT], SWIGLU_LIMIT)
        up = jnp.clip(gu[:, D_EXPERT:], -SWIGLU_LIMIT, SWIGLU_LIMIT)
        act = (up + 1.0) * gate * jax.nn.sigmoid(SWIGLU_ALPHA * gate)
        y = jnp.dot(act.astype(jnp.bfloat16), wdn_bf[...], preferred_element_type=jnp.float32) + bdn_ref[...]
        _store_tiles(y_ref, _rows_to_tiles(y))

    @pl.when(i >= n_act_ref[0])
    def _():
        y_ref[...] = jnp.zeros_like(y_ref)


def _moe_experts(xs, blk_e, n_act, nxt_e, w_gu, b_gu, w_dn, b_dn, layer):
    n_rows = xs.shape[0]
    d = SUBLANE * LANE
    n_blk = n_rows // MOE_BM

    def blk(i, be, na, nx):
        return jnp.minimum(i, na[0] - 1)

    return pl.pallas_call(
        functools.partial(_expert_body, layer=layer),
        grid_spec=pltpu.PrefetchScalarGridSpec(
            num_scalar_prefetch=3,
            grid=(n_blk,),
            in_specs=[pl.BlockSpec((MOE_BM, SUBLANE, LANE), lambda i, be, na, nx: (blk(i, be, na, nx), 0, 0)),
                      pl.BlockSpec(memory_space=pl.ANY),
                      pl.BlockSpec((None, 1, 2 * D_EXPERT), lambda i, be, na, nx: (be[blk(i, be, na, nx)], 0, 0)),
                      pl.BlockSpec(memory_space=pl.ANY),
                      pl.BlockSpec((None, 1, d), lambda i, be, na, nx: (be[blk(i, be, na, nx)], 0, 0))],
            out_specs=pl.BlockSpec((MOE_BM, SUBLANE, LANE), lambda i, be, na, nx: (i, 0, 0)),
            scratch_shapes=[pltpu.VMEM((2, d, 2 * D_EXPERT), jnp.float32),
                            pltpu.VMEM((2, D_EXPERT, d), jnp.float32),
                            pltpu.VMEM((d, 2 * D_EXPERT), jnp.bfloat16),
                            pltpu.VMEM((D_EXPERT, d), jnp.bfloat16),
                            pltpu.SemaphoreType.DMA((2, 2)),
                            pltpu.SMEM((1,), jnp.int32)],
        ),
        out_shape=jax.ShapeDtypeStruct((n_rows, SUBLANE, LANE), jnp.float32),
        compiler_params=pltpu.CompilerParams(dimension_semantics=("arbitrary",),
                                             vmem_limit_bytes=MOE_VMEM_LIMIT),
        name="moe_experts",
    )(blk_e, n_act, nxt_e, xs, w_gu, b_gu.reshape(N_EXPERTS, 1, -1), w_dn, b_dn.reshape(N_EXPERTS, 1, -1))


def _combine_body(dest_ref, dest_nxt_ref, *rest, tile_starts):
    n_streams = len(tile_starts) - 1
    ins, ys_ref = rest[:3 * n_streams], rest[3 * n_streams]
    y_refs = rest[3 * n_streams + 1:-2]
    if len(y_refs) == 1:
        y_refs = y_refs * n_streams
    buf_ref, sems = rest[-2:]
    i = pl.program_id(0)
    n = pl.num_programs(0)
    tm = y_refs[0].shape[0]

    def fetch(d_ref, slot):
        def issue(t, carry):
            for k in range(TOP_K):
                pltpu.make_async_copy(ys_ref.at[d_ref[TOP_K * t + k]], buf_ref.at[slot, k, t],
                                      sems.at[slot]).start(priority=k % 2)
            return carry
        lax.fori_loop(0, tm, issue, 0, unroll=MOE_ISSUE_UNROLL)

    @pl.when(i == 0)
    def _():
        fetch(dest_ref, 0)

    @pl.when(i + 1 < n)
    def _():
        fetch(dest_nxt_ref, (i + 1) % 2)

    slot = i % 2
    for k in range(TOP_K):
        pltpu.make_async_copy(ys_ref.at[pl.ds(0, tm)], buf_ref.at[slot, k], sems.at[slot]).wait()
    for s in range(n_streams):
        gate_ref, x_ref, gt_ref = ins[3 * s:3 * s + 3]

        @pl.when((i >= tile_starts[s]) & (i < tile_starts[s + 1]))
        def _():
            g = gate_ref[...]
            groups = tm // SUBLANE
            gk = [jnp.broadcast_to(g[:, k:k + 1], (tm, LANE)).reshape(groups, SUBLANE, LANE) for k in range(TOP_K)]
            accs = []
            for j in range(SUBLANE):
                acc = None
                for k in range(TOP_K):
                    splat = jnp.broadcast_to(gk[k][:, j:j + 1, :], (groups, SUBLANE, LANE))
                    term = splat * buf_ref.at[slot, k][pl.ds(j, groups, stride=SUBLANE)]
                    acc = term if acc is None else acc + term
                accs.append(acc.reshape(tm, LANE))
            y_refs[s][...] = x_ref[...] + gt_ref[...] * jnp.concatenate(_transpose8(accs), axis=1)


def _moe_combine(ys, dest_flat, gates, xs2d, gts, joint):
    d = xs2d[0].shape[1]
    starts = _stream_tiles(xs2d)
    n_tiles = starts[-1]
    in_specs = [pl.BlockSpec((TOP_K * MOE_TM,), lambda i: (i,), memory_space=pltpu.SMEM),
                pl.BlockSpec((TOP_K * MOE_TM,), lambda i: (jnp.minimum(i + 1, n_tiles - 1),),
                             memory_space=pltpu.SMEM)]
    args = [dest_flat, dest_flat]
    for s, (gate, x2d, gt) in enumerate(zip(gates, xs2d, gts)):
        in_specs += [_stream_spec(LANE, starts, s), _stream_spec(d, starts, s), pl.BlockSpec((1, d), lambda i: (0, 0))]
        args += [gate, x2d, gt.astype(jnp.float32).reshape(1, d)]
    in_specs.append(pl.BlockSpec(memory_space=pl.ANY))
    args.append(ys)
    if joint:
        out_specs = [pl.BlockSpec((MOE_TM, d), lambda i: (i, 0))]
        out_shape = [jax.ShapeDtypeStruct((n_tiles * MOE_TM, d), jnp.float32)]
    else:
        out_specs = [_stream_spec(d, starts, s) for s in range(len(xs2d))]
        out_shape = [jax.ShapeDtypeStruct(x2d.shape, jnp.float32) for x2d in xs2d]
    return pl.pallas_call(
        functools.partial(_combine_body, tile_starts=starts),
        grid=(n_tiles,),
        in_specs=in_specs,
        out_specs=out_specs,
        out_shape=out_shape,
        scratch_shapes=[pltpu.VMEM((2, TOP_K, MOE_TM, SUBLANE, LANE), jnp.float32),
                        pltpu.SemaphoreType.DMA((2,))],
        compiler_params=pltpu.CompilerParams(dimension_semantics=("arbitrary",)),
        name="moe_combine",
    )(*args)


def moe_layer(streams, g, router_w, router_b, w_gu, b_gu, w_dn, b_dn, layer, joint=False):
    i32 = jnp.int32
    routed = []
    cnt = jnp.zeros((SUBLANE, LANE), jnp.float32)
    for x2d, shift, scale, _ in streams:
        assert x2d.shape[0] % MOE_TM == 0
        h, idx, gate, rank, cnt = _moe_route(x2d, g, shift, scale, router_w, router_b, cnt)
        routed.append((h, idx, gate, rank))
    counts = cnt[0, :N_EXPERTS].astype(i32)
    padded = (counts + MOE_BM - 1) // MOE_BM * MOE_BM
    pad_end = jnp.cumsum(padded)
    off = (pad_end - padded).astype(i32)
    padded = padded.astype(i32)
    n_tok = sum(s[0].shape[0] for s in streams)
    n_blk = -(-(n_tok * TOP_K) // MOE_BM) + N_EXPERTS
    blk_e = jnp.minimum(jnp.sum(jnp.arange(n_blk)[:, None] * MOE_BM >= pad_end[None, :], axis=1),
                        N_EXPERTS - 1).astype(i32)
    n_act = (pad_end[-1:] // MOE_BM).astype(i32)
    experts = jnp.arange(N_EXPERTS, dtype=i32)
    later_with_rows = (padded > 0)[None, :] & (experts[None, :] > experts[:, None])
    nxt_e = jnp.min(jnp.where(later_with_rows, experts[None, :], N_EXPERTS), axis=1)
    nxt_e = jnp.where(nxt_e == N_EXPERTS, -1, nxt_e).astype(i32)
    dests = []
    for _, idx, _, rank in routed:
        dest = rank + jnp.sum(jnp.where(idx[..., None] == experts, off, 0), axis=-1)
        dests.append(dest.reshape(-1).astype(i32))
    dest_flat = jnp.concatenate(dests)
    xs = _moe_scatter([r[0] for r in routed], dest_flat, off, padded, n_blk * MOE_BM)
    ys = _moe_experts(xs, blk_e, n_act, nxt_e, w_gu, b_gu, w_dn, b_dn, layer)
    return _moe_combine(ys, dest_flat, [r[2] for r in routed], [s[0] for s in streams], [s[3] for s in streams],
                        joint)


def kernel(x, c, ctx, c_ctx, norm_g, ada_w, ada_b, w_out, w_in_even, hy_conv_w, hy_conv_b,
           hy_w1, hy_b1, hy_w2, hy_b2, hy_w3, hy_b3, hy_w4, hy_freq, hy_filter_bias,
           att_q_norm, att_k_norm, att_sink, w_in_odd, ssd_conv_w, ssd_conv_b, ssd_dt_bias,
           ssd_A_log, ssd_D, ssd_norm, hg_lower_bounds, hg_norm, router_w, router_b,
           moe_w_gu, moe_b_gu, moe_w_dn, moe_b_dn):
    lbs = jax.nn.softmax(hg_lower_bounds.astype(jnp.float32), axis=0)
    lbs = jnp.cumsum(lbs, axis=0) - lbs[0]
    xc = ctx
    n_ctx = ctx.shape[1]
    for layer in range(DEPTH):
        ctx_out = layer < DEPTH - 1
        i = layer // 2
        sh, sc, gt = adaln(c, ada_w[layer], ada_b[layer], 0)
        sh_c, sc_c, gt_c = adaln(c_ctx, ada_w[layer], ada_b[layer], 0)
        assert (layer % 2 == 0) == ctx_out
        moe = functools.partial(moe_layer, g=norm_g[layer, 1], router_w=router_w[layer], router_b=router_b[layer],
                                w_gu=moe_w_gu, b_gu=moe_b_gu[layer], w_dn=moe_w_dn, b_dn=moe_b_dn[layer], layer=layer)
        sh2, sc2, gt2 = adaln(c, ada_w[layer], ada_b[layer], 1)
        if layer % 2 == 0:
            x, xc = even_layer(x, xc, norm_g[layer, 0], (sh, sc), (sh_c, sc_c), w_in_even[i], hy_conv_w[i],
                               hy_conv_b[i], hy_w1[i], hy_b1[i], hy_w2[i], hy_b2[i], hy_w3[i], hy_b3[i], hy_w4[i],
                               hy_freq[i], hy_filter_bias[i], att_q_norm[i], att_k_norm[i], att_sink[i], gt, gt_c,
                               w_out[layer])
            sh2_c, sc2_c, gt2_c = adaln(c_ctx, ada_w[layer], ada_b[layer], 1)
            x_all = moe([(xc[0], sh2_c, sc2_c, gt2_c), (x[0], sh2, sc2, gt2)], joint=True)[0]
        else:
            x = odd_layer(x_all, n_ctx, norm_g[layer, 0], (sh, sc), (sh_c, sc_c), lbs[layer], w_in_odd[i],
                          ssd_conv_w[i], ssd_conv_b[i], ssd_dt_bias[i], ssd_A_log[i], ssd_D[i], ssd_norm[i],
                          hg_norm[i], gt, w_out[layer])
            x = moe([(x, sh2, sc2, gt2)])[0][None]
    return x
```
